```python
import jax, jax.numpy as jnp
from jax import lax
import numpy as np

D_MODEL = 1024
BATCH = 8
SEQ = 8192
DEPTH = 1

ATTN_WIDTH = D_MODEL // 2
POOL_WIDTH = D_MODEL - ATTN_WIDTH
HEAD_DIM = 64
N_Q_HEADS = ATTN_WIDTH // HEAD_DIM
N_KV_HEADS = 2
GQA_GROUP = N_Q_HEADS // N_KV_HEADS
KV_WIDTH = N_KV_HEADS * HEAD_DIM
WINDOW = 128
BLOCK = 128
ROPE_THETA = 10000.0
POOL_SIZES = (2, 4, 8, 16)
N_POOL_GROUPS = len(POOL_SIZES)
POOL_GROUP_WIDTH = POOL_WIDTH // N_POOL_GROUPS
IN_WIDTH = ATTN_WIDTH + 2 * KV_WIDTH + POOL_WIDTH
D_FF = -(-8 * D_MODEL // (3 * 256)) * 256
RMS_EPS = 1e-5

kernel_name = "hybrid_swa_sink_multiscale_pool_block"


def rmsnorm(x, g):
    xf = x.astype(jnp.float32)
    y = xf * lax.rsqrt(jnp.mean(xf * xf, axis=-1, keepdims=True) + RMS_EPS)
    return (y * g.astype(jnp.float32)).astype(x.dtype)


def rope_tables(seq):
    inv_freq = 1.0 / (ROPE_THETA ** (jnp.arange(0, HEAD_DIM, 2, dtype=jnp.float32) / HEAD_DIM))
    ang = jnp.arange(seq, dtype=jnp.float32)[:, None] * inv_freq[None, :]
    return jnp.cos(ang), jnp.sin(ang)


def apply_rope(t, cos, sin):
    t1, t2 = jnp.split(t.astype(jnp.float32), 2, axis=-1)
    c = cos[None, :, None, :]
    s = sin[None, :, None, :]
    return jnp.concatenate([t1 * c - t2 * s, t2 * c + t1 * s], axis=-1).astype(t.dtype)


def sliding_window_attention_with_sinks(q, k, v, sinks):
    b, s = q.shape[0], q.shape[1]
    nb = s // BLOCK
    qb = q.reshape(b, nb, BLOCK, N_KV_HEADS, GQA_GROUP, HEAD_DIM)

    def band(t):
        t = t.reshape(b, nb, BLOCK, N_KV_HEADS, HEAD_DIM)
        prev = jnp.pad(t, ((0, 0), (1, 0), (0, 0), (0, 0), (0, 0)))[:, :-1]
        return jnp.concatenate([prev, t], axis=2)

    kb, vb = band(k), band(v)
    scores = jnp.einsum('bnqkgd,bnskd->bnkgqs', qb, kb,
                        preferred_element_type=jnp.float32) * (HEAD_DIM ** -0.5)
    qi = jnp.arange(BLOCK)[:, None] + BLOCK
    sj = jnp.arange(2 * BLOCK)[None, :]
    delta = qi - sj
    in_window = (delta >= 0) & (delta < WINDOW)
    key_pos = jnp.arange(nb)[:, None] * BLOCK + sj - BLOCK
    mask = in_window[None] & (key_pos >= 0)[:, None, :]
    scores = jnp.where(mask[None, :, None, None], scores, -jnp.inf)
    sink = sinks.astype(jnp.float32).reshape(N_KV_HEADS, GQA_GROUP)[None, None, :, :, None, None]
    m = jnp.maximum(jnp.max(scores, axis=-1, keepdims=True), sink)
    p = jnp.exp(scores - m)
    p = p / (jnp.sum(p, axis=-1, keepdims=True) + jnp.exp(sink - m))
    out = jnp.einsum('bnkgqs,bnskd->bnqkgd', p.astype(v.dtype), vb)
    return out.reshape(b, s, N_Q_HEADS * HEAD_DIM)


def multiscale_causal_pool(u, w_pool, b_pool, pool_scale):
    b, s, _ = u.shape
    uf = u.astype(jnp.float32).reshape(b, s, N_POOL_GROUPS, POOL_GROUP_WIDTH)
    cs = jnp.pad(jnp.cumsum(uf, axis=1), ((0, 0), (1, 0), (0, 0), (0, 0)))
    t = jnp.arange(s)[:, None]
    sizes = jnp.array(POOL_SIZES, dtype=jnp.int32)[None, :]
    start = jnp.maximum(t + 1 - sizes, 0)
    g_idx = jnp.arange(N_POOL_GROUPS)[None, :]
    window_sum = cs[:, 1:] - cs[:, start, g_idx]
    count = (t + 1 - start).astype(jnp.float32)
    mixed = window_sum / count[None, :, :, None] - uf
    y = jnp.einsum('bsgc,gcd->bsgd', mixed.astype(u.dtype), w_pool) + b_pool
    y = y * pool_scale
    return y.reshape(b, s, POOL_WIDTH)


def _fwd_setup_inputs(seed: int = 0) -> dict:
    key = jax.random.key(seed)
    ks = jax.random.split(key, 16)
    f32 = jnp.float32
    nrm = lambda k, shape, scale: jax.random.normal(k, shape, f32) * scale
    return {
        "x": nrm(ks[0], (BATCH, SEQ, D_MODEL), 1.0),
        "g_mix": 1.0 + nrm(ks[1], (DEPTH, D_MODEL), 0.02),
        "w_in": nrm(ks[2], (DEPTH, D_MODEL, IN_WIDTH), D_MODEL ** -0.5),
        "b_in": nrm(ks[3], (DEPTH, IN_WIDTH), 0.02),
        "sinks": nrm(ks[4], (DEPTH, N_Q_HEADS), 1.0),
        "w_pool": nrm(ks[5], (DEPTH, N_POOL_GROUPS, POOL_GROUP_WIDTH, POOL_GROUP_WIDTH), POOL_GROUP_WIDTH ** -0.5),
        "b_pool": nrm(ks[6], (DEPTH, N_POOL_GROUPS, POOL_GROUP_WIDTH), 0.02),
        "pool_scale": 1.0 + nrm(ks[7], (DEPTH, N_POOL_GROUPS, POOL_GROUP_WIDTH), 0.1),
        "w_out": nrm(ks[8], (DEPTH, ATTN_WIDTH + POOL_WIDTH, D_MODEL), (ATTN_WIDTH + POOL_WIDTH) ** -0.5),
        "b_out": nrm(ks[9], (DEPTH, D_MODEL), 0.02),
        "g_ffn": 1.0 + nrm(ks[10], (DEPTH, D_MODEL), 0.02),
        "w_gate": nrm(ks[11], (DEPTH, D_MODEL, D_FF), D_MODEL ** -0.5),
        "w_up": nrm(ks[12], (DEPTH, D_MODEL, D_FF), D_MODEL ** -0.5),
        "w_down": nrm(ks[13], (DEPTH, D_FF, D_MODEL), D_FF ** -0.5),
        "g_final": 1.0 + nrm(ks[14], (D_MODEL,), 0.02),
    }


def _fwd_reference(x, g_mix, w_in, b_in, sinks, w_pool, b_pool, pool_scale, w_out, b_out,
              g_ffn, w_gate, w_up, w_down, g_final):
    b, s, _ = x.shape
    cos, sin = rope_tables(s)
    for i in range(DEPTH):
        h = rmsnorm(x, g_mix[i])
        z = h @ w_in[i] + b_in[i]
        q = z[..., :ATTN_WIDTH].reshape(b, s, N_Q_HEADS, HEAD_DIM)
        k = z[..., ATTN_WIDTH:ATTN_WIDTH + KV_WIDTH].reshape(b, s, N_KV_HEADS, HEAD_DIM)
        v = z[..., ATTN_WIDTH + KV_WIDTH:ATTN_WIDTH + 2 * KV_WIDTH].reshape(b, s, N_KV_HEADS, HEAD_DIM)
        u = z[..., ATTN_WIDTH + 2 * KV_WIDTH:]
        q = apply_rope(q, cos, sin)
        k = apply_rope(k, cos, sin)
        attn = sliding_window_attention_with_sinks(q, k, v, sinks[i])
        pool = multiscale_causal_pool(u, w_pool[i], b_pool[i], pool_scale[i])
        x = x + jnp.concatenate([attn, pool], axis=-1) @ w_out[i] + b_out[i]
        h = rmsnorm(x, g_ffn[i])
        x = x + (jax.nn.silu(h @ w_gate[i]) * (h @ w_up[i])) @ w_down[i]
    return rmsnorm(x, g_final)


import jax as _jax
import jax.numpy as _jnp

TWIN_FORMAT = 'train_step'
FWD_PARAMS = ['x', 'g_mix', 'w_in', 'b_in', 'sinks', 'w_pool', 'b_pool', 'pool_scale', 'w_out', 'b_out', 'g_ffn', 'w_gate', 'w_up', 'w_down', 'g_final']
TWIN_WEIGHTS = ['g_mix', 'w_in', 'b_in', 'sinks', 'w_pool', 'b_pool', 'pool_scale', 'w_out', 'b_out', 'g_ffn', 'w_gate', 'w_up', 'w_down', 'g_final']
TWIN_DIFF_INPUT = 'x'
TWIN_INPUTS = ['x', 'g_mix', 'w_in', 'b_in', 'sinks', 'w_pool', 'b_pool', 'pool_scale', 'w_out', 'b_out', 'g_ffn', 'w_gate', 'w_up', 'w_down', 'g_final', 'loss_target', 'm_g_mix', 'm_w_in', 'm_b_in', 'm_sinks', 'm_w_pool', 'm_b_pool', 'm_pool_scale', 'm_w_out', 'm_b_out', 'm_g_ffn', 'm_w_gate', 'm_w_up', 'm_w_down', 'm_g_final', 'v_g_mix', 'v_w_in', 'v_b_in', 'v_sinks', 'v_w_pool', 'v_b_pool', 'v_pool_scale', 'v_w_out', 'v_b_out', 'v_g_ffn', 'v_w_gate', 'v_w_up', 'v_w_down', 'v_g_final']
TWIN_OUTPUTS = ['loss', 'grad_x', 'grad_g_mix', 'grad_w_in', 'grad_b_in', 'grad_sinks', 'grad_w_pool', 'grad_b_pool', 'grad_pool_scale', 'grad_w_out', 'grad_b_out', 'grad_g_ffn', 'grad_w_gate', 'grad_w_up', 'grad_w_down', 'grad_g_final', 'delta_g_mix', 'delta_w_in', 'delta_b_in', 'delta_sinks', 'delta_w_pool', 'delta_b_pool', 'delta_pool_scale', 'delta_w_out', 'delta_b_out', 'delta_g_ffn', 'delta_w_gate', 'delta_w_up', 'delta_w_down', 'delta_g_final', 'new_m_g_mix', 'new_m_w_in', 'new_m_b_in', 'new_m_sinks', 'new_m_w_pool', 'new_m_b_pool', 'new_m_pool_scale', 'new_m_w_out', 'new_m_b_out', 'new_m_g_ffn', 'new_m_w_gate', 'new_m_w_up', 'new_m_w_down', 'new_m_g_final', 'new_v_g_mix', 'new_v_w_in', 'new_v_b_in', 'new_v_sinks', 'new_v_w_pool', 'new_v_b_pool', 'new_v_pool_scale', 'new_v_w_out', 'new_v_b_out', 'new_v_g_ffn', 'new_v_w_gate', 'new_v_w_up', 'new_v_w_down', 'new_v_g_final']
TWIN_LEAF_KINDS = {'loss': 'loss', 'grad_x': 'grad_x', 'grad_g_mix': 'grad_w', 'grad_w_in': 'grad_w', 'grad_b_in': 'grad_w', 'grad_sinks': 'grad_w', 'grad_w_pool': 'grad_w', 'grad_b_pool': 'grad_w', 'grad_pool_scale': 'grad_w', 'grad_w_out': 'grad_w', 'grad_b_out': 'grad_w', 'grad_g_ffn': 'grad_w', 'grad_w_gate': 'grad_w', 'grad_w_up': 'grad_w', 'grad_w_down': 'grad_w', 'grad_g_final': 'grad_w', 'delta_g_mix': 'delta_w', 'delta_w_in': 'delta_w', 'delta_b_in': 'delta_w', 'delta_sinks': 'delta_w', 'delta_w_pool': 'delta_w', 'delta_b_pool': 'delta_w', 'delta_pool_scale': 'delta_w', 'delta_w_out': 'delta_w', 'delta_b_out': 'delta_w', 'delta_g_ffn': 'delta_w', 'delta_w_gate': 'delta_w', 'delta_w_up': 'delta_w', 'delta_w_down': 'delta_w', 'delta_g_final': 'delta_w', 'new_m_g_mix': 'new_m', 'new_m_w_in': 'new_m', 'new_m_b_in': 'new_m', 'new_m_sinks': 'new_m', 'new_m_w_pool': 'new_m', 'new_m_b_pool': 'new_m', 'new_m_pool_scale': 'new_m', 'new_m_w_out': 'new_m', 'new_m_b_out': 'new_m', 'new_m_g_ffn': 'new_m', 'new_m_w_gate': 'new_m', 'new_m_w_up': 'new_m', 'new_m_w_down': 'new_m', 'new_m_g_final': 'new_m', 'new_v_g_mix': 'new_v', 'new_v_w_in': 'new_v', 'new_v_b_in': 'new_v', 'new_v_sinks': 'new_v', 'new_v_w_pool': 'new_v', 'new_v_b_pool': 'new_v', 'new_v_pool_scale': 'new_v', 'new_v_w_out': 'new_v', 'new_v_b_out': 'new_v', 'new_v_g_ffn': 'new_v', 'new_v_w_gate': 'new_v', 'new_v_w_up': 'new_v', 'new_v_w_down': 'new_v', 'new_v_g_final': 'new_v'}


def _forward(args):
    return _fwd_reference(*[args[k] for k in FWD_PARAMS])


def _output_shape():
    def fwd():
        inp = _fwd_setup_inputs(0)
        return _fwd_reference(*[inp[k] for k in FWD_PARAMS])
    out = _jax.eval_shape(fwd)
    return out.shape, out.dtype

N_MICROBATCH = 1
ADAM_LR = 0.001
ADAM_B1 = 0.9
ADAM_B2 = 0.999
ADAM_EPS = 1e-08
ADAM_WD = 0.01
ADAM_STEP = 10
PER_EXAMPLE_BATCH_AXIS = {'x': 0, 'loss_target': 0}
SHARED_INPUTS = []
_WEIGHT_DTYPES = {'g_mix': _jnp.float32, 'w_in': _jnp.float32, 'b_in': _jnp.float32, 'sinks': _jnp.float32, 'w_pool': _jnp.float32, 'b_pool': _jnp.float32, 'pool_scale': _jnp.float32, 'w_out': _jnp.float32, 'b_out': _jnp.float32, 'g_ffn': _jnp.float32, 'w_gate': _jnp.float32, 'w_up': _jnp.float32, 'w_down': _jnp.float32, 'g_final': _jnp.float32}
MOMENT_SCALE = {'g_mix': 1.560741e-01, 'w_in': 1.371114e-01, 'b_in': 2.278225e-01, 'sinks': 3.625089e-02, 'w_pool': 2.089111e-01, 'b_pool': 2.385726e-01, 'pool_scale': 2.141019e-01, 'w_out': 1.480890e-01, 'b_out': 2.652726e-01, 'g_ffn': 1.695457e-01, 'w_gate': 7.042329e-02, 'w_up': 6.862537e-02, 'w_down': 1.134030e-01, 'g_final': 6.403964e+01}


def _to_microbatches(a, axis):
    t = _jnp.moveaxis(a, axis, 0)
    t = t.reshape((N_MICROBATCH, t.shape[0] // N_MICROBATCH) + t.shape[1:])
    return _jnp.moveaxis(t, 1, axis + 1)


def setup_inputs(seed: int = 0) -> dict:
    inp = _fwd_setup_inputs(seed)
    key = _jax.random.fold_in(_jax.random.key(seed), 7919)
    shape, _ = _output_shape()
    out = dict(inp)
    out["loss_target"] = _jax.random.normal(_jax.random.fold_in(key, 0), shape, _jnp.float32)
    for i, name in enumerate(TWIN_WEIGHTS):
        w = inp[name].astype(_jnp.float32)
        if MOMENT_SCALE is None:
            s = _jnp.sqrt(_jnp.mean(_jnp.square(w)) + 1e-30)
        else:
            s = MOMENT_SCALE[name]
        km, kv = _jax.random.split(_jax.random.fold_in(key, i + 1))
        out[name] = w
        out["m_" + name] = s * _jax.random.normal(km, w.shape, _jnp.float32)
        out["v_" + name] = (s * s) * _jax.random.uniform(kv, w.shape, _jnp.float32, 0.5, 1.5)
    if N_MICROBATCH > 1:
        for name, axis in PER_EXAMPLE_BATCH_AXIS.items():
            out[name] = _to_microbatches(out[name], axis)
    return {'x': out['x'], 'g_mix': out['g_mix'], 'w_in': out['w_in'], 'b_in': out['b_in'], 'sinks': out['sinks'], 'w_pool': out['w_pool'], 'b_pool': out['b_pool'], 'pool_scale': out['pool_scale'], 'w_out': out['w_out'], 'b_out': out['b_out'], 'g_ffn': out['g_ffn'], 'w_gate': out['w_gate'], 'w_up': out['w_up'], 'w_down': out['w_down'], 'g_final': out['g_final'], 'loss_target': out['loss_target'], 'm_g_mix': out['m_g_mix'], 'm_w_in': out['m_w_in'], 'm_b_in': out['m_b_in'], 'm_sinks': out['m_sinks'], 'm_w_pool': out['m_w_pool'], 'm_b_pool': out['m_b_pool'], 'm_pool_scale': out['m_pool_scale'], 'm_w_out': out['m_w_out'], 'm_b_out': out['m_b_out'], 'm_g_ffn': out['m_g_ffn'], 'm_w_gate': out['m_w_gate'], 'm_w_up': out['m_w_up'], 'm_w_down': out['m_w_down'], 'm_g_final': out['m_g_final'], 'v_g_mix': out['v_g_mix'], 'v_w_in': out['v_w_in'], 'v_b_in': out['v_b_in'], 'v_sinks': out['v_sinks'], 'v_w_pool': out['v_w_pool'], 'v_b_pool': out['v_b_pool'], 'v_pool_scale': out['v_pool_scale'], 'v_w_out': out['v_w_out'], 'v_b_out': out['v_b_out'], 'v_g_ffn': out['v_g_ffn'], 'v_w_gate': out['v_w_gate'], 'v_w_up': out['v_w_up'], 'v_w_down': out['v_w_down'], 'v_g_final': out['v_g_final']}


def _loss(weights, diff, rest, loss_target):
    with _jax.named_scope("forward"):
        args = {**rest, TWIN_DIFF_INPUT: diff, **{k: w.astype(_WEIGHT_DTYPES[k]) for k, w in weights.items()}}
        y = _forward(args)
    with _jax.named_scope("loss_head"):
        err = _jnp.square(y.astype(_jnp.float32) - loss_target)
        return 0.5 * _jnp.sum(_jnp.mean(err, axis=-1)) if err.ndim else 0.5 * err


def _adamw(w, g, m, v):
    m = ADAM_B1 * m + (1.0 - ADAM_B1) * g
    v = ADAM_B2 * v + (1.0 - ADAM_B2) * _jnp.square(g)
    m_hat = m / (1.0 - ADAM_B1 ** ADAM_STEP)
    v_hat = v / (1.0 - ADAM_B2 ** ADAM_STEP)
    delta = -ADAM_LR * (m_hat / (_jnp.sqrt(v_hat) + ADAM_EPS) + ADAM_WD * w)
    return delta, m, v


def reference(x, g_mix, w_in, b_in, sinks, w_pool, b_pool, pool_scale, w_out, b_out, g_ffn, w_gate, w_up, w_down, g_final, loss_target, m_g_mix, m_w_in, m_b_in, m_sinks, m_w_pool, m_b_pool, m_pool_scale, m_w_out, m_b_out, m_g_ffn, m_w_gate, m_w_up, m_w_down, m_g_final, v_g_mix, v_w_in, v_b_in, v_sinks, v_w_pool, v_b_pool, v_pool_scale, v_w_out, v_b_out, v_g_ffn, v_w_gate, v_w_up, v_w_down, v_g_final):
    given = dict(x=x, g_mix=g_mix, w_in=w_in, b_in=b_in, sinks=sinks, w_pool=w_pool, b_pool=b_pool, pool_scale=pool_scale, w_out=w_out, b_out=b_out, g_ffn=g_ffn, w_gate=w_gate, w_up=w_up, w_down=w_down, g_final=g_final, loss_target=loss_target, m_g_mix=m_g_mix, m_w_in=m_w_in, m_b_in=m_b_in, m_sinks=m_sinks, m_w_pool=m_w_pool, m_b_pool=m_b_pool, m_pool_scale=m_pool_scale, m_w_out=m_w_out, m_b_out=m_b_out, m_g_ffn=m_g_ffn, m_w_gate=m_w_gate, m_w_up=m_w_up, m_w_down=m_w_down, m_g_final=m_g_final, v_g_mix=v_g_mix, v_w_in=v_w_in, v_b_in=v_b_in, v_sinks=v_sinks, v_w_pool=v_w_pool, v_b_pool=v_b_pool, v_pool_scale=v_pool_scale, v_w_out=v_w_out, v_b_out=v_b_out, v_g_ffn=v_g_ffn, v_w_gate=v_w_gate, v_w_up=v_w_up, v_w_down=v_w_down, v_g_final=v_g_final)
    weights = {n: given[n] for n in TWIN_WEIGHTS}
    shared = {n: given[n] for n in SHARED_INPUTS}
    per_example = {n: given[n] for n in ['x']}
    grad_fn = _jax.value_and_grad(_loss, argnums=(0, 1))

    def one_microbatch(ex, loss_target):
        ex = dict(ex)
        diff = ex.pop(TWIN_DIFF_INPUT)
        return grad_fn(weights, diff, {**shared, **ex}, loss_target)

    if N_MICROBATCH == 1:
        loss, (grad_w, grad_x) = one_microbatch(per_example, given["loss_target"])
    else:
        def body(carry, xs):
            loss_sum, grad_sum = carry
            l_k, (gw_k, gx_k) = one_microbatch(xs[0], xs[1])
            with _jax.named_scope("update"):
                return (loss_sum + l_k, _jax.tree.map(_jnp.add, grad_sum, gw_k)), gx_k

        init = (_jnp.zeros((), _jnp.float32), _jax.tree.map(_jnp.zeros_like, weights))
        (loss, grad_w), grad_x = _jax.lax.scan(body, init, (per_example, given["loss_target"]))
    with _jax.named_scope("update"):
        delta_w, new_m, new_v = {}, {}, {}
        for n in TWIN_WEIGHTS:
            delta_w[n], new_m[n], new_v[n] = _adamw(weights[n], grad_w[n], given["m_" + n], given["v_" + n])
    return (loss, grad_x, *[grad_w[n] for n in TWIN_WEIGHTS], *[delta_w[n] for n in TWIN_WEIGHTS],
            *[new_m[n] for n in TWIN_WEIGHTS], *[new_v[n] for n in TWIN_WEIGHTS])
```

```python
import functools

import jax
import jax.numpy as jnp
from jax import lax
from jax.experimental import pallas as pl
from jax.experimental.pallas import tpu as pltpu

D_MODEL = 1024
ATTN_WIDTH = 512
KV_WIDTH = 128
POOL_WIDTH = 512
HEAD_DIM = 64
N_Q_HEADS = 8
N_KV_HEADS = 2
GQA_GROUP = 4
BLOCK = 128
POOL_SIZES = (2, 4, 8, 16)
POOL_GROUP_WIDTH = 128
POOL_HALO = 16
IN_WIDTH = 1280
D_FF = 2816
RMS_EPS = 1e-5
ROPE_THETA = 10000.0
Q_SCALE = HEAD_DIM ** -0.5

ADAM_LR = 0.001
ADAM_B1 = 0.9
ADAM_B2 = 0.999
ADAM_EPS = 1e-08
ADAM_WD = 0.01
ADAM_STEP = 10

N_DEV = 8
N_CHIPS = 4
LANES = 128
VMEM_LIMIT_BYTES = 56 * 1024 * 1024

F32 = jnp.float32
BF16 = jnp.bfloat16
MESH = pl.DeviceIdType.MESH
HBM = pl.BlockSpec(memory_space=pltpu.HBM)
VMEM = pl.BlockSpec(memory_space=pltpu.VMEM)


def _params(*semantics):
    return pltpu.CompilerParams(dimension_semantics=semantics or None, vmem_limit_bytes=VMEM_LIMIT_BYTES)


def _nn(a, b):
    return jnp.dot(a, b, preferred_element_type=F32)


def _nt(a, b):
    return lax.dot_general(a, b, (((1,), (1,)), ((), ())), preferred_element_type=F32)


def _tn(a, b):
    return lax.dot_general(a, b, (((0,), (0,)), ((), ())), preferred_element_type=F32)


def _full(shape):
    return pl.BlockSpec(shape, lambda *_: (0,) * len(shape))


def _rows(tm, width):
    return pl.BlockSpec((tm, width), lambda i, *_: (i, 0))


def _rot_half(t):
    n = t.shape[1]
    lane = lax.broadcasted_iota(jnp.int32, t.shape, 1)
    return jnp.where((lane % HEAD_DIM) < HEAD_DIM // 2, pltpu.roll(t, n - HEAD_DIM // 2, 1), pltpu.roll(t, HEAD_DIM // 2, 1))


def _rope(t, cos, sin):
    reps = t.shape[1] // LANES
    if reps > 1:
        cos, sin = jnp.tile(cos, (1, reps)), jnp.tile(sin, (1, reps))
    return t * cos + _rot_half(t) * sin


def _rope_bwd(d, cos, sin):
    reps = d.shape[1] // LANES
    if reps > 1:
        cos, sin = jnp.tile(cos, (1, reps)), jnp.tile(sin, (1, reps))
    return d * cos + _rot_half(d * sin)


def _rms(x):
    r = lax.rsqrt(jnp.mean(x * x, axis=-1, keepdims=True) + RMS_EPS)
    return x * r, r


def _rms_bwd(dh, n, r, g):
    dn = dh * g
    dx = r * (dn - n * jnp.mean(dn * n, axis=-1, keepdims=True))
    return dx, jnp.sum(dh * n, axis=0, keepdims=True)


def _token_tile(s):
    return min(512, s)


def _fwd_inproj(x, g_mix, win_t, b_in, cos, sin, w_pool, b_pool, pool_scale):
    s = x.shape[0]
    tm = _token_tile(s)

    def body(x_ref, g_ref, w_ref, b_ref, cos_ref, sin_ref, wp_ref, bp_ref, ps_ref,
             q_ref, k_ref, v_ref, mix_ref, pool_ref, tail_ref):
        i = pl.program_id(0)

        @pl.when(i == 0)
        def _():
            tail_ref[...] = jnp.zeros_like(tail_ref)

        n, _ = _rms(x_ref[...])
        h = (n * g_ref[...]).astype(BF16)
        z = _nt(h, w_ref[...]) + b_ref[...]
        cos_t, sin_t = cos_ref[...], sin_ref[...]
        q_ref[...] = (_rope(z[:, :ATTN_WIDTH], cos_t, sin_t) * Q_SCALE).astype(BF16)
        k_ref[...] = _rope(z[:, ATTN_WIDTH:ATTN_WIDTH + KV_WIDTH], cos_t, sin_t).astype(BF16)
        v_ref[...] = z[:, ATTN_WIDTH + KV_WIDTH:ATTN_WIDTH + 2 * KV_WIDTH].astype(BF16)
        u = z[:, ATTN_WIDTH + 2 * KV_WIDTH:]
        u_ext = jnp.concatenate([tail_ref[...], u], axis=0)
        tail_ref[...] = u[tm - POOL_HALO:, :]
        pos = lax.broadcasted_iota(jnp.int32, (tm, POOL_GROUP_WIDTH), 0) + i * tm
        for g, size in enumerate(POOL_SIZES):
            cols = slice(g * POOL_GROUP_WIDTH, (g + 1) * POOL_GROUP_WIDTH)
            a = u_ext[:, cols]
            shift = 1
            while shift < size:
                a = a + pltpu.roll(a, shift, 0)
                shift *= 2
            count = jnp.minimum(pos + 1, size).astype(F32)
            mixed = (a[POOL_HALO:, :] / count - u[:, cols]).astype(BF16)
            pre = _nn(mixed, wp_ref[g]) + bp_ref[:, cols]
            mix_ref[:, cols] = mixed
            pool_ref[:, cols] = (pre * ps_ref[:, cols]).astype(BF16)

    bf = lambda w: jax.ShapeDtypeStruct((s, w), BF16)
    return pl.pallas_call(
        body, name="fwd_inproj", grid=(s // tm,),
        in_specs=[_rows(tm, D_MODEL), _full((1, D_MODEL)), _full((IN_WIDTH, D_MODEL)), _full((1, IN_WIDTH)),
                  _rows(tm, LANES), _rows(tm, LANES), _full((4, POOL_GROUP_WIDTH, POOL_GROUP_WIDTH)),
                  _full((1, POOL_WIDTH)), _full((1, POOL_WIDTH))],
        out_specs=[_rows(tm, ATTN_WIDTH), _rows(tm, KV_WIDTH), _rows(tm, KV_WIDTH), _rows(tm, POOL_WIDTH), _rows(tm, POOL_WIDTH)],
        out_shape=[bf(ATTN_WIDTH), bf(KV_WIDTH), bf(KV_WIDTH), bf(POOL_WIDTH), bf(POOL_WIDTH)],
        scratch_shapes=[pltpu.VMEM((POOL_HALO, POOL_WIDTH), F32)],
        compiler_params=_params("arbitrary"),
    )(x, g_mix, win_t, b_in, cos, sin, w_pool, b_pool, pool_scale)


def _band_mask(i):
    r = lax.broadcasted_iota(jnp.int32, (BLOCK, 2 * BLOCK), 0)
    j = lax.broadcasted_iota(jnp.int32, (BLOCK, 2 * BLOCK), 1)
    return (j > r) & (j <= r + BLOCK) & ((i > 0) | (j >= BLOCK))


def _attn_fwd(q, k, v, sinks):
    s = q.shape[0]
    nb = s // BLOCK

    def body(q_ref, kc_ref, kp_ref, vc_ref, vp_ref, sink_ref, o_ref, lse_ref):
        i = pl.program_id(0)
        mask = _band_mask(i)
        kband = jnp.concatenate([kp_ref[...], kc_ref[...]], axis=0)
        vband = jnp.concatenate([vp_ref[...], vc_ref[...]], axis=0)
        for h in range(N_Q_HEADS):
            kv = h // GQA_GROUP
            kh = kband[:, kv * HEAD_DIM:(kv + 1) * HEAD_DIM]
            vh = vband[:, kv * HEAD_DIM:(kv + 1) * HEAD_DIM]
            sc = jnp.where(mask, _nt(q_ref[:, h * HEAD_DIM:(h + 1) * HEAD_DIM], kh), -jnp.inf)
            sink = sink_ref[:, h:h + 1]
            m = jnp.maximum(jnp.max(sc, axis=-1, keepdims=True), sink)
            p = jnp.exp(sc - m)
            denom = jnp.sum(p, axis=-1, keepdims=True) + jnp.exp(sink - m)
            p = p / denom
            o_ref[:, h * HEAD_DIM:(h + 1) * HEAD_DIM] = _nn(p.astype(BF16), vh).astype(BF16)
            lse_ref[:, h:h + 1] = m + jnp.log(denom)

    cur = lambda w: pl.BlockSpec((BLOCK, w), lambda i: (i, 0))
    prev = lambda w: pl.BlockSpec((BLOCK, w), lambda i: (jnp.maximum(i - 1, 0), 0))
    return pl.pallas_call(
        body, name="attn_fwd", grid=(nb,),
        in_specs=[cur(ATTN_WIDTH), cur(KV_WIDTH), prev(KV_WIDTH), cur(KV_WIDTH), prev(KV_WIDTH), _full((1, N_Q_HEADS))],
        out_specs=[cur(ATTN_WIDTH), cur(N_Q_HEADS)],
        out_shape=[jax.ShapeDtypeStruct((s, ATTN_WIDTH), BF16), jax.ShapeDtypeStruct((s, N_Q_HEADS), F32)],
        compiler_params=_params("parallel"),
    )(q, k, k, v, v, sinks)


def _fwd_outproj(attn, pool, w_out, b_out, x, g_ffn):
    s = x.shape[0]
    tm = _token_tile(s)

    def body(a_ref, p_ref, w_ref, b_ref, x_ref, g_ref, x2_ref, h2_ref):
        x2 = x_ref[...] + _nn(a_ref[...], w_ref[:ATTN_WIDTH, :]) + _nn(p_ref[...], w_ref[ATTN_WIDTH:, :]) + b_ref[...]
        x2_ref[...] = x2
        n, _ = _rms(x2)
        h2_ref[...] = (n * g_ref[...]).astype(BF16)

    return pl.pallas_call(
        body, name="fwd_outproj", grid=(s // tm,),
        in_specs=[_rows(tm, ATTN_WIDTH), _rows(tm, POOL_WIDTH), _full((D_MODEL, D_MODEL)), _full((1, D_MODEL)),
                  _rows(tm, D_MODEL), _full((1, D_MODEL))],
        out_specs=[_rows(tm, D_MODEL), _rows(tm, D_MODEL)],
        out_shape=[jax.ShapeDtypeStruct((s, D_MODEL), F32), jax.ShapeDtypeStruct((s, D_MODEL), BF16)],
        compiler_params=_params("parallel"),
    )(attn, pool, w_out, b_out, x, g_ffn)


FF_CHUNK = 256


def _ffn_token_tile(s):
    return min(1024, s)


def _fwd_ffn_loss(h2, x2, wg_t, wu_t, wd, g_final, target):
    s = x2.shape[0]
    tm = _ffn_token_tile(s)
    nc = D_FF // FF_CHUNK

    def body(h_ref, x2_ref, wg_ref, wu_ref, wd_ref, g_ref, t_ref,
             gate_ref, up_ref, dx3_ref, dx3b_ref, sq_ref, dg_ref, acc_ref):
        i, j = pl.program_id(0), pl.program_id(1)

        @pl.when((i == 0) & (j == 0))
        def _():
            sq_ref[...] = jnp.zeros_like(sq_ref)
            dg_ref[...] = jnp.zeros_like(dg_ref)

        h = h_ref[...]
        gate = _nt(h, wg_ref[...])
        up = _nt(h, wu_ref[...])
        gate_ref[...] = gate.astype(BF16)
        up_ref[...] = up.astype(BF16)
        act = (gate * jax.nn.sigmoid(gate) * up).astype(BF16)
        part = _nn(act, wd_ref[...])

        @pl.when(j == 0)
        def _():
            acc_ref[...] = part

        @pl.when(j > 0)
        def _():
            acc_ref[...] += part

        @pl.when(j == nc - 1)
        def _():
            x3 = x2_ref[...] + acc_ref[...]
            n, r = _rms(x3)
            g = g_ref[...]
            diff = n * g - t_ref[...]
            sq_ref[...] += jnp.sum(diff * diff, axis=0, keepdims=True)
            dx3, dg = _rms_bwd(diff * (1.0 / D_MODEL), n, r, g)
            dg_ref[...] += dg
            dx3_ref[...] = dx3
            dx3b_ref[...] = dx3.astype(BF16)

    tok = lambda w: pl.BlockSpec((tm, w), lambda i, j: (i, 0))
    wblk = pl.BlockSpec((FF_CHUNK, D_MODEL), lambda i, j: (j, 0))
    ablk = pl.BlockSpec((tm, FF_CHUNK), lambda i, j: (i, j))
    vec = pl.BlockSpec((1, D_MODEL), lambda i, j: (0, 0))
    return pl.pallas_call(
        body, name="fwd_ffn_loss", grid=(s // tm, nc),
        in_specs=[tok(D_MODEL), tok(D_MODEL), wblk, wblk, wblk, vec, tok(D_MODEL)],
        out_specs=[ablk, ablk, tok(D_MODEL), tok(D_MODEL), vec, vec],
        out_shape=[jax.ShapeDtypeStruct((s, D_FF), BF16), jax.ShapeDtypeStruct((s, D_FF), BF16),
                   jax.ShapeDtypeStruct((s, D_MODEL), F32), jax.ShapeDtypeStruct((s, D_MODEL), BF16),
                   jax.ShapeDtypeStruct((1, D_MODEL), F32), jax.ShapeDtypeStruct((1, D_MODEL), F32)],
        scratch_shapes=[pltpu.VMEM((tm, D_MODEL), F32)],
        compiler_params=_params("arbitrary", "arbitrary"),
    )(h2, x2, wg_t, wu_t, wd, g_final, target)


def _bwd_ffn(dx3, gate, up, wd, wg_t, wu_t, x2, g_ffn):
    s = x2.shape[0]
    tm = _ffn_token_tile(s)
    nc = D_FF // FF_CHUNK

    def body(dx3_ref, gate_ref, up_ref, wd_ref, wg_ref, wu_ref, x2_ref, g_ref,
             dgate_ref, dup_ref, act_ref, dx2_ref, dx2b_ref, dg_ref, db_ref, acc_ref):
        i, j = pl.program_id(0), pl.program_id(1)

        @pl.when((i == 0) & (j == 0))
        def _():
            dg_ref[...] = jnp.zeros_like(dg_ref)
            db_ref[...] = jnp.zeros_like(db_ref)

        dact = _nt(dx3_ref[...].astype(BF16), wd_ref[...])
        gate = gate_ref[...].astype(F32)
        up = up_ref[...].astype(F32)
        sig = jax.nn.sigmoid(gate)
        silu = gate * sig
        act_ref[...] = (silu * up).astype(BF16)
        dup = (dact * silu).astype(BF16)
        dgate = (dact * up * (sig + silu * (1.0 - sig))).astype(BF16)
        dup_ref[...] = dup
        dgate_ref[...] = dgate
        part = _nn(dgate, wg_ref[...]) + _nn(dup, wu_ref[...])

        @pl.when(j == 0)
        def _():
            acc_ref[...] = part

        @pl.when(j > 0)
        def _():
            acc_ref[...] += part

        @pl.when(j == nc - 1)
        def _():
            n, r = _rms(x2_ref[...])
            dx, dg = _rms_bwd(acc_ref[...], n, r, g_ref[...])
            dx2 = dx3_ref[...] + dx
            dg_ref[...] += dg
            db_ref[...] += jnp.sum(dx2, axis=0, keepdims=True)
            dx2_ref[...] = dx2
            dx2b_ref[...] = dx2.astype(BF16)

    tok = lambda w: pl.BlockSpec((tm, w), lambda i, j: (i, 0))
    wblk = pl.BlockSpec((FF_CHUNK, D_MODEL), lambda i, j: (j, 0))
    ablk = pl.BlockSpec((tm, FF_CHUNK), lambda i, j: (i, j))
    vec = pl.BlockSpec((1, D_MODEL), lambda i, j: (0, 0))
    act_shape = jax.ShapeDtypeStruct((s, D_FF), BF16)
    return pl.pallas_call(
        body, name="bwd_ffn", grid=(s // tm, nc),
        in_specs=[tok(D_MODEL), ablk, ablk, wblk, wblk, wblk, tok(D_MODEL), vec],
        out_specs=[ablk, ablk, ablk, tok(D_MODEL), tok(D_MODEL), vec, vec],
        out_shape=[act_shape, act_shape, act_shape,
                   jax.ShapeDtypeStruct((s, D_MODEL), F32), jax.ShapeDtypeStruct((s, D_MODEL), BF16),
                   jax.ShapeDtypeStruct((1, D_MODEL), F32), jax.ShapeDtypeStruct((1, D_MODEL), F32)],
        scratch_shapes=[pltpu.VMEM((tm, D_MODEL), F32)],
        compiler_params=_params("arbitrary", "arbitrary"),
    )(dx3, gate, up, wd, wg_t, wu_t, x2, g_ffn)


TN_ROW_CHUNK = 256


def _weight_grad(a, b, name):
    s, m = a.shape
    n_out = b.shape[1]
    tm = _token_tile(s)

    def body(a_ref, b_ref, o_ref):
        @pl.when(pl.program_id(0) == 0)
        def _():
            o_ref[...] = jnp.zeros_like(o_ref)

        bt = b_ref[...]
        for m0 in range(0, m, TN_ROW_CHUNK):
            o_ref[m0:m0 + TN_ROW_CHUNK, :] += _tn(a_ref[:, m0:m0 + TN_ROW_CHUNK], bt)

    return pl.pallas_call(
        body, name=name, grid=(s // tm,),
        in_specs=[_rows(tm, m), _rows(tm, n_out)],
        out_specs=_full((m, n_out)),
        out_shape=jax.ShapeDtypeStruct((m, n_out), F32),
        compiler_params=_params("arbitrary"),
    )(a, b)


def _bwd_outproj_pool(dx2b, attn, pool, mixed, w_out, w_pool, b_pool, pool_scale):
    s = dx2b.shape[0]
    tm = _token_tile(s)
    nt = s // tm

    def body(dx_ref, a_ref, p_ref, mix_ref, w_ref, wp_ref, bp_ref, ps_ref,
             dattn_ref, du_ref, dwout_ref, dwp_ref, dbp_ref, dps_ref, head_ref):
        step = pl.program_id(0)
        tile = nt - 1 - step

        @pl.when(step == 0)
        def _():
            head_ref[...] = jnp.zeros_like(head_ref)
            dwout_ref[...] = jnp.zeros_like(dwout_ref)
            dwp_ref[...] = jnp.zeros_like(dwp_ref)
            dbp_ref[...] = jnp.zeros_like(dbp_ref)
            dps_ref[...] = jnp.zeros_like(dps_ref)

        dx = dx_ref[...]
        dwout_ref[:ATTN_WIDTH, :] += _tn(a_ref[...], dx)
        dwout_ref[ATTN_WIDTH:, :] += _tn(p_ref[...], dx)
        dcat = _nt(dx, w_ref[...])
        dattn_ref[...] = dcat[:, :ATTN_WIDTH].astype(BF16)
        dpool = dcat[:, ATTN_WIDTH:]
        pos = lax.broadcasted_iota(jnp.int32, (tm, POOL_GROUP_WIDTH), 0) + tile * tm
        head = head_ref[...]
        n_ext = tm + POOL_HALO
        for g, size in enumerate(POOL_SIZES):
            cols = slice(g * POOL_GROUP_WIDTH, (g + 1) * POOL_GROUP_WIDTH)
            mixed_g = mix_ref[:, cols]
            pre = _nn(mixed_g, wp_ref[g]) + bp_ref[:, cols]
            dy = dpool[:, cols]
            dps_ref[:, cols] += jnp.sum(dy * pre, axis=0, keepdims=True)
            dpre = dy * ps_ref[:, cols]
            dbp_ref[:, cols] += jnp.sum(dpre, axis=0, keepdims=True)
            dpre_b = dpre.astype(BF16)
            dwp_ref[g] += _tn(mixed_g, dpre_b)
            dmixed = _nt(dpre_b, wp_ref[g])
            w = dmixed / jnp.minimum(pos + 1, size).astype(F32)
            head_ref[:, cols] = w[:POOL_HALO, :]
            a = jnp.concatenate([w, head[:, cols]], axis=0)
            shift = 1
            while shift < size:
                a = a + pltpu.roll(a, n_ext - shift, 0)
                shift *= 2
            du_ref[:, cols] = (a[:tm, :] - dmixed).astype(BF16)

    rev = lambda w: pl.BlockSpec((tm, w), lambda i: (nt - 1 - i, 0))
    return pl.pallas_call(
        body, name="bwd_outproj_pool", grid=(nt,),
        in_specs=[rev(D_MODEL), rev(ATTN_WIDTH), rev(POOL_WIDTH), rev(POOL_WIDTH), _full((D_MODEL, D_MODEL)),
                  _full((4, POOL_GROUP_WIDTH, POOL_GROUP_WIDTH)), _full((1, POOL_WIDTH)), _full((1, POOL_WIDTH))],
        out_specs=[rev(ATTN_WIDTH), rev(POOL_WIDTH), _full((D_MODEL, D_MODEL)),
                   _full((4, POOL_GROUP_WIDTH, POOL_GROUP_WIDTH)), _full((1, POOL_WIDTH)), _full((1, POOL_WIDTH))],
        out_shape=[jax.ShapeDtypeStruct((s, ATTN_WIDTH), BF16), jax.ShapeDtypeStruct((s, POOL_WIDTH), BF16),
                   jax.ShapeDtypeStruct((D_MODEL, D_MODEL), F32),
                   jax.ShapeDtypeStruct((4, POOL_GROUP_WIDTH, POOL_GROUP_WIDTH), F32),
                   jax.ShapeDtypeStruct((1, POOL_WIDTH), F32), jax.ShapeDtypeStruct((1, POOL_WIDTH), F32)],
        scratch_shapes=[pltpu.VMEM((POOL_HALO, POOL_WIDTH), F32)],
        compiler_params=_params("arbitrary"),
    )(dx2b, attn, pool, mixed, w_out, w_pool, b_pool, pool_scale)


def _attn_bwd(q, k, v, dattn, lse, sinks):
    s = q.shape[0]
    nb = s // BLOCK

    def body(q_ref, kc_ref, kp_ref, vc_ref, vp_ref, do_ref, lse_ref, sink_ref,
             dq_ref, dk_ref, dv_ref, dsink_ref, dkc_ref, dvc_ref):
        i = pl.program_id(0)

        @pl.when(i == 0)
        def _():
            dkc_ref[...] = jnp.zeros_like(dkc_ref)
            dvc_ref[...] = jnp.zeros_like(dvc_ref)
            dsink_ref[...] = jnp.zeros_like(dsink_ref)

        @pl.when(i < nb)
        def _():
            mask = _band_mask(i)
            kband = jnp.concatenate([kp_ref[...], kc_ref[...]], axis=0)
            vband = jnp.concatenate([vp_ref[...], vc_ref[...]], axis=0)
            for kv in range(N_KV_HEADS):
                kh = kband[:, kv * HEAD_DIM:(kv + 1) * HEAD_DIM]
                vh = vband[:, kv * HEAD_DIM:(kv + 1) * HEAD_DIM]
                dk_band = jnp.zeros((2 * BLOCK, HEAD_DIM), F32)
                dv_band = jnp.zeros((2 * BLOCK, HEAD_DIM), F32)
                for g in range(GQA_GROUP):
                    h = kv * GQA_GROUP + g
                    hc = slice(h * HEAD_DIM, (h + 1) * HEAD_DIM)
                    qh = q_ref[:, hc]
                    doh = do_ref[:, hc]
                    lse_h = lse_ref[:, h:h + 1]
                    p = jnp.where(mask, jnp.exp(_nt(qh, kh) - lse_h), 0.0)
                    dp = _nt(doh, vh)
                    delta = jnp.sum(p * dp, axis=-1, keepdims=True)
                    ds = (p * (dp - delta)).astype(BF16)
                    dsink_ref[:, h:h + 1] -= jnp.sum(jnp.exp(sink_ref[:, h:h + 1] - lse_h) * delta, axis=0, keepdims=True)
                    dq_ref[:, hc] = _nn(ds, kh) * Q_SCALE
                    dk_band = dk_band + _tn(ds, qh)
                    dv_band = dv_band + _tn(p.astype(BF16), doh)
                kc = slice(kv * HEAD_DIM, (kv + 1) * HEAD_DIM)
                dk_ref[:, kc] = dkc_ref[:, kc] + dk_band[:BLOCK, :]
                dv_ref[:, kc] = dvc_ref[:, kc] + dv_band[:BLOCK, :]
                dkc_ref[:, kc] = dk_band[BLOCK:, :]
                dvc_ref[:, kc] = dv_band[BLOCK:, :]

        @pl.when(i == nb)
        def _():
            dk_ref[...] = dkc_ref[...]
            dv_ref[...] = dvc_ref[...]

    cur = lambda w: pl.BlockSpec((BLOCK, w), lambda i: (jnp.minimum(i, nb - 1), 0))
    prev = lambda w: pl.BlockSpec((BLOCK, w), lambda i: (jnp.clip(i - 1, 0, nb - 1), 0))
    lag = lambda w: pl.BlockSpec((BLOCK, w), lambda i: (jnp.maximum(i - 1, 0), 0))
    return pl.pallas_call(
        body, name="attn_bwd", grid=(nb + 1,),
        in_specs=[cur(ATTN_WIDTH), cur(KV_WIDTH), prev(KV_WIDTH), cur(KV_WIDTH), prev(KV_WIDTH), cur(ATTN_WIDTH),
                  cur(N_Q_HEADS), _full((1, N_Q_HEADS))],
        out_specs=[cur(ATTN_WIDTH), lag(KV_WIDTH), lag(KV_WIDTH), _full((1, N_Q_HEADS))],
        out_shape=[jax.ShapeDtypeStruct((s, ATTN_WIDTH), F32), jax.ShapeDtypeStruct((s, KV_WIDTH), F32),
                   jax.ShapeDtypeStruct((s, KV_WIDTH), F32), jax.ShapeDtypeStruct((1, N_Q_HEADS), F32)],
        scratch_shapes=[pltpu.VMEM((BLOCK, KV_WIDTH), F32), pltpu.VMEM((BLOCK, KV_WIDTH), F32)],
        compiler_params=_params("arbitrary"),
    )(q, k, k, v, v, dattn, lse, sinks)


def _bwd_inproj(dq, dk, dv, du, cos, sin, win_t, x, g_mix, dx2):
    s = x.shape[0]
    tm = _token_tile(s)

    def body(dq_ref, dk_ref, dv_ref, du_ref, cos_ref, sin_ref, w_ref, x_ref, g_ref, dx2_ref,
             dx_ref, dw_ref, db_ref, dg_ref):
        @pl.when(pl.program_id(0) == 0)
        def _():
            dw_ref[...] = jnp.zeros_like(dw_ref)
            db_ref[...] = jnp.zeros_like(db_ref)
            dg_ref[...] = jnp.zeros_like(dg_ref)

        cos_t, sin_t = cos_ref[...], sin_ref[...]
        dz32 = jnp.concatenate([_rope_bwd(dq_ref[...], cos_t, sin_t), _rope_bwd(dk_ref[...], cos_t, sin_t),
                                dv_ref[...], du_ref[...].astype(F32)], axis=1)
        db_ref[...] += jnp.sum(dz32, axis=0, keepdims=True)
        dz = dz32.astype(BF16)
        g = g_ref[...]
        n, r = _rms(x_ref[...])
        h = (n * g).astype(BF16)
        for m0 in range(0, IN_WIDTH, TN_ROW_CHUNK):
            dw_ref[m0:m0 + TN_ROW_CHUNK, :] += _tn(dz[:, m0:m0 + TN_ROW_CHUNK], h)
        dx, dg = _rms_bwd(_nn(dz, w_ref[...]), n, r, g)
        dg_ref[...] += dg
        dx_ref[...] = dx2_ref[...] + dx

    return pl.pallas_call(
        body, name="bwd_inproj", grid=(s // tm,),
        in_specs=[_rows(tm, ATTN_WIDTH), _rows(tm, KV_WIDTH), _rows(tm, KV_WIDTH), _rows(tm, POOL_WIDTH),
                  _rows(tm, LANES), _rows(tm, LANES), _full((IN_WIDTH, D_MODEL)), _rows(tm, D_MODEL),
                  _full((1, D_MODEL)), _rows(tm, D_MODEL)],
        out_specs=[_rows(tm, D_MODEL), _full((IN_WIDTH, D_MODEL)), _full((1, IN_WIDTH)), _full((1, D_MODEL))],
        out_shape=[jax.ShapeDtypeStruct((s, D_MODEL), F32), jax.ShapeDtypeStruct((IN_WIDTH, D_MODEL), F32),
                   jax.ShapeDtypeStruct((1, IN_WIDTH), F32), jax.ShapeDtypeStruct((1, D_MODEL), F32)],
        compiler_params=_params("arbitrary"),
    )(dq, dk, dv, du, cos, sin, win_t, x, g_mix, dx2)


def _rope_tables(s):
    inv_freq = 1.0 / (ROPE_THETA ** (jnp.arange(0, HEAD_DIM, 2, dtype=F32) / HEAD_DIM))
    ang = jnp.arange(s, dtype=F32)[:, None] * inv_freq[None, :]
    cos, sin = jnp.cos(ang), jnp.sin(ang)
    return jnp.tile(cos, (1, 4)), jnp.tile(jnp.concatenate([-sin, sin], axis=1), (1, 2))


def _local_step(x, target, g_mix, b_in, sinks, w_pool, b_pool, pool_scale, b_out, g_ffn, g_final,
                win_t, w_out, wg_t, wu_t, wd):
    cos, sin = _rope_tables(x.shape[0])
    wp_b = w_pool.astype(BF16)
    bp = b_pool.reshape(1, POOL_WIDTH)
    ps = pool_scale.reshape(1, POOL_WIDTH)
    q, k, v, mixed, pool = _fwd_inproj(x, g_mix, win_t, b_in, cos, sin, wp_b, bp, ps)
    attn, lse = _attn_fwd(q, k, v, sinks)
    x2, h2 = _fwd_outproj(attn, pool, w_out, b_out, x, g_ffn)
    gate, up, dx3, dx3b, sq, dg_final = _fwd_ffn_loss(h2, x2, wg_t, wu_t, wd, g_final, target)
    dgate, dup, act, dx2, dx2b, dg_ffn, db_out = _bwd_ffn(dx3, gate, up, wd, wg_t, wu_t, x2, g_ffn)
    d_wd = _weight_grad(act, dx3b, "grad_w_down")
    d_wg_t = _weight_grad(dgate, h2, "grad_w_gate")
    d_wu_t = _weight_grad(dup, h2, "grad_w_up")
    dattn, du, d_wout, d_wpool, d_bpool, d_pscale = _bwd_outproj_pool(dx2b, attn, pool, mixed, w_out, wp_b, bp, ps)
    dq, dk, dv, d_sinks = _attn_bwd(q, k, v, dattn, lse, sinks)
    dx, d_win_t, d_bin, d_gmix = _bwd_inproj(dq, dk, dv, du, cos, sin, win_t, x, g_mix, dx2)
    small = dict(g_mix=d_gmix, b_in=d_bin, sinks=d_sinks, w_pool=d_wpool, b_pool=d_bpool, pool_scale=d_pscale,
                 b_out=db_out, g_ffn=dg_ffn, g_final=dg_final, loss=sq)
    return dx, (d_win_t, d_wout, d_wg_t, d_wu_t, d_wd), small


def _place():
    return lax.axis_index("x"), lax.axis_index("y"), lax.axis_index("c")


def _other_chips(x, y):
    return [(1 - x, y), (x, 1 - y), (1 - x, 1 - y)]


def _all_gather_rows(blocks, name):
    nm = len(blocks)

    def body(*refs):
        ins, outs = refs[:nm], refs[nm:2 * nm]
        send_sems, recv_sems, local_sems = refs[2 * nm:]
        x, y, c = _place()
        me, sibling = (x, y, c), (x, y, 1 - c)
        chips = _other_chips(x, y)

        def rows(m, px, py, pc):
            r = ins[m].shape[0]
            return outs[m].at[pl.ds((4 * px + 2 * py + pc) * r, r), :]

        def copy(m, k, block, to, src=None):
            return pltpu.make_async_remote_copy(
                src_ref=rows(m, *block) if src is None else src, dst_ref=rows(m, *block),
                send_sem=send_sems.at[k * nm + m], recv_sem=recv_sems.at[k * nm + m],
                device_id=to, device_id_type=MESH)

        mine = [pltpu.make_async_copy(ins[m], rows(m, *me), local_sems.at[m]) for m in range(nm)]
        for cp in mine:
            cp.start()
        first = [copy(m, 0, me, sibling, src=ins[m]) for m in range(nm)]
        first += [copy(m, 1 + j, me, (*chip, c), src=ins[m]) for j, chip in enumerate(chips) for m in range(nm)]
        for cp in first:
            cp.start()
        passed = []
        for j, chip in enumerate(chips):
            for m in range(nm):
                copy(m, 1 + j, (*chip, c), me).wait_recv()
                fwd = copy(m, 4 + j, (*chip, c), sibling)
                fwd.start()
                passed.append(fwd)
        for m in range(nm):
            copy(m, 0, sibling, me).wait_recv()
        for j, chip in enumerate(chips):
            for m in range(nm):
                copy(m, 4 + j, (*chip, 1 - c), me).wait_recv()
        for cp in first + passed:
            cp.wait_send()
        for cp in mine:
            cp.wait()

    return pl.pallas_call(
        body, name=name,
        in_specs=[HBM] * nm, out_specs=[HBM] * nm,
        out_shape=[jax.ShapeDtypeStruct((N_DEV * b.shape[0], b.shape[1]), b.dtype) for b in blocks],
        scratch_shapes=[pltpu.SemaphoreType.DMA((7 * nm,)), pltpu.SemaphoreType.DMA((7 * nm,)), pltpu.SemaphoreType.DMA((nm,))],
        compiler_params=pltpu.CompilerParams(has_side_effects=True),
    )(*blocks)


def _exchange_with_sibling(grads):
    nm = len(grads)

    def body(*refs):
        ins, outs = refs[:nm], refs[nm:2 * nm]
        send_sems, recv_sems = refs[2 * nm:]
        x, y, c = _place()
        sibling = (x, y, 1 - c)
        copies = []
        for m in range(nm):
            r = ins[m].shape[0] // N_DEV
            for q in range(N_CHIPS):
                copies.append(pltpu.make_async_remote_copy(
                    src_ref=ins[m].at[pl.ds((2 * q + 1 - c) * r, r), :], dst_ref=outs[m].at[pl.ds(q * r, r), :],
                    send_sem=send_sems.at[m * N_CHIPS + q], recv_sem=recv_sems.at[m * N_CHIPS + q],
                    device_id=sibling, device_id_type=MESH))
        for cp in copies:
            cp.start()
        for cp in copies:
            cp.wait_recv()
        for cp in copies:
            cp.wait_send()

    return pl.pallas_call(
        body, name="grad_exchange_sibling",
        in_specs=[HBM] * nm, out_specs=[HBM] * nm,
        out_shape=[jax.ShapeDtypeStruct((g.shape[0] // 2, g.shape[1]), F32) for g in grads],
        scratch_shapes=[pltpu.SemaphoreType.DMA((nm * N_CHIPS,)), pltpu.SemaphoreType.DMA((nm * N_CHIPS,))],
        compiler_params=pltpu.CompilerParams(has_side_effects=True),
    )(*grads)


def _chip_sum(grad, from_sibling, place):
    r = grad.shape[0] // N_DEV
    w = grad.shape[1]

    def body(place_ref, g_ref, s_ref, wire_ref, own_ref):
        total = g_ref[...] + s_ref[...]
        wire_ref[...] = total.astype(BF16)

        @pl.when(pl.program_id(0) == place_ref[1])
        def _():
            own_ref[...] = total

    grid_spec = pltpu.PrefetchScalarGridSpec(
        num_scalar_prefetch=1, grid=(N_CHIPS,),
        in_specs=[pl.BlockSpec((r, w), lambda q, p: (2 * q + p[0], 0)), pl.BlockSpec((r, w), lambda q, p: (q, 0))],
        out_specs=[pl.BlockSpec((r, w), lambda q, p: (q, 0)), pl.BlockSpec((r, w), lambda q, p: (0, 0))])
    return pl.pallas_call(
        body, name="grad_chip_sum", grid_spec=grid_spec,
        out_shape=[jax.ShapeDtypeStruct((N_CHIPS * r, w), BF16), jax.ShapeDtypeStruct((r, w), F32)],
        compiler_params=_params("arbitrary"),
    )(place, grad, from_sibling)


def _exchange_between_chips(wires):
    nm = len(wires)

    def body(*refs):
        ins, outs = refs[:nm], refs[nm:2 * nm]
        send_sems, recv_sems = refs[2 * nm:]
        x, y, c = _place()
        copies = []
        for m in range(nm):
            r = ins[m].shape[0] // N_CHIPS
            for j, (px, py) in enumerate(_other_chips(x, y)):
                copies.append(pltpu.make_async_remote_copy(
                    src_ref=ins[m].at[pl.ds((2 * px + py) * r, r), :], dst_ref=outs[m].at[pl.ds(j * r, r), :],
                    send_sem=send_sems.at[m * 3 + j], recv_sem=recv_sems.at[m * 3 + j],
                    device_id=(px, py, c), device_id_type=MESH))
        for cp in copies:
            cp.start()
        for cp in copies:
            cp.wait_recv()
        for cp in copies:
            cp.wait_send()

    return pl.pallas_call(
        body, name="grad_exchange_chips",
        in_specs=[HBM] * nm, out_specs=[HBM] * nm,
        out_shape=[jax.ShapeDtypeStruct((3 * (w.shape[0] // N_CHIPS), w.shape[1]), BF16) for w in wires],
        scratch_shapes=[pltpu.SemaphoreType.DMA((nm * 3,)), pltpu.SemaphoreType.DMA((nm * 3,))],
        compiler_params=pltpu.CompilerParams(has_side_effects=True),
    )(*wires)


def _final_sum(own, received):
    r, w = own.shape

    def body(own_ref, rec_ref, o_ref):
        o_ref[...] = ((own_ref[...] + rec_ref[0:r, :].astype(F32)) + rec_ref[r:2 * r, :].astype(F32)) + rec_ref[2 * r:, :].astype(F32)

    return pl.pallas_call(
        body, name="grad_final_sum", in_specs=[VMEM, VMEM], out_specs=VMEM,
        out_shape=jax.ShapeDtypeStruct((r, w), F32), compiler_params=_params(),
    )(own, received)


def _adamw_math(w, g, m, v):
    m = ADAM_B1 * m + (1.0 - ADAM_B1) * g
    v = ADAM_B2 * v + (1.0 - ADAM_B2) * jnp.square(g)
    m_hat = m / (1.0 - ADAM_B1 ** ADAM_STEP)
    v_hat = v / (1.0 - ADAM_B2 ** ADAM_STEP)
    delta = -ADAM_LR * (m_hat / (jnp.sqrt(v_hat) + ADAM_EPS) + ADAM_WD * w)
    return delta, m, v


def _adamw(w, g, m, v):
    def body(w_ref, g_ref, m_ref, v_ref, d_ref, nm_ref, nv_ref):
        d_ref[...], nm_ref[...], nv_ref[...] = _adamw_math(w_ref[...], g_ref[...], m_ref[...], v_ref[...])

    shape = jax.ShapeDtypeStruct(w.shape, F32)
    return pl.pallas_call(
        body, name="adamw", in_specs=[VMEM] * 4, out_specs=[VMEM] * 3, out_shape=[shape] * 3,
        compiler_params=_params(),
    )(w, g, m, v)


SMALL = (("g_mix", 1024), ("b_in", 1280), ("sinks", 8), ("w_pool", 65536), ("b_pool", 512), ("pool_scale", 512),
         ("b_out", 1024), ("g_ffn", 1024), ("g_final", 1024), ("loss", 1024))


def _small_rows(size):
    return -(-size // (8 * LANES)) * 8


SMALL_ROWS = sum(_small_rows(size) for _, size in SMALL)


def _pack_small(values):
    parts = []
    for name, size in SMALL:
        flat = values[name].reshape(-1).astype(F32)
        parts.append(jnp.pad(flat, (0, _small_rows(size) * LANES - size)).reshape(-1, LANES))
    return jnp.concatenate(parts, axis=0)


def _unpack_small(packed, shapes):
    out, row = {}, 0
    for name, size in SMALL:
        rows = _small_rows(size)
        if name in shapes:
            out[name] = packed[row:row + rows].reshape(-1)[:size].reshape(shapes[name])
        row += rows
    return out


def _small_allreduce_adamw(part, w, m, v):
    rows_n = SMALL_ROWS

    def body(p_ref, w_ref, m_ref, v_ref, g_ref, d_ref, nm_ref, nv_ref, all_ref, send_sems, recv_sems, local_sem):
        x, y, c = _place()
        me, sibling = (x, y, c), (x, y, 1 - c)
        chips = _other_chips(x, y)

        def rows(px, py, pc):
            return all_ref.at[pl.ds((4 * px + 2 * py + pc) * rows_n, rows_n), :]

        def copy(k, block, to, src=None):
            return pltpu.make_async_remote_copy(
                src_ref=rows(*block) if src is None else src, dst_ref=rows(*block),
                send_sem=send_sems.at[k], recv_sem=recv_sems.at[k], device_id=to, device_id_type=MESH)

        mine = pltpu.make_async_copy(p_ref, rows(*me), local_sem)
        mine.start()
        first = [copy(0, me, sibling, src=p_ref)]
        first += [copy(1 + j, me, (*chip, c), src=p_ref) for j, chip in enumerate(chips)]
        for cp in first:
            cp.start()
        passed = [copy(4 + j, (*chip, c), sibling) for j, chip in enumerate(chips)]
        for j, chip in enumerate(chips):
            copy(1 + j, (*chip, c), me).wait_recv()
            passed[j].start()
        copy(0, sibling, me).wait_recv()
        for j, chip in enumerate(chips):
            copy(4 + j, (*chip, 1 - c), me).wait_recv()
        for cp in first + passed:
            cp.wait_send()
        mine.wait()
        total = all_ref[0:rows_n, :]
        for dev in range(1, N_DEV):
            total = total + all_ref[dev * rows_n:(dev + 1) * rows_n, :]
        g_ref[...] = total
        d_ref[...], nm_ref[...], nv_ref[...] = _adamw_math(w_ref[...], total, m_ref[...], v_ref[...])

    shape = jax.ShapeDtypeStruct((rows_n, LANES), F32)
    return pl.pallas_call(
        body, name="small_allreduce_adamw", in_specs=[VMEM] * 4, out_specs=[VMEM] * 4, out_shape=[shape] * 4,
        scratch_shapes=[pltpu.VMEM((N_DEV * rows_n, LANES), F32), pltpu.SemaphoreType.DMA((7,)), pltpu.SemaphoreType.DMA((7,)),
                        pltpu.SemaphoreType.DMA],
        compiler_params=pltpu.CompilerParams(has_side_effects=True, vmem_limit_bytes=VMEM_LIMIT_BYTES),
    )(part, w, m, v)


def kernel(x, g_mix, w_in, b_in, sinks, w_pool, b_pool, pool_scale, w_out, b_out, g_ffn, w_gate, w_up, w_down, g_final, loss_target, m_g_mix, m_w_in, m_b_in, m_sinks, m_w_pool, m_b_pool, m_pool_scale, m_w_out, m_b_out, m_g_ffn, m_w_gate, m_w_up, m_w_down, m_g_final, v_g_mix, v_w_in, v_b_in, v_sinks, v_w_pool, v_b_pool, v_pool_scale, v_w_out, v_b_out, v_g_ffn, v_w_gate, v_w_up, v_w_down, v_g_final):
    weights = dict(g_mix=g_mix, w_in=w_in, b_in=b_in, sinks=sinks, w_pool=w_pool, b_pool=b_pool, pool_scale=pool_scale,
                   w_out=w_out, b_out=b_out, g_ffn=g_ffn, w_gate=w_gate, w_up=w_up, w_down=w_down, g_final=g_final)
    mom1 = dict(g_mix=m_g_mix, w_in=m_w_in, b_in=m_b_in, sinks=m_sinks, w_pool=m_w_pool, b_pool=m_b_pool,
                pool_scale=m_pool_scale, w_out=m_w_out, b_out=m_b_out, g_ffn=m_g_ffn, w_gate=m_w_gate, w_up=m_w_up,
                w_down=m_w_down, g_final=m_g_final)
    mom2 = dict(g_mix=v_g_mix, w_in=v_w_in, b_in=v_b_in, sinks=v_sinks, w_pool=v_w_pool, b_pool=v_b_pool,
                pool_scale=v_pool_scale, w_out=v_w_out, b_out=v_b_out, g_ffn=v_g_ffn, w_gate=v_w_gate, w_up=v_w_up,
                w_down=v_w_down, g_final=v_g_final)
    order = ("g_mix", "w_in", "b_in", "sinks", "w_pool", "b_pool", "pool_scale", "w_out", "b_out", "g_ffn",
             "w_gate", "w_up", "w_down", "g_final")
    big = ("w_in", "w_out", "w_gate", "w_up", "w_down")
    transposed = ("w_in", "w_gate", "w_up")

    def row_shard(name, a):
        return a[0].T if name in transposed else a[0]

    gathered = _all_gather_rows([row_shard(n, weights[n]).astype(BF16) for n in big], "weights_all_gather")

    dx, grads, small = _local_step(x[0], loss_target[0], g_mix, b_in, sinks, w_pool[0], b_pool[0], pool_scale[0],
                                   b_out, g_ffn, g_final.reshape(1, D_MODEL), *gathered)

    px, py, pc = _place()
    place = jnp.stack([pc, 2 * px + py]).astype(jnp.int32)
    from_sibling = _exchange_with_sibling(list(grads))
    sums = [_chip_sum(g, s, place) for g, s in zip(grads, from_sibling)]
    received = _exchange_between_chips([wire for wire, _ in sums])
    grad, delta, new_m, new_v = {}, {}, {}, {}
    for n, (_, own), rec in zip(big, sums, received):
        g = _final_sum(own, rec)
        if n in transposed:
            g = g.T
        d, nm_, nv_ = _adamw(weights[n][0], g, mom1[n][0], mom2[n][0])
        grad[n], delta[n], new_m[n], new_v[n] = g[None], d[None], nm_[None], nv_[None]

    shapes = {n: weights[n].shape for n in order if n not in big}
    zero_loss = jnp.zeros((1, D_MODEL), F32)
    packed = _small_allreduce_adamw(
        _pack_small(small), _pack_small({**weights, "loss": zero_loss}),
        _pack_small({**mom1, "loss": zero_loss}), _pack_small({**mom2, "loss": zero_loss}))
    for store, pk in zip((grad, delta, new_m, new_v), packed):
        store.update(_unpack_small(pk, shapes))
    loss_rows = _unpack_small(packed[0], {"loss": (D_MODEL,)})["loss"]
    loss = (0.5 / D_MODEL) * jnp.sum(loss_rows)

    return (loss, dx[None], *[grad[n] for n in order], *[delta[n] for n in order],
            *[new_m[n] for n in order], *[new_v[n] for n in order])
```

```python
import functools

import jax
import jax.numpy as jnp
from jax import lax
from jax.experimental import pallas as pl
from jax.experimental.pallas import tpu as pltpu

D_MODEL = 1024
ATTN_WIDTH = 512
KV_WIDTH = 128
POOL_WIDTH = 512
HEAD_DIM = 64
N_Q_HEADS = 8
N_KV_HEADS = 2
GQA_GROUP = 4
BLOCK = 128
POOL_SIZES = (2, 4, 8, 16)
POOL_GROUP_WIDTH = 128
POOL_HALO = 16
IN_WIDTH = 1280
D_FF = 2816
RMS_EPS = 1e-5
ROPE_THETA = 10000.0
Q_SCALE = HEAD_DIM ** -0.5

ADAM_LR = 0.001
ADAM_B1 = 0.9
ADAM_B2 = 0.999
ADAM_EPS = 1e-08
ADAM_WD = 0.01
ADAM_STEP = 10

N_DEV = 8
N_CHIPS = 4
LANES = 128
VMEM_LIMIT_BYTES = 56 * 1024 * 1024

F32 = jnp.float32
BF16 = jnp.bfloat16
MESH = pl.DeviceIdType.MESH
HBM = pl.BlockSpec(memory_space=pltpu.HBM)
VMEM = pl.BlockSpec(memory_space=pltpu.VMEM)


def _params(*semantics):
    return pltpu.CompilerParams(dimension_semantics=semantics or None, vmem_limit_bytes=VMEM_LIMIT_BYTES)


def _nn(a, b):
    return jnp.dot(a, b, preferred_element_type=F32)


def _nt(a, b):
    return lax.dot_general(a, b, (((1,), (1,)), ((), ())), preferred_element_type=F32)


def _tn(a, b):
    return lax.dot_general(a, b, (((0,), (0,)), ((), ())), preferred_element_type=F32)


def _full(shape):
    return pl.BlockSpec(shape, lambda *_: (0,) * len(shape))


def _rows(tm, width):
    return pl.BlockSpec((tm, width), lambda i, *_: (i, 0))


def _rot_half(t):
    n = t.shape[1]
    lane = lax.broadcasted_iota(jnp.int32, t.shape, 1)
    return jnp.where((lane % HEAD_DIM) < HEAD_DIM // 2, pltpu.roll(t, n - HEAD_DIM // 2, 1), pltpu.roll(t, HEAD_DIM // 2, 1))


def _rope(t, cos, sin):
    reps = t.shape[1] // LANES
    if reps > 1:
        cos, sin = jnp.tile(cos, (1, reps)), jnp.tile(sin, (1, reps))
    return t * cos + _rot_half(t) * sin


def _rope_bwd(d, cos, sin):
    reps = d.shape[1] // LANES
    if reps > 1:
        cos, sin = jnp.tile(cos, (1, reps)), jnp.tile(sin, (1, reps))
    return d * cos + _rot_half(d * sin)


KV_SPREAD = 4 * LANES


def _spread_kv(t):
    low = lax.broadcasted_iota(jnp.int32, t.shape, 1) < HEAD_DIM
    swapped = pltpu.roll(t, HEAD_DIM, 1)
    zero = jnp.zeros_like(t)
    return jnp.concatenate([jnp.where(low, t, zero), jnp.where(low, zero, swapped),
                            jnp.where(low, swapped, zero), jnp.where(low, zero, t)], axis=1)


def _rms(x):
    r = lax.rsqrt(jnp.mean(x * x, axis=-1, keepdims=True) + RMS_EPS)
    return x * r, r


def _rms_bwd(dh, n, r, g):
    dn = dh * g
    dx = r * (dn - n * jnp.mean(dn * n, axis=-1, keepdims=True))
    return dx, jnp.sum(dh * n, axis=0, keepdims=True)


def _token_tile(s):
    return min(512, s)


def _fwd_inproj(x, g_mix, win_t, b_in, cos, sin, w_pool, b_pool, pool_scale):
    s = x.shape[0]
    tm = _token_tile(s)

    def body(x_ref, g_ref, w_ref, b_ref, cos_ref, sin_ref, wp_ref, bp_ref, ps_ref,
             q_ref, k_ref, v_ref, mix_ref, pool_ref, tail_ref):
        i = pl.program_id(0)

        @pl.when(i == 0)
        def _():
            tail_ref[...] = jnp.zeros_like(tail_ref)

        n, _ = _rms(x_ref[...])
        h = (n * g_ref[...]).astype(BF16)
        z = _nt(h, w_ref[...]) + b_ref[...]
        cos_t, sin_t = cos_ref[...], sin_ref[...]
        q_ref[...] = (_rope(z[:, :ATTN_WIDTH], cos_t, sin_t) * Q_SCALE).astype(BF16)
        k_ref[...] = _spread_kv(_rope(z[:, ATTN_WIDTH:ATTN_WIDTH + KV_WIDTH], cos_t, sin_t)).astype(BF16)
        v_ref[...] = _spread_kv(z[:, ATTN_WIDTH + KV_WIDTH:ATTN_WIDTH + 2 * KV_WIDTH]).astype(BF16)
        u = z[:, ATTN_WIDTH + 2 * KV_WIDTH:]
        u_ext = jnp.concatenate([tail_ref[...], u], axis=0)
        tail_ref[...] = u[tm - POOL_HALO:, :]
        pos = lax.broadcasted_iota(jnp.int32, (tm, POOL_GROUP_WIDTH), 0) + i * tm
        for g, size in enumerate(POOL_SIZES):
            cols = slice(g * POOL_GROUP_WIDTH, (g + 1) * POOL_GROUP_WIDTH)
            a = u_ext[:, cols]
            shift = 1
            while shift < size:
                a = a + pltpu.roll(a, shift, 0)
                shift *= 2
            count = jnp.minimum(pos + 1, size).astype(F32)
            mixed = (a[POOL_HALO:, :] / count - u[:, cols]).astype(BF16)
            pre = _nn(mixed, wp_ref[g]) + bp_ref[:, cols]
            mix_ref[:, cols] = mixed
            pool_ref[:, cols] = (pre * ps_ref[:, cols]).astype(BF16)

    bf = lambda w: jax.ShapeDtypeStruct((s, w), BF16)
    return pl.pallas_call(
        body, name="fwd_inproj", grid=(s // tm,),
        in_specs=[_rows(tm, D_MODEL), _full((1, D_MODEL)), _full((IN_WIDTH, D_MODEL)), _full((1, IN_WIDTH)),
                  _rows(tm, LANES), _rows(tm, LANES), _full((4, POOL_GROUP_WIDTH, POOL_GROUP_WIDTH)),
                  _full((1, POOL_WIDTH)), _full((1, POOL_WIDTH))],
        out_specs=[_rows(tm, ATTN_WIDTH), _rows(tm, KV_SPREAD), _rows(tm, KV_SPREAD), _rows(tm, POOL_WIDTH), _rows(tm, POOL_WIDTH)],
        out_shape=[bf(ATTN_WIDTH), bf(KV_SPREAD), bf(KV_SPREAD), bf(POOL_WIDTH), bf(POOL_WIDTH)],
        scratch_shapes=[pltpu.VMEM((POOL_HALO, POOL_WIDTH), F32)],
        compiler_params=_params("arbitrary"),
    )(x, g_mix, win_t, b_in, cos, sin, w_pool, b_pool, pool_scale)


ATTN_TILE = 512
PAIR = 2 * LANES


def _band_masks(tile):
    r = lax.broadcasted_iota(jnp.int32, (2 * BLOCK, 4 * BLOCK), 0) % BLOCK
    j = lax.broadcasted_iota(jnp.int32, (2 * BLOCK, 4 * BLOCK), 1) % (2 * BLOCK)
    band = (j > r) & (j <= r + BLOCK)
    return band & ((tile > 0) | (j >= BLOCK)), band


def _band(cur_ref, prev_ref, b, kv):
    halves = []
    for half in range(2):
        cols = slice(kv * PAIR + half * LANES, kv * PAIR + (half + 1) * LANES)
        if b == 0:
            halves.append(jnp.concatenate([prev_ref[:, cols], cur_ref[0:BLOCK, cols]], axis=0))
        else:
            halves.append(cur_ref[(b - 1) * BLOCK:(b + 1) * BLOCK, cols])
    return jnp.concatenate(halves, axis=0)


def _stack_pair(ref, rows, kv):
    return jnp.concatenate([ref[rows, kv * PAIR:kv * PAIR + LANES], ref[rows, kv * PAIR + LANES:(kv + 1) * PAIR]], axis=0)


def _pair_heads(kv, half):
    return GQA_GROUP * kv + half, GQA_GROUP * kv + 2 + half


def _per_row(ref, rows, top, bottom):
    return jnp.concatenate([ref[rows, top:top + 1], ref[rows, bottom:bottom + 1]], axis=0)


def _sink_rows(sink_ref, top, bottom):
    upper = lax.broadcasted_iota(jnp.int32, (2 * BLOCK, 1), 0) < BLOCK
    return jnp.where(upper, sink_ref[:, top:top + 1], sink_ref[:, bottom:bottom + 1])


def _attn_fwd(q, kz, vz, sinks):
    s = q.shape[0]
    tq = min(ATTN_TILE, s)

    def body(q_ref, k_ref, kp_ref, v_ref, vp_ref, sink_ref, o_ref, lse_ref):
        first, band = _band_masks(pl.program_id(0))
        for b in range(tq // BLOCK):
            rows = slice(b * BLOCK, (b + 1) * BLOCK)
            mask = first if b == 0 else band
            for kv in range(N_KV_HEADS):
                sc = jnp.where(mask, _nt(_stack_pair(q_ref, rows, kv), _band(k_ref, kp_ref, b, kv)), -jnp.inf)
                probs = []
                for half in range(2):
                    top, bottom = _pair_heads(kv, half)
                    sink = _sink_rows(sink_ref, top, bottom)
                    sh = sc[:, half * 2 * BLOCK:(half + 1) * 2 * BLOCK]
                    m = jnp.maximum(jnp.max(sh, axis=-1, keepdims=True), sink)
                    p = jnp.exp(sh - m)
                    denom = jnp.sum(p, axis=-1, keepdims=True) + jnp.exp(sink - m)
                    probs.append((p / denom).astype(BF16))
                    lse = m + jnp.log(denom)
                    lse_ref[rows, top:top + 1] = lse[:BLOCK]
                    lse_ref[rows, bottom:bottom + 1] = lse[BLOCK:]
                o = _nn(jnp.concatenate(probs, axis=1), _band(v_ref, vp_ref, b, kv)).astype(BF16)
                o_ref[rows, kv * PAIR:kv * PAIR + LANES] = o[:BLOCK]
                o_ref[rows, kv * PAIR + LANES:(kv + 1) * PAIR] = o[BLOCK:]

    per = tq // BLOCK
    cur = lambda w: pl.BlockSpec((tq, w), lambda i: (i, 0))
    prev = pl.BlockSpec((BLOCK, KV_SPREAD), lambda i: (jnp.maximum(per * i - 1, 0), 0))
    return pl.pallas_call(
        body, name="attn_fwd", grid=(s // tq,),
        in_specs=[cur(ATTN_WIDTH), cur(KV_SPREAD), prev, cur(KV_SPREAD), prev, _full((1, N_Q_HEADS))],
        out_specs=[cur(ATTN_WIDTH), cur(N_Q_HEADS)],
        out_shape=[jax.ShapeDtypeStruct((s, ATTN_WIDTH), BF16), jax.ShapeDtypeStruct((s, N_Q_HEADS), F32)],
        compiler_params=_params("parallel"),
    )(q, kz, kz, vz, vz, sinks)


def _fwd_outproj(attn, pool, w_out, b_out, x, g_ffn):
    s = x.shape[0]
    tm = _token_tile(s)

    def body(a_ref, p_ref, w_ref, b_ref, x_ref, g_ref, x2_ref, h2_ref):
        x2 = x_ref[...] + _nn(a_ref[...], w_ref[:ATTN_WIDTH, :]) + _nn(p_ref[...], w_ref[ATTN_WIDTH:, :]) + b_ref[...]
        x2_ref[...] = x2
        n, _ = _rms(x2)
        h2_ref[...] = (n * g_ref[...]).astype(BF16)

    return pl.pallas_call(
        body, name="fwd_outproj", grid=(s // tm,),
        in_specs=[_rows(tm, ATTN_WIDTH), _rows(tm, POOL_WIDTH), _full((D_MODEL, D_MODEL)), _full((1, D_MODEL)),
                  _rows(tm, D_MODEL), _full((1, D_MODEL))],
        out_specs=[_rows(tm, D_MODEL), _rows(tm, D_MODEL)],
        out_shape=[jax.ShapeDtypeStruct((s, D_MODEL), F32), jax.ShapeDtypeStruct((s, D_MODEL), BF16)],
        compiler_params=_params("parallel"),
    )(attn, pool, w_out, b_out, x, g_ffn)


FF_CHUNK = 256


def _resident(shape):
    return pl.BlockSpec(shape, lambda *_: (0,) * len(shape), pipeline_mode=pl.Buffered(1))


def _fwd_ffn_act(h2, wg_t, wu_t):
    s = h2.shape[0]
    tm = _token_tile(s)

    def body(h_ref, wg_ref, wu_ref, gate_ref, up_ref, act_ref):
        h = h_ref[...]
        for c0 in range(0, D_FF, FF_CHUNK):
            cols = slice(c0, c0 + FF_CHUNK)
            gate = _nt(h, wg_ref[cols, :])
            up = _nt(h, wu_ref[cols, :])
            gate_ref[:, cols] = gate.astype(BF16)
            up_ref[:, cols] = up.astype(BF16)
            act_ref[:, cols] = (gate * jax.nn.sigmoid(gate) * up).astype(BF16)

    act_shape = jax.ShapeDtypeStruct((s, D_FF), BF16)
    return pl.pallas_call(
        body, name="fwd_ffn_act", grid=(s // tm,),
        in_specs=[_rows(tm, D_MODEL), _resident((D_FF, D_MODEL)), _resident((D_FF, D_MODEL))],
        out_specs=[_rows(tm, D_FF)] * 3, out_shape=[act_shape] * 3,
        compiler_params=_params("parallel"),
    )(h2, wg_t, wu_t)


def _fwd_down_loss(act, x2, wd, g_final, target):
    s = x2.shape[0]
    tm = _token_tile(s)

    def body(a_ref, x2_ref, wd_ref, g_ref, t_ref, dx3_ref, dx3b_ref, sq_ref, dg_ref):
        @pl.when(pl.program_id(0) == 0)
        def _():
            sq_ref[...] = jnp.zeros_like(sq_ref)
            dg_ref[...] = jnp.zeros_like(dg_ref)

        x3 = x2_ref[...] + _nn(a_ref[...], wd_ref[...])
        n, r = _rms(x3)
        g = g_ref[...]
        diff = n * g - t_ref[...]
        sq_ref[...] += jnp.sum(diff * diff, axis=0, keepdims=True)
        dx3, dg = _rms_bwd(diff * (1.0 / D_MODEL), n, r, g)
        dg_ref[...] += dg
        dx3_ref[...] = dx3
        dx3b_ref[...] = dx3.astype(BF16)

    return pl.pallas_call(
        body, name="fwd_down_loss", grid=(s // tm,),
        in_specs=[_rows(tm, D_FF), _rows(tm, D_MODEL), _resident((D_FF, D_MODEL)), _full((1, D_MODEL)), _rows(tm, D_MODEL)],
        out_specs=[_rows(tm, D_MODEL), _rows(tm, D_MODEL), _full((1, D_MODEL)), _full((1, D_MODEL))],
        out_shape=[jax.ShapeDtypeStruct((s, D_MODEL), F32), jax.ShapeDtypeStruct((s, D_MODEL), BF16),
                   jax.ShapeDtypeStruct((1, D_MODEL), F32), jax.ShapeDtypeStruct((1, D_MODEL), F32)],
        compiler_params=_params("arbitrary"),
    )(act, x2, wd, g_final, target)


def _bwd_ffn_act(dx3b, gate, up, wd):
    s = dx3b.shape[0]
    tm = _token_tile(s)

    def body(dx3_ref, gate_ref, up_ref, wd_ref, dgate_ref, dup_ref):
        dx3 = dx3_ref[...]
        for c0 in range(0, D_FF, FF_CHUNK):
            cols = slice(c0, c0 + FF_CHUNK)
            dact = _nt(dx3, wd_ref[cols, :])
            gate = gate_ref[:, cols].astype(F32)
            up = up_ref[:, cols].astype(F32)
            sig = jax.nn.sigmoid(gate)
            silu = gate * sig
            dup_ref[:, cols] = (dact * silu).astype(BF16)
            dgate_ref[:, cols] = (dact * up * (sig + silu * (1.0 - sig))).astype(BF16)

    act_shape = jax.ShapeDtypeStruct((s, D_FF), BF16)
    return pl.pallas_call(
        body, name="bwd_ffn_act", grid=(s // tm,),
        in_specs=[_rows(tm, D_MODEL), _rows(tm, D_FF), _rows(tm, D_FF), _resident((D_FF, D_MODEL))],
        out_specs=[_rows(tm, D_FF)] * 2, out_shape=[act_shape] * 2,
        compiler_params=_params("parallel"),
    )(dx3b, gate, up, wd)


def _bwd_ffn_in(dgate, dup, wg_t, wu_t, x2, dx3, g_ffn):
    s = x2.shape[0]
    tm = _token_tile(s)

    def body(dgate_ref, dup_ref, wg_ref, wu_ref, x2_ref, dx3_ref, g_ref, dx2_ref, dx2b_ref, dg_ref, db_ref):
        @pl.when(pl.program_id(0) == 0)
        def _():
            dg_ref[...] = jnp.zeros_like(dg_ref)
            db_ref[...] = jnp.zeros_like(db_ref)

        dh2 = _nn(dgate_ref[...], wg_ref[...]) + _nn(dup_ref[...], wu_ref[...])
        n, r = _rms(x2_ref[...])
        dx, dg = _rms_bwd(dh2, n, r, g_ref[...])
        dx2 = dx3_ref[...] + dx
        dg_ref[...] += dg
        db_ref[...] += jnp.sum(dx2, axis=0, keepdims=True)
        dx2_ref[...] = dx2
        dx2b_ref[...] = dx2.astype(BF16)

    return pl.pallas_call(
        body, name="bwd_ffn_in", grid=(s // tm,),
        in_specs=[_rows(tm, D_FF), _rows(tm, D_FF), _resident((D_FF, D_MODEL)), _resident((D_FF, D_MODEL)),
                  _rows(tm, D_MODEL), _rows(tm, D_MODEL), _full((1, D_MODEL))],
        out_specs=[_rows(tm, D_MODEL), _rows(tm, D_MODEL), _full((1, D_MODEL)), _full((1, D_MODEL))],
        out_shape=[jax.ShapeDtypeStruct((s, D_MODEL), F32), jax.ShapeDtypeStruct((s, D_MODEL), BF16),
                   jax.ShapeDtypeStruct((1, D_MODEL), F32), jax.ShapeDtypeStruct((1, D_MODEL), F32)],
        compiler_params=_params("arbitrary"),
    )(dgate, dup, wg_t, wu_t, x2, dx3, g_ffn)


TN_ROW_CHUNK = 256


def _weight_grad(a, b, name):
    s, m = a.shape
    n_out = b.shape[1]
    tm = _token_tile(s)

    def body(a_ref, b_ref, o_ref):
        @pl.when(pl.program_id(0) == 0)
        def _():
            o_ref[...] = jnp.zeros_like(o_ref)

        bt = b_ref[...]
        for m0 in range(0, m, TN_ROW_CHUNK):
            o_ref[m0:m0 + TN_ROW_CHUNK, :] += _tn(a_ref[:, m0:m0 + TN_ROW_CHUNK], bt)

    return pl.pallas_call(
        body, name=name, grid=(s // tm,),
        in_specs=[_rows(tm, m), _rows(tm, n_out)],
        out_specs=_full((m, n_out)),
        out_shape=jax.ShapeDtypeStruct((m, n_out), F32),
        compiler_params=_params("arbitrary"),
    )(a, b)


def _bwd_outproj_pool(dx2b, attn, pool, mixed, w_out, w_pool, b_pool, pool_scale):
    s = dx2b.shape[0]
    tm = _token_tile(s)
    nt = s // tm

    def body(dx_ref, a_ref, p_ref, mix_ref, w_ref, wp_ref, bp_ref, ps_ref,
             dattn_ref, du_ref, dwout_ref, dwp_ref, dbp_ref, dps_ref, head_ref):
        step = pl.program_id(0)
        tile = nt - 1 - step

        @pl.when(step == 0)
        def _():
            head_ref[...] = jnp.zeros_like(head_ref)
            dwout_ref[...] = jnp.zeros_like(dwout_ref)
            dwp_ref[...] = jnp.zeros_like(dwp_ref)
            dbp_ref[...] = jnp.zeros_like(dbp_ref)
            dps_ref[...] = jnp.zeros_like(dps_ref)

        dx = dx_ref[...]
        dwout_ref[:ATTN_WIDTH, :] += _tn(a_ref[...], dx)
        dwout_ref[ATTN_WIDTH:, :] += _tn(p_ref[...], dx)
        dcat = _nt(dx, w_ref[...])
        dattn_ref[...] = dcat[:, :ATTN_WIDTH].astype(BF16)
        dpool = dcat[:, ATTN_WIDTH:]
        pos = lax.broadcasted_iota(jnp.int32, (tm, POOL_GROUP_WIDTH), 0) + tile * tm
        head = head_ref[...]
        n_ext = tm + POOL_HALO
        for g, size in enumerate(POOL_SIZES):
            cols = slice(g * POOL_GROUP_WIDTH, (g + 1) * POOL_GROUP_WIDTH)
            mixed_g = mix_ref[:, cols]
            pre = _nn(mixed_g, wp_ref[g]) + bp_ref[:, cols]
            dy = dpool[:, cols]
            dps_ref[:, cols] += jnp.sum(dy * pre, axis=0, keepdims=True)
            dpre = dy * ps_ref[:, cols]
            dbp_ref[:, cols] += jnp.sum(dpre, axis=0, keepdims=True)
            dpre_b = dpre.astype(BF16)
            dwp_ref[g] += _tn(mixed_g, dpre_b)
            dmixed = _nt(dpre_b, wp_ref[g])
            w = dmixed / jnp.minimum(pos + 1, size).astype(F32)
            head_ref[:, cols] = w[:POOL_HALO, :]
            a = jnp.concatenate([w, head[:, cols]], axis=0)
            shift = 1
            while shift < size:
                a = a + pltpu.roll(a, n_ext - shift, 0)
                shift *= 2
            du_ref[:, cols] = (a[:tm, :] - dmixed).astype(BF16)

    rev = lambda w: pl.BlockSpec((tm, w), lambda i: (nt - 1 - i, 0))
    return pl.pallas_call(
        body, name="bwd_outproj_pool", grid=(nt,),
        in_specs=[rev(D_MODEL), rev(ATTN_WIDTH), rev(POOL_WIDTH), rev(POOL_WIDTH), _full((D_MODEL, D_MODEL)),
                  _full((4, POOL_GROUP_WIDTH, POOL_GROUP_WIDTH)), _full((1, POOL_WIDTH)), _full((1, POOL_WIDTH))],
        out_specs=[rev(ATTN_WIDTH), rev(POOL_WIDTH), _full((D_MODEL, D_MODEL)),
                   _full((4, POOL_GROUP_WIDTH, POOL_GROUP_WIDTH)), _full((1, POOL_WIDTH)), _full((1, POOL_WIDTH))],
        out_shape=[jax.ShapeDtypeStruct((s, ATTN_WIDTH), BF16), jax.ShapeDtypeStruct((s, POOL_WIDTH), BF16),
                   jax.ShapeDtypeStruct((D_MODEL, D_MODEL), F32),
                   jax.ShapeDtypeStruct((4, POOL_GROUP_WIDTH, POOL_GROUP_WIDTH), F32),
                   jax.ShapeDtypeStruct((1, POOL_WIDTH), F32), jax.ShapeDtypeStruct((1, POOL_WIDTH), F32)],
        scratch_shapes=[pltpu.VMEM((POOL_HALO, POOL_WIDTH), F32)],
        compiler_params=_params("arbitrary"),
    )(dx2b, attn, pool, mixed, w_out, w_pool, b_pool, pool_scale)


def _fold_spread(t):
    low = lax.broadcasted_iota(jnp.int32, (2 * BLOCK, LANES), 1) < HEAD_DIM
    kept = jnp.where(low, t[:2 * BLOCK, :], t[2 * BLOCK:, :])
    return kept + pltpu.roll(kept, HEAD_DIM, 1)


def _attn_bwd(q, kz, vz, dattn, lse, sinks):
    s = q.shape[0]
    tq = min(ATTN_TILE, s)
    nt = s // tq
    per = tq // BLOCK

    def body(q_ref, k_ref, kp_ref, v_ref, vp_ref, do_ref, lse_ref, sink_ref,
             dq_ref, dk_ref, dv_ref, dsink_ref, dk_acc, dv_acc, dk_carry, dv_carry):
        step = pl.program_id(0)

        @pl.when(step == 0)
        def _():
            dk_carry[...] = jnp.zeros_like(dk_carry)
            dv_carry[...] = jnp.zeros_like(dv_carry)
            dsink_ref[...] = jnp.zeros_like(dsink_ref)

        dk_acc[0:tq, :] = jnp.zeros((tq, KV_WIDTH), F32)
        dv_acc[0:tq, :] = jnp.zeros((tq, KV_WIDTH), F32)
        dk_acc[tq:, :] = dk_carry[...]
        dv_acc[tq:, :] = dv_carry[...]
        first, band = _band_masks(nt - 1 - step)
        low = lax.broadcasted_iota(jnp.int32, (2 * BLOCK, LANES), 1) < HEAD_DIM
        for b in range(per):
            rows = slice(b * BLOCK, (b + 1) * BLOCK)
            mask = first if b == 0 else band
            dk_heads, dv_heads = [], []
            for kv in range(N_KV_HEADS):
                qab = _stack_pair(q_ref, rows, kv)
                doab = _stack_pair(do_ref, rows, kv)
                kzb = _band(k_ref, kp_ref, b, kv)
                sc = _nt(qab, kzb)
                dp = _nt(doab, _band(v_ref, vp_ref, b, kv))
                probs, dscores = [], []
                for half in range(2):
                    top, bottom = _pair_heads(kv, half)
                    cols = slice(half * 2 * BLOCK, (half + 1) * 2 * BLOCK)
                    lse_h = _per_row(lse_ref, rows, top, bottom)
                    p = jnp.where(mask[:, cols], jnp.exp(sc[:, cols] - lse_h), 0.0)
                    dph = dp[:, cols]
                    delta = jnp.sum(p * dph, axis=-1, keepdims=True)
                    probs.append(p.astype(BF16))
                    dscores.append((p * (dph - delta)).astype(BF16))
                    leak = jnp.exp(_sink_rows(sink_ref, top, bottom) - lse_h) * delta
                    dsink_ref[:, top:top + 1] -= jnp.sum(leak[:BLOCK], axis=0, keepdims=True)
                    dsink_ref[:, bottom:bottom + 1] -= jnp.sum(leak[BLOCK:], axis=0, keepdims=True)
                ds = jnp.concatenate(dscores, axis=1)
                dqab = _nn(ds, kzb) * Q_SCALE
                dq_ref[rows, kv * PAIR:kv * PAIR + LANES] = dqab[:BLOCK]
                dq_ref[rows, kv * PAIR + LANES:(kv + 1) * PAIR] = dqab[BLOCK:]
                dk_heads.append(_fold_spread(_tn(ds, qab)))
                dv_heads.append(_fold_spread(_tn(jnp.concatenate(probs, axis=1), doab)))
            band_rows = slice(b * BLOCK, (b + 2) * BLOCK)
            dk_acc[band_rows, :] += jnp.where(low, dk_heads[0], dk_heads[1])
            dv_acc[band_rows, :] += jnp.where(low, dv_heads[0], dv_heads[1])
        dk_ref[...] = dk_acc[BLOCK:, :]
        dv_ref[...] = dv_acc[BLOCK:, :]
        dk_carry[...] = dk_acc[0:BLOCK, :]
        dv_carry[...] = dv_acc[0:BLOCK, :]

    cur = lambda w: pl.BlockSpec((tq, w), lambda i: (nt - 1 - i, 0))
    prev = pl.BlockSpec((BLOCK, KV_SPREAD), lambda i: (jnp.maximum(per * (nt - 1 - i) - 1, 0), 0))
    acc = pltpu.VMEM((tq + BLOCK, KV_WIDTH), F32)
    carry = pltpu.VMEM((BLOCK, KV_WIDTH), F32)
    return pl.pallas_call(
        body, name="attn_bwd", grid=(nt,),
        in_specs=[cur(ATTN_WIDTH), cur(KV_SPREAD), prev, cur(KV_SPREAD), prev, cur(ATTN_WIDTH),
                  cur(N_Q_HEADS), _full((1, N_Q_HEADS))],
        out_specs=[cur(ATTN_WIDTH), cur(KV_WIDTH), cur(KV_WIDTH), _full((1, N_Q_HEADS))],
        out_shape=[jax.ShapeDtypeStruct((s, ATTN_WIDTH), F32), jax.ShapeDtypeStruct((s, KV_WIDTH), F32),
                   jax.ShapeDtypeStruct((s, KV_WIDTH), F32), jax.ShapeDtypeStruct((1, N_Q_HEADS), F32)],
        scratch_shapes=[acc, acc, carry, carry],
        compiler_params=_params("arbitrary"),
    )(q, kz, kz, vz, vz, dattn, lse, sinks)


def _bwd_inproj(dq, dk, dv, du, cos, sin, win_t, x, g_mix, dx2):
    s = x.shape[0]
    tm = _token_tile(s)

    def body(dq_ref, dk_ref, dv_ref, du_ref, cos_ref, sin_ref, w_ref, x_ref, g_ref, dx2_ref,
             dx_ref, dw_ref, db_ref, dg_ref):
        @pl.when(pl.program_id(0) == 0)
        def _():
            dw_ref[...] = jnp.zeros_like(dw_ref)
            db_ref[...] = jnp.zeros_like(db_ref)
            dg_ref[...] = jnp.zeros_like(dg_ref)

        cos_t, sin_t = cos_ref[...], sin_ref[...]
        dz32 = jnp.concatenate([_rope_bwd(dq_ref[...], cos_t, sin_t), _rope_bwd(dk_ref[...], cos_t, sin_t),
                                dv_ref[...], du_ref[...].astype(F32)], axis=1)
        db_ref[...] += jnp.sum(dz32, axis=0, keepdims=True)
        dz = dz32.astype(BF16)
        g = g_ref[...]
        n, r = _rms(x_ref[...])
        h = (n * g).astype(BF16)
        for m0 in range(0, IN_WIDTH, TN_ROW_CHUNK):
            dw_ref[m0:m0 + TN_ROW_CHUNK, :] += _tn(dz[:, m0:m0 + TN_ROW_CHUNK], h)
        dx, dg = _rms_bwd(_nn(dz, w_ref[...]), n, r, g)
        dg_ref[...] += dg
        dx_ref[...] = dx2_ref[...] + dx

    return pl.pallas_call(
        body, name="bwd_inproj", grid=(s // tm,),
        in_specs=[_rows(tm, ATTN_WIDTH), _rows(tm, KV_WIDTH), _rows(tm, KV_WIDTH), _rows(tm, POOL_WIDTH),
                  _rows(tm, LANES), _rows(tm, LANES), _full((IN_WIDTH, D_MODEL)), _rows(tm, D_MODEL),
                  _full((1, D_MODEL)), _rows(tm, D_MODEL)],
        out_specs=[_rows(tm, D_MODEL), _full((IN_WIDTH, D_MODEL)), _full((1, IN_WIDTH)), _full((1, D_MODEL))],
        out_shape=[jax.ShapeDtypeStruct((s, D_MODEL), F32), jax.ShapeDtypeStruct((IN_WIDTH, D_MODEL), F32),
                   jax.ShapeDtypeStruct((1, IN_WIDTH), F32), jax.ShapeDtypeStruct((1, D_MODEL), F32)],
        compiler_params=_params("arbitrary"),
    )(dq, dk, dv, du, cos, sin, win_t, x, g_mix, dx2)


def _rope_tables(s):
    inv_freq = 1.0 / (ROPE_THETA ** (jnp.arange(0, HEAD_DIM, 2, dtype=F32) / HEAD_DIM))
    ang = jnp.arange(s, dtype=F32)[:, None] * inv_freq[None, :]
    cos, sin = jnp.cos(ang), jnp.sin(ang)
    return jnp.tile(cos, (1, 4)), jnp.tile(jnp.concatenate([-sin, sin], axis=1), (1, 2))


def _local_step(x, target, g_mix, b_in, sinks, w_pool, b_pool, pool_scale, b_out, g_ffn, g_final,
                win_t, w_out, wg_t, wu_t, wd):
    cos, sin = _rope_tables(x.shape[0])
    wp_b = w_pool.astype(BF16)
    bp = b_pool.reshape(1, POOL_WIDTH)
    ps = pool_scale.reshape(1, POOL_WIDTH)
    q, k, v, mixed, pool = _fwd_inproj(x, g_mix, win_t, b_in, cos, sin, wp_b, bp, ps)
    attn, lse = _attn_fwd(q, k, v, sinks)
    x2, h2 = _fwd_outproj(attn, pool, w_out, b_out, x, g_ffn)
    gate, up, act = _fwd_ffn_act(h2, wg_t, wu_t)
    dx3, dx3b, sq, dg_final = _fwd_down_loss(act, x2, wd, g_final, target)
    dgate, dup = _bwd_ffn_act(dx3b, gate, up, wd)
    dx2, dx2b, dg_ffn, db_out = _bwd_ffn_in(dgate, dup, wg_t, wu_t, x2, dx3, g_ffn)
    d_wd = _weight_grad(act, dx3b, "grad_w_down")
    d_wg_t = _weight_grad(dgate, h2, "grad_w_gate")
    d_wu_t = _weight_grad(dup, h2, "grad_w_up")
    dattn, du, d_wout, d_wpool, d_bpool, d_pscale = _bwd_outproj_pool(dx2b, attn, pool, mixed, w_out, wp_b, bp, ps)
    dq, dk, dv, d_sinks = _attn_bwd(q, k, v, dattn, lse, sinks)
    dx, d_win_t, d_bin, d_gmix = _bwd_inproj(dq, dk, dv, du, cos, sin, win_t, x, g_mix, dx2)
    small = dict(g_mix=d_gmix, b_in=d_bin, sinks=d_sinks, w_pool=d_wpool, b_pool=d_bpool, pool_scale=d_pscale,
                 b_out=db_out, g_ffn=dg_ffn, g_final=dg_final, loss=sq)
    return dx, (d_win_t, d_wout, d_wg_t, d_wu_t, d_wd), small


def _place():
    return lax.axis_index("x"), lax.axis_index("y"), lax.axis_index("c")


def _other_chips(x, y):
    return [(1 - x, y), (x, 1 - y), (1 - x, 1 - y)]


def _all_gather_rows(blocks, name):
    nm = len(blocks)

    def body(*refs):
        ins, outs = refs[:nm], refs[nm:2 * nm]
        send_sems, recv_sems, local_sems = refs[2 * nm:]
        x, y, c = _place()
        me, sibling = (x, y, c), (x, y, 1 - c)
        chips = _other_chips(x, y)

        def rows(m, px, py, pc):
            r = ins[m].shape[0]
            return outs[m].at[pl.ds((4 * px + 2 * py + pc) * r, r), :]

        def copy(m, k, block, to, src=None):
            return pltpu.make_async_remote_copy(
                src_ref=rows(m, *block) if src is None else src, dst_ref=rows(m, *block),
                send_sem=send_sems.at[k * nm + m], recv_sem=recv_sems.at[k * nm + m],
                device_id=to, device_id_type=MESH)

        mine = [pltpu.make_async_copy(ins[m], rows(m, *me), local_sems.at[m]) for m in range(nm)]
        for cp in mine:
            cp.start()
        first = [copy(m, 0, me, sibling, src=ins[m]) for m in range(nm)]
        first += [copy(m, 1 + j, me, (*chip, c), src=ins[m]) for j, chip in enumerate(chips) for m in range(nm)]
        for cp in first:
            cp.start()
        passed = []
        for j, chip in enumerate(chips):
            for m in range(nm):
                copy(m, 1 + j, (*chip, c), me).wait_recv()
                fwd = copy(m, 4 + j, (*chip, c), sibling)
                fwd.start()
                passed.append(fwd)
        for m in range(nm):
            copy(m, 0, sibling, me).wait_recv()
        for j, chip in enumerate(chips):
            for m in range(nm):
                copy(m, 4 + j, (*chip, 1 - c), me).wait_recv()
        for cp in first + passed:
            cp.wait_send()
        for cp in mine:
            cp.wait()

    return pl.pallas_call(
        body, name=name,
        in_specs=[HBM] * nm, out_specs=[HBM] * nm,
        out_shape=[jax.ShapeDtypeStruct((N_DEV * b.shape[0], b.shape[1]), b.dtype) for b in blocks],
        scratch_shapes=[pltpu.SemaphoreType.DMA((7 * nm,)), pltpu.SemaphoreType.DMA((7 * nm,)), pltpu.SemaphoreType.DMA((nm,))],
        compiler_params=pltpu.CompilerParams(has_side_effects=True),
    )(*blocks)


def _exchange_with_sibling(grads):
    nm = len(grads)

    def body(*refs):
        ins, outs = refs[:nm], refs[nm:2 * nm]
        send_sems, recv_sems = refs[2 * nm:]
        x, y, c = _place()
        sibling = (x, y, 1 - c)
        copies = []
        for m in range(nm):
            r = ins[m].shape[0] // N_DEV
            for q in range(N_CHIPS):
                copies.append(pltpu.make_async_remote_copy(
                    src_ref=ins[m].at[pl.ds((2 * q + 1 - c) * r, r), :], dst_ref=outs[m].at[pl.ds(q * r, r), :],
                    send_sem=send_sems.at[m * N_CHIPS + q], recv_sem=recv_sems.at[m * N_CHIPS + q],
                    device_id=sibling, device_id_type=MESH))
        for cp in copies:
            cp.start()
        for cp in copies:
            cp.wait_recv()
        for cp in copies:
            cp.wait_send()

    return pl.pallas_call(
        body, name="grad_exchange_sibling",
        in_specs=[HBM] * nm, out_specs=[HBM] * nm,
        out_shape=[jax.ShapeDtypeStruct((g.shape[0] // 2, g.shape[1]), F32) for g in grads],
        scratch_shapes=[pltpu.SemaphoreType.DMA((nm * N_CHIPS,)), pltpu.SemaphoreType.DMA((nm * N_CHIPS,))],
        compiler_params=pltpu.CompilerParams(has_side_effects=True),
    )(*grads)


def _chip_sum(grad, from_sibling, place):
    r = grad.shape[0] // N_DEV
    w = grad.shape[1]

    def body(place_ref, g_ref, s_ref, wire_ref, own_ref):
        total = g_ref[...] + s_ref[...]
        wire_ref[...] = total.astype(BF16)

        @pl.when(pl.program_id(0) == place_ref[1])
        def _():
            own_ref[...] = total

    grid_spec = pltpu.PrefetchScalarGridSpec(
        num_scalar_prefetch=1, grid=(N_CHIPS,),
        in_specs=[pl.BlockSpec((r, w), lambda q, p: (2 * q + p[0], 0)), pl.BlockSpec((r, w), lambda q, p: (q, 0))],
        out_specs=[pl.BlockSpec((r, w), lambda q, p: (q, 0)), pl.BlockSpec((r, w), lambda q, p: (0, 0))])
    return pl.pallas_call(
        body, name="grad_chip_sum", grid_spec=grid_spec,
        out_shape=[jax.ShapeDtypeStruct((N_CHIPS * r, w), BF16), jax.ShapeDtypeStruct((r, w), F32)],
        compiler_params=_params("arbitrary"),
    )(place, grad, from_sibling)


def _exchange_between_chips(wires):
    nm = len(wires)

    def body(*refs):
        ins, outs = refs[:nm], refs[nm:2 * nm]
        send_sems, recv_sems = refs[2 * nm:]
        x, y, c = _place()
        copies = []
        for m in range(nm):
            r = ins[m].shape[0] // N_CHIPS
            for j, (px, py) in enumerate(_other_chips(x, y)):
                copies.append(pltpu.make_async_remote_copy(
                    src_ref=ins[m].at[pl.ds((2 * px + py) * r, r), :], dst_ref=outs[m].at[pl.ds(j * r, r), :],
                    send_sem=send_sems.at[m * 3 + j], recv_sem=recv_sems.at[m * 3 + j],
                    device_id=(px, py, c), device_id_type=MESH))
        for cp in copies:
            cp.start()
        for cp in copies:
            cp.wait_recv()
        for cp in copies:
            cp.wait_send()

    return pl.pallas_call(
        body, name="grad_exchange_chips",
        in_specs=[HBM] * nm, out_specs=[HBM] * nm,
        out_shape=[jax.ShapeDtypeStruct((3 * (w.shape[0] // N_CHIPS), w.shape[1]), BF16) for w in wires],
        scratch_shapes=[pltpu.SemaphoreType.DMA((nm * 3,)), pltpu.SemaphoreType.DMA((nm * 3,))],
        compiler_params=pltpu.CompilerParams(has_side_effects=True),
    )(*wires)


def _final_sum(own, received):
    r, w = own.shape

    def body(own_ref, rec_ref, o_ref):
        o_ref[...] = ((own_ref[...] + rec_ref[0:r, :].astype(F32)) + rec_ref[r:2 * r, :].astype(F32)) + rec_ref[2 * r:, :].astype(F32)

    return pl.pallas_call(
        body, name="grad_final_sum", in_specs=[VMEM, VMEM], out_specs=VMEM,
        out_shape=jax.ShapeDtypeStruct((r, w), F32), compiler_params=_params(),
    )(own, received)


def _adamw_math(w, g, m, v):
    m = ADAM_B1 * m + (1.0 - ADAM_B1) * g
    v = ADAM_B2 * v + (1.0 - ADAM_B2) * jnp.square(g)
    m_hat = m / (1.0 - ADAM_B1 ** ADAM_STEP)
    v_hat = v / (1.0 - ADAM_B2 ** ADAM_STEP)
    delta = -ADAM_LR * (m_hat / (jnp.sqrt(v_hat) + ADAM_EPS) + ADAM_WD * w)
    return delta, m, v


def _adamw(w, g, m, v):
    def body(w_ref, g_ref, m_ref, v_ref, d_ref, nm_ref, nv_ref):
        d_ref[...], nm_ref[...], nv_ref[...] = _adamw_math(w_ref[...], g_ref[...], m_ref[...], v_ref[...])

    shape = jax.ShapeDtypeStruct(w.shape, F32)
    return pl.pallas_call(
        body, name="adamw", in_specs=[VMEM] * 4, out_specs=[VMEM] * 3, out_shape=[shape] * 3,
        compiler_params=_params(),
    )(w, g, m, v)


SMALL = (("g_mix", 1024), ("b_in", 1280), ("sinks", 8), ("w_pool", 65536), ("b_pool", 512), ("pool_scale", 512),
         ("b_out", 1024), ("g_ffn", 1024), ("g_final", 1024), ("loss", 1024))


def _small_rows(size):
    return -(-size // (8 * LANES)) * 8


SMALL_ROWS = sum(_small_rows(size) for _, size in SMALL)


def _pack_small(values):
    parts = []
    for name, size in SMALL:
        flat = values[name].reshape(-1).astype(F32)
        parts.append(jnp.pad(flat, (0, _small_rows(size) * LANES - size)).reshape(-1, LANES))
    return jnp.concatenate(parts, axis=0)


def _unpack_small(packed, shapes):
    out, row = {}, 0
    for name, size in SMALL:
        rows = _small_rows(size)
        if name in shapes:
            out[name] = packed[row:row + rows].reshape(-1)[:size].reshape(shapes[name])
        row += rows
    return out


def _small_allreduce_adamw(part, w, m, v):
    rows_n = SMALL_ROWS

    def body(p_ref, w_ref, m_ref, v_ref, g_ref, d_ref, nm_ref, nv_ref, all_ref, send_sems, recv_sems, local_sem):
        x, y, c = _place()
        me, sibling = (x, y, c), (x, y, 1 - c)
        chips = _other_chips(x, y)

        def rows(px, py, pc):
            return all_ref.at[pl.ds((4 * px + 2 * py + pc) * rows_n, rows_n), :]

        def copy(k, block, to, src=None):
            return pltpu.make_async_remote_copy(
                src_ref=rows(*block) if src is None else src, dst_ref=rows(*block),
                send_sem=send_sems.at[k], recv_sem=recv_sems.at[k], device_id=to, device_id_type=MESH)

        mine = pltpu.make_async_copy(p_ref, rows(*me), local_sem)
        mine.start()
        first = [copy(0, me, sibling, src=p_ref)]
        first += [copy(1 + j, me, (*chip, c), src=p_ref) for j, chip in enumerate(chips)]
        for cp in first:
            cp.start()
        passed = [copy(4 + j, (*chip, c), sibling) for j, chip in enumerate(chips)]
        for j, chip in enumerate(chips):
            copy(1 + j, (*chip, c), me).wait_recv()
            passed[j].start()
        copy(0, sibling, me).wait_recv()
        for j, chip in enumerate(chips):
            copy(4 + j, (*chip, 1 - c), me).wait_recv()
        for cp in first + passed:
            cp.wait_send()
        mine.wait()
        total = all_ref[0:rows_n, :]
        for dev in range(1, N_DEV):
            total = total + all_ref[dev * rows_n:(dev + 1) * rows_n, :]
        g_ref[...] = total
        d_ref[...], nm_ref[...], nv_ref[...] = _adamw_math(w_ref[...], total, m_ref[...], v_ref[...])

    shape = jax.ShapeDtypeStruct((rows_n, LANES), F32)
    return pl.pallas_call(
        body, name="small_allreduce_adamw", in_specs=[VMEM] * 4, out_specs=[VMEM] * 4, out_shape=[shape] * 4,
        scratch_shapes=[pltpu.VMEM((N_DEV * rows_n, LANES), F32), pltpu.SemaphoreType.DMA((7,)), pltpu.SemaphoreType.DMA((7,)),
                        pltpu.SemaphoreType.DMA],
        compiler_params=pltpu.CompilerParams(has_side_effects=True, vmem_limit_bytes=VMEM_LIMIT_BYTES),
    )(part, w, m, v)


def kernel(x, g_mix, w_in, b_in, sinks, w_pool, b_pool, pool_scale, w_out, b_out, g_ffn, w_gate, w_up, w_down, g_final, loss_target, m_g_mix, m_w_in, m_b_in, m_sinks, m_w_pool, m_b_pool, m_pool_scale, m_w_out, m_b_out, m_g_ffn, m_w_gate, m_w_up, m_w_down, m_g_final, v_g_mix, v_w_in, v_b_in, v_sinks, v_w_pool, v_b_pool, v_pool_scale, v_w_out, v_b_out, v_g_ffn, v_w_gate, v_w_up, v_w_down, v_g_final):
    weights = dict(g_mix=g_mix, w_in=w_in, b_in=b_in, sinks=sinks, w_pool=w_pool, b_pool=b_pool, pool_scale=pool_scale,
                   w_out=w_out, b_out=b_out, g_ffn=g_ffn, w_gate=w_gate, w_up=w_up, w_down=w_down, g_final=g_final)
    mom1 = dict(g_mix=m_g_mix, w_in=m_w_in, b_in=m_b_in, sinks=m_sinks, w_pool=m_w_pool, b_pool=m_b_pool,
                pool_scale=m_pool_scale, w_out=m_w_out, b_out=m_b_out, g_ffn=m_g_ffn, w_gate=m_w_gate, w_up=m_w_up,
                w_down=m_w_down, g_final=m_g_final)
    mom2 = dict(g_mix=v_g_mix, w_in=v_w_in, b_in=v_b_in, sinks=v_sinks, w_pool=v_w_pool, b_pool=v_b_pool,
                pool_scale=v_pool_scale, w_out=v_w_out, b_out=v_b_out, g_ffn=v_g_ffn, w_gate=v_w_gate, w_up=v_w_up,
                w_down=v_w_down, g_final=v_g_final)
    order = ("g_mix", "w_in", "b_in", "sinks", "w_pool", "b_pool", "pool_scale", "w_out", "b_out", "g_ffn",
             "w_gate", "w_up", "w_down", "g_final")
    big = ("w_in", "w_out", "w_gate", "w_up", "w_down")
    transposed = ("w_in", "w_gate", "w_up")

    def row_shard(name, a):
        return a[0].T if name in transposed else a[0]

    gathered = _all_gather_rows([row_shard(n, weights[n]).astype(BF16) for n in big], "weights_all_gather")

    dx, grads, small = _local_step(x[0], loss_target[0], g_mix, b_in, sinks, w_pool[0], b_pool[0], pool_scale[0],
                                   b_out, g_ffn, g_final.reshape(1, D_MODEL), *gathered)

    px, py, pc = _place()
    place = jnp.stack([pc, 2 * px + py]).astype(jnp.int32)
    from_sibling = _exchange_with_sibling(list(grads))
    sums = [_chip_sum(g, s, place) for g, s in zip(grads, from_sibling)]
    received = _exchange_between_chips([wire for wire, _ in sums])
    grad, delta, new_m, new_v = {}, {}, {}, {}
    for n, (_, own), rec in zip(big, sums, received):
        g = _final_sum(own, rec)
        if n in transposed:
            g = g.T
        d, nm_, nv_ = _adamw(weights[n][0], g, mom1[n][0], mom2[n][0])
        grad[n], delta[n], new_m[n], new_v[n] = g[None], d[None], nm_[None], nv_[None]

    shapes = {n: weights[n].shape for n in order if n not in big}
    zero_loss = jnp.zeros((1, D_MODEL), F32)
    packed = _small_allreduce_adamw(
        _pack_small(small), _pack_small({**weights, "loss": zero_loss}),
        _pack_small({**mom1, "loss": zero_loss}), _pack_small({**mom2, "loss": zero_loss}))
    for store, pk in zip((grad, delta, new_m, new_v), packed):
        store.update(_unpack_small(pk, shapes))
    loss_rows = _unpack_small(packed[0], {"loss": (D_MODEL,)})["loss"]
    loss = (0.5 / D_MODEL) * jnp.sum(loss_rows)

    return (loss, dx[None], *[grad[n] for n in order], *[delta[n] for n in order],
            *[new_m[n] for n in order], *[new_v[n] for n in order])
```

```python
from typing import Any, Callable, NamedTuple, Sequence

import jax
import jax.numpy as jnp
from jax import lax
from jax.experimental import pallas as pl
from jax.experimental.pallas import tpu as pltpu

D_MODEL = 1024
ATTN_WIDTH = 512
KV_WIDTH = 128
POOL_WIDTH = 512
HEAD_DIM = 64
N_Q_HEADS = 8
N_KV_HEADS = 2
GQA_GROUP = 4
BLOCK = 128
POOL_SIZES = (2, 4, 8, 16)
POOL_GROUP_WIDTH = 128
POOL_HALO = 16
IN_WIDTH = 1280
D_FF = 2816
RMS_EPS = 1e-5
ROPE_THETA = 10000.0
Q_SCALE = HEAD_DIM ** -0.5

ADAM_LR = 0.001
ADAM_B1 = 0.9
ADAM_B2 = 0.999
ADAM_EPS = 1e-08
ADAM_WD = 0.01
ADAM_STEP = 10

N_DEV = 8
N_CHIPS = 4
LANES = 128
VMEM_LIMIT_BYTES = 56 * 1024 * 1024

F32 = jnp.float32
BF16 = jnp.bfloat16
MESH = pl.DeviceIdType.MESH
HBM = pl.BlockSpec(memory_space=pltpu.HBM)
VMEM = pl.BlockSpec(memory_space=pltpu.VMEM)


def _params(*semantics):
    return pltpu.CompilerParams(dimension_semantics=semantics or None, vmem_limit_bytes=VMEM_LIMIT_BYTES)


def _nn(a, b):
    return jnp.dot(a, b, preferred_element_type=F32)


def _nt(a, b):
    return lax.dot_general(a, b, (((1,), (1,)), ((), ())), preferred_element_type=F32)


def _tn(a, b):
    return lax.dot_general(a, b, (((0,), (0,)), ((), ())), preferred_element_type=F32)


def _full(shape):
    return pl.BlockSpec(shape, lambda *_: (0,) * len(shape))


def _rows(tm, width):
    return pl.BlockSpec((tm, width), lambda i, *_: (i, 0))


class _Rider(NamedTuple):
    arrays: Sequence[Any]
    out_shape: Sequence[Any]
    sems: Sequence[Any]
    start: Callable[..., None]
    finish: Callable[..., None]


def _gridded(body, rider, *, name, grid, in_specs, out_specs, out_shape, scratch_shapes, args):
    params = _params("arbitrary")
    if rider is None:
        return pl.pallas_call(body, name=name, grid=grid, in_specs=in_specs, out_specs=out_specs, out_shape=out_shape,
                              scratch_shapes=scratch_shapes, compiler_params=params)(*args)
    bounds, total = [], 0
    for n in (len(in_specs), len(rider.arrays), len(out_specs), len(rider.out_shape), len(scratch_shapes), len(rider.sems)):
        bounds.append((total, total + n))
        total += n
    last = grid[0] - 1

    def riding(*refs):
        ins, r_ins, outs, r_outs, scratch, r_sems = (refs[lo:hi] for lo, hi in bounds)

        @pl.when(pl.program_id(0) == 0)
        def _():
            rider.start(r_ins, r_outs, r_sems)

        body(*ins, *outs, *scratch)

        @pl.when(pl.program_id(0) == last)
        def _():
            rider.finish(r_ins, r_outs, r_sems)

    return pl.pallas_call(
        riding, name=name, grid=grid, in_specs=list(in_specs) + [HBM] * len(rider.arrays),
        out_specs=list(out_specs) + [HBM] * len(rider.out_shape), out_shape=list(out_shape) + list(rider.out_shape),
        scratch_shapes=list(scratch_shapes) + list(rider.sems), compiler_params=params)(*args, *rider.arrays)


def _alone(rider, name):
    n_in, n_out = len(rider.arrays), len(rider.out_shape)

    def body(*refs):
        parts = refs[:n_in], refs[n_in:n_in + n_out], refs[n_in + n_out:]
        rider.start(*parts)
        rider.finish(*parts)

    return pl.pallas_call(body, name=name, in_specs=[HBM] * n_in, out_specs=[HBM] * n_out, out_shape=list(rider.out_shape),
                          scratch_shapes=list(rider.sems))(*rider.arrays)


def _rot_half(t):
    n = t.shape[1]
    lane = lax.broadcasted_iota(jnp.int32, t.shape, 1)
    return jnp.where((lane % HEAD_DIM) < HEAD_DIM // 2, pltpu.roll(t, n - HEAD_DIM // 2, 1), pltpu.roll(t, HEAD_DIM // 2, 1))


def _rope(t, cos, sin):
    reps = t.shape[1] // LANES
    if reps > 1:
        cos, sin = jnp.tile(cos, (1, reps)), jnp.tile(sin, (1, reps))
    return t * cos + _rot_half(t) * sin


def _rope_bwd(d, cos, sin):
    reps = d.shape[1] // LANES
    if reps > 1:
        cos, sin = jnp.tile(cos, (1, reps)), jnp.tile(sin, (1, reps))
    return d * cos + _rot_half(d * sin)


KV_SPREAD = 4 * LANES


def _spread_kv(t):
    low = lax.broadcasted_iota(jnp.int32, t.shape, 1) < HEAD_DIM
    swapped = pltpu.roll(t, HEAD_DIM, 1)
    zero = jnp.zeros_like(t)
    return jnp.concatenate([jnp.where(low, t, zero), jnp.where(low, zero, swapped),
                            jnp.where(low, swapped, zero), jnp.where(low, zero, t)], axis=1)


def _rms(x):
    r = lax.rsqrt(jnp.mean(x * x, axis=-1, keepdims=True) + RMS_EPS)
    return x * r, r


def _rms_bwd(dh, n, r, g):
    dn = dh * g
    dx = r * (dn - n * jnp.mean(dn * n, axis=-1, keepdims=True))
    return dx, jnp.sum(dh * n, axis=0, keepdims=True)


def _token_tile(s):
    return min(512, s)


def _fwd_inproj(x, g_mix, win_t, b_in, cos, sin, w_pool, b_pool, pool_scale, rider=None):
    s = x.shape[0]
    tm = _token_tile(s)

    def body(x_ref, g_ref, w_ref, b_ref, cos_ref, sin_ref, wp_ref, bp_ref, ps_ref,
             q_ref, k_ref, v_ref, mix_ref, pool_ref, tail_ref):
        i = pl.program_id(0)

        @pl.when(i == 0)
        def _():
            tail_ref[...] = jnp.zeros_like(tail_ref)

        n, _ = _rms(x_ref[...])
        h = (n * g_ref[...]).astype(BF16)
        z = _nt(h, w_ref[...]) + b_ref[...]
        cos_t, sin_t = cos_ref[...], sin_ref[...]
        q_ref[...] = (_rope(z[:, :ATTN_WIDTH], cos_t, sin_t) * Q_SCALE).astype(BF16)
        k_ref[...] = _spread_kv(_rope(z[:, ATTN_WIDTH:ATTN_WIDTH + KV_WIDTH], cos_t, sin_t)).astype(BF16)
        v_ref[...] = _spread_kv(z[:, ATTN_WIDTH + KV_WIDTH:ATTN_WIDTH + 2 * KV_WIDTH]).astype(BF16)
        u = z[:, ATTN_WIDTH + 2 * KV_WIDTH:]
        u_ext = jnp.concatenate([tail_ref[...], u], axis=0)
        tail_ref[...] = u[tm - POOL_HALO:, :]
        pos = lax.broadcasted_iota(jnp.int32, (tm, POOL_GROUP_WIDTH), 0) + i * tm
        for g, size in enumerate(POOL_SIZES):
            cols = slice(g * POOL_GROUP_WIDTH, (g + 1) * POOL_GROUP_WIDTH)
            a = u_ext[:, cols]
            shift = 1
            while shift < size:
                a = a + pltpu.roll(a, shift, 0)
                shift *= 2
            count = jnp.minimum(pos + 1, size).astype(F32)
            mixed = (a[POOL_HALO:, :] / count - u[:, cols]).astype(BF16)
            pre = _nn(mixed, wp_ref[g]) + bp_ref[:, cols]
            mix_ref[:, cols] = mixed
            pool_ref[:, cols] = (pre * ps_ref[:, cols]).astype(BF16)

    bf = lambda w: jax.ShapeDtypeStruct((s, w), BF16)
    return _gridded(
        body, rider, name="fwd_inproj", grid=(s // tm,),
        in_specs=[_rows(tm, D_MODEL), _full((1, D_MODEL)), _full((IN_WIDTH, D_MODEL)), _full((1, IN_WIDTH)),
                  _rows(tm, LANES), _rows(tm, LANES), _full((4, POOL_GROUP_WIDTH, POOL_GROUP_WIDTH)),
                  _full((1, POOL_WIDTH)), _full((1, POOL_WIDTH))],
        out_specs=[_rows(tm, ATTN_WIDTH), _rows(tm, KV_SPREAD), _rows(tm, KV_SPREAD), _rows(tm, POOL_WIDTH), _rows(tm, POOL_WIDTH)],
        out_shape=[bf(ATTN_WIDTH), bf(KV_SPREAD), bf(KV_SPREAD), bf(POOL_WIDTH), bf(POOL_WIDTH)],
        scratch_shapes=[pltpu.VMEM((POOL_HALO, POOL_WIDTH), F32)],
        args=(x, g_mix, win_t, b_in, cos, sin, w_pool, b_pool, pool_scale))


ATTN_TILE = 512
PAIR = 2 * LANES


def _band_masks(tile):
    r = lax.broadcasted_iota(jnp.int32, (2 * BLOCK, 4 * BLOCK), 0) % BLOCK
    j = lax.broadcasted_iota(jnp.int32, (2 * BLOCK, 4 * BLOCK), 1) % (2 * BLOCK)
    band = (j > r) & (j <= r + BLOCK)
    return band & ((tile > 0) | (j >= BLOCK)), band


def _band(cur_ref, prev_ref, b, kv):
    halves = []
    for half in range(2):
        cols = slice(kv * PAIR + half * LANES, kv * PAIR + (half + 1) * LANES)
        if b == 0:
            halves.append(jnp.concatenate([prev_ref[:, cols], cur_ref[0:BLOCK, cols]], axis=0))
        else:
            halves.append(cur_ref[(b - 1) * BLOCK:(b + 1) * BLOCK, cols])
    return jnp.concatenate(halves, axis=0)


def _stack_pair(ref, rows, kv):
    return jnp.concatenate([ref[rows, kv * PAIR:kv * PAIR + LANES], ref[rows, kv * PAIR + LANES:(kv + 1) * PAIR]], axis=0)


def _pair_heads(kv, half):
    return GQA_GROUP * kv + half, GQA_GROUP * kv + 2 + half


def _per_row(ref, rows, top, bottom):
    return jnp.concatenate([ref[rows, top:top + 1], ref[rows, bottom:bottom + 1]], axis=0)


def _sink_rows(sink_ref, top, bottom):
    upper = lax.broadcasted_iota(jnp.int32, (2 * BLOCK, 1), 0) < BLOCK
    return jnp.where(upper, sink_ref[:, top:top + 1], sink_ref[:, bottom:bottom + 1])


def _attn_fwd(q, kz, vz, sinks, rider=None):
    s = q.shape[0]
    tq = min(ATTN_TILE, s)

    def body(q_ref, k_ref, kp_ref, v_ref, vp_ref, sink_ref, o_ref, lse_ref):
        first, band = _band_masks(pl.program_id(0))
        for b in range(tq // BLOCK):
            rows = slice(b * BLOCK, (b + 1) * BLOCK)
            mask = first if b == 0 else band
            for kv in range(N_KV_HEADS):
                sc = jnp.where(mask, _nt(_stack_pair(q_ref, rows, kv), _band(k_ref, kp_ref, b, kv)), -jnp.inf)
                probs = []
                for half in range(2):
                    top, bottom = _pair_heads(kv, half)
                    sink = _sink_rows(sink_ref, top, bottom)
                    sh = sc[:, half * 2 * BLOCK:(half + 1) * 2 * BLOCK]
                    m = jnp.maximum(jnp.max(sh, axis=-1, keepdims=True), sink)
                    p = jnp.exp(sh - m)
                    denom = jnp.sum(p, axis=-1, keepdims=True) + jnp.exp(sink - m)
                    probs.append((p / denom).astype(BF16))
                    lse = m + jnp.log(denom)
                    lse_ref[rows, top:top + 1] = lse[:BLOCK]
                    lse_ref[rows, bottom:bottom + 1] = lse[BLOCK:]
                o = _nn(jnp.concatenate(probs, axis=1), _band(v_ref, vp_ref, b, kv)).astype(BF16)
                o_ref[rows, kv * PAIR:kv * PAIR + LANES] = o[:BLOCK]
                o_ref[rows, kv * PAIR + LANES:(kv + 1) * PAIR] = o[BLOCK:]

    per = tq // BLOCK
    cur = lambda w: pl.BlockSpec((tq, w), lambda i: (i, 0))
    prev = pl.BlockSpec((BLOCK, KV_SPREAD), lambda i: (jnp.maximum(per * i - 1, 0), 0))
    return _gridded(
        body, rider, name="attn_fwd", grid=(s // tq,),
        in_specs=[cur(ATTN_WIDTH), cur(KV_SPREAD), prev, cur(KV_SPREAD), prev, _full((1, N_Q_HEADS))],
        out_specs=[cur(ATTN_WIDTH), cur(N_Q_HEADS)],
        out_shape=[jax.ShapeDtypeStruct((s, ATTN_WIDTH), BF16), jax.ShapeDtypeStruct((s, N_Q_HEADS), F32)],
        scratch_shapes=[], args=(q, kz, kz, vz, vz, sinks))


def _fwd_outproj(attn, pool, w_out, b_out, x, g_ffn):
    s = x.shape[0]
    tm = _token_tile(s)

    def body(a_ref, p_ref, w_ref, b_ref, x_ref, g_ref, x2_ref, h2_ref):
        x2 = x_ref[...] + _nn(a_ref[...], w_ref[:ATTN_WIDTH, :]) + _nn(p_ref[...], w_ref[ATTN_WIDTH:, :]) + b_ref[...]
        x2_ref[...] = x2
        n, _ = _rms(x2)
        h2_ref[...] = (n * g_ref[...]).astype(BF16)

    return pl.pallas_call(
        body, name="fwd_outproj", grid=(s // tm,),
        in_specs=[_rows(tm, ATTN_WIDTH), _rows(tm, POOL_WIDTH), _full((D_MODEL, D_MODEL)), _full((1, D_MODEL)),
                  _rows(tm, D_MODEL), _full((1, D_MODEL))],
        out_specs=[_rows(tm, D_MODEL), _rows(tm, D_MODEL)],
        out_shape=[jax.ShapeDtypeStruct((s, D_MODEL), F32), jax.ShapeDtypeStruct((s, D_MODEL), BF16)],
        compiler_params=_params("parallel"),
    )(attn, pool, w_out, b_out, x, g_ffn)


FF_CHUNK = 256


def _resident(shape):
    return pl.BlockSpec(shape, lambda *_: (0,) * len(shape), pipeline_mode=pl.Buffered(1))


def _fwd_ffn_act(h2, wg_t, wu_t):
    s = h2.shape[0]
    tm = _token_tile(s)

    def body(h_ref, wg_ref, wu_ref, gate_ref, up_ref, act_ref):
        h = h_ref[...]
        for c0 in range(0, D_FF, FF_CHUNK):
            cols = slice(c0, c0 + FF_CHUNK)
            gate = _nt(h, wg_ref[cols, :])
            up = _nt(h, wu_ref[cols, :])
            gate_ref[:, cols] = gate.astype(BF16)
            up_ref[:, cols] = up.astype(BF16)
            act_ref[:, cols] = (gate * jax.nn.sigmoid(gate) * up).astype(BF16)

    act_shape = jax.ShapeDtypeStruct((s, D_FF), BF16)
    return pl.pallas_call(
        body, name="fwd_ffn_act", grid=(s // tm,),
        in_specs=[_rows(tm, D_MODEL), _resident((D_FF, D_MODEL)), _resident((D_FF, D_MODEL))],
        out_specs=[_rows(tm, D_FF)] * 3, out_shape=[act_shape] * 3,
        compiler_params=_params("parallel"),
    )(h2, wg_t, wu_t)


def _fwd_down_loss(act, x2, wd, g_final, target):
    s = x2.shape[0]
    tm = _token_tile(s)

    def body(a_ref, x2_ref, wd_ref, g_ref, t_ref, dx3_ref, dx3b_ref, sq_ref, dg_ref):
        @pl.when(pl.program_id(0) == 0)
        def _():
            sq_ref[...] = jnp.zeros_like(sq_ref)
            dg_ref[...] = jnp.zeros_like(dg_ref)

        x3 = x2_ref[...] + _nn(a_ref[...], wd_ref[...])
        n, r = _rms(x3)
        g = g_ref[...]
        diff = n * g - t_ref[...]
        sq_ref[...] += jnp.sum(diff * diff, axis=0, keepdims=True)
        dx3, dg = _rms_bwd(diff * (1.0 / D_MODEL), n, r, g)
        dg_ref[...] += dg
        dx3_ref[...] = dx3
        dx3b_ref[...] = dx3.astype(BF16)

    return pl.pallas_call(
        body, name="fwd_down_loss", grid=(s // tm,),
        in_specs=[_rows(tm, D_FF), _rows(tm, D_MODEL), _resident((D_FF, D_MODEL)), _full((1, D_MODEL)), _rows(tm, D_MODEL)],
        out_specs=[_rows(tm, D_MODEL), _rows(tm, D_MODEL), _full((1, D_MODEL)), _full((1, D_MODEL))],
        out_shape=[jax.ShapeDtypeStruct((s, D_MODEL), F32), jax.ShapeDtypeStruct((s, D_MODEL), BF16),
                   jax.ShapeDtypeStruct((1, D_MODEL), F32), jax.ShapeDtypeStruct((1, D_MODEL), F32)],
        compiler_params=_params("arbitrary"),
    )(act, x2, wd, g_final, target)


def _bwd_ffn_act(dx3b, gate, up, wd):
    s = dx3b.shape[0]
    tm = _token_tile(s)

    def body(dx3_ref, gate_ref, up_ref, wd_ref, dgate_ref, dup_ref):
        dx3 = dx3_ref[...]
        for c0 in range(0, D_FF, FF_CHUNK):
            cols = slice(c0, c0 + FF_CHUNK)
            dact = _nt(dx3, wd_ref[cols, :])
            gate = gate_ref[:, cols].astype(F32)
            up = up_ref[:, cols].astype(F32)
            sig = jax.nn.sigmoid(gate)
            silu = gate * sig
            dup_ref[:, cols] = (dact * silu).astype(BF16)
            dgate_ref[:, cols] = (dact * up * (sig + silu * (1.0 - sig))).astype(BF16)

    act_shape = jax.ShapeDtypeStruct((s, D_FF), BF16)
    return pl.pallas_call(
        body, name="bwd_ffn_act", grid=(s // tm,),
        in_specs=[_rows(tm, D_MODEL), _rows(tm, D_FF), _rows(tm, D_FF), _resident((D_FF, D_MODEL))],
        out_specs=[_rows(tm, D_FF)] * 2, out_shape=[act_shape] * 2,
        compiler_params=_params("parallel"),
    )(dx3b, gate, up, wd)


def _bwd_ffn_in(dgate, dup, wg_t, wu_t, x2, dx3, g_ffn):
    s = x2.shape[0]
    tm = _token_tile(s)

    def body(dgate_ref, dup_ref, wg_ref, wu_ref, x2_ref, dx3_ref, g_ref, dx2_ref, dx2b_ref, dg_ref, db_ref):
        @pl.when(pl.program_id(0) == 0)
        def _():
            dg_ref[...] = jnp.zeros_like(dg_ref)
            db_ref[...] = jnp.zeros_like(db_ref)

        dh2 = _nn(dgate_ref[...], wg_ref[...]) + _nn(dup_ref[...], wu_ref[...])
        n, r = _rms(x2_ref[...])
        dx, dg = _rms_bwd(dh2, n, r, g_ref[...])
        dx2 = dx3_ref[...] + dx
        dg_ref[...] += dg
        db_ref[...] += jnp.sum(dx2, axis=0, keepdims=True)
        dx2_ref[...] = dx2
        dx2b_ref[...] = dx2.astype(BF16)

    return pl.pallas_call(
        body, name="bwd_ffn_in", grid=(s // tm,),
        in_specs=[_rows(tm, D_FF), _rows(tm, D_FF), _resident((D_FF, D_MODEL)), _resident((D_FF, D_MODEL)),
                  _rows(tm, D_MODEL), _rows(tm, D_MODEL), _full((1, D_MODEL))],
        out_specs=[_rows(tm, D_MODEL), _rows(tm, D_MODEL), _full((1, D_MODEL)), _full((1, D_MODEL))],
        out_shape=[jax.ShapeDtypeStruct((s, D_MODEL), F32), jax.ShapeDtypeStruct((s, D_MODEL), BF16),
                   jax.ShapeDtypeStruct((1, D_MODEL), F32), jax.ShapeDtypeStruct((1, D_MODEL), F32)],
        compiler_params=_params("arbitrary"),
    )(dgate, dup, wg_t, wu_t, x2, dx3, g_ffn)


TN_ROW_CHUNK = 256


def _weight_grad(a, b, name):
    s, m = a.shape
    n_out = b.shape[1]
    tm = _token_tile(s)

    def body(a_ref, b_ref, o_ref):
        @pl.when(pl.program_id(0) == 0)
        def _():
            o_ref[...] = jnp.zeros_like(o_ref)

        bt = b_ref[...]
        for m0 in range(0, m, TN_ROW_CHUNK):
            o_ref[m0:m0 + TN_ROW_CHUNK, :] += _tn(a_ref[:, m0:m0 + TN_ROW_CHUNK], bt)

    return pl.pallas_call(
        body, name=name, grid=(s // tm,),
        in_specs=[_rows(tm, m), _rows(tm, n_out)],
        out_specs=_full((m, n_out)),
        out_shape=jax.ShapeDtypeStruct((m, n_out), F32),
        compiler_params=_params("arbitrary"),
    )(a, b)


def _bwd_outproj_pool(dx2b, attn, pool, mixed, w_out, w_pool, b_pool, pool_scale, rider=None):
    s = dx2b.shape[0]
    tm = _token_tile(s)
    nt = s // tm

    def body(dx_ref, a_ref, p_ref, mix_ref, w_ref, wp_ref, bp_ref, ps_ref,
             dattn_ref, du_ref, dwout_ref, dwp_ref, dbp_ref, dps_ref, head_ref):
        step = pl.program_id(0)
        tile = nt - 1 - step

        @pl.when(step == 0)
        def _():
            head_ref[...] = jnp.zeros_like(head_ref)
            dwout_ref[...] = jnp.zeros_like(dwout_ref)
            dwp_ref[...] = jnp.zeros_like(dwp_ref)
            dbp_ref[...] = jnp.zeros_like(dbp_ref)
            dps_ref[...] = jnp.zeros_like(dps_ref)

        dx = dx_ref[...]
        dwout_ref[:ATTN_WIDTH, :] += _tn(a_ref[...], dx)
        dwout_ref[ATTN_WIDTH:, :] += _tn(p_ref[...], dx)
        dcat = _nt(dx, w_ref[...])
        dattn_ref[...] = dcat[:, :ATTN_WIDTH].astype(BF16)
        dpool = dcat[:, ATTN_WIDTH:]
        pos = lax.broadcasted_iota(jnp.int32, (tm, POOL_GROUP_WIDTH), 0) + tile * tm
        head = head_ref[...]
        n_ext = tm + POOL_HALO
        for g, size in enumerate(POOL_SIZES):
            cols = slice(g * POOL_GROUP_WIDTH, (g + 1) * POOL_GROUP_WIDTH)
            mixed_g = mix_ref[:, cols]
            pre = _nn(mixed_g, wp_ref[g]) + bp_ref[:, cols]
            dy = dpool[:, cols]
            dps_ref[:, cols] += jnp.sum(dy * pre, axis=0, keepdims=True)
            dpre = dy * ps_ref[:, cols]
            dbp_ref[:, cols] += jnp.sum(dpre, axis=0, keepdims=True)
            dpre_b = dpre.astype(BF16)
            dwp_ref[g] += _tn(mixed_g, dpre_b)
            dmixed = _nt(dpre_b, wp_ref[g])
            w = dmixed / jnp.minimum(pos + 1, size).astype(F32)
            head_ref[:, cols] = w[:POOL_HALO, :]
            a = jnp.concatenate([w, head[:, cols]], axis=0)
            shift = 1
            while shift < size:
                a = a + pltpu.roll(a, n_ext - shift, 0)
                shift *= 2
            du_ref[:, cols] = (a[:tm, :] - dmixed).astype(BF16)

    rev = lambda w: pl.BlockSpec((tm, w), lambda i: (nt - 1 - i, 0))
    return _gridded(
        body, rider, name="bwd_outproj_pool", grid=(nt,),
        in_specs=[rev(D_MODEL), rev(ATTN_WIDTH), rev(POOL_WIDTH), rev(POOL_WIDTH), _full((D_MODEL, D_MODEL)),
                  _full((4, POOL_GROUP_WIDTH, POOL_GROUP_WIDTH)), _full((1, POOL_WIDTH)), _full((1, POOL_WIDTH))],
        out_specs=[rev(ATTN_WIDTH), rev(POOL_WIDTH), _full((D_MODEL, D_MODEL)),
                   _full((4, POOL_GROUP_WIDTH, POOL_GROUP_WIDTH)), _full((1, POOL_WIDTH)), _full((1, POOL_WIDTH))],
        out_shape=[jax.ShapeDtypeStruct((s, ATTN_WIDTH), BF16), jax.ShapeDtypeStruct((s, POOL_WIDTH), BF16),
                   jax.ShapeDtypeStruct((D_MODEL, D_MODEL), F32),
                   jax.ShapeDtypeStruct((4, POOL_GROUP_WIDTH, POOL_GROUP_WIDTH), F32),
                   jax.ShapeDtypeStruct((1, POOL_WIDTH), F32), jax.ShapeDtypeStruct((1, POOL_WIDTH), F32)],
        scratch_shapes=[pltpu.VMEM((POOL_HALO, POOL_WIDTH), F32)],
        args=(dx2b, attn, pool, mixed, w_out, w_pool, b_pool, pool_scale))


def _fold_spread(t):
    low = lax.broadcasted_iota(jnp.int32, (2 * BLOCK, LANES), 1) < HEAD_DIM
    kept = jnp.where(low, t[:2 * BLOCK, :], t[2 * BLOCK:, :])
    return kept + pltpu.roll(kept, HEAD_DIM, 1)


def _attn_bwd(q, kz, vz, dattn, lse, sinks, rider=None):
    s = q.shape[0]
    tq = min(ATTN_TILE, s)
    nt = s // tq
    per = tq // BLOCK

    def body(q_ref, k_ref, kp_ref, v_ref, vp_ref, do_ref, lse_ref, sink_ref,
             dq_ref, dk_ref, dv_ref, dsink_ref, dk_acc, dv_acc, dk_carry, dv_carry):
        step = pl.program_id(0)

        @pl.when(step == 0)
        def _():
            dk_carry[...] = jnp.zeros_like(dk_carry)
            dv_carry[...] = jnp.zeros_like(dv_carry)
            dsink_ref[...] = jnp.zeros_like(dsink_ref)

        dk_acc[0:tq, :] = jnp.zeros((tq, KV_WIDTH), F32)
        dv_acc[0:tq, :] = jnp.zeros((tq, KV_WIDTH), F32)
        dk_acc[tq:, :] = dk_carry[...]
        dv_acc[tq:, :] = dv_carry[...]
        first, band = _band_masks(nt - 1 - step)
        low = lax.broadcasted_iota(jnp.int32, (2 * BLOCK, LANES), 1) < HEAD_DIM
        for b in range(per):
            rows = slice(b * BLOCK, (b + 1) * BLOCK)
            mask = first if b == 0 else band
            dk_heads, dv_heads = [], []
            for kv in range(N_KV_HEADS):
                qab = _stack_pair(q_ref, rows, kv)
                doab = _stack_pair(do_ref, rows, kv)
                kzb = _band(k_ref, kp_ref, b, kv)
                sc = _nt(qab, kzb)
                dp = _nt(doab, _band(v_ref, vp_ref, b, kv))
                probs, dscores = [], []
                for half in range(2):
                    top, bottom = _pair_heads(kv, half)
                    cols = slice(half * 2 * BLOCK, (half + 1) * 2 * BLOCK)
                    lse_h = _per_row(lse_ref, rows, top, bottom)
                    p = jnp.where(mask[:, cols], jnp.exp(sc[:, cols] - lse_h), 0.0)
                    dph = dp[:, cols]
                    delta = jnp.sum(p * dph, axis=-1, keepdims=True)
                    probs.append(p.astype(BF16))
                    dscores.append((p * (dph - delta)).astype(BF16))
                    leak = jnp.exp(_sink_rows(sink_ref, top, bottom) - lse_h) * delta
                    dsink_ref[:, top:top + 1] -= jnp.sum(leak[:BLOCK], axis=0, keepdims=True)
                    dsink_ref[:, bottom:bottom + 1] -= jnp.sum(leak[BLOCK:], axis=0, keepdims=True)
                ds = jnp.concatenate(dscores, axis=1)
                dqab = _nn(ds, kzb) * Q_SCALE
                dq_ref[rows, kv * PAIR:kv * PAIR + LANES] = dqab[:BLOCK]
                dq_ref[rows, kv * PAIR + LANES:(kv + 1) * PAIR] = dqab[BLOCK:]
                dk_heads.append(_fold_spread(_tn(ds, qab)))
                dv_heads.append(_fold_spread(_tn(jnp.concatenate(probs, axis=1), doab)))
            band_rows = slice(b * BLOCK, (b + 2) * BLOCK)
            dk_acc[band_rows, :] += jnp.where(low, dk_heads[0], dk_heads[1])
            dv_acc[band_rows, :] += jnp.where(low, dv_heads[0], dv_heads[1])
        dk_ref[...] = dk_acc[BLOCK:, :]
        dv_ref[...] = dv_acc[BLOCK:, :]
        dk_carry[...] = dk_acc[0:BLOCK, :]
        dv_carry[...] = dv_acc[0:BLOCK, :]

    cur = lambda w: pl.BlockSpec((tq, w), lambda i: (nt - 1 - i, 0))
    prev = pl.BlockSpec((BLOCK, KV_SPREAD), lambda i: (jnp.maximum(per * (nt - 1 - i) - 1, 0), 0))
    acc = pltpu.VMEM((tq + BLOCK, KV_WIDTH), F32)
    carry = pltpu.VMEM((BLOCK, KV_WIDTH), F32)
    return _gridded(
        body, rider, name="attn_bwd", grid=(nt,),
        in_specs=[cur(ATTN_WIDTH), cur(KV_SPREAD), prev, cur(KV_SPREAD), prev, cur(ATTN_WIDTH),
                  cur(N_Q_HEADS), _full((1, N_Q_HEADS))],
        out_specs=[cur(ATTN_WIDTH), cur(KV_WIDTH), cur(KV_WIDTH), _full((1, N_Q_HEADS))],
        out_shape=[jax.ShapeDtypeStruct((s, ATTN_WIDTH), F32), jax.ShapeDtypeStruct((s, KV_WIDTH), F32),
                   jax.ShapeDtypeStruct((s, KV_WIDTH), F32), jax.ShapeDtypeStruct((1, N_Q_HEADS), F32)],
        scratch_shapes=[acc, acc, carry, carry],
        args=(q, kz, kz, vz, vz, dattn, lse, sinks))


def _bwd_inproj(dq, dk, dv, du, cos, sin, win_t, x, g_mix, dx2):
    s = x.shape[0]
    tm = _token_tile(s)

    def body(dq_ref, dk_ref, dv_ref, du_ref, cos_ref, sin_ref, w_ref, x_ref, g_ref, dx2_ref,
             dx_ref, dw_ref, db_ref, dg_ref):
        @pl.when(pl.program_id(0) == 0)
        def _():
            dw_ref[...] = jnp.zeros_like(dw_ref)
            db_ref[...] = jnp.zeros_like(db_ref)
            dg_ref[...] = jnp.zeros_like(dg_ref)

        cos_t, sin_t = cos_ref[...], sin_ref[...]
        dz32 = jnp.concatenate([_rope_bwd(dq_ref[...], cos_t, sin_t), _rope_bwd(dk_ref[...], cos_t, sin_t),
                                dv_ref[...], du_ref[...].astype(F32)], axis=1)
        db_ref[...] += jnp.sum(dz32, axis=0, keepdims=True)
        dz = dz32.astype(BF16)
        g = g_ref[...]
        n, r = _rms(x_ref[...])
        h = (n * g).astype(BF16)
        for m0 in range(0, IN_WIDTH, TN_ROW_CHUNK):
            dw_ref[m0:m0 + TN_ROW_CHUNK, :] += _tn(dz[:, m0:m0 + TN_ROW_CHUNK], h)
        dx, dg = _rms_bwd(_nn(dz, w_ref[...]), n, r, g)
        dg_ref[...] += dg
        dx_ref[...] = dx2_ref[...] + dx

    return pl.pallas_call(
        body, name="bwd_inproj", grid=(s // tm,),
        in_specs=[_rows(tm, ATTN_WIDTH), _rows(tm, KV_WIDTH), _rows(tm, KV_WIDTH), _rows(tm, POOL_WIDTH),
                  _rows(tm, LANES), _rows(tm, LANES), _full((IN_WIDTH, D_MODEL)), _rows(tm, D_MODEL),
                  _full((1, D_MODEL)), _rows(tm, D_MODEL)],
        out_specs=[_rows(tm, D_MODEL), _full((IN_WIDTH, D_MODEL)), _full((1, IN_WIDTH)), _full((1, D_MODEL))],
        out_shape=[jax.ShapeDtypeStruct((s, D_MODEL), F32), jax.ShapeDtypeStruct((IN_WIDTH, D_MODEL), F32),
                   jax.ShapeDtypeStruct((1, IN_WIDTH), F32), jax.ShapeDtypeStruct((1, D_MODEL), F32)],
        compiler_params=_params("arbitrary"),
    )(dq, dk, dv, du, cos, sin, win_t, x, g_mix, dx2)


def _rope_tables(s):
    inv_freq = 1.0 / (ROPE_THETA ** (jnp.arange(0, HEAD_DIM, 2, dtype=F32) / HEAD_DIM))
    ang = jnp.arange(s, dtype=F32)[:, None] * inv_freq[None, :]
    cos, sin = jnp.cos(ang), jnp.sin(ang)
    return jnp.tile(cos, (1, 4)), jnp.tile(jnp.concatenate([-sin, sin], axis=1), (1, 2))


def _place():
    return lax.axis_index("x"), lax.axis_index("y"), lax.axis_index("c")


def _other_chips(x, y):
    return [(1 - x, y), (x, 1 - y), (1 - x, 1 - y)]


def _gather_rider(blocks):
    nm = len(blocks)

    def plan(ins, outs, sems):
        send_sems, recv_sems, local_sems = sems
        x, y, c = _place()
        me, sibling = (x, y, c), (x, y, 1 - c)
        chips = _other_chips(x, y)

        def rows(m, px, py, pc):
            r = ins[m].shape[0]
            return outs[m].at[pl.ds((4 * px + 2 * py + pc) * r, r), :]

        def copy(m, k, block, to, src=None):
            return pltpu.make_async_remote_copy(
                src_ref=rows(m, *block) if src is None else src, dst_ref=rows(m, *block),
                send_sem=send_sems.at[k * nm + m], recv_sem=recv_sems.at[k * nm + m],
                device_id=to, device_id_type=MESH)

        mine = [pltpu.make_async_copy(ins[m], rows(m, *me), local_sems.at[m]) for m in range(nm)]
        first = [copy(m, 0, me, sibling, src=ins[m]) for m in range(nm)]
        first += [copy(m, 1 + j, me, (*chip, c), src=ins[m]) for j, chip in enumerate(chips) for m in range(nm)]
        return me, sibling, chips, copy, mine, first

    def start(ins, outs, sems):
        *_, mine, first = plan(ins, outs, sems)
        for cp in mine + first:
            cp.start()

    def finish(ins, outs, sems):
        me, sibling, chips, copy, mine, first = plan(ins, outs, sems)
        c = me[2]
        passed = []
        for j, chip in enumerate(chips):
            for m in range(nm):
                copy(m, 1 + j, (*chip, c), me).wait_recv()
                passed.append(copy(m, 4 + j, (*chip, c), sibling))
                passed[-1].start()
        for m in range(nm):
            copy(m, 0, sibling, me).wait_recv()
        for j, chip in enumerate(chips):
            for m in range(nm):
                copy(m, 4 + j, (*chip, 1 - c), me).wait_recv()
        for cp in first + passed:
            cp.wait_send()
        for cp in mine:
            cp.wait()

    return _Rider(
        arrays=list(blocks), out_shape=[jax.ShapeDtypeStruct((N_DEV * b.shape[0], b.shape[1]), b.dtype) for b in blocks],
        sems=[pltpu.SemaphoreType.DMA((7 * nm,)), pltpu.SemaphoreType.DMA((7 * nm,)), pltpu.SemaphoreType.DMA((nm,))],
        start=start, finish=finish)


def _exchange_rider(copies_of, arrays, out_shape, n_copies):
    def copies(ins, outs, sems):
        send_sems, recv_sems = sems
        return [pltpu.make_async_remote_copy(src_ref=src, dst_ref=dst, send_sem=send_sems.at[k], recv_sem=recv_sems.at[k],
                                             device_id=to, device_id_type=MESH)
                for k, (src, dst, to) in enumerate(copies_of(ins, outs))]

    def start(ins, outs, sems):
        for cp in copies(ins, outs, sems):
            cp.start()

    def finish(ins, outs, sems):
        cps = copies(ins, outs, sems)
        for cp in cps:
            cp.wait_recv()
        for cp in cps:
            cp.wait_send()

    return _Rider(arrays=list(arrays), out_shape=out_shape,
                  sems=[pltpu.SemaphoreType.DMA((n_copies,)), pltpu.SemaphoreType.DMA((n_copies,))], start=start, finish=finish)


def _sibling_rider(grads):
    def copies_of(ins, outs):
        x, y, c = _place()
        for g_ref, o_ref in zip(ins, outs):
            r = g_ref.shape[0] // N_DEV
            for q in range(N_CHIPS):
                yield g_ref.at[pl.ds((2 * q + 1 - c) * r, r), :], o_ref.at[pl.ds(q * r, r), :], (x, y, 1 - c)

    return _exchange_rider(copies_of, grads, [jax.ShapeDtypeStruct((g.shape[0] // 2, g.shape[1]), F32) for g in grads],
                           len(grads) * N_CHIPS)


def _chip_sum(grad, from_sibling, place):
    r = grad.shape[0] // N_DEV
    w = grad.shape[1]

    def body(place_ref, g_ref, s_ref, wire_ref, own_ref):
        total = g_ref[...] + s_ref[...]
        wire_ref[...] = total.astype(BF16)

        @pl.when(pl.program_id(0) == place_ref[1])
        def _():
            own_ref[...] = total

    grid_spec = pltpu.PrefetchScalarGridSpec(
        num_scalar_prefetch=1, grid=(N_CHIPS,),
        in_specs=[pl.BlockSpec((r, w), lambda q, p: (2 * q + p[0], 0)), pl.BlockSpec((r, w), lambda q, p: (q, 0))],
        out_specs=[pl.BlockSpec((r, w), lambda q, p: (q, 0)), pl.BlockSpec((r, w), lambda q, p: (0, 0))])
    return pl.pallas_call(
        body, name="grad_chip_sum", grid_spec=grid_spec,
        out_shape=[jax.ShapeDtypeStruct((N_CHIPS * r, w), BF16), jax.ShapeDtypeStruct((r, w), F32)],
        compiler_params=_params("arbitrary"),
    )(place, grad, from_sibling)


def _chips_rider(wires):
    def copies_of(ins, outs):
        x, y, c = _place()
        for w_ref, o_ref in zip(ins, outs):
            r = w_ref.shape[0] // N_CHIPS
            for j, (px, py) in enumerate(_other_chips(x, y)):
                yield w_ref.at[pl.ds((2 * px + py) * r, r), :], o_ref.at[pl.ds(j * r, r), :], (px, py, c)

    return _exchange_rider(copies_of, wires,
                           [jax.ShapeDtypeStruct((3 * (w.shape[0] // N_CHIPS), w.shape[1]), BF16) for w in wires], len(wires) * 3)


def _final_sum(own, received):
    r, w = own.shape

    def body(own_ref, rec_ref, o_ref):
        o_ref[...] = ((own_ref[...] + rec_ref[0:r, :].astype(F32)) + rec_ref[r:2 * r, :].astype(F32)) + rec_ref[2 * r:, :].astype(F32)

    return pl.pallas_call(
        body, name="grad_final_sum", in_specs=[VMEM, VMEM], out_specs=VMEM,
        out_shape=jax.ShapeDtypeStruct((r, w), F32), compiler_params=_params(),
    )(own, received)


def _adamw_math(w, g, m, v):
    m = ADAM_B1 * m + (1.0 - ADAM_B1) * g
    v = ADAM_B2 * v + (1.0 - ADAM_B2) * jnp.square(g)
    m_hat = m / (1.0 - ADAM_B1 ** ADAM_STEP)
    v_hat = v / (1.0 - ADAM_B2 ** ADAM_STEP)
    delta = -ADAM_LR * (m_hat / (jnp.sqrt(v_hat) + ADAM_EPS) + ADAM_WD * w)
    return delta, m, v


def _adamw(w, g, m, v):
    def body(w_ref, g_ref, m_ref, v_ref, d_ref, nm_ref, nv_ref):
        d_ref[...], nm_ref[...], nv_ref[...] = _adamw_math(w_ref[...], g_ref[...], m_ref[...], v_ref[...])

    shape = jax.ShapeDtypeStruct(w.shape, F32)
    return pl.pallas_call(
        body, name="adamw", in_specs=[VMEM] * 4, out_specs=[VMEM] * 3, out_shape=[shape] * 3,
        compiler_params=_params(),
    )(w, g, m, v)


SMALL = (("g_mix", 1024), ("b_in", 1280), ("sinks", 8), ("w_pool", 65536), ("b_pool", 512), ("pool_scale", 512),
         ("b_out", 1024), ("g_ffn", 1024), ("g_final", 1024), ("loss", 1024))


def _small_rows(size):
    return -(-size // (8 * LANES)) * 8


SMALL_ROWS = sum(_small_rows(size) for _, size in SMALL)


def _pack_small(values):
    parts = []
    for name, size in SMALL:
        flat = values[name].reshape(-1).astype(F32)
        parts.append(jnp.pad(flat, (0, _small_rows(size) * LANES - size)).reshape(-1, LANES))
    return jnp.concatenate(parts, axis=0)


def _unpack_small(packed, shapes):
    out, row = {}, 0
    for name, size in SMALL:
        rows = _small_rows(size)
        if name in shapes:
            out[name] = packed[row:row + rows].reshape(-1)[:size].reshape(shapes[name])
        row += rows
    return out


def _small_allreduce_adamw(part, w, m, v):
    rows_n = SMALL_ROWS

    def body(p_ref, w_ref, m_ref, v_ref, g_ref, d_ref, nm_ref, nv_ref, all_ref, send_sems, recv_sems, local_sem):
        x, y, c = _place()
        me, sibling = (x, y, c), (x, y, 1 - c)
        chips = _other_chips(x, y)

        def rows(px, py, pc):
            return all_ref.at[pl.ds((4 * px + 2 * py + pc) * rows_n, rows_n), :]

        def copy(k, block, to, src=None):
            return pltpu.make_async_remote_copy(
                src_ref=rows(*block) if src is None else src, dst_ref=rows(*block),
                send_sem=send_sems.at[k], recv_sem=recv_sems.at[k], device_id=to, device_id_type=MESH)

        mine = pltpu.make_async_copy(p_ref, rows(*me), local_sem)
        mine.start()
        first = [copy(0, me, sibling, src=p_ref)]
        first += [copy(1 + j, me, (*chip, c), src=p_ref) for j, chip in enumerate(chips)]
        for cp in first:
            cp.start()
        passed = [copy(4 + j, (*chip, c), sibling) for j, chip in enumerate(chips)]
        for j, chip in enumerate(chips):
            copy(1 + j, (*chip, c), me).wait_recv()
            passed[j].start()
        copy(0, sibling, me).wait_recv()
        for j, chip in enumerate(chips):
            copy(4 + j, (*chip, 1 - c), me).wait_recv()
        for cp in first + passed:
            cp.wait_send()
        mine.wait()
        total = all_ref[0:rows_n, :]
        for dev in range(1, N_DEV):
            total = total + all_ref[dev * rows_n:(dev + 1) * rows_n, :]
        g_ref[...] = total
        d_ref[...], nm_ref[...], nv_ref[...] = _adamw_math(w_ref[...], total, m_ref[...], v_ref[...])

    shape = jax.ShapeDtypeStruct((rows_n, LANES), F32)
    return pl.pallas_call(
        body, name="small_allreduce_adamw", in_specs=[VMEM] * 4, out_specs=[VMEM] * 4, out_shape=[shape] * 4,
        scratch_shapes=[pltpu.VMEM((N_DEV * rows_n, LANES), F32), pltpu.SemaphoreType.DMA((7,)), pltpu.SemaphoreType.DMA((7,)),
                        pltpu.SemaphoreType.DMA],
        compiler_params=pltpu.CompilerParams(has_side_effects=True, vmem_limit_bytes=VMEM_LIMIT_BYTES),
    )(part, w, m, v)


def kernel(x, g_mix, w_in, b_in, sinks, w_pool, b_pool, pool_scale, w_out, b_out, g_ffn, w_gate, w_up, w_down, g_final, loss_target, m_g_mix, m_w_in, m_b_in, m_sinks, m_w_pool, m_b_pool, m_pool_scale, m_w_out, m_b_out, m_g_ffn, m_w_gate, m_w_up, m_w_down, m_g_final, v_g_mix, v_w_in, v_b_in, v_sinks, v_w_pool, v_b_pool, v_pool_scale, v_w_out, v_b_out, v_g_ffn, v_w_gate, v_w_up, v_w_down, v_g_final):
    weights = dict(g_mix=g_mix, w_in=w_in, b_in=b_in, sinks=sinks, w_pool=w_pool, b_pool=b_pool, pool_scale=pool_scale,
                   w_out=w_out, b_out=b_out, g_ffn=g_ffn, w_gate=w_gate, w_up=w_up, w_down=w_down, g_final=g_final)
    mom1 = dict(g_mix=m_g_mix, w_in=m_w_in, b_in=m_b_in, sinks=m_sinks, w_pool=m_w_pool, b_pool=m_b_pool,
                pool_scale=m_pool_scale, w_out=m_w_out, b_out=m_b_out, g_ffn=m_g_ffn, w_gate=m_w_gate, w_up=m_w_up,
                w_down=m_w_down, g_final=m_g_final)
    mom2 = dict(g_mix=v_g_mix, w_in=v_w_in, b_in=v_b_in, sinks=v_sinks, w_pool=v_w_pool, b_pool=v_b_pool,
                pool_scale=v_pool_scale, w_out=v_w_out, b_out=v_b_out, g_ffn=v_g_ffn, w_gate=v_w_gate, w_up=v_w_up,
                w_down=v_w_down, g_final=v_g_final)
    order = ("g_mix", "w_in", "b_in", "sinks", "w_pool", "b_pool", "pool_scale", "w_out", "b_out", "g_ffn",
             "w_gate", "w_up", "w_down", "g_final")
    big = ("w_in", "w_out", "w_gate", "w_up", "w_down")
    transposed = ("w_in", "w_gate", "w_up")

    def row_shard(name, a):
        return a[0].T if name in transposed else a[0]

    shard = {n: row_shard(n, weights[n]).astype(BF16) for n in big}
    xs, target = x[0], loss_target[0]
    cos, sin = _rope_tables(xs.shape[0])
    wp_b = w_pool[0].astype(BF16)
    bp = b_pool.reshape(1, POOL_WIDTH)
    ps = pool_scale.reshape(1, POOL_WIDTH)
    g_fin = g_final.reshape(1, D_MODEL)
    px, py, pc = _place()
    place = jnp.stack([pc, 2 * px + py]).astype(jnp.int32)

    (win_t,) = _alone(_gather_rider([shard["w_in"]]), "gather_w_in")
    q, kz, vz, mixed, pool, w_out_b, wg_t = _fwd_inproj(
        xs, g_mix, win_t, b_in, cos, sin, wp_b, bp, ps, rider=_gather_rider([shard["w_out"], shard["w_gate"]]))
    attn, lse, wu_t, wd = _attn_fwd(q, kz, vz, sinks, rider=_gather_rider([shard["w_up"], shard["w_down"]]))
    x2, h2 = _fwd_outproj(attn, pool, w_out_b, b_out, xs, g_ffn)
    gate, up, act = _fwd_ffn_act(h2, wg_t, wu_t)
    dx3, dx3b, sq, dg_final = _fwd_down_loss(act, x2, wd, g_fin, target)

    dgate, dup = _bwd_ffn_act(dx3b, gate, up, wd)
    dx2, dx2b, dg_ffn, db_out = _bwd_ffn_in(dgate, dup, wg_t, wu_t, x2, dx3, g_ffn)
    ffn_grads = [_weight_grad(dgate, h2, "grad_w_gate"), _weight_grad(dup, h2, "grad_w_up"),
                 _weight_grad(act, dx3b, "grad_w_down")]
    dattn, du, d_wout, d_wpool, d_bpool, d_pscale, *ffn_sibling = _bwd_outproj_pool(
        dx2b, attn, pool, mixed, w_out_b, wp_b, bp, ps, rider=_sibling_rider(ffn_grads))
    ffn_sums = [_chip_sum(g, s, place) for g, s in zip(ffn_grads, ffn_sibling)]
    dq, dk, dv, d_sinks, *ffn_received = _attn_bwd(q, kz, vz, dattn, lse, sinks,
                                                    rider=_chips_rider([wire for wire, _ in ffn_sums]))
    dx, d_win_t, d_bin, d_gmix = _bwd_inproj(dq, dk, dv, du, cos, sin, win_t, xs, g_mix, dx2)
    mix_grads = [d_win_t, d_wout]
    mix_sibling = _alone(_sibling_rider(mix_grads), "grad_exchange_sibling")
    mix_sums = [_chip_sum(g, s, place) for g, s in zip(mix_grads, mix_sibling)]
    mix_received = _alone(_chips_rider([wire for wire, _ in mix_sums]), "grad_exchange_chips")
    small = dict(g_mix=d_gmix, b_in=d_bin, sinks=d_sinks, w_pool=d_wpool, b_pool=d_bpool, pool_scale=d_pscale,
                 b_out=db_out, g_ffn=dg_ffn, g_final=dg_final, loss=sq)

    reduced = dict(zip(("w_gate", "w_up", "w_down", "w_in", "w_out"),
                       zip(ffn_sums + mix_sums, list(ffn_received) + list(mix_received))))
    grad, delta, new_m, new_v = {}, {}, {}, {}
    for n in big:
        (_, own), rec = reduced[n]
        g = _final_sum(own, rec)
        if n in transposed:
            g = g.T
        d, nm_, nv_ = _adamw(weights[n][0], g, mom1[n][0], mom2[n][0])
        grad[n], delta[n], new_m[n], new_v[n] = g[None], d[None], nm_[None], nv_[None]

    shapes = {n: weights[n].shape for n in order if n not in big}
    zero_loss = jnp.zeros((1, D_MODEL), F32)
    packed = _small_allreduce_adamw(
        _pack_small(small), _pack_small({**weights, "loss": zero_loss}),
        _pack_small({**mom1, "loss": zero_loss}), _pack_small({**mom2, "loss": zero_loss}))
    for store, pk in zip((grad, delta, new_m, new_v), packed):
        store.update(_unpack_small(pk, shapes))
    loss_rows = _unpack_small(packed[0], {"loss": (D_MODEL,)})["loss"]
    loss = (0.5 / D_MODEL) * jnp.sum(loss_rows)

    return (loss, dx[None], *[grad[n] for n in order], *[delta[n] for n in order],
            *[new_m[n] for n in order], *[new_v[n] for n in order])
```

```python
from typing import Any, Callable, NamedTuple, Sequence

import jax
import jax.numpy as jnp
from jax import lax
from jax.experimental import pallas as pl
from jax.experimental.pallas import tpu as pltpu

D_MODEL = 1024
ATTN_WIDTH = 512
KV_WIDTH = 128
POOL_WIDTH = 512
HEAD_DIM = 64
N_Q_HEADS = 8
N_KV_HEADS = 2
GQA_GROUP = 4
BLOCK = 128
POOL_SIZES = (2, 4, 8, 16)
POOL_GROUP_WIDTH = 128
POOL_HALO = 16
IN_WIDTH = 1280
D_FF = 2816
RMS_EPS = 1e-5
ROPE_THETA = 10000.0
Q_SCALE = HEAD_DIM ** -0.5

ADAM_LR = 0.001
ADAM_B1 = 0.9
ADAM_B2 = 0.999
ADAM_EPS = 1e-08
ADAM_WD = 0.01
ADAM_STEP = 10

N_DEV = 8
N_CHIPS = 4
LANES = 128
VMEM_LIMIT_BYTES = 56 * 1024 * 1024

F32 = jnp.float32
BF16 = jnp.bfloat16
MESH = pl.DeviceIdType.MESH
HBM = pl.BlockSpec(memory_space=pltpu.HBM)
VMEM = pl.BlockSpec(memory_space=pltpu.VMEM)


def _params(*semantics):
    return pltpu.CompilerParams(dimension_semantics=semantics or None, vmem_limit_bytes=VMEM_LIMIT_BYTES)


def _nn(a, b):
    return jnp.dot(a, b, preferred_element_type=F32)


def _nt(a, b):
    return lax.dot_general(a, b, (((1,), (1,)), ((), ())), preferred_element_type=F32)


def _tn(a, b):
    return lax.dot_general(a, b, (((0,), (0,)), ((), ())), preferred_element_type=F32)


def _full(shape):
    return pl.BlockSpec(shape, lambda *_: (0,) * len(shape))


def _rows(tm, width):
    return pl.BlockSpec((tm, width), lambda i, *_: (i, 0))


class _Rider(NamedTuple):
    arrays: Sequence[Any]
    out_shape: Sequence[Any]
    sems: Sequence[Any]
    start: Callable[..., None]
    finish: Callable[..., None]


def _gridded(body, rider, *, name, grid, in_specs, out_specs, out_shape, scratch_shapes, args):
    params = _params("arbitrary")
    if rider is None:
        return pl.pallas_call(body, name=name, grid=grid, in_specs=in_specs, out_specs=out_specs, out_shape=out_shape,
                              scratch_shapes=scratch_shapes, compiler_params=params)(*args)
    bounds, total = [], 0
    for n in (len(in_specs), len(rider.arrays), len(out_specs), len(rider.out_shape), len(scratch_shapes), len(rider.sems)):
        bounds.append((total, total + n))
        total += n
    last = grid[0] - 1

    def riding(*refs):
        ins, r_ins, outs, r_outs, scratch, r_sems = (refs[lo:hi] for lo, hi in bounds)

        @pl.when(pl.program_id(0) == 0)
        def _():
            rider.start(r_ins, r_outs, r_sems)

        body(*ins, *outs, *scratch)

        @pl.when(pl.program_id(0) == last)
        def _():
            rider.finish(r_ins, r_outs, r_sems)

    return pl.pallas_call(
        riding, name=name, grid=grid, in_specs=list(in_specs) + [HBM] * len(rider.arrays),
        out_specs=list(out_specs) + [HBM] * len(rider.out_shape), out_shape=list(out_shape) + list(rider.out_shape),
        scratch_shapes=list(scratch_shapes) + list(rider.sems), compiler_params=params)(*args, *rider.arrays)


def _join(*riders):
    def phase(which):
        def run(ins, outs, sems):
            i = o = s = 0
            for r in riders:
                ni, no, ns = len(r.arrays), len(r.out_shape), len(r.sems)
                getattr(r, which)(ins[i:i + ni], outs[o:o + no], sems[s:s + ns])
                i, o, s = i + ni, o + no, s + ns
        return run

    return _Rider(arrays=[a for r in riders for a in r.arrays], out_shape=[a for r in riders for a in r.out_shape],
                  sems=[a for r in riders for a in r.sems], start=phase("start"), finish=phase("finish"))


def _alone(rider, name):
    n_in, n_out = len(rider.arrays), len(rider.out_shape)

    def body(*refs):
        parts = refs[:n_in], refs[n_in:n_in + n_out], refs[n_in + n_out:]
        rider.start(*parts)
        rider.finish(*parts)

    return pl.pallas_call(body, name=name, in_specs=[HBM] * n_in, out_specs=[HBM] * n_out, out_shape=list(rider.out_shape),
                          scratch_shapes=list(rider.sems))(*rider.arrays)


def _rot_half(t):
    n = t.shape[1]
    lane = lax.broadcasted_iota(jnp.int32, t.shape, 1)
    return jnp.where((lane % HEAD_DIM) < HEAD_DIM // 2, pltpu.roll(t, n - HEAD_DIM // 2, 1), pltpu.roll(t, HEAD_DIM // 2, 1))


def _rope(t, cos, sin):
    reps = t.shape[1] // LANES
    if reps > 1:
        cos, sin = jnp.tile(cos, (1, reps)), jnp.tile(sin, (1, reps))
    return t * cos + _rot_half(t) * sin


def _rope_bwd(d, cos, sin):
    reps = d.shape[1] // LANES
    if reps > 1:
        cos, sin = jnp.tile(cos, (1, reps)), jnp.tile(sin, (1, reps))
    return d * cos + _rot_half(d * sin)


KV_SPREAD = 4 * LANES


def _spread_kv(t):
    low = lax.broadcasted_iota(jnp.int32, t.shape, 1) < HEAD_DIM
    swapped = pltpu.roll(t, HEAD_DIM, 1)
    zero = jnp.zeros_like(t)
    return jnp.concatenate([jnp.where(low, t, zero), jnp.where(low, zero, swapped),
                            jnp.where(low, swapped, zero), jnp.where(low, zero, t)], axis=1)


def _rms(x):
    r = lax.rsqrt(jnp.mean(x * x, axis=-1, keepdims=True) + RMS_EPS)
    return x * r, r


def _rms_bwd(dh, n, r, g):
    dn = dh * g
    dx = r * (dn - n * jnp.mean(dn * n, axis=-1, keepdims=True))
    return dx, jnp.sum(dh * n, axis=0, keepdims=True)


def _token_tile(s):
    return min(512, s)


def _fwd_inproj(x, g_mix, win_t, b_in, cos, sin, w_pool, b_pool, pool_scale, rider=None):
    s = x.shape[0]
    tm = _token_tile(s)

    def body(x_ref, g_ref, w_ref, b_ref, cos_ref, sin_ref, wp_ref, bp_ref, ps_ref,
             q_ref, k_ref, v_ref, mix_ref, pool_ref, tail_ref):
        i = pl.program_id(0)

        @pl.when(i == 0)
        def _():
            tail_ref[...] = jnp.zeros_like(tail_ref)

        n, _ = _rms(x_ref[...])
        h = (n * g_ref[...]).astype(BF16)
        z = _nt(h, w_ref[...]) + b_ref[...]
        cos_t, sin_t = cos_ref[...], sin_ref[...]
        q_ref[...] = (_rope(z[:, :ATTN_WIDTH], cos_t, sin_t) * Q_SCALE).astype(BF16)
        k_ref[...] = _spread_kv(_rope(z[:, ATTN_WIDTH:ATTN_WIDTH + KV_WIDTH], cos_t, sin_t)).astype(BF16)
        v_ref[...] = _spread_kv(z[:, ATTN_WIDTH + KV_WIDTH:ATTN_WIDTH + 2 * KV_WIDTH]).astype(BF16)
        u = z[:, ATTN_WIDTH + 2 * KV_WIDTH:]
        u_ext = jnp.concatenate([tail_ref[...], u], axis=0)
        tail_ref[...] = u[tm - POOL_HALO:, :]
        pos = lax.broadcasted_iota(jnp.int32, (tm, POOL_GROUP_WIDTH), 0) + i * tm
        for g, size in enumerate(POOL_SIZES):
            cols = slice(g * POOL_GROUP_WIDTH, (g + 1) * POOL_GROUP_WIDTH)
            a = u_ext[:, cols]
            shift = 1
            while shift < size:
                a = a + pltpu.roll(a, shift, 0)
                shift *= 2
            count = jnp.minimum(pos + 1, size).astype(F32)
            mixed = (a[POOL_HALO:, :] / count - u[:, cols]).astype(BF16)
            pre = _nn(mixed, wp_ref[g]) + bp_ref[:, cols]
            mix_ref[:, cols] = mixed
            pool_ref[:, cols] = (pre * ps_ref[:, cols]).astype(BF16)

    bf = lambda w: jax.ShapeDtypeStruct((s, w), BF16)
    return _gridded(
        body, rider, name="fwd_inproj", grid=(s // tm,),
        in_specs=[_rows(tm, D_MODEL), _full((1, D_MODEL)), _full((IN_WIDTH, D_MODEL)), _full((1, IN_WIDTH)),
                  _rows(tm, LANES), _rows(tm, LANES), _full((4, POOL_GROUP_WIDTH, POOL_GROUP_WIDTH)),
                  _full((1, POOL_WIDTH)), _full((1, POOL_WIDTH))],
        out_specs=[_rows(tm, ATTN_WIDTH), _rows(tm, KV_SPREAD), _rows(tm, KV_SPREAD), _rows(tm, POOL_WIDTH), _rows(tm, POOL_WIDTH)],
        out_shape=[bf(ATTN_WIDTH), bf(KV_SPREAD), bf(KV_SPREAD), bf(POOL_WIDTH), bf(POOL_WIDTH)],
        scratch_shapes=[pltpu.VMEM((POOL_HALO, POOL_WIDTH), F32)],
        args=(x, g_mix, win_t, b_in, cos, sin, w_pool, b_pool, pool_scale))


ATTN_TILE = 512
PAIR = 2 * LANES


def _band_masks(tile):
    r = lax.broadcasted_iota(jnp.int32, (2 * BLOCK, 4 * BLOCK), 0) % BLOCK
    j = lax.broadcasted_iota(jnp.int32, (2 * BLOCK, 4 * BLOCK), 1) % (2 * BLOCK)
    band = (j > r) & (j <= r + BLOCK)
    return band & ((tile > 0) | (j >= BLOCK)), band


def _band(cur_ref, prev_ref, b, kv):
    halves = []
    for half in range(2):
        cols = slice(kv * PAIR + half * LANES, kv * PAIR + (half + 1) * LANES)
        if b == 0:
            halves.append(jnp.concatenate([prev_ref[:, cols], cur_ref[0:BLOCK, cols]], axis=0))
        else:
            halves.append(cur_ref[(b - 1) * BLOCK:(b + 1) * BLOCK, cols])
    return jnp.concatenate(halves, axis=0)


def _stack_pair(ref, rows, kv):
    return jnp.concatenate([ref[rows, kv * PAIR:kv * PAIR + LANES], ref[rows, kv * PAIR + LANES:(kv + 1) * PAIR]], axis=0)


def _pair_heads(kv, half):
    return GQA_GROUP * kv + half, GQA_GROUP * kv + 2 + half


def _per_row(ref, rows, top, bottom):
    return jnp.concatenate([ref[rows, top:top + 1], ref[rows, bottom:bottom + 1]], axis=0)


def _sink_rows(sink_ref, top, bottom):
    upper = lax.broadcasted_iota(jnp.int32, (2 * BLOCK, 1), 0) < BLOCK
    return jnp.where(upper, sink_ref[:, top:top + 1], sink_ref[:, bottom:bottom + 1])


def _attn_fwd(q, kz, vz, sinks, rider=None):
    s = q.shape[0]
    tq = min(ATTN_TILE, s)

    def body(q_ref, k_ref, kp_ref, v_ref, vp_ref, sink_ref, o_ref, lse_ref):
        first, band = _band_masks(pl.program_id(0))
        for b in range(tq // BLOCK):
            rows = slice(b * BLOCK, (b + 1) * BLOCK)
            mask = first if b == 0 else band
            for kv in range(N_KV_HEADS):
                sc = jnp.where(mask, _nt(_stack_pair(q_ref, rows, kv), _band(k_ref, kp_ref, b, kv)), -jnp.inf)
                probs = []
                for half in range(2):
                    top, bottom = _pair_heads(kv, half)
                    sink = _sink_rows(sink_ref, top, bottom)
                    sh = sc[:, half * 2 * BLOCK:(half + 1) * 2 * BLOCK]
                    m = jnp.maximum(jnp.max(sh, axis=-1, keepdims=True), sink)
                    p = jnp.exp(sh - m)
                    denom = jnp.sum(p, axis=-1, keepdims=True) + jnp.exp(sink - m)
                    probs.append((p * (1.0 / denom)).astype(BF16))
                    lse = m + jnp.log(denom)
                    lse_ref[rows, top:top + 1] = lse[:BLOCK]
                    lse_ref[rows, bottom:bottom + 1] = lse[BLOCK:]
                o = _nn(jnp.concatenate(probs, axis=1), _band(v_ref, vp_ref, b, kv)).astype(BF16)
                o_ref[rows, kv * PAIR:kv * PAIR + LANES] = o[:BLOCK]
                o_ref[rows, kv * PAIR + LANES:(kv + 1) * PAIR] = o[BLOCK:]

    per = tq // BLOCK
    cur = lambda w: pl.BlockSpec((tq, w), lambda i: (i, 0))
    prev = pl.BlockSpec((BLOCK, KV_SPREAD), lambda i: (jnp.maximum(per * i - 1, 0), 0))
    return _gridded(
        body, rider, name="attn_fwd", grid=(s // tq,),
        in_specs=[cur(ATTN_WIDTH), cur(KV_SPREAD), prev, cur(KV_SPREAD), prev, _full((1, N_Q_HEADS))],
        out_specs=[cur(ATTN_WIDTH), cur(N_Q_HEADS)],
        out_shape=[jax.ShapeDtypeStruct((s, ATTN_WIDTH), BF16), jax.ShapeDtypeStruct((s, N_Q_HEADS), F32)],
        scratch_shapes=[], args=(q, kz, kz, vz, vz, sinks))


def _fwd_outproj(attn, pool, w_out, b_out, x, g_ffn):
    s = x.shape[0]
    tm = _token_tile(s)

    def body(a_ref, p_ref, w_ref, b_ref, x_ref, g_ref, x2_ref, h2_ref):
        x2 = x_ref[...] + _nn(a_ref[...], w_ref[:ATTN_WIDTH, :]) + _nn(p_ref[...], w_ref[ATTN_WIDTH:, :]) + b_ref[...]
        x2_ref[...] = x2
        n, _ = _rms(x2)
        h2_ref[...] = (n * g_ref[...]).astype(BF16)

    return pl.pallas_call(
        body, name="fwd_outproj", grid=(s // tm,),
        in_specs=[_rows(tm, ATTN_WIDTH), _rows(tm, POOL_WIDTH), _full((D_MODEL, D_MODEL)), _full((1, D_MODEL)),
                  _rows(tm, D_MODEL), _full((1, D_MODEL))],
        out_specs=[_rows(tm, D_MODEL), _rows(tm, D_MODEL)],
        out_shape=[jax.ShapeDtypeStruct((s, D_MODEL), F32), jax.ShapeDtypeStruct((s, D_MODEL), BF16)],
        compiler_params=_params("parallel"),
    )(attn, pool, w_out, b_out, x, g_ffn)


FF_CHUNK = 256


def _resident(shape):
    return pl.BlockSpec(shape, lambda *_: (0,) * len(shape), pipeline_mode=pl.Buffered(1))


def _fwd_ffn_act(h2, wg_t, wu_t):
    s = h2.shape[0]
    tm = _token_tile(s)

    def body(h_ref, wg_ref, wu_ref, gate_ref, up_ref, act_ref):
        h = h_ref[...]
        for c0 in range(0, D_FF, FF_CHUNK):
            cols = slice(c0, c0 + FF_CHUNK)
            gate = _nt(h, wg_ref[cols, :])
            up = _nt(h, wu_ref[cols, :])
            gate_ref[:, cols] = gate.astype(BF16)
            up_ref[:, cols] = up.astype(BF16)
            act_ref[:, cols] = (gate * jax.nn.sigmoid(gate) * up).astype(BF16)

    act_shape = jax.ShapeDtypeStruct((s, D_FF), BF16)
    return pl.pallas_call(
        body, name="fwd_ffn_act", grid=(s // tm,),
        in_specs=[_rows(tm, D_MODEL), _resident((D_FF, D_MODEL)), _resident((D_FF, D_MODEL))],
        out_specs=[_rows(tm, D_FF)] * 3, out_shape=[act_shape] * 3,
        compiler_params=_params("parallel"),
    )(h2, wg_t, wu_t)


def _fwd_down_loss(act, x2, wd, g_final, target):
    s = x2.shape[0]
    tm = _token_tile(s)

    def body(a_ref, x2_ref, wd_ref, g_ref, t_ref, dx3_ref, dx3b_ref, sq_ref, dg_ref):
        @pl.when(pl.program_id(0) == 0)
        def _():
            sq_ref[...] = jnp.zeros_like(sq_ref)
            dg_ref[...] = jnp.zeros_like(dg_ref)

        x3 = x2_ref[...] + _nn(a_ref[...], wd_ref[...])
        n, r = _rms(x3)
        g = g_ref[...]
        diff = n * g - t_ref[...]
        sq_ref[...] += jnp.sum(diff * diff, axis=0, keepdims=True)
        dx3, dg = _rms_bwd(diff * (1.0 / D_MODEL), n, r, g)
        dg_ref[...] += dg
        dx3_ref[...] = dx3
        dx3b_ref[...] = dx3.astype(BF16)

    return pl.pallas_call(
        body, name="fwd_down_loss", grid=(s // tm,),
        in_specs=[_rows(tm, D_FF), _rows(tm, D_MODEL), _resident((D_FF, D_MODEL)), _full((1, D_MODEL)), _rows(tm, D_MODEL)],
        out_specs=[_rows(tm, D_MODEL), _rows(tm, D_MODEL), _full((1, D_MODEL)), _full((1, D_MODEL))],
        out_shape=[jax.ShapeDtypeStruct((s, D_MODEL), F32), jax.ShapeDtypeStruct((s, D_MODEL), BF16),
                   jax.ShapeDtypeStruct((1, D_MODEL), F32), jax.ShapeDtypeStruct((1, D_MODEL), F32)],
        compiler_params=_params("arbitrary"),
    )(act, x2, wd, g_final, target)


def _bwd_ffn_act(dx3b, gate, up, wd):
    s = dx3b.shape[0]
    tm = _token_tile(s)

    def body(dx3_ref, gate_ref, up_ref, wd_ref, dgate_ref, dup_ref):
        dx3 = dx3_ref[...]
        for c0 in range(0, D_FF, FF_CHUNK):
            cols = slice(c0, c0 + FF_CHUNK)
            dact = _nt(dx3, wd_ref[cols, :])
            gate = gate_ref[:, cols].astype(F32)
            up = up_ref[:, cols].astype(F32)
            sig = jax.nn.sigmoid(gate)
            silu = gate * sig
            dup_ref[:, cols] = (dact * silu).astype(BF16)
            dgate_ref[:, cols] = (dact * up * (sig + silu * (1.0 - sig))).astype(BF16)

    act_shape = jax.ShapeDtypeStruct((s, D_FF), BF16)
    return pl.pallas_call(
        body, name="bwd_ffn_act", grid=(s // tm,),
        in_specs=[_rows(tm, D_MODEL), _rows(tm, D_FF), _rows(tm, D_FF), _resident((D_FF, D_MODEL))],
        out_specs=[_rows(tm, D_FF)] * 2, out_shape=[act_shape] * 2,
        compiler_params=_params("parallel"),
    )(dx3b, gate, up, wd)


def _bwd_ffn_in(dgate, dup, wg_t, wu_t, x2, dx3, g_ffn):
    s = x2.shape[0]
    tm = _token_tile(s)

    def body(dgate_ref, dup_ref, wg_ref, wu_ref, x2_ref, dx3_ref, g_ref, dx2_ref, dx2b_ref, dg_ref, db_ref):
        @pl.when(pl.program_id(0) == 0)
        def _():
            dg_ref[...] = jnp.zeros_like(dg_ref)
            db_ref[...] = jnp.zeros_like(db_ref)

        dh2 = _nn(dgate_ref[...], wg_ref[...]) + _nn(dup_ref[...], wu_ref[...])
        n, r = _rms(x2_ref[...])
        dx, dg = _rms_bwd(dh2, n, r, g_ref[...])
        dx2 = dx3_ref[...] + dx
        dg_ref[...] += dg
        db_ref[...] += jnp.sum(dx2, axis=0, keepdims=True)
        dx2_ref[...] = dx2
        dx2b_ref[...] = dx2.astype(BF16)

    return pl.pallas_call(
        body, name="bwd_ffn_in", grid=(s // tm,),
        in_specs=[_rows(tm, D_FF), _rows(tm, D_FF), _resident((D_FF, D_MODEL)), _resident((D_FF, D_MODEL)),
                  _rows(tm, D_MODEL), _rows(tm, D_MODEL), _full((1, D_MODEL))],
        out_specs=[_rows(tm, D_MODEL), _rows(tm, D_MODEL), _full((1, D_MODEL)), _full((1, D_MODEL))],
        out_shape=[jax.ShapeDtypeStruct((s, D_MODEL), F32), jax.ShapeDtypeStruct((s, D_MODEL), BF16),
                   jax.ShapeDtypeStruct((1, D_MODEL), F32), jax.ShapeDtypeStruct((1, D_MODEL), F32)],
        compiler_params=_params("arbitrary"),
    )(dgate, dup, wg_t, wu_t, x2, dx3, g_ffn)


TN_ROW_CHUNK = 256


def _weight_grad(a, b, name):
    s, m = a.shape
    n_out = b.shape[1]
    tm = _token_tile(s)

    def body(a_ref, b_ref, o_ref):
        @pl.when(pl.program_id(0) == 0)
        def _():
            o_ref[...] = jnp.zeros_like(o_ref)

        bt = b_ref[...]
        for m0 in range(0, m, TN_ROW_CHUNK):
            o_ref[m0:m0 + TN_ROW_CHUNK, :] += _tn(a_ref[:, m0:m0 + TN_ROW_CHUNK], bt)

    return pl.pallas_call(
        body, name=name, grid=(s // tm,),
        in_specs=[_rows(tm, m), _rows(tm, n_out)],
        out_specs=_full((m, n_out)),
        out_shape=jax.ShapeDtypeStruct((m, n_out), F32),
        compiler_params=_params("arbitrary"),
    )(a, b)


def _bwd_outproj_pool(dx2b, attn, pool, mixed, w_out, w_pool, b_pool, pool_scale, rider=None):
    s = dx2b.shape[0]
    tm = _token_tile(s)
    nt = s // tm

    def body(dx_ref, a_ref, p_ref, mix_ref, w_ref, wp_ref, bp_ref, ps_ref,
             dattn_ref, du_ref, dwout_ref, dwp_ref, dbp_ref, dps_ref, head_ref):
        step = pl.program_id(0)
        tile = nt - 1 - step

        @pl.when(step == 0)
        def _():
            head_ref[...] = jnp.zeros_like(head_ref)
            dwout_ref[...] = jnp.zeros_like(dwout_ref)
            dwp_ref[...] = jnp.zeros_like(dwp_ref)
            dbp_ref[...] = jnp.zeros_like(dbp_ref)
            dps_ref[...] = jnp.zeros_like(dps_ref)

        dx = dx_ref[...]
        dwout_ref[:ATTN_WIDTH, :] += _tn(a_ref[...], dx)
        dwout_ref[ATTN_WIDTH:, :] += _tn(p_ref[...], dx)
        dcat = _nt(dx, w_ref[...])
        dattn_ref[...] = dcat[:, :ATTN_WIDTH].astype(BF16)
        dpool = dcat[:, ATTN_WIDTH:]
        pos = lax.broadcasted_iota(jnp.int32, (tm, POOL_GROUP_WIDTH), 0) + tile * tm
        head = head_ref[...]
        n_ext = tm + POOL_HALO
        for g, size in enumerate(POOL_SIZES):
            cols = slice(g * POOL_GROUP_WIDTH, (g + 1) * POOL_GROUP_WIDTH)
            mixed_g = mix_ref[:, cols]
            pre = _nn(mixed_g, wp_ref[g]) + bp_ref[:, cols]
            dy = dpool[:, cols]
            dps_ref[:, cols] += jnp.sum(dy * pre, axis=0, keepdims=True)
            dpre = dy * ps_ref[:, cols]
            dbp_ref[:, cols] += jnp.sum(dpre, axis=0, keepdims=True)
            dpre_b = dpre.astype(BF16)
            dwp_ref[g] += _tn(mixed_g, dpre_b)
            dmixed = _nt(dpre_b, wp_ref[g])
            w = dmixed / jnp.minimum(pos + 1, size).astype(F32)
            head_ref[:, cols] = w[:POOL_HALO, :]
            a = jnp.concatenate([w, head[:, cols]], axis=0)
            shift = 1
            while shift < size:
                a = a + pltpu.roll(a, n_ext - shift, 0)
                shift *= 2
            du_ref[:, cols] = (a[:tm, :] - dmixed).astype(BF16)

    rev = lambda w: pl.BlockSpec((tm, w), lambda i: (nt - 1 - i, 0))
    return _gridded(
        body, rider, name="bwd_outproj_pool", grid=(nt,),
        in_specs=[rev(D_MODEL), rev(ATTN_WIDTH), rev(POOL_WIDTH), rev(POOL_WIDTH), _full((D_MODEL, D_MODEL)),
                  _full((4, POOL_GROUP_WIDTH, POOL_GROUP_WIDTH)), _full((1, POOL_WIDTH)), _full((1, POOL_WIDTH))],
        out_specs=[rev(ATTN_WIDTH), rev(POOL_WIDTH), _full((D_MODEL, D_MODEL)),
                   _full((4, POOL_GROUP_WIDTH, POOL_GROUP_WIDTH)), _full((1, POOL_WIDTH)), _full((1, POOL_WIDTH))],
        out_shape=[jax.ShapeDtypeStruct((s, ATTN_WIDTH), BF16), jax.ShapeDtypeStruct((s, POOL_WIDTH), BF16),
                   jax.ShapeDtypeStruct((D_MODEL, D_MODEL), F32),
                   jax.ShapeDtypeStruct((4, POOL_GROUP_WIDTH, POOL_GROUP_WIDTH), F32),
                   jax.ShapeDtypeStruct((1, POOL_WIDTH), F32), jax.ShapeDtypeStruct((1, POOL_WIDTH), F32)],
        scratch_shapes=[pltpu.VMEM((POOL_HALO, POOL_WIDTH), F32)],
        args=(dx2b, attn, pool, mixed, w_out, w_pool, b_pool, pool_scale))


def _fold_spread(t):
    low = lax.broadcasted_iota(jnp.int32, (2 * BLOCK, LANES), 1) < HEAD_DIM
    kept = jnp.where(low, t[:2 * BLOCK, :], t[2 * BLOCK:, :])
    return kept + pltpu.roll(kept, HEAD_DIM, 1)


def _attn_bwd(q, kz, vz, dattn, lse, sinks, rider=None):
    s = q.shape[0]
    tq = min(ATTN_TILE, s)
    nt = s // tq
    per = tq // BLOCK

    def body(q_ref, k_ref, kp_ref, v_ref, vp_ref, do_ref, lse_ref, sink_ref,
             dq_ref, dk_ref, dv_ref, dsink_ref, dk_acc, dv_acc, dk_carry, dv_carry):
        step = pl.program_id(0)

        @pl.when(step == 0)
        def _():
            dk_carry[...] = jnp.zeros_like(dk_carry)
            dv_carry[...] = jnp.zeros_like(dv_carry)
            dsink_ref[...] = jnp.zeros_like(dsink_ref)

        dk_acc[0:tq, :] = jnp.zeros((tq, KV_WIDTH), F32)
        dv_acc[0:tq, :] = jnp.zeros((tq, KV_WIDTH), F32)
        dk_acc[tq:, :] = dk_carry[...]
        dv_acc[tq:, :] = dv_carry[...]
        first, band = _band_masks(nt - 1 - step)
        low = lax.broadcasted_iota(jnp.int32, (2 * BLOCK, LANES), 1) < HEAD_DIM
        for b in range(per):
            rows = slice(b * BLOCK, (b + 1) * BLOCK)
            mask = first if b == 0 else band
            dk_heads, dv_heads = [], []
            for kv in range(N_KV_HEADS):
                qab = _stack_pair(q_ref, rows, kv)
                doab = _stack_pair(do_ref, rows, kv)
                kzb = _band(k_ref, kp_ref, b, kv)
                sc = _nt(qab, kzb)
                dp = _nt(doab, _band(v_ref, vp_ref, b, kv))
                probs, dscores = [], []
                for half in range(2):
                    top, bottom = _pair_heads(kv, half)
                    cols = slice(half * 2 * BLOCK, (half + 1) * 2 * BLOCK)
                    lse_h = _per_row(lse_ref, rows, top, bottom)
                    p = jnp.where(mask[:, cols], jnp.exp(sc[:, cols] - lse_h), 0.0)
                    dph = dp[:, cols]
                    delta = jnp.sum(p * dph, axis=-1, keepdims=True)
                    probs.append(p.astype(BF16))
                    dscores.append((p * (dph - delta)).astype(BF16))
                    leak = jnp.exp(_sink_rows(sink_ref, top, bottom) - lse_h) * delta
                    dsink_ref[:, top:top + 1] -= jnp.sum(leak[:BLOCK], axis=0, keepdims=True)
                    dsink_ref[:, bottom:bottom + 1] -= jnp.sum(leak[BLOCK:], axis=0, keepdims=True)
                ds = jnp.concatenate(dscores, axis=1)
                dqab = _nn(ds, kzb) * Q_SCALE
                dq_ref[rows, kv * PAIR:kv * PAIR + LANES] = dqab[:BLOCK]
                dq_ref[rows, kv * PAIR + LANES:(kv + 1) * PAIR] = dqab[BLOCK:]
                dk_heads.append(_fold_spread(_tn(ds, qab)))
                dv_heads.append(_fold_spread(_tn(jnp.concatenate(probs, axis=1), doab)))
            band_rows = slice(b * BLOCK, (b + 2) * BLOCK)
            dk_acc[band_rows, :] += jnp.where(low, dk_heads[0], dk_heads[1])
            dv_acc[band_rows, :] += jnp.where(low, dv_heads[0], dv_heads[1])
        dk_ref[...] = dk_acc[BLOCK:, :]
        dv_ref[...] = dv_acc[BLOCK:, :]
        dk_carry[...] = dk_acc[0:BLOCK, :]
        dv_carry[...] = dv_acc[0:BLOCK, :]

    cur = lambda w: pl.BlockSpec((tq, w), lambda i: (nt - 1 - i, 0))
    prev = pl.BlockSpec((BLOCK, KV_SPREAD), lambda i: (jnp.maximum(per * (nt - 1 - i) - 1, 0), 0))
    acc = pltpu.VMEM((tq + BLOCK, KV_WIDTH), F32)
    carry = pltpu.VMEM((BLOCK, KV_WIDTH), F32)
    return _gridded(
        body, rider, name="attn_bwd", grid=(nt,),
        in_specs=[cur(ATTN_WIDTH), cur(KV_SPREAD), prev, cur(KV_SPREAD), prev, cur(ATTN_WIDTH),
                  cur(N_Q_HEADS), _full((1, N_Q_HEADS))],
        out_specs=[cur(ATTN_WIDTH), cur(KV_WIDTH), cur(KV_WIDTH), _full((1, N_Q_HEADS))],
        out_shape=[jax.ShapeDtypeStruct((s, ATTN_WIDTH), F32), jax.ShapeDtypeStruct((s, KV_WIDTH), F32),
                   jax.ShapeDtypeStruct((s, KV_WIDTH), F32), jax.ShapeDtypeStruct((1, N_Q_HEADS), F32)],
        scratch_shapes=[acc, acc, carry, carry],
        args=(q, kz, kz, vz, vz, dattn, lse, sinks))


def _bwd_inproj(dq, dk, dv, du, cos, sin, win_t, x, g_mix, dx2):
    s = x.shape[0]
    tm = _token_tile(s)

    def body(dq_ref, dk_ref, dv_ref, du_ref, cos_ref, sin_ref, w_ref, x_ref, g_ref, dx2_ref,
             dx_ref, dw_ref, db_ref, dg_ref):
        @pl.when(pl.program_id(0) == 0)
        def _():
            dw_ref[...] = jnp.zeros_like(dw_ref)
            db_ref[...] = jnp.zeros_like(db_ref)
            dg_ref[...] = jnp.zeros_like(dg_ref)

        cos_t, sin_t = cos_ref[...], sin_ref[...]
        dz32 = jnp.concatenate([_rope_bwd(dq_ref[...], cos_t, sin_t), _rope_bwd(dk_ref[...], cos_t, sin_t),
                                dv_ref[...], du_ref[...].astype(F32)], axis=1)
        db_ref[...] += jnp.sum(dz32, axis=0, keepdims=True)
        dz = dz32.astype(BF16)
        g = g_ref[...]
        n, r = _rms(x_ref[...])
        h = (n * g).astype(BF16)
        for m0 in range(0, IN_WIDTH, TN_ROW_CHUNK):
            dw_ref[m0:m0 + TN_ROW_CHUNK, :] += _tn(dz[:, m0:m0 + TN_ROW_CHUNK], h)
        dx, dg = _rms_bwd(_nn(dz, w_ref[...]), n, r, g)
        dg_ref[...] += dg
        dx_ref[...] = dx2_ref[...] + dx

    return _gridded(
        body, None, name="bwd_inproj", grid=(s // tm,),
        in_specs=[_rows(tm, ATTN_WIDTH), _rows(tm, KV_WIDTH), _rows(tm, KV_WIDTH), _rows(tm, POOL_WIDTH),
                  _rows(tm, LANES), _rows(tm, LANES), _full((IN_WIDTH, D_MODEL)), _rows(tm, D_MODEL),
                  _full((1, D_MODEL)), _rows(tm, D_MODEL)],
        out_specs=[_rows(tm, D_MODEL), _full((IN_WIDTH, D_MODEL)), _full((1, IN_WIDTH)), _full((1, D_MODEL))],
        out_shape=[jax.ShapeDtypeStruct((s, D_MODEL), F32), jax.ShapeDtypeStruct((IN_WIDTH, D_MODEL), F32),
                   jax.ShapeDtypeStruct((1, IN_WIDTH), F32), jax.ShapeDtypeStruct((1, D_MODEL), F32)],
        scratch_shapes=[], args=(dq, dk, dv, du, cos, sin, win_t, x, g_mix, dx2))


def _rope_tables(s):
    inv_freq = 1.0 / (ROPE_THETA ** (jnp.arange(0, HEAD_DIM, 2, dtype=F32) / HEAD_DIM))
    ang = jnp.arange(s, dtype=F32)[:, None] * inv_freq[None, :]
    cos, sin = jnp.cos(ang), jnp.sin(ang)
    return jnp.tile(cos, (1, 4)), jnp.tile(jnp.concatenate([-sin, sin], axis=1), (1, 2))


def _place():
    return lax.axis_index("x"), lax.axis_index("y"), lax.axis_index("c")


def _other_chips(x, y):
    return [(1 - x, y), (x, 1 - y), (1 - x, 1 - y)]


def _gather_rider(blocks):
    nm = len(blocks)

    def plan(ins, outs, sems):
        send_sems, recv_sems, local_sems = sems
        x, y, c = _place()
        me, sibling = (x, y, c), (x, y, 1 - c)
        chips = _other_chips(x, y)

        def rows(m, px, py, pc):
            r = ins[m].shape[0]
            return outs[m].at[pl.ds((4 * px + 2 * py + pc) * r, r), :]

        def copy(m, k, block, to, src=None):
            return pltpu.make_async_remote_copy(
                src_ref=rows(m, *block) if src is None else src, dst_ref=rows(m, *block),
                send_sem=send_sems.at[k * nm + m], recv_sem=recv_sems.at[k * nm + m],
                device_id=to, device_id_type=MESH)

        mine = [pltpu.make_async_copy(ins[m], rows(m, *me), local_sems.at[m]) for m in range(nm)]
        first = [copy(m, 0, me, sibling, src=ins[m]) for m in range(nm)]
        first += [copy(m, 1 + j, me, (*chip, c), src=ins[m]) for j, chip in enumerate(chips) for m in range(nm)]
        return me, sibling, chips, copy, mine, first

    def start(ins, outs, sems):
        *_, mine, first = plan(ins, outs, sems)
        for cp in mine + first:
            cp.start()

    def finish(ins, outs, sems):
        me, sibling, chips, copy, mine, first = plan(ins, outs, sems)
        c = me[2]
        passed = []
        for j, chip in enumerate(chips):
            for m in range(nm):
                copy(m, 1 + j, (*chip, c), me).wait_recv()
                passed.append(copy(m, 4 + j, (*chip, c), sibling))
                passed[-1].start()
        for m in range(nm):
            copy(m, 0, sibling, me).wait_recv()
        for j, chip in enumerate(chips):
            for m in range(nm):
                copy(m, 4 + j, (*chip, 1 - c), me).wait_recv()
        for cp in first + passed:
            cp.wait_send()
        for cp in mine:
            cp.wait()

    return _Rider(
        arrays=list(blocks), out_shape=[jax.ShapeDtypeStruct((N_DEV * b.shape[0], b.shape[1]), b.dtype) for b in blocks],
        sems=[pltpu.SemaphoreType.DMA((7 * nm,)), pltpu.SemaphoreType.DMA((7 * nm,)), pltpu.SemaphoreType.DMA((nm,))],
        start=start, finish=finish)


def _exchange_rider(copies_of, arrays, out_shape, n_copies):
    def copies(ins, outs, sems):
        send_sems, recv_sems = sems
        return [pltpu.make_async_remote_copy(src_ref=src, dst_ref=dst, send_sem=send_sems.at[k], recv_sem=recv_sems.at[k],
                                             device_id=to, device_id_type=MESH)
                for k, (src, dst, to) in enumerate(copies_of(ins, outs))]

    def start(ins, outs, sems):
        for cp in copies(ins, outs, sems):
            cp.start()

    def finish(ins, outs, sems):
        cps = copies(ins, outs, sems)
        for cp in cps:
            cp.wait_recv()
        for cp in cps:
            cp.wait_send()

    return _Rider(arrays=list(arrays), out_shape=out_shape,
                  sems=[pltpu.SemaphoreType.DMA((n_copies,)), pltpu.SemaphoreType.DMA((n_copies,))], start=start, finish=finish)


def _sibling_rider(grads):
    def copies_of(ins, outs):
        x, y, c = _place()
        for g_ref, o_ref in zip(ins, outs):
            r = g_ref.shape[0] // N_DEV
            for q in range(N_CHIPS):
                yield g_ref.at[pl.ds((2 * q + 1 - c) * r, r), :], o_ref.at[pl.ds(q * r, r), :], (x, y, 1 - c)

    return _exchange_rider(copies_of, grads, [jax.ShapeDtypeStruct((g.shape[0] // 2, g.shape[1]), F32) for g in grads],
                           len(grads) * N_CHIPS)


def _chip_sum(grad, from_sibling, place):
    r = grad.shape[0] // N_DEV
    w = grad.shape[1]

    def body(place_ref, g_ref, s_ref, wire_ref, own_ref):
        total = g_ref[...] + s_ref[...]
        wire_ref[...] = total.astype(BF16)

        @pl.when(pl.program_id(0) == place_ref[1])
        def _():
            own_ref[...] = total

    grid_spec = pltpu.PrefetchScalarGridSpec(
        num_scalar_prefetch=1, grid=(N_CHIPS,),
        in_specs=[pl.BlockSpec((r, w), lambda q, p: (2 * q + p[0], 0)), pl.BlockSpec((r, w), lambda q, p: (q, 0))],
        out_specs=[pl.BlockSpec((r, w), lambda q, p: (q, 0)), pl.BlockSpec((r, w), lambda q, p: (0, 0))])
    return pl.pallas_call(
        body, name="grad_chip_sum", grid_spec=grid_spec,
        out_shape=[jax.ShapeDtypeStruct((N_CHIPS * r, w), BF16), jax.ShapeDtypeStruct((r, w), F32)],
        compiler_params=_params("arbitrary"),
    )(place, grad, from_sibling)


def _chips_rider(wires):
    def copies_of(ins, outs):
        x, y, c = _place()
        for w_ref, o_ref in zip(ins, outs):
            r = w_ref.shape[0] // N_CHIPS
            for j, (px, py) in enumerate(_other_chips(x, y)):
                yield w_ref.at[pl.ds((2 * px + py) * r, r), :], o_ref.at[pl.ds(j * r, r), :], (px, py, c)

    return _exchange_rider(copies_of, wires,
                           [jax.ShapeDtypeStruct((3 * (w.shape[0] // N_CHIPS), w.shape[1]), BF16) for w in wires], len(wires) * 3)


def _adamw_math(w, g, m, v):
    m = ADAM_B1 * m + (1.0 - ADAM_B1) * g
    v = ADAM_B2 * v + (1.0 - ADAM_B2) * jnp.square(g)
    m_hat = m / (1.0 - ADAM_B1 ** ADAM_STEP)
    v_hat = v / (1.0 - ADAM_B2 ** ADAM_STEP)
    delta = -ADAM_LR * (m_hat / (jnp.sqrt(v_hat) + ADAM_EPS) + ADAM_WD * w)
    return delta, m, v


def _reduce_adamw(own, received, w, m, v):
    r = own.shape[0]

    def body(own_ref, rec_ref, w_ref, m_ref, v_ref, g_ref, d_ref, nm_ref, nv_ref):
        g = ((own_ref[...] + rec_ref[0:r, :].astype(F32)) + rec_ref[r:2 * r, :].astype(F32)) + rec_ref[2 * r:, :].astype(F32)
        g_ref[...] = g
        d_ref[...], nm_ref[...], nv_ref[...] = _adamw_math(w_ref[...], g, m_ref[...], v_ref[...])

    shape = jax.ShapeDtypeStruct(own.shape, F32)
    return pl.pallas_call(
        body, name="reduce_adamw", in_specs=[VMEM] * 5, out_specs=[VMEM] * 4, out_shape=[shape] * 4,
        compiler_params=_params(),
    )(own, received, w, m, v)


SMALL = (("g_mix", 1024), ("b_in", 1280), ("sinks", 8), ("w_pool", 65536), ("b_pool", 512), ("pool_scale", 512),
         ("b_out", 1024), ("g_ffn", 1024), ("g_final", 1024), ("loss", 1024))


def _small_rows(size):
    return -(-size // (8 * LANES)) * 8


SMALL_ROWS = sum(_small_rows(size) for _, size in SMALL)


def _pack_small(values):
    parts = []
    for name, size in SMALL:
        flat = values[name].reshape(-1).astype(F32)
        parts.append(jnp.pad(flat, (0, _small_rows(size) * LANES - size)).reshape(-1, LANES))
    return jnp.concatenate(parts, axis=0)


def _unpack_small(packed, shapes):
    out, row = {}, 0
    for name, size in SMALL:
        rows = _small_rows(size)
        if name in shapes:
            out[name] = packed[row:row + rows].reshape(-1)[:size].reshape(shapes[name])
        row += rows
    return out


def _small_allreduce_adamw(part, w, m, v):
    rows_n = SMALL_ROWS

    def body(p_ref, w_ref, m_ref, v_ref, g_ref, d_ref, nm_ref, nv_ref, all_ref, send_sems, recv_sems, local_sem):
        x, y, c = _place()
        me, sibling = (x, y, c), (x, y, 1 - c)
        chips = _other_chips(x, y)

        def rows(px, py, pc):
            return all_ref.at[pl.ds((4 * px + 2 * py + pc) * rows_n, rows_n), :]

        def copy(k, block, to, src=None):
            return pltpu.make_async_remote_copy(
                src_ref=rows(*block) if src is None else src, dst_ref=rows(*block),
                send_sem=send_sems.at[k], recv_sem=recv_sems.at[k], device_id=to, device_id_type=MESH)

        mine = pltpu.make_async_copy(p_ref, rows(*me), local_sem)
        mine.start()
        first = [copy(0, me, sibling, src=p_ref)]
        first += [copy(1 + j, me, (*chip, c), src=p_ref) for j, chip in enumerate(chips)]
        for cp in first:
            cp.start()
        passed = [copy(4 + j, (*chip, c), sibling) for j, chip in enumerate(chips)]
        for j, chip in enumerate(chips):
            copy(1 + j, (*chip, c), me).wait_recv()
            passed[j].start()
        copy(0, sibling, me).wait_recv()
        for j, chip in enumerate(chips):
            copy(4 + j, (*chip, 1 - c), me).wait_recv()
        for cp in first + passed:
            cp.wait_send()
        mine.wait()
        total = all_ref[0:rows_n, :]
        for dev in range(1, N_DEV):
            total = total + all_ref[dev * rows_n:(dev + 1) * rows_n, :]
        g_ref[...] = total
        d_ref[...], nm_ref[...], nv_ref[...] = _adamw_math(w_ref[...], total, m_ref[...], v_ref[...])

    shape = jax.ShapeDtypeStruct((rows_n, LANES), F32)
    return pl.pallas_call(
        body, name="small_allreduce_adamw", in_specs=[VMEM] * 4, out_specs=[VMEM] * 4, out_shape=[shape] * 4,
        scratch_shapes=[pltpu.VMEM((N_DEV * rows_n, LANES), F32), pltpu.SemaphoreType.DMA((7,)), pltpu.SemaphoreType.DMA((7,)),
                        pltpu.SemaphoreType.DMA],
        compiler_params=pltpu.CompilerParams(has_side_effects=True, vmem_limit_bytes=VMEM_LIMIT_BYTES),
    )(part, w, m, v)


def kernel(x, g_mix, w_in, b_in, sinks, w_pool, b_pool, pool_scale, w_out, b_out, g_ffn, w_gate, w_up, w_down, g_final, loss_target, m_g_mix, m_w_in, m_b_in, m_sinks, m_w_pool, m_b_pool, m_pool_scale, m_w_out, m_b_out, m_g_ffn, m_w_gate, m_w_up, m_w_down, m_g_final, v_g_mix, v_w_in, v_b_in, v_sinks, v_w_pool, v_b_pool, v_pool_scale, v_w_out, v_b_out, v_g_ffn, v_w_gate, v_w_up, v_w_down, v_g_final):
    weights = dict(g_mix=g_mix, w_in=w_in, b_in=b_in, sinks=sinks, w_pool=w_pool, b_pool=b_pool, pool_scale=pool_scale,
                   w_out=w_out, b_out=b_out, g_ffn=g_ffn, w_gate=w_gate, w_up=w_up, w_down=w_down, g_final=g_final)
    mom1 = dict(g_mix=m_g_mix, w_in=m_w_in, b_in=m_b_in, sinks=m_sinks, w_pool=m_w_pool, b_pool=m_b_pool,
                pool_scale=m_pool_scale, w_out=m_w_out, b_out=m_b_out, g_ffn=m_g_ffn, w_gate=m_w_gate, w_up=m_w_up,
                w_down=m_w_down, g_final=m_g_final)
    mom2 = dict(g_mix=v_g_mix, w_in=v_w_in, b_in=v_b_in, sinks=v_sinks, w_pool=v_w_pool, b_pool=v_b_pool,
                pool_scale=v_pool_scale, w_out=v_w_out, b_out=v_b_out, g_ffn=v_g_ffn, w_gate=v_w_gate, w_up=v_w_up,
                w_down=v_w_down, g_final=v_g_final)
    order = ("g_mix", "w_in", "b_in", "sinks", "w_pool", "b_pool", "pool_scale", "w_out", "b_out", "g_ffn",
             "w_gate", "w_up", "w_down", "g_final")
    big = ("w_in", "w_out", "w_gate", "w_up", "w_down")
    transposed = ("w_in", "w_gate", "w_up")

    def row_shard(name, a):
        return a[0].T if name in transposed else a[0]

    shard = {n: row_shard(n, weights[n]).astype(BF16) for n in big}
    xs, target = x[0], loss_target[0]
    cos, sin = _rope_tables(xs.shape[0])
    wp_b = w_pool[0].astype(BF16)
    bp = b_pool.reshape(1, POOL_WIDTH)
    ps = pool_scale.reshape(1, POOL_WIDTH)
    g_fin = g_final.reshape(1, D_MODEL)
    px, py, pc = _place()
    place = jnp.stack([pc, 2 * px + py]).astype(jnp.int32)

    (win_t,) = _alone(_gather_rider([shard["w_in"]]), "gather_w_in")
    q, kz, vz, mixed, pool, w_out_b, wg_t = _fwd_inproj(
        xs, g_mix, win_t, b_in, cos, sin, wp_b, bp, ps, rider=_gather_rider([shard["w_out"], shard["w_gate"]]))
    attn, lse, wu_t, wd = _attn_fwd(q, kz, vz, sinks, rider=_gather_rider([shard["w_up"], shard["w_down"]]))
    x2, h2 = _fwd_outproj(attn, pool, w_out_b, b_out, xs, g_ffn)
    gate, up, act = _fwd_ffn_act(h2, wg_t, wu_t)
    dx3, dx3b, sq, dg_final = _fwd_down_loss(act, x2, wd, g_fin, target)

    dgate, dup = _bwd_ffn_act(dx3b, gate, up, wd)
    dx2, dx2b, dg_ffn, db_out = _bwd_ffn_in(dgate, dup, wg_t, wu_t, x2, dx3, g_ffn)
    ffn_grads = [_weight_grad(dgate, h2, "grad_w_gate"), _weight_grad(dup, h2, "grad_w_up"),
                 _weight_grad(act, dx3b, "grad_w_down")]
    dattn, du, d_wout, d_wpool, d_bpool, d_pscale, *ffn_sibling = _bwd_outproj_pool(
        dx2b, attn, pool, mixed, w_out_b, wp_b, bp, ps, rider=_sibling_rider(ffn_grads))
    ffn_sums = [_chip_sum(g, s, place) for g, s in zip(ffn_grads, ffn_sibling)]
    dq, dk, dv, d_sinks, *landed = _attn_bwd(
        q, kz, vz, dattn, lse, sinks, rider=_join(_chips_rider([wire for wire, _ in ffn_sums]), _sibling_rider([d_wout])))
    ffn_received, wout_sibling = landed[:3], landed[3]
    wout_sum = _chip_sum(d_wout, wout_sibling, place)
    dx, d_win_t, d_bin, d_gmix = _bwd_inproj(dq, dk, dv, du, cos, sin, win_t, xs, g_mix, dx2)
    (win_sibling,) = _alone(_sibling_rider([d_win_t]), "grad_exchange_sibling")
    win_sum = _chip_sum(d_win_t, win_sibling, place)
    wout_received, win_received = _alone(_chips_rider([wout_sum[0], win_sum[0]]), "grad_exchange_chips")
    small = dict(g_mix=d_gmix, b_in=d_bin, sinks=d_sinks, w_pool=d_wpool, b_pool=d_bpool, pool_scale=d_pscale,
                 b_out=db_out, g_ffn=dg_ffn, g_final=dg_final, loss=sq)

    reduced = dict(zip(("w_gate", "w_up", "w_down", "w_out", "w_in"),
                       zip(ffn_sums + [wout_sum, win_sum], list(ffn_received) + [wout_received, win_received])))
    grad, delta, new_m, new_v = {}, {}, {}, {}
    for n in big:
        (_, own), rec = reduced[n]
        results = _reduce_adamw(own, rec, row_shard(n, weights[n]), row_shard(n, mom1[n]), row_shard(n, mom2[n]))
        grad[n], delta[n], new_m[n], new_v[n] = [(a.T if n in transposed else a)[None] for a in results]

    shapes = {n: weights[n].shape for n in order if n not in big}
    zero_loss = jnp.zeros((1, D_MODEL), F32)
    packed = _small_allreduce_adamw(
        _pack_small(small), _pack_small({**weights, "loss": zero_loss}),
        _pack_small({**mom1, "loss": zero_loss}), _pack_small({**mom2, "loss": zero_loss}))
    for store, pk in zip((grad, delta, new_m, new_v), packed):
        store.update(_unpack_small(pk, shapes))
    loss_rows = _unpack_small(packed[0], {"loss": (D_MODEL,)})["loss"]
    loss = (0.5 / D_MODEL) * jnp.sum(loss_rows)

    return (loss, dx[None], *[grad[n] for n in order], *[delta[n] for n in order],
            *[new_m[n] for n in order], *[new_v[n] for n in order])
```

```python
from typing import Any, Callable, NamedTuple, Sequence

import jax
import jax.numpy as jnp
from jax import lax
from jax.experimental import pallas as pl
from jax.experimental.pallas import tpu as pltpu

D_MODEL = 1024
ATTN_WIDTH = 512
KV_WIDTH = 128
POOL_WIDTH = 512
HEAD_DIM = 64
N_Q_HEADS = 8
N_KV_HEADS = 2
GQA_GROUP = 4
BLOCK = 128
POOL_SIZES = (2, 4, 8, 16)
POOL_GROUP_WIDTH = 128
POOL_HALO = 16
IN_WIDTH = 1280
D_FF = 2816
RMS_EPS = 1e-5
ROPE_THETA = 10000.0
Q_SCALE = HEAD_DIM ** -0.5

ADAM_LR = 0.001
ADAM_B1 = 0.9
ADAM_B2 = 0.999
ADAM_EPS = 1e-08
ADAM_WD = 0.01
ADAM_STEP = 10

N_DEV = 8
N_CHIPS = 4
LANES = 128
VMEM_LIMIT_BYTES = 56 * 1024 * 1024

F32 = jnp.float32
BF16 = jnp.bfloat16
MESH = pl.DeviceIdType.MESH
HBM = pl.BlockSpec(memory_space=pltpu.HBM)
VMEM = pl.BlockSpec(memory_space=pltpu.VMEM)


def _params(*semantics):
    return pltpu.CompilerParams(dimension_semantics=semantics or None, vmem_limit_bytes=VMEM_LIMIT_BYTES)


def _nn(a, b):
    return jnp.dot(a, b, preferred_element_type=F32)


def _nt(a, b):
    return lax.dot_general(a, b, (((1,), (1,)), ((), ())), preferred_element_type=F32)


def _tn(a, b):
    return lax.dot_general(a, b, (((0,), (0,)), ((), ())), preferred_element_type=F32)


def _full(shape):
    return pl.BlockSpec(shape, lambda *_: (0,) * len(shape))


def _rows(tm, width):
    return pl.BlockSpec((tm, width), lambda i, *_: (i, 0))


class _Rider(NamedTuple):
    arrays: Sequence[Any]
    out_shape: Sequence[Any]
    sems: Sequence[Any]
    start: Callable[..., None]
    finish: Callable[..., None]


def _gridded(body, rider, *, name, grid, in_specs, out_specs, out_shape, scratch_shapes, args):
    params = _params("arbitrary")
    if rider is None:
        return pl.pallas_call(body, name=name, grid=grid, in_specs=in_specs, out_specs=out_specs, out_shape=out_shape,
                              scratch_shapes=scratch_shapes, compiler_params=params)(*args)
    bounds, total = [], 0
    for n in (len(in_specs), len(rider.arrays), len(out_specs), len(rider.out_shape), len(scratch_shapes), len(rider.sems)):
        bounds.append((total, total + n))
        total += n
    last = grid[0] - 1

    def riding(*refs):
        ins, r_ins, outs, r_outs, scratch, r_sems = (refs[lo:hi] for lo, hi in bounds)

        @pl.when(pl.program_id(0) == 0)
        def _():
            rider.start(r_ins, r_outs, r_sems)

        body(*ins, *outs, *scratch)

        @pl.when(pl.program_id(0) == last)
        def _():
            rider.finish(r_ins, r_outs, r_sems)

    return pl.pallas_call(
        riding, name=name, grid=grid, in_specs=list(in_specs) + [HBM] * len(rider.arrays),
        out_specs=list(out_specs) + [HBM] * len(rider.out_shape), out_shape=list(out_shape) + list(rider.out_shape),
        scratch_shapes=list(scratch_shapes) + list(rider.sems), compiler_params=params)(*args, *rider.arrays)


def _join(*riders):
    def phase(which):
        def run(ins, outs, sems):
            i = o = s = 0
            for r in riders:
                ni, no, ns = len(r.arrays), len(r.out_shape), len(r.sems)
                getattr(r, which)(ins[i:i + ni], outs[o:o + no], sems[s:s + ns])
                i, o, s = i + ni, o + no, s + ns
        return run

    return _Rider(arrays=[a for r in riders for a in r.arrays], out_shape=[a for r in riders for a in r.out_shape],
                  sems=[a for r in riders for a in r.sems], start=phase("start"), finish=phase("finish"))


def _alone(rider, name):
    n_in, n_out = len(rider.arrays), len(rider.out_shape)

    def body(*refs):
        parts = refs[:n_in], refs[n_in:n_in + n_out], refs[n_in + n_out:]
        rider.start(*parts)
        rider.finish(*parts)

    return pl.pallas_call(body, name=name, in_specs=[HBM] * n_in, out_specs=[HBM] * n_out, out_shape=list(rider.out_shape),
                          scratch_shapes=list(rider.sems))(*rider.arrays)


def _rot_half(t):
    n = t.shape[1]
    lane = lax.broadcasted_iota(jnp.int32, t.shape, 1)
    return jnp.where((lane % HEAD_DIM) < HEAD_DIM // 2, pltpu.roll(t, n - HEAD_DIM // 2, 1), pltpu.roll(t, HEAD_DIM // 2, 1))


def _rope(t, cos, sin):
    reps = t.shape[1] // LANES
    if reps > 1:
        cos, sin = jnp.tile(cos, (1, reps)), jnp.tile(sin, (1, reps))
    return t * cos + _rot_half(t) * sin


def _rope_bwd(d, cos, sin):
    reps = d.shape[1] // LANES
    if reps > 1:
        cos, sin = jnp.tile(cos, (1, reps)), jnp.tile(sin, (1, reps))
    return d * cos + _rot_half(d * sin)


KV_SPREAD = 4 * LANES


def _spread_kv(t):
    low = lax.broadcasted_iota(jnp.int32, t.shape, 1) < HEAD_DIM
    swapped = pltpu.roll(t, HEAD_DIM, 1)
    zero = jnp.zeros_like(t)
    return jnp.concatenate([jnp.where(low, t, zero), jnp.where(low, zero, swapped),
                            jnp.where(low, swapped, zero), jnp.where(low, zero, t)], axis=1)


def _rms(x):
    r = lax.rsqrt(jnp.mean(x * x, axis=-1, keepdims=True) + RMS_EPS)
    return x * r, r


def _rms_bwd(dh, n, r, g):
    dn = dh * g
    dx = r * (dn - n * jnp.mean(dn * n, axis=-1, keepdims=True))
    return dx, jnp.sum(dh * n, axis=0, keepdims=True)


def _token_tile(s):
    return min(512, s)


def _fwd_inproj(x, g_mix, win_t, b_in, cos, sin, w_pool, b_pool, pool_scale, rider=None):
    s = x.shape[0]
    tm = _token_tile(s)

    def body(x_ref, g_ref, w_ref, b_ref, cos_ref, sin_ref, wp_ref, bp_ref, ps_ref,
             q_ref, k_ref, v_ref, vt_ref, mix_ref, pool_ref, tail_ref):
        i = pl.program_id(0)

        @pl.when(i == 0)
        def _():
            tail_ref[...] = jnp.zeros_like(tail_ref)

        n, _ = _rms(x_ref[...])
        h = (n * g_ref[...]).astype(BF16)
        z = _nt(h, w_ref[...]) + b_ref[...]
        cos_t, sin_t = cos_ref[...], sin_ref[...]
        q_ref[...] = (_rope(z[:, :ATTN_WIDTH], cos_t, sin_t) * Q_SCALE).astype(BF16)
        k_ref[...] = _spread_kv(_rope(z[:, ATTN_WIDTH:ATTN_WIDTH + KV_WIDTH], cos_t, sin_t)).astype(BF16)
        vz = _spread_kv(z[:, ATTN_WIDTH + KV_WIDTH:ATTN_WIDTH + 2 * KV_WIDTH])
        v_ref[...] = vz.astype(BF16)
        vt_ref[...] = vz.T.astype(BF16)
        u = z[:, ATTN_WIDTH + 2 * KV_WIDTH:]
        u_ext = jnp.concatenate([tail_ref[...], u], axis=0)
        tail_ref[...] = u[tm - POOL_HALO:, :]
        pos = lax.broadcasted_iota(jnp.int32, (tm, POOL_GROUP_WIDTH), 0) + i * tm
        for g, size in enumerate(POOL_SIZES):
            cols = slice(g * POOL_GROUP_WIDTH, (g + 1) * POOL_GROUP_WIDTH)
            a = u_ext[:, cols]
            shift = 1
            while shift < size:
                a = a + pltpu.roll(a, shift, 0)
                shift *= 2
            count = jnp.minimum(pos + 1, size).astype(F32)
            mixed = (a[POOL_HALO:, :] / count - u[:, cols]).astype(BF16)
            pre = _nn(mixed, wp_ref[g]) + bp_ref[:, cols]
            mix_ref[:, cols] = mixed
            pool_ref[:, cols] = (pre * ps_ref[:, cols]).astype(BF16)

    bf = lambda w: jax.ShapeDtypeStruct((s, w), BF16)
    return _gridded(
        body, rider, name="fwd_inproj", grid=(s // tm,),
        in_specs=[_rows(tm, D_MODEL), _full((1, D_MODEL)), _full((IN_WIDTH, D_MODEL)), _full((1, IN_WIDTH)),
                  _rows(tm, LANES), _rows(tm, LANES), _full((4, POOL_GROUP_WIDTH, POOL_GROUP_WIDTH)),
                  _full((1, POOL_WIDTH)), _full((1, POOL_WIDTH))],
        out_specs=[_rows(tm, ATTN_WIDTH), _rows(tm, KV_SPREAD), _rows(tm, KV_SPREAD),
                   pl.BlockSpec((KV_SPREAD, tm), lambda i: (0, i)), _rows(tm, POOL_WIDTH), _rows(tm, POOL_WIDTH)],
        out_shape=[bf(ATTN_WIDTH), bf(KV_SPREAD), bf(KV_SPREAD), jax.ShapeDtypeStruct((KV_SPREAD, s), BF16),
                   bf(POOL_WIDTH), bf(POOL_WIDTH)],
        scratch_shapes=[pltpu.VMEM((POOL_HALO, POOL_WIDTH), F32)],
        args=(x, g_mix, win_t, b_in, cos, sin, w_pool, b_pool, pool_scale))


ATTN_TILE = 512
PAIR = 2 * LANES


def _band_masks(tile):
    j = lax.broadcasted_iota(jnp.int32, (4 * BLOCK, 2 * BLOCK), 0) % (2 * BLOCK)
    r = lax.broadcasted_iota(jnp.int32, (4 * BLOCK, 2 * BLOCK), 1) % BLOCK
    band = (j > r) & (j <= r + BLOCK)
    return band & ((tile > 0) | (j >= BLOCK)), band


def _band(cur_ref, prev_ref, b, kv):
    halves = []
    for half in range(2):
        cols = slice(kv * PAIR + half * LANES, kv * PAIR + (half + 1) * LANES)
        if b == 0:
            halves.append(jnp.concatenate([prev_ref[:, cols], cur_ref[0:BLOCK, cols]], axis=0))
        else:
            halves.append(cur_ref[(b - 1) * BLOCK:(b + 1) * BLOCK, cols])
    return jnp.concatenate(halves, axis=0)


def _stack_pair(ref, rows, kv):
    return jnp.concatenate([ref[rows, kv * PAIR:kv * PAIR + LANES], ref[rows, kv * PAIR + LANES:(kv + 1) * PAIR]], axis=0)


def _pair_heads(kv, half):
    return GQA_GROUP * kv + half, GQA_GROUP * kv + 2 + half


def _band_t(cur_ref, prev_ref, b, kv):
    halves = []
    for half in range(2):
        lanes = slice(kv * PAIR + half * LANES, kv * PAIR + (half + 1) * LANES)
        if b == 0:
            halves.append(jnp.concatenate([prev_ref[lanes, :], cur_ref[lanes, 0:BLOCK]], axis=1))
        else:
            halves.append(cur_ref[lanes, (b - 1) * BLOCK:(b + 1) * BLOCK])
    return jnp.concatenate(halves, axis=1)


def _reduce_rows(x, op, reduce):
    while x.shape[0] > 8:
        half = x.shape[0] // 2
        x = op(x[:half], x[half:])
    return reduce(x, axis=0, keepdims=True)


def _per_query(ref, rows, top, bottom):
    return jnp.concatenate([ref[top:top + 1, rows], ref[bottom:bottom + 1, rows]], axis=1)


def _sink_per_query(sink_ref, top, bottom):
    first_slab = lax.broadcasted_iota(jnp.int32, (1, 2 * BLOCK), 1) < BLOCK
    return jnp.where(first_slab, sink_ref[:, top:top + 1], sink_ref[:, bottom:bottom + 1])


def _attn_fwd(q, kz, vt, sinks, rider=None):
    s = q.shape[0]
    tq = min(ATTN_TILE, s)

    def body(q_ref, k_ref, kp_ref, vt_ref, vtp_ref, sink_ref, o_ref, lse_ref):
        first, band = _band_masks(pl.program_id(0))
        chains = [(b, kv) for b in range(tq // BLOCK) for kv in range(N_KV_HEADS)]

        def scores(b, kv):
            rows = slice(b * BLOCK, (b + 1) * BLOCK)
            return _nt(_band(k_ref, kp_ref, b, kv), _stack_pair(q_ref, rows, kv))

        def store(b, kv, ot):
            rows = slice(b * BLOCK, (b + 1) * BLOCK)
            o = ot.T.astype(BF16)
            o_ref[rows, kv * PAIR:kv * PAIR + LANES] = o[:BLOCK]
            o_ref[rows, kv * PAIR + LANES:(kv + 1) * PAIR] = o[BLOCK:]

        ahead = scores(*chains[0])
        behind = None
        for n, (b, kv) in enumerate(chains):
            rows = slice(b * BLOCK, (b + 1) * BLOCK)
            st = jnp.where(first if b == 0 else band, ahead, -jnp.inf)
            if n + 1 < len(chains):
                ahead = scores(*chains[n + 1])
            probs = []
            for half in range(2):
                top, bottom = _pair_heads(kv, half)
                sink = _sink_per_query(sink_ref, top, bottom)
                sh = st[half * 2 * BLOCK:(half + 1) * 2 * BLOCK, :]
                m = jnp.maximum(_reduce_rows(sh, jnp.maximum, jnp.max), sink)
                p = jnp.exp(sh - m)
                denom = _reduce_rows(p, jnp.add, jnp.sum) + jnp.exp(sink - m)
                probs.append((p * (1.0 / denom)).astype(BF16))
                lse = m + jnp.log(denom)
                lse_ref[top:top + 1, rows] = lse[:, :BLOCK]
                lse_ref[bottom:bottom + 1, rows] = lse[:, BLOCK:]
            ot = _nn(_band_t(vt_ref, vtp_ref, b, kv), jnp.concatenate(probs, axis=0))
            if behind is not None:
                store(*behind)
            behind = (b, kv, ot)
        store(*behind)

    per = tq // BLOCK
    cur = lambda w: pl.BlockSpec((tq, w), lambda i: (i, 0))
    prev = pl.BlockSpec((BLOCK, KV_SPREAD), lambda i: (jnp.maximum(per * i - 1, 0), 0))
    cur_t = pl.BlockSpec((KV_SPREAD, tq), lambda i: (0, i))
    prev_t = pl.BlockSpec((KV_SPREAD, BLOCK), lambda i: (0, jnp.maximum(per * i - 1, 0)))
    return _gridded(
        body, rider, name="attn_fwd", grid=(s // tq,),
        in_specs=[cur(ATTN_WIDTH), cur(KV_SPREAD), prev, cur_t, prev_t, _full((1, N_Q_HEADS))],
        out_specs=[cur(ATTN_WIDTH), pl.BlockSpec((N_Q_HEADS, tq), lambda i: (0, i))],
        out_shape=[jax.ShapeDtypeStruct((s, ATTN_WIDTH), BF16), jax.ShapeDtypeStruct((N_Q_HEADS, s), F32)],
        scratch_shapes=[], args=(q, kz, kz, vt, vt, sinks))


FF_CHUNK = 256
TN_ROW_CHUNK = 256


def _resident(shape):
    return pl.BlockSpec(shape, lambda *_: (0,) * len(shape), pipeline_mode=pl.Buffered(1))


def _accumulate_tn(acc_ref, a_ref, b):
    for m0 in range(0, acc_ref.shape[0], TN_ROW_CHUNK):
        acc_ref[m0:m0 + TN_ROW_CHUNK, :] += _tn(a_ref[:, m0:m0 + TN_ROW_CHUNK], b)


def _fwd_outproj_ffn_act(attn, pool, w_out, b_out, x, g_ffn, wg_t, wu_t, rider=None):
    s = x.shape[0]
    tm = _token_tile(s)

    def body(a_ref, p_ref, w_ref, b_ref, x_ref, g_ref, wg_ref, wu_ref, x2_ref, h2_ref, gate_ref, up_ref, act_ref):
        x2 = x_ref[...] + _nn(a_ref[...], w_ref[:ATTN_WIDTH, :]) + _nn(p_ref[...], w_ref[ATTN_WIDTH:, :]) + b_ref[...]
        x2_ref[...] = x2
        n, _ = _rms(x2)
        h = (n * g_ref[...]).astype(BF16)
        h2_ref[...] = h

        def products(c0):
            return _nt(h, wg_ref[c0:c0 + FF_CHUNK, :]), _nt(h, wu_ref[c0:c0 + FF_CHUNK, :])

        ahead = products(0)
        for c0 in range(0, D_FF, FF_CHUNK):
            cols = slice(c0, c0 + FF_CHUNK)
            gate, up = ahead
            if c0 + FF_CHUNK < D_FF:
                ahead = products(c0 + FF_CHUNK)
            gate_ref[:, cols] = gate.astype(BF16)
            up_ref[:, cols] = up.astype(BF16)
            act_ref[:, cols] = (gate * jax.nn.sigmoid(gate) * up).astype(BF16)

    act_shape = jax.ShapeDtypeStruct((s, D_FF), BF16)
    return _gridded(
        body, rider, name="fwd_outproj_ffn_act", grid=(s // tm,),
        in_specs=[_rows(tm, ATTN_WIDTH), _rows(tm, POOL_WIDTH), _resident((D_MODEL, D_MODEL)), _full((1, D_MODEL)),
                  _rows(tm, D_MODEL), _full((1, D_MODEL)), _resident((D_FF, D_MODEL)), _resident((D_FF, D_MODEL))],
        out_specs=[_rows(tm, D_MODEL), _rows(tm, D_MODEL)] + [_rows(tm, D_FF)] * 3,
        out_shape=[jax.ShapeDtypeStruct((s, D_MODEL), F32), jax.ShapeDtypeStruct((s, D_MODEL), BF16)] + [act_shape] * 3,
        scratch_shapes=[], args=(attn, pool, w_out, b_out, x, g_ffn, wg_t, wu_t))


def _fwd_down_loss(act, x2, wd, g_final, target):
    s = x2.shape[0]
    tm = _token_tile(s)
    last = s // tm - 1

    def body(a_ref, x2_ref, wd_ref, g_ref, t_ref, dx3_ref, dx3b_ref, sq_ref, dg_ref, dwd_ref, acc_ref, sem):
        @pl.when(pl.program_id(0) == 0)
        def _():
            sq_ref[...] = jnp.zeros_like(sq_ref)
            dg_ref[...] = jnp.zeros_like(dg_ref)
            acc_ref[...] = jnp.zeros_like(acc_ref)

        x3 = x2_ref[...] + _nn(a_ref[...], wd_ref[...])
        n, r = _rms(x3)
        g = g_ref[...]
        diff = n * g - t_ref[...]
        sq_ref[...] += jnp.sum(diff * diff, axis=0, keepdims=True)
        dx3, dg = _rms_bwd(diff * (1.0 / D_MODEL), n, r, g)
        dg_ref[...] += dg
        dx3_ref[...] = dx3
        dx3b = dx3.astype(BF16)
        dx3b_ref[...] = dx3b
        _accumulate_tn(acc_ref, a_ref, dx3b)

        @pl.when(pl.program_id(0) == last)
        def _():
            out = pltpu.make_async_copy(acc_ref, dwd_ref, sem)
            out.start()
            out.wait()

    return pl.pallas_call(
        body, name="fwd_down_loss", grid=(s // tm,),
        in_specs=[_rows(tm, D_FF), _rows(tm, D_MODEL), _resident((D_FF, D_MODEL)), _full((1, D_MODEL)), _rows(tm, D_MODEL)],
        out_specs=[_rows(tm, D_MODEL), _rows(tm, D_MODEL), _full((1, D_MODEL)), _full((1, D_MODEL)), HBM],
        out_shape=[jax.ShapeDtypeStruct((s, D_MODEL), F32), jax.ShapeDtypeStruct((s, D_MODEL), BF16),
                   jax.ShapeDtypeStruct((1, D_MODEL), F32), jax.ShapeDtypeStruct((1, D_MODEL), F32),
                   jax.ShapeDtypeStruct((D_FF, D_MODEL), F32)],
        scratch_shapes=[pltpu.VMEM((D_FF, D_MODEL), F32), pltpu.SemaphoreType.DMA],
        compiler_params=_params("arbitrary"),
    )(act, x2, wd, g_final, target)


def _bwd_ffn_act(dx3b, gate, up, wd, rider=None):
    s = dx3b.shape[0]
    tm = _token_tile(s)

    def body(dx3_ref, gate_ref, up_ref, wd_ref, dgate_ref, dup_ref):
        dx3 = dx3_ref[...]
        ahead = _nt(dx3, wd_ref[0:FF_CHUNK, :])
        for c0 in range(0, D_FF, FF_CHUNK):
            cols = slice(c0, c0 + FF_CHUNK)
            dact = ahead
            if c0 + FF_CHUNK < D_FF:
                ahead = _nt(dx3, wd_ref[c0 + FF_CHUNK:c0 + 2 * FF_CHUNK, :])
            gate = gate_ref[:, cols].astype(F32)
            up = up_ref[:, cols].astype(F32)
            sig = jax.nn.sigmoid(gate)
            silu = gate * sig
            dup_ref[:, cols] = (dact * silu).astype(BF16)
            dgate_ref[:, cols] = (dact * up * (sig + silu * (1.0 - sig))).astype(BF16)

    act_shape = jax.ShapeDtypeStruct((s, D_FF), BF16)
    return _gridded(
        body, rider, name="bwd_ffn_act", grid=(s // tm,),
        in_specs=[_rows(tm, D_MODEL), _rows(tm, D_FF), _rows(tm, D_FF), _resident((D_FF, D_MODEL))],
        out_specs=[_rows(tm, D_FF)] * 2, out_shape=[act_shape] * 2,
        scratch_shapes=[], args=(dx3b, gate, up, wd))


def _bwd_ffn_in(dgate, dup, wg_t, wu_t, x2, dx3, g_ffn, rider=None):
    s = x2.shape[0]
    tm = _token_tile(s)

    def body(dgate_ref, dup_ref, wg_ref, wu_ref, x2_ref, dx3_ref, g_ref, dx2_ref, dx2b_ref, dg_ref, db_ref):
        @pl.when(pl.program_id(0) == 0)
        def _():
            dg_ref[...] = jnp.zeros_like(dg_ref)
            db_ref[...] = jnp.zeros_like(db_ref)

        dh2 = _nn(dgate_ref[...], wg_ref[...]) + _nn(dup_ref[...], wu_ref[...])
        n, r = _rms(x2_ref[...])
        dx, dg = _rms_bwd(dh2, n, r, g_ref[...])
        dx2 = dx3_ref[...] + dx
        dg_ref[...] += dg
        db_ref[...] += jnp.sum(dx2, axis=0, keepdims=True)
        dx2_ref[...] = dx2
        dx2b_ref[...] = dx2.astype(BF16)

    return _gridded(
        body, rider, name="bwd_ffn_in", grid=(s // tm,),
        in_specs=[_rows(tm, D_FF), _rows(tm, D_FF), _resident((D_FF, D_MODEL)), _resident((D_FF, D_MODEL)),
                  _rows(tm, D_MODEL), _rows(tm, D_MODEL), _full((1, D_MODEL))],
        out_specs=[_rows(tm, D_MODEL), _rows(tm, D_MODEL), _full((1, D_MODEL)), _full((1, D_MODEL))],
        out_shape=[jax.ShapeDtypeStruct((s, D_MODEL), F32), jax.ShapeDtypeStruct((s, D_MODEL), BF16),
                   jax.ShapeDtypeStruct((1, D_MODEL), F32), jax.ShapeDtypeStruct((1, D_MODEL), F32)],
        scratch_shapes=[], args=(dgate, dup, wg_t, wu_t, x2, dx3, g_ffn))


def _ffn_in_weight_grads(dgate, dup, h2):
    s = h2.shape[0]
    tm = _token_tile(s)
    last = s // tm - 1

    def body(dg_ref, du_ref, h_ref, dwg_ref, dwu_ref, accg_ref, accu_ref, sems):
        @pl.when(pl.program_id(0) == 0)
        def _():
            accg_ref[...] = jnp.zeros_like(accg_ref)
            accu_ref[...] = jnp.zeros_like(accu_ref)

        h = h_ref[...]
        _accumulate_tn(accg_ref, dg_ref, h)
        _accumulate_tn(accu_ref, du_ref, h)

        @pl.when(pl.program_id(0) == last)
        def _():
            outs = [pltpu.make_async_copy(accg_ref, dwg_ref, sems.at[0]), pltpu.make_async_copy(accu_ref, dwu_ref, sems.at[1])]
            for cp in outs:
                cp.start()
            for cp in outs:
                cp.wait()

    shape = jax.ShapeDtypeStruct((D_FF, D_MODEL), F32)
    return pl.pallas_call(
        body, name="grad_w_gate_up", grid=(s // tm,),
        in_specs=[_rows(tm, D_FF), _rows(tm, D_FF), _rows(tm, D_MODEL)],
        out_specs=[HBM, HBM], out_shape=[shape, shape],
        scratch_shapes=[pltpu.VMEM((D_FF, D_MODEL), F32), pltpu.VMEM((D_FF, D_MODEL), F32), pltpu.SemaphoreType.DMA((2,))],
        compiler_params=_params("arbitrary"),
    )(dgate, dup, h2)


def _bwd_outproj_pool(dx2b, attn, pool, mixed, w_out, w_pool, b_pool, pool_scale, rider=None):
    s = dx2b.shape[0]
    tm = _token_tile(s)
    nt = s // tm

    def body(dx_ref, a_ref, p_ref, mix_ref, w_ref, wp_ref, bp_ref, ps_ref,
             dattn_ref, du_ref, dwout_ref, dwp_ref, dbp_ref, dps_ref, head_ref):
        step = pl.program_id(0)
        tile = nt - 1 - step

        @pl.when(step == 0)
        def _():
            head_ref[...] = jnp.zeros_like(head_ref)
            dwout_ref[...] = jnp.zeros_like(dwout_ref)
            dwp_ref[...] = jnp.zeros_like(dwp_ref)
            dbp_ref[...] = jnp.zeros_like(dbp_ref)
            dps_ref[...] = jnp.zeros_like(dps_ref)

        dx = dx_ref[...]
        dwout_ref[:ATTN_WIDTH, :] += _tn(a_ref[...], dx)
        dwout_ref[ATTN_WIDTH:, :] += _tn(p_ref[...], dx)
        dcat = _nt(dx, w_ref[...])
        dattn_ref[...] = dcat[:, :ATTN_WIDTH].astype(BF16)
        dpool = dcat[:, ATTN_WIDTH:]
        pos = lax.broadcasted_iota(jnp.int32, (tm, POOL_GROUP_WIDTH), 0) + tile * tm
        head = head_ref[...]
        n_ext = tm + POOL_HALO
        for g, size in enumerate(POOL_SIZES):
            cols = slice(g * POOL_GROUP_WIDTH, (g + 1) * POOL_GROUP_WIDTH)
            mixed_g = mix_ref[:, cols]
            pre = _nn(mixed_g, wp_ref[g]) + bp_ref[:, cols]
            dy = dpool[:, cols]
            dps_ref[:, cols] += jnp.sum(dy * pre, axis=0, keepdims=True)
            dpre = dy * ps_ref[:, cols]
            dbp_ref[:, cols] += jnp.sum(dpre, axis=0, keepdims=True)
            dpre_b = dpre.astype(BF16)
            dwp_ref[g] += _tn(mixed_g, dpre_b)
            dmixed = _nt(dpre_b, wp_ref[g])
            w = dmixed / jnp.minimum(pos + 1, size).astype(F32)
            head_ref[:, cols] = w[:POOL_HALO, :]
            a = jnp.concatenate([w, head[:, cols]], axis=0)
            shift = 1
            while shift < size:
                a = a + pltpu.roll(a, n_ext - shift, 0)
                shift *= 2
            du_ref[:, cols] = (a[:tm, :] - dmixed).astype(BF16)

    rev = lambda w: pl.BlockSpec((tm, w), lambda i: (nt - 1 - i, 0))
    return _gridded(
        body, rider, name="bwd_outproj_pool", grid=(nt,),
        in_specs=[rev(D_MODEL), rev(ATTN_WIDTH), rev(POOL_WIDTH), rev(POOL_WIDTH), _full((D_MODEL, D_MODEL)),
                  _full((4, POOL_GROUP_WIDTH, POOL_GROUP_WIDTH)), _full((1, POOL_WIDTH)), _full((1, POOL_WIDTH))],
        out_specs=[rev(ATTN_WIDTH), rev(POOL_WIDTH), _full((D_MODEL, D_MODEL)),
                   _full((4, POOL_GROUP_WIDTH, POOL_GROUP_WIDTH)), _full((1, POOL_WIDTH)), _full((1, POOL_WIDTH))],
        out_shape=[jax.ShapeDtypeStruct((s, ATTN_WIDTH), BF16), jax.ShapeDtypeStruct((s, POOL_WIDTH), BF16),
                   jax.ShapeDtypeStruct((D_MODEL, D_MODEL), F32),
                   jax.ShapeDtypeStruct((4, POOL_GROUP_WIDTH, POOL_GROUP_WIDTH), F32),
                   jax.ShapeDtypeStruct((1, POOL_WIDTH), F32), jax.ShapeDtypeStruct((1, POOL_WIDTH), F32)],
        scratch_shapes=[pltpu.VMEM((POOL_HALO, POOL_WIDTH), F32)],
        args=(dx2b, attn, pool, mixed, w_out, w_pool, b_pool, pool_scale))


def _fold_spread(t):
    low = lax.broadcasted_iota(jnp.int32, (2 * BLOCK, LANES), 1) < HEAD_DIM
    kept = jnp.where(low, t[:2 * BLOCK, :], t[2 * BLOCK:, :])
    return kept + pltpu.roll(kept, HEAD_DIM, 1)


def _attn_bwd(q, kz, vz, dattn, lse, sinks, rider=None):
    s = q.shape[0]
    tq = min(ATTN_TILE, s)
    nt = s // tq
    per = tq // BLOCK

    def body(q_ref, k_ref, kp_ref, v_ref, vp_ref, do_ref, lse_ref, sink_ref,
             dq_ref, dk_ref, dv_ref, dsink_ref, dk_acc, dv_acc, dk_carry, dv_carry):
        step = pl.program_id(0)

        @pl.when(step == 0)
        def _():
            dk_carry[...] = jnp.zeros_like(dk_carry)
            dv_carry[...] = jnp.zeros_like(dv_carry)
            dsink_ref[...] = jnp.zeros_like(dsink_ref)

        dk_acc[0:tq, :] = jnp.zeros((tq, KV_WIDTH), F32)
        dv_acc[0:tq, :] = jnp.zeros((tq, KV_WIDTH), F32)
        dk_acc[tq:, :] = dk_carry[...]
        dv_acc[tq:, :] = dv_carry[...]
        first, band = _band_masks(nt - 1 - step)
        low = lax.broadcasted_iota(jnp.int32, (2 * BLOCK, LANES), 1) < HEAD_DIM
        chains = [(b, kv) for b in range(per) for kv in range(N_KV_HEADS)]

        def operands(b, kv):
            rows = slice(b * BLOCK, (b + 1) * BLOCK)
            qab = _stack_pair(q_ref, rows, kv)
            doab = _stack_pair(do_ref, rows, kv)
            kzb = _band(k_ref, kp_ref, b, kv)
            return qab, doab, kzb, _nt(kzb, qab), _nt(_band(v_ref, vp_ref, b, kv), doab)

        folded = {}

        def finish(b, kv, dqab, dkz, dvz):
            rows = slice(b * BLOCK, (b + 1) * BLOCK)
            dq_ref[rows, kv * PAIR:kv * PAIR + LANES] = dqab[:BLOCK] * Q_SCALE
            dq_ref[rows, kv * PAIR + LANES:(kv + 1) * PAIR] = dqab[BLOCK:] * Q_SCALE
            folded[kv] = (_fold_spread(dkz), _fold_spread(dvz))
            if kv == N_KV_HEADS - 1:
                band_rows = slice(b * BLOCK, (b + 2) * BLOCK)
                dk_acc[band_rows, :] += jnp.where(low, folded[0][0], folded[1][0])
                dv_acc[band_rows, :] += jnp.where(low, folded[0][1], folded[1][1])

        ahead = operands(*chains[0])
        behind = None
        for n, (b, kv) in enumerate(chains):
            rows = slice(b * BLOCK, (b + 1) * BLOCK)
            mask = first if b == 0 else band
            qab, doab, kzb, st, dpt = ahead
            if n + 1 < len(chains):
                ahead = operands(*chains[n + 1])
            probs, dscores = [], []
            for half in range(2):
                top, bottom = _pair_heads(kv, half)
                keys = slice(half * 2 * BLOCK, (half + 1) * 2 * BLOCK)
                lse_h = _per_query(lse_ref, rows, top, bottom)
                p = jnp.where(mask[keys, :], jnp.exp(st[keys, :] - lse_h), 0.0)
                dph = dpt[keys, :]
                delta = _reduce_rows(p * dph, jnp.add, jnp.sum)
                probs.append(p.astype(BF16))
                dscores.append((p * (dph - delta)).astype(BF16))
                leak = jnp.exp(_sink_per_query(sink_ref, top, bottom) - lse_h) * delta
                dsink_ref[:, top:top + 1] -= jnp.sum(leak[:, :BLOCK], axis=1, keepdims=True)
                dsink_ref[:, bottom:bottom + 1] -= jnp.sum(leak[:, BLOCK:], axis=1, keepdims=True)
            ds = jnp.concatenate(dscores, axis=0)
            results = (_tn(ds, kzb), _nn(ds, qab), _nn(jnp.concatenate(probs, axis=0), doab))
            if behind is not None:
                finish(*behind)
            behind = (b, kv, *results)
        finish(*behind)
        dk_ref[...] = dk_acc[BLOCK:, :]
        dv_ref[...] = dv_acc[BLOCK:, :]
        dk_carry[...] = dk_acc[0:BLOCK, :]
        dv_carry[...] = dv_acc[0:BLOCK, :]

    cur = lambda w: pl.BlockSpec((tq, w), lambda i: (nt - 1 - i, 0))
    prev = pl.BlockSpec((BLOCK, KV_SPREAD), lambda i: (jnp.maximum(per * (nt - 1 - i) - 1, 0), 0))
    acc = pltpu.VMEM((tq + BLOCK, KV_WIDTH), F32)
    carry = pltpu.VMEM((BLOCK, KV_WIDTH), F32)
    return _gridded(
        body, rider, name="attn_bwd", grid=(nt,),
        in_specs=[cur(ATTN_WIDTH), cur(KV_SPREAD), prev, cur(KV_SPREAD), prev, cur(ATTN_WIDTH),
                  pl.BlockSpec((N_Q_HEADS, tq), lambda i: (0, nt - 1 - i)), _full((1, N_Q_HEADS))],
        out_specs=[cur(ATTN_WIDTH), cur(KV_WIDTH), cur(KV_WIDTH), _full((1, N_Q_HEADS))],
        out_shape=[jax.ShapeDtypeStruct((s, ATTN_WIDTH), F32), jax.ShapeDtypeStruct((s, KV_WIDTH), F32),
                   jax.ShapeDtypeStruct((s, KV_WIDTH), F32), jax.ShapeDtypeStruct((1, N_Q_HEADS), F32)],
        scratch_shapes=[acc, acc, carry, carry],
        args=(q, kz, kz, vz, vz, dattn, lse, sinks))


def _bwd_inproj(dq, dk, dv, du, cos, sin, win_t, x, g_mix, dx2):
    s = x.shape[0]
    tm = _token_tile(s)

    def body(dq_ref, dk_ref, dv_ref, du_ref, cos_ref, sin_ref, w_ref, x_ref, g_ref, dx2_ref,
             dx_ref, dw_ref, db_ref, dg_ref):
        @pl.when(pl.program_id(0) == 0)
        def _():
            dw_ref[...] = jnp.zeros_like(dw_ref)
            db_ref[...] = jnp.zeros_like(db_ref)
            dg_ref[...] = jnp.zeros_like(dg_ref)

        cos_t, sin_t = cos_ref[...], sin_ref[...]
        dz32 = jnp.concatenate([_rope_bwd(dq_ref[...], cos_t, sin_t), _rope_bwd(dk_ref[...], cos_t, sin_t),
                                dv_ref[...], du_ref[...].astype(F32)], axis=1)
        db_ref[...] += jnp.sum(dz32, axis=0, keepdims=True)
        dz = dz32.astype(BF16)
        g = g_ref[...]
        n, r = _rms(x_ref[...])
        h = (n * g).astype(BF16)
        for m0 in range(0, IN_WIDTH, TN_ROW_CHUNK):
            dw_ref[m0:m0 + TN_ROW_CHUNK, :] += _tn(dz[:, m0:m0 + TN_ROW_CHUNK], h)
        dx, dg = _rms_bwd(_nn(dz, w_ref[...]), n, r, g)
        dg_ref[...] += dg
        dx_ref[...] = dx2_ref[...] + dx

    return _gridded(
        body, None, name="bwd_inproj", grid=(s // tm,),
        in_specs=[_rows(tm, ATTN_WIDTH), _rows(tm, KV_WIDTH), _rows(tm, KV_WIDTH), _rows(tm, POOL_WIDTH),
                  _rows(tm, LANES), _rows(tm, LANES), _full((IN_WIDTH, D_MODEL)), _rows(tm, D_MODEL),
                  _full((1, D_MODEL)), _rows(tm, D_MODEL)],
        out_specs=[_rows(tm, D_MODEL), _full((IN_WIDTH, D_MODEL)), _full((1, IN_WIDTH)), _full((1, D_MODEL))],
        out_shape=[jax.ShapeDtypeStruct((s, D_MODEL), F32), jax.ShapeDtypeStruct((IN_WIDTH, D_MODEL), F32),
                   jax.ShapeDtypeStruct((1, IN_WIDTH), F32), jax.ShapeDtypeStruct((1, D_MODEL), F32)],
        scratch_shapes=[], args=(dq, dk, dv, du, cos, sin, win_t, x, g_mix, dx2))


def _rope_tables(s):
    inv_freq = 1.0 / (ROPE_THETA ** (jnp.arange(0, HEAD_DIM, 2, dtype=F32) / HEAD_DIM))
    ang = jnp.arange(s, dtype=F32)[:, None] * inv_freq[None, :]
    cos, sin = jnp.cos(ang), jnp.sin(ang)
    return jnp.tile(cos, (1, 4)), jnp.tile(jnp.concatenate([-sin, sin], axis=1), (1, 2))


def _place():
    return lax.axis_index("x"), lax.axis_index("y"), lax.axis_index("c")


def _other_chips(x, y):
    return [(1 - x, y), (x, 1 - y), (1 - x, 1 - y)]


def _gather_rider(blocks):
    nm = len(blocks)

    def plan(ins, outs, sems):
        send_sems, recv_sems, local_sems = sems
        x, y, c = _place()
        me, sibling = (x, y, c), (x, y, 1 - c)
        chips = _other_chips(x, y)

        def rows(m, px, py, pc):
            r = ins[m].shape[0]
            return outs[m].at[pl.ds((4 * px + 2 * py + pc) * r, r), :]

        def copy(m, k, block, to, src=None):
            return pltpu.make_async_remote_copy(
                src_ref=rows(m, *block) if src is None else src, dst_ref=rows(m, *block),
                send_sem=send_sems.at[k * nm + m], recv_sem=recv_sems.at[k * nm + m],
                device_id=to, device_id_type=MESH)

        mine = [pltpu.make_async_copy(ins[m], rows(m, *me), local_sems.at[m]) for m in range(nm)]
        first = [copy(m, 0, me, sibling, src=ins[m]) for m in range(nm)]
        first += [copy(m, 1 + j, me, (*chip, c), src=ins[m]) for j, chip in enumerate(chips) for m in range(nm)]
        return me, sibling, chips, copy, mine, first

    def start(ins, outs, sems):
        *_, mine, first = plan(ins, outs, sems)
        for cp in mine + first:
            cp.start()

    def finish(ins, outs, sems):
        me, sibling, chips, copy, mine, first = plan(ins, outs, sems)
        c = me[2]
        passed = []
        for j, chip in enumerate(chips):
            for m in range(nm):
                copy(m, 1 + j, (*chip, c), me).wait_recv()
                passed.append(copy(m, 4 + j, (*chip, c), sibling))
                passed[-1].start()
        for m in range(nm):
            copy(m, 0, sibling, me).wait_recv()
        for j, chip in enumerate(chips):
            for m in range(nm):
                copy(m, 4 + j, (*chip, 1 - c), me).wait_recv()
        for cp in first + passed:
            cp.wait_send()
        for cp in mine:
            cp.wait()

    return _Rider(
        arrays=list(blocks), out_shape=[jax.ShapeDtypeStruct((N_DEV * b.shape[0], b.shape[1]), b.dtype) for b in blocks],
        sems=[pltpu.SemaphoreType.DMA((7 * nm,)), pltpu.SemaphoreType.DMA((7 * nm,)), pltpu.SemaphoreType.DMA((nm,))],
        start=start, finish=finish)


def _exchange_rider(copies_of, arrays, out_shape, n_copies):
    def copies(ins, outs, sems):
        send_sems, recv_sems = sems
        return [pltpu.make_async_remote_copy(src_ref=src, dst_ref=dst, send_sem=send_sems.at[k], recv_sem=recv_sems.at[k],
                                             device_id=to, device_id_type=MESH)
                for k, (src, dst, to) in enumerate(copies_of(ins, outs))]

    def start(ins, outs, sems):
        for cp in copies(ins, outs, sems):
            cp.start()

    def finish(ins, outs, sems):
        cps = copies(ins, outs, sems)
        for cp in cps:
            cp.wait_recv()
        for cp in cps:
            cp.wait_send()

    return _Rider(arrays=list(arrays), out_shape=out_shape,
                  sems=[pltpu.SemaphoreType.DMA((n_copies,)), pltpu.SemaphoreType.DMA((n_copies,))], start=start, finish=finish)


def _sibling_rider(grads):
    def copies_of(ins, outs):
        x, y, c = _place()
        for g_ref, o_ref in zip(ins, outs):
            r = g_ref.shape[0] // N_DEV
            for q in range(N_CHIPS):
                yield g_ref.at[pl.ds((2 * q + 1 - c) * r, r), :], o_ref.at[pl.ds(q * r, r), :], (x, y, 1 - c)

    return _exchange_rider(copies_of, grads, [jax.ShapeDtypeStruct((g.shape[0] // 2, g.shape[1]), F32) for g in grads],
                           len(grads) * N_CHIPS)


def _chip_sum(grad, from_sibling, place):
    r = grad.shape[0] // N_DEV
    w = grad.shape[1]

    def body(place_ref, g_ref, s_ref, wire_ref, own_ref):
        total = g_ref[...] + s_ref[...]
        wire_ref[...] = total.astype(BF16)

        @pl.when(pl.program_id(0) == place_ref[1])
        def _():
            own_ref[...] = total

    grid_spec = pltpu.PrefetchScalarGridSpec(
        num_scalar_prefetch=1, grid=(N_CHIPS,),
        in_specs=[pl.BlockSpec((r, w), lambda q, p: (2 * q + p[0], 0)), pl.BlockSpec((r, w), lambda q, p: (q, 0))],
        out_specs=[pl.BlockSpec((r, w), lambda q, p: (q, 0)), pl.BlockSpec((r, w), lambda q, p: (0, 0))])
    return pl.pallas_call(
        body, name="grad_chip_sum", grid_spec=grid_spec,
        out_shape=[jax.ShapeDtypeStruct((N_CHIPS * r, w), BF16), jax.ShapeDtypeStruct((r, w), F32)],
        compiler_params=_params("arbitrary"),
    )(place, grad, from_sibling)


def _chips_rider(wires):
    def copies_of(ins, outs):
        x, y, c = _place()
        for w_ref, o_ref in zip(ins, outs):
            r = w_ref.shape[0] // N_CHIPS
            for j, (px, py) in enumerate(_other_chips(x, y)):
                yield w_ref.at[pl.ds((2 * px + py) * r, r), :], o_ref.at[pl.ds(j * r, r), :], (px, py, c)

    return _exchange_rider(copies_of, wires,
                           [jax.ShapeDtypeStruct((3 * (w.shape[0] // N_CHIPS), w.shape[1]), BF16) for w in wires], len(wires) * 3)


def _adamw_math(w, g, m, v):
    m = ADAM_B1 * m + (1.0 - ADAM_B1) * g
    v = ADAM_B2 * v + (1.0 - ADAM_B2) * jnp.square(g)
    m_hat = m / (1.0 - ADAM_B1 ** ADAM_STEP)
    v_hat = v / (1.0 - ADAM_B2 ** ADAM_STEP)
    delta = -ADAM_LR * (m_hat / (jnp.sqrt(v_hat) + ADAM_EPS) + ADAM_WD * w)
    return delta, m, v


def _reduce_adamw(own, received, w, m, v):
    r = own.shape[0]

    def body(own_ref, rec_ref, w_ref, m_ref, v_ref, g_ref, d_ref, nm_ref, nv_ref):
        g = ((own_ref[...] + rec_ref[0:r, :].astype(F32)) + rec_ref[r:2 * r, :].astype(F32)) + rec_ref[2 * r:, :].astype(F32)
        g_ref[...] = g
        d_ref[...], nm_ref[...], nv_ref[...] = _adamw_math(w_ref[...], g, m_ref[...], v_ref[...])

    shape = jax.ShapeDtypeStruct(own.shape, F32)
    return pl.pallas_call(
        body, name="reduce_adamw", in_specs=[VMEM] * 5, out_specs=[VMEM] * 4, out_shape=[shape] * 4,
        compiler_params=_params(),
    )(own, received, w, m, v)


SMALL = (("g_mix", 1024), ("b_in", 1280), ("sinks", 8), ("w_pool", 65536), ("b_pool", 512), ("pool_scale", 512),
         ("b_out", 1024), ("g_ffn", 1024), ("g_final", 1024), ("loss", 1024))


def _small_rows(size):
    return -(-size // (8 * LANES)) * 8


SMALL_ROWS = sum(_small_rows(size) for _, size in SMALL)


def _pack_small(values):
    parts = []
    for name, size in SMALL:
        flat = values[name].reshape(-1).astype(F32)
        parts.append(jnp.pad(flat, (0, _small_rows(size) * LANES - size)).reshape(-1, LANES))
    return jnp.concatenate(parts, axis=0)


def _unpack_small(packed, shapes):
    out, row = {}, 0
    for name, size in SMALL:
        rows = _small_rows(size)
        if name in shapes:
            out[name] = packed[row:row + rows].reshape(-1)[:size].reshape(shapes[name])
        row += rows
    return out


def _small_allreduce_adamw(part, w, m, v):
    rows_n = SMALL_ROWS

    def body(p_ref, w_ref, m_ref, v_ref, g_ref, d_ref, nm_ref, nv_ref, all_ref, send_sems, recv_sems, local_sem):
        x, y, c = _place()
        me, sibling = (x, y, c), (x, y, 1 - c)
        chips = _other_chips(x, y)

        def rows(px, py, pc):
            return all_ref.at[pl.ds((4 * px + 2 * py + pc) * rows_n, rows_n), :]

        def copy(k, block, to, src=None):
            return pltpu.make_async_remote_copy(
                src_ref=rows(*block) if src is None else src, dst_ref=rows(*block),
                send_sem=send_sems.at[k], recv_sem=recv_sems.at[k], device_id=to, device_id_type=MESH)

        mine = pltpu.make_async_copy(p_ref, rows(*me), local_sem)
        mine.start()
        first = [copy(0, me, sibling, src=p_ref)]
        first += [copy(1 + j, me, (*chip, c), src=p_ref) for j, chip in enumerate(chips)]
        for cp in first:
            cp.start()
        passed = [copy(4 + j, (*chip, c), sibling) for j, chip in enumerate(chips)]
        for j, chip in enumerate(chips):
            copy(1 + j, (*chip, c), me).wait_recv()
            passed[j].start()
        copy(0, sibling, me).wait_recv()
        for j, chip in enumerate(chips):
            copy(4 + j, (*chip, 1 - c), me).wait_recv()
        for cp in first + passed:
            cp.wait_send()
        mine.wait()
        total = all_ref[0:rows_n, :]
        for dev in range(1, N_DEV):
            total = total + all_ref[dev * rows_n:(dev + 1) * rows_n, :]
        g_ref[...] = total
        d_ref[...], nm_ref[...], nv_ref[...] = _adamw_math(w_ref[...], total, m_ref[...], v_ref[...])

    shape = jax.ShapeDtypeStruct((rows_n, LANES), F32)
    return pl.pallas_call(
        body, name="small_allreduce_adamw", in_specs=[VMEM] * 4, out_specs=[VMEM] * 4, out_shape=[shape] * 4,
        scratch_shapes=[pltpu.VMEM((N_DEV * rows_n, LANES), F32), pltpu.SemaphoreType.DMA((7,)), pltpu.SemaphoreType.DMA((7,)),
                        pltpu.SemaphoreType.DMA],
        compiler_params=pltpu.CompilerParams(has_side_effects=True, vmem_limit_bytes=VMEM_LIMIT_BYTES),
    )(part, w, m, v)


def kernel(x, g_mix, w_in, b_in, sinks, w_pool, b_pool, pool_scale, w_out, b_out, g_ffn, w_gate, w_up, w_down, g_final, loss_target, m_g_mix, m_w_in, m_b_in, m_sinks, m_w_pool, m_b_pool, m_pool_scale, m_w_out, m_b_out, m_g_ffn, m_w_gate, m_w_up, m_w_down, m_g_final, v_g_mix, v_w_in, v_b_in, v_sinks, v_w_pool, v_b_pool, v_pool_scale, v_w_out, v_b_out, v_g_ffn, v_w_gate, v_w_up, v_w_down, v_g_final):
    weights = dict(g_mix=g_mix, w_in=w_in, b_in=b_in, sinks=sinks, w_pool=w_pool, b_pool=b_pool, pool_scale=pool_scale,
                   w_out=w_out, b_out=b_out, g_ffn=g_ffn, w_gate=w_gate, w_up=w_up, w_down=w_down, g_final=g_final)
    mom1 = dict(g_mix=m_g_mix, w_in=m_w_in, b_in=m_b_in, sinks=m_sinks, w_pool=m_w_pool, b_pool=m_b_pool,
                pool_scale=m_pool_scale, w_out=m_w_out, b_out=m_b_out, g_ffn=m_g_ffn, w_gate=m_w_gate, w_up=m_w_up,
                w_down=m_w_down, g_final=m_g_final)
    mom2 = dict(g_mix=v_g_mix, w_in=v_w_in, b_in=v_b_in, sinks=v_sinks, w_pool=v_w_pool, b_pool=v_b_pool,
                pool_scale=v_pool_scale, w_out=v_w_out, b_out=v_b_out, g_ffn=v_g_ffn, w_gate=v_w_gate, w_up=v_w_up,
                w_down=v_w_down, g_final=v_g_final)
    order = ("g_mix", "w_in", "b_in", "sinks", "w_pool", "b_pool", "pool_scale", "w_out", "b_out", "g_ffn",
             "w_gate", "w_up", "w_down", "g_final")
    big = ("w_in", "w_out", "w_gate", "w_up", "w_down")
    transposed = ("w_in", "w_gate", "w_up")

    def row_shard(name, a):
        return a[0].T if name in transposed else a[0]

    shard = {n: row_shard(n, weights[n]).astype(BF16) for n in big}
    xs, target = x[0], loss_target[0]
    cos, sin = _rope_tables(xs.shape[0])
    wp_b = w_pool[0].astype(BF16)
    bp = b_pool.reshape(1, POOL_WIDTH)
    ps = pool_scale.reshape(1, POOL_WIDTH)
    g_fin = g_final.reshape(1, D_MODEL)
    px, py, pc = _place()
    place = jnp.stack([pc, 2 * px + py]).astype(jnp.int32)

    (win_t,) = _alone(_gather_rider([shard["w_in"]]), "gather_w_in")
    q, kz, vz, vt, mixed, pool, w_out_b, wg_t = _fwd_inproj(
        xs, g_mix, win_t, b_in, cos, sin, wp_b, bp, ps, rider=_gather_rider([shard["w_out"], shard["w_gate"]]))
    attn, lse, wu_t = _attn_fwd(q, kz, vt, sinks, rider=_gather_rider([shard["w_up"]]))
    x2, h2, gate, up, act, wd = _fwd_outproj_ffn_act(attn, pool, w_out_b, b_out, xs, g_ffn, wg_t, wu_t,
                                                      rider=_gather_rider([shard["w_down"]]))
    dx3, dx3b, sq, dg_final, d_wd = _fwd_down_loss(act, x2, wd, g_fin, target)

    dgate, dup, wd_sibling = _bwd_ffn_act(dx3b, gate, up, wd, rider=_sibling_rider([d_wd]))
    wd_sum = _chip_sum(d_wd, wd_sibling, place)
    dx2, dx2b, dg_ffn, db_out, wd_received = _bwd_ffn_in(dgate, dup, wg_t, wu_t, x2, dx3, g_ffn,
                                                         rider=_chips_rider([wd_sum[0]]))
    in_grads = _ffn_in_weight_grads(dgate, dup, h2)
    dattn, du, d_wout, d_wpool, d_bpool, d_pscale, *in_sibling = _bwd_outproj_pool(
        dx2b, attn, pool, mixed, w_out_b, wp_b, bp, ps, rider=_sibling_rider(in_grads))
    in_sums = [_chip_sum(g, s, place) for g, s in zip(in_grads, in_sibling)]
    dq, dk, dv, d_sinks, *landed = _attn_bwd(
        q, kz, vz, dattn, lse, sinks, rider=_join(_chips_rider([wire for wire, _ in in_sums]), _sibling_rider([d_wout])))
    ffn_sums, ffn_received = in_sums + [wd_sum], landed[:2] + [wd_received]
    wout_sum = _chip_sum(d_wout, landed[2], place)
    dx, d_win_t, d_bin, d_gmix = _bwd_inproj(dq, dk, dv, du, cos, sin, win_t, xs, g_mix, dx2)
    (win_sibling,) = _alone(_sibling_rider([d_win_t]), "grad_exchange_sibling")
    win_sum = _chip_sum(d_win_t, win_sibling, place)
    wout_received, win_received = _alone(_chips_rider([wout_sum[0], win_sum[0]]), "grad_exchange_chips")
    small = dict(g_mix=d_gmix, b_in=d_bin, sinks=d_sinks, w_pool=d_wpool, b_pool=d_bpool, pool_scale=d_pscale,
                 b_out=db_out, g_ffn=dg_ffn, g_final=dg_final, loss=sq)

    reduced = dict(zip(("w_gate", "w_up", "w_down", "w_out", "w_in"),
                       zip(ffn_sums + [wout_sum, win_sum], list(ffn_received) + [wout_received, win_received])))
    grad, delta, new_m, new_v = {}, {}, {}, {}
    for n in big:
        (_, own), rec = reduced[n]
        results = _reduce_adamw(own, rec, row_shard(n, weights[n]), row_shard(n, mom1[n]), row_shard(n, mom2[n]))
        grad[n], delta[n], new_m[n], new_v[n] = [(a.T if n in transposed else a)[None] for a in results]

    shapes = {n: weights[n].shape for n in order if n not in big}
    zero_loss = jnp.zeros((1, D_MODEL), F32)
    packed = _small_allreduce_adamw(
        _pack_small(small), _pack_small({**weights, "loss": zero_loss}),
        _pack_small({**mom1, "loss": zero_loss}), _pack_small({**mom2, "loss": zero_loss}))
    for store, pk in zip((grad, delta, new_m, new_v), packed):
        store.update(_unpack_small(pk, shapes))
    loss_rows = _unpack_small(packed[0], {"loss": (D_MODEL,)})["loss"]
    loss = (0.5 / D_MODEL) * jnp.sum(loss_rows)

    return (loss, dx[None], *[grad[n] for n in order], *[delta[n] for n in order],
            *[new_m[n] for n in order], *[new_v[n] for n in order])
```

```python
from typing import Any, Callable, NamedTuple, Sequence

import jax
import jax.numpy as jnp
from jax import lax
from jax.experimental import pallas as pl
from jax.experimental.pallas import tpu as pltpu

D_MODEL = 1024
ATTN_WIDTH = 512
KV_WIDTH = 128
POOL_WIDTH = 512
HEAD_DIM = 64
N_Q_HEADS = 8
N_KV_HEADS = 2
GQA_GROUP = 4
BLOCK = 128
POOL_SIZES = (2, 4, 8, 16)
POOL_GROUP_WIDTH = 128
POOL_HALO = 16
IN_WIDTH = 1280
D_FF = 2816
RMS_EPS = 1e-5
ROPE_THETA = 10000.0
Q_SCALE = HEAD_DIM ** -0.5

ADAM_LR = 0.001
ADAM_B1 = 0.9
ADAM_B2 = 0.999
ADAM_EPS = 1e-08
ADAM_WD = 0.01
ADAM_STEP = 10

N_DEV = 8
N_CHIPS = 4
LANES = 128
VMEM_LIMIT_BYTES = 60 * 1024 * 1024

F32 = jnp.float32
BF16 = jnp.bfloat16
MESH = pl.DeviceIdType.MESH
HBM = pl.BlockSpec(memory_space=pltpu.HBM)
VMEM = pl.BlockSpec(memory_space=pltpu.VMEM)


def _params(*semantics):
    return pltpu.CompilerParams(dimension_semantics=semantics or None, vmem_limit_bytes=VMEM_LIMIT_BYTES)


def _nn(a, b):
    return jnp.dot(a, b, preferred_element_type=F32)


def _nt(a, b):
    return lax.dot_general(a, b, (((1,), (1,)), ((), ())), preferred_element_type=F32)


def _tn(a, b):
    return lax.dot_general(a, b, (((0,), (0,)), ((), ())), preferred_element_type=F32)


def _full(shape):
    return pl.BlockSpec(shape, lambda *_: (0,) * len(shape))


def _rows(tm, width):
    return pl.BlockSpec((tm, width), lambda i, *_: (i, 0))


class _Rider(NamedTuple):
    arrays: Sequence[Any]
    out_shape: Sequence[Any]
    sems: Sequence[Any]
    start: Callable[..., None]
    finish: Callable[..., None]


def _gridded(body, rider, *, name, grid, in_specs, out_specs, out_shape, scratch_shapes, args):
    params = _params("arbitrary")
    if rider is None:
        return pl.pallas_call(body, name=name, grid=grid, in_specs=in_specs, out_specs=out_specs, out_shape=out_shape,
                              scratch_shapes=scratch_shapes, compiler_params=params)(*args)
    bounds, total = [], 0
    for n in (len(in_specs), len(rider.arrays), len(out_specs), len(rider.out_shape), len(scratch_shapes), len(rider.sems)):
        bounds.append((total, total + n))
        total += n
    last = grid[0] - 1

    def riding(*refs):
        ins, r_ins, outs, r_outs, scratch, r_sems = (refs[lo:hi] for lo, hi in bounds)

        @pl.when(pl.program_id(0) == 0)
        def _():
            rider.start(r_ins, r_outs, r_sems)

        body(*ins, *outs, *scratch)

        @pl.when(pl.program_id(0) == last)
        def _():
            rider.finish(r_ins, r_outs, r_sems)

    return pl.pallas_call(
        riding, name=name, grid=grid, in_specs=list(in_specs) + [HBM] * len(rider.arrays),
        out_specs=list(out_specs) + [HBM] * len(rider.out_shape), out_shape=list(out_shape) + list(rider.out_shape),
        scratch_shapes=list(scratch_shapes) + list(rider.sems), compiler_params=params)(*args, *rider.arrays)


def _join(*riders):
    def phase(which):
        def run(ins, outs, sems):
            i = o = s = 0
            for r in riders:
                ni, no, ns = len(r.arrays), len(r.out_shape), len(r.sems)
                getattr(r, which)(ins[i:i + ni], outs[o:o + no], sems[s:s + ns])
                i, o, s = i + ni, o + no, s + ns
        return run

    return _Rider(arrays=[a for r in riders for a in r.arrays], out_shape=[a for r in riders for a in r.out_shape],
                  sems=[a for r in riders for a in r.sems], start=phase("start"), finish=phase("finish"))


def _alone(rider, name):
    n_in, n_out = len(rider.arrays), len(rider.out_shape)

    def body(*refs):
        parts = refs[:n_in], refs[n_in:n_in + n_out], refs[n_in + n_out:]
        rider.start(*parts)
        rider.finish(*parts)

    return pl.pallas_call(body, name=name, in_specs=[HBM] * n_in, out_specs=[HBM] * n_out, out_shape=list(rider.out_shape),
                          scratch_shapes=list(rider.sems))(*rider.arrays)


def _rot_half(t):
    n = t.shape[1]
    lane = lax.broadcasted_iota(jnp.int32, t.shape, 1)
    return jnp.where((lane % HEAD_DIM) < HEAD_DIM // 2, pltpu.roll(t, n - HEAD_DIM // 2, 1), pltpu.roll(t, HEAD_DIM // 2, 1))


def _rope(t, cos, sin):
    reps = t.shape[1] // LANES
    if reps > 1:
        cos, sin = jnp.tile(cos, (1, reps)), jnp.tile(sin, (1, reps))
    return t * cos + _rot_half(t) * sin


def _rope_bwd(d, cos, sin):
    reps = d.shape[1] // LANES
    if reps > 1:
        cos, sin = jnp.tile(cos, (1, reps)), jnp.tile(sin, (1, reps))
    return d * cos + _rot_half(d * sin)


KV_SPREAD = 4 * LANES


def _spread_kv(t):
    low = lax.broadcasted_iota(jnp.int32, t.shape, 1) < HEAD_DIM
    swapped = pltpu.roll(t, HEAD_DIM, 1)
    zero = jnp.zeros_like(t)
    return jnp.concatenate([jnp.where(low, t, zero), jnp.where(low, zero, swapped),
                            jnp.where(low, swapped, zero), jnp.where(low, zero, t)], axis=1)


def _rms(x):
    r = lax.rsqrt(jnp.mean(x * x, axis=-1, keepdims=True) + RMS_EPS)
    return x * r, r


def _rms_bwd(dh, n, r, g):
    dn = dh * g
    dx = r * (dn - n * jnp.mean(dn * n, axis=-1, keepdims=True))
    return dx, jnp.sum(dh * n, axis=0, keepdims=True)


def _token_tile(s):
    return min(512, s)


def _fwd_inproj(x, g_mix, win_t, b_in, cos, sin, w_pool, b_pool, pool_scale, rider=None):
    s = x.shape[0]
    tm = _token_tile(s)

    def body(x_ref, g_ref, w_ref, b_ref, cos_ref, sin_ref, wp_ref, bp_ref, ps_ref,
             q_ref, k_ref, v_ref, vt_ref, mix_ref, pool_ref, tail_ref):
        i = pl.program_id(0)

        @pl.when(i == 0)
        def _():
            tail_ref[...] = jnp.zeros_like(tail_ref)

        n, _ = _rms(x_ref[...])
        h = (n * g_ref[...]).astype(BF16)
        z = _nt(h, w_ref[...]) + b_ref[...]
        cos_t, sin_t = cos_ref[...], sin_ref[...]
        q_ref[...] = (_rope(z[:, :ATTN_WIDTH], cos_t, sin_t) * Q_SCALE).astype(BF16)
        k_ref[...] = _spread_kv(_rope(z[:, ATTN_WIDTH:ATTN_WIDTH + KV_WIDTH], cos_t, sin_t)).astype(BF16)
        vz = _spread_kv(z[:, ATTN_WIDTH + KV_WIDTH:ATTN_WIDTH + 2 * KV_WIDTH])
        v_ref[...] = vz.astype(BF16)
        vt_ref[...] = vz.T.astype(BF16)
        u = z[:, ATTN_WIDTH + 2 * KV_WIDTH:]
        u_ext = jnp.concatenate([tail_ref[...], u], axis=0)
        tail_ref[...] = u[tm - POOL_HALO:, :]
        pos = lax.broadcasted_iota(jnp.int32, (tm, POOL_GROUP_WIDTH), 0) + i * tm
        for g, size in enumerate(POOL_SIZES):
            cols = slice(g * POOL_GROUP_WIDTH, (g + 1) * POOL_GROUP_WIDTH)
            a = u_ext[:, cols]
            shift = 1
            while shift < size:
                a = a + pltpu.roll(a, shift, 0)
                shift *= 2
            count = jnp.minimum(pos + 1, size).astype(F32)
            mixed = (a[POOL_HALO:, :] / count - u[:, cols]).astype(BF16)
            pre = _nn(mixed, wp_ref[g]) + bp_ref[:, cols]
            mix_ref[:, cols] = mixed
            pool_ref[:, cols] = (pre * ps_ref[:, cols]).astype(BF16)

    bf = lambda w: jax.ShapeDtypeStruct((s, w), BF16)
    return _gridded(
        body, rider, name="fwd_inproj", grid=(s // tm,),
        in_specs=[_rows(tm, D_MODEL), _full((1, D_MODEL)), _full((IN_WIDTH, D_MODEL)), _full((1, IN_WIDTH)),
                  _rows(tm, LANES), _rows(tm, LANES), _full((4, POOL_GROUP_WIDTH, POOL_GROUP_WIDTH)),
                  _full((1, POOL_WIDTH)), _full((1, POOL_WIDTH))],
        out_specs=[_rows(tm, ATTN_WIDTH), _rows(tm, KV_SPREAD), _rows(tm, KV_SPREAD),
                   pl.BlockSpec((KV_SPREAD, tm), lambda i: (0, i)), _rows(tm, POOL_WIDTH), _rows(tm, POOL_WIDTH)],
        out_shape=[bf(ATTN_WIDTH), bf(KV_SPREAD), bf(KV_SPREAD), jax.ShapeDtypeStruct((KV_SPREAD, s), BF16),
                   bf(POOL_WIDTH), bf(POOL_WIDTH)],
        scratch_shapes=[pltpu.VMEM((POOL_HALO, POOL_WIDTH), F32)],
        args=(x, g_mix, win_t, b_in, cos, sin, w_pool, b_pool, pool_scale))


ATTN_TILE = 512
PAIR = 2 * LANES


def _band_masks(tile):
    j = lax.broadcasted_iota(jnp.int32, (4 * BLOCK, 2 * BLOCK), 0) % (2 * BLOCK)
    r = lax.broadcasted_iota(jnp.int32, (4 * BLOCK, 2 * BLOCK), 1) % BLOCK
    band = (j > r) & (j <= r + BLOCK)
    return band & ((tile > 0) | (j >= BLOCK)), band


def _band(cur_ref, prev_ref, b, kv):
    halves = []
    for half in range(2):
        cols = slice(kv * PAIR + half * LANES, kv * PAIR + (half + 1) * LANES)
        if b == 0:
            halves.append(jnp.concatenate([prev_ref[:, cols], cur_ref[0:BLOCK, cols]], axis=0))
        else:
            halves.append(cur_ref[(b - 1) * BLOCK:(b + 1) * BLOCK, cols])
    return jnp.concatenate(halves, axis=0)


def _stack_pair(ref, rows, kv):
    return jnp.concatenate([ref[rows, kv * PAIR:kv * PAIR + LANES], ref[rows, kv * PAIR + LANES:(kv + 1) * PAIR]], axis=0)


def _pair_heads(kv, half):
    return GQA_GROUP * kv + half, GQA_GROUP * kv + 2 + half


def _band_t(cur_ref, prev_ref, b, kv):
    halves = []
    for half in range(2):
        lanes = slice(kv * PAIR + half * LANES, kv * PAIR + (half + 1) * LANES)
        if b == 0:
            halves.append(jnp.concatenate([prev_ref[lanes, :], cur_ref[lanes, 0:BLOCK]], axis=1))
        else:
            halves.append(cur_ref[lanes, (b - 1) * BLOCK:(b + 1) * BLOCK])
    return jnp.concatenate(halves, axis=1)


def _reduce_rows(x, op, reduce):
    while x.shape[0] > 8:
        half = x.shape[0] // 2
        x = op(x[:half], x[half:])
    return reduce(x, axis=0, keepdims=True)


def _per_query(ref, rows, top, bottom):
    return jnp.concatenate([ref[top:top + 1, rows], ref[bottom:bottom + 1, rows]], axis=1)


def _sink_per_query(sink_ref, top, bottom):
    first_slab = lax.broadcasted_iota(jnp.int32, (1, 2 * BLOCK), 1) < BLOCK
    return jnp.where(first_slab, sink_ref[:, top:top + 1], sink_ref[:, bottom:bottom + 1])


def _attn_fwd(q, kz, vt, sinks, rider=None):
    s = q.shape[0]
    tq = min(ATTN_TILE, s)

    def body(q_ref, k_ref, kp_ref, vt_ref, vtp_ref, sink_ref, o_ref, lse_ref):
        first, band = _band_masks(pl.program_id(0))
        chains = [(b, kv) for b in range(tq // BLOCK) for kv in range(N_KV_HEADS)]

        def scores(b, kv):
            rows = slice(b * BLOCK, (b + 1) * BLOCK)
            return _nt(_band(k_ref, kp_ref, b, kv), _stack_pair(q_ref, rows, kv))

        def store(b, kv, ot):
            rows = slice(b * BLOCK, (b + 1) * BLOCK)
            o = ot.T.astype(BF16)
            o_ref[rows, kv * PAIR:kv * PAIR + LANES] = o[:BLOCK]
            o_ref[rows, kv * PAIR + LANES:(kv + 1) * PAIR] = o[BLOCK:]

        ahead = scores(*chains[0])
        behind = None
        for n, (b, kv) in enumerate(chains):
            rows = slice(b * BLOCK, (b + 1) * BLOCK)
            st = jnp.where(first if b == 0 else band, ahead, -jnp.inf)
            if n + 1 < len(chains):
                ahead = scores(*chains[n + 1])
            probs = []
            for half in range(2):
                top, bottom = _pair_heads(kv, half)
                sink = _sink_per_query(sink_ref, top, bottom)
                sh = st[half * 2 * BLOCK:(half + 1) * 2 * BLOCK, :]
                m = jnp.maximum(_reduce_rows(sh, jnp.maximum, jnp.max), sink)
                p = jnp.exp(sh - m)
                denom = _reduce_rows(p, jnp.add, jnp.sum) + jnp.exp(sink - m)
                probs.append((p * (1.0 / denom)).astype(BF16))
                lse = m + jnp.log(denom)
                lse_ref[top:top + 1, rows] = lse[:, :BLOCK]
                lse_ref[bottom:bottom + 1, rows] = lse[:, BLOCK:]
            ot = _nn(_band_t(vt_ref, vtp_ref, b, kv), jnp.concatenate(probs, axis=0))
            if behind is not None:
                store(*behind)
            behind = (b, kv, ot)
        store(*behind)

    per = tq // BLOCK
    cur = lambda w: pl.BlockSpec((tq, w), lambda i: (i, 0))
    prev = pl.BlockSpec((BLOCK, KV_SPREAD), lambda i: (jnp.maximum(per * i - 1, 0), 0))
    cur_t = pl.BlockSpec((KV_SPREAD, tq), lambda i: (0, i))
    prev_t = pl.BlockSpec((KV_SPREAD, BLOCK), lambda i: (0, jnp.maximum(per * i - 1, 0)))
    return _gridded(
        body, rider, name="attn_fwd", grid=(s // tq,),
        in_specs=[cur(ATTN_WIDTH), cur(KV_SPREAD), prev, cur_t, prev_t, _full((1, N_Q_HEADS))],
        out_specs=[cur(ATTN_WIDTH), pl.BlockSpec((N_Q_HEADS, tq), lambda i: (0, i))],
        out_shape=[jax.ShapeDtypeStruct((s, ATTN_WIDTH), BF16), jax.ShapeDtypeStruct((N_Q_HEADS, s), F32)],
        scratch_shapes=[], args=(q, kz, kz, vt, vt, sinks))


FF_CHUNK = 256
TN_ROW_CHUNK = 256


def _resident(shape):
    return pl.BlockSpec(shape, lambda *_: (0,) * len(shape), pipeline_mode=pl.Buffered(1))


def _accumulate_tn(acc_ref, a_ref, b):
    for m0 in range(0, acc_ref.shape[0], TN_ROW_CHUNK):
        acc_ref[m0:m0 + TN_ROW_CHUNK, :] += _tn(a_ref[:, m0:m0 + TN_ROW_CHUNK], b)


def _fwd_outproj_ffn_act(attn, pool, w_out, b_out, x, g_ffn, wg_t, wu_t, rider=None):
    s = x.shape[0]
    tm = _token_tile(s)

    def body(a_ref, p_ref, w_ref, b_ref, x_ref, g_ref, wg_ref, wu_ref, x2_ref, h2_ref, silu_ref, slope_ref, act_ref):
        x2 = x_ref[...] + _nn(a_ref[...], w_ref[:ATTN_WIDTH, :]) + _nn(p_ref[...], w_ref[ATTN_WIDTH:, :]) + b_ref[...]
        x2_ref[...] = x2
        n, _ = _rms(x2)
        h = (n * g_ref[...]).astype(BF16)
        h2_ref[...] = h

        def products(c0):
            return _nt(h, wg_ref[c0:c0 + FF_CHUNK, :]), _nt(h, wu_ref[c0:c0 + FF_CHUNK, :])

        ahead = products(0)
        for c0 in range(0, D_FF, FF_CHUNK):
            cols = slice(c0, c0 + FF_CHUNK)
            gate, up = ahead
            if c0 + FF_CHUNK < D_FF:
                ahead = products(c0 + FF_CHUNK)
            sig = jax.nn.sigmoid(gate)
            silu = gate * sig
            silu_ref[:, cols] = silu.astype(BF16)
            slope_ref[:, cols] = (up * (sig + silu * (1.0 - sig))).astype(BF16)
            act_ref[:, cols] = (silu * up).astype(BF16)

    act_shape = jax.ShapeDtypeStruct((s, D_FF), BF16)
    return _gridded(
        body, rider, name="fwd_outproj_ffn_act", grid=(s // tm,),
        in_specs=[_rows(tm, ATTN_WIDTH), _rows(tm, POOL_WIDTH), _resident((D_MODEL, D_MODEL)), _full((1, D_MODEL)),
                  _rows(tm, D_MODEL), _full((1, D_MODEL)), _resident((D_FF, D_MODEL)), _resident((D_FF, D_MODEL))],
        out_specs=[_rows(tm, D_MODEL), _rows(tm, D_MODEL)] + [_rows(tm, D_FF)] * 3,
        out_shape=[jax.ShapeDtypeStruct((s, D_MODEL), F32), jax.ShapeDtypeStruct((s, D_MODEL), BF16)] + [act_shape] * 3,
        scratch_shapes=[], args=(attn, pool, w_out, b_out, x, g_ffn, wg_t, wu_t))


def _fwd_down_loss(act, x2, wd, g_final, target):
    s = x2.shape[0]
    tm = _token_tile(s)
    last = s // tm - 1

    def body(a_ref, x2_ref, wd_ref, g_ref, t_ref, dx3_ref, dx3b_ref, sq_ref, dg_ref, dwd_ref, acc_ref, sem):
        @pl.when(pl.program_id(0) == 0)
        def _():
            sq_ref[...] = jnp.zeros_like(sq_ref)
            dg_ref[...] = jnp.zeros_like(dg_ref)
            acc_ref[...] = jnp.zeros_like(acc_ref)

        x3 = x2_ref[...] + _nn(a_ref[...], wd_ref[...])
        n, r = _rms(x3)
        g = g_ref[...]
        diff = n * g - t_ref[...]
        sq_ref[...] += jnp.sum(diff * diff, axis=0, keepdims=True)
        dx3, dg = _rms_bwd(diff * (1.0 / D_MODEL), n, r, g)
        dg_ref[...] += dg
        dx3_ref[...] = dx3
        dx3b = dx3.astype(BF16)
        dx3b_ref[...] = dx3b
        _accumulate_tn(acc_ref, a_ref, dx3b)

        @pl.when(pl.program_id(0) == last)
        def _():
            out = pltpu.make_async_copy(acc_ref, dwd_ref, sem)
            out.start()
            out.wait()

    return pl.pallas_call(
        body, name="fwd_down_loss", grid=(s // tm,),
        in_specs=[_rows(tm, D_FF), _rows(tm, D_MODEL), _resident((D_FF, D_MODEL)), _full((1, D_MODEL)), _rows(tm, D_MODEL)],
        out_specs=[_rows(tm, D_MODEL), _rows(tm, D_MODEL), _full((1, D_MODEL)), _full((1, D_MODEL)), HBM],
        out_shape=[jax.ShapeDtypeStruct((s, D_MODEL), F32), jax.ShapeDtypeStruct((s, D_MODEL), BF16),
                   jax.ShapeDtypeStruct((1, D_MODEL), F32), jax.ShapeDtypeStruct((1, D_MODEL), F32),
                   jax.ShapeDtypeStruct((D_FF, D_MODEL), F32)],
        scratch_shapes=[pltpu.VMEM((D_FF, D_MODEL), F32), pltpu.SemaphoreType.DMA],
        compiler_params=_params("arbitrary"),
    )(act, x2, wd, g_final, target)


FFN_BWD_TILE = 256


def _bwd_ffn(dx3b, dx3, silu, slope, h2, x2, wd, wg_t, wu_t, g_ffn, rider=None):
    s = x2.shape[0]
    tm = min(FFN_BWD_TILE, s)
    last = s // tm - 1

    def body(dx3b_ref, dx3_ref, silu_ref, slope_ref, h_ref, x2_ref, wd_ref, wg_ref, wu_ref, g_ref,
             dx2_ref, dx2b_ref, dg_ref, db_ref, dwg_ref, dwu_ref, dgate_ref, dup_ref, accg_ref, accu_ref, sems):
        @pl.when(pl.program_id(0) == 0)
        def _():
            dg_ref[...] = jnp.zeros_like(dg_ref)
            db_ref[...] = jnp.zeros_like(db_ref)
            accg_ref[...] = jnp.zeros_like(accg_ref)
            accu_ref[...] = jnp.zeros_like(accu_ref)

        dx3b = dx3b_ref[...]
        ahead = _nt(dx3b, wd_ref[0:FF_CHUNK, :])
        for c0 in range(0, D_FF, FF_CHUNK):
            cols = slice(c0, c0 + FF_CHUNK)
            dact = ahead
            if c0 + FF_CHUNK < D_FF:
                ahead = _nt(dx3b, wd_ref[c0 + FF_CHUNK:c0 + 2 * FF_CHUNK, :])
            dup_ref[:, cols] = (dact * silu_ref[:, cols].astype(F32)).astype(BF16)
            dgate_ref[:, cols] = (dact * slope_ref[:, cols].astype(F32)).astype(BF16)
        dh2 = _nn(dgate_ref[...], wg_ref[...]) + _nn(dup_ref[...], wu_ref[...])
        h = h_ref[...]
        _accumulate_tn(accg_ref, dgate_ref, h)
        _accumulate_tn(accu_ref, dup_ref, h)
        n, r = _rms(x2_ref[...])
        dx, dg = _rms_bwd(dh2, n, r, g_ref[...])
        dx2 = dx3_ref[...] + dx
        dg_ref[...] += dg
        db_ref[...] += jnp.sum(dx2, axis=0, keepdims=True)
        dx2_ref[...] = dx2
        dx2b_ref[...] = dx2.astype(BF16)

        @pl.when(pl.program_id(0) == last)
        def _():
            outs = [pltpu.make_async_copy(accg_ref, dwg_ref, sems.at[0]), pltpu.make_async_copy(accu_ref, dwu_ref, sems.at[1])]
            for cp in outs:
                cp.start()
            for cp in outs:
                cp.wait()

    grad_shape = jax.ShapeDtypeStruct((D_FF, D_MODEL), F32)
    weight = _resident((D_FF, D_MODEL))
    return _gridded(
        body, rider, name="bwd_ffn", grid=(s // tm,),
        in_specs=[_rows(tm, D_MODEL), _rows(tm, D_MODEL), _rows(tm, D_FF), _rows(tm, D_FF), _rows(tm, D_MODEL),
                  _rows(tm, D_MODEL), weight, weight, weight, _full((1, D_MODEL))],
        out_specs=[_rows(tm, D_MODEL), _rows(tm, D_MODEL), _full((1, D_MODEL)), _full((1, D_MODEL)), HBM, HBM],
        out_shape=[jax.ShapeDtypeStruct((s, D_MODEL), F32), jax.ShapeDtypeStruct((s, D_MODEL), BF16),
                   jax.ShapeDtypeStruct((1, D_MODEL), F32), jax.ShapeDtypeStruct((1, D_MODEL), F32), grad_shape, grad_shape],
        scratch_shapes=[pltpu.VMEM((tm, D_FF), BF16), pltpu.VMEM((tm, D_FF), BF16),
                        pltpu.VMEM((D_FF, D_MODEL), F32), pltpu.VMEM((D_FF, D_MODEL), F32), pltpu.SemaphoreType.DMA((2,))],
        args=(dx3b, dx3, silu, slope, h2, x2, wd, wg_t, wu_t, g_ffn))


def _bwd_outproj_pool(dx2b, attn, pool, mixed, w_out, w_pool, b_pool, pool_scale, rider=None):
    s = dx2b.shape[0]
    tm = _token_tile(s)
    nt = s // tm

    def body(dx_ref, a_ref, p_ref, mix_ref, w_ref, wp_ref, bp_ref, ps_ref,
             dattn_ref, du_ref, dwout_ref, dwp_ref, dbp_ref, dps_ref, head_ref):
        step = pl.program_id(0)
        tile = nt - 1 - step

        @pl.when(step == 0)
        def _():
            head_ref[...] = jnp.zeros_like(head_ref)
            dwout_ref[...] = jnp.zeros_like(dwout_ref)
            dwp_ref[...] = jnp.zeros_like(dwp_ref)
            dbp_ref[...] = jnp.zeros_like(dbp_ref)
            dps_ref[...] = jnp.zeros_like(dps_ref)

        dx = dx_ref[...]
        dwout_ref[:ATTN_WIDTH, :] += _tn(a_ref[...], dx)
        dwout_ref[ATTN_WIDTH:, :] += _tn(p_ref[...], dx)
        dcat = _nt(dx, w_ref[...])
        dattn_ref[...] = dcat[:, :ATTN_WIDTH].astype(BF16)
        dpool = dcat[:, ATTN_WIDTH:]
        pos = lax.broadcasted_iota(jnp.int32, (tm, POOL_GROUP_WIDTH), 0) + tile * tm
        head = head_ref[...]
        n_ext = tm + POOL_HALO
        for g, size in enumerate(POOL_SIZES):
            cols = slice(g * POOL_GROUP_WIDTH, (g + 1) * POOL_GROUP_WIDTH)
            mixed_g = mix_ref[:, cols]
            pre = _nn(mixed_g, wp_ref[g]) + bp_ref[:, cols]
            dy = dpool[:, cols]
            dps_ref[:, cols] += jnp.sum(dy * pre, axis=0, keepdims=True)
            dpre = dy * ps_ref[:, cols]
            dbp_ref[:, cols] += jnp.sum(dpre, axis=0, keepdims=True)
            dpre_b = dpre.astype(BF16)
            dwp_ref[g] += _tn(mixed_g, dpre_b)
            dmixed = _nt(dpre_b, wp_ref[g])
            w = dmixed / jnp.minimum(pos + 1, size).astype(F32)
            head_ref[:, cols] = w[:POOL_HALO, :]
            a = jnp.concatenate([w, head[:, cols]], axis=0)
            shift = 1
            while shift < size:
                a = a + pltpu.roll(a, n_ext - shift, 0)
                shift *= 2
            du_ref[:, cols] = (a[:tm, :] - dmixed).astype(BF16)

    rev = lambda w: pl.BlockSpec((tm, w), lambda i: (nt - 1 - i, 0))
    return _gridded(
        body, rider, name="bwd_outproj_pool", grid=(nt,),
        in_specs=[rev(D_MODEL), rev(ATTN_WIDTH), rev(POOL_WIDTH), rev(POOL_WIDTH), _full((D_MODEL, D_MODEL)),
                  _full((4, POOL_GROUP_WIDTH, POOL_GROUP_WIDTH)), _full((1, POOL_WIDTH)), _full((1, POOL_WIDTH))],
        out_specs=[rev(ATTN_WIDTH), rev(POOL_WIDTH), _full((D_MODEL, D_MODEL)),
                   _full((4, POOL_GROUP_WIDTH, POOL_GROUP_WIDTH)), _full((1, POOL_WIDTH)), _full((1, POOL_WIDTH))],
        out_shape=[jax.ShapeDtypeStruct((s, ATTN_WIDTH), BF16), jax.ShapeDtypeStruct((s, POOL_WIDTH), BF16),
                   jax.ShapeDtypeStruct((D_MODEL, D_MODEL), F32),
                   jax.ShapeDtypeStruct((4, POOL_GROUP_WIDTH, POOL_GROUP_WIDTH), F32),
                   jax.ShapeDtypeStruct((1, POOL_WIDTH), F32), jax.ShapeDtypeStruct((1, POOL_WIDTH), F32)],
        scratch_shapes=[pltpu.VMEM((POOL_HALO, POOL_WIDTH), F32)],
        args=(dx2b, attn, pool, mixed, w_out, w_pool, b_pool, pool_scale))


def _fold_spread(t):
    low = lax.broadcasted_iota(jnp.int32, (2 * BLOCK, LANES), 1) < HEAD_DIM
    kept = jnp.where(low, t[:2 * BLOCK, :], t[2 * BLOCK:, :])
    return kept + pltpu.roll(kept, HEAD_DIM, 1)


def _attn_bwd(q, kz, vz, dattn, lse, sinks, rider=None):
    s = q.shape[0]
    tq = min(ATTN_TILE, s)
    nt = s // tq
    per = tq // BLOCK

    def body(q_ref, k_ref, kp_ref, v_ref, vp_ref, do_ref, lse_ref, sink_ref,
             dq_ref, dk_ref, dv_ref, dsink_ref, dk_acc, dv_acc, dk_carry, dv_carry):
        step = pl.program_id(0)

        @pl.when(step == 0)
        def _():
            dk_carry[...] = jnp.zeros_like(dk_carry)
            dv_carry[...] = jnp.zeros_like(dv_carry)
            dsink_ref[...] = jnp.zeros_like(dsink_ref)

        dk_acc[0:tq, :] = jnp.zeros((tq, KV_WIDTH), F32)
        dv_acc[0:tq, :] = jnp.zeros((tq, KV_WIDTH), F32)
        dk_acc[tq:, :] = dk_carry[...]
        dv_acc[tq:, :] = dv_carry[...]
        first, band = _band_masks(nt - 1 - step)
        low = lax.broadcasted_iota(jnp.int32, (2 * BLOCK, LANES), 1) < HEAD_DIM
        chains = [(b, kv) for b in range(per) for kv in range(N_KV_HEADS)]

        def operands(b, kv):
            rows = slice(b * BLOCK, (b + 1) * BLOCK)
            qab = _stack_pair(q_ref, rows, kv)
            doab = _stack_pair(do_ref, rows, kv)
            kzb = _band(k_ref, kp_ref, b, kv)
            return qab, doab, kzb, _nt(kzb, qab), _nt(_band(v_ref, vp_ref, b, kv), doab)

        folded = {}

        def finish(b, kv, dqab, dkz, dvz):
            rows = slice(b * BLOCK, (b + 1) * BLOCK)
            dq_ref[rows, kv * PAIR:kv * PAIR + LANES] = dqab[:BLOCK] * Q_SCALE
            dq_ref[rows, kv * PAIR + LANES:(kv + 1) * PAIR] = dqab[BLOCK:] * Q_SCALE
            folded[kv] = (_fold_spread(dkz), _fold_spread(dvz))
            if kv == N_KV_HEADS - 1:
                band_rows = slice(b * BLOCK, (b + 2) * BLOCK)
                dk_acc[band_rows, :] += jnp.where(low, folded[0][0], folded[1][0])
                dv_acc[band_rows, :] += jnp.where(low, folded[0][1], folded[1][1])

        ahead = operands(*chains[0])
        behind = None
        for n, (b, kv) in enumerate(chains):
            rows = slice(b * BLOCK, (b + 1) * BLOCK)
            mask = first if b == 0 else band
            qab, doab, kzb, st, dpt = ahead
            if n + 1 < len(chains):
                ahead = operands(*chains[n + 1])
            probs, dscores = [], []
            for half in range(2):
                top, bottom = _pair_heads(kv, half)
                keys = slice(half * 2 * BLOCK, (half + 1) * 2 * BLOCK)
                lse_h = _per_query(lse_ref, rows, top, bottom)
                p = jnp.where(mask[keys, :], jnp.exp(st[keys, :] - lse_h), 0.0)
                dph = dpt[keys, :]
                delta = _reduce_rows(p * dph, jnp.add, jnp.sum)
                probs.append(p.astype(BF16))
                dscores.append((p * (dph - delta)).astype(BF16))
                leak = jnp.exp(_sink_per_query(sink_ref, top, bottom) - lse_h) * delta
                dsink_ref[:, top:top + 1] -= jnp.sum(leak[:, :BLOCK], axis=1, keepdims=True)
                dsink_ref[:, bottom:bottom + 1] -= jnp.sum(leak[:, BLOCK:], axis=1, keepdims=True)
            ds = jnp.concatenate(dscores, axis=0)
            results = (_tn(ds, kzb), _nn(ds, qab), _nn(jnp.concatenate(probs, axis=0), doab))
            if behind is not None:
                finish(*behind)
            behind = (b, kv, *results)
        finish(*behind)
        dk_ref[...] = dk_acc[BLOCK:, :]
        dv_ref[...] = dv_acc[BLOCK:, :]
        dk_carry[...] = dk_acc[0:BLOCK, :]
        dv_carry[...] = dv_acc[0:BLOCK, :]

    cur = lambda w: pl.BlockSpec((tq, w), lambda i: (nt - 1 - i, 0))
    prev = pl.BlockSpec((BLOCK, KV_SPREAD), lambda i: (jnp.maximum(per * (nt - 1 - i) - 1, 0), 0))
    acc = pltpu.VMEM((tq + BLOCK, KV_WIDTH), F32)
    carry = pltpu.VMEM((BLOCK, KV_WIDTH), F32)
    return _gridded(
        body, rider, name="attn_bwd", grid=(nt,),
        in_specs=[cur(ATTN_WIDTH), cur(KV_SPREAD), prev, cur(KV_SPREAD), prev, cur(ATTN_WIDTH),
                  pl.BlockSpec((N_Q_HEADS, tq), lambda i: (0, nt - 1 - i)), _full((1, N_Q_HEADS))],
        out_specs=[cur(ATTN_WIDTH), cur(KV_WIDTH), cur(KV_WIDTH), _full((1, N_Q_HEADS))],
        out_shape=[jax.ShapeDtypeStruct((s, ATTN_WIDTH), F32), jax.ShapeDtypeStruct((s, KV_WIDTH), F32),
                   jax.ShapeDtypeStruct((s, KV_WIDTH), F32), jax.ShapeDtypeStruct((1, N_Q_HEADS), F32)],
        scratch_shapes=[acc, acc, carry, carry],
        args=(q, kz, kz, vz, vz, dattn, lse, sinks))


def _bwd_inproj(dq, dk, dv, du, cos, sin, win_t, x, g_mix, dx2):
    s = x.shape[0]
    tm = _token_tile(s)

    def body(dq_ref, dk_ref, dv_ref, du_ref, cos_ref, sin_ref, w_ref, x_ref, g_ref, dx2_ref,
             dx_ref, dw_ref, db_ref, dg_ref):
        @pl.when(pl.program_id(0) == 0)
        def _():
            dw_ref[...] = jnp.zeros_like(dw_ref)
            db_ref[...] = jnp.zeros_like(db_ref)
            dg_ref[...] = jnp.zeros_like(dg_ref)

        cos_t, sin_t = cos_ref[...], sin_ref[...]
        dz32 = jnp.concatenate([_rope_bwd(dq_ref[...], cos_t, sin_t), _rope_bwd(dk_ref[...], cos_t, sin_t),
                                dv_ref[...], du_ref[...].astype(F32)], axis=1)
        db_ref[...] += jnp.sum(dz32, axis=0, keepdims=True)
        dz = dz32.astype(BF16)
        g = g_ref[...]
        n, r = _rms(x_ref[...])
        h = (n * g).astype(BF16)
        dh = _nn(dz, w_ref[...])
        for m0 in range(0, IN_WIDTH, TN_ROW_CHUNK):
            dw_ref[m0:m0 + TN_ROW_CHUNK, :] += _tn(dz[:, m0:m0 + TN_ROW_CHUNK], h)
        dx, dg = _rms_bwd(dh, n, r, g)
        dg_ref[...] += dg
        dx_ref[...] = dx2_ref[...] + dx

    return _gridded(
        body, None, name="bwd_inproj", grid=(s // tm,),
        in_specs=[_rows(tm, ATTN_WIDTH), _rows(tm, KV_WIDTH), _rows(tm, KV_WIDTH), _rows(tm, POOL_WIDTH),
                  _rows(tm, LANES), _rows(tm, LANES), _full((IN_WIDTH, D_MODEL)), _rows(tm, D_MODEL),
                  _full((1, D_MODEL)), _rows(tm, D_MODEL)],
        out_specs=[_rows(tm, D_MODEL), _full((IN_WIDTH, D_MODEL)), _full((1, IN_WIDTH)), _full((1, D_MODEL))],
        out_shape=[jax.ShapeDtypeStruct((s, D_MODEL), F32), jax.ShapeDtypeStruct((IN_WIDTH, D_MODEL), F32),
                   jax.ShapeDtypeStruct((1, IN_WIDTH), F32), jax.ShapeDtypeStruct((1, D_MODEL), F32)],
        scratch_shapes=[], args=(dq, dk, dv, du, cos, sin, win_t, x, g_mix, dx2))


def _rope_tables(s):
    inv_freq = 1.0 / (ROPE_THETA ** (jnp.arange(0, HEAD_DIM, 2, dtype=F32) / HEAD_DIM))
    ang = jnp.arange(s, dtype=F32)[:, None] * inv_freq[None, :]
    cos, sin = jnp.cos(ang), jnp.sin(ang)
    return jnp.tile(cos, (1, 4)), jnp.tile(jnp.concatenate([-sin, sin], axis=1), (1, 2))


def _place():
    return lax.axis_index("x"), lax.axis_index("y"), lax.axis_index("c")


def _other_chips(x, y):
    return [(1 - x, y), (x, 1 - y), (1 - x, 1 - y)]


def _gather_rider(blocks):
    nm = len(blocks)

    def plan(ins, outs, sems):
        send_sems, recv_sems, local_sems = sems
        x, y, c = _place()
        me, sibling = (x, y, c), (x, y, 1 - c)
        chips = _other_chips(x, y)

        def rows(m, px, py, pc):
            r = ins[m].shape[0]
            return outs[m].at[pl.ds((4 * px + 2 * py + pc) * r, r), :]

        def copy(m, k, block, to, src=None):
            return pltpu.make_async_remote_copy(
                src_ref=rows(m, *block) if src is None else src, dst_ref=rows(m, *block),
                send_sem=send_sems.at[k * nm + m], recv_sem=recv_sems.at[k * nm + m],
                device_id=to, device_id_type=MESH)

        mine = [pltpu.make_async_copy(ins[m], rows(m, *me), local_sems.at[m]) for m in range(nm)]
        first = [copy(m, 0, me, sibling, src=ins[m]) for m in range(nm)]
        first += [copy(m, 1 + j, me, (*chip, c), src=ins[m]) for j, chip in enumerate(chips) for m in range(nm)]
        return me, sibling, chips, copy, mine, first

    def start(ins, outs, sems):
        *_, mine, first = plan(ins, outs, sems)
        for cp in mine + first:
            cp.start()

    def finish(ins, outs, sems):
        me, sibling, chips, copy, mine, first = plan(ins, outs, sems)
        c = me[2]
        passed = []
        for j, chip in enumerate(chips):
            for m in range(nm):
                copy(m, 1 + j, (*chip, c), me).wait_recv()
                passed.append(copy(m, 4 + j, (*chip, c), sibling))
                passed[-1].start()
        for m in range(nm):
            copy(m, 0, sibling, me).wait_recv()
        for j, chip in enumerate(chips):
            for m in range(nm):
                copy(m, 4 + j, (*chip, 1 - c), me).wait_recv()
        for cp in first + passed:
            cp.wait_send()
        for cp in mine:
            cp.wait()

    return _Rider(
        arrays=list(blocks), out_shape=[jax.ShapeDtypeStruct((N_DEV * b.shape[0], b.shape[1]), b.dtype) for b in blocks],
        sems=[pltpu.SemaphoreType.DMA((7 * nm,)), pltpu.SemaphoreType.DMA((7 * nm,)), pltpu.SemaphoreType.DMA((nm,))],
        start=start, finish=finish)


def _exchange_rider(copies_of, arrays, out_shape, n_copies):
    def copies(ins, outs, sems):
        send_sems, recv_sems = sems
        return [pltpu.make_async_remote_copy(src_ref=src, dst_ref=dst, send_sem=send_sems.at[k], recv_sem=recv_sems.at[k],
                                             device_id=to, device_id_type=MESH)
                for k, (src, dst, to) in enumerate(copies_of(ins, outs))]

    def start(ins, outs, sems):
        for cp in copies(ins, outs, sems):
            cp.start()

    def finish(ins, outs, sems):
        cps = copies(ins, outs, sems)
        for cp in cps:
            cp.wait_recv()
        for cp in cps:
            cp.wait_send()

    return _Rider(arrays=list(arrays), out_shape=out_shape,
                  sems=[pltpu.SemaphoreType.DMA((n_copies,)), pltpu.SemaphoreType.DMA((n_copies,))], start=start, finish=finish)


def _sibling_rider(grads):
    def copies_of(ins, outs):
        x, y, c = _place()
        for g_ref, o_ref in zip(ins, outs):
            r = g_ref.shape[0] // N_DEV
            for q in range(N_CHIPS):
                yield g_ref.at[pl.ds((2 * q + 1 - c) * r, r), :], o_ref.at[pl.ds(q * r, r), :], (x, y, 1 - c)

    return _exchange_rider(copies_of, grads, [jax.ShapeDtypeStruct((g.shape[0] // 2, g.shape[1]), F32) for g in grads],
                           len(grads) * N_CHIPS)


def _chip_sum(grad, from_sibling, place):
    r = grad.shape[0] // N_DEV
    w = grad.shape[1]

    def body(place_ref, g_ref, s_ref, wire_ref, own_ref):
        total = g_ref[...] + s_ref[...]
        wire_ref[...] = total.astype(BF16)

        @pl.when(pl.program_id(0) == place_ref[1])
        def _():
            own_ref[...] = total

    grid_spec = pltpu.PrefetchScalarGridSpec(
        num_scalar_prefetch=1, grid=(N_CHIPS,),
        in_specs=[pl.BlockSpec((r, w), lambda q, p: (2 * q + p[0], 0)), pl.BlockSpec((r, w), lambda q, p: (q, 0))],
        out_specs=[pl.BlockSpec((r, w), lambda q, p: (q, 0)), pl.BlockSpec((r, w), lambda q, p: (0, 0))])
    return pl.pallas_call(
        body, name="grad_chip_sum", grid_spec=grid_spec,
        out_shape=[jax.ShapeDtypeStruct((N_CHIPS * r, w), BF16), jax.ShapeDtypeStruct((r, w), F32)],
        compiler_params=_params("arbitrary"),
    )(place, grad, from_sibling)


def _chips_rider(wires):
    def copies_of(ins, outs):
        x, y, c = _place()
        for w_ref, o_ref in zip(ins, outs):
            r = w_ref.shape[0] // N_CHIPS
            for j, (px, py) in enumerate(_other_chips(x, y)):
                yield w_ref.at[pl.ds((2 * px + py) * r, r), :], o_ref.at[pl.ds(j * r, r), :], (px, py, c)

    return _exchange_rider(copies_of, wires,
                           [jax.ShapeDtypeStruct((3 * (w.shape[0] // N_CHIPS), w.shape[1]), BF16) for w in wires], len(wires) * 3)


def _adamw_math(w, g, m, v):
    m = ADAM_B1 * m + (1.0 - ADAM_B1) * g
    v = ADAM_B2 * v + (1.0 - ADAM_B2) * jnp.square(g)
    m_hat = m / (1.0 - ADAM_B1 ** ADAM_STEP)
    v_hat = v / (1.0 - ADAM_B2 ** ADAM_STEP)
    delta = -ADAM_LR * (m_hat / (jnp.sqrt(v_hat) + ADAM_EPS) + ADAM_WD * w)
    return delta, m, v


def _reduce_adamw(own, received, w, m, v):
    r = own.shape[0]

    def body(own_ref, rec_ref, w_ref, m_ref, v_ref, g_ref, d_ref, nm_ref, nv_ref):
        g = ((own_ref[...] + rec_ref[0:r, :].astype(F32)) + rec_ref[r:2 * r, :].astype(F32)) + rec_ref[2 * r:, :].astype(F32)
        g_ref[...] = g
        d_ref[...], nm_ref[...], nv_ref[...] = _adamw_math(w_ref[...], g, m_ref[...], v_ref[...])

    shape = jax.ShapeDtypeStruct(own.shape, F32)
    return pl.pallas_call(
        body, name="reduce_adamw", in_specs=[VMEM] * 5, out_specs=[VMEM] * 4, out_shape=[shape] * 4,
        compiler_params=_params(),
    )(own, received, w, m, v)


SMALL = (("g_mix", 1024), ("b_in", 1280), ("sinks", 8), ("w_pool", 65536), ("b_pool", 512), ("pool_scale", 512),
         ("b_out", 1024), ("g_ffn", 1024), ("g_final", 1024), ("loss", 1024))


def _small_rows(size):
    return -(-size // (8 * LANES)) * 8


SMALL_ROWS = sum(_small_rows(size) for _, size in SMALL)


def _pack_small(values):
    parts = []
    for name, size in SMALL:
        flat = values[name].reshape(-1).astype(F32)
        parts.append(jnp.pad(flat, (0, _small_rows(size) * LANES - size)).reshape(-1, LANES))
    return jnp.concatenate(parts, axis=0)


def _unpack_small(packed, shapes):
    out, row = {}, 0
    for name, size in SMALL:
        rows = _small_rows(size)
        if name in shapes:
            out[name] = packed[row:row + rows].reshape(-1)[:size].reshape(shapes[name])
        row += rows
    return out


def _small_allreduce_adamw(part, w, m, v):
    rows_n = SMALL_ROWS

    def body(p_ref, w_ref, m_ref, v_ref, g_ref, d_ref, nm_ref, nv_ref, all_ref, send_sems, recv_sems, local_sem):
        x, y, c = _place()
        me, sibling = (x, y, c), (x, y, 1 - c)
        chips = _other_chips(x, y)

        def rows(px, py, pc):
            return all_ref.at[pl.ds((4 * px + 2 * py + pc) * rows_n, rows_n), :]

        def copy(k, block, to, src=None):
            return pltpu.make_async_remote_copy(
                src_ref=rows(*block) if src is None else src, dst_ref=rows(*block),
                send_sem=send_sems.at[k], recv_sem=recv_sems.at[k], device_id=to, device_id_type=MESH)

        mine = pltpu.make_async_copy(p_ref, rows(*me), local_sem)
        mine.start()
        first = [copy(0, me, sibling, src=p_ref)]
        first += [copy(1 + j, me, (*chip, c), src=p_ref) for j, chip in enumerate(chips)]
        for cp in first:
            cp.start()
        passed = [copy(4 + j, (*chip, c), sibling) for j, chip in enumerate(chips)]
        for j, chip in enumerate(chips):
            copy(1 + j, (*chip, c), me).wait_recv()
            passed[j].start()
        copy(0, sibling, me).wait_recv()
        for j, chip in enumerate(chips):
            copy(4 + j, (*chip, 1 - c), me).wait_recv()
        for cp in first + passed:
            cp.wait_send()
        mine.wait()
        total = all_ref[0:rows_n, :]
        for dev in range(1, N_DEV):
            total = total + all_ref[dev * rows_n:(dev + 1) * rows_n, :]
        g_ref[...] = total
        d_ref[...], nm_ref[...], nv_ref[...] = _adamw_math(w_ref[...], total, m_ref[...], v_ref[...])

    shape = jax.ShapeDtypeStruct((rows_n, LANES), F32)
    return pl.pallas_call(
        body, name="small_allreduce_adamw", in_specs=[VMEM] * 4, out_specs=[VMEM] * 4, out_shape=[shape] * 4,
        scratch_shapes=[pltpu.VMEM((N_DEV * rows_n, LANES), F32), pltpu.SemaphoreType.DMA((7,)), pltpu.SemaphoreType.DMA((7,)),
                        pltpu.SemaphoreType.DMA],
        compiler_params=pltpu.CompilerParams(has_side_effects=True, vmem_limit_bytes=VMEM_LIMIT_BYTES),
    )(part, w, m, v)


def kernel(x, g_mix, w_in, b_in, sinks, w_pool, b_pool, pool_scale, w_out, b_out, g_ffn, w_gate, w_up, w_down, g_final, loss_target, m_g_mix, m_w_in, m_b_in, m_sinks, m_w_pool, m_b_pool, m_pool_scale, m_w_out, m_b_out, m_g_ffn, m_w_gate, m_w_up, m_w_down, m_g_final, v_g_mix, v_w_in, v_b_in, v_sinks, v_w_pool, v_b_pool, v_pool_scale, v_w_out, v_b_out, v_g_ffn, v_w_gate, v_w_up, v_w_down, v_g_final):
    weights = dict(g_mix=g_mix, w_in=w_in, b_in=b_in, sinks=sinks, w_pool=w_pool, b_pool=b_pool, pool_scale=pool_scale,
                   w_out=w_out, b_out=b_out, g_ffn=g_ffn, w_gate=w_gate, w_up=w_up, w_down=w_down, g_final=g_final)
    mom1 = dict(g_mix=m_g_mix, w_in=m_w_in, b_in=m_b_in, sinks=m_sinks, w_pool=m_w_pool, b_pool=m_b_pool,
                pool_scale=m_pool_scale, w_out=m_w_out, b_out=m_b_out, g_ffn=m_g_ffn, w_gate=m_w_gate, w_up=m_w_up,
                w_down=m_w_down, g_final=m_g_final)
    mom2 = dict(g_mix=v_g_mix, w_in=v_w_in, b_in=v_b_in, sinks=v_sinks, w_pool=v_w_pool, b_pool=v_b_pool,
                pool_scale=v_pool_scale, w_out=v_w_out, b_out=v_b_out, g_ffn=v_g_ffn, w_gate=v_w_gate, w_up=v_w_up,
                w_down=v_w_down, g_final=v_g_final)
    order = ("g_mix", "w_in", "b_in", "sinks", "w_pool", "b_pool", "pool_scale", "w_out", "b_out", "g_ffn",
             "w_gate", "w_up", "w_down", "g_final")
    big = ("w_in", "w_out", "w_gate", "w_up", "w_down")
    transposed = ("w_in", "w_gate", "w_up")

    def row_shard(name, a):
        return a[0].T if name in transposed else a[0]

    shard = {n: row_shard(n, weights[n]).astype(BF16) for n in big}
    xs, target = x[0], loss_target[0]
    cos, sin = _rope_tables(xs.shape[0])
    wp_b = w_pool[0].astype(BF16)
    bp = b_pool.reshape(1, POOL_WIDTH)
    ps = pool_scale.reshape(1, POOL_WIDTH)
    g_fin = g_final.reshape(1, D_MODEL)
    px, py, pc = _place()
    place = jnp.stack([pc, 2 * px + py]).astype(jnp.int32)

    (win_t,) = _alone(_gather_rider([shard["w_in"]]), "gather_w_in")
    q, kz, vz, vt, mixed, pool, w_out_b, wg_t = _fwd_inproj(
        xs, g_mix, win_t, b_in, cos, sin, wp_b, bp, ps, rider=_gather_rider([shard["w_out"], shard["w_gate"]]))
    attn, lse, wu_t = _attn_fwd(q, kz, vt, sinks, rider=_gather_rider([shard["w_up"]]))
    x2, h2, silu, slope, act, wd = _fwd_outproj_ffn_act(attn, pool, w_out_b, b_out, xs, g_ffn, wg_t, wu_t,
                                                      rider=_gather_rider([shard["w_down"]]))
    dx3, dx3b, sq, dg_final, d_wd = _fwd_down_loss(act, x2, wd, g_fin, target)

    dx2, dx2b, dg_ffn, db_out, d_wg_t, d_wu_t, wd_sibling = _bwd_ffn(
        dx3b, dx3, silu, slope, h2, x2, wd, wg_t, wu_t, g_ffn, rider=_sibling_rider([d_wd]))
    wd_sum = _chip_sum(d_wd, wd_sibling, place)
    in_grads = [d_wg_t, d_wu_t]
    dattn, du, d_wout, d_wpool, d_bpool, d_pscale, wd_received, *in_sibling = _bwd_outproj_pool(
        dx2b, attn, pool, mixed, w_out_b, wp_b, bp, ps, rider=_join(_chips_rider([wd_sum[0]]), _sibling_rider(in_grads)))
    in_sums = [_chip_sum(g, s, place) for g, s in zip(in_grads, in_sibling)]
    dq, dk, dv, d_sinks, *landed = _attn_bwd(
        q, kz, vz, dattn, lse, sinks, rider=_join(_chips_rider([wire for wire, _ in in_sums]), _sibling_rider([d_wout])))
    ffn_sums, ffn_received = in_sums + [wd_sum], landed[:2] + [wd_received]
    wout_sum = _chip_sum(d_wout, landed[2], place)
    dx, d_win_t, d_bin, d_gmix = _bwd_inproj(dq, dk, dv, du, cos, sin, win_t, xs, g_mix, dx2)
    (win_sibling,) = _alone(_sibling_rider([d_win_t]), "grad_exchange_sibling")
    win_sum = _chip_sum(d_win_t, win_sibling, place)
    wout_received, win_received = _alone(_chips_rider([wout_sum[0], win_sum[0]]), "grad_exchange_chips")
    small = dict(g_mix=d_gmix, b_in=d_bin, sinks=d_sinks, w_pool=d_wpool, b_pool=d_bpool, pool_scale=d_pscale,
                 b_out=db_out, g_ffn=dg_ffn, g_final=dg_final, loss=sq)

    reduced = dict(zip(("w_gate", "w_up", "w_down", "w_out", "w_in"),
                       zip(ffn_sums + [wout_sum, win_sum], list(ffn_received) + [wout_received, win_received])))
    grad, delta, new_m, new_v = {}, {}, {}, {}
    for n in big:
        (_, own), rec = reduced[n]
        results = _reduce_adamw(own, rec, row_shard(n, weights[n]), row_shard(n, mom1[n]), row_shard(n, mom2[n]))
        grad[n], delta[n], new_m[n], new_v[n] = [(a.T if n in transposed else a)[None] for a in results]

    shapes = {n: weights[n].shape for n in order if n not in big}
    zero_loss = jnp.zeros((1, D_MODEL), F32)
    packed = _small_allreduce_adamw(
        _pack_small(small), _pack_small({**weights, "loss": zero_loss}),
        _pack_small({**mom1, "loss": zero_loss}), _pack_small({**mom2, "loss": zero_loss}))
    for store, pk in zip((grad, delta, new_m, new_v), packed):
        store.update(_unpack_small(pk, shapes))
    loss_rows = _unpack_small(packed[0], {"loss": (D_MODEL,)})["loss"]
    loss = (0.5 / D_MODEL) * jnp.sum(loss_rows)

    return (loss, dx[None], *[grad[n] for n in order], *[delta[n] for n in order],
            *[new_m[n] for n in order], *[new_v[n] for n in order])
```

```python
from typing import Any, Callable, NamedTuple, Sequence

import jax
import jax.numpy as jnp
from jax import lax
from jax.experimental import pallas as pl
from jax.experimental.pallas import tpu as pltpu

D_MODEL = 1024
ATTN_WIDTH = 512
KV_WIDTH = 128
POOL_WIDTH = 512
HEAD_DIM = 64
N_Q_HEADS = 8
N_KV_HEADS = 2
GQA_GROUP = 4
BLOCK = 128
POOL_SIZES = (2, 4, 8, 16)
POOL_GROUP_WIDTH = 128
POOL_HALO = 16
IN_WIDTH = 1280
D_FF = 2816
RMS_EPS = 1e-5
ROPE_THETA = 10000.0
Q_SCALE = HEAD_DIM ** -0.5

ADAM_LR = 0.001
ADAM_B1 = 0.9
ADAM_B2 = 0.999
ADAM_EPS = 1e-08
ADAM_WD = 0.01
ADAM_STEP = 10

N_DEV = 8
N_CHIPS = 4
LANES = 128
VMEM_LIMIT_BYTES = 60 * 1024 * 1024

F32 = jnp.float32
BF16 = jnp.bfloat16
MESH = pl.DeviceIdType.MESH
HBM = pl.BlockSpec(memory_space=pltpu.HBM)
VMEM = pl.BlockSpec(memory_space=pltpu.VMEM)


def _params(*semantics):
    return pltpu.CompilerParams(dimension_semantics=semantics or None, vmem_limit_bytes=VMEM_LIMIT_BYTES)


def _nn(a, b):
    return jnp.dot(a, b, preferred_element_type=F32)


def _nt(a, b):
    return lax.dot_general(a, b, (((1,), (1,)), ((), ())), preferred_element_type=F32)


def _tn(a, b):
    return lax.dot_general(a, b, (((0,), (0,)), ((), ())), preferred_element_type=F32)


def _full(shape):
    return pl.BlockSpec(shape, lambda *_: (0,) * len(shape))


def _rows(tm, width):
    return pl.BlockSpec((tm, width), lambda i, *_: (i, 0))


class _Rider(NamedTuple):
    arrays: Sequence[Any]
    out_shape: Sequence[Any]
    sems: Sequence[Any]
    start: Callable[..., None]
    finish: Callable[..., None]


def _gridded(body, rider, *, name, grid, in_specs, out_specs, out_shape, scratch_shapes, args):
    params = _params("arbitrary")
    if rider is None:
        return pl.pallas_call(body, name=name, grid=grid, in_specs=in_specs, out_specs=out_specs, out_shape=out_shape,
                              scratch_shapes=scratch_shapes, compiler_params=params)(*args)
    bounds, total = [], 0
    for n in (len(in_specs), len(rider.arrays), len(out_specs), len(rider.out_shape), len(scratch_shapes), len(rider.sems)):
        bounds.append((total, total + n))
        total += n
    last = grid[0] - 1

    def riding(*refs):
        ins, r_ins, outs, r_outs, scratch, r_sems = (refs[lo:hi] for lo, hi in bounds)

        @pl.when(pl.program_id(0) == 0)
        def _():
            rider.start(r_ins, r_outs, r_sems)

        body(*ins, *outs, *scratch)

        @pl.when(pl.program_id(0) == last)
        def _():
            rider.finish(r_ins, r_outs, r_sems)

    return pl.pallas_call(
        riding, name=name, grid=grid, in_specs=list(in_specs) + [HBM] * len(rider.arrays),
        out_specs=list(out_specs) + [HBM] * len(rider.out_shape), out_shape=list(out_shape) + list(rider.out_shape),
        scratch_shapes=list(scratch_shapes) + list(rider.sems), compiler_params=params)(*args, *rider.arrays)


def _join(*riders):
    def phase(which):
        def run(ins, outs, sems):
            i = o = s = 0
            for r in riders:
                ni, no, ns = len(r.arrays), len(r.out_shape), len(r.sems)
                getattr(r, which)(ins[i:i + ni], outs[o:o + no], sems[s:s + ns])
                i, o, s = i + ni, o + no, s + ns
        return run

    return _Rider(arrays=[a for r in riders for a in r.arrays], out_shape=[a for r in riders for a in r.out_shape],
                  sems=[a for r in riders for a in r.sems], start=phase("start"), finish=phase("finish"))


def _alone(rider, name):
    n_in, n_out = len(rider.arrays), len(rider.out_shape)

    def body(*refs):
        parts = refs[:n_in], refs[n_in:n_in + n_out], refs[n_in + n_out:]
        rider.start(*parts)
        rider.finish(*parts)

    return pl.pallas_call(body, name=name, in_specs=[HBM] * n_in, out_specs=[HBM] * n_out, out_shape=list(rider.out_shape),
                          scratch_shapes=list(rider.sems))(*rider.arrays)


def _rot_half(t):
    n = t.shape[1]
    lane = lax.broadcasted_iota(jnp.int32, t.shape, 1)
    return jnp.where((lane % HEAD_DIM) < HEAD_DIM // 2, pltpu.roll(t, n - HEAD_DIM // 2, 1), pltpu.roll(t, HEAD_DIM // 2, 1))


def _rope(t, cos, sin):
    reps = t.shape[1] // LANES
    if reps > 1:
        cos, sin = jnp.tile(cos, (1, reps)), jnp.tile(sin, (1, reps))
    return t * cos + _rot_half(t) * sin


def _rope_bwd(d, cos, sin):
    reps = d.shape[1] // LANES
    if reps > 1:
        cos, sin = jnp.tile(cos, (1, reps)), jnp.tile(sin, (1, reps))
    return d * cos + _rot_half(d * sin)


KV_SPREAD = 4 * LANES


def _spread_kv(t):
    low = lax.broadcasted_iota(jnp.int32, t.shape, 1) < HEAD_DIM
    swapped = pltpu.roll(t, HEAD_DIM, 1)
    zero = jnp.zeros_like(t)
    return jnp.concatenate([jnp.where(low, t, zero), jnp.where(low, zero, swapped),
                            jnp.where(low, swapped, zero), jnp.where(low, zero, t)], axis=1)


def _rms(x):
    r = lax.rsqrt(jnp.mean(x * x, axis=-1, keepdims=True) + RMS_EPS)
    return x * r, r


def _rms_bwd(dh, n, r, g):
    dn = dh * g
    dx = r * (dn - n * jnp.mean(dn * n, axis=-1, keepdims=True))
    return dx, jnp.sum(dh * n, axis=0, keepdims=True)


def _token_tile(s):
    return min(512, s)


def _fwd_inproj(x, g_mix, win_t, b_in, cos, sin, w_pool, b_pool, pool_scale, rider=None):
    s = x.shape[0]
    tm = _token_tile(s)

    def body(x_ref, g_ref, w_ref, b_ref, cos_ref, sin_ref, wp_ref, bp_ref, ps_ref,
             q_ref, k_ref, v_ref, vt_ref, mix_ref, pool_ref, tail_ref):
        i = pl.program_id(0)

        @pl.when(i == 0)
        def _():
            tail_ref[...] = jnp.zeros_like(tail_ref)

        n, _ = _rms(x_ref[...])
        h = (n * g_ref[...]).astype(BF16)
        z = _nt(h, w_ref[...]) + b_ref[...]
        cos_t, sin_t = cos_ref[...], sin_ref[...]
        q_ref[...] = (_rope(z[:, :ATTN_WIDTH], cos_t, sin_t) * Q_SCALE).astype(BF16)
        k_ref[...] = _spread_kv(_rope(z[:, ATTN_WIDTH:ATTN_WIDTH + KV_WIDTH], cos_t, sin_t)).astype(BF16)
        vz = _spread_kv(z[:, ATTN_WIDTH + KV_WIDTH:ATTN_WIDTH + 2 * KV_WIDTH])
        v_ref[...] = vz.astype(BF16)
        vt_ref[...] = vz.T.astype(BF16)
        u = z[:, ATTN_WIDTH + 2 * KV_WIDTH:]
        u_ext = jnp.concatenate([tail_ref[...], u], axis=0)
        tail_ref[...] = u[tm - POOL_HALO:, :]
        pos = lax.broadcasted_iota(jnp.int32, (tm, POOL_GROUP_WIDTH), 0) + i * tm
        for g, size in enumerate(POOL_SIZES):
            cols = slice(g * POOL_GROUP_WIDTH, (g + 1) * POOL_GROUP_WIDTH)
            a = u_ext[:, cols]
            shift = 1
            while shift < size:
                a = a + pltpu.roll(a, shift, 0)
                shift *= 2
            count = jnp.minimum(pos + 1, size).astype(F32)
            mixed = (a[POOL_HALO:, :] / count - u[:, cols]).astype(BF16)
            pre = _nn(mixed, wp_ref[g]) + bp_ref[:, cols]
            mix_ref[:, cols] = mixed
            pool_ref[:, cols] = (pre * ps_ref[:, cols]).astype(BF16)

    bf = lambda w: jax.ShapeDtypeStruct((s, w), BF16)
    return _gridded(
        body, rider, name="fwd_inproj", grid=(s // tm,),
        in_specs=[_rows(tm, D_MODEL), _full((1, D_MODEL)), _full((IN_WIDTH, D_MODEL)), _full((1, IN_WIDTH)),
                  _rows(tm, LANES), _rows(tm, LANES), _full((4, POOL_GROUP_WIDTH, POOL_GROUP_WIDTH)),
                  _full((1, POOL_WIDTH)), _full((1, POOL_WIDTH))],
        out_specs=[_rows(tm, ATTN_WIDTH), _rows(tm, KV_SPREAD), _rows(tm, KV_SPREAD),
                   pl.BlockSpec((KV_SPREAD, tm), lambda i: (0, i)), _rows(tm, POOL_WIDTH), _rows(tm, POOL_WIDTH)],
        out_shape=[bf(ATTN_WIDTH), bf(KV_SPREAD), bf(KV_SPREAD), jax.ShapeDtypeStruct((KV_SPREAD, s), BF16),
                   bf(POOL_WIDTH), bf(POOL_WIDTH)],
        scratch_shapes=[pltpu.VMEM((POOL_HALO, POOL_WIDTH), F32)],
        args=(x, g_mix, win_t, b_in, cos, sin, w_pool, b_pool, pool_scale))


ATTN_TILE = 512
PAIR = 2 * LANES


def _band_masks(tile):
    j = lax.broadcasted_iota(jnp.int32, (4 * BLOCK, 2 * BLOCK), 0) % (2 * BLOCK)
    r = lax.broadcasted_iota(jnp.int32, (4 * BLOCK, 2 * BLOCK), 1) % BLOCK
    band = (j > r) & (j <= r + BLOCK)
    return band & ((tile > 0) | (j >= BLOCK)), band


def _band(cur_ref, prev_ref, b, kv):
    halves = []
    for half in range(2):
        cols = slice(kv * PAIR + half * LANES, kv * PAIR + (half + 1) * LANES)
        if b == 0:
            halves.append(jnp.concatenate([prev_ref[:, cols], cur_ref[0:BLOCK, cols]], axis=0))
        else:
            halves.append(cur_ref[(b - 1) * BLOCK:(b + 1) * BLOCK, cols])
    return jnp.concatenate(halves, axis=0)


def _stack_pair(ref, rows, kv):
    return jnp.concatenate([ref[rows, kv * PAIR:kv * PAIR + LANES], ref[rows, kv * PAIR + LANES:(kv + 1) * PAIR]], axis=0)


def _pair_heads(kv, half):
    return GQA_GROUP * kv + half, GQA_GROUP * kv + 2 + half


def _band_t(cur_ref, prev_ref, b, kv):
    halves = []
    for half in range(2):
        lanes = slice(kv * PAIR + half * LANES, kv * PAIR + (half + 1) * LANES)
        if b == 0:
            halves.append(jnp.concatenate([prev_ref[lanes, :], cur_ref[lanes, 0:BLOCK]], axis=1))
        else:
            halves.append(cur_ref[lanes, (b - 1) * BLOCK:(b + 1) * BLOCK])
    return jnp.concatenate(halves, axis=1)


def _reduce_rows(x, op, reduce):
    while x.shape[0] > 8:
        half = x.shape[0] // 2
        x = op(x[:half], x[half:])
    return reduce(x, axis=0, keepdims=True)


def _per_query(ref, rows, top, bottom):
    return jnp.concatenate([ref[top:top + 1, rows], ref[bottom:bottom + 1, rows]], axis=1)


def _sink_per_query(sink_ref, top, bottom):
    first_slab = lax.broadcasted_iota(jnp.int32, (1, 2 * BLOCK), 1) < BLOCK
    return jnp.where(first_slab, sink_ref[:, top:top + 1], sink_ref[:, bottom:bottom + 1])


def _attn_fwd(q, kz, vt, sinks, rider=None):
    s = q.shape[0]
    tq = min(ATTN_TILE, s)

    def body(q_ref, k_ref, kp_ref, vt_ref, vtp_ref, sink_ref, o_ref, lse_ref):
        first, band = _band_masks(pl.program_id(0))
        chains = [(b, kv) for b in range(tq // BLOCK) for kv in range(N_KV_HEADS)]

        def scores(b, kv):
            rows = slice(b * BLOCK, (b + 1) * BLOCK)
            return _nt(_band(k_ref, kp_ref, b, kv), _stack_pair(q_ref, rows, kv))

        def store(b, kv, ot):
            rows = slice(b * BLOCK, (b + 1) * BLOCK)
            o = ot.T.astype(BF16)
            o_ref[rows, kv * PAIR:kv * PAIR + LANES] = o[:BLOCK]
            o_ref[rows, kv * PAIR + LANES:(kv + 1) * PAIR] = o[BLOCK:]

        ahead = scores(*chains[0])
        behind = None
        for n, (b, kv) in enumerate(chains):
            rows = slice(b * BLOCK, (b + 1) * BLOCK)
            st = jnp.where(first if b == 0 else band, ahead, -jnp.inf)
            if n + 1 < len(chains):
                ahead = scores(*chains[n + 1])
            probs = []
            for half in range(2):
                top, bottom = _pair_heads(kv, half)
                sink = _sink_per_query(sink_ref, top, bottom)
                sh = st[half * 2 * BLOCK:(half + 1) * 2 * BLOCK, :]
                m = jnp.maximum(_reduce_rows(sh, jnp.maximum, jnp.max), sink)
                p = jnp.exp(sh - m)
                denom = _reduce_rows(p, jnp.add, jnp.sum) + jnp.exp(sink - m)
                probs.append((p * (1.0 / denom)).astype(BF16))
                lse = m + jnp.log(denom)
                lse_ref[top:top + 1, rows] = lse[:, :BLOCK]
                lse_ref[bottom:bottom + 1, rows] = lse[:, BLOCK:]
            ot = _nn(_band_t(vt_ref, vtp_ref, b, kv), jnp.concatenate(probs, axis=0))
            if behind is not None:
                store(*behind)
            behind = (b, kv, ot)
        store(*behind)

    per = tq // BLOCK
    cur = lambda w: pl.BlockSpec((tq, w), lambda i: (i, 0))
    prev = pl.BlockSpec((BLOCK, KV_SPREAD), lambda i: (jnp.maximum(per * i - 1, 0), 0))
    cur_t = pl.BlockSpec((KV_SPREAD, tq), lambda i: (0, i))
    prev_t = pl.BlockSpec((KV_SPREAD, BLOCK), lambda i: (0, jnp.maximum(per * i - 1, 0)))
    return _gridded(
        body, rider, name="attn_fwd", grid=(s // tq,),
        in_specs=[cur(ATTN_WIDTH), cur(KV_SPREAD), prev, cur_t, prev_t, _full((1, N_Q_HEADS))],
        out_specs=[cur(ATTN_WIDTH), pl.BlockSpec((N_Q_HEADS, tq), lambda i: (0, i))],
        out_shape=[jax.ShapeDtypeStruct((s, ATTN_WIDTH), BF16), jax.ShapeDtypeStruct((N_Q_HEADS, s), F32)],
        scratch_shapes=[], args=(q, kz, kz, vt, vt, sinks))


FF_CHUNK = 256
TN_ROW_CHUNK = 256


def _resident(shape):
    return pl.BlockSpec(shape, lambda *_: (0,) * len(shape), pipeline_mode=pl.Buffered(1))


def _accumulate_tn(acc_ref, a_ref, b):
    for m0 in range(0, acc_ref.shape[0], TN_ROW_CHUNK):
        acc_ref[m0:m0 + TN_ROW_CHUNK, :] += _tn(a_ref[:, m0:m0 + TN_ROW_CHUNK], b)


def _fwd_outproj_ffn_act(attn, pool, w_out, b_out, x, g_ffn, wg_t, wu_t, rider=None):
    s = x.shape[0]
    tm = _token_tile(s)

    def body(a_ref, p_ref, w_ref, b_ref, x_ref, g_ref, wg_ref, wu_ref, x2_ref, gate_ref, up_ref, act_ref):
        x2 = x_ref[...] + _nn(a_ref[...], w_ref[:ATTN_WIDTH, :]) + _nn(p_ref[...], w_ref[ATTN_WIDTH:, :]) + b_ref[...]
        x2_ref[...] = x2
        n, _ = _rms(x2)
        h = (n * g_ref[...]).astype(BF16)

        def products(c0):
            return _nt(h, wg_ref[c0:c0 + FF_CHUNK, :]), _nt(h, wu_ref[c0:c0 + FF_CHUNK, :])

        ahead = products(0)
        for c0 in range(0, D_FF, FF_CHUNK):
            cols = slice(c0, c0 + FF_CHUNK)
            gate, up = ahead
            if c0 + FF_CHUNK < D_FF:
                ahead = products(c0 + FF_CHUNK)
            gate_ref[:, cols] = gate.astype(BF16)
            up_ref[:, cols] = up.astype(BF16)
            act_ref[:, cols] = (gate * jax.nn.sigmoid(gate) * up).astype(BF16)

    act_shape = jax.ShapeDtypeStruct((s, D_FF), BF16)
    return _gridded(
        body, rider, name="fwd_outproj_ffn_act", grid=(s // tm,),
        in_specs=[_rows(tm, ATTN_WIDTH), _rows(tm, POOL_WIDTH), _resident((D_MODEL, D_MODEL)), _full((1, D_MODEL)),
                  _rows(tm, D_MODEL), _full((1, D_MODEL)), _resident((D_FF, D_MODEL)), _resident((D_FF, D_MODEL))],
        out_specs=[_rows(tm, D_MODEL)] + [_rows(tm, D_FF)] * 3,
        out_shape=[jax.ShapeDtypeStruct((s, D_MODEL), F32)] + [act_shape] * 3,
        scratch_shapes=[], args=(attn, pool, w_out, b_out, x, g_ffn, wg_t, wu_t))


def _fwd_down_loss(act, x2, wd, g_final, target):
    s = x2.shape[0]
    tm = _token_tile(s)
    last = s // tm - 1

    def body(a_ref, x2_ref, wd_ref, g_ref, t_ref, dx3_ref, dx3b_ref, sq_ref, dg_ref, dwd_ref, acc_ref, sem):
        @pl.when(pl.program_id(0) == 0)
        def _():
            sq_ref[...] = jnp.zeros_like(sq_ref)
            dg_ref[...] = jnp.zeros_like(dg_ref)
            acc_ref[...] = jnp.zeros_like(acc_ref)

        x3 = x2_ref[...] + _nn(a_ref[...], wd_ref[...])
        n, r = _rms(x3)
        g = g_ref[...]
        diff = n * g - t_ref[...]
        sq_ref[...] += jnp.sum(diff * diff, axis=0, keepdims=True)
        dx3, dg = _rms_bwd(diff * (1.0 / D_MODEL), n, r, g)
        dg_ref[...] += dg
        dx3_ref[...] = dx3
        dx3b = dx3.astype(BF16)
        dx3b_ref[...] = dx3b
        _accumulate_tn(acc_ref, a_ref, dx3b)

        @pl.when(pl.program_id(0) == last)
        def _():
            out = pltpu.make_async_copy(acc_ref, dwd_ref, sem)
            out.start()
            out.wait()

    return pl.pallas_call(
        body, name="fwd_down_loss", grid=(s // tm,),
        in_specs=[_rows(tm, D_FF), _rows(tm, D_MODEL), _resident((D_FF, D_MODEL)), _full((1, D_MODEL)), _rows(tm, D_MODEL)],
        out_specs=[_rows(tm, D_MODEL), _rows(tm, D_MODEL), _full((1, D_MODEL)), _full((1, D_MODEL)), HBM],
        out_shape=[jax.ShapeDtypeStruct((s, D_MODEL), F32), jax.ShapeDtypeStruct((s, D_MODEL), BF16),
                   jax.ShapeDtypeStruct((1, D_MODEL), F32), jax.ShapeDtypeStruct((1, D_MODEL), F32),
                   jax.ShapeDtypeStruct((D_FF, D_MODEL), F32)],
        scratch_shapes=[pltpu.VMEM((D_FF, D_MODEL), F32), pltpu.SemaphoreType.DMA],
        compiler_params=_params("arbitrary"),
    )(act, x2, wd, g_final, target)


FFN_BWD_TILE = 256


def _bwd_ffn(dx3b, dx3, gate, up, x2, wd, wg_t, wu_t, g_ffn, rider=None):
    s = x2.shape[0]
    tm = min(FFN_BWD_TILE, s)
    last = s // tm - 1

    def body(dx3b_ref, dx3_ref, gate_ref, up_ref, x2_ref, wd_ref, wg_ref, wu_ref, g_ref,
             dx2_ref, dx2b_ref, dg_ref, db_ref, dwg_ref, dwu_ref, dgate_ref, dup_ref, accg_ref, accu_ref, sems):
        @pl.when(pl.program_id(0) == 0)
        def _():
            dg_ref[...] = jnp.zeros_like(dg_ref)
            db_ref[...] = jnp.zeros_like(db_ref)
            accg_ref[...] = jnp.zeros_like(accg_ref)
            accu_ref[...] = jnp.zeros_like(accu_ref)

        dx3b = dx3b_ref[...]
        g = g_ref[...]
        n, r = _rms(x2_ref[...])
        h = (n * g).astype(BF16)
        ahead = _nt(dx3b, wd_ref[0:FF_CHUNK, :])
        for c0 in range(0, D_FF, FF_CHUNK):
            cols = slice(c0, c0 + FF_CHUNK)
            dact = ahead
            if c0 + FF_CHUNK < D_FF:
                ahead = _nt(dx3b, wd_ref[c0 + FF_CHUNK:c0 + 2 * FF_CHUNK, :])
            gate = gate_ref[:, cols].astype(F32)
            up = up_ref[:, cols].astype(F32)
            sig = jax.nn.sigmoid(gate)
            silu = gate * sig
            dup = (dact * silu).astype(BF16)
            dgate = (dact * up * (sig + silu * (1.0 - sig))).astype(BF16)
            dup_ref[:, cols] = dup
            dgate_ref[:, cols] = dgate
            accg_ref[cols, :] += _tn(dgate, h)
            accu_ref[cols, :] += _tn(dup, h)
        dh2 = _nn(dgate_ref[...], wg_ref[...]) + _nn(dup_ref[...], wu_ref[...])
        dx, dg = _rms_bwd(dh2, n, r, g)
        dx2 = dx3_ref[...] + dx
        dg_ref[...] += dg
        db_ref[...] += jnp.sum(dx2, axis=0, keepdims=True)
        dx2_ref[...] = dx2
        dx2b_ref[...] = dx2.astype(BF16)

        @pl.when(pl.program_id(0) == last)
        def _():
            outs = [pltpu.make_async_copy(accg_ref, dwg_ref, sems.at[0]), pltpu.make_async_copy(accu_ref, dwu_ref, sems.at[1])]
            for cp in outs:
                cp.start()
            for cp in outs:
                cp.wait()

    grad_shape = jax.ShapeDtypeStruct((D_FF, D_MODEL), F32)
    weight = _resident((D_FF, D_MODEL))
    return _gridded(
        body, rider, name="bwd_ffn", grid=(s // tm,),
        in_specs=[_rows(tm, D_MODEL), _rows(tm, D_MODEL), _rows(tm, D_FF), _rows(tm, D_FF),
                  _rows(tm, D_MODEL), weight, weight, weight, _full((1, D_MODEL))],
        out_specs=[_rows(tm, D_MODEL), _rows(tm, D_MODEL), _full((1, D_MODEL)), _full((1, D_MODEL)), HBM, HBM],
        out_shape=[jax.ShapeDtypeStruct((s, D_MODEL), F32), jax.ShapeDtypeStruct((s, D_MODEL), BF16),
                   jax.ShapeDtypeStruct((1, D_MODEL), F32), jax.ShapeDtypeStruct((1, D_MODEL), F32), grad_shape, grad_shape],
        scratch_shapes=[pltpu.VMEM((tm, D_FF), BF16), pltpu.VMEM((tm, D_FF), BF16),
                        pltpu.VMEM((D_FF, D_MODEL), F32), pltpu.VMEM((D_FF, D_MODEL), F32), pltpu.SemaphoreType.DMA((2,))],
        args=(dx3b, dx3, gate, up, x2, wd, wg_t, wu_t, g_ffn))


def _bwd_outproj_pool(dx2b, attn, pool, mixed, w_out, w_pool, b_pool, pool_scale, rider=None):
    s = dx2b.shape[0]
    tm = _token_tile(s)
    nt = s // tm

    def body(dx_ref, a_ref, p_ref, mix_ref, w_ref, wp_ref, bp_ref, ps_ref,
             dattn_ref, du_ref, dwout_ref, dwp_ref, dbp_ref, dps_ref, head_ref):
        step = pl.program_id(0)
        tile = nt - 1 - step

        @pl.when(step == 0)
        def _():
            head_ref[...] = jnp.zeros_like(head_ref)
            dwout_ref[...] = jnp.zeros_like(dwout_ref)
            dwp_ref[...] = jnp.zeros_like(dwp_ref)
            dbp_ref[...] = jnp.zeros_like(dbp_ref)
            dps_ref[...] = jnp.zeros_like(dps_ref)

        dx = dx_ref[...]
        dwout_ref[:ATTN_WIDTH, :] += _tn(a_ref[...], dx)
        dwout_ref[ATTN_WIDTH:, :] += _tn(p_ref[...], dx)
        dcat = _nt(dx, w_ref[...])
        dattn_ref[...] = dcat[:, :ATTN_WIDTH].astype(BF16)
        dpool = dcat[:, ATTN_WIDTH:]
        pos = lax.broadcasted_iota(jnp.int32, (tm, POOL_GROUP_WIDTH), 0) + tile * tm
        head = head_ref[...]
        n_ext = tm + POOL_HALO
        for g, size in enumerate(POOL_SIZES):
            cols = slice(g * POOL_GROUP_WIDTH, (g + 1) * POOL_GROUP_WIDTH)
            mixed_g = mix_ref[:, cols]
            pre = _nn(mixed_g, wp_ref[g]) + bp_ref[:, cols]
            dy = dpool[:, cols]
            dps_ref[:, cols] += jnp.sum(dy * pre, axis=0, keepdims=True)
            dpre = dy * ps_ref[:, cols]
            dbp_ref[:, cols] += jnp.sum(dpre, axis=0, keepdims=True)
            dpre_b = dpre.astype(BF16)
            dwp_ref[g] += _tn(mixed_g, dpre_b)
            dmixed = _nt(dpre_b, wp_ref[g])
            w = dmixed / jnp.minimum(pos + 1, size).astype(F32)
            head_ref[:, cols] = w[:POOL_HALO, :]
            a = jnp.concatenate([w, head[:, cols]], axis=0)
            shift = 1
            while shift < size:
                a = a + pltpu.roll(a, n_ext - shift, 0)
                shift *= 2
            du_ref[:, cols] = (a[:tm, :] - dmixed).astype(BF16)

    rev = lambda w: pl.BlockSpec((tm, w), lambda i: (nt - 1 - i, 0))
    return _gridded(
        body, rider, name="bwd_outproj_pool", grid=(nt,),
        in_specs=[rev(D_MODEL), rev(ATTN_WIDTH), rev(POOL_WIDTH), rev(POOL_WIDTH), _full((D_MODEL, D_MODEL)),
                  _full((4, POOL_GROUP_WIDTH, POOL_GROUP_WIDTH)), _full((1, POOL_WIDTH)), _full((1, POOL_WIDTH))],
        out_specs=[rev(ATTN_WIDTH), rev(POOL_WIDTH), _full((D_MODEL, D_MODEL)),
                   _full((4, POOL_GROUP_WIDTH, POOL_GROUP_WIDTH)), _full((1, POOL_WIDTH)), _full((1, POOL_WIDTH))],
        out_shape=[jax.ShapeDtypeStruct((s, ATTN_WIDTH), BF16), jax.ShapeDtypeStruct((s, POOL_WIDTH), BF16),
                   jax.ShapeDtypeStruct((D_MODEL, D_MODEL), F32),
                   jax.ShapeDtypeStruct((4, POOL_GROUP_WIDTH, POOL_GROUP_WIDTH), F32),
                   jax.ShapeDtypeStruct((1, POOL_WIDTH), F32), jax.ShapeDtypeStruct((1, POOL_WIDTH), F32)],
        scratch_shapes=[pltpu.VMEM((POOL_HALO, POOL_WIDTH), F32)],
        args=(dx2b, attn, pool, mixed, w_out, w_pool, b_pool, pool_scale))


def _fold_spread(t):
    low = lax.broadcasted_iota(jnp.int32, (2 * BLOCK, LANES), 1) < HEAD_DIM
    kept = jnp.where(low, t[:2 * BLOCK, :], t[2 * BLOCK:, :])
    return kept + pltpu.roll(kept, HEAD_DIM, 1)


def _attn_bwd(q, kz, vz, dattn, lse, sinks, rider=None):
    s = q.shape[0]
    tq = min(ATTN_TILE, s)
    nt = s // tq
    per = tq // BLOCK

    def body(q_ref, k_ref, kp_ref, v_ref, vp_ref, do_ref, lse_ref, sink_ref,
             dq_ref, dk_ref, dv_ref, dsink_ref, dk_acc, dv_acc, dk_carry, dv_carry):
        step = pl.program_id(0)

        @pl.when(step == 0)
        def _():
            dk_carry[...] = jnp.zeros_like(dk_carry)
            dv_carry[...] = jnp.zeros_like(dv_carry)
            dsink_ref[...] = jnp.zeros_like(dsink_ref)

        dk_acc[0:tq, :] = jnp.zeros((tq, KV_WIDTH), F32)
        dv_acc[0:tq, :] = jnp.zeros((tq, KV_WIDTH), F32)
        dk_acc[tq:, :] = dk_carry[...]
        dv_acc[tq:, :] = dv_carry[...]
        first, band = _band_masks(nt - 1 - step)
        low = lax.broadcasted_iota(jnp.int32, (2 * BLOCK, LANES), 1) < HEAD_DIM
        chains = [(b, kv) for b in range(per) for kv in range(N_KV_HEADS)]

        def operands(b, kv):
            rows = slice(b * BLOCK, (b + 1) * BLOCK)
            qab = _stack_pair(q_ref, rows, kv)
            doab = _stack_pair(do_ref, rows, kv)
            kzb = _band(k_ref, kp_ref, b, kv)
            return qab, doab, kzb, _nt(kzb, qab), _nt(_band(v_ref, vp_ref, b, kv), doab)

        folded = {}

        def finish(b, kv, dqab, dkz, dvz):
            rows = slice(b * BLOCK, (b + 1) * BLOCK)
            dq_ref[rows, kv * PAIR:kv * PAIR + LANES] = dqab[:BLOCK] * Q_SCALE
            dq_ref[rows, kv * PAIR + LANES:(kv + 1) * PAIR] = dqab[BLOCK:] * Q_SCALE
            folded[kv] = (_fold_spread(dkz), _fold_spread(dvz))
            if kv == N_KV_HEADS - 1:
                band_rows = slice(b * BLOCK, (b + 2) * BLOCK)
                dk_acc[band_rows, :] += jnp.where(low, folded[0][0], folded[1][0])
                dv_acc[band_rows, :] += jnp.where(low, folded[0][1], folded[1][1])

        ahead = operands(*chains[0])
        behind = None
        for n, (b, kv) in enumerate(chains):
            rows = slice(b * BLOCK, (b + 1) * BLOCK)
            mask = first if b == 0 else band
            qab, doab, kzb, st, dpt = ahead
            if n + 1 < len(chains):
                ahead = operands(*chains[n + 1])
            probs, dscores = [], []
            for half in range(2):
                top, bottom = _pair_heads(kv, half)
                keys = slice(half * 2 * BLOCK, (half + 1) * 2 * BLOCK)
                lse_h = _per_query(lse_ref, rows, top, bottom)
                p = jnp.where(mask[keys, :], jnp.exp(st[keys, :] - lse_h), 0.0)
                dph = dpt[keys, :]
                delta = _reduce_rows(p * dph, jnp.add, jnp.sum)
                probs.append(p.astype(BF16))
                dscores.append((p * (dph - delta)).astype(BF16))
                leak = jnp.exp(_sink_per_query(sink_ref, top, bottom) - lse_h) * delta
                dsink_ref[:, top:top + 1] -= jnp.sum(leak[:, :BLOCK], axis=1, keepdims=True)
                dsink_ref[:, bottom:bottom + 1] -= jnp.sum(leak[:, BLOCK:], axis=1, keepdims=True)
            ds = jnp.concatenate(dscores, axis=0)
            results = (_tn(ds, kzb), _nn(ds, qab), _nn(jnp.concatenate(probs, axis=0), doab))
            if behind is not None:
                finish(*behind)
            behind = (b, kv, *results)
        finish(*behind)
        dk_ref[...] = dk_acc[BLOCK:, :]
        dv_ref[...] = dv_acc[BLOCK:, :]
        dk_carry[...] = dk_acc[0:BLOCK, :]
        dv_carry[...] = dv_acc[0:BLOCK, :]

    cur = lambda w: pl.BlockSpec((tq, w), lambda i: (nt - 1 - i, 0))
    prev = pl.BlockSpec((BLOCK, KV_SPREAD), lambda i: (jnp.maximum(per * (nt - 1 - i) - 1, 0), 0))
    acc = pltpu.VMEM((tq + BLOCK, KV_WIDTH), F32)
    carry = pltpu.VMEM((BLOCK, KV_WIDTH), F32)
    return _gridded(
        body, rider, name="attn_bwd", grid=(nt,),
        in_specs=[cur(ATTN_WIDTH), cur(KV_SPREAD), prev, cur(KV_SPREAD), prev, cur(ATTN_WIDTH),
                  pl.BlockSpec((N_Q_HEADS, tq), lambda i: (0, nt - 1 - i)), _full((1, N_Q_HEADS))],
        out_specs=[cur(ATTN_WIDTH), cur(KV_WIDTH), cur(KV_WIDTH), _full((1, N_Q_HEADS))],
        out_shape=[jax.ShapeDtypeStruct((s, ATTN_WIDTH), F32), jax.ShapeDtypeStruct((s, KV_WIDTH), F32),
                   jax.ShapeDtypeStruct((s, KV_WIDTH), F32), jax.ShapeDtypeStruct((1, N_Q_HEADS), F32)],
        scratch_shapes=[acc, acc, carry, carry],
        args=(q, kz, kz, vz, vz, dattn, lse, sinks))


def _bwd_inproj(dq, dk, dv, du, cos, sin, win_t, x, g_mix, dx2):
    s = x.shape[0]
    tm = _token_tile(s)

    def body(dq_ref, dk_ref, dv_ref, du_ref, cos_ref, sin_ref, w_ref, x_ref, g_ref, dx2_ref,
             dx_ref, dw_ref, db_ref, dg_ref):
        @pl.when(pl.program_id(0) == 0)
        def _():
            dw_ref[...] = jnp.zeros_like(dw_ref)
            db_ref[...] = jnp.zeros_like(db_ref)
            dg_ref[...] = jnp.zeros_like(dg_ref)

        cos_t, sin_t = cos_ref[...], sin_ref[...]
        dz32 = jnp.concatenate([_rope_bwd(dq_ref[...], cos_t, sin_t), _rope_bwd(dk_ref[...], cos_t, sin_t),
                                dv_ref[...], du_ref[...].astype(F32)], axis=1)
        db_ref[...] += jnp.sum(dz32, axis=0, keepdims=True)
        dz = dz32.astype(BF16)
        g = g_ref[...]
        n, r = _rms(x_ref[...])
        h = (n * g).astype(BF16)
        dh = _nn(dz, w_ref[...])
        for m0 in range(0, IN_WIDTH, TN_ROW_CHUNK):
            dw_ref[m0:m0 + TN_ROW_CHUNK, :] += _tn(dz[:, m0:m0 + TN_ROW_CHUNK], h)
        dx, dg = _rms_bwd(dh, n, r, g)
        dg_ref[...] += dg
        dx_ref[...] = dx2_ref[...] + dx

    return _gridded(
        body, None, name="bwd_inproj", grid=(s // tm,),
        in_specs=[_rows(tm, ATTN_WIDTH), _rows(tm, KV_WIDTH), _rows(tm, KV_WIDTH), _rows(tm, POOL_WIDTH),
                  _rows(tm, LANES), _rows(tm, LANES), _full((IN_WIDTH, D_MODEL)), _rows(tm, D_MODEL),
                  _full((1, D_MODEL)), _rows(tm, D_MODEL)],
        out_specs=[_rows(tm, D_MODEL), _full((IN_WIDTH, D_MODEL)), _full((1, IN_WIDTH)), _full((1, D_MODEL))],
        out_shape=[jax.ShapeDtypeStruct((s, D_MODEL), F32), jax.ShapeDtypeStruct((IN_WIDTH, D_MODEL), F32),
                   jax.ShapeDtypeStruct((1, IN_WIDTH), F32), jax.ShapeDtypeStruct((1, D_MODEL), F32)],
        scratch_shapes=[], args=(dq, dk, dv, du, cos, sin, win_t, x, g_mix, dx2))


def _rope_tables(s):
    inv_freq = 1.0 / (ROPE_THETA ** (jnp.arange(0, HEAD_DIM, 2, dtype=F32) / HEAD_DIM))
    ang = jnp.arange(s, dtype=F32)[:, None] * inv_freq[None, :]
    cos, sin = jnp.cos(ang), jnp.sin(ang)
    return jnp.tile(cos, (1, 4)), jnp.tile(jnp.concatenate([-sin, sin], axis=1), (1, 2))


def _place():
    return lax.axis_index("x"), lax.axis_index("y"), lax.axis_index("c")


def _other_chips(x, y):
    return [(1 - x, y), (x, 1 - y), (1 - x, 1 - y)]


def _gather_rider(blocks):
    nm = len(blocks)

    def plan(ins, outs, sems):
        send_sems, recv_sems, local_sems = sems
        x, y, c = _place()
        me, sibling = (x, y, c), (x, y, 1 - c)
        chips = _other_chips(x, y)

        def rows(m, px, py, pc):
            r = ins[m].shape[0]
            return outs[m].at[pl.ds((4 * px + 2 * py + pc) * r, r), :]

        def copy(m, k, block, to, src=None):
            return pltpu.make_async_remote_copy(
                src_ref=rows(m, *block) if src is None else src, dst_ref=rows(m, *block),
                send_sem=send_sems.at[k * nm + m], recv_sem=recv_sems.at[k * nm + m],
                device_id=to, device_id_type=MESH)

        mine = [pltpu.make_async_copy(ins[m], rows(m, *me), local_sems.at[m]) for m in range(nm)]
        first = [copy(m, 0, me, sibling, src=ins[m]) for m in range(nm)]
        first += [copy(m, 1 + j, me, (*chip, c), src=ins[m]) for j, chip in enumerate(chips) for m in range(nm)]
        return me, sibling, chips, copy, mine, first

    def start(ins, outs, sems):
        *_, mine, first = plan(ins, outs, sems)
        for cp in mine + first:
            cp.start()

    def finish(ins, outs, sems):
        me, sibling, chips, copy, mine, first = plan(ins, outs, sems)
        c = me[2]
        passed = []
        for j, chip in enumerate(chips):
            for m in range(nm):
                copy(m, 1 + j, (*chip, c), me).wait_recv()
                passed.append(copy(m, 4 + j, (*chip, c), sibling))
                passed[-1].start()
        for m in range(nm):
            copy(m, 0, sibling, me).wait_recv()
        for j, chip in enumerate(chips):
            for m in range(nm):
                copy(m, 4 + j, (*chip, 1 - c), me).wait_recv()
        for cp in first + passed:
            cp.wait_send()
        for cp in mine:
            cp.wait()

    return _Rider(
        arrays=list(blocks), out_shape=[jax.ShapeDtypeStruct((N_DEV * b.shape[0], b.shape[1]), b.dtype) for b in blocks],
        sems=[pltpu.SemaphoreType.DMA((7 * nm,)), pltpu.SemaphoreType.DMA((7 * nm,)), pltpu.SemaphoreType.DMA((nm,))],
        start=start, finish=finish)


def _exchange_rider(copies_of, arrays, out_shape, n_copies):
    def copies(ins, outs, sems):
        send_sems, recv_sems = sems
        return [pltpu.make_async_remote_copy(src_ref=src, dst_ref=dst, send_sem=send_sems.at[k], recv_sem=recv_sems.at[k],
                                             device_id=to, device_id_type=MESH)
                for k, (src, dst, to) in enumerate(copies_of(ins, outs))]

    def start(ins, outs, sems):
        for cp in copies(ins, outs, sems):
            cp.start()

    def finish(ins, outs, sems):
        cps = copies(ins, outs, sems)
        for cp in cps:
            cp.wait_recv()
        for cp in cps:
            cp.wait_send()

    return _Rider(arrays=list(arrays), out_shape=out_shape,
                  sems=[pltpu.SemaphoreType.DMA((n_copies,)), pltpu.SemaphoreType.DMA((n_copies,))], start=start, finish=finish)


def _sibling_rider(grads):
    def copies_of(ins, outs):
        x, y, c = _place()
        for g_ref, o_ref in zip(ins, outs):
            r = g_ref.shape[0] // N_DEV
            for q in range(N_CHIPS):
                yield g_ref.at[pl.ds((2 * q + 1 - c) * r, r), :], o_ref.at[pl.ds(q * r, r), :], (x, y, 1 - c)

    return _exchange_rider(copies_of, grads, [jax.ShapeDtypeStruct((g.shape[0] // 2, g.shape[1]), F32) for g in grads],
                           len(grads) * N_CHIPS)


def _chip_sum(grad, from_sibling, place):
    r = grad.shape[0] // N_DEV
    w = grad.shape[1]

    def body(place_ref, g_ref, s_ref, wire_ref, own_ref):
        total = g_ref[...] + s_ref[...]
        wire_ref[...] = total.astype(BF16)

        @pl.when(pl.program_id(0) == place_ref[1])
        def _():
            own_ref[...] = total

    grid_spec = pltpu.PrefetchScalarGridSpec(
        num_scalar_prefetch=1, grid=(N_CHIPS,),
        in_specs=[pl.BlockSpec((r, w), lambda q, p: (2 * q + p[0], 0)), pl.BlockSpec((r, w), lambda q, p: (q, 0))],
        out_specs=[pl.BlockSpec((r, w), lambda q, p: (q, 0)), pl.BlockSpec((r, w), lambda q, p: (0, 0))])
    return pl.pallas_call(
        body, name="grad_chip_sum", grid_spec=grid_spec,
        out_shape=[jax.ShapeDtypeStruct((N_CHIPS * r, w), BF16), jax.ShapeDtypeStruct((r, w), F32)],
        compiler_params=_params("arbitrary"),
    )(place, grad, from_sibling)


def _chips_rider(wires):
    def copies_of(ins, outs):
        x, y, c = _place()
        for w_ref, o_ref in zip(ins, outs):
            r = w_ref.shape[0] // N_CHIPS
            for j, (px, py) in enumerate(_other_chips(x, y)):
                yield w_ref.at[pl.ds((2 * px + py) * r, r), :], o_ref.at[pl.ds(j * r, r), :], (px, py, c)

    return _exchange_rider(copies_of, wires,
                           [jax.ShapeDtypeStruct((3 * (w.shape[0] // N_CHIPS), w.shape[1]), BF16) for w in wires], len(wires) * 3)


def _adamw_math(w, g, m, v):
    m = ADAM_B1 * m + (1.0 - ADAM_B1) * g
    v = ADAM_B2 * v + (1.0 - ADAM_B2) * jnp.square(g)
    m_hat = m / (1.0 - ADAM_B1 ** ADAM_STEP)
    v_hat = v / (1.0 - ADAM_B2 ** ADAM_STEP)
    delta = -ADAM_LR * (m_hat / (jnp.sqrt(v_hat) + ADAM_EPS) + ADAM_WD * w)
    return delta, m, v


def _reduce_adamw(own, received, w, m, v):
    r = own.shape[0]

    def body(own_ref, rec_ref, w_ref, m_ref, v_ref, g_ref, d_ref, nm_ref, nv_ref):
        g = ((own_ref[...] + rec_ref[0:r, :].astype(F32)) + rec_ref[r:2 * r, :].astype(F32)) + rec_ref[2 * r:, :].astype(F32)
        g_ref[...] = g
        d_ref[...], nm_ref[...], nv_ref[...] = _adamw_math(w_ref[...], g, m_ref[...], v_ref[...])

    shape = jax.ShapeDtypeStruct(own.shape, F32)
    return pl.pallas_call(
        body, name="reduce_adamw", in_specs=[VMEM] * 5, out_specs=[VMEM] * 4, out_shape=[shape] * 4,
        compiler_params=_params(),
    )(own, received, w, m, v)


SMALL_EARLY = (("w_pool", 65536), ("b_pool", 512), ("pool_scale", 512), ("b_out", 1024), ("g_ffn", 1024),
               ("g_final", 1024), ("loss", 1024))
SMALL_LATE = (("sinks", 8), ("g_mix", 1024), ("b_in", 1280))
SMALL = SMALL_EARLY + SMALL_LATE


def _small_rows(size):
    return -(-size // (8 * LANES)) * 8


def _pack_small(values, entries=SMALL):
    parts = []
    for name, size in entries:
        flat = values[name].reshape(-1).astype(F32)
        parts.append(jnp.pad(flat, (0, _small_rows(size) * LANES - size)).reshape(-1, LANES))
    return jnp.concatenate(parts, axis=0)


def _unpack_small(packed, shapes):
    out, row = {}, 0
    for name, size in SMALL:
        rows = _small_rows(size)
        if name in shapes:
            out[name] = packed[row:row + rows].reshape(-1)[:size].reshape(shapes[name])
        row += rows
    return out


def _small_sum_adamw(gathered_early, gathered_late, w, m, v):
    def body(e_ref, l_ref, w_ref, m_ref, v_ref, g_ref, d_ref, nm_ref, nv_ref):
        def total(ref):
            rows = ref.shape[0] // N_DEV
            acc = ref[0:rows, :]
            for dev in range(1, N_DEV):
                acc = acc + ref[dev * rows:(dev + 1) * rows, :]
            return acc

        g = jnp.concatenate([total(e_ref), total(l_ref)], axis=0)
        g_ref[...] = g
        d_ref[...], nm_ref[...], nv_ref[...] = _adamw_math(w_ref[...], g, m_ref[...], v_ref[...])

    shape = jax.ShapeDtypeStruct(w.shape, F32)
    return pl.pallas_call(
        body, name="small_sum_adamw", in_specs=[VMEM] * 5, out_specs=[VMEM] * 4, out_shape=[shape] * 4,
        compiler_params=_params(),
    )(gathered_early, gathered_late, w, m, v)


def kernel(x, g_mix, w_in, b_in, sinks, w_pool, b_pool, pool_scale, w_out, b_out, g_ffn, w_gate, w_up, w_down, g_final, loss_target, m_g_mix, m_w_in, m_b_in, m_sinks, m_w_pool, m_b_pool, m_pool_scale, m_w_out, m_b_out, m_g_ffn, m_w_gate, m_w_up, m_w_down, m_g_final, v_g_mix, v_w_in, v_b_in, v_sinks, v_w_pool, v_b_pool, v_pool_scale, v_w_out, v_b_out, v_g_ffn, v_w_gate, v_w_up, v_w_down, v_g_final):
    weights = dict(g_mix=g_mix, w_in=w_in, b_in=b_in, sinks=sinks, w_pool=w_pool, b_pool=b_pool, pool_scale=pool_scale,
                   w_out=w_out, b_out=b_out, g_ffn=g_ffn, w_gate=w_gate, w_up=w_up, w_down=w_down, g_final=g_final)
    mom1 = dict(g_mix=m_g_mix, w_in=m_w_in, b_in=m_b_in, sinks=m_sinks, w_pool=m_w_pool, b_pool=m_b_pool,
                pool_scale=m_pool_scale, w_out=m_w_out, b_out=m_b_out, g_ffn=m_g_ffn, w_gate=m_w_gate, w_up=m_w_up,
                w_down=m_w_down, g_final=m_g_final)
    mom2 = dict(g_mix=v_g_mix, w_in=v_w_in, b_in=v_b_in, sinks=v_sinks, w_pool=v_w_pool, b_pool=v_b_pool,
                pool_scale=v_pool_scale, w_out=v_w_out, b_out=v_b_out, g_ffn=v_g_ffn, w_gate=v_w_gate, w_up=v_w_up,
                w_down=v_w_down, g_final=v_g_final)
    order = ("g_mix", "w_in", "b_in", "sinks", "w_pool", "b_pool", "pool_scale", "w_out", "b_out", "g_ffn",
             "w_gate", "w_up", "w_down", "g_final")
    big = ("w_in", "w_out", "w_gate", "w_up", "w_down")
    transposed = ("w_in", "w_gate", "w_up")

    def row_shard(name, a):
        return a[0].T if name in transposed else a[0]

    shard = {n: row_shard(n, weights[n]).astype(BF16) for n in big}
    xs, target = x[0], loss_target[0]
    cos, sin = _rope_tables(xs.shape[0])
    wp_b = w_pool[0].astype(BF16)
    bp = b_pool.reshape(1, POOL_WIDTH)
    ps = pool_scale.reshape(1, POOL_WIDTH)
    g_fin = g_final.reshape(1, D_MODEL)
    px, py, pc = _place()
    place = jnp.stack([pc, 2 * px + py]).astype(jnp.int32)

    (win_t,) = _alone(_gather_rider([shard["w_in"]]), "gather_w_in")
    q, kz, vz, vt, mixed, pool, w_out_b, wg_t = _fwd_inproj(
        xs, g_mix, win_t, b_in, cos, sin, wp_b, bp, ps, rider=_gather_rider([shard["w_out"], shard["w_gate"]]))
    attn, lse, wu_t = _attn_fwd(q, kz, vt, sinks, rider=_gather_rider([shard["w_up"]]))
    x2, gate, up, act, wd = _fwd_outproj_ffn_act(attn, pool, w_out_b, b_out, xs, g_ffn, wg_t, wu_t,
                                                 rider=_gather_rider([shard["w_down"]]))
    dx3, dx3b, sq, dg_final, d_wd = _fwd_down_loss(act, x2, wd, g_fin, target)

    dx2, dx2b, dg_ffn, db_out, d_wg_t, d_wu_t, wd_sibling = _bwd_ffn(
        dx3b, dx3, gate, up, x2, wd, wg_t, wu_t, g_ffn, rider=_sibling_rider([d_wd]))
    wd_sum = _chip_sum(d_wd, wd_sibling, place)
    in_grads = [d_wg_t, d_wu_t]
    dattn, du, d_wout, d_wpool, d_bpool, d_pscale, wd_received, *in_sibling = _bwd_outproj_pool(
        dx2b, attn, pool, mixed, w_out_b, wp_b, bp, ps, rider=_join(_chips_rider([wd_sum[0]]), _sibling_rider(in_grads)))
    in_sums = [_chip_sum(g, s, place) for g, s in zip(in_grads, in_sibling)]
    small_early = _pack_small(dict(w_pool=d_wpool, b_pool=d_bpool, pool_scale=d_pscale, b_out=db_out, g_ffn=dg_ffn,
                                   g_final=dg_final, loss=sq), SMALL_EARLY)
    dq, dk, dv, d_sinks, *landed = _attn_bwd(
        q, kz, vz, dattn, lse, sinks,
        rider=_join(_chips_rider([wire for wire, _ in in_sums]), _sibling_rider([d_wout]), _gather_rider([small_early])))
    ffn_sums, ffn_received = in_sums + [wd_sum], landed[:2] + [wd_received]
    wout_sum = _chip_sum(d_wout, landed[2], place)
    gathered_early = landed[3]
    dx, d_win_t, d_bin, d_gmix = _bwd_inproj(dq, dk, dv, du, cos, sin, win_t, xs, g_mix, dx2)
    (win_sibling,) = _alone(_sibling_rider([d_win_t]), "grad_exchange_sibling")
    win_sum = _chip_sum(d_win_t, win_sibling, place)
    small_late = _pack_small(dict(sinks=d_sinks, g_mix=d_gmix, b_in=d_bin), SMALL_LATE)
    wout_received, win_received, gathered_late = _alone(
        _join(_chips_rider([wout_sum[0], win_sum[0]]), _gather_rider([small_late])), "grad_exchange_chips")

    reduced = dict(zip(("w_gate", "w_up", "w_down", "w_out", "w_in"),
                       zip(ffn_sums + [wout_sum, win_sum], list(ffn_received) + [wout_received, win_received])))
    grad, delta, new_m, new_v = {}, {}, {}, {}
    for n in big:
        (_, own), rec = reduced[n]
        results = _reduce_adamw(own, rec, row_shard(n, weights[n]), row_shard(n, mom1[n]), row_shard(n, mom2[n]))
        grad[n], delta[n], new_m[n], new_v[n] = [(a.T if n in transposed else a)[None] for a in results]

    shapes = {n: weights[n].shape for n in order if n not in big}
    zero_loss = jnp.zeros((1, D_MODEL), F32)
    packed = _small_sum_adamw(
        gathered_early, gathered_late, _pack_small({**weights, "loss": zero_loss}),
        _pack_small({**mom1, "loss": zero_loss}), _pack_small({**mom2, "loss": zero_loss}))
    for store, pk in zip((grad, delta, new_m, new_v), packed):
        store.update(_unpack_small(pk, shapes))
    loss_rows = _unpack_small(packed[0], {"loss": (D_MODEL,)})["loss"]
    loss = (0.5 / D_MODEL) * jnp.sum(loss_rows)

    return (loss, dx[None], *[grad[n] for n in order], *[delta[n] for n in order],
            *[new_m[n] for n in order], *[new_v[n] for n in order])
```

```python
from typing import Any, Callable, NamedTuple, Sequence

import jax
import jax.numpy as jnp
from jax import lax
from jax.experimental import pallas as pl
from jax.experimental.pallas import tpu as pltpu

D_MODEL = 1024
ATTN_WIDTH = 512
KV_WIDTH = 128
POOL_WIDTH = 512
HEAD_DIM = 64
N_Q_HEADS = 8
N_KV_HEADS = 2
GQA_GROUP = 4
BLOCK = 128
POOL_SIZES = (2, 4, 8, 16)
POOL_GROUP_WIDTH = 128
POOL_HALO = 16
IN_WIDTH = 1280
D_FF = 2816
RMS_EPS = 1e-5
ROPE_THETA = 10000.0
Q_SCALE = HEAD_DIM ** -0.5

ADAM_LR = 0.001
ADAM_B1 = 0.9
ADAM_B2 = 0.999
ADAM_EPS = 1e-08
ADAM_WD = 0.01
ADAM_STEP = 10

N_DEV = 8
N_CHIPS = 4
LANES = 128
VMEM_LIMIT_BYTES = 60 * 1024 * 1024

F32 = jnp.float32
BF16 = jnp.bfloat16
MESH = pl.DeviceIdType.MESH
HBM = pl.BlockSpec(memory_space=pltpu.HBM)
VMEM = pl.BlockSpec(memory_space=pltpu.VMEM)


def _params(*semantics):
    return pltpu.CompilerParams(dimension_semantics=semantics or None, vmem_limit_bytes=VMEM_LIMIT_BYTES)


def _nn(a, b):
    return jnp.dot(a, b, preferred_element_type=F32)


def _nt(a, b):
    return lax.dot_general(a, b, (((1,), (1,)), ((), ())), preferred_element_type=F32)


def _tn(a, b):
    return lax.dot_general(a, b, (((0,), (0,)), ((), ())), preferred_element_type=F32)


def _full(shape):
    return pl.BlockSpec(shape, lambda *_: (0,) * len(shape))


def _rows(tm, width):
    return pl.BlockSpec((tm, width), lambda i, *_: (i, 0))


def _nothing(ins, outs, sems):
    del ins, outs, sems


RELAY_STEPS_BEFORE_LAST = 2


class _Rider(NamedTuple):
    arrays: Sequence[Any]
    out_shape: Sequence[Any]
    sems: Sequence[Any]
    start: Callable[..., None]
    finish: Callable[..., None]
    relay: Callable[..., None] = _nothing


def _gridded(body, rider, *, name, grid, in_specs, out_specs, out_shape, scratch_shapes, args):
    params = _params("arbitrary")
    if rider is None:
        return pl.pallas_call(body, name=name, grid=grid, in_specs=in_specs, out_specs=out_specs, out_shape=out_shape,
                              scratch_shapes=scratch_shapes, compiler_params=params)(*args)
    bounds, total = [], 0
    for n in (len(in_specs), len(rider.arrays), len(out_specs), len(rider.out_shape), len(scratch_shapes), len(rider.sems)):
        bounds.append((total, total + n))
        total += n
    last = grid[0] - 1
    relay_step = max(last - RELAY_STEPS_BEFORE_LAST, 0)

    def riding(*refs):
        ins, r_ins, outs, r_outs, scratch, r_sems = (refs[lo:hi] for lo, hi in bounds)

        @pl.when(pl.program_id(0) == 0)
        def _():
            rider.start(r_ins, r_outs, r_sems)

        body(*ins, *outs, *scratch)

        @pl.when(pl.program_id(0) == relay_step)
        def _():
            rider.relay(r_ins, r_outs, r_sems)

        @pl.when(pl.program_id(0) == last)
        def _():
            rider.finish(r_ins, r_outs, r_sems)

    return pl.pallas_call(
        riding, name=name, grid=grid, in_specs=list(in_specs) + [HBM] * len(rider.arrays),
        out_specs=list(out_specs) + [HBM] * len(rider.out_shape), out_shape=list(out_shape) + list(rider.out_shape),
        scratch_shapes=list(scratch_shapes) + list(rider.sems), compiler_params=params)(*args, *rider.arrays)


def _join(*riders):
    def phase(which):
        def run(ins, outs, sems):
            i = o = s = 0
            for r in riders:
                ni, no, ns = len(r.arrays), len(r.out_shape), len(r.sems)
                getattr(r, which)(ins[i:i + ni], outs[o:o + no], sems[s:s + ns])
                i, o, s = i + ni, o + no, s + ns
        return run

    return _Rider(arrays=[a for r in riders for a in r.arrays], out_shape=[a for r in riders for a in r.out_shape],
                  sems=[a for r in riders for a in r.sems], start=phase("start"), finish=phase("finish"), relay=phase("relay"))


def _alone(rider, name):
    n_in, n_out = len(rider.arrays), len(rider.out_shape)

    def body(*refs):
        parts = refs[:n_in], refs[n_in:n_in + n_out], refs[n_in + n_out:]
        rider.start(*parts)
        rider.relay(*parts)
        rider.finish(*parts)

    return pl.pallas_call(body, name=name, in_specs=[HBM] * n_in, out_specs=[HBM] * n_out, out_shape=list(rider.out_shape),
                          scratch_shapes=list(rider.sems))(*rider.arrays)


def _rot_half(t):
    n = t.shape[1]
    lane = lax.broadcasted_iota(jnp.int32, t.shape, 1)
    return jnp.where((lane % HEAD_DIM) < HEAD_DIM // 2, pltpu.roll(t, n - HEAD_DIM // 2, 1), pltpu.roll(t, HEAD_DIM // 2, 1))


def _rope(t, cos, sin):
    reps = t.shape[1] // LANES
    if reps > 1:
        cos, sin = jnp.tile(cos, (1, reps)), jnp.tile(sin, (1, reps))
    return t * cos + _rot_half(t) * sin


def _rope_bwd(d, cos, sin):
    reps = d.shape[1] // LANES
    if reps > 1:
        cos, sin = jnp.tile(cos, (1, reps)), jnp.tile(sin, (1, reps))
    return d * cos + _rot_half(d * sin)


KV_SPREAD = 4 * LANES


def _spread_kv(t):
    low = lax.broadcasted_iota(jnp.int32, t.shape, 1) < HEAD_DIM
    swapped = pltpu.roll(t, HEAD_DIM, 1)
    zero = jnp.zeros_like(t)
    return jnp.concatenate([jnp.where(low, t, zero), jnp.where(low, zero, swapped),
                            jnp.where(low, swapped, zero), jnp.where(low, zero, t)], axis=1)


def _rms(x):
    r = lax.rsqrt(jnp.mean(x * x, axis=-1, keepdims=True) + RMS_EPS)
    return x * r, r


def _rms_bwd(dh, n, r, g):
    dn = dh * g
    dx = r * (dn - n * jnp.mean(dn * n, axis=-1, keepdims=True))
    return dx, jnp.sum(dh * n, axis=0, keepdims=True)


def _token_tile(s):
    return min(512, s)


def _fwd_inproj(x, g_mix, win_t, b_in, cos, sin, w_pool, b_pool, pool_scale, rider=None):
    s = x.shape[0]
    tm = _token_tile(s)

    def body(x_ref, g_ref, w_ref, b_ref, cos_ref, sin_ref, wp_ref, bp_ref, ps_ref,
             q_ref, k_ref, v_ref, vt_ref, mix_ref, pool_ref, tail_ref):
        i = pl.program_id(0)

        @pl.when(i == 0)
        def _():
            tail_ref[...] = jnp.zeros_like(tail_ref)

        n, _ = _rms(x_ref[...])
        h = (n * g_ref[...]).astype(BF16)
        z = _nt(h, w_ref[...]) + b_ref[...]
        cos_t, sin_t = cos_ref[...], sin_ref[...]
        q_ref[...] = (_rope(z[:, :ATTN_WIDTH], cos_t, sin_t) * Q_SCALE).astype(BF16)
        k_ref[...] = _spread_kv(_rope(z[:, ATTN_WIDTH:ATTN_WIDTH + KV_WIDTH], cos_t, sin_t)).astype(BF16)
        vz = _spread_kv(z[:, ATTN_WIDTH + KV_WIDTH:ATTN_WIDTH + 2 * KV_WIDTH])
        v_ref[...] = vz.astype(BF16)
        vt_ref[...] = vz.T.astype(BF16)
        u = z[:, ATTN_WIDTH + 2 * KV_WIDTH:]
        u_ext = jnp.concatenate([tail_ref[...], u], axis=0)
        tail_ref[...] = u[tm - POOL_HALO:, :]
        pos = lax.broadcasted_iota(jnp.int32, (tm, POOL_GROUP_WIDTH), 0) + i * tm
        for g, size in enumerate(POOL_SIZES):
            cols = slice(g * POOL_GROUP_WIDTH, (g + 1) * POOL_GROUP_WIDTH)
            a = u_ext[:, cols]
            shift = 1
            while shift < size:
                a = a + pltpu.roll(a, shift, 0)
                shift *= 2
            count = jnp.minimum(pos + 1, size).astype(F32)
            mixed = (a[POOL_HALO:, :] / count - u[:, cols]).astype(BF16)
            pre = _nn(mixed, wp_ref[g]) + bp_ref[:, cols]
            mix_ref[:, cols] = mixed
            pool_ref[:, cols] = (pre * ps_ref[:, cols]).astype(BF16)

    bf = lambda w: jax.ShapeDtypeStruct((s, w), BF16)
    return _gridded(
        body, rider, name="fwd_inproj", grid=(s // tm,),
        in_specs=[_rows(tm, D_MODEL), _full((1, D_MODEL)), _full((IN_WIDTH, D_MODEL)), _full((1, IN_WIDTH)),
                  _rows(tm, LANES), _rows(tm, LANES), _full((4, POOL_GROUP_WIDTH, POOL_GROUP_WIDTH)),
                  _full((1, POOL_WIDTH)), _full((1, POOL_WIDTH))],
        out_specs=[_rows(tm, ATTN_WIDTH), _rows(tm, KV_SPREAD), _rows(tm, KV_SPREAD),
                   pl.BlockSpec((KV_SPREAD, tm), lambda i: (0, i)), _rows(tm, POOL_WIDTH), _rows(tm, POOL_WIDTH)],
        out_shape=[bf(ATTN_WIDTH), bf(KV_SPREAD), bf(KV_SPREAD), jax.ShapeDtypeStruct((KV_SPREAD, s), BF16),
                   bf(POOL_WIDTH), bf(POOL_WIDTH)],
        scratch_shapes=[pltpu.VMEM((POOL_HALO, POOL_WIDTH), F32)],
        args=(x, g_mix, win_t, b_in, cos, sin, w_pool, b_pool, pool_scale))


ATTN_TILE = 512
PAIR = 2 * LANES


def _band_masks(tile):
    j = lax.broadcasted_iota(jnp.int32, (4 * BLOCK, 2 * BLOCK), 0) % (2 * BLOCK)
    r = lax.broadcasted_iota(jnp.int32, (4 * BLOCK, 2 * BLOCK), 1) % BLOCK
    band = (j > r) & (j <= r + BLOCK)
    return band & ((tile > 0) | (j >= BLOCK)), band


def _band(cur_ref, prev_ref, b, kv):
    halves = []
    for half in range(2):
        cols = slice(kv * PAIR + half * LANES, kv * PAIR + (half + 1) * LANES)
        if b == 0:
            halves.append(jnp.concatenate([prev_ref[:, cols], cur_ref[0:BLOCK, cols]], axis=0))
        else:
            halves.append(cur_ref[(b - 1) * BLOCK:(b + 1) * BLOCK, cols])
    return jnp.concatenate(halves, axis=0)


def _stack_pair(ref, rows, kv):
    return jnp.concatenate([ref[rows, kv * PAIR:kv * PAIR + LANES], ref[rows, kv * PAIR + LANES:(kv + 1) * PAIR]], axis=0)


def _pair_heads(kv, half):
    return GQA_GROUP * kv + half, GQA_GROUP * kv + 2 + half


def _band_t(cur_ref, prev_ref, b, kv):
    halves = []
    for half in range(2):
        lanes = slice(kv * PAIR + half * LANES, kv * PAIR + (half + 1) * LANES)
        if b == 0:
            halves.append(jnp.concatenate([prev_ref[lanes, :], cur_ref[lanes, 0:BLOCK]], axis=1))
        else:
            halves.append(cur_ref[lanes, (b - 1) * BLOCK:(b + 1) * BLOCK])
    return jnp.concatenate(halves, axis=1)


def _reduce_rows(x, op, reduce):
    while x.shape[0] > 8:
        half = x.shape[0] // 2
        x = op(x[:half], x[half:])
    return reduce(x, axis=0, keepdims=True)


def _per_query(ref, rows, top, bottom):
    return jnp.concatenate([ref[top:top + 1, rows], ref[bottom:bottom + 1, rows]], axis=1)


def _sink_per_query(sink_ref, top, bottom):
    first_slab = lax.broadcasted_iota(jnp.int32, (1, 2 * BLOCK), 1) < BLOCK
    return jnp.where(first_slab, sink_ref[:, top:top + 1], sink_ref[:, bottom:bottom + 1])


def _attn_fwd(q, kz, vt, sinks, rider=None):
    s = q.shape[0]
    tq = min(ATTN_TILE, s)

    def body(q_ref, k_ref, kp_ref, vt_ref, vtp_ref, sink_ref, o_ref, lse_ref):
        first, band = _band_masks(pl.program_id(0))
        chains = [(b, kv) for b in range(tq // BLOCK) for kv in range(N_KV_HEADS)]

        def scores(b, kv):
            rows = slice(b * BLOCK, (b + 1) * BLOCK)
            return _nt(_band(k_ref, kp_ref, b, kv), _stack_pair(q_ref, rows, kv))

        def store(b, kv, ot):
            rows = slice(b * BLOCK, (b + 1) * BLOCK)
            o = ot.T.astype(BF16)
            o_ref[rows, kv * PAIR:kv * PAIR + LANES] = o[:BLOCK]
            o_ref[rows, kv * PAIR + LANES:(kv + 1) * PAIR] = o[BLOCK:]

        ahead = scores(*chains[0])
        behind = None
        for n, (b, kv) in enumerate(chains):
            rows = slice(b * BLOCK, (b + 1) * BLOCK)
            st = jnp.where(first if b == 0 else band, ahead, -jnp.inf)
            if n + 1 < len(chains):
                ahead = scores(*chains[n + 1])
            probs = []
            for half in range(2):
                top, bottom = _pair_heads(kv, half)
                sink = _sink_per_query(sink_ref, top, bottom)
                sh = st[half * 2 * BLOCK:(half + 1) * 2 * BLOCK, :]
                m = jnp.maximum(_reduce_rows(sh, jnp.maximum, jnp.max), sink)
                p = jnp.exp(sh - m)
                denom = _reduce_rows(p, jnp.add, jnp.sum) + jnp.exp(sink - m)
                probs.append((p * (1.0 / denom)).astype(BF16))
                lse = m + jnp.log(denom)
                lse_ref[top:top + 1, rows] = lse[:, :BLOCK]
                lse_ref[bottom:bottom + 1, rows] = lse[:, BLOCK:]
            ot = _nn(_band_t(vt_ref, vtp_ref, b, kv), jnp.concatenate(probs, axis=0))
            if behind is not None:
                store(*behind)
            behind = (b, kv, ot)
        store(*behind)

    per = tq // BLOCK
    cur = lambda w: pl.BlockSpec((tq, w), lambda i: (i, 0))
    prev = pl.BlockSpec((BLOCK, KV_SPREAD), lambda i: (jnp.maximum(per * i - 1, 0), 0))
    cur_t = pl.BlockSpec((KV_SPREAD, tq), lambda i: (0, i))
    prev_t = pl.BlockSpec((KV_SPREAD, BLOCK), lambda i: (0, jnp.maximum(per * i - 1, 0)))
    return _gridded(
        body, rider, name="attn_fwd", grid=(s // tq,),
        in_specs=[cur(ATTN_WIDTH), cur(KV_SPREAD), prev, cur_t, prev_t, _full((1, N_Q_HEADS))],
        out_specs=[cur(ATTN_WIDTH), pl.BlockSpec((N_Q_HEADS, tq), lambda i: (0, i))],
        out_shape=[jax.ShapeDtypeStruct((s, ATTN_WIDTH), BF16), jax.ShapeDtypeStruct((N_Q_HEADS, s), F32)],
        scratch_shapes=[], args=(q, kz, kz, vt, vt, sinks))


FF_CHUNK = 256
TN_ROW_CHUNK = 256


def _resident(shape):
    return pl.BlockSpec(shape, lambda *_: (0,) * len(shape), pipeline_mode=pl.Buffered(1))


def _accumulate_tn(acc_ref, a_ref, b):
    for m0 in range(0, acc_ref.shape[0], TN_ROW_CHUNK):
        acc_ref[m0:m0 + TN_ROW_CHUNK, :] += _tn(a_ref[:, m0:m0 + TN_ROW_CHUNK], b)


def _fwd_outproj_ffn_act(attn, pool, w_out, b_out, x, g_ffn, wg_t, wu_t, rider=None):
    s = x.shape[0]
    tm = _token_tile(s)

    def body(a_ref, p_ref, w_ref, b_ref, x_ref, g_ref, wg_ref, wu_ref, x2_ref, gate_ref, up_ref, act_ref):
        x2 = x_ref[...] + _nn(a_ref[...], w_ref[:ATTN_WIDTH, :]) + _nn(p_ref[...], w_ref[ATTN_WIDTH:, :]) + b_ref[...]
        x2_ref[...] = x2
        n, _ = _rms(x2)
        h = (n * g_ref[...]).astype(BF16)

        def products(c0):
            return _nt(h, wg_ref[c0:c0 + FF_CHUNK, :]), _nt(h, wu_ref[c0:c0 + FF_CHUNK, :])

        ahead = products(0)
        for c0 in range(0, D_FF, FF_CHUNK):
            cols = slice(c0, c0 + FF_CHUNK)
            gate, up = ahead
            if c0 + FF_CHUNK < D_FF:
                ahead = products(c0 + FF_CHUNK)
            gate_ref[:, cols] = gate.astype(BF16)
            up_ref[:, cols] = up.astype(BF16)
            act_ref[:, cols] = (gate * jax.nn.sigmoid(gate) * up).astype(BF16)

    act_shape = jax.ShapeDtypeStruct((s, D_FF), BF16)
    return _gridded(
        body, rider, name="fwd_outproj_ffn_act", grid=(s // tm,),
        in_specs=[_rows(tm, ATTN_WIDTH), _rows(tm, POOL_WIDTH), _resident((D_MODEL, D_MODEL)), _full((1, D_MODEL)),
                  _rows(tm, D_MODEL), _full((1, D_MODEL)), _resident((D_FF, D_MODEL)), _resident((D_FF, D_MODEL))],
        out_specs=[_rows(tm, D_MODEL)] + [_rows(tm, D_FF)] * 3,
        out_shape=[jax.ShapeDtypeStruct((s, D_MODEL), F32)] + [act_shape] * 3,
        scratch_shapes=[], args=(attn, pool, w_out, b_out, x, g_ffn, wg_t, wu_t))


def _fwd_down_loss(act, x2, wd, g_final, target):
    s = x2.shape[0]
    tm = _token_tile(s)
    last = s // tm - 1

    def body(a_ref, x2_ref, wd_ref, g_ref, t_ref, dx3_ref, sq_ref, dg_ref, dwd_ref, acc_ref, sem):
        @pl.when(pl.program_id(0) == 0)
        def _():
            sq_ref[...] = jnp.zeros_like(sq_ref)
            dg_ref[...] = jnp.zeros_like(dg_ref)
            acc_ref[...] = jnp.zeros_like(acc_ref)

        x3 = x2_ref[...] + _nn(a_ref[...], wd_ref[...])
        n, r = _rms(x3)
        g = g_ref[...]
        diff = n * g - t_ref[...]
        sq_ref[...] += jnp.sum(diff * diff, axis=0, keepdims=True)
        dx3, dg = _rms_bwd(diff * (1.0 / D_MODEL), n, r, g)
        dg_ref[...] += dg
        dx3_ref[...] = dx3
        _accumulate_tn(acc_ref, a_ref, dx3.astype(BF16))

        @pl.when(pl.program_id(0) == last)
        def _():
            out = pltpu.make_async_copy(acc_ref, dwd_ref, sem)
            out.start()
            out.wait()

    return pl.pallas_call(
        body, name="fwd_down_loss", grid=(s // tm,),
        in_specs=[_rows(tm, D_FF), _rows(tm, D_MODEL), _resident((D_FF, D_MODEL)), _full((1, D_MODEL)), _rows(tm, D_MODEL)],
        out_specs=[_rows(tm, D_MODEL), _full((1, D_MODEL)), _full((1, D_MODEL)), HBM],
        out_shape=[jax.ShapeDtypeStruct((s, D_MODEL), F32),
                   jax.ShapeDtypeStruct((1, D_MODEL), F32), jax.ShapeDtypeStruct((1, D_MODEL), F32),
                   jax.ShapeDtypeStruct((D_FF, D_MODEL), F32)],
        scratch_shapes=[pltpu.VMEM((D_FF, D_MODEL), F32), pltpu.SemaphoreType.DMA],
        compiler_params=_params("arbitrary"),
    )(act, x2, wd, g_final, target)


FFN_BWD_TILE = 256


def _bwd_ffn(dx3, gate, up, x2, wd, wg_t, wu_t, g_ffn, rider=None):
    s = x2.shape[0]
    tm = min(FFN_BWD_TILE, s)
    last = s // tm - 1

    def body(dx3_ref, gate_ref, up_ref, x2_ref, wd_ref, wg_ref, wu_ref, g_ref,
             dx2_ref, dg_ref, db_ref, dwg_ref, dwu_ref, dgate_ref, dup_ref, accg_ref, accu_ref, sems):
        @pl.when(pl.program_id(0) == 0)
        def _():
            dg_ref[...] = jnp.zeros_like(dg_ref)
            db_ref[...] = jnp.zeros_like(db_ref)
            accg_ref[...] = jnp.zeros_like(accg_ref)
            accu_ref[...] = jnp.zeros_like(accu_ref)

        dx3b = dx3_ref[...].astype(BF16)
        g = g_ref[...]
        n, r = _rms(x2_ref[...])
        h = (n * g).astype(BF16)
        ahead = _nt(dx3b, wd_ref[0:FF_CHUNK, :])
        for c0 in range(0, D_FF, FF_CHUNK):
            cols = slice(c0, c0 + FF_CHUNK)
            dact = ahead
            if c0 + FF_CHUNK < D_FF:
                ahead = _nt(dx3b, wd_ref[c0 + FF_CHUNK:c0 + 2 * FF_CHUNK, :])
            gate = gate_ref[:, cols].astype(F32)
            up = up_ref[:, cols].astype(F32)
            sig = jax.nn.sigmoid(gate)
            silu = gate * sig
            dup = (dact * silu).astype(BF16)
            dgate = (dact * up * (sig + silu * (1.0 - sig))).astype(BF16)
            dup_ref[:, cols] = dup
            dgate_ref[:, cols] = dgate
            accg_ref[cols, :] += _tn(dgate, h)
            accu_ref[cols, :] += _tn(dup, h)
        dh2 = _nn(dgate_ref[...], wg_ref[...]) + _nn(dup_ref[...], wu_ref[...])
        dx, dg = _rms_bwd(dh2, n, r, g)
        dx2 = dx3_ref[...] + dx
        dg_ref[...] += dg
        db_ref[...] += jnp.sum(dx2, axis=0, keepdims=True)
        dx2_ref[...] = dx2

        @pl.when(pl.program_id(0) == last)
        def _():
            outs = [pltpu.make_async_copy(accg_ref, dwg_ref, sems.at[0]), pltpu.make_async_copy(accu_ref, dwu_ref, sems.at[1])]
            for cp in outs:
                cp.start()
            for cp in outs:
                cp.wait()

    grad_shape = jax.ShapeDtypeStruct((D_FF, D_MODEL), F32)
    weight = _resident((D_FF, D_MODEL))
    return _gridded(
        body, rider, name="bwd_ffn", grid=(s // tm,),
        in_specs=[_rows(tm, D_MODEL), _rows(tm, D_FF), _rows(tm, D_FF),
                  _rows(tm, D_MODEL), weight, weight, weight, _full((1, D_MODEL))],
        out_specs=[_rows(tm, D_MODEL), _full((1, D_MODEL)), _full((1, D_MODEL)), HBM, HBM],
        out_shape=[jax.ShapeDtypeStruct((s, D_MODEL), F32),
                   jax.ShapeDtypeStruct((1, D_MODEL), F32), jax.ShapeDtypeStruct((1, D_MODEL), F32), grad_shape, grad_shape],
        scratch_shapes=[pltpu.VMEM((tm, D_FF), BF16), pltpu.VMEM((tm, D_FF), BF16),
                        pltpu.VMEM((D_FF, D_MODEL), F32), pltpu.VMEM((D_FF, D_MODEL), F32), pltpu.SemaphoreType.DMA((2,))],
        args=(dx3, gate, up, x2, wd, wg_t, wu_t, g_ffn))


def _bwd_outproj_pool(dx2, attn, pool, mixed, w_out, w_pool, b_pool, pool_scale, rider=None):
    s = dx2.shape[0]
    tm = _token_tile(s)
    nt = s // tm

    def body(dx_ref, a_ref, p_ref, mix_ref, w_ref, wp_ref, bp_ref, ps_ref,
             dattn_ref, du_ref, dwout_ref, dwp_ref, dbp_ref, dps_ref, head_ref):
        step = pl.program_id(0)
        tile = nt - 1 - step

        @pl.when(step == 0)
        def _():
            head_ref[...] = jnp.zeros_like(head_ref)
            dwout_ref[...] = jnp.zeros_like(dwout_ref)
            dwp_ref[...] = jnp.zeros_like(dwp_ref)
            dbp_ref[...] = jnp.zeros_like(dbp_ref)
            dps_ref[...] = jnp.zeros_like(dps_ref)

        dx = dx_ref[...].astype(BF16)
        dwout_ref[:ATTN_WIDTH, :] += _tn(a_ref[...], dx)
        dwout_ref[ATTN_WIDTH:, :] += _tn(p_ref[...], dx)
        dcat = _nt(dx, w_ref[...])
        dattn_ref[...] = dcat[:, :ATTN_WIDTH].astype(BF16)
        dpool = dcat[:, ATTN_WIDTH:]
        pos = lax.broadcasted_iota(jnp.int32, (tm, POOL_GROUP_WIDTH), 0) + tile * tm
        head = head_ref[...]
        n_ext = tm + POOL_HALO
        for g, size in enumerate(POOL_SIZES):
            cols = slice(g * POOL_GROUP_WIDTH, (g + 1) * POOL_GROUP_WIDTH)
            mixed_g = mix_ref[:, cols]
            pre = _nn(mixed_g, wp_ref[g]) + bp_ref[:, cols]
            dy = dpool[:, cols]
            dps_ref[:, cols] += jnp.sum(dy * pre, axis=0, keepdims=True)
            dpre = dy * ps_ref[:, cols]
            dbp_ref[:, cols] += jnp.sum(dpre, axis=0, keepdims=True)
            dpre_b = dpre.astype(BF16)
            dwp_ref[g] += _tn(mixed_g, dpre_b)
            dmixed = _nt(dpre_b, wp_ref[g])
            w = dmixed / jnp.minimum(pos + 1, size).astype(F32)
            head_ref[:, cols] = w[:POOL_HALO, :]
            a = jnp.concatenate([w, head[:, cols]], axis=0)
            shift = 1
            while shift < size:
                a = a + pltpu.roll(a, n_ext - shift, 0)
                shift *= 2
            du_ref[:, cols] = (a[:tm, :] - dmixed).astype(BF16)

    rev = lambda w: pl.BlockSpec((tm, w), lambda i: (nt - 1 - i, 0))
    return _gridded(
        body, rider, name="bwd_outproj_pool", grid=(nt,),
        in_specs=[rev(D_MODEL), rev(ATTN_WIDTH), rev(POOL_WIDTH), rev(POOL_WIDTH), _full((D_MODEL, D_MODEL)),
                  _full((4, POOL_GROUP_WIDTH, POOL_GROUP_WIDTH)), _full((1, POOL_WIDTH)), _full((1, POOL_WIDTH))],
        out_specs=[rev(ATTN_WIDTH), rev(POOL_WIDTH), _full((D_MODEL, D_MODEL)),
                   _full((4, POOL_GROUP_WIDTH, POOL_GROUP_WIDTH)), _full((1, POOL_WIDTH)), _full((1, POOL_WIDTH))],
        out_shape=[jax.ShapeDtypeStruct((s, ATTN_WIDTH), BF16), jax.ShapeDtypeStruct((s, POOL_WIDTH), BF16),
                   jax.ShapeDtypeStruct((D_MODEL, D_MODEL), F32),
                   jax.ShapeDtypeStruct((4, POOL_GROUP_WIDTH, POOL_GROUP_WIDTH), F32),
                   jax.ShapeDtypeStruct((1, POOL_WIDTH), F32), jax.ShapeDtypeStruct((1, POOL_WIDTH), F32)],
        scratch_shapes=[pltpu.VMEM((POOL_HALO, POOL_WIDTH), F32)],
        args=(dx2, attn, pool, mixed, w_out, w_pool, b_pool, pool_scale))


def _fold_spread(t):
    low = lax.broadcasted_iota(jnp.int32, (2 * BLOCK, LANES), 1) < HEAD_DIM
    kept = jnp.where(low, t[:2 * BLOCK, :], t[2 * BLOCK:, :])
    return kept + pltpu.roll(kept, HEAD_DIM, 1)


def _attn_bwd(q, kz, vz, dattn, lse, sinks, rider=None):
    s = q.shape[0]
    tq = min(ATTN_TILE, s)
    nt = s // tq
    per = tq // BLOCK

    def body(q_ref, k_ref, kp_ref, v_ref, vp_ref, do_ref, lse_ref, sink_ref,
             dq_ref, dk_ref, dv_ref, dsink_ref, dk_acc, dv_acc, dk_carry, dv_carry):
        step = pl.program_id(0)

        @pl.when(step == 0)
        def _():
            dk_carry[...] = jnp.zeros_like(dk_carry)
            dv_carry[...] = jnp.zeros_like(dv_carry)
            dsink_ref[...] = jnp.zeros_like(dsink_ref)

        dk_acc[0:tq, :] = jnp.zeros((tq, KV_WIDTH), F32)
        dv_acc[0:tq, :] = jnp.zeros((tq, KV_WIDTH), F32)
        dk_acc[tq:, :] = dk_carry[...]
        dv_acc[tq:, :] = dv_carry[...]
        first, band = _band_masks(nt - 1 - step)
        low = lax.broadcasted_iota(jnp.int32, (2 * BLOCK, LANES), 1) < HEAD_DIM
        chains = [(b, kv) for b in range(per) for kv in range(N_KV_HEADS)]

        def operands(b, kv):
            rows = slice(b * BLOCK, (b + 1) * BLOCK)
            qab = _stack_pair(q_ref, rows, kv)
            doab = _stack_pair(do_ref, rows, kv)
            kzb = _band(k_ref, kp_ref, b, kv)
            return qab, doab, kzb, _nt(kzb, qab), _nt(_band(v_ref, vp_ref, b, kv), doab)

        folded = {}

        def finish(b, kv, dqab, dkz, dvz):
            rows = slice(b * BLOCK, (b + 1) * BLOCK)
            dq_ref[rows, kv * PAIR:kv * PAIR + LANES] = dqab[:BLOCK] * Q_SCALE
            dq_ref[rows, kv * PAIR + LANES:(kv + 1) * PAIR] = dqab[BLOCK:] * Q_SCALE
            folded[kv] = (_fold_spread(dkz), _fold_spread(dvz))
            if kv == N_KV_HEADS - 1:
                band_rows = slice(b * BLOCK, (b + 2) * BLOCK)
                dk_acc[band_rows, :] += jnp.where(low, folded[0][0], folded[1][0])
                dv_acc[band_rows, :] += jnp.where(low, folded[0][1], folded[1][1])

        ahead = operands(*chains[0])
        behind = None
        for n, (b, kv) in enumerate(chains):
            rows = slice(b * BLOCK, (b + 1) * BLOCK)
            mask = first if b == 0 else band
            qab, doab, kzb, st, dpt = ahead
            if n + 1 < len(chains):
                ahead = operands(*chains[n + 1])
            probs, dscores = [], []
            for half in range(2):
                top, bottom = _pair_heads(kv, half)
                keys = slice(half * 2 * BLOCK, (half + 1) * 2 * BLOCK)
                lse_h = _per_query(lse_ref, rows, top, bottom)
                p = jnp.where(mask[keys, :], jnp.exp(st[keys, :] - lse_h), 0.0)
                dph = dpt[keys, :]
                delta = _reduce_rows(p * dph, jnp.add, jnp.sum)
                probs.append(p.astype(BF16))
                dscores.append((p * (dph - delta)).astype(BF16))
                leak = jnp.exp(_sink_per_query(sink_ref, top, bottom) - lse_h) * delta
                dsink_ref[:, top:top + 1] -= jnp.sum(leak[:, :BLOCK], axis=1, keepdims=True)
                dsink_ref[:, bottom:bottom + 1] -= jnp.sum(leak[:, BLOCK:], axis=1, keepdims=True)
            ds = jnp.concatenate(dscores, axis=0)
            results = (_tn(ds, kzb), _nn(ds, qab), _nn(jnp.concatenate(probs, axis=0), doab))
            if behind is not None:
                finish(*behind)
            behind = (b, kv, *results)
        finish(*behind)
        dk_ref[...] = dk_acc[BLOCK:, :]
        dv_ref[...] = dv_acc[BLOCK:, :]
        dk_carry[...] = dk_acc[0:BLOCK, :]
        dv_carry[...] = dv_acc[0:BLOCK, :]

    cur = lambda w: pl.BlockSpec((tq, w), lambda i: (nt - 1 - i, 0))
    prev = pl.BlockSpec((BLOCK, KV_SPREAD), lambda i: (jnp.maximum(per * (nt - 1 - i) - 1, 0), 0))
    acc = pltpu.VMEM((tq + BLOCK, KV_WIDTH), F32)
    carry = pltpu.VMEM((BLOCK, KV_WIDTH), F32)
    return _gridded(
        body, rider, name="attn_bwd", grid=(nt,),
        in_specs=[cur(ATTN_WIDTH), cur(KV_SPREAD), prev, cur(KV_SPREAD), prev, cur(ATTN_WIDTH),
                  pl.BlockSpec((N_Q_HEADS, tq), lambda i: (0, nt - 1 - i)), _full((1, N_Q_HEADS))],
        out_specs=[cur(ATTN_WIDTH), cur(KV_WIDTH), cur(KV_WIDTH), _full((1, N_Q_HEADS))],
        out_shape=[jax.ShapeDtypeStruct((s, ATTN_WIDTH), F32), jax.ShapeDtypeStruct((s, KV_WIDTH), F32),
                   jax.ShapeDtypeStruct((s, KV_WIDTH), F32), jax.ShapeDtypeStruct((1, N_Q_HEADS), F32)],
        scratch_shapes=[acc, acc, carry, carry],
        args=(q, kz, kz, vz, vz, dattn, lse, sinks))


def _bwd_inproj(dq, dk, dv, du, cos, sin, win_t, x, g_mix, dx2):
    s = x.shape[0]
    tm = _token_tile(s)

    def body(dq_ref, dk_ref, dv_ref, du_ref, cos_ref, sin_ref, w_ref, x_ref, g_ref, dx2_ref,
             dx_ref, dw_ref, db_ref, dg_ref):
        @pl.when(pl.program_id(0) == 0)
        def _():
            dw_ref[...] = jnp.zeros_like(dw_ref)
            db_ref[...] = jnp.zeros_like(db_ref)
            dg_ref[...] = jnp.zeros_like(dg_ref)

        cos_t, sin_t = cos_ref[...], sin_ref[...]
        dz32 = jnp.concatenate([_rope_bwd(dq_ref[...], cos_t, sin_t), _rope_bwd(dk_ref[...], cos_t, sin_t),
                                dv_ref[...], du_ref[...].astype(F32)], axis=1)
        db_ref[...] += jnp.sum(dz32, axis=0, keepdims=True)
        dz = dz32.astype(BF16)
        g = g_ref[...]
        n, r = _rms(x_ref[...])
        h = (n * g).astype(BF16)
        dh = _nn(dz, w_ref[...])
        for m0 in range(0, IN_WIDTH, TN_ROW_CHUNK):
            dw_ref[m0:m0 + TN_ROW_CHUNK, :] += _tn(dz[:, m0:m0 + TN_ROW_CHUNK], h)
        dx, dg = _rms_bwd(dh, n, r, g)
        dg_ref[...] += dg
        dx_ref[...] = dx2_ref[...] + dx

    return _gridded(
        body, None, name="bwd_inproj", grid=(s // tm,),
        in_specs=[_rows(tm, ATTN_WIDTH), _rows(tm, KV_WIDTH), _rows(tm, KV_WIDTH), _rows(tm, POOL_WIDTH),
                  _rows(tm, LANES), _rows(tm, LANES), _full((IN_WIDTH, D_MODEL)), _rows(tm, D_MODEL),
                  _full((1, D_MODEL)), _rows(tm, D_MODEL)],
        out_specs=[_rows(tm, D_MODEL), _full((IN_WIDTH, D_MODEL)), _full((1, IN_WIDTH)), _full((1, D_MODEL))],
        out_shape=[jax.ShapeDtypeStruct((s, D_MODEL), F32), jax.ShapeDtypeStruct((IN_WIDTH, D_MODEL), F32),
                   jax.ShapeDtypeStruct((1, IN_WIDTH), F32), jax.ShapeDtypeStruct((1, D_MODEL), F32)],
        scratch_shapes=[], args=(dq, dk, dv, du, cos, sin, win_t, x, g_mix, dx2))


def _rope_tables(s):
    inv_freq = 1.0 / (ROPE_THETA ** (jnp.arange(0, HEAD_DIM, 2, dtype=F32) / HEAD_DIM))
    ang = jnp.arange(s, dtype=F32)[:, None] * jnp.tile(inv_freq, 4)[None, :]
    sign = jnp.tile(jnp.repeat(jnp.array([-1.0, 1.0], F32), HEAD_DIM // 2), 2)
    return jnp.cos(ang), jnp.sin(ang) * sign[None, :]


def _place():
    return lax.axis_index("x"), lax.axis_index("y"), lax.axis_index("c")


def _other_chips(x, y):
    return [(1 - x, y), (x, 1 - y), (1 - x, 1 - y)]


def _gather_rider(blocks):
    nm = len(blocks)

    def plan(ins, outs, sems):
        send_sems, recv_sems, local_sems = sems
        x, y, c = _place()
        me, sibling = (x, y, c), (x, y, 1 - c)
        chips = _other_chips(x, y)

        def rows(m, px, py, pc):
            r = ins[m].shape[0]
            return outs[m].at[pl.ds((4 * px + 2 * py + pc) * r, r), :]

        def copy(m, k, block, to, src=None):
            return pltpu.make_async_remote_copy(
                src_ref=rows(m, *block) if src is None else src, dst_ref=rows(m, *block),
                send_sem=send_sems.at[k * nm + m], recv_sem=recv_sems.at[k * nm + m],
                device_id=to, device_id_type=MESH)

        mine = [pltpu.make_async_copy(ins[m], rows(m, *me), local_sems.at[m]) for m in range(nm)]
        first = [copy(m, 0, me, sibling, src=ins[m]) for m in range(nm)]
        first += [copy(m, 1 + j, me, (*chip, c), src=ins[m]) for j, chip in enumerate(chips) for m in range(nm)]
        return me, sibling, chips, copy, mine, first

    def start(ins, outs, sems):
        *_, mine, first = plan(ins, outs, sems)
        for cp in mine + first:
            cp.start()

    def passed_on(ins, outs, sems):
        me, sibling, chips, copy, _, _ = plan(ins, outs, sems)
        return [copy(m, 4 + j, (*chip, me[2]), sibling) for j, chip in enumerate(chips) for m in range(nm)]

    def relay(ins, outs, sems):
        me, _, chips, copy, _, _ = plan(ins, outs, sems)
        forwards = passed_on(ins, outs, sems)
        for j, chip in enumerate(chips):
            for m in range(nm):
                copy(m, 1 + j, (*chip, me[2]), me).wait_recv()
                forwards[j * nm + m].start()

    def finish(ins, outs, sems):
        me, sibling, chips, copy, mine, first = plan(ins, outs, sems)
        for m in range(nm):
            copy(m, 0, sibling, me).wait_recv()
        for j, chip in enumerate(chips):
            for m in range(nm):
                copy(m, 4 + j, (*chip, 1 - me[2]), me).wait_recv()
        for cp in first + passed_on(ins, outs, sems):
            cp.wait_send()
        for cp in mine:
            cp.wait()

    return _Rider(
        arrays=list(blocks), out_shape=[jax.ShapeDtypeStruct((N_DEV * b.shape[0], b.shape[1]), b.dtype) for b in blocks],
        sems=[pltpu.SemaphoreType.DMA((7 * nm,)), pltpu.SemaphoreType.DMA((7 * nm,)), pltpu.SemaphoreType.DMA((nm,))],
        start=start, finish=finish, relay=relay)


def _exchange_rider(copies_of, arrays, out_shape, n_copies):
    def copies(ins, outs, sems):
        send_sems, recv_sems = sems
        return [pltpu.make_async_remote_copy(src_ref=src, dst_ref=dst, send_sem=send_sems.at[k], recv_sem=recv_sems.at[k],
                                             device_id=to, device_id_type=MESH)
                for k, (src, dst, to) in enumerate(copies_of(ins, outs))]

    def start(ins, outs, sems):
        for cp in copies(ins, outs, sems):
            cp.start()

    def finish(ins, outs, sems):
        cps = copies(ins, outs, sems)
        for cp in cps:
            cp.wait_recv()
        for cp in cps:
            cp.wait_send()

    return _Rider(arrays=list(arrays), out_shape=out_shape,
                  sems=[pltpu.SemaphoreType.DMA((n_copies,)), pltpu.SemaphoreType.DMA((n_copies,))], start=start, finish=finish)


def _sibling_rider(grads):
    def copies_of(ins, outs):
        x, y, c = _place()
        for g_ref, o_ref in zip(ins, outs):
            r = g_ref.shape[0] // N_DEV
            for q in range(N_CHIPS):
                yield g_ref.at[pl.ds((2 * q + 1 - c) * r, r), :], o_ref.at[pl.ds(q * r, r), :], (x, y, 1 - c)

    return _exchange_rider(copies_of, grads, [jax.ShapeDtypeStruct((g.shape[0] // 2, g.shape[1]), F32) for g in grads],
                           len(grads) * N_CHIPS)


def _chip_sum(grad, from_sibling, place):
    r = grad.shape[0] // N_DEV
    w = grad.shape[1]

    def body(place_ref, g_ref, s_ref, wire_ref, own_ref):
        total = g_ref[...] + s_ref[...]
        wire_ref[...] = total.astype(BF16)

        @pl.when(pl.program_id(0) == place_ref[1])
        def _():
            own_ref[...] = total

    grid_spec = pltpu.PrefetchScalarGridSpec(
        num_scalar_prefetch=1, grid=(N_CHIPS,),
        in_specs=[pl.BlockSpec((r, w), lambda q, p: (2 * q + p[0], 0)), pl.BlockSpec((r, w), lambda q, p: (q, 0))],
        out_specs=[pl.BlockSpec((r, w), lambda q, p: (q, 0)), pl.BlockSpec((r, w), lambda q, p: (0, 0))])
    return pl.pallas_call(
        body, name="grad_chip_sum", grid_spec=grid_spec,
        out_shape=[jax.ShapeDtypeStruct((N_CHIPS * r, w), BF16), jax.ShapeDtypeStruct((r, w), F32)],
        compiler_params=_params("arbitrary"),
    )(place, grad, from_sibling)


def _chips_rider(wires):
    def copies_of(ins, outs):
        x, y, c = _place()
        for w_ref, o_ref in zip(ins, outs):
            r = w_ref.shape[0] // N_CHIPS
            for j, (px, py) in enumerate(_other_chips(x, y)):
                yield w_ref.at[pl.ds((2 * px + py) * r, r), :], o_ref.at[pl.ds(j * r, r), :], (px, py, c)

    return _exchange_rider(copies_of, wires,
                           [jax.ShapeDtypeStruct((3 * (w.shape[0] // N_CHIPS), w.shape[1]), BF16) for w in wires], len(wires) * 3)


def _adamw_math(w, g, m, v):
    m = ADAM_B1 * m + (1.0 - ADAM_B1) * g
    v = ADAM_B2 * v + (1.0 - ADAM_B2) * jnp.square(g)
    m_hat = m / (1.0 - ADAM_B1 ** ADAM_STEP)
    v_hat = v / (1.0 - ADAM_B2 ** ADAM_STEP)
    delta = -ADAM_LR * (m_hat / (jnp.sqrt(v_hat) + ADAM_EPS) + ADAM_WD * w)
    return delta, m, v


def _reduce_adamw(own, received, w, m, v):
    r = own.shape[0]

    def body(own_ref, rec_ref, w_ref, m_ref, v_ref, g_ref, d_ref, nm_ref, nv_ref):
        g = ((own_ref[...] + rec_ref[0:r, :].astype(F32)) + rec_ref[r:2 * r, :].astype(F32)) + rec_ref[2 * r:, :].astype(F32)
        g_ref[...] = g
        d_ref[...], nm_ref[...], nv_ref[...] = _adamw_math(w_ref[...], g, m_ref[...], v_ref[...])

    shape = jax.ShapeDtypeStruct(own.shape, F32)
    return pl.pallas_call(
        body, name="reduce_adamw", in_specs=[VMEM] * 5, out_specs=[VMEM] * 4, out_shape=[shape] * 4,
        compiler_params=_params(),
    )(own, received, w, m, v)


SMALL_EARLY = (("w_pool", 65536), ("b_pool", 512), ("pool_scale", 512), ("b_out", 1024), ("g_ffn", 1024),
               ("g_final", 1024), ("loss", 1024))
SMALL_LATE = (("sinks", 8), ("g_mix", 1024), ("b_in", 1280))
SMALL = SMALL_EARLY + SMALL_LATE


def _small_rows(size):
    return -(-size // (8 * LANES)) * 8


def _pack_small(values, entries=SMALL):
    parts = []
    for name, size in entries:
        flat = values[name].reshape(-1).astype(F32)
        parts.append(jnp.pad(flat, (0, _small_rows(size) * LANES - size)).reshape(-1, LANES))
    return jnp.concatenate(parts, axis=0)


def _unpack_small(packed, shapes):
    out, row = {}, 0
    for name, size in SMALL:
        rows = _small_rows(size)
        if name in shapes:
            out[name] = packed[row:row + rows].reshape(-1)[:size].reshape(shapes[name])
        row += rows
    return out


def _small_sum_adamw(gathered_early, gathered_late, w, m, v):
    def body(e_ref, l_ref, w_ref, m_ref, v_ref, g_ref, d_ref, nm_ref, nv_ref):
        def total(ref):
            rows = ref.shape[0] // N_DEV
            acc = ref[0:rows, :]
            for dev in range(1, N_DEV):
                acc = acc + ref[dev * rows:(dev + 1) * rows, :]
            return acc

        g = jnp.concatenate([total(e_ref), total(l_ref)], axis=0)
        g_ref[...] = g
        d_ref[...], nm_ref[...], nv_ref[...] = _adamw_math(w_ref[...], g, m_ref[...], v_ref[...])

    shape = jax.ShapeDtypeStruct(w.shape, F32)
    return pl.pallas_call(
        body, name="small_sum_adamw", in_specs=[VMEM] * 5, out_specs=[VMEM] * 4, out_shape=[shape] * 4,
        compiler_params=_params(),
    )(gathered_early, gathered_late, w, m, v)


def kernel(x, g_mix, w_in, b_in, sinks, w_pool, b_pool, pool_scale, w_out, b_out, g_ffn, w_gate, w_up, w_down, g_final, loss_target, m_g_mix, m_w_in, m_b_in, m_sinks, m_w_pool, m_b_pool, m_pool_scale, m_w_out, m_b_out, m_g_ffn, m_w_gate, m_w_up, m_w_down, m_g_final, v_g_mix, v_w_in, v_b_in, v_sinks, v_w_pool, v_b_pool, v_pool_scale, v_w_out, v_b_out, v_g_ffn, v_w_gate, v_w_up, v_w_down, v_g_final):
    weights = dict(g_mix=g_mix, w_in=w_in, b_in=b_in, sinks=sinks, w_pool=w_pool, b_pool=b_pool, pool_scale=pool_scale,
                   w_out=w_out, b_out=b_out, g_ffn=g_ffn, w_gate=w_gate, w_up=w_up, w_down=w_down, g_final=g_final)
    mom1 = dict(g_mix=m_g_mix, w_in=m_w_in, b_in=m_b_in, sinks=m_sinks, w_pool=m_w_pool, b_pool=m_b_pool,
                pool_scale=m_pool_scale, w_out=m_w_out, b_out=m_b_out, g_ffn=m_g_ffn, w_gate=m_w_gate, w_up=m_w_up,
                w_down=m_w_down, g_final=m_g_final)
    mom2 = dict(g_mix=v_g_mix, w_in=v_w_in, b_in=v_b_in, sinks=v_sinks, w_pool=v_w_pool, b_pool=v_b_pool,
                pool_scale=v_pool_scale, w_out=v_w_out, b_out=v_b_out, g_ffn=v_g_ffn, w_gate=v_w_gate, w_up=v_w_up,
                w_down=v_w_down, g_final=v_g_final)
    order = ("g_mix", "w_in", "b_in", "sinks", "w_pool", "b_pool", "pool_scale", "w_out", "b_out", "g_ffn",
             "w_gate", "w_up", "w_down", "g_final")
    big = ("w_in", "w_out", "w_gate", "w_up", "w_down")
    transposed = ("w_in", "w_gate", "w_up")

    def row_shard(name, a):
        return a[0].T if name in transposed else a[0]

    shard = {n: row_shard(n, weights[n]).astype(BF16) for n in big}
    xs, target = x[0], loss_target[0]
    cos, sin = _rope_tables(xs.shape[0])
    wp_b = w_pool[0].astype(BF16)
    bp = b_pool.reshape(1, POOL_WIDTH)
    ps = pool_scale.reshape(1, POOL_WIDTH)
    g_fin = g_final.reshape(1, D_MODEL)
    px, py, pc = _place()
    place = jnp.stack([pc, 2 * px + py]).astype(jnp.int32)

    (win_t,) = _alone(_gather_rider([shard["w_in"]]), "gather_w_in")
    q, kz, vz, vt, mixed, pool, wg_t = _fwd_inproj(
        xs, g_mix, win_t, b_in, cos, sin, wp_b, bp, ps, rider=_gather_rider([shard["w_gate"]]))
    attn, lse, w_out_b, wu_t = _attn_fwd(q, kz, vt, sinks, rider=_gather_rider([shard["w_out"], shard["w_up"]]))
    x2, gate, up, act, wd = _fwd_outproj_ffn_act(attn, pool, w_out_b, b_out, xs, g_ffn, wg_t, wu_t,
                                                 rider=_gather_rider([shard["w_down"]]))
    dx3, sq, dg_final, d_wd = _fwd_down_loss(act, x2, wd, g_fin, target)

    dx2, dg_ffn, db_out, d_wg_t, d_wu_t, wd_sibling = _bwd_ffn(
        dx3, gate, up, x2, wd, wg_t, wu_t, g_ffn, rider=_sibling_rider([d_wd]))
    wd_sum = _chip_sum(d_wd, wd_sibling, place)
    in_grads = [d_wg_t, d_wu_t]
    dattn, du, d_wout, d_wpool, d_bpool, d_pscale, wd_received, *in_sibling = _bwd_outproj_pool(
        dx2, attn, pool, mixed, w_out_b, wp_b, bp, ps, rider=_join(_chips_rider([wd_sum[0]]), _sibling_rider(in_grads)))
    in_sums = [_chip_sum(g, s, place) for g, s in zip(in_grads, in_sibling)]
    small_early = _pack_small(dict(w_pool=d_wpool, b_pool=d_bpool, pool_scale=d_pscale, b_out=db_out, g_ffn=dg_ffn,
                                   g_final=dg_final, loss=sq), SMALL_EARLY)
    dq, dk, dv, d_sinks, *landed = _attn_bwd(
        q, kz, vz, dattn, lse, sinks,
        rider=_join(_chips_rider([wire for wire, _ in in_sums]), _sibling_rider([d_wout]), _gather_rider([small_early])))
    ffn_sums, ffn_received = in_sums + [wd_sum], landed[:2] + [wd_received]
    wout_sum = _chip_sum(d_wout, landed[2], place)
    gathered_early = landed[3]
    dx, d_win_t, d_bin, d_gmix = _bwd_inproj(dq, dk, dv, du, cos, sin, win_t, xs, g_mix, dx2)
    (win_sibling,) = _alone(_sibling_rider([d_win_t]), "grad_exchange_sibling")
    win_sum = _chip_sum(d_win_t, win_sibling, place)
    small_late = _pack_small(dict(sinks=d_sinks, g_mix=d_gmix, b_in=d_bin), SMALL_LATE)
    wout_received, win_received, gathered_late = _alone(
        _join(_chips_rider([wout_sum[0], win_sum[0]]), _gather_rider([small_late])), "grad_exchange_chips")

    reduced = dict(zip(("w_gate", "w_up", "w_down", "w_out", "w_in"),
                       zip(ffn_sums + [wout_sum, win_sum], list(ffn_received) + [wout_received, win_received])))
    grad, delta, new_m, new_v = {}, {}, {}, {}
    for n in big:
        (_, own), rec = reduced[n]
        results = _reduce_adamw(own, rec, row_shard(n, weights[n]), row_shard(n, mom1[n]), row_shard(n, mom2[n]))
        grad[n], delta[n], new_m[n], new_v[n] = [(a.T if n in transposed else a)[None] for a in results]

    shapes = {n: weights[n].shape for n in order if n not in big}
    zero_loss = jnp.zeros((1, D_MODEL), F32)
    packed = _small_sum_adamw(
        gathered_early, gathered_late, _pack_small({**weights, "loss": zero_loss}),
        _pack_small({**mom1, "loss": zero_loss}), _pack_small({**mom2, "loss": zero_loss}))
    for store, pk in zip((grad, delta, new_m, new_v), packed):
        store.update(_unpack_small(pk, shapes))
    loss_rows = _unpack_small(packed[0], {"loss": (D_MODEL,)})["loss"]
    loss = (0.5 / D_MODEL) * jnp.sum(loss_rows)

    return (loss, dx[None], *[grad[n] for n in order], *[delta[n] for n in order],
            *[new_m[n] for n in order], *[new_v[n] for n in order])
```

```python
from typing import Any, Callable, NamedTuple, Sequence

import jax
import jax.numpy as jnp
from jax import lax
from jax.experimental import pallas as pl
from jax.experimental.pallas import tpu as pltpu

D_MODEL = 1024
ATTN_WIDTH = 512
KV_WIDTH = 128
POOL_WIDTH = 512
HEAD_DIM = 64
N_Q_HEADS = 8
N_KV_HEADS = 2
GQA_GROUP = 4
BLOCK = 128
POOL_SIZES = (2, 4, 8, 16)
POOL_GROUP_WIDTH = 128
POOL_HALO = 16
IN_WIDTH = 1280
D_FF = 2816
RMS_EPS = 1e-5
ROPE_THETA = 10000.0
Q_SCALE = HEAD_DIM ** -0.5

ADAM_LR = 0.001
ADAM_B1 = 0.9
ADAM_B2 = 0.999
ADAM_EPS = 1e-08
ADAM_WD = 0.01
ADAM_STEP = 10

N_DEV = 8
N_CHIPS = 4
LANES = 128
VMEM_LIMIT_BYTES = 60 * 1024 * 1024

F32 = jnp.float32
BF16 = jnp.bfloat16
MESH = pl.DeviceIdType.MESH
HBM = pl.BlockSpec(memory_space=pltpu.HBM)
VMEM = pl.BlockSpec(memory_space=pltpu.VMEM)


def _params(*semantics):
    return pltpu.CompilerParams(dimension_semantics=semantics or None, vmem_limit_bytes=VMEM_LIMIT_BYTES)


def _nn(a, b):
    return jnp.dot(a, b, preferred_element_type=F32)


def _nt(a, b):
    return lax.dot_general(a, b, (((1,), (1,)), ((), ())), preferred_element_type=F32)


def _tn(a, b):
    return lax.dot_general(a, b, (((0,), (0,)), ((), ())), preferred_element_type=F32)


def _full(shape):
    return pl.BlockSpec(shape, lambda *_: (0,) * len(shape))


def _rows(tm, width):
    return pl.BlockSpec((tm, width), lambda i, *_: (i, 0))


def _nothing(ins, outs, sems):
    del ins, outs, sems


RELAY_STEPS_BEFORE_LAST = 2


class _Rider(NamedTuple):
    arrays: Sequence[Any]
    out_shape: Sequence[Any]
    sems: Sequence[Any]
    start: Callable[..., None]
    finish: Callable[..., None]
    relay: Callable[..., None] = _nothing
    relay_early: bool = False


def _gridded(body, rider, *, name, grid, in_specs, out_specs, out_shape, scratch_shapes, args):
    params = _params("arbitrary")
    if rider is None:
        return pl.pallas_call(body, name=name, grid=grid, in_specs=in_specs, out_specs=out_specs, out_shape=out_shape,
                              scratch_shapes=scratch_shapes, compiler_params=params)(*args)
    bounds, total = [], 0
    for n in (len(in_specs), len(rider.arrays), len(out_specs), len(rider.out_shape), len(scratch_shapes), len(rider.sems)):
        bounds.append((total, total + n))
        total += n
    last = grid[0] - 1
    relay_step = max(last - RELAY_STEPS_BEFORE_LAST, 0) if rider.relay_early else last

    def riding(*refs):
        ins, r_ins, outs, r_outs, scratch, r_sems = (refs[lo:hi] for lo, hi in bounds)

        @pl.when(pl.program_id(0) == 0)
        def _():
            rider.start(r_ins, r_outs, r_sems)

        body(*ins, *outs, *scratch)

        @pl.when(pl.program_id(0) == relay_step)
        def _():
            rider.relay(r_ins, r_outs, r_sems)

        @pl.when(pl.program_id(0) == last)
        def _():
            rider.finish(r_ins, r_outs, r_sems)

    return pl.pallas_call(
        riding, name=name, grid=grid, in_specs=list(in_specs) + [HBM] * len(rider.arrays),
        out_specs=list(out_specs) + [HBM] * len(rider.out_shape), out_shape=list(out_shape) + list(rider.out_shape),
        scratch_shapes=list(scratch_shapes) + list(rider.sems), compiler_params=params)(*args, *rider.arrays)


def _join(*riders):
    def phase(which):
        def run(ins, outs, sems):
            i = o = s = 0
            for r in riders:
                ni, no, ns = len(r.arrays), len(r.out_shape), len(r.sems)
                getattr(r, which)(ins[i:i + ni], outs[o:o + no], sems[s:s + ns])
                i, o, s = i + ni, o + no, s + ns
        return run

    return _Rider(arrays=[a for r in riders for a in r.arrays], out_shape=[a for r in riders for a in r.out_shape],
                  sems=[a for r in riders for a in r.sems], start=phase("start"), finish=phase("finish"), relay=phase("relay"),
                  relay_early=all(r.relay_early for r in riders if r.relay is not _nothing))


def _alone(rider, name):
    n_in, n_out = len(rider.arrays), len(rider.out_shape)

    def body(*refs):
        parts = refs[:n_in], refs[n_in:n_in + n_out], refs[n_in + n_out:]
        rider.start(*parts)
        rider.relay(*parts)
        rider.finish(*parts)

    return pl.pallas_call(body, name=name, in_specs=[HBM] * n_in, out_specs=[HBM] * n_out, out_shape=list(rider.out_shape),
                          scratch_shapes=list(rider.sems))(*rider.arrays)


def _rot_half(t):
    n = t.shape[1]
    lane = lax.broadcasted_iota(jnp.int32, t.shape, 1)
    return jnp.where((lane % HEAD_DIM) < HEAD_DIM // 2, pltpu.roll(t, n - HEAD_DIM // 2, 1), pltpu.roll(t, HEAD_DIM // 2, 1))


def _rope(t, cos, sin):
    reps = t.shape[1] // LANES
    if reps > 1:
        cos, sin = jnp.tile(cos, (1, reps)), jnp.tile(sin, (1, reps))
    return t * cos + _rot_half(t) * sin


def _rope_bwd(d, cos, sin):
    reps = d.shape[1] // LANES
    if reps > 1:
        cos, sin = jnp.tile(cos, (1, reps)), jnp.tile(sin, (1, reps))
    return d * cos + _rot_half(d * sin)


KV_SPREAD = 4 * LANES


def _spread_kv(t):
    low = lax.broadcasted_iota(jnp.int32, t.shape, 1) < HEAD_DIM
    swapped = pltpu.roll(t, HEAD_DIM, 1)
    zero = jnp.zeros_like(t)
    return jnp.concatenate([jnp.where(low, t, zero), jnp.where(low, zero, swapped),
                            jnp.where(low, swapped, zero), jnp.where(low, zero, t)], axis=1)


def _rms(x):
    r = lax.rsqrt(jnp.mean(x * x, axis=-1, keepdims=True) + RMS_EPS)
    return x * r, r


def _rms_bwd(dh, n, r, g):
    dn = dh * g
    dx = r * (dn - n * jnp.mean(dn * n, axis=-1, keepdims=True))
    return dx, jnp.sum(dh * n, axis=0, keepdims=True)


def _token_tile(s):
    return min(512, s)


def _fwd_inproj(x, g_mix, win_t, b_in, cos, sin, w_pool, b_pool, pool_scale, rider=None):
    s = x.shape[0]
    tm = _token_tile(s)

    def body(x_ref, g_ref, w_ref, b_ref, cos_ref, sin_ref, wp_ref, bp_ref, ps_ref,
             q_ref, k_ref, v_ref, vt_ref, mix_ref, pool_ref, tail_ref):
        i = pl.program_id(0)

        @pl.when(i == 0)
        def _():
            tail_ref[...] = jnp.zeros_like(tail_ref)

        n, _ = _rms(x_ref[...])
        h = (n * g_ref[...]).astype(BF16)
        z = _nt(h, w_ref[...]) + b_ref[...]
        cos_t, sin_t = cos_ref[...], sin_ref[...]
        q_ref[...] = (_rope(z[:, :ATTN_WIDTH], cos_t, sin_t) * Q_SCALE).astype(BF16)
        k_ref[...] = _spread_kv(_rope(z[:, ATTN_WIDTH:ATTN_WIDTH + KV_WIDTH], cos_t, sin_t)).astype(BF16)
        vz = _spread_kv(z[:, ATTN_WIDTH + KV_WIDTH:ATTN_WIDTH + 2 * KV_WIDTH])
        v_ref[...] = vz.astype(BF16)
        vt_ref[...] = vz.T.astype(BF16)
        u = z[:, ATTN_WIDTH + 2 * KV_WIDTH:]
        u_ext = jnp.concatenate([tail_ref[...], u], axis=0)
        tail_ref[...] = u[tm - POOL_HALO:, :]
        pos = lax.broadcasted_iota(jnp.int32, (tm, POOL_GROUP_WIDTH), 0) + i * tm
        for g, size in enumerate(POOL_SIZES):
            cols = slice(g * POOL_GROUP_WIDTH, (g + 1) * POOL_GROUP_WIDTH)
            a = u_ext[:, cols]
            shift = 1
            while shift < size:
                a = a + pltpu.roll(a, shift, 0)
                shift *= 2
            count = jnp.minimum(pos + 1, size).astype(F32)
            mixed = (a[POOL_HALO:, :] / count - u[:, cols]).astype(BF16)
            pre = _nn(mixed, wp_ref[g]) + bp_ref[:, cols]
            mix_ref[:, cols] = mixed
            pool_ref[:, cols] = (pre * ps_ref[:, cols]).astype(BF16)

    bf = lambda w: jax.ShapeDtypeStruct((s, w), BF16)
    return _gridded(
        body, rider, name="fwd_inproj", grid=(s // tm,),
        in_specs=[_rows(tm, D_MODEL), _full((1, D_MODEL)), _full((IN_WIDTH, D_MODEL)), _full((1, IN_WIDTH)),
                  _rows(tm, LANES), _rows(tm, LANES), _full((4, POOL_GROUP_WIDTH, POOL_GROUP_WIDTH)),
                  _full((1, POOL_WIDTH)), _full((1, POOL_WIDTH))],
        out_specs=[_rows(tm, ATTN_WIDTH), _rows(tm, KV_SPREAD), _rows(tm, KV_SPREAD),
                   pl.BlockSpec((KV_SPREAD, tm), lambda i: (0, i)), _rows(tm, POOL_WIDTH), _rows(tm, POOL_WIDTH)],
        out_shape=[bf(ATTN_WIDTH), bf(KV_SPREAD), bf(KV_SPREAD), jax.ShapeDtypeStruct((KV_SPREAD, s), BF16),
                   bf(POOL_WIDTH), bf(POOL_WIDTH)],
        scratch_shapes=[pltpu.VMEM((POOL_HALO, POOL_WIDTH), F32)],
        args=(x, g_mix, win_t, b_in, cos, sin, w_pool, b_pool, pool_scale))


ATTN_TILE = 512
PAIR = 2 * LANES


def _band_masks(tile):
    j = lax.broadcasted_iota(jnp.int32, (4 * BLOCK, 2 * BLOCK), 0) % (2 * BLOCK)
    r = lax.broadcasted_iota(jnp.int32, (4 * BLOCK, 2 * BLOCK), 1) % BLOCK
    band = (j > r) & (j <= r + BLOCK)
    return band & ((tile > 0) | (j >= BLOCK)), band


def _band(cur_ref, prev_ref, b, kv):
    halves = []
    for half in range(2):
        cols = slice(kv * PAIR + half * LANES, kv * PAIR + (half + 1) * LANES)
        if b == 0:
            halves.append(jnp.concatenate([prev_ref[:, cols], cur_ref[0:BLOCK, cols]], axis=0))
        else:
            halves.append(cur_ref[(b - 1) * BLOCK:(b + 1) * BLOCK, cols])
    return jnp.concatenate(halves, axis=0)


def _stack_pair(ref, rows, kv):
    return jnp.concatenate([ref[rows, kv * PAIR:kv * PAIR + LANES], ref[rows, kv * PAIR + LANES:(kv + 1) * PAIR]], axis=0)


def _pair_heads(kv, half):
    return GQA_GROUP * kv + half, GQA_GROUP * kv + 2 + half


def _band_t(cur_ref, prev_ref, b, kv):
    halves = []
    for half in range(2):
        lanes = slice(kv * PAIR + half * LANES, kv * PAIR + (half + 1) * LANES)
        if b == 0:
            halves.append(jnp.concatenate([prev_ref[lanes, :], cur_ref[lanes, 0:BLOCK]], axis=1))
        else:
            halves.append(cur_ref[lanes, (b - 1) * BLOCK:(b + 1) * BLOCK])
    return jnp.concatenate(halves, axis=1)


def _reduce_rows(x, op, reduce):
    while x.shape[0] > 8:
        half = x.shape[0] // 2
        x = op(x[:half], x[half:])
    return reduce(x, axis=0, keepdims=True)


def _per_query(ref, rows, top, bottom):
    return jnp.concatenate([ref[top:top + 1, rows], ref[bottom:bottom + 1, rows]], axis=1)


def _sink_per_query(sink_ref, top, bottom):
    first_slab = lax.broadcasted_iota(jnp.int32, (1, 2 * BLOCK), 1) < BLOCK
    return jnp.where(first_slab, sink_ref[:, top:top + 1], sink_ref[:, bottom:bottom + 1])


def _attn_fwd(q, kz, vt, sinks, rider=None):
    s = q.shape[0]
    tq = min(ATTN_TILE, s)

    def body(q_ref, k_ref, kp_ref, vt_ref, vtp_ref, sink_ref, o_ref, lse_ref):
        first, band = _band_masks(pl.program_id(0))
        chains = [(b, kv) for b in range(tq // BLOCK) for kv in range(N_KV_HEADS)]

        def scores(b, kv):
            rows = slice(b * BLOCK, (b + 1) * BLOCK)
            return _nt(_band(k_ref, kp_ref, b, kv), _stack_pair(q_ref, rows, kv))

        def store(b, kv, ot):
            rows = slice(b * BLOCK, (b + 1) * BLOCK)
            o = ot.T.astype(BF16)
            o_ref[rows, kv * PAIR:kv * PAIR + LANES] = o[:BLOCK]
            o_ref[rows, kv * PAIR + LANES:(kv + 1) * PAIR] = o[BLOCK:]

        ahead = scores(*chains[0])
        behind = None
        for n, (b, kv) in enumerate(chains):
            rows = slice(b * BLOCK, (b + 1) * BLOCK)
            st = jnp.where(first if b == 0 else band, ahead, -jnp.inf)
            if n + 1 < len(chains):
                ahead = scores(*chains[n + 1])
            probs = []
            for half in range(2):
                top, bottom = _pair_heads(kv, half)
                sink = _sink_per_query(sink_ref, top, bottom)
                sh = st[half * 2 * BLOCK:(half + 1) * 2 * BLOCK, :]
                m = jnp.maximum(_reduce_rows(sh, jnp.maximum, jnp.max), sink)
                p = jnp.exp(sh - m)
                denom = _reduce_rows(p, jnp.add, jnp.sum) + jnp.exp(sink - m)
                probs.append((p * (1.0 / denom)).astype(BF16))
                lse = m + jnp.log(denom)
                lse_ref[top:top + 1, rows] = lse[:, :BLOCK]
                lse_ref[bottom:bottom + 1, rows] = lse[:, BLOCK:]
            ot = _nn(_band_t(vt_ref, vtp_ref, b, kv), jnp.concatenate(probs, axis=0))
            if behind is not None:
                store(*behind)
            behind = (b, kv, ot)
        store(*behind)

    per = tq // BLOCK
    cur = lambda w: pl.BlockSpec((tq, w), lambda i: (i, 0))
    prev = pl.BlockSpec((BLOCK, KV_SPREAD), lambda i: (jnp.maximum(per * i - 1, 0), 0))
    cur_t = pl.BlockSpec((KV_SPREAD, tq), lambda i: (0, i))
    prev_t = pl.BlockSpec((KV_SPREAD, BLOCK), lambda i: (0, jnp.maximum(per * i - 1, 0)))
    return _gridded(
        body, rider, name="attn_fwd", grid=(s // tq,),
        in_specs=[cur(ATTN_WIDTH), cur(KV_SPREAD), prev, cur_t, prev_t, _full((1, N_Q_HEADS))],
        out_specs=[cur(ATTN_WIDTH), pl.BlockSpec((N_Q_HEADS, tq), lambda i: (0, i))],
        out_shape=[jax.ShapeDtypeStruct((s, ATTN_WIDTH), BF16), jax.ShapeDtypeStruct((N_Q_HEADS, s), F32)],
        scratch_shapes=[], args=(q, kz, kz, vt, vt, sinks))


FF_CHUNK = 256
TN_ROW_CHUNK = 256


def _resident(shape):
    return pl.BlockSpec(shape, lambda *_: (0,) * len(shape), pipeline_mode=pl.Buffered(1))


def _flush_rows(acc_ref, out_ref, sem, rows, is_last):
    @pl.when(is_last)
    def _():
        pltpu.make_async_copy(acc_ref.at[rows, :], out_ref.at[rows, :], sem).start()


def _flush_wait(acc_ref, out_ref, sem, is_last):
    @pl.when(is_last)
    def _():
        pltpu.make_async_copy(acc_ref, out_ref, sem).wait()


def _accumulate_tn(acc_ref, a_ref, b, out_ref, sem, is_last):
    for m0 in range(0, acc_ref.shape[0], TN_ROW_CHUNK):
        rows = slice(m0, m0 + TN_ROW_CHUNK)
        acc_ref[rows, :] += _tn(a_ref[:, rows], b)
        _flush_rows(acc_ref, out_ref, sem, rows, is_last)
    _flush_wait(acc_ref, out_ref, sem, is_last)


def _fwd_outproj_ffn_act(attn, pool, w_out, b_out, x, g_ffn, wg_t, wu_t, rider=None):
    s = x.shape[0]
    tm = _token_tile(s)

    def body(a_ref, p_ref, w_ref, b_ref, x_ref, g_ref, wg_ref, wu_ref, x2_ref, gate_ref, up_ref, act_ref):
        x2 = x_ref[...] + _nn(a_ref[...], w_ref[:ATTN_WIDTH, :]) + _nn(p_ref[...], w_ref[ATTN_WIDTH:, :]) + b_ref[...]
        x2_ref[...] = x2
        n, _ = _rms(x2)
        h = (n * g_ref[...]).astype(BF16)

        def products(c0):
            return _nt(h, wg_ref[c0:c0 + FF_CHUNK, :]), _nt(h, wu_ref[c0:c0 + FF_CHUNK, :])

        ahead = products(0)
        for c0 in range(0, D_FF, FF_CHUNK):
            cols = slice(c0, c0 + FF_CHUNK)
            gate, up = ahead
            if c0 + FF_CHUNK < D_FF:
                ahead = products(c0 + FF_CHUNK)
            gate_ref[:, cols] = gate.astype(BF16)
            up_ref[:, cols] = up.astype(BF16)
            act_ref[:, cols] = (gate * jax.nn.sigmoid(gate) * up).astype(BF16)

    act_shape = jax.ShapeDtypeStruct((s, D_FF), BF16)
    return _gridded(
        body, rider, name="fwd_outproj_ffn_act", grid=(s // tm,),
        in_specs=[_rows(tm, ATTN_WIDTH), _rows(tm, POOL_WIDTH), _resident((D_MODEL, D_MODEL)), _full((1, D_MODEL)),
                  _rows(tm, D_MODEL), _full((1, D_MODEL)), _resident((D_FF, D_MODEL)), _resident((D_FF, D_MODEL))],
        out_specs=[_rows(tm, D_MODEL)] + [_rows(tm, D_FF)] * 3,
        out_shape=[jax.ShapeDtypeStruct((s, D_MODEL), F32)] + [act_shape] * 3,
        scratch_shapes=[], args=(attn, pool, w_out, b_out, x, g_ffn, wg_t, wu_t))


def _fwd_down_loss(act, x2, wd, g_final, target):
    s = x2.shape[0]
    tm = _token_tile(s)
    last = s // tm - 1

    def body(a_ref, x2_ref, wd_ref, g_ref, t_ref, dx3_ref, sq_ref, dg_ref, dwd_ref, acc_ref, sem):
        @pl.when(pl.program_id(0) == 0)
        def _():
            sq_ref[...] = jnp.zeros_like(sq_ref)
            dg_ref[...] = jnp.zeros_like(dg_ref)
            acc_ref[...] = jnp.zeros_like(acc_ref)

        x3 = x2_ref[...] + _nn(a_ref[...], wd_ref[...])
        n, r = _rms(x3)
        g = g_ref[...]
        diff = n * g - t_ref[...]
        sq_ref[...] += jnp.sum(diff * diff, axis=0, keepdims=True)
        dx3, dg = _rms_bwd(diff * (1.0 / D_MODEL), n, r, g)
        dg_ref[...] += dg
        dx3_ref[...] = dx3
        _accumulate_tn(acc_ref, a_ref, dx3.astype(BF16), dwd_ref, sem, pl.program_id(0) == last)

    return pl.pallas_call(
        body, name="fwd_down_loss", grid=(s // tm,),
        in_specs=[_rows(tm, D_FF), _rows(tm, D_MODEL), _resident((D_FF, D_MODEL)), _full((1, D_MODEL)), _rows(tm, D_MODEL)],
        out_specs=[_rows(tm, D_MODEL), _full((1, D_MODEL)), _full((1, D_MODEL)), HBM],
        out_shape=[jax.ShapeDtypeStruct((s, D_MODEL), F32),
                   jax.ShapeDtypeStruct((1, D_MODEL), F32), jax.ShapeDtypeStruct((1, D_MODEL), F32),
                   jax.ShapeDtypeStruct((D_FF, D_MODEL), F32)],
        scratch_shapes=[pltpu.VMEM((D_FF, D_MODEL), F32), pltpu.SemaphoreType.DMA],
        compiler_params=_params("arbitrary"),
    )(act, x2, wd, g_final, target)


FFN_BWD_TILE = 256


def _bwd_ffn(dx3, gate, up, x2, wd, wg_t, wu_t, g_ffn, rider=None):
    s = x2.shape[0]
    tm = min(FFN_BWD_TILE, s)
    last = s // tm - 1

    def body(dx3_ref, gate_ref, up_ref, x2_ref, wd_ref, wg_ref, wu_ref, g_ref,
             dx2_ref, dg_ref, db_ref, dwg_ref, dwu_ref, dgate_ref, dup_ref, accg_ref, accu_ref, sems):
        @pl.when(pl.program_id(0) == 0)
        def _():
            dg_ref[...] = jnp.zeros_like(dg_ref)
            db_ref[...] = jnp.zeros_like(db_ref)
            accg_ref[...] = jnp.zeros_like(accg_ref)
            accu_ref[...] = jnp.zeros_like(accu_ref)

        is_last = pl.program_id(0) == last
        dx3b = dx3_ref[...].astype(BF16)
        g = g_ref[...]
        n, r = _rms(x2_ref[...])
        h = (n * g).astype(BF16)
        ahead = _nt(dx3b, wd_ref[0:FF_CHUNK, :])
        for c0 in range(0, D_FF, FF_CHUNK):
            cols = slice(c0, c0 + FF_CHUNK)
            dact = ahead
            if c0 + FF_CHUNK < D_FF:
                ahead = _nt(dx3b, wd_ref[c0 + FF_CHUNK:c0 + 2 * FF_CHUNK, :])
            gate = gate_ref[:, cols].astype(F32)
            up = up_ref[:, cols].astype(F32)
            sig = jax.nn.sigmoid(gate)
            silu = gate * sig
            dup = (dact * silu).astype(BF16)
            dgate = (dact * up * (sig + silu * (1.0 - sig))).astype(BF16)
            dup_ref[:, cols] = dup
            dgate_ref[:, cols] = dgate
            accg_ref[cols, :] += _tn(dgate, h)
            accu_ref[cols, :] += _tn(dup, h)
            _flush_rows(accg_ref, dwg_ref, sems.at[0], cols, is_last)
            _flush_rows(accu_ref, dwu_ref, sems.at[1], cols, is_last)
        dh2 = _nn(dgate_ref[...], wg_ref[...]) + _nn(dup_ref[...], wu_ref[...])
        dx, dg = _rms_bwd(dh2, n, r, g)
        dx2 = dx3_ref[...] + dx
        dg_ref[...] += dg
        db_ref[...] += jnp.sum(dx2, axis=0, keepdims=True)
        dx2_ref[...] = dx2
        _flush_wait(accg_ref, dwg_ref, sems.at[0], is_last)
        _flush_wait(accu_ref, dwu_ref, sems.at[1], is_last)

    grad_shape = jax.ShapeDtypeStruct((D_FF, D_MODEL), F32)
    weight = _resident((D_FF, D_MODEL))
    return _gridded(
        body, rider, name="bwd_ffn", grid=(s // tm,),
        in_specs=[_rows(tm, D_MODEL), _rows(tm, D_FF), _rows(tm, D_FF),
                  _rows(tm, D_MODEL), weight, weight, weight, _full((1, D_MODEL))],
        out_specs=[_rows(tm, D_MODEL), _full((1, D_MODEL)), _full((1, D_MODEL)), HBM, HBM],
        out_shape=[jax.ShapeDtypeStruct((s, D_MODEL), F32),
                   jax.ShapeDtypeStruct((1, D_MODEL), F32), jax.ShapeDtypeStruct((1, D_MODEL), F32), grad_shape, grad_shape],
        scratch_shapes=[pltpu.VMEM((tm, D_FF), BF16), pltpu.VMEM((tm, D_FF), BF16),
                        pltpu.VMEM((D_FF, D_MODEL), F32), pltpu.VMEM((D_FF, D_MODEL), F32), pltpu.SemaphoreType.DMA((2,))],
        args=(dx3, gate, up, x2, wd, wg_t, wu_t, g_ffn))


def _bwd_outproj_pool(dx2, attn, pool, mixed, w_out, w_pool, b_pool, pool_scale, rider=None):
    s = dx2.shape[0]
    tm = _token_tile(s)
    nt = s // tm

    def body(dx_ref, a_ref, p_ref, mix_ref, w_ref, wp_ref, bp_ref, ps_ref,
             dattn_ref, du_ref, dwout_ref, dwp_ref, dbp_ref, dps_ref, head_ref):
        step = pl.program_id(0)
        tile = nt - 1 - step

        @pl.when(step == 0)
        def _():
            head_ref[...] = jnp.zeros_like(head_ref)
            dwout_ref[...] = jnp.zeros_like(dwout_ref)
            dwp_ref[...] = jnp.zeros_like(dwp_ref)
            dbp_ref[...] = jnp.zeros_like(dbp_ref)
            dps_ref[...] = jnp.zeros_like(dps_ref)

        dx = dx_ref[...].astype(BF16)
        dwout_ref[:ATTN_WIDTH, :] += _tn(a_ref[...], dx)
        dwout_ref[ATTN_WIDTH:, :] += _tn(p_ref[...], dx)
        dcat = _nt(dx, w_ref[...])
        dattn_ref[...] = dcat[:, :ATTN_WIDTH].astype(BF16)
        dpool = dcat[:, ATTN_WIDTH:]
        pos = lax.broadcasted_iota(jnp.int32, (tm, POOL_GROUP_WIDTH), 0) + tile * tm
        head = head_ref[...]
        n_ext = tm + POOL_HALO
        for g, size in enumerate(POOL_SIZES):
            cols = slice(g * POOL_GROUP_WIDTH, (g + 1) * POOL_GROUP_WIDTH)
            mixed_g = mix_ref[:, cols]
            pre = _nn(mixed_g, wp_ref[g]) + bp_ref[:, cols]
            dy = dpool[:, cols]
            dps_ref[:, cols] += jnp.sum(dy * pre, axis=0, keepdims=True)
            dpre = dy * ps_ref[:, cols]
            dbp_ref[:, cols] += jnp.sum(dpre, axis=0, keepdims=True)
            dpre_b = dpre.astype(BF16)
            dwp_ref[g] += _tn(mixed_g, dpre_b)
            dmixed = _nt(dpre_b, wp_ref[g])
            w = dmixed / jnp.minimum(pos + 1, size).astype(F32)
            head_ref[:, cols] = w[:POOL_HALO, :]
            a = jnp.concatenate([w, head[:, cols]], axis=0)
            shift = 1
            while shift < size:
                a = a + pltpu.roll(a, n_ext - shift, 0)
                shift *= 2
            du_ref[:, cols] = (a[:tm, :] - dmixed).astype(BF16)

    rev = lambda w: pl.BlockSpec((tm, w), lambda i: (nt - 1 - i, 0))
    return _gridded(
        body, rider, name="bwd_outproj_pool", grid=(nt,),
        in_specs=[rev(D_MODEL), rev(ATTN_WIDTH), rev(POOL_WIDTH), rev(POOL_WIDTH), _full((D_MODEL, D_MODEL)),
                  _full((4, POOL_GROUP_WIDTH, POOL_GROUP_WIDTH)), _full((1, POOL_WIDTH)), _full((1, POOL_WIDTH))],
        out_specs=[rev(ATTN_WIDTH), rev(POOL_WIDTH), _full((D_MODEL, D_MODEL)),
                   _full((4, POOL_GROUP_WIDTH, POOL_GROUP_WIDTH)), _full((1, POOL_WIDTH)), _full((1, POOL_WIDTH))],
        out_shape=[jax.ShapeDtypeStruct((s, ATTN_WIDTH), BF16), jax.ShapeDtypeStruct((s, POOL_WIDTH), BF16),
                   jax.ShapeDtypeStruct((D_MODEL, D_MODEL), F32),
                   jax.ShapeDtypeStruct((4, POOL_GROUP_WIDTH, POOL_GROUP_WIDTH), F32),
                   jax.ShapeDtypeStruct((1, POOL_WIDTH), F32), jax.ShapeDtypeStruct((1, POOL_WIDTH), F32)],
        scratch_shapes=[pltpu.VMEM((POOL_HALO, POOL_WIDTH), F32)],
        args=(dx2, attn, pool, mixed, w_out, w_pool, b_pool, pool_scale))


def _fold_spread(t):
    low = lax.broadcasted_iota(jnp.int32, (2 * BLOCK, LANES), 1) < HEAD_DIM
    kept = jnp.where(low, t[:2 * BLOCK, :], t[2 * BLOCK:, :])
    return kept + pltpu.roll(kept, HEAD_DIM, 1)


def _attn_bwd(q, kz, vz, dattn, lse, sinks, rider=None):
    s = q.shape[0]
    tq = min(ATTN_TILE, s)
    nt = s // tq
    per = tq // BLOCK

    def body(q_ref, k_ref, kp_ref, v_ref, vp_ref, do_ref, lse_ref, sink_ref,
             dq_ref, dk_ref, dv_ref, dsink_ref, dk_acc, dv_acc, dk_carry, dv_carry):
        step = pl.program_id(0)

        @pl.when(step == 0)
        def _():
            dk_carry[...] = jnp.zeros_like(dk_carry)
            dv_carry[...] = jnp.zeros_like(dv_carry)
            dsink_ref[...] = jnp.zeros_like(dsink_ref)

        dk_acc[0:tq, :] = jnp.zeros((tq, KV_WIDTH), F32)
        dv_acc[0:tq, :] = jnp.zeros((tq, KV_WIDTH), F32)
        dk_acc[tq:, :] = dk_carry[...]
        dv_acc[tq:, :] = dv_carry[...]
        first, band = _band_masks(nt - 1 - step)
        low = lax.broadcasted_iota(jnp.int32, (2 * BLOCK, LANES), 1) < HEAD_DIM
        chains = [(b, kv) for b in range(per) for kv in range(N_KV_HEADS)]

        def operands(b, kv):
            rows = slice(b * BLOCK, (b + 1) * BLOCK)
            qab = _stack_pair(q_ref, rows, kv)
            doab = _stack_pair(do_ref, rows, kv)
            kzb = _band(k_ref, kp_ref, b, kv)
            return qab, doab, kzb, _nt(kzb, qab), _nt(_band(v_ref, vp_ref, b, kv), doab)

        folded = {}

        def finish(b, kv, dqab, dkz, dvz):
            rows = slice(b * BLOCK, (b + 1) * BLOCK)
            dq_ref[rows, kv * PAIR:kv * PAIR + LANES] = dqab[:BLOCK] * Q_SCALE
            dq_ref[rows, kv * PAIR + LANES:(kv + 1) * PAIR] = dqab[BLOCK:] * Q_SCALE
            folded[kv] = (_fold_spread(dkz), _fold_spread(dvz))
            if kv == N_KV_HEADS - 1:
                band_rows = slice(b * BLOCK, (b + 2) * BLOCK)
                dk_acc[band_rows, :] += jnp.where(low, folded[0][0], folded[1][0])
                dv_acc[band_rows, :] += jnp.where(low, folded[0][1], folded[1][1])

        ahead = operands(*chains[0])
        behind = None
        for n, (b, kv) in enumerate(chains):
            rows = slice(b * BLOCK, (b + 1) * BLOCK)
            mask = first if b == 0 else band
            qab, doab, kzb, st, dpt = ahead
            if n + 1 < len(chains):
                ahead = operands(*chains[n + 1])
            probs, dscores = [], []
            for half in range(2):
                top, bottom = _pair_heads(kv, half)
                keys = slice(half * 2 * BLOCK, (half + 1) * 2 * BLOCK)
                lse_h = _per_query(lse_ref, rows, top, bottom)
                p = jnp.where(mask[keys, :], jnp.exp(st[keys, :] - lse_h), 0.0)
                dph = dpt[keys, :]
                delta = _reduce_rows(p * dph, jnp.add, jnp.sum)
                probs.append(p.astype(BF16))
                dscores.append((p * (dph - delta)).astype(BF16))
                leak = jnp.exp(_sink_per_query(sink_ref, top, bottom) - lse_h) * delta
                dsink_ref[:, top:top + 1] -= jnp.sum(leak[:, :BLOCK], axis=1, keepdims=True)
                dsink_ref[:, bottom:bottom + 1] -= jnp.sum(leak[:, BLOCK:], axis=1, keepdims=True)
            ds = jnp.concatenate(dscores, axis=0)
            results = (_tn(ds, kzb), _nn(ds, qab), _nn(jnp.concatenate(probs, axis=0), doab))
            if behind is not None:
                finish(*behind)
            behind = (b, kv, *results)
        finish(*behind)
        dk_ref[...] = dk_acc[BLOCK:, :]
        dv_ref[...] = dv_acc[BLOCK:, :]
        dk_carry[...] = dk_acc[0:BLOCK, :]
        dv_carry[...] = dv_acc[0:BLOCK, :]

    cur = lambda w: pl.BlockSpec((tq, w), lambda i: (nt - 1 - i, 0))
    prev = pl.BlockSpec((BLOCK, KV_SPREAD), lambda i: (jnp.maximum(per * (nt - 1 - i) - 1, 0), 0))
    acc = pltpu.VMEM((tq + BLOCK, KV_WIDTH), F32)
    carry = pltpu.VMEM((BLOCK, KV_WIDTH), F32)
    return _gridded(
        body, rider, name="attn_bwd", grid=(nt,),
        in_specs=[cur(ATTN_WIDTH), cur(KV_SPREAD), prev, cur(KV_SPREAD), prev, cur(ATTN_WIDTH),
                  pl.BlockSpec((N_Q_HEADS, tq), lambda i: (0, nt - 1 - i)), _full((1, N_Q_HEADS))],
        out_specs=[cur(ATTN_WIDTH), cur(KV_WIDTH), cur(KV_WIDTH), _full((1, N_Q_HEADS))],
        out_shape=[jax.ShapeDtypeStruct((s, ATTN_WIDTH), F32), jax.ShapeDtypeStruct((s, KV_WIDTH), F32),
                   jax.ShapeDtypeStruct((s, KV_WIDTH), F32), jax.ShapeDtypeStruct((1, N_Q_HEADS), F32)],
        scratch_shapes=[acc, acc, carry, carry],
        args=(q, kz, kz, vz, vz, dattn, lse, sinks))


def _bwd_inproj(dq, dk, dv, du, cos, sin, win_t, x, g_mix, dx2):
    s = x.shape[0]
    tm = _token_tile(s)

    def body(dq_ref, dk_ref, dv_ref, du_ref, cos_ref, sin_ref, w_ref, x_ref, g_ref, dx2_ref,
             dx_ref, dw_ref, db_ref, dg_ref):
        @pl.when(pl.program_id(0) == 0)
        def _():
            dw_ref[...] = jnp.zeros_like(dw_ref)
            db_ref[...] = jnp.zeros_like(db_ref)
            dg_ref[...] = jnp.zeros_like(dg_ref)

        cos_t, sin_t = cos_ref[...], sin_ref[...]
        dz32 = jnp.concatenate([_rope_bwd(dq_ref[...], cos_t, sin_t), _rope_bwd(dk_ref[...], cos_t, sin_t),
                                dv_ref[...], du_ref[...].astype(F32)], axis=1)
        db_ref[...] += jnp.sum(dz32, axis=0, keepdims=True)
        dz = dz32.astype(BF16)
        g = g_ref[...]
        n, r = _rms(x_ref[...])
        h = (n * g).astype(BF16)
        dh = _nn(dz, w_ref[...])
        for m0 in range(0, IN_WIDTH, TN_ROW_CHUNK):
            dw_ref[m0:m0 + TN_ROW_CHUNK, :] += _tn(dz[:, m0:m0 + TN_ROW_CHUNK], h)
        dx, dg = _rms_bwd(dh, n, r, g)
        dg_ref[...] += dg
        dx_ref[...] = dx2_ref[...] + dx

    return _gridded(
        body, None, name="bwd_inproj", grid=(s // tm,),
        in_specs=[_rows(tm, ATTN_WIDTH), _rows(tm, KV_WIDTH), _rows(tm, KV_WIDTH), _rows(tm, POOL_WIDTH),
                  _rows(tm, LANES), _rows(tm, LANES), _full((IN_WIDTH, D_MODEL)), _rows(tm, D_MODEL),
                  _full((1, D_MODEL)), _rows(tm, D_MODEL)],
        out_specs=[_rows(tm, D_MODEL), _full((IN_WIDTH, D_MODEL)), _full((1, IN_WIDTH)), _full((1, D_MODEL))],
        out_shape=[jax.ShapeDtypeStruct((s, D_MODEL), F32), jax.ShapeDtypeStruct((IN_WIDTH, D_MODEL), F32),
                   jax.ShapeDtypeStruct((1, IN_WIDTH), F32), jax.ShapeDtypeStruct((1, D_MODEL), F32)],
        scratch_shapes=[], args=(dq, dk, dv, du, cos, sin, win_t, x, g_mix, dx2))


def _rope_tables(s):
    inv_freq = 1.0 / (ROPE_THETA ** (jnp.arange(0, HEAD_DIM, 2, dtype=F32) / HEAD_DIM))
    ang = jnp.arange(s, dtype=F32)[:, None] * inv_freq[None, :]
    cos, sin = jnp.cos(ang), jnp.sin(ang)
    return jnp.tile(cos, (1, 4)), jnp.tile(jnp.concatenate([-sin, sin], axis=1), (1, 2))


def _place():
    return lax.axis_index("x"), lax.axis_index("y"), lax.axis_index("c")


def _other_chips(x, y):
    return [(1 - x, y), (x, 1 - y), (1 - x, 1 - y)]


def _gather_rider(blocks, relay_early=False):
    nm = len(blocks)

    def plan(ins, outs, sems):
        send_sems, recv_sems, local_sems = sems
        x, y, c = _place()
        me, sibling = (x, y, c), (x, y, 1 - c)
        chips = _other_chips(x, y)

        def rows(m, px, py, pc):
            r = ins[m].shape[0]
            return outs[m].at[pl.ds((4 * px + 2 * py + pc) * r, r), :]

        def copy(m, k, block, to, src=None):
            return pltpu.make_async_remote_copy(
                src_ref=rows(m, *block) if src is None else src, dst_ref=rows(m, *block),
                send_sem=send_sems.at[k * nm + m], recv_sem=recv_sems.at[k * nm + m],
                device_id=to, device_id_type=MESH)

        mine = [pltpu.make_async_copy(ins[m], rows(m, *me), local_sems.at[m]) for m in range(nm)]
        first = [copy(m, 0, me, sibling, src=ins[m]) for m in range(nm)]
        first += [copy(m, 1 + j, me, (*chip, c), src=ins[m]) for j, chip in enumerate(chips) for m in range(nm)]
        return me, sibling, chips, copy, mine, first

    def start(ins, outs, sems):
        *_, mine, first = plan(ins, outs, sems)
        for cp in mine + first:
            cp.start()

    def passed_on(ins, outs, sems):
        me, sibling, chips, copy, _, _ = plan(ins, outs, sems)
        return [copy(m, 4 + j, (*chip, me[2]), sibling) for j, chip in enumerate(chips) for m in range(nm)]

    def relay(ins, outs, sems):
        me, _, chips, copy, _, _ = plan(ins, outs, sems)
        forwards = passed_on(ins, outs, sems)
        for j, chip in enumerate(chips):
            for m in range(nm):
                copy(m, 1 + j, (*chip, me[2]), me).wait_recv()
                forwards[j * nm + m].start()

    def finish(ins, outs, sems):
        me, sibling, chips, copy, mine, first = plan(ins, outs, sems)
        for m in range(nm):
            copy(m, 0, sibling, me).wait_recv()
        for j, chip in enumerate(chips):
            for m in range(nm):
                copy(m, 4 + j, (*chip, 1 - me[2]), me).wait_recv()
        for cp in first + passed_on(ins, outs, sems):
            cp.wait_send()
        for cp in mine:
            cp.wait()

    return _Rider(
        arrays=list(blocks), out_shape=[jax.ShapeDtypeStruct((N_DEV * b.shape[0], b.shape[1]), b.dtype) for b in blocks],
        sems=[pltpu.SemaphoreType.DMA((7 * nm,)), pltpu.SemaphoreType.DMA((7 * nm,)), pltpu.SemaphoreType.DMA((nm,))],
        start=start, finish=finish, relay=relay, relay_early=relay_early)


def _exchange_rider(copies_of, arrays, out_shape, n_copies):
    def copies(ins, outs, sems):
        send_sems, recv_sems = sems
        return [pltpu.make_async_remote_copy(src_ref=src, dst_ref=dst, send_sem=send_sems.at[k], recv_sem=recv_sems.at[k],
                                             device_id=to, device_id_type=MESH)
                for k, (src, dst, to) in enumerate(copies_of(ins, outs))]

    def start(ins, outs, sems):
        for cp in copies(ins, outs, sems):
            cp.start()

    def finish(ins, outs, sems):
        cps = copies(ins, outs, sems)
        for cp in cps:
            cp.wait_recv()
        for cp in cps:
            cp.wait_send()

    return _Rider(arrays=list(arrays), out_shape=out_shape,
                  sems=[pltpu.SemaphoreType.DMA((n_copies,)), pltpu.SemaphoreType.DMA((n_copies,))], start=start, finish=finish)


def _sibling_rider(grads):
    def copies_of(ins, outs):
        x, y, c = _place()
        for g_ref, o_ref in zip(ins, outs):
            r = g_ref.shape[0] // N_DEV
            for q in range(N_CHIPS):
                yield g_ref.at[pl.ds((2 * q + 1 - c) * r, r), :], o_ref.at[pl.ds(q * r, r), :], (x, y, 1 - c)

    return _exchange_rider(copies_of, grads, [jax.ShapeDtypeStruct((g.shape[0] // 2, g.shape[1]), F32) for g in grads],
                           len(grads) * N_CHIPS)


def _chip_sum(grad, from_sibling, place):
    r = grad.shape[0] // N_DEV
    w = grad.shape[1]

    def body(place_ref, g_ref, s_ref, wire_ref, own_ref):
        total = g_ref[...] + s_ref[...]
        wire_ref[...] = total.astype(BF16)

        @pl.when(pl.program_id(0) == place_ref[1])
        def _():
            own_ref[...] = total

    grid_spec = pltpu.PrefetchScalarGridSpec(
        num_scalar_prefetch=1, grid=(N_CHIPS,),
        in_specs=[pl.BlockSpec((r, w), lambda q, p: (2 * q + p[0], 0)), pl.BlockSpec((r, w), lambda q, p: (q, 0))],
        out_specs=[pl.BlockSpec((r, w), lambda q, p: (q, 0)), pl.BlockSpec((r, w), lambda q, p: (0, 0))])
    return pl.pallas_call(
        body, name="grad_chip_sum", grid_spec=grid_spec,
        out_shape=[jax.ShapeDtypeStruct((N_CHIPS * r, w), BF16), jax.ShapeDtypeStruct((r, w), F32)],
        compiler_params=_params("arbitrary"),
    )(place, grad, from_sibling)


def _chips_rider(wires):
    def copies_of(ins, outs):
        x, y, c = _place()
        for w_ref, o_ref in zip(ins, outs):
            r = w_ref.shape[0] // N_CHIPS
            for j, (px, py) in enumerate(_other_chips(x, y)):
                yield w_ref.at[pl.ds((2 * px + py) * r, r), :], o_ref.at[pl.ds(j * r, r), :], (px, py, c)

    return _exchange_rider(copies_of, wires,
                           [jax.ShapeDtypeStruct((3 * (w.shape[0] // N_CHIPS), w.shape[1]), BF16) for w in wires], len(wires) * 3)


def _adamw_math(w, g, m, v):
    m = ADAM_B1 * m + (1.0 - ADAM_B1) * g
    v = ADAM_B2 * v + (1.0 - ADAM_B2) * jnp.square(g)
    m_hat = m / (1.0 - ADAM_B1 ** ADAM_STEP)
    v_hat = v / (1.0 - ADAM_B2 ** ADAM_STEP)
    delta = -ADAM_LR * (m_hat / (jnp.sqrt(v_hat) + ADAM_EPS) + ADAM_WD * w)
    return delta, m, v


def _reduce_adamw(own, received, w, m, v):
    r = own.shape[0]

    def body(own_ref, rec_ref, w_ref, m_ref, v_ref, g_ref, d_ref, nm_ref, nv_ref):
        g = ((own_ref[...] + rec_ref[0:r, :].astype(F32)) + rec_ref[r:2 * r, :].astype(F32)) + rec_ref[2 * r:, :].astype(F32)
        g_ref[...] = g
        d_ref[...], nm_ref[...], nv_ref[...] = _adamw_math(w_ref[...], g, m_ref[...], v_ref[...])

    shape = jax.ShapeDtypeStruct(own.shape, F32)
    return pl.pallas_call(
        body, name="reduce_adamw", in_specs=[VMEM] * 5, out_specs=[VMEM] * 4, out_shape=[shape] * 4,
        compiler_params=_params(),
    )(own, received, w, m, v)


SMALL_EARLY = (("w_pool", 65536), ("b_pool", 512), ("pool_scale", 512), ("b_out", 1024), ("g_ffn", 1024),
               ("g_final", 1024), ("loss", 1024))
SMALL_LATE = (("sinks", 8), ("g_mix", 1024), ("b_in", 1280))
SMALL = SMALL_EARLY + SMALL_LATE


def _small_rows(size):
    return -(-size // (8 * LANES)) * 8


def _pack_small(values, entries=SMALL):
    parts = []
    for name, size in entries:
        flat = values[name].reshape(-1).astype(F32)
        parts.append(jnp.pad(flat, (0, _small_rows(size) * LANES - size)).reshape(-1, LANES))
    return jnp.concatenate(parts, axis=0)


def _unpack_small(packed, shapes):
    out, row = {}, 0
    for name, size in SMALL:
        rows = _small_rows(size)
        if name in shapes:
            out[name] = packed[row:row + rows].reshape(-1)[:size].reshape(shapes[name])
        row += rows
    return out


def _small_sum_adamw(gathered_early, gathered_late, w, m, v):
    def body(e_ref, l_ref, w_ref, m_ref, v_ref, g_ref, d_ref, nm_ref, nv_ref):
        def total(ref):
            rows = ref.shape[0] // N_DEV
            acc = ref[0:rows, :]
            for dev in range(1, N_DEV):
                acc = acc + ref[dev * rows:(dev + 1) * rows, :]
            return acc

        g = jnp.concatenate([total(e_ref), total(l_ref)], axis=0)
        g_ref[...] = g
        d_ref[...], nm_ref[...], nv_ref[...] = _adamw_math(w_ref[...], g, m_ref[...], v_ref[...])

    shape = jax.ShapeDtypeStruct(w.shape, F32)
    return pl.pallas_call(
        body, name="small_sum_adamw", in_specs=[VMEM] * 5, out_specs=[VMEM] * 4, out_shape=[shape] * 4,
        compiler_params=_params(),
    )(gathered_early, gathered_late, w, m, v)


def kernel(x, g_mix, w_in, b_in, sinks, w_pool, b_pool, pool_scale, w_out, b_out, g_ffn, w_gate, w_up, w_down, g_final, loss_target, m_g_mix, m_w_in, m_b_in, m_sinks, m_w_pool, m_b_pool, m_pool_scale, m_w_out, m_b_out, m_g_ffn, m_w_gate, m_w_up, m_w_down, m_g_final, v_g_mix, v_w_in, v_b_in, v_sinks, v_w_pool, v_b_pool, v_pool_scale, v_w_out, v_b_out, v_g_ffn, v_w_gate, v_w_up, v_w_down, v_g_final):
    weights = dict(g_mix=g_mix, w_in=w_in, b_in=b_in, sinks=sinks, w_pool=w_pool, b_pool=b_pool, pool_scale=pool_scale,
                   w_out=w_out, b_out=b_out, g_ffn=g_ffn, w_gate=w_gate, w_up=w_up, w_down=w_down, g_final=g_final)
    mom1 = dict(g_mix=m_g_mix, w_in=m_w_in, b_in=m_b_in, sinks=m_sinks, w_pool=m_w_pool, b_pool=m_b_pool,
                pool_scale=m_pool_scale, w_out=m_w_out, b_out=m_b_out, g_ffn=m_g_ffn, w_gate=m_w_gate, w_up=m_w_up,
                w_down=m_w_down, g_final=m_g_final)
    mom2 = dict(g_mix=v_g_mix, w_in=v_w_in, b_in=v_b_in, sinks=v_sinks, w_pool=v_w_pool, b_pool=v_b_pool,
                pool_scale=v_pool_scale, w_out=v_w_out, b_out=v_b_out, g_ffn=v_g_ffn, w_gate=v_w_gate, w_up=v_w_up,
                w_down=v_w_down, g_final=v_g_final)
    order = ("g_mix", "w_in", "b_in", "sinks", "w_pool", "b_pool", "pool_scale", "w_out", "b_out", "g_ffn",
             "w_gate", "w_up", "w_down", "g_final")
    big = ("w_in", "w_out", "w_gate", "w_up", "w_down")
    transposed = ("w_in", "w_gate", "w_up")

    def row_shard(name, a):
        return a[0].T if name in transposed else a[0]

    shard = {n: row_shard(n, weights[n]).astype(BF16) for n in big}
    xs, target = x[0], loss_target[0]
    cos, sin = _rope_tables(xs.shape[0])
    wp_b = w_pool[0].astype(BF16)
    bp = b_pool.reshape(1, POOL_WIDTH)
    ps = pool_scale.reshape(1, POOL_WIDTH)
    g_fin = g_final.reshape(1, D_MODEL)
    px, py, pc = _place()
    place = jnp.stack([pc, 2 * px + py]).astype(jnp.int32)

    (win_t,) = _alone(_gather_rider([shard["w_in"]]), "gather_w_in")
    q, kz, vz, vt, mixed, pool, w_out_b, wg_t = _fwd_inproj(
        xs, g_mix, win_t, b_in, cos, sin, wp_b, bp, ps,
        rider=_gather_rider([shard["w_out"], shard["w_gate"]], relay_early=True))
    attn, lse, wu_t = _attn_fwd(q, kz, vt, sinks, rider=_gather_rider([shard["w_up"]], relay_early=True))
    x2, gate, up, act, wd = _fwd_outproj_ffn_act(attn, pool, w_out_b, b_out, xs, g_ffn, wg_t, wu_t,
                                                 rider=_gather_rider([shard["w_down"]], relay_early=True))
    dx3, sq, dg_final, d_wd = _fwd_down_loss(act, x2, wd, g_fin, target)

    dx2, dg_ffn, db_out, d_wg_t, d_wu_t, wd_sibling = _bwd_ffn(
        dx3, gate, up, x2, wd, wg_t, wu_t, g_ffn, rider=_sibling_rider([d_wd]))
    wd_sum = _chip_sum(d_wd, wd_sibling, place)
    in_grads = [d_wg_t, d_wu_t]
    dattn, du, d_wout, d_wpool, d_bpool, d_pscale, wd_received, *in_sibling = _bwd_outproj_pool(
        dx2, attn, pool, mixed, w_out_b, wp_b, bp, ps, rider=_join(_chips_rider([wd_sum[0]]), _sibling_rider(in_grads)))
    in_sums = [_chip_sum(g, s, place) for g, s in zip(in_grads, in_sibling)]
    small_early = _pack_small(dict(w_pool=d_wpool, b_pool=d_bpool, pool_scale=d_pscale, b_out=db_out, g_ffn=dg_ffn,
                                   g_final=dg_final, loss=sq), SMALL_EARLY)
    dq, dk, dv, d_sinks, *landed = _attn_bwd(
        q, kz, vz, dattn, lse, sinks,
        rider=_join(_chips_rider([wire for wire, _ in in_sums]), _sibling_rider([d_wout]), _gather_rider([small_early])))
    ffn_sums, ffn_received = in_sums + [wd_sum], landed[:2] + [wd_received]
    wout_sum = _chip_sum(d_wout, landed[2], place)
    gathered_early = landed[3]
    dx, d_win_t, d_bin, d_gmix = _bwd_inproj(dq, dk, dv, du, cos, sin, win_t, xs, g_mix, dx2)
    (win_sibling,) = _alone(_sibling_rider([d_win_t]), "grad_exchange_sibling")
    win_sum = _chip_sum(d_win_t, win_sibling, place)
    small_late = _pack_small(dict(sinks=d_sinks, g_mix=d_gmix, b_in=d_bin), SMALL_LATE)
    wout_received, win_received, gathered_late = _alone(
        _join(_chips_rider([wout_sum[0], win_sum[0]]), _gather_rider([small_late])), "grad_exchange_chips")

    reduced = dict(zip(("w_gate", "w_up", "w_down", "w_out", "w_in"),
                       zip(ffn_sums + [wout_sum, win_sum], list(ffn_received) + [wout_received, win_received])))
    grad, delta, new_m, new_v = {}, {}, {}, {}
    for n in big:
        (_, own), rec = reduced[n]
        results = _reduce_adamw(own, rec, row_shard(n, weights[n]), row_shard(n, mom1[n]), row_shard(n, mom2[n]))
        grad[n], delta[n], new_m[n], new_v[n] = [(a.T if n in transposed else a)[None] for a in results]

    shapes = {n: weights[n].shape for n in order if n not in big}
    zero_loss = jnp.zeros((1, D_MODEL), F32)
    packed = _small_sum_adamw(
        gathered_early, gathered_late, _pack_small({**weights, "loss": zero_loss}),
        _pack_small({**mom1, "loss": zero_loss}), _pack_small({**mom2, "loss": zero_loss}))
    for store, pk in zip((grad, delta, new_m, new_v), packed):
        store.update(_unpack_small(pk, shapes))
    loss_rows = _unpack_small(packed[0], {"loss": (D_MODEL,)})["loss"]
    loss = (0.5 / D_MODEL) * jnp.sum(loss_rows)

    return (loss, dx[None], *[grad[n] for n in order], *[delta[n] for n in order],
            *[new_m[n] for n in order], *[new_v[n] for n in order])
```

```python
from typing import Any, Callable, NamedTuple, Sequence

import jax
import jax.numpy as jnp
from jax import lax
from jax.experimental import pallas as pl
from jax.experimental.pallas import tpu as pltpu

D_MODEL = 1024
ATTN_WIDTH = 512
KV_WIDTH = 128
POOL_WIDTH = 512
HEAD_DIM = 64
N_Q_HEADS = 8
N_KV_HEADS = 2
GQA_GROUP = 4
BLOCK = 128
POOL_SIZES = (2, 4, 8, 16)
POOL_GROUP_WIDTH = 128
POOL_HALO = 16
IN_WIDTH = 1280
D_FF = 2816
RMS_EPS = 1e-5
ROPE_THETA = 10000.0
Q_SCALE = HEAD_DIM ** -0.5

ADAM_LR = 0.001
ADAM_B1 = 0.9
ADAM_B2 = 0.999
ADAM_EPS = 1e-08
ADAM_WD = 0.01
ADAM_STEP = 10

N_DEV = 8
N_CHIPS = 4
LANES = 128
VMEM_LIMIT_BYTES = 60 * 1024 * 1024

F32 = jnp.float32
BF16 = jnp.bfloat16
MESH = pl.DeviceIdType.MESH
HBM = pl.BlockSpec(memory_space=pltpu.HBM)
VMEM = pl.BlockSpec(memory_space=pltpu.VMEM)


def _params(*semantics):
    return pltpu.CompilerParams(dimension_semantics=semantics or None, vmem_limit_bytes=VMEM_LIMIT_BYTES)


def _nn(a, b):
    return jnp.dot(a, b, preferred_element_type=F32)


def _nt(a, b):
    return lax.dot_general(a, b, (((1,), (1,)), ((), ())), preferred_element_type=F32)


def _tn(a, b):
    return lax.dot_general(a, b, (((0,), (0,)), ((), ())), preferred_element_type=F32)


def _full(shape):
    return pl.BlockSpec(shape, lambda *_: (0,) * len(shape))


def _rows(tm, width):
    return pl.BlockSpec((tm, width), lambda i, *_: (i, 0))


def _nothing(ins, outs, sems):
    del ins, outs, sems


RELAY_STEPS_BEFORE_LAST = 2


class _Rider(NamedTuple):
    arrays: Sequence[Any]
    out_shape: Sequence[Any]
    sems: Sequence[Any]
    start: Callable[..., None]
    finish: Callable[..., None]
    relay: Callable[..., None] = _nothing
    relay_early: bool = False


def _gridded(body, rider, *, name, grid, in_specs, out_specs, out_shape, scratch_shapes, args):
    params = _params("arbitrary")
    if rider is None:
        return pl.pallas_call(body, name=name, grid=grid, in_specs=in_specs, out_specs=out_specs, out_shape=out_shape,
                              scratch_shapes=scratch_shapes, compiler_params=params)(*args)
    bounds, total = [], 0
    for n in (len(in_specs), len(rider.arrays), len(out_specs), len(rider.out_shape), len(scratch_shapes), len(rider.sems)):
        bounds.append((total, total + n))
        total += n
    last = grid[0] - 1
    relay_step = max(last - RELAY_STEPS_BEFORE_LAST, 0) if rider.relay_early else last

    def riding(*refs):
        ins, r_ins, outs, r_outs, scratch, r_sems = (refs[lo:hi] for lo, hi in bounds)

        @pl.when(pl.program_id(0) == 0)
        def _():
            rider.start(r_ins, r_outs, r_sems)

        body(*ins, *outs, *scratch)

        @pl.when(pl.program_id(0) == relay_step)
        def _():
            rider.relay(r_ins, r_outs, r_sems)

        @pl.when(pl.program_id(0) == last)
        def _():
            rider.finish(r_ins, r_outs, r_sems)

    return pl.pallas_call(
        riding, name=name, grid=grid, in_specs=list(in_specs) + [HBM] * len(rider.arrays),
        out_specs=list(out_specs) + [HBM] * len(rider.out_shape), out_shape=list(out_shape) + list(rider.out_shape),
        scratch_shapes=list(scratch_shapes) + list(rider.sems), compiler_params=params)(*args, *rider.arrays)


def _join(*riders):
    def phase(which):
        def run(ins, outs, sems):
            i = o = s = 0
            for r in riders:
                ni, no, ns = len(r.arrays), len(r.out_shape), len(r.sems)
                getattr(r, which)(ins[i:i + ni], outs[o:o + no], sems[s:s + ns])
                i, o, s = i + ni, o + no, s + ns
        return run

    return _Rider(arrays=[a for r in riders for a in r.arrays], out_shape=[a for r in riders for a in r.out_shape],
                  sems=[a for r in riders for a in r.sems], start=phase("start"), finish=phase("finish"), relay=phase("relay"),
                  relay_early=all(r.relay_early for r in riders if r.relay is not _nothing))


def _alone(rider, name):
    n_in, n_out = len(rider.arrays), len(rider.out_shape)

    def body(*refs):
        parts = refs[:n_in], refs[n_in:n_in + n_out], refs[n_in + n_out:]
        rider.start(*parts)
        rider.relay(*parts)
        rider.finish(*parts)

    return pl.pallas_call(body, name=name, in_specs=[HBM] * n_in, out_specs=[HBM] * n_out, out_shape=list(rider.out_shape),
                          scratch_shapes=list(rider.sems))(*rider.arrays)


def _rot_half(t):
    n = t.shape[1]
    lane = lax.broadcasted_iota(jnp.int32, t.shape, 1)
    return jnp.where((lane % HEAD_DIM) < HEAD_DIM // 2, pltpu.roll(t, n - HEAD_DIM // 2, 1), pltpu.roll(t, HEAD_DIM // 2, 1))


def _rope(t, cos, sin):
    reps = t.shape[1] // LANES
    if reps > 1:
        cos, sin = jnp.tile(cos, (1, reps)), jnp.tile(sin, (1, reps))
    return t * cos + _rot_half(t) * sin


def _rope_bwd(d, cos, sin):
    reps = d.shape[1] // LANES
    if reps > 1:
        cos, sin = jnp.tile(cos, (1, reps)), jnp.tile(sin, (1, reps))
    return d * cos + _rot_half(d * sin)


KV_SPREAD = 4 * LANES


def _spread_kv(t):
    low = lax.broadcasted_iota(jnp.int32, t.shape, 1) < HEAD_DIM
    swapped = pltpu.roll(t, HEAD_DIM, 1)
    zero = jnp.zeros_like(t)
    return jnp.concatenate([jnp.where(low, t, zero), jnp.where(low, zero, swapped),
                            jnp.where(low, swapped, zero), jnp.where(low, zero, t)], axis=1)


def _rms(x):
    r = lax.rsqrt(jnp.mean(x * x, axis=-1, keepdims=True) + RMS_EPS)
    return x * r, r


def _rms_bwd(dh, n, r, g):
    dn = dh * g
    dx = r * (dn - n * jnp.mean(dn * n, axis=-1, keepdims=True))
    return dx, jnp.sum(dh * n, axis=0, keepdims=True)


def _token_tile(s):
    return min(512, s)


def _window_mean(window_sum, pos, size):
    head = window_sum[:POOL_HALO, :] / jnp.minimum(pos[:POOL_HALO, :] + 1, size).astype(F32)
    return jnp.concatenate([head, window_sum[POOL_HALO:, :] * (1.0 / size)], axis=0)


def _fwd_inproj(x, g_mix, win_t, b_in, cos, sin, w_pool, b_pool, pool_scale, rider=None):
    s = x.shape[0]
    tm = _token_tile(s)

    def body(x_ref, g_ref, w_ref, b_ref, cos_ref, sin_ref, wp_ref, bp_ref, ps_ref,
             q_ref, k_ref, v_ref, vt_ref, mix_ref, pool_ref, tail_ref):
        i = pl.program_id(0)

        @pl.when(i == 0)
        def _():
            tail_ref[...] = jnp.zeros_like(tail_ref)

        n, _ = _rms(x_ref[...])
        h = (n * g_ref[...]).astype(BF16)
        z = _nt(h, w_ref[...]) + b_ref[...]
        cos_t, sin_t = cos_ref[...], sin_ref[...]
        q_ref[...] = (_rope(z[:, :ATTN_WIDTH], cos_t, sin_t) * Q_SCALE).astype(BF16)
        k_ref[...] = _spread_kv(_rope(z[:, ATTN_WIDTH:ATTN_WIDTH + KV_WIDTH], cos_t, sin_t)).astype(BF16)
        vz = _spread_kv(z[:, ATTN_WIDTH + KV_WIDTH:ATTN_WIDTH + 2 * KV_WIDTH])
        v_ref[...] = vz.astype(BF16)
        vt_ref[...] = vz.T.astype(BF16)
        u = z[:, ATTN_WIDTH + 2 * KV_WIDTH:]
        u_ext = jnp.concatenate([tail_ref[...], u], axis=0)
        tail_ref[...] = u[tm - POOL_HALO:, :]
        pos = lax.broadcasted_iota(jnp.int32, (tm, POOL_GROUP_WIDTH), 0) + i * tm
        for g, size in enumerate(POOL_SIZES):
            cols = slice(g * POOL_GROUP_WIDTH, (g + 1) * POOL_GROUP_WIDTH)
            a = u_ext[:, cols]
            shift = 1
            while shift < size:
                a = a + pltpu.roll(a, shift, 0)
                shift *= 2
            mixed = (_window_mean(a[POOL_HALO:, :], pos, size) - u[:, cols]).astype(BF16)
            pre = _nn(mixed, wp_ref[g]) + bp_ref[:, cols]
            mix_ref[:, cols] = mixed
            pool_ref[:, cols] = (pre * ps_ref[:, cols]).astype(BF16)

    bf = lambda w: jax.ShapeDtypeStruct((s, w), BF16)
    return _gridded(
        body, rider, name="fwd_inproj", grid=(s // tm,),
        in_specs=[_rows(tm, D_MODEL), _full((1, D_MODEL)), _full((IN_WIDTH, D_MODEL)), _full((1, IN_WIDTH)),
                  _rows(tm, LANES), _rows(tm, LANES), _full((4, POOL_GROUP_WIDTH, POOL_GROUP_WIDTH)),
                  _full((1, POOL_WIDTH)), _full((1, POOL_WIDTH))],
        out_specs=[_rows(tm, ATTN_WIDTH), _rows(tm, KV_SPREAD), _rows(tm, KV_SPREAD),
                   pl.BlockSpec((KV_SPREAD, tm), lambda i: (0, i)), _rows(tm, POOL_WIDTH), _rows(tm, POOL_WIDTH)],
        out_shape=[bf(ATTN_WIDTH), bf(KV_SPREAD), bf(KV_SPREAD), jax.ShapeDtypeStruct((KV_SPREAD, s), BF16),
                   bf(POOL_WIDTH), bf(POOL_WIDTH)],
        scratch_shapes=[pltpu.VMEM((POOL_HALO, POOL_WIDTH), F32)],
        args=(x, g_mix, win_t, b_in, cos, sin, w_pool, b_pool, pool_scale))


ATTN_TILE = 1024
PAIR = 2 * LANES


def _band_masks(tile):
    j = lax.broadcasted_iota(jnp.int32, (4 * BLOCK, 2 * BLOCK), 0) % (2 * BLOCK)
    r = lax.broadcasted_iota(jnp.int32, (4 * BLOCK, 2 * BLOCK), 1) % BLOCK
    band = (j > r) & (j <= r + BLOCK)
    return band & ((tile > 0) | (j >= BLOCK)), band


def _band(cur_ref, prev_ref, b, kv):
    halves = []
    for half in range(2):
        cols = slice(kv * PAIR + half * LANES, kv * PAIR + (half + 1) * LANES)
        if b == 0:
            halves.append(jnp.concatenate([prev_ref[:, cols], cur_ref[0:BLOCK, cols]], axis=0))
        else:
            halves.append(cur_ref[(b - 1) * BLOCK:(b + 1) * BLOCK, cols])
    return jnp.concatenate(halves, axis=0)


def _stack_pair(ref, rows, kv):
    return jnp.concatenate([ref[rows, kv * PAIR:kv * PAIR + LANES], ref[rows, kv * PAIR + LANES:(kv + 1) * PAIR]], axis=0)


def _pair_heads(kv, half):
    return GQA_GROUP * kv + half, GQA_GROUP * kv + 2 + half


def _band_t(cur_ref, prev_ref, b, kv):
    halves = []
    for half in range(2):
        lanes = slice(kv * PAIR + half * LANES, kv * PAIR + (half + 1) * LANES)
        if b == 0:
            halves.append(jnp.concatenate([prev_ref[lanes, :], cur_ref[lanes, 0:BLOCK]], axis=1))
        else:
            halves.append(cur_ref[lanes, (b - 1) * BLOCK:(b + 1) * BLOCK])
    return jnp.concatenate(halves, axis=1)


def _reduce_rows(x, op, reduce):
    while x.shape[0] > 8:
        half = x.shape[0] // 2
        x = op(x[:half], x[half:])
    return reduce(x, axis=0, keepdims=True)


def _per_query(ref, rows, top, bottom):
    return jnp.concatenate([ref[top:top + 1, rows], ref[bottom:bottom + 1, rows]], axis=1)


def _sink_per_query(sink_ref, top, bottom):
    first_slab = lax.broadcasted_iota(jnp.int32, (1, 2 * BLOCK), 1) < BLOCK
    return jnp.where(first_slab, sink_ref[:, top:top + 1], sink_ref[:, bottom:bottom + 1])


def _attn_fwd(q, kz, vt, sinks, rider=None):
    s = q.shape[0]
    tq = min(ATTN_TILE, s)

    def body(q_ref, k_ref, kp_ref, vt_ref, vtp_ref, sink_ref, o_ref, lse_ref):
        first, band = _band_masks(pl.program_id(0))
        chains = [(b, kv) for b in range(tq // BLOCK) for kv in range(N_KV_HEADS)]

        def scores(b, kv):
            rows = slice(b * BLOCK, (b + 1) * BLOCK)
            return _nt(_band(k_ref, kp_ref, b, kv), _stack_pair(q_ref, rows, kv))

        def store(b, kv, ot):
            rows = slice(b * BLOCK, (b + 1) * BLOCK)
            o = ot.T.astype(BF16)
            o_ref[rows, kv * PAIR:kv * PAIR + LANES] = o[:BLOCK]
            o_ref[rows, kv * PAIR + LANES:(kv + 1) * PAIR] = o[BLOCK:]

        ahead = scores(*chains[0])
        behind = None
        for n, (b, kv) in enumerate(chains):
            rows = slice(b * BLOCK, (b + 1) * BLOCK)
            st = jnp.where(first if b == 0 else band, ahead, -jnp.inf)
            if n + 1 < len(chains):
                ahead = scores(*chains[n + 1])
            probs = []
            for half in range(2):
                top, bottom = _pair_heads(kv, half)
                sink = _sink_per_query(sink_ref, top, bottom)
                sh = st[half * 2 * BLOCK:(half + 1) * 2 * BLOCK, :]
                m = jnp.maximum(_reduce_rows(sh, jnp.maximum, jnp.max), sink)
                p = jnp.exp(sh - m)
                denom = _reduce_rows(p, jnp.add, jnp.sum) + jnp.exp(sink - m)
                probs.append((p * (1.0 / denom)).astype(BF16))
                lse = m + jnp.log(denom)
                lse_ref[top:top + 1, rows] = lse[:, :BLOCK]
                lse_ref[bottom:bottom + 1, rows] = lse[:, BLOCK:]
            ot = _nn(_band_t(vt_ref, vtp_ref, b, kv), jnp.concatenate(probs, axis=0))
            if behind is not None:
                store(*behind)
            behind = (b, kv, ot)
        store(*behind)

    per = tq // BLOCK
    cur = lambda w: pl.BlockSpec((tq, w), lambda i: (i, 0))
    prev = pl.BlockSpec((BLOCK, KV_SPREAD), lambda i: (jnp.maximum(per * i - 1, 0), 0))
    cur_t = pl.BlockSpec((KV_SPREAD, tq), lambda i: (0, i))
    prev_t = pl.BlockSpec((KV_SPREAD, BLOCK), lambda i: (0, jnp.maximum(per * i - 1, 0)))
    return _gridded(
        body, rider, name="attn_fwd", grid=(s // tq,),
        in_specs=[cur(ATTN_WIDTH), cur(KV_SPREAD), prev, cur_t, prev_t, _full((1, N_Q_HEADS))],
        out_specs=[cur(ATTN_WIDTH), pl.BlockSpec((N_Q_HEADS, tq), lambda i: (0, i))],
        out_shape=[jax.ShapeDtypeStruct((s, ATTN_WIDTH), BF16), jax.ShapeDtypeStruct((N_Q_HEADS, s), F32)],
        scratch_shapes=[], args=(q, kz, kz, vt, vt, sinks))


FF_CHUNK = 256
TN_ROW_CHUNK = 256


def _resident(shape):
    return pl.BlockSpec(shape, lambda *_: (0,) * len(shape), pipeline_mode=pl.Buffered(1))


def _flush_rows(acc_ref, out_ref, sem, rows, is_last):
    @pl.when(is_last)
    def _():
        pltpu.make_async_copy(acc_ref.at[rows, :], out_ref.at[rows, :], sem).start()


def _flush_wait(acc_ref, out_ref, sem, is_last):
    @pl.when(is_last)
    def _():
        pltpu.make_async_copy(acc_ref, out_ref, sem).wait()


def _accumulate_tn(acc_ref, a_ref, b, out_ref, sem, is_last):
    for m0 in range(0, acc_ref.shape[0], TN_ROW_CHUNK):
        rows = slice(m0, m0 + TN_ROW_CHUNK)
        acc_ref[rows, :] += _tn(a_ref[:, rows], b)
        _flush_rows(acc_ref, out_ref, sem, rows, is_last)
    _flush_wait(acc_ref, out_ref, sem, is_last)


def _fwd_outproj_ffn_act(attn, pool, w_out, b_out, x, g_ffn, wg_t, wu_t, rider=None):
    s = x.shape[0]
    tm = _token_tile(s)

    def body(a_ref, p_ref, w_ref, b_ref, x_ref, g_ref, wg_ref, wu_ref, x2_ref, gate_ref, up_ref, act_ref):
        x2 = x_ref[...] + _nn(a_ref[...], w_ref[:ATTN_WIDTH, :]) + _nn(p_ref[...], w_ref[ATTN_WIDTH:, :]) + b_ref[...]
        x2_ref[...] = x2
        n, _ = _rms(x2)
        h = (n * g_ref[...]).astype(BF16)

        def products(c0):
            return _nt(h, wg_ref[c0:c0 + FF_CHUNK, :]), _nt(h, wu_ref[c0:c0 + FF_CHUNK, :])

        ahead = products(0)
        for c0 in range(0, D_FF, FF_CHUNK):
            cols = slice(c0, c0 + FF_CHUNK)
            gate, up = ahead
            if c0 + FF_CHUNK < D_FF:
                ahead = products(c0 + FF_CHUNK)
            gate_ref[:, cols] = gate.astype(BF16)
            up_ref[:, cols] = up.astype(BF16)
            act_ref[:, cols] = (gate * jax.nn.sigmoid(gate) * up).astype(BF16)

    act_shape = jax.ShapeDtypeStruct((s, D_FF), BF16)
    return _gridded(
        body, rider, name="fwd_outproj_ffn_act", grid=(s // tm,),
        in_specs=[_rows(tm, ATTN_WIDTH), _rows(tm, POOL_WIDTH), _resident((D_MODEL, D_MODEL)), _full((1, D_MODEL)),
                  _rows(tm, D_MODEL), _full((1, D_MODEL)), _resident((D_FF, D_MODEL)), _resident((D_FF, D_MODEL))],
        out_specs=[_rows(tm, D_MODEL)] + [_rows(tm, D_FF)] * 3,
        out_shape=[jax.ShapeDtypeStruct((s, D_MODEL), F32)] + [act_shape] * 3,
        scratch_shapes=[], args=(attn, pool, w_out, b_out, x, g_ffn, wg_t, wu_t))


def _fwd_down_loss(act, x2, wd, g_final, target):
    s = x2.shape[0]
    tm = _token_tile(s)
    last = s // tm - 1

    def body(a_ref, x2_ref, wd_ref, g_ref, t_ref, dx3_ref, sq_ref, dg_ref, dwd_ref, acc_ref, sem):
        @pl.when(pl.program_id(0) == 0)
        def _():
            sq_ref[...] = jnp.zeros_like(sq_ref)
            dg_ref[...] = jnp.zeros_like(dg_ref)
            acc_ref[...] = jnp.zeros_like(acc_ref)

        x3 = x2_ref[...] + _nn(a_ref[...], wd_ref[...])
        n, r = _rms(x3)
        g = g_ref[...]
        diff = n * g - t_ref[...]
        sq_ref[...] += jnp.sum(diff * diff, axis=0, keepdims=True)
        dx3, dg = _rms_bwd(diff * (1.0 / D_MODEL), n, r, g)
        dg_ref[...] += dg
        dx3_ref[...] = dx3
        _accumulate_tn(acc_ref, a_ref, dx3.astype(BF16), dwd_ref, sem, pl.program_id(0) == last)

    return pl.pallas_call(
        body, name="fwd_down_loss", grid=(s // tm,),
        in_specs=[_rows(tm, D_FF), _rows(tm, D_MODEL), _resident((D_FF, D_MODEL)), _full((1, D_MODEL)), _rows(tm, D_MODEL)],
        out_specs=[_rows(tm, D_MODEL), _full((1, D_MODEL)), _full((1, D_MODEL)), HBM],
        out_shape=[jax.ShapeDtypeStruct((s, D_MODEL), F32),
                   jax.ShapeDtypeStruct((1, D_MODEL), F32), jax.ShapeDtypeStruct((1, D_MODEL), F32),
                   jax.ShapeDtypeStruct((D_FF, D_MODEL), F32)],
        scratch_shapes=[pltpu.VMEM((D_FF, D_MODEL), F32), pltpu.SemaphoreType.DMA],
        compiler_params=_params("arbitrary"),
    )(act, x2, wd, g_final, target)


FFN_BWD_TILE = 256


def _bwd_ffn(dx3, gate, up, x2, wd, wg_t, wu_t, g_ffn, rider=None):
    s = x2.shape[0]
    tm = min(FFN_BWD_TILE, s)
    last = s // tm - 1

    def body(dx3_ref, gate_ref, up_ref, x2_ref, wd_ref, wg_ref, wu_ref, g_ref,
             dx2_ref, dg_ref, db_ref, dwg_ref, dwu_ref, dgate_ref, dup_ref, accg_ref, accu_ref, sems):
        @pl.when(pl.program_id(0) == 0)
        def _():
            dg_ref[...] = jnp.zeros_like(dg_ref)
            db_ref[...] = jnp.zeros_like(db_ref)
            accg_ref[...] = jnp.zeros_like(accg_ref)
            accu_ref[...] = jnp.zeros_like(accu_ref)

        dx3b = dx3_ref[...].astype(BF16)
        g = g_ref[...]
        n, r = _rms(x2_ref[...])
        h = (n * g).astype(BF16)
        ahead = _nt(dx3b, wd_ref[0:FF_CHUNK, :])
        for c0 in range(0, D_FF, FF_CHUNK):
            cols = slice(c0, c0 + FF_CHUNK)
            dact = ahead
            if c0 + FF_CHUNK < D_FF:
                ahead = _nt(dx3b, wd_ref[c0 + FF_CHUNK:c0 + 2 * FF_CHUNK, :])
            gate = gate_ref[:, cols].astype(F32)
            up = up_ref[:, cols].astype(F32)
            sig = jax.nn.sigmoid(gate)
            silu = gate * sig
            dup = (dact * silu).astype(BF16)
            dgate = (dact * up * (sig + silu * (1.0 - sig))).astype(BF16)
            dup_ref[:, cols] = dup
            dgate_ref[:, cols] = dgate
            accg_ref[cols, :] += _tn(dgate, h)
            accu_ref[cols, :] += _tn(dup, h)
        dh2 = _nn(dgate_ref[...], wg_ref[...]) + _nn(dup_ref[...], wu_ref[...])
        dx, dg = _rms_bwd(dh2, n, r, g)
        dx2 = dx3_ref[...] + dx
        dg_ref[...] += dg
        db_ref[...] += jnp.sum(dx2, axis=0, keepdims=True)
        dx2_ref[...] = dx2

        @pl.when(pl.program_id(0) == last)
        def _():
            outs = [pltpu.make_async_copy(accg_ref, dwg_ref, sems.at[0]), pltpu.make_async_copy(accu_ref, dwu_ref, sems.at[1])]
            for cp in outs:
                cp.start()
            for cp in outs:
                cp.wait()

    grad_shape = jax.ShapeDtypeStruct((D_FF, D_MODEL), F32)
    weight = _resident((D_FF, D_MODEL))
    return _gridded(
        body, rider, name="bwd_ffn", grid=(s // tm,),
        in_specs=[_rows(tm, D_MODEL), _rows(tm, D_FF), _rows(tm, D_FF),
                  _rows(tm, D_MODEL), weight, weight, weight, _full((1, D_MODEL))],
        out_specs=[_rows(tm, D_MODEL), _full((1, D_MODEL)), _full((1, D_MODEL)), HBM, HBM],
        out_shape=[jax.ShapeDtypeStruct((s, D_MODEL), F32),
                   jax.ShapeDtypeStruct((1, D_MODEL), F32), jax.ShapeDtypeStruct((1, D_MODEL), F32), grad_shape, grad_shape],
        scratch_shapes=[pltpu.VMEM((tm, D_FF), BF16), pltpu.VMEM((tm, D_FF), BF16),
                        pltpu.VMEM((D_FF, D_MODEL), F32), pltpu.VMEM((D_FF, D_MODEL), F32), pltpu.SemaphoreType.DMA((2,))],
        args=(dx3, gate, up, x2, wd, wg_t, wu_t, g_ffn))


def _bwd_outproj_pool(dx2, attn, pool, mixed, w_out, w_pool, b_pool, pool_scale, rider=None):
    s = dx2.shape[0]
    tm = min(2 * _token_tile(s), s)
    nt = s // tm

    def body(dx_ref, a_ref, p_ref, mix_ref, w_ref, wp_ref, bp_ref, ps_ref,
             dattn_ref, du_ref, dwout_ref, dwp_ref, dbp_ref, dps_ref, head_ref):
        step = pl.program_id(0)
        tile = nt - 1 - step

        @pl.when(step == 0)
        def _():
            head_ref[...] = jnp.zeros_like(head_ref)
            dwout_ref[...] = jnp.zeros_like(dwout_ref)
            dwp_ref[...] = jnp.zeros_like(dwp_ref)
            dbp_ref[...] = jnp.zeros_like(dbp_ref)
            dps_ref[...] = jnp.zeros_like(dps_ref)

        dx = dx_ref[...].astype(BF16)
        dwout_ref[:ATTN_WIDTH, :] += _tn(a_ref[...], dx)
        dwout_ref[ATTN_WIDTH:, :] += _tn(p_ref[...], dx)
        dcat = _nt(dx, w_ref[...])
        dattn_ref[...] = dcat[:, :ATTN_WIDTH].astype(BF16)
        dpool = dcat[:, ATTN_WIDTH:]
        pos = lax.broadcasted_iota(jnp.int32, (tm, POOL_GROUP_WIDTH), 0) + tile * tm
        head = head_ref[...]
        n_ext = tm + POOL_HALO
        for g, size in enumerate(POOL_SIZES):
            cols = slice(g * POOL_GROUP_WIDTH, (g + 1) * POOL_GROUP_WIDTH)
            mixed_g = mix_ref[:, cols]
            pre = _nn(mixed_g, wp_ref[g]) + bp_ref[:, cols]
            dy = dpool[:, cols]
            dps_ref[:, cols] += jnp.sum(dy * pre, axis=0, keepdims=True)
            dpre = dy * ps_ref[:, cols]
            dbp_ref[:, cols] += jnp.sum(dpre, axis=0, keepdims=True)
            dpre_b = dpre.astype(BF16)
            dwp_ref[g] += _tn(mixed_g, dpre_b)
            dmixed = _nt(dpre_b, wp_ref[g])
            w = _window_mean(dmixed, pos, size)
            head_ref[:, cols] = w[:POOL_HALO, :]
            a = jnp.concatenate([w, head[:, cols]], axis=0)
            shift = 1
            while shift < size:
                a = a + pltpu.roll(a, n_ext - shift, 0)
                shift *= 2
            du_ref[:, cols] = (a[:tm, :] - dmixed).astype(BF16)

    rev = lambda w: pl.BlockSpec((tm, w), lambda i: (nt - 1 - i, 0))
    return _gridded(
        body, rider, name="bwd_outproj_pool", grid=(nt,),
        in_specs=[rev(D_MODEL), rev(ATTN_WIDTH), rev(POOL_WIDTH), rev(POOL_WIDTH), _full((D_MODEL, D_MODEL)),
                  _full((4, POOL_GROUP_WIDTH, POOL_GROUP_WIDTH)), _full((1, POOL_WIDTH)), _full((1, POOL_WIDTH))],
        out_specs=[rev(ATTN_WIDTH), rev(POOL_WIDTH), _full((D_MODEL, D_MODEL)),
                   _full((4, POOL_GROUP_WIDTH, POOL_GROUP_WIDTH)), _full((1, POOL_WIDTH)), _full((1, POOL_WIDTH))],
        out_shape=[jax.ShapeDtypeStruct((s, ATTN_WIDTH), BF16), jax.ShapeDtypeStruct((s, POOL_WIDTH), BF16),
                   jax.ShapeDtypeStruct((D_MODEL, D_MODEL), F32),
                   jax.ShapeDtypeStruct((4, POOL_GROUP_WIDTH, POOL_GROUP_WIDTH), F32),
                   jax.ShapeDtypeStruct((1, POOL_WIDTH), F32), jax.ShapeDtypeStruct((1, POOL_WIDTH), F32)],
        scratch_shapes=[pltpu.VMEM((POOL_HALO, POOL_WIDTH), F32)],
        args=(dx2, attn, pool, mixed, w_out, w_pool, b_pool, pool_scale))


def _fold_spread(t):
    low = lax.broadcasted_iota(jnp.int32, (2 * BLOCK, LANES), 1) < HEAD_DIM
    kept = jnp.where(low, t[:2 * BLOCK, :], t[2 * BLOCK:, :])
    return kept + pltpu.roll(kept, HEAD_DIM, 1)


def _attn_bwd(q, kz, vz, dattn, lse, sinks, rider=None):
    s = q.shape[0]
    tq = min(ATTN_TILE, s)
    nt = s // tq
    per = tq // BLOCK

    def body(q_ref, k_ref, kp_ref, v_ref, vp_ref, do_ref, lse_ref, sink_ref,
             dq_ref, dk_ref, dv_ref, dsink_ref, dk_acc, dv_acc, dk_carry, dv_carry):
        step = pl.program_id(0)

        @pl.when(step == 0)
        def _():
            dk_carry[...] = jnp.zeros_like(dk_carry)
            dv_carry[...] = jnp.zeros_like(dv_carry)
            dsink_ref[...] = jnp.zeros_like(dsink_ref)

        dk_acc[0:tq, :] = jnp.zeros((tq, KV_WIDTH), F32)
        dv_acc[0:tq, :] = jnp.zeros((tq, KV_WIDTH), F32)
        dk_acc[tq:, :] = dk_carry[...]
        dv_acc[tq:, :] = dv_carry[...]
        first, band = _band_masks(nt - 1 - step)
        low = lax.broadcasted_iota(jnp.int32, (2 * BLOCK, LANES), 1) < HEAD_DIM
        chains = [(b, kv) for b in range(per) for kv in range(N_KV_HEADS)]

        def operands(b, kv):
            rows = slice(b * BLOCK, (b + 1) * BLOCK)
            qab = _stack_pair(q_ref, rows, kv)
            doab = _stack_pair(do_ref, rows, kv)
            kzb = _band(k_ref, kp_ref, b, kv)
            return qab, doab, kzb, _nt(kzb, qab), _nt(_band(v_ref, vp_ref, b, kv), doab)

        folded = {}

        def finish(b, kv, dqab, dkz, dvz):
            rows = slice(b * BLOCK, (b + 1) * BLOCK)
            dq_ref[rows, kv * PAIR:kv * PAIR + LANES] = dqab[:BLOCK] * Q_SCALE
            dq_ref[rows, kv * PAIR + LANES:(kv + 1) * PAIR] = dqab[BLOCK:] * Q_SCALE
            folded[kv] = (_fold_spread(dkz), _fold_spread(dvz))
            if kv == N_KV_HEADS - 1:
                band_rows = slice(b * BLOCK, (b + 2) * BLOCK)
                dk_acc[band_rows, :] += jnp.where(low, folded[0][0], folded[1][0])
                dv_acc[band_rows, :] += jnp.where(low, folded[0][1], folded[1][1])

        ahead = operands(*chains[0])
        behind = None
        for n, (b, kv) in enumerate(chains):
            rows = slice(b * BLOCK, (b + 1) * BLOCK)
            mask = first if b == 0 else band
            qab, doab, kzb, st, dpt = ahead
            if n + 1 < len(chains):
                ahead = operands(*chains[n + 1])
            probs, dscores = [], []
            for half in range(2):
                top, bottom = _pair_heads(kv, half)
                keys = slice(half * 2 * BLOCK, (half + 1) * 2 * BLOCK)
                lse_h = _per_query(lse_ref, rows, top, bottom)
                p = jnp.where(mask[keys, :], jnp.exp(st[keys, :] - lse_h), 0.0)
                dph = dpt[keys, :]
                delta = _reduce_rows(p * dph, jnp.add, jnp.sum)
                probs.append(p.astype(BF16))
                dscores.append((p * (dph - delta)).astype(BF16))
                leak = jnp.exp(_sink_per_query(sink_ref, top, bottom) - lse_h) * delta
                dsink_ref[:, top:top + 1] -= jnp.sum(leak[:, :BLOCK], axis=1, keepdims=True)
                dsink_ref[:, bottom:bottom + 1] -= jnp.sum(leak[:, BLOCK:], axis=1, keepdims=True)
            ds = jnp.concatenate(dscores, axis=0)
            results = (_tn(ds, kzb), _nn(ds, qab), _nn(jnp.concatenate(probs, axis=0), doab))
            if behind is not None:
                finish(*behind)
            behind = (b, kv, *results)
        finish(*behind)
        dk_ref[...] = dk_acc[BLOCK:, :]
        dv_ref[...] = dv_acc[BLOCK:, :]
        dk_carry[...] = dk_acc[0:BLOCK, :]
        dv_carry[...] = dv_acc[0:BLOCK, :]

    cur = lambda w: pl.BlockSpec((tq, w), lambda i: (nt - 1 - i, 0))
    prev = pl.BlockSpec((BLOCK, KV_SPREAD), lambda i: (jnp.maximum(per * (nt - 1 - i) - 1, 0), 0))
    acc = pltpu.VMEM((tq + BLOCK, KV_WIDTH), F32)
    carry = pltpu.VMEM((BLOCK, KV_WIDTH), F32)
    return _gridded(
        body, rider, name="attn_bwd", grid=(nt,),
        in_specs=[cur(ATTN_WIDTH), cur(KV_SPREAD), prev, cur(KV_SPREAD), prev, cur(ATTN_WIDTH),
                  pl.BlockSpec((N_Q_HEADS, tq), lambda i: (0, nt - 1 - i)), _full((1, N_Q_HEADS))],
        out_specs=[cur(ATTN_WIDTH), cur(KV_WIDTH), cur(KV_WIDTH), _full((1, N_Q_HEADS))],
        out_shape=[jax.ShapeDtypeStruct((s, ATTN_WIDTH), F32), jax.ShapeDtypeStruct((s, KV_WIDTH), F32),
                   jax.ShapeDtypeStruct((s, KV_WIDTH), F32), jax.ShapeDtypeStruct((1, N_Q_HEADS), F32)],
        scratch_shapes=[acc, acc, carry, carry],
        args=(q, kz, kz, vz, vz, dattn, lse, sinks))


def _bwd_inproj(dq, dk, dv, du, cos, sin, win_t, x, g_mix, dx2):
    s = x.shape[0]
    tm = _token_tile(s)

    def body(dq_ref, dk_ref, dv_ref, du_ref, cos_ref, sin_ref, w_ref, x_ref, g_ref, dx2_ref,
             dx_ref, dw_ref, db_ref, dg_ref):
        @pl.when(pl.program_id(0) == 0)
        def _():
            dw_ref[...] = jnp.zeros_like(dw_ref)
            db_ref[...] = jnp.zeros_like(db_ref)
            dg_ref[...] = jnp.zeros_like(dg_ref)

        cos_t, sin_t = cos_ref[...], sin_ref[...]
        dz32 = jnp.concatenate([_rope_bwd(dq_ref[...], cos_t, sin_t), _rope_bwd(dk_ref[...], cos_t, sin_t),
                                dv_ref[...], du_ref[...].astype(F32)], axis=1)
        db_ref[...] += jnp.sum(dz32, axis=0, keepdims=True)
        dz = dz32.astype(BF16)
        g = g_ref[...]
        n, r = _rms(x_ref[...])
        h = (n * g).astype(BF16)
        dh = _nn(dz, w_ref[...])
        for m0 in range(0, IN_WIDTH, TN_ROW_CHUNK):
            dw_ref[m0:m0 + TN_ROW_CHUNK, :] += _tn(dz[:, m0:m0 + TN_ROW_CHUNK], h)
        dx, dg = _rms_bwd(dh, n, r, g)
        dg_ref[...] += dg
        dx_ref[...] = dx2_ref[...] + dx

    return _gridded(
        body, None, name="bwd_inproj", grid=(s // tm,),
        in_specs=[_rows(tm, ATTN_WIDTH), _rows(tm, KV_WIDTH), _rows(tm, KV_WIDTH), _rows(tm, POOL_WIDTH),
                  _rows(tm, LANES), _rows(tm, LANES), _full((IN_WIDTH, D_MODEL)), _rows(tm, D_MODEL),
                  _full((1, D_MODEL)), _rows(tm, D_MODEL)],
        out_specs=[_rows(tm, D_MODEL), _full((IN_WIDTH, D_MODEL)), _full((1, IN_WIDTH)), _full((1, D_MODEL))],
        out_shape=[jax.ShapeDtypeStruct((s, D_MODEL), F32), jax.ShapeDtypeStruct((IN_WIDTH, D_MODEL), F32),
                   jax.ShapeDtypeStruct((1, IN_WIDTH), F32), jax.ShapeDtypeStruct((1, D_MODEL), F32)],
        scratch_shapes=[], args=(dq, dk, dv, du, cos, sin, win_t, x, g_mix, dx2))


def _rope_tables(s):
    inv_freq = 1.0 / (ROPE_THETA ** (jnp.arange(0, HEAD_DIM, 2, dtype=F32) / HEAD_DIM))
    ang = jnp.arange(s, dtype=F32)[:, None] * inv_freq[None, :]
    cos, sin = jnp.cos(ang), jnp.sin(ang)
    return jnp.tile(cos, (1, 4)), jnp.tile(jnp.concatenate([-sin, sin], axis=1), (1, 2))


def _place():
    return lax.axis_index("x"), lax.axis_index("y"), lax.axis_index("c")


def _other_chips(x, y):
    return [(1 - x, y), (x, 1 - y), (1 - x, 1 - y)]


def _gather_rider(blocks, relay_early=False):
    nm = len(blocks)

    def plan(ins, outs, sems):
        send_sems, recv_sems, local_sems = sems
        x, y, c = _place()
        me, sibling = (x, y, c), (x, y, 1 - c)
        chips = _other_chips(x, y)

        def rows(m, px, py, pc):
            r = ins[m].shape[0]
            return outs[m].at[pl.ds((4 * px + 2 * py + pc) * r, r), :]

        def copy(m, k, block, to, src=None):
            return pltpu.make_async_remote_copy(
                src_ref=rows(m, *block) if src is None else src, dst_ref=rows(m, *block),
                send_sem=send_sems.at[k * nm + m], recv_sem=recv_sems.at[k * nm + m],
                device_id=to, device_id_type=MESH)

        mine = [pltpu.make_async_copy(ins[m], rows(m, *me), local_sems.at[m]) for m in range(nm)]
        first = [copy(m, 0, me, sibling, src=ins[m]) for m in range(nm)]
        first += [copy(m, 1 + j, me, (*chip, c), src=ins[m]) for j, chip in enumerate(chips) for m in range(nm)]
        return me, sibling, chips, copy, mine, first

    def start(ins, outs, sems):
        *_, mine, first = plan(ins, outs, sems)
        for cp in mine + first:
            cp.start()

    def passed_on(ins, outs, sems):
        me, sibling, chips, copy, _, _ = plan(ins, outs, sems)
        return [copy(m, 4 + j, (*chip, me[2]), sibling) for j, chip in enumerate(chips) for m in range(nm)]

    def relay(ins, outs, sems):
        me, _, chips, copy, _, _ = plan(ins, outs, sems)
        forwards = passed_on(ins, outs, sems)
        for j, chip in enumerate(chips):
            for m in range(nm):
                copy(m, 1 + j, (*chip, me[2]), me).wait_recv()
                forwards[j * nm + m].start()

    def finish(ins, outs, sems):
        me, sibling, chips, copy, mine, first = plan(ins, outs, sems)
        for m in range(nm):
            copy(m, 0, sibling, me).wait_recv()
        for j, chip in enumerate(chips):
            for m in range(nm):
                copy(m, 4 + j, (*chip, 1 - me[2]), me).wait_recv()
        for cp in first + passed_on(ins, outs, sems):
            cp.wait_send()
        for cp in mine:
            cp.wait()

    return _Rider(
        arrays=list(blocks), out_shape=[jax.ShapeDtypeStruct((N_DEV * b.shape[0], b.shape[1]), b.dtype) for b in blocks],
        sems=[pltpu.SemaphoreType.DMA((7 * nm,)), pltpu.SemaphoreType.DMA((7 * nm,)), pltpu.SemaphoreType.DMA((nm,))],
        start=start, finish=finish, relay=relay, relay_early=relay_early)


def _exchange_rider(copies_of, arrays, out_shape, n_copies):
    def copies(ins, outs, sems):
        send_sems, recv_sems = sems
        return [pltpu.make_async_remote_copy(src_ref=src, dst_ref=dst, send_sem=send_sems.at[k], recv_sem=recv_sems.at[k],
                                             device_id=to, device_id_type=MESH)
                for k, (src, dst, to) in enumerate(copies_of(ins, outs))]

    def start(ins, outs, sems):
        for cp in copies(ins, outs, sems):
            cp.start()

    def finish(ins, outs, sems):
        cps = copies(ins, outs, sems)
        for cp in cps:
            cp.wait_recv()
        for cp in cps:
            cp.wait_send()

    return _Rider(arrays=list(arrays), out_shape=out_shape,
                  sems=[pltpu.SemaphoreType.DMA((n_copies,)), pltpu.SemaphoreType.DMA((n_copies,))], start=start, finish=finish)


def _sibling_rider(grads):
    def copies_of(ins, outs):
        x, y, c = _place()
        for g_ref, o_ref in zip(ins, outs):
            r = g_ref.shape[0] // N_DEV
            for q in range(N_CHIPS):
                yield g_ref.at[pl.ds((2 * q + 1 - c) * r, r), :], o_ref.at[pl.ds(q * r, r), :], (x, y, 1 - c)

    return _exchange_rider(copies_of, grads, [jax.ShapeDtypeStruct((g.shape[0] // 2, g.shape[1]), F32) for g in grads],
                           len(grads) * N_CHIPS)


def _chip_sum(grad, from_sibling, place):
    r = grad.shape[0] // N_DEV
    w = grad.shape[1]

    def body(place_ref, g_ref, s_ref, wire_ref, own_ref):
        total = g_ref[...] + s_ref[...]
        wire_ref[...] = total.astype(BF16)

        @pl.when(pl.program_id(0) == place_ref[1])
        def _():
            own_ref[...] = total

    grid_spec = pltpu.PrefetchScalarGridSpec(
        num_scalar_prefetch=1, grid=(N_CHIPS,),
        in_specs=[pl.BlockSpec((r, w), lambda q, p: (2 * q + p[0], 0)), pl.BlockSpec((r, w), lambda q, p: (q, 0))],
        out_specs=[pl.BlockSpec((r, w), lambda q, p: (q, 0)), pl.BlockSpec((r, w), lambda q, p: (0, 0))])
    return pl.pallas_call(
        body, name="grad_chip_sum", grid_spec=grid_spec,
        out_shape=[jax.ShapeDtypeStruct((N_CHIPS * r, w), BF16), jax.ShapeDtypeStruct((r, w), F32)],
        compiler_params=_params("arbitrary"),
    )(place, grad, from_sibling)


def _chips_rider(wires):
    def copies_of(ins, outs):
        x, y, c = _place()
        for w_ref, o_ref in zip(ins, outs):
            r = w_ref.shape[0] // N_CHIPS
            for j, (px, py) in enumerate(_other_chips(x, y)):
                yield w_ref.at[pl.ds((2 * px + py) * r, r), :], o_ref.at[pl.ds(j * r, r), :], (px, py, c)

    return _exchange_rider(copies_of, wires,
                           [jax.ShapeDtypeStruct((3 * (w.shape[0] // N_CHIPS), w.shape[1]), BF16) for w in wires], len(wires) * 3)


def _adamw_math(w, g, m, v):
    m = ADAM_B1 * m + (1.0 - ADAM_B1) * g
    v = ADAM_B2 * v + (1.0 - ADAM_B2) * jnp.square(g)
    m_hat = m / (1.0 - ADAM_B1 ** ADAM_STEP)
    v_hat = v / (1.0 - ADAM_B2 ** ADAM_STEP)
    delta = -ADAM_LR * (m_hat / (jnp.sqrt(v_hat) + ADAM_EPS) + ADAM_WD * w)
    return delta, m, v


def _reduce_adamw(own, received, w, m, v):
    r = own.shape[0]

    def body(own_ref, rec_ref, w_ref, m_ref, v_ref, g_ref, d_ref, nm_ref, nv_ref):
        g = ((own_ref[...] + rec_ref[0:r, :].astype(F32)) + rec_ref[r:2 * r, :].astype(F32)) + rec_ref[2 * r:, :].astype(F32)
        g_ref[...] = g
        d_ref[...], nm_ref[...], nv_ref[...] = _adamw_math(w_ref[...], g, m_ref[...], v_ref[...])

    shape = jax.ShapeDtypeStruct(own.shape, F32)
    return pl.pallas_call(
        body, name="reduce_adamw", in_specs=[VMEM] * 5, out_specs=[VMEM] * 4, out_shape=[shape] * 4,
        compiler_params=_params(),
    )(own, received, w, m, v)


SMALL_EARLY = (("w_pool", 65536), ("b_pool", 512), ("pool_scale", 512), ("b_out", 1024), ("g_ffn", 1024),
               ("g_final", 1024), ("loss", 1024))
SMALL_LATE = (("sinks", 8), ("g_mix", 1024), ("b_in", 1280))
SMALL = SMALL_EARLY + SMALL_LATE


def _small_rows(size):
    return -(-size // (8 * LANES)) * 8


def _pack_small(values, entries=SMALL):
    parts = []
    for name, size in entries:
        flat = values[name].reshape(-1).astype(F32)
        parts.append(jnp.pad(flat, (0, _small_rows(size) * LANES - size)).reshape(-1, LANES))
    return jnp.concatenate(parts, axis=0)


def _unpack_small(packed, shapes):
    out, row = {}, 0
    for name, size in SMALL:
        rows = _small_rows(size)
        if name in shapes:
            out[name] = packed[row:row + rows].reshape(-1)[:size].reshape(shapes[name])
        row += rows
    return out


def _small_sum_adamw(gathered_early, gathered_late, w, m, v):
    def body(e_ref, l_ref, w_ref, m_ref, v_ref, g_ref, d_ref, nm_ref, nv_ref):
        def total(ref):
            rows = ref.shape[0] // N_DEV
            acc = ref[0:rows, :]
            for dev in range(1, N_DEV):
                acc = acc + ref[dev * rows:(dev + 1) * rows, :]
            return acc

        g = jnp.concatenate([total(e_ref), total(l_ref)], axis=0)
        g_ref[...] = g
        d_ref[...], nm_ref[...], nv_ref[...] = _adamw_math(w_ref[...], g, m_ref[...], v_ref[...])

    shape = jax.ShapeDtypeStruct(w.shape, F32)
    return pl.pallas_call(
        body, name="small_sum_adamw", in_specs=[VMEM] * 5, out_specs=[VMEM] * 4, out_shape=[shape] * 4,
        compiler_params=_params(),
    )(gathered_early, gathered_late, w, m, v)


def kernel(x, g_mix, w_in, b_in, sinks, w_pool, b_pool, pool_scale, w_out, b_out, g_ffn, w_gate, w_up, w_down, g_final, loss_target, m_g_mix, m_w_in, m_b_in, m_sinks, m_w_pool, m_b_pool, m_pool_scale, m_w_out, m_b_out, m_g_ffn, m_w_gate, m_w_up, m_w_down, m_g_final, v_g_mix, v_w_in, v_b_in, v_sinks, v_w_pool, v_b_pool, v_pool_scale, v_w_out, v_b_out, v_g_ffn, v_w_gate, v_w_up, v_w_down, v_g_final):
    weights = dict(g_mix=g_mix, w_in=w_in, b_in=b_in, sinks=sinks, w_pool=w_pool, b_pool=b_pool, pool_scale=pool_scale,
                   w_out=w_out, b_out=b_out, g_ffn=g_ffn, w_gate=w_gate, w_up=w_up, w_down=w_down, g_final=g_final)
    mom1 = dict(g_mix=m_g_mix, w_in=m_w_in, b_in=m_b_in, sinks=m_sinks, w_pool=m_w_pool, b_pool=m_b_pool,
                pool_scale=m_pool_scale, w_out=m_w_out, b_out=m_b_out, g_ffn=m_g_ffn, w_gate=m_w_gate, w_up=m_w_up,
                w_down=m_w_down, g_final=m_g_final)
    mom2 = dict(g_mix=v_g_mix, w_in=v_w_in, b_in=v_b_in, sinks=v_sinks, w_pool=v_w_pool, b_pool=v_b_pool,
                pool_scale=v_pool_scale, w_out=v_w_out, b_out=v_b_out, g_ffn=v_g_ffn, w_gate=v_w_gate, w_up=v_w_up,
                w_down=v_w_down, g_final=v_g_final)
    order = ("g_mix", "w_in", "b_in", "sinks", "w_pool", "b_pool", "pool_scale", "w_out", "b_out", "g_ffn",
             "w_gate", "w_up", "w_down", "g_final")
    big = ("w_in", "w_out", "w_gate", "w_up", "w_down")
    transposed = ("w_in", "w_gate", "w_up")

    def row_shard(name, a):
        return a[0].T if name in transposed else a[0]

    shard = {n: row_shard(n, weights[n]).astype(BF16) for n in big}
    xs, target = x[0], loss_target[0]
    cos, sin = _rope_tables(xs.shape[0])
    wp_b = w_pool[0].astype(BF16)
    bp = b_pool.reshape(1, POOL_WIDTH)
    ps = pool_scale.reshape(1, POOL_WIDTH)
    g_fin = g_final.reshape(1, D_MODEL)
    px, py, pc = _place()
    place = jnp.stack([pc, 2 * px + py]).astype(jnp.int32)

    (win_t,) = _alone(_gather_rider([shard["w_in"]]), "gather_w_in")
    q, kz, vz, vt, mixed, pool, w_out_b, wg_t = _fwd_inproj(
        xs, g_mix, win_t, b_in, cos, sin, wp_b, bp, ps,
        rider=_gather_rider([shard["w_out"], shard["w_gate"]], relay_early=True))
    attn, lse, wu_t = _attn_fwd(q, kz, vt, sinks, rider=_gather_rider([shard["w_up"]]))
    x2, gate, up, act, wd = _fwd_outproj_ffn_act(attn, pool, w_out_b, b_out, xs, g_ffn, wg_t, wu_t,
                                                 rider=_gather_rider([shard["w_down"]], relay_early=True))
    dx3, sq, dg_final, d_wd = _fwd_down_loss(act, x2, wd, g_fin, target)

    dx2, dg_ffn, db_out, d_wg_t, d_wu_t, wd_sibling = _bwd_ffn(
        dx3, gate, up, x2, wd, wg_t, wu_t, g_ffn, rider=_sibling_rider([d_wd]))
    wd_sum = _chip_sum(d_wd, wd_sibling, place)
    in_grads = [d_wg_t, d_wu_t]
    dattn, du, d_wout, d_wpool, d_bpool, d_pscale, wd_received, *in_sibling = _bwd_outproj_pool(
        dx2, attn, pool, mixed, w_out_b, wp_b, bp, ps, rider=_join(_chips_rider([wd_sum[0]]), _sibling_rider(in_grads)))
    in_sums = [_chip_sum(g, s, place) for g, s in zip(in_grads, in_sibling)]
    small_early = _pack_small(dict(w_pool=d_wpool, b_pool=d_bpool, pool_scale=d_pscale, b_out=db_out, g_ffn=dg_ffn,
                                   g_final=dg_final, loss=sq), SMALL_EARLY)
    dq, dk, dv, d_sinks, *landed = _attn_bwd(
        q, kz, vz, dattn, lse, sinks,
        rider=_join(_chips_rider([wire for wire, _ in in_sums]), _sibling_rider([d_wout]), _gather_rider([small_early])))
    ffn_sums, ffn_received = in_sums + [wd_sum], landed[:2] + [wd_received]
    wout_sum = _chip_sum(d_wout, landed[2], place)
    gathered_early = landed[3]
    dx, d_win_t, d_bin, d_gmix = _bwd_inproj(dq, dk, dv, du, cos, sin, win_t, xs, g_mix, dx2)
    (win_sibling,) = _alone(_sibling_rider([d_win_t]), "grad_exchange_sibling")
    win_sum = _chip_sum(d_win_t, win_sibling, place)
    small_late = _pack_small(dict(sinks=d_sinks, g_mix=d_gmix, b_in=d_bin), SMALL_LATE)
    wout_received, win_received, gathered_late = _alone(
        _join(_chips_rider([wout_sum[0], win_sum[0]]), _gather_rider([small_late])), "grad_exchange_chips")

    reduced = dict(zip(("w_gate", "w_up", "w_down", "w_out", "w_in"),
                       zip(ffn_sums + [wout_sum, win_sum], list(ffn_received) + [wout_received, win_received])))
    grad, delta, new_m, new_v = {}, {}, {}, {}
    for n in big:
        (_, own), rec = reduced[n]
        results = _reduce_adamw(own, rec, row_shard(n, weights[n]), row_shard(n, mom1[n]), row_shard(n, mom2[n]))
        grad[n], delta[n], new_m[n], new_v[n] = [(a.T if n in transposed else a)[None] for a in results]

    shapes = {n: weights[n].shape for n in order if n not in big}
    zero_loss = jnp.zeros((1, D_MODEL), F32)
    packed = _small_sum_adamw(
        gathered_early, gathered_late, _pack_small({**weights, "loss": zero_loss}),
        _pack_small({**mom1, "loss": zero_loss}), _pack_small({**mom2, "loss": zero_loss}))
    for store, pk in zip((grad, delta, new_m, new_v), packed):
        store.update(_unpack_small(pk, shapes))
    loss_rows = _unpack_small(packed[0], {"loss": (D_MODEL,)})["loss"]
    loss = (0.5 / D_MODEL) * jnp.sum(loss_rows)

    return (loss, dx[None], *[grad[n] for n in order], *[delta[n] for n in order],
            *[new_m[n] for n in order], *[new_v[n] for n in order])
```

```python
from typing import Any, Callable, NamedTuple, Sequence

import jax
import jax.numpy as jnp
from jax import lax
from jax.experimental import pallas as pl
from jax.experimental.pallas import tpu as pltpu

D_MODEL = 1024
ATTN_WIDTH = 512
KV_WIDTH = 128
POOL_WIDTH = 512
HEAD_DIM = 64
N_Q_HEADS = 8
N_KV_HEADS = 2
GQA_GROUP = 4
BLOCK = 128
POOL_SIZES = (2, 4, 8, 16)
POOL_GROUP_WIDTH = 128
POOL_HALO = 16
IN_WIDTH = 1280
D_FF = 2816
RMS_EPS = 1e-5
ROPE_THETA = 10000.0
Q_SCALE = HEAD_DIM ** -0.5

ADAM_LR = 0.001
ADAM_B1 = 0.9
ADAM_B2 = 0.999
ADAM_EPS = 1e-08
ADAM_WD = 0.01
ADAM_STEP = 10

N_DEV = 8
N_CHIPS = 4
LANES = 128
VMEM_LIMIT_BYTES = 60 * 1024 * 1024

F32 = jnp.float32
BF16 = jnp.bfloat16
MESH = pl.DeviceIdType.MESH
HBM = pl.BlockSpec(memory_space=pltpu.HBM)
VMEM = pl.BlockSpec(memory_space=pltpu.VMEM)


def _params(*semantics):
    return pltpu.CompilerParams(dimension_semantics=semantics or None, vmem_limit_bytes=VMEM_LIMIT_BYTES)


def _nn(a, b):
    return jnp.dot(a, b, preferred_element_type=F32)


def _nt(a, b):
    return lax.dot_general(a, b, (((1,), (1,)), ((), ())), preferred_element_type=F32)


def _tn(a, b):
    return lax.dot_general(a, b, (((0,), (0,)), ((), ())), preferred_element_type=F32)


def _full(shape):
    return pl.BlockSpec(shape, lambda *_: (0,) * len(shape))


def _rows(tm, width):
    return pl.BlockSpec((tm, width), lambda i, *_: (i, 0))


def _nothing(ins, outs, sems):
    del ins, outs, sems


RELAY_STEPS_BEFORE_LAST = 2


class _Rider(NamedTuple):
    arrays: Sequence[Any]
    out_shape: Sequence[Any]
    sems: Sequence[Any]
    start: Callable[..., None]
    finish: Callable[..., None]
    relay: Callable[..., None] = _nothing
    relay_early: bool = False


def _gridded(body, rider, *, name, grid, in_specs, out_specs, out_shape, scratch_shapes, args):
    params = _params("arbitrary")
    if rider is None:
        return pl.pallas_call(body, name=name, grid=grid, in_specs=in_specs, out_specs=out_specs, out_shape=out_shape,
                              scratch_shapes=scratch_shapes, compiler_params=params)(*args)
    bounds, total = [], 0
    for n in (len(in_specs), len(rider.arrays), len(out_specs), len(rider.out_shape), len(scratch_shapes), len(rider.sems)):
        bounds.append((total, total + n))
        total += n
    last = grid[0] - 1
    relay_step = max(last - RELAY_STEPS_BEFORE_LAST, 0) if rider.relay_early else last

    def riding(*refs):
        ins, r_ins, outs, r_outs, scratch, r_sems = (refs[lo:hi] for lo, hi in bounds)

        @pl.when(pl.program_id(0) == 0)
        def _():
            rider.start(r_ins, r_outs, r_sems)

        body(*ins, *outs, *scratch)

        @pl.when(pl.program_id(0) == relay_step)
        def _():
            rider.relay(r_ins, r_outs, r_sems)

        @pl.when(pl.program_id(0) == last)
        def _():
            rider.finish(r_ins, r_outs, r_sems)

    return pl.pallas_call(
        riding, name=name, grid=grid, in_specs=list(in_specs) + [HBM] * len(rider.arrays),
        out_specs=list(out_specs) + [HBM] * len(rider.out_shape), out_shape=list(out_shape) + list(rider.out_shape),
        scratch_shapes=list(scratch_shapes) + list(rider.sems), compiler_params=params)(*args, *rider.arrays)


def _join(*riders):
    def phase(which):
        def run(ins, outs, sems):
            i = o = s = 0
            for r in riders:
                ni, no, ns = len(r.arrays), len(r.out_shape), len(r.sems)
                getattr(r, which)(ins[i:i + ni], outs[o:o + no], sems[s:s + ns])
                i, o, s = i + ni, o + no, s + ns
        return run

    return _Rider(arrays=[a for r in riders for a in r.arrays], out_shape=[a for r in riders for a in r.out_shape],
                  sems=[a for r in riders for a in r.sems], start=phase("start"), finish=phase("finish"), relay=phase("relay"),
                  relay_early=all(r.relay_early for r in riders if r.relay is not _nothing))


def _alone(rider, name):
    n_in, n_out = len(rider.arrays), len(rider.out_shape)

    def body(*refs):
        parts = refs[:n_in], refs[n_in:n_in + n_out], refs[n_in + n_out:]
        rider.start(*parts)
        rider.relay(*parts)
        rider.finish(*parts)

    return pl.pallas_call(body, name=name, in_specs=[HBM] * n_in, out_specs=[HBM] * n_out, out_shape=list(rider.out_shape),
                          scratch_shapes=list(rider.sems))(*rider.arrays)


def _rot_half(t):
    n = t.shape[1]
    lane = lax.broadcasted_iota(jnp.int32, t.shape, 1)
    return jnp.where((lane % HEAD_DIM) < HEAD_DIM // 2, pltpu.roll(t, n - HEAD_DIM // 2, 1), pltpu.roll(t, HEAD_DIM // 2, 1))


def _rope(t, cos, sin):
    reps = t.shape[1] // LANES
    if reps > 1:
        cos, sin = jnp.tile(cos, (1, reps)), jnp.tile(sin, (1, reps))
    return t * cos + _rot_half(t) * sin


def _rope_bwd(d, cos, sin):
    reps = d.shape[1] // LANES
    if reps > 1:
        cos, sin = jnp.tile(cos, (1, reps)), jnp.tile(sin, (1, reps))
    return d * cos + _rot_half(d * sin)


KV_SPREAD = 4 * LANES


def _spread_kv(t):
    low = lax.broadcasted_iota(jnp.int32, t.shape, 1) < HEAD_DIM
    swapped = pltpu.roll(t, HEAD_DIM, 1)
    zero = jnp.zeros_like(t)
    return jnp.concatenate([jnp.where(low, t, zero), jnp.where(low, zero, swapped),
                            jnp.where(low, swapped, zero), jnp.where(low, zero, t)], axis=1)


def _rms(x):
    r = lax.rsqrt(jnp.mean(x * x, axis=-1, keepdims=True) + RMS_EPS)
    return x * r, r


def _rms_bwd(dh, n, r, g):
    dn = dh * g
    dx = r * (dn - n * jnp.mean(dn * n, axis=-1, keepdims=True))
    return dx, jnp.sum(dh * n, axis=0, keepdims=True)


def _token_tile(s):
    return min(512, s)


def _window_mean(window_sum, pos, size):
    head = window_sum[:POOL_HALO, :] / jnp.minimum(pos[:POOL_HALO, :] + 1, size).astype(F32)
    return jnp.concatenate([head, window_sum[POOL_HALO:, :] * (1.0 / size)], axis=0)


def _fwd_inproj(x, g_mix, win_t, b_in, cos, sin, w_pool, b_pool, pool_scale, rider=None):
    s = x.shape[0]
    tm = _token_tile(s)

    def body(x_ref, g_ref, w_ref, b_ref, cos_ref, sin_ref, wp_ref, bp_ref, ps_ref,
             q_ref, k_ref, v_ref, vt_ref, mix_ref, pool_ref, tail_ref):
        i = pl.program_id(0)

        @pl.when(i == 0)
        def _():
            tail_ref[...] = jnp.zeros_like(tail_ref)

        n, _ = _rms(x_ref[...])
        h = (n * g_ref[...]).astype(BF16)
        z = _nt(h, w_ref[...]) + b_ref[...]
        cos_t, sin_t = cos_ref[...], sin_ref[...]
        q_ref[...] = (_rope(z[:, :ATTN_WIDTH], cos_t, sin_t) * Q_SCALE).astype(BF16)
        k_ref[...] = _spread_kv(_rope(z[:, ATTN_WIDTH:ATTN_WIDTH + KV_WIDTH], cos_t, sin_t)).astype(BF16)
        vz = _spread_kv(z[:, ATTN_WIDTH + KV_WIDTH:ATTN_WIDTH + 2 * KV_WIDTH])
        v_ref[...] = vz.astype(BF16)
        vt_ref[...] = vz.T.astype(BF16)
        u = z[:, ATTN_WIDTH + 2 * KV_WIDTH:]
        u_ext = jnp.concatenate([tail_ref[...], u], axis=0)
        tail_ref[...] = u[tm - POOL_HALO:, :]
        pos = lax.broadcasted_iota(jnp.int32, (tm, POOL_GROUP_WIDTH), 0) + i * tm
        for g, size in enumerate(POOL_SIZES):
            cols = slice(g * POOL_GROUP_WIDTH, (g + 1) * POOL_GROUP_WIDTH)
            a = u_ext[:, cols]
            shift = 1
            while shift < size:
                a = a + pltpu.roll(a, shift, 0)
                shift *= 2
            mixed = (_window_mean(a[POOL_HALO:, :], pos, size) - u[:, cols]).astype(BF16)
            pre = _nn(mixed, wp_ref[g]) + bp_ref[:, cols]
            mix_ref[:, cols] = mixed
            pool_ref[:, cols] = (pre * ps_ref[:, cols]).astype(BF16)

    bf = lambda w: jax.ShapeDtypeStruct((s, w), BF16)
    return _gridded(
        body, rider, name="fwd_inproj", grid=(s // tm,),
        in_specs=[_rows(tm, D_MODEL), _full((1, D_MODEL)), _full((IN_WIDTH, D_MODEL)), _full((1, IN_WIDTH)),
                  _rows(tm, LANES), _rows(tm, LANES), _full((4, POOL_GROUP_WIDTH, POOL_GROUP_WIDTH)),
                  _full((1, POOL_WIDTH)), _full((1, POOL_WIDTH))],
        out_specs=[_rows(tm, ATTN_WIDTH), _rows(tm, KV_SPREAD), _rows(tm, KV_SPREAD),
                   pl.BlockSpec((KV_SPREAD, tm), lambda i: (0, i)), _rows(tm, POOL_WIDTH), _rows(tm, POOL_WIDTH)],
        out_shape=[bf(ATTN_WIDTH), bf(KV_SPREAD), bf(KV_SPREAD), jax.ShapeDtypeStruct((KV_SPREAD, s), BF16),
                   bf(POOL_WIDTH), bf(POOL_WIDTH)],
        scratch_shapes=[pltpu.VMEM((POOL_HALO, POOL_WIDTH), F32)],
        args=(x, g_mix, win_t, b_in, cos, sin, w_pool, b_pool, pool_scale))


ATTN_TILE = 1024
PAIR = 2 * LANES


def _band_masks(tile):
    j = lax.broadcasted_iota(jnp.int32, (4 * BLOCK, 2 * BLOCK), 0) % (2 * BLOCK)
    r = lax.broadcasted_iota(jnp.int32, (4 * BLOCK, 2 * BLOCK), 1) % BLOCK
    band = (j > r) & (j <= r + BLOCK)
    return band & ((tile > 0) | (j >= BLOCK)), band


def _band(cur_ref, prev_ref, b, kv):
    halves = []
    for half in range(2):
        cols = slice(kv * PAIR + half * LANES, kv * PAIR + (half + 1) * LANES)
        if b == 0:
            halves.append(jnp.concatenate([prev_ref[:, cols], cur_ref[0:BLOCK, cols]], axis=0))
        else:
            halves.append(cur_ref[(b - 1) * BLOCK:(b + 1) * BLOCK, cols])
    return jnp.concatenate(halves, axis=0)


def _stack_pair(ref, rows, kv):
    return jnp.concatenate([ref[rows, kv * PAIR:kv * PAIR + LANES], ref[rows, kv * PAIR + LANES:(kv + 1) * PAIR]], axis=0)


def _pair_heads(kv, half):
    return GQA_GROUP * kv + half, GQA_GROUP * kv + 2 + half


def _band_t(cur_ref, prev_ref, b, kv):
    halves = []
    for half in range(2):
        lanes = slice(kv * PAIR + half * LANES, kv * PAIR + (half + 1) * LANES)
        if b == 0:
            halves.append(jnp.concatenate([prev_ref[lanes, :], cur_ref[lanes, 0:BLOCK]], axis=1))
        else:
            halves.append(cur_ref[lanes, (b - 1) * BLOCK:(b + 1) * BLOCK])
    return jnp.concatenate(halves, axis=1)


def _reduce_rows(x, op, reduce):
    while x.shape[0] > 8:
        half = x.shape[0] // 2
        x = op(x[:half], x[half:])
    return reduce(x, axis=0, keepdims=True)


def _per_query(ref, rows, top, bottom):
    return jnp.concatenate([ref[top:top + 1, rows], ref[bottom:bottom + 1, rows]], axis=1)


def _sink_per_query(sink_ref, top, bottom):
    first_slab = lax.broadcasted_iota(jnp.int32, (1, 2 * BLOCK), 1) < BLOCK
    return jnp.where(first_slab, sink_ref[:, top:top + 1], sink_ref[:, bottom:bottom + 1])


def _attn_fwd(q, kz, vt, sinks, rider=None):
    s = q.shape[0]
    tq = min(ATTN_TILE, s)

    def body(q_ref, k_ref, kp_ref, vt_ref, vtp_ref, sink_ref, o_ref, lse_ref):
        first, band = _band_masks(pl.program_id(0))
        chains = [(b, kv) for b in range(tq // BLOCK) for kv in range(N_KV_HEADS)]

        def scores(b, kv):
            rows = slice(b * BLOCK, (b + 1) * BLOCK)
            return _nt(_band(k_ref, kp_ref, b, kv), _stack_pair(q_ref, rows, kv))

        def store(b, kv, ot):
            rows = slice(b * BLOCK, (b + 1) * BLOCK)
            o = ot.T.astype(BF16)
            o_ref[rows, kv * PAIR:kv * PAIR + LANES] = o[:BLOCK]
            o_ref[rows, kv * PAIR + LANES:(kv + 1) * PAIR] = o[BLOCK:]

        ahead = scores(*chains[0])
        behind = None
        for n, (b, kv) in enumerate(chains):
            rows = slice(b * BLOCK, (b + 1) * BLOCK)
            st = jnp.where(first if b == 0 else band, ahead, -jnp.inf)
            if n + 1 < len(chains):
                ahead = scores(*chains[n + 1])
            probs = []
            for half in range(2):
                top, bottom = _pair_heads(kv, half)
                sink = _sink_per_query(sink_ref, top, bottom)
                sh = st[half * 2 * BLOCK:(half + 1) * 2 * BLOCK, :]
                m = jnp.maximum(_reduce_rows(sh, jnp.maximum, jnp.max), sink)
                p = jnp.exp(sh - m)
                denom = _reduce_rows(p, jnp.add, jnp.sum) + jnp.exp(sink - m)
                probs.append((p * (1.0 / denom)).astype(BF16))
                lse = m + jnp.log(denom)
                lse_ref[top:top + 1, rows] = lse[:, :BLOCK]
                lse_ref[bottom:bottom + 1, rows] = lse[:, BLOCK:]
            ot = _nn(_band_t(vt_ref, vtp_ref, b, kv), jnp.concatenate(probs, axis=0))
            if behind is not None:
                store(*behind)
            behind = (b, kv, ot)
        store(*behind)

    per = tq // BLOCK
    cur = lambda w: pl.BlockSpec((tq, w), lambda i: (i, 0))
    prev = pl.BlockSpec((BLOCK, KV_SPREAD), lambda i: (jnp.maximum(per * i - 1, 0), 0))
    cur_t = pl.BlockSpec((KV_SPREAD, tq), lambda i: (0, i))
    prev_t = pl.BlockSpec((KV_SPREAD, BLOCK), lambda i: (0, jnp.maximum(per * i - 1, 0)))
    return _gridded(
        body, rider, name="attn_fwd", grid=(s // tq,),
        in_specs=[cur(ATTN_WIDTH), cur(KV_SPREAD), prev, cur_t, prev_t, _full((1, N_Q_HEADS))],
        out_specs=[cur(ATTN_WIDTH), pl.BlockSpec((N_Q_HEADS, tq), lambda i: (0, i))],
        out_shape=[jax.ShapeDtypeStruct((s, ATTN_WIDTH), BF16), jax.ShapeDtypeStruct((N_Q_HEADS, s), F32)],
        scratch_shapes=[], args=(q, kz, kz, vt, vt, sinks))


FF_CHUNK = 256
TN_ROW_CHUNK = 256


def _resident(shape):
    return pl.BlockSpec(shape, lambda *_: (0,) * len(shape), pipeline_mode=pl.Buffered(1))


def _flush_rows(acc_ref, out_ref, sem, rows, is_last):
    @pl.when(is_last)
    def _():
        pltpu.make_async_copy(acc_ref.at[rows, :], out_ref.at[rows, :], sem).start()


def _flush_wait(acc_ref, out_ref, sem, is_last):
    @pl.when(is_last)
    def _():
        pltpu.make_async_copy(acc_ref, out_ref, sem).wait()


def _accumulate_tn(acc_ref, a_ref, b, out_ref, sem, is_last):
    for m0 in range(0, acc_ref.shape[0], TN_ROW_CHUNK):
        rows = slice(m0, m0 + TN_ROW_CHUNK)
        acc_ref[rows, :] += _tn(a_ref[:, rows], b)
        _flush_rows(acc_ref, out_ref, sem, rows, is_last)
    _flush_wait(acc_ref, out_ref, sem, is_last)


def _fwd_outproj_ffn_act(attn, pool, w_out, b_out, x, g_ffn, wg_t, wu_t, rider=None):
    s = x.shape[0]
    tm = _token_tile(s)

    def body(a_ref, p_ref, w_ref, b_ref, x_ref, g_ref, wg_ref, wu_ref, x2_ref, gate_ref, up_ref, act_ref):
        x2 = x_ref[...] + _nn(a_ref[...], w_ref[:ATTN_WIDTH, :]) + _nn(p_ref[...], w_ref[ATTN_WIDTH:, :]) + b_ref[...]
        x2_ref[...] = x2
        n, _ = _rms(x2)
        h = (n * g_ref[...]).astype(BF16)

        def products(c0):
            return _nt(h, wg_ref[c0:c0 + FF_CHUNK, :]), _nt(h, wu_ref[c0:c0 + FF_CHUNK, :])

        ahead = products(0)
        for c0 in range(0, D_FF, FF_CHUNK):
            cols = slice(c0, c0 + FF_CHUNK)
            gate, up = ahead
            if c0 + FF_CHUNK < D_FF:
                ahead = products(c0 + FF_CHUNK)
            gate_ref[:, cols] = gate.astype(BF16)
            up_ref[:, cols] = up.astype(BF16)
            act_ref[:, cols] = (gate * jax.nn.sigmoid(gate) * up).astype(BF16)

    act_shape = jax.ShapeDtypeStruct((s, D_FF), BF16)
    return _gridded(
        body, rider, name="fwd_outproj_ffn_act", grid=(s // tm,),
        in_specs=[_rows(tm, ATTN_WIDTH), _rows(tm, POOL_WIDTH), _resident((D_MODEL, D_MODEL)), _full((1, D_MODEL)),
                  _rows(tm, D_MODEL), _full((1, D_MODEL)), _resident((D_FF, D_MODEL)), _resident((D_FF, D_MODEL))],
        out_specs=[_rows(tm, D_MODEL)] + [_rows(tm, D_FF)] * 3,
        out_shape=[jax.ShapeDtypeStruct((s, D_MODEL), F32)] + [act_shape] * 3,
        scratch_shapes=[], args=(attn, pool, w_out, b_out, x, g_ffn, wg_t, wu_t))


def _fwd_down_loss(act, x2, wd, g_final, target):
    s = x2.shape[0]
    tm = _token_tile(s)
    last = s // tm - 1

    def body(a_ref, x2_ref, wd_ref, g_ref, t_ref, dx3_ref, sq_ref, dg_ref, dwd_ref, acc_ref, sem):
        @pl.when(pl.program_id(0) == 0)
        def _():
            sq_ref[...] = jnp.zeros_like(sq_ref)
            dg_ref[...] = jnp.zeros_like(dg_ref)
            acc_ref[...] = jnp.zeros_like(acc_ref)

        x3 = x2_ref[...] + _nn(a_ref[...], wd_ref[...])
        n, r = _rms(x3)
        g = g_ref[...]
        diff = n * g - t_ref[...]
        sq_ref[...] += jnp.sum(diff * diff, axis=0, keepdims=True)
        dx3, dg = _rms_bwd(diff * (1.0 / D_MODEL), n, r, g)
        dg_ref[...] += dg
        dx3_ref[...] = dx3
        _accumulate_tn(acc_ref, a_ref, dx3.astype(BF16), dwd_ref, sem, pl.program_id(0) == last)

    return pl.pallas_call(
        body, name="fwd_down_loss", grid=(s // tm,),
        in_specs=[_rows(tm, D_FF), _rows(tm, D_MODEL), _resident((D_FF, D_MODEL)), _full((1, D_MODEL)), _rows(tm, D_MODEL)],
        out_specs=[_rows(tm, D_MODEL), _full((1, D_MODEL)), _full((1, D_MODEL)), HBM],
        out_shape=[jax.ShapeDtypeStruct((s, D_MODEL), F32),
                   jax.ShapeDtypeStruct((1, D_MODEL), F32), jax.ShapeDtypeStruct((1, D_MODEL), F32),
                   jax.ShapeDtypeStruct((D_FF, D_MODEL), F32)],
        scratch_shapes=[pltpu.VMEM((D_FF, D_MODEL), F32), pltpu.SemaphoreType.DMA],
        compiler_params=_params("arbitrary"),
    )(act, x2, wd, g_final, target)


FFN_BWD_TILE = 256


def _bwd_ffn(dx3, gate, up, x2, wd, wg_t, wu_t, g_ffn, rider=None):
    s = x2.shape[0]
    tm = min(FFN_BWD_TILE, s)
    last = s // tm - 1

    def body(dx3_ref, gate_ref, up_ref, x2_ref, wd_ref, wg_ref, wu_ref, g_ref,
             dx2_ref, dg_ref, db_ref, dwg_ref, dwu_ref, dgate_ref, dup_ref, accg_ref, accu_ref, sems):
        @pl.when(pl.program_id(0) == 0)
        def _():
            dg_ref[...] = jnp.zeros_like(dg_ref)
            db_ref[...] = jnp.zeros_like(db_ref)
            accg_ref[...] = jnp.zeros_like(accg_ref)
            accu_ref[...] = jnp.zeros_like(accu_ref)

        dx3b = dx3_ref[...].astype(BF16)
        g = g_ref[...]
        n, r = _rms(x2_ref[...])
        h = (n * g).astype(BF16)
        ahead = _nt(dx3b, wd_ref[0:FF_CHUNK, :])
        for c0 in range(0, D_FF, FF_CHUNK):
            cols = slice(c0, c0 + FF_CHUNK)
            dact = ahead
            if c0 + FF_CHUNK < D_FF:
                ahead = _nt(dx3b, wd_ref[c0 + FF_CHUNK:c0 + 2 * FF_CHUNK, :])
            gate = gate_ref[:, cols].astype(F32)
            up = up_ref[:, cols].astype(F32)
            sig = jax.nn.sigmoid(gate)
            silu = gate * sig
            dup = (dact * silu).astype(BF16)
            dgate = (dact * up * (sig + silu * (1.0 - sig))).astype(BF16)
            dup_ref[:, cols] = dup
            dgate_ref[:, cols] = dgate
            accg_ref[cols, :] += _tn(dgate, h)
            accu_ref[cols, :] += _tn(dup, h)
        dh2 = _nn(dgate_ref[...], wg_ref[...]) + _nn(dup_ref[...], wu_ref[...])
        dx, dg = _rms_bwd(dh2, n, r, g)
        dx2 = dx3_ref[...] + dx
        dg_ref[...] += dg
        db_ref[...] += jnp.sum(dx2, axis=0, keepdims=True)
        dx2_ref[...] = dx2

        @pl.when(pl.program_id(0) == last)
        def _():
            outs = [pltpu.make_async_copy(accg_ref, dwg_ref, sems.at[0]), pltpu.make_async_copy(accu_ref, dwu_ref, sems.at[1])]
            for cp in outs:
                cp.start()
            for cp in outs:
                cp.wait()

    grad_shape = jax.ShapeDtypeStruct((D_FF, D_MODEL), F32)
    weight = _resident((D_FF, D_MODEL))
    return _gridded(
        body, rider, name="bwd_ffn", grid=(s // tm,),
        in_specs=[_rows(tm, D_MODEL), _rows(tm, D_FF), _rows(tm, D_FF),
                  _rows(tm, D_MODEL), weight, weight, weight, _full((1, D_MODEL))],
        out_specs=[_rows(tm, D_MODEL), _full((1, D_MODEL)), _full((1, D_MODEL)), HBM, HBM],
        out_shape=[jax.ShapeDtypeStruct((s, D_MODEL), F32),
                   jax.ShapeDtypeStruct((1, D_MODEL), F32), jax.ShapeDtypeStruct((1, D_MODEL), F32), grad_shape, grad_shape],
        scratch_shapes=[pltpu.VMEM((tm, D_FF), BF16), pltpu.VMEM((tm, D_FF), BF16),
                        pltpu.VMEM((D_FF, D_MODEL), F32), pltpu.VMEM((D_FF, D_MODEL), F32), pltpu.SemaphoreType.DMA((2,))],
        args=(dx3, gate, up, x2, wd, wg_t, wu_t, g_ffn))


def _bwd_outproj_pool(dx2, attn, pool, mixed, w_out, w_pool, b_pool, pool_scale, rider=None):
    s = dx2.shape[0]
    tm = min(2 * _token_tile(s), s)
    nt = s // tm

    def body(dx_ref, a_ref, p_ref, mix_ref, w_ref, wp_ref, bp_ref, ps_ref,
             dattn_ref, du_ref, dwout_ref, dwp_ref, dbp_ref, dps_ref, head_ref):
        step = pl.program_id(0)
        tile = nt - 1 - step

        @pl.when(step == 0)
        def _():
            head_ref[...] = jnp.zeros_like(head_ref)
            dwout_ref[...] = jnp.zeros_like(dwout_ref)
            dwp_ref[...] = jnp.zeros_like(dwp_ref)
            dbp_ref[...] = jnp.zeros_like(dbp_ref)
            dps_ref[...] = jnp.zeros_like(dps_ref)

        dx = dx_ref[...].astype(BF16)
        dwout_ref[:ATTN_WIDTH, :] += _tn(a_ref[...], dx)
        dwout_ref[ATTN_WIDTH:, :] += _tn(p_ref[...], dx)
        dcat = _nt(dx, w_ref[...])
        dattn_ref[...] = dcat[:, :ATTN_WIDTH].astype(BF16)
        dpool = dcat[:, ATTN_WIDTH:]
        pos = lax.broadcasted_iota(jnp.int32, (tm, POOL_GROUP_WIDTH), 0) + tile * tm
        head = head_ref[...]
        n_ext = tm + POOL_HALO
        for g, size in enumerate(POOL_SIZES):
            cols = slice(g * POOL_GROUP_WIDTH, (g + 1) * POOL_GROUP_WIDTH)
            mixed_g = mix_ref[:, cols]
            pre = _nn(mixed_g, wp_ref[g]) + bp_ref[:, cols]
            dy = dpool[:, cols]
            dps_ref[:, cols] += jnp.sum(dy * pre, axis=0, keepdims=True)
            dpre = dy * ps_ref[:, cols]
            dbp_ref[:, cols] += jnp.sum(dpre, axis=0, keepdims=True)
            dpre_b = dpre.astype(BF16)
            dwp_ref[g] += _tn(mixed_g, dpre_b)
            dmixed = _nt(dpre_b, wp_ref[g])
            w = _window_mean(dmixed, pos, size)
            head_ref[:, cols] = w[:POOL_HALO, :]
            a = jnp.concatenate([w, head[:, cols]], axis=0)
            shift = 1
            while shift < size:
                a = a + pltpu.roll(a, n_ext - shift, 0)
                shift *= 2
            du_ref[:, cols] = (a[:tm, :] - dmixed).astype(BF16)

    rev = lambda w: pl.BlockSpec((tm, w), lambda i: (nt - 1 - i, 0))
    return _gridded(
        body, rider, name="bwd_outproj_pool", grid=(nt,),
        in_specs=[rev(D_MODEL), rev(ATTN_WIDTH), rev(POOL_WIDTH), rev(POOL_WIDTH), _full((D_MODEL, D_MODEL)),
                  _full((4, POOL_GROUP_WIDTH, POOL_GROUP_WIDTH)), _full((1, POOL_WIDTH)), _full((1, POOL_WIDTH))],
        out_specs=[rev(ATTN_WIDTH), rev(POOL_WIDTH), _full((D_MODEL, D_MODEL)),
                   _full((4, POOL_GROUP_WIDTH, POOL_GROUP_WIDTH)), _full((1, POOL_WIDTH)), _full((1, POOL_WIDTH))],
        out_shape=[jax.ShapeDtypeStruct((s, ATTN_WIDTH), BF16), jax.ShapeDtypeStruct((s, POOL_WIDTH), BF16),
                   jax.ShapeDtypeStruct((D_MODEL, D_MODEL), F32),
                   jax.ShapeDtypeStruct((4, POOL_GROUP_WIDTH, POOL_GROUP_WIDTH), F32),
                   jax.ShapeDtypeStruct((1, POOL_WIDTH), F32), jax.ShapeDtypeStruct((1, POOL_WIDTH), F32)],
        scratch_shapes=[pltpu.VMEM((POOL_HALO, POOL_WIDTH), F32)],
        args=(dx2, attn, pool, mixed, w_out, w_pool, b_pool, pool_scale))


def _fold_spread(t):
    low = lax.broadcasted_iota(jnp.int32, (2 * BLOCK, LANES), 1) < HEAD_DIM
    kept = jnp.where(low, t[:2 * BLOCK, :], t[2 * BLOCK:, :])
    return kept + pltpu.roll(kept, HEAD_DIM, 1)


def _attn_bwd(q, kz, vz, dattn, lse, sinks, rider=None):
    s = q.shape[0]
    tq = min(ATTN_TILE, s)
    nt = s // tq
    per = tq // BLOCK

    def body(q_ref, k_ref, kp_ref, v_ref, vp_ref, do_ref, lse_ref, sink_ref,
             dq_ref, dk_ref, dv_ref, dsink_ref, dk_acc, dv_acc, dk_carry, dv_carry):
        step = pl.program_id(0)

        @pl.when(step == 0)
        def _():
            dk_carry[...] = jnp.zeros_like(dk_carry)
            dv_carry[...] = jnp.zeros_like(dv_carry)
            dsink_ref[...] = jnp.zeros_like(dsink_ref)

        dk_acc[0:tq, :] = jnp.zeros((tq, KV_WIDTH), F32)
        dv_acc[0:tq, :] = jnp.zeros((tq, KV_WIDTH), F32)
        dk_acc[tq:, :] = dk_carry[...]
        dv_acc[tq:, :] = dv_carry[...]
        first, band = _band_masks(nt - 1 - step)
        low = lax.broadcasted_iota(jnp.int32, (2 * BLOCK, LANES), 1) < HEAD_DIM
        chains = [(b, kv) for b in range(per) for kv in range(N_KV_HEADS)]

        def operands(b, kv):
            rows = slice(b * BLOCK, (b + 1) * BLOCK)
            qab = _stack_pair(q_ref, rows, kv)
            doab = _stack_pair(do_ref, rows, kv)
            kzb = _band(k_ref, kp_ref, b, kv)
            return qab, doab, kzb, _nt(kzb, qab), _nt(_band(v_ref, vp_ref, b, kv), doab)

        folded = {}

        def finish(b, kv, dqab, dkz, dvz):
            rows = slice(b * BLOCK, (b + 1) * BLOCK)
            dq_ref[rows, kv * PAIR:kv * PAIR + LANES] = dqab[:BLOCK] * Q_SCALE
            dq_ref[rows, kv * PAIR + LANES:(kv + 1) * PAIR] = dqab[BLOCK:] * Q_SCALE
            folded[kv] = (_fold_spread(dkz), _fold_spread(dvz))
            if kv == N_KV_HEADS - 1:
                band_rows = slice(b * BLOCK, (b + 2) * BLOCK)
                dk_acc[band_rows, :] += jnp.where(low, folded[0][0], folded[1][0])
                dv_acc[band_rows, :] += jnp.where(low, folded[0][1], folded[1][1])

        ahead = operands(*chains[0])
        behind = None
        for n, (b, kv) in enumerate(chains):
            rows = slice(b * BLOCK, (b + 1) * BLOCK)
            mask = first if b == 0 else band
            qab, doab, kzb, st, dpt = ahead
            if n + 1 < len(chains):
                ahead = operands(*chains[n + 1])
            probs, dscores = [], []
            for half in range(2):
                top, bottom = _pair_heads(kv, half)
                keys = slice(half * 2 * BLOCK, (half + 1) * 2 * BLOCK)
                lse_h = _per_query(lse_ref, rows, top, bottom)
                p = jnp.where(mask[keys, :], jnp.exp(st[keys, :] - lse_h), 0.0)
                dph = dpt[keys, :]
                delta = _reduce_rows(p * dph, jnp.add, jnp.sum)
                probs.append(p.astype(BF16))
                dscores.append((p * (dph - delta)).astype(BF16))
                leak = jnp.exp(_sink_per_query(sink_ref, top, bottom) - lse_h) * delta
                dsink_ref[:, top:top + 1] -= jnp.sum(leak[:, :BLOCK], axis=1, keepdims=True)
                dsink_ref[:, bottom:bottom + 1] -= jnp.sum(leak[:, BLOCK:], axis=1, keepdims=True)
            ds = jnp.concatenate(dscores, axis=0)
            results = (_tn(ds, kzb), _nn(ds, qab), _nn(jnp.concatenate(probs, axis=0), doab))
            if behind is not None:
                finish(*behind)
            behind = (b, kv, *results)
        finish(*behind)
        dk_ref[...] = dk_acc[BLOCK:, :]
        dv_ref[...] = dv_acc[BLOCK:, :]
        dk_carry[...] = dk_acc[0:BLOCK, :]
        dv_carry[...] = dv_acc[0:BLOCK, :]

    cur = lambda w: pl.BlockSpec((tq, w), lambda i: (nt - 1 - i, 0))
    prev = pl.BlockSpec((BLOCK, KV_SPREAD), lambda i: (jnp.maximum(per * (nt - 1 - i) - 1, 0), 0))
    acc = pltpu.VMEM((tq + BLOCK, KV_WIDTH), F32)
    carry = pltpu.VMEM((BLOCK, KV_WIDTH), F32)
    return _gridded(
        body, rider, name="attn_bwd", grid=(nt,),
        in_specs=[cur(ATTN_WIDTH), cur(KV_SPREAD), prev, cur(KV_SPREAD), prev, cur(ATTN_WIDTH),
                  pl.BlockSpec((N_Q_HEADS, tq), lambda i: (0, nt - 1 - i)), _full((1, N_Q_HEADS))],
        out_specs=[cur(ATTN_WIDTH), cur(KV_WIDTH), cur(KV_WIDTH), _full((1, N_Q_HEADS))],
        out_shape=[jax.ShapeDtypeStruct((s, ATTN_WIDTH), F32), jax.ShapeDtypeStruct((s, KV_WIDTH), F32),
                   jax.ShapeDtypeStruct((s, KV_WIDTH), F32), jax.ShapeDtypeStruct((1, N_Q_HEADS), F32)],
        scratch_shapes=[acc, acc, carry, carry],
        args=(q, kz, kz, vz, vz, dattn, lse, sinks))


def _bwd_inproj(dq, dk, dv, du, cos, sin, win_t, x, g_mix, dx2):
    s = x.shape[0]
    tm = _token_tile(s)

    def body(dq_ref, dk_ref, dv_ref, du_ref, cos_ref, sin_ref, w_ref, x_ref, g_ref, dx2_ref,
             dx_ref, dw_ref, db_ref, dg_ref):
        @pl.when(pl.program_id(0) == 0)
        def _():
            dw_ref[...] = jnp.zeros_like(dw_ref)
            db_ref[...] = jnp.zeros_like(db_ref)
            dg_ref[...] = jnp.zeros_like(dg_ref)

        cos_t, sin_t = cos_ref[...], sin_ref[...]
        dz32 = jnp.concatenate([_rope_bwd(dq_ref[...], cos_t, sin_t), _rope_bwd(dk_ref[...], cos_t, sin_t),
                                dv_ref[...], du_ref[...].astype(F32)], axis=1)
        db_ref[...] += jnp.sum(dz32, axis=0, keepdims=True)
        dz = dz32.astype(BF16)
        g = g_ref[...]
        n, r = _rms(x_ref[...])
        h = (n * g).astype(BF16)
        dh = _nn(dz, w_ref[...])
        for m0 in range(0, IN_WIDTH, TN_ROW_CHUNK):
            dw_ref[m0:m0 + TN_ROW_CHUNK, :] += _tn(dz[:, m0:m0 + TN_ROW_CHUNK], h)
        dx, dg = _rms_bwd(dh, n, r, g)
        dg_ref[...] += dg
        dx_ref[...] = dx2_ref[...] + dx

    return _gridded(
        body, None, name="bwd_inproj", grid=(s // tm,),
        in_specs=[_rows(tm, ATTN_WIDTH), _rows(tm, KV_WIDTH), _rows(tm, KV_WIDTH), _rows(tm, POOL_WIDTH),
                  _rows(tm, LANES), _rows(tm, LANES), _full((IN_WIDTH, D_MODEL)), _rows(tm, D_MODEL),
                  _full((1, D_MODEL)), _rows(tm, D_MODEL)],
        out_specs=[_rows(tm, D_MODEL), _full((IN_WIDTH, D_MODEL)), _full((1, IN_WIDTH)), _full((1, D_MODEL))],
        out_shape=[jax.ShapeDtypeStruct((s, D_MODEL), F32), jax.ShapeDtypeStruct((IN_WIDTH, D_MODEL), F32),
                   jax.ShapeDtypeStruct((1, IN_WIDTH), F32), jax.ShapeDtypeStruct((1, D_MODEL), F32)],
        scratch_shapes=[], args=(dq, dk, dv, du, cos, sin, win_t, x, g_mix, dx2))


def _rope_tables(s):
    inv_freq = 1.0 / (ROPE_THETA ** (jnp.arange(0, HEAD_DIM, 2, dtype=F32) / HEAD_DIM))
    ang = jnp.arange(s, dtype=F32)[:, None] * inv_freq[None, :]
    cos, sin = jnp.cos(ang), jnp.sin(ang)
    return jnp.tile(cos, (1, 4)), jnp.tile(jnp.concatenate([-sin, sin], axis=1), (1, 2))


def _place():
    return lax.axis_index("x"), lax.axis_index("y"), lax.axis_index("c")


def _other_chips(x, y):
    return [(1 - x, y), (x, 1 - y), (1 - x, 1 - y)]


def _gather_rider(blocks, relay_early=False):
    nm = len(blocks)

    def plan(ins, outs, sems):
        send_sems, recv_sems, local_sems = sems
        x, y, c = _place()
        me, sibling = (x, y, c), (x, y, 1 - c)
        chips = _other_chips(x, y)

        def rows(m, px, py, pc):
            r = ins[m].shape[0]
            return outs[m].at[pl.ds((4 * px + 2 * py + pc) * r, r), :]

        def copy(m, k, block, to, src=None):
            return pltpu.make_async_remote_copy(
                src_ref=rows(m, *block) if src is None else src, dst_ref=rows(m, *block),
                send_sem=send_sems.at[k * nm + m], recv_sem=recv_sems.at[k * nm + m],
                device_id=to, device_id_type=MESH)

        mine = [pltpu.make_async_copy(ins[m], rows(m, *me), local_sems.at[m]) for m in range(nm)]
        first = [copy(m, 0, me, sibling, src=ins[m]) for m in range(nm)]
        first += [copy(m, 1 + j, me, (*chip, c), src=ins[m]) for j, chip in enumerate(chips) for m in range(nm)]
        return me, sibling, chips, copy, mine, first

    def start(ins, outs, sems):
        *_, mine, first = plan(ins, outs, sems)
        for cp in mine + first:
            cp.start()

    def passed_on(ins, outs, sems):
        me, sibling, chips, copy, _, _ = plan(ins, outs, sems)
        return [copy(m, 4 + j, (*chip, me[2]), sibling) for j, chip in enumerate(chips) for m in range(nm)]

    def relay(ins, outs, sems):
        me, _, chips, copy, _, _ = plan(ins, outs, sems)
        forwards = passed_on(ins, outs, sems)
        for j, chip in enumerate(chips):
            for m in range(nm):
                copy(m, 1 + j, (*chip, me[2]), me).wait_recv()
                forwards[j * nm + m].start()

    def finish(ins, outs, sems):
        me, sibling, chips, copy, mine, first = plan(ins, outs, sems)
        for m in range(nm):
            copy(m, 0, sibling, me).wait_recv()
        for j, chip in enumerate(chips):
            for m in range(nm):
                copy(m, 4 + j, (*chip, 1 - me[2]), me).wait_recv()
        for cp in first + passed_on(ins, outs, sems):
            cp.wait_send()
        for cp in mine:
            cp.wait()

    return _Rider(
        arrays=list(blocks), out_shape=[jax.ShapeDtypeStruct((N_DEV * b.shape[0], b.shape[1]), b.dtype) for b in blocks],
        sems=[pltpu.SemaphoreType.DMA((7 * nm,)), pltpu.SemaphoreType.DMA((7 * nm,)), pltpu.SemaphoreType.DMA((nm,))],
        start=start, finish=finish, relay=relay, relay_early=relay_early)


def _exchange_rider(copies_of, arrays, out_shape, n_copies):
    def copies(ins, outs, sems):
        send_sems, recv_sems = sems
        return [pltpu.make_async_remote_copy(src_ref=src, dst_ref=dst, send_sem=send_sems.at[k], recv_sem=recv_sems.at[k],
                                             device_id=to, device_id_type=MESH)
                for k, (src, dst, to) in enumerate(copies_of(ins, outs))]

    def start(ins, outs, sems):
        for cp in copies(ins, outs, sems):
            cp.start()

    def finish(ins, outs, sems):
        cps = copies(ins, outs, sems)
        for cp in cps:
            cp.wait_recv()
        for cp in cps:
            cp.wait_send()

    return _Rider(arrays=list(arrays), out_shape=out_shape,
                  sems=[pltpu.SemaphoreType.DMA((n_copies,)), pltpu.SemaphoreType.DMA((n_copies,))], start=start, finish=finish)


def _sibling_rider(grads):
    def copies_of(ins, outs):
        x, y, c = _place()
        for g_ref, o_ref in zip(ins, outs):
            r = g_ref.shape[0] // N_DEV
            for q in range(N_CHIPS):
                yield g_ref.at[pl.ds((2 * q + 1 - c) * r, r), :], o_ref.at[pl.ds(q * r, r), :], (x, y, 1 - c)

    return _exchange_rider(copies_of, grads, [jax.ShapeDtypeStruct((g.shape[0] // 2, g.shape[1]), F32) for g in grads],
                           len(grads) * N_CHIPS)


def _chip_sum(grad, from_sibling, place):
    r = grad.shape[0] // N_DEV
    w = grad.shape[1]

    def body(place_ref, g_ref, s_ref, wire_ref, own_ref):
        total = g_ref[...] + s_ref[...]
        wire_ref[...] = total.astype(BF16)

        @pl.when(pl.program_id(0) == place_ref[1])
        def _():
            own_ref[...] = total

    grid_spec = pltpu.PrefetchScalarGridSpec(
        num_scalar_prefetch=1, grid=(N_CHIPS,),
        in_specs=[pl.BlockSpec((r, w), lambda q, p: (2 * q + p[0], 0)), pl.BlockSpec((r, w), lambda q, p: (q, 0))],
        out_specs=[pl.BlockSpec((r, w), lambda q, p: (q, 0)), pl.BlockSpec((r, w), lambda q, p: (0, 0))])
    return pl.pallas_call(
        body, name="grad_chip_sum", grid_spec=grid_spec,
        out_shape=[jax.ShapeDtypeStruct((N_CHIPS * r, w), BF16), jax.ShapeDtypeStruct((r, w), F32)],
        compiler_params=_params("arbitrary"),
    )(place, grad, from_sibling)


def _chips_rider(wires):
    def copies_of(ins, outs):
        x, y, c = _place()
        for w_ref, o_ref in zip(ins, outs):
            r = w_ref.shape[0] // N_CHIPS
            for j, (px, py) in enumerate(_other_chips(x, y)):
                yield w_ref.at[pl.ds((2 * px + py) * r, r), :], o_ref.at[pl.ds(j * r, r), :], (px, py, c)

    return _exchange_rider(copies_of, wires,
                           [jax.ShapeDtypeStruct((3 * (w.shape[0] // N_CHIPS), w.shape[1]), BF16) for w in wires], len(wires) * 3)


def _adamw_math(w, g, m, v):
    m = ADAM_B1 * m + (1.0 - ADAM_B1) * g
    v = ADAM_B2 * v + (1.0 - ADAM_B2) * jnp.square(g)
    m_hat = m / (1.0 - ADAM_B1 ** ADAM_STEP)
    v_hat = v / (1.0 - ADAM_B2 ** ADAM_STEP)
    delta = -ADAM_LR * (m_hat / (jnp.sqrt(v_hat) + ADAM_EPS) + ADAM_WD * w)
    return delta, m, v


def _reduce_adamw(own, received, w, m, v):
    r = own.shape[0]

    def body(own_ref, rec_ref, w_ref, m_ref, v_ref, g_ref, d_ref, nm_ref, nv_ref):
        g = ((own_ref[...] + rec_ref[0:r, :].astype(F32)) + rec_ref[r:2 * r, :].astype(F32)) + rec_ref[2 * r:, :].astype(F32)
        g_ref[...] = g
        d_ref[...], nm_ref[...], nv_ref[...] = _adamw_math(w_ref[...], g, m_ref[...], v_ref[...])

    shape = jax.ShapeDtypeStruct(own.shape, F32)
    return pl.pallas_call(
        body, name="reduce_adamw", in_specs=[VMEM] * 5, out_specs=[VMEM] * 4, out_shape=[shape] * 4,
        compiler_params=_params(),
    )(own, received, w, m, v)


SMALL_WIDE = (("w_pool", 65536),)
SMALL_EARLY = (("b_pool", 512), ("pool_scale", 512), ("b_out", 1024), ("g_ffn", 1024), ("g_final", 1024), ("loss", 1024))
SMALL_LATE = (("sinks", 8), ("g_mix", 1024), ("b_in", 1280))
SMALL = SMALL_WIDE + SMALL_EARLY + SMALL_LATE


def _small_rows(size):
    return -(-size // (8 * LANES)) * 8


def _pack_small(values, entries=SMALL):
    parts = []
    for name, size in entries:
        flat = values[name].reshape(-1).astype(F32)
        parts.append(jnp.pad(flat, (0, _small_rows(size) * LANES - size)).reshape(-1, LANES))
    return jnp.concatenate(parts, axis=0)


def _unpack_small(packed, shapes):
    out, row = {}, 0
    for name, size in SMALL:
        rows = _small_rows(size)
        if name in shapes:
            out[name] = packed[row:row + rows].reshape(-1)[:size].reshape(shapes[name])
        row += rows
    return out


def _small_sum_adamw(gathered, w, m, v):
    n = len(gathered)

    def body(*refs):
        w_ref, m_ref, v_ref, g_ref, d_ref, nm_ref, nv_ref = refs[n:]

        def total(ref):
            rows = ref.shape[0] // N_DEV
            acc = ref[0:rows, :].astype(F32)
            for dev in range(1, N_DEV):
                acc = acc + ref[dev * rows:(dev + 1) * rows, :].astype(F32)
            return acc

        g = jnp.concatenate([total(ref) for ref in refs[:n]], axis=0)
        g_ref[...] = g
        d_ref[...], nm_ref[...], nv_ref[...] = _adamw_math(w_ref[...], g, m_ref[...], v_ref[...])

    shape = jax.ShapeDtypeStruct(w.shape, F32)
    return pl.pallas_call(
        body, name="small_sum_adamw", in_specs=[VMEM] * (n + 3), out_specs=[VMEM] * 4, out_shape=[shape] * 4,
        compiler_params=_params(),
    )(*gathered, w, m, v)


def kernel(x, g_mix, w_in, b_in, sinks, w_pool, b_pool, pool_scale, w_out, b_out, g_ffn, w_gate, w_up, w_down, g_final, loss_target, m_g_mix, m_w_in, m_b_in, m_sinks, m_w_pool, m_b_pool, m_pool_scale, m_w_out, m_b_out, m_g_ffn, m_w_gate, m_w_up, m_w_down, m_g_final, v_g_mix, v_w_in, v_b_in, v_sinks, v_w_pool, v_b_pool, v_pool_scale, v_w_out, v_b_out, v_g_ffn, v_w_gate, v_w_up, v_w_down, v_g_final):
    weights = dict(g_mix=g_mix, w_in=w_in, b_in=b_in, sinks=sinks, w_pool=w_pool, b_pool=b_pool, pool_scale=pool_scale,
                   w_out=w_out, b_out=b_out, g_ffn=g_ffn, w_gate=w_gate, w_up=w_up, w_down=w_down, g_final=g_final)
    mom1 = dict(g_mix=m_g_mix, w_in=m_w_in, b_in=m_b_in, sinks=m_sinks, w_pool=m_w_pool, b_pool=m_b_pool,
                pool_scale=m_pool_scale, w_out=m_w_out, b_out=m_b_out, g_ffn=m_g_ffn, w_gate=m_w_gate, w_up=m_w_up,
                w_down=m_w_down, g_final=m_g_final)
    mom2 = dict(g_mix=v_g_mix, w_in=v_w_in, b_in=v_b_in, sinks=v_sinks, w_pool=v_w_pool, b_pool=v_b_pool,
                pool_scale=v_pool_scale, w_out=v_w_out, b_out=v_b_out, g_ffn=v_g_ffn, w_gate=v_w_gate, w_up=v_w_up,
                w_down=v_w_down, g_final=v_g_final)
    order = ("g_mix", "w_in", "b_in", "sinks", "w_pool", "b_pool", "pool_scale", "w_out", "b_out", "g_ffn",
             "w_gate", "w_up", "w_down", "g_final")
    big = ("w_in", "w_out", "w_gate", "w_up", "w_down")
    transposed = ("w_in", "w_gate", "w_up")

    def row_shard(name, a):
        return a[0].T if name in transposed else a[0]

    shard = {n: row_shard(n, weights[n]).astype(BF16) for n in big}
    xs, target = x[0], loss_target[0]
    cos, sin = _rope_tables(xs.shape[0])
    wp_b = w_pool[0].astype(BF16)
    bp = b_pool.reshape(1, POOL_WIDTH)
    ps = pool_scale.reshape(1, POOL_WIDTH)
    g_fin = g_final.reshape(1, D_MODEL)
    px, py, pc = _place()
    place = jnp.stack([pc, 2 * px + py]).astype(jnp.int32)

    (win_t,) = _alone(_gather_rider([shard["w_in"]]), "gather_w_in")
    q, kz, vz, vt, mixed, pool, w_out_b, wg_t = _fwd_inproj(
        xs, g_mix, win_t, b_in, cos, sin, wp_b, bp, ps,
        rider=_gather_rider([shard["w_out"], shard["w_gate"]], relay_early=True))
    attn, lse, wu_t = _attn_fwd(q, kz, vt, sinks, rider=_gather_rider([shard["w_up"]]))
    x2, gate, up, act, wd = _fwd_outproj_ffn_act(attn, pool, w_out_b, b_out, xs, g_ffn, wg_t, wu_t,
                                                 rider=_gather_rider([shard["w_down"]], relay_early=True))
    dx3, sq, dg_final, d_wd = _fwd_down_loss(act, x2, wd, g_fin, target)

    dx2, dg_ffn, db_out, d_wg_t, d_wu_t, wd_sibling = _bwd_ffn(
        dx3, gate, up, x2, wd, wg_t, wu_t, g_ffn, rider=_sibling_rider([d_wd]))
    wd_sum = _chip_sum(d_wd, wd_sibling, place)
    in_grads = [d_wg_t, d_wu_t]
    dattn, du, d_wout, d_wpool, d_bpool, d_pscale, wd_received, *in_sibling = _bwd_outproj_pool(
        dx2, attn, pool, mixed, w_out_b, wp_b, bp, ps, rider=_join(_chips_rider([wd_sum[0]]), _sibling_rider(in_grads)))
    in_sums = [_chip_sum(g, s, place) for g, s in zip(in_grads, in_sibling)]
    small_wide = _pack_small(dict(w_pool=d_wpool), SMALL_WIDE).astype(BF16)
    small_early = _pack_small(dict(b_pool=d_bpool, pool_scale=d_pscale, b_out=db_out, g_ffn=dg_ffn,
                                   g_final=dg_final, loss=sq), SMALL_EARLY)
    dq, dk, dv, d_sinks, *landed = _attn_bwd(
        q, kz, vz, dattn, lse, sinks,
        rider=_join(_chips_rider([wire for wire, _ in in_sums]), _sibling_rider([d_wout]),
                    _gather_rider([small_wide, small_early])))
    ffn_sums, ffn_received = in_sums + [wd_sum], landed[:2] + [wd_received]
    wout_sum = _chip_sum(d_wout, landed[2], place)
    gathered_wide, gathered_early = landed[3], landed[4]
    dx, d_win_t, d_bin, d_gmix = _bwd_inproj(dq, dk, dv, du, cos, sin, win_t, xs, g_mix, dx2)
    (win_sibling,) = _alone(_sibling_rider([d_win_t]), "grad_exchange_sibling")
    win_sum = _chip_sum(d_win_t, win_sibling, place)
    small_late = _pack_small(dict(sinks=d_sinks, g_mix=d_gmix, b_in=d_bin), SMALL_LATE)
    wout_received, win_received, gathered_late = _alone(
        _join(_chips_rider([wout_sum[0], win_sum[0]]), _gather_rider([small_late])), "grad_exchange_chips")

    reduced = dict(zip(("w_gate", "w_up", "w_down", "w_out", "w_in"),
                       zip(ffn_sums + [wout_sum, win_sum], list(ffn_received) + [wout_received, win_received])))
    grad, delta, new_m, new_v = {}, {}, {}, {}
    for n in big:
        (_, own), rec = reduced[n]
        results = _reduce_adamw(own, rec, row_shard(n, weights[n]), row_shard(n, mom1[n]), row_shard(n, mom2[n]))
        grad[n], delta[n], new_m[n], new_v[n] = [(a.T if n in transposed else a)[None] for a in results]

    shapes = {n: weights[n].shape for n in order if n not in big}
    zero_loss = jnp.zeros((1, D_MODEL), F32)
    packed = _small_sum_adamw(
        [gathered_wide, gathered_early, gathered_late], _pack_small({**weights, "loss": zero_loss}),
        _pack_small({**mom1, "loss": zero_loss}), _pack_small({**mom2, "loss": zero_loss}))
    for store, pk in zip((grad, delta, new_m, new_v), packed):
        store.update(_unpack_small(pk, shapes))
    loss_rows = _unpack_small(packed[0], {"loss": (D_MODEL,)})["loss"]
    loss = (0.5 / D_MODEL) * jnp.sum(loss_rows)

    return (loss, dx[None], *[grad[n] for n in order], *[delta[n] for n in order],
            *[new_m[n] for n in order], *[new_v[n] for n in order])
```

```python
from typing import Any, Callable, NamedTuple, Sequence

import jax
import jax.numpy as jnp
from jax import lax
from jax.experimental import pallas as pl
from jax.experimental.pallas import tpu as pltpu

D_MODEL = 1024
ATTN_WIDTH = 512
KV_WIDTH = 128
POOL_WIDTH = 512
HEAD_DIM = 64
N_Q_HEADS = 8
N_KV_HEADS = 2
GQA_GROUP = 4
BLOCK = 128
POOL_SIZES = (2, 4, 8, 16)
POOL_GROUP_WIDTH = 128
POOL_HALO = 16
IN_WIDTH = 1280
D_FF = 2816
RMS_EPS = 1e-5
ROPE_THETA = 10000.0
Q_SCALE = HEAD_DIM ** -0.5

ADAM_LR = 0.001
ADAM_B1 = 0.9
ADAM_B2 = 0.999
ADAM_EPS = 1e-08
ADAM_WD = 0.01
ADAM_STEP = 10

N_DEV = 8
N_CHIPS = 4
LANES = 128
VMEM_LIMIT_BYTES = 60 * 1024 * 1024

F32 = jnp.float32
BF16 = jnp.bfloat16
MESH = pl.DeviceIdType.MESH
HBM = pl.BlockSpec(memory_space=pltpu.HBM)
VMEM = pl.BlockSpec(memory_space=pltpu.VMEM)


def _params(*semantics):
    return pltpu.CompilerParams(dimension_semantics=semantics or None, vmem_limit_bytes=VMEM_LIMIT_BYTES)


def _nn(a, b):
    return jnp.dot(a, b, preferred_element_type=F32)


def _nt(a, b):
    return lax.dot_general(a, b, (((1,), (1,)), ((), ())), preferred_element_type=F32)


def _tn(a, b):
    return lax.dot_general(a, b, (((0,), (0,)), ((), ())), preferred_element_type=F32)


def _full(shape):
    return pl.BlockSpec(shape, lambda *_: (0,) * len(shape))


def _rows(tm, width):
    return pl.BlockSpec((tm, width), lambda i, *_: (i, 0))


def _nothing(ins, outs, sems):
    del ins, outs, sems


RELAY_STEPS_BEFORE_LAST = 2


class _Rider(NamedTuple):
    arrays: Sequence[Any]
    out_shape: Sequence[Any]
    sems: Sequence[Any]
    start: Callable[..., None]
    finish: Callable[..., None]
    relay: Callable[..., None] = _nothing
    relay_early: bool = False


def _gridded(body, rider, *, name, grid, in_specs, out_specs, out_shape, scratch_shapes, args):
    params = _params("arbitrary")
    if rider is None:
        return pl.pallas_call(body, name=name, grid=grid, in_specs=in_specs, out_specs=out_specs, out_shape=out_shape,
                              scratch_shapes=scratch_shapes, compiler_params=params)(*args)
    bounds, total = [], 0
    for n in (len(in_specs), len(rider.arrays), len(out_specs), len(rider.out_shape), len(scratch_shapes), len(rider.sems)):
        bounds.append((total, total + n))
        total += n
    last = grid[0] - 1
    relay_step = max(last - RELAY_STEPS_BEFORE_LAST, 0) if rider.relay_early else last

    def riding(*refs):
        ins, r_ins, outs, r_outs, scratch, r_sems = (refs[lo:hi] for lo, hi in bounds)

        @pl.when(pl.program_id(0) == 0)
        def _():
            rider.start(r_ins, r_outs, r_sems)

        body(*ins, *outs, *scratch)

        @pl.when(pl.program_id(0) == relay_step)
        def _():
            rider.relay(r_ins, r_outs, r_sems)

        @pl.when(pl.program_id(0) == last)
        def _():
            rider.finish(r_ins, r_outs, r_sems)

    return pl.pallas_call(
        riding, name=name, grid=grid, in_specs=list(in_specs) + [HBM] * len(rider.arrays),
        out_specs=list(out_specs) + [HBM] * len(rider.out_shape), out_shape=list(out_shape) + list(rider.out_shape),
        scratch_shapes=list(scratch_shapes) + list(rider.sems), compiler_params=params)(*args, *rider.arrays)


def _join(*riders):
    def phase(which):
        def run(ins, outs, sems):
            i = o = s = 0
            for r in riders:
                ni, no, ns = len(r.arrays), len(r.out_shape), len(r.sems)
                getattr(r, which)(ins[i:i + ni], outs[o:o + no], sems[s:s + ns])
                i, o, s = i + ni, o + no, s + ns
        return run

    return _Rider(arrays=[a for r in riders for a in r.arrays], out_shape=[a for r in riders for a in r.out_shape],
                  sems=[a for r in riders for a in r.sems], start=phase("start"), finish=phase("finish"), relay=phase("relay"),
                  relay_early=all(r.relay_early for r in riders if r.relay is not _nothing))


def _alone(rider, name):
    n_in, n_out = len(rider.arrays), len(rider.out_shape)

    def body(*refs):
        parts = refs[:n_in], refs[n_in:n_in + n_out], refs[n_in + n_out:]
        rider.start(*parts)
        rider.relay(*parts)
        rider.finish(*parts)

    return pl.pallas_call(body, name=name, in_specs=[HBM] * n_in, out_specs=[HBM] * n_out, out_shape=list(rider.out_shape),
                          scratch_shapes=list(rider.sems))(*rider.arrays)


def _rot_half(t):
    n = t.shape[1]
    lane = lax.broadcasted_iota(jnp.int32, t.shape, 1)
    return jnp.where((lane % HEAD_DIM) < HEAD_DIM // 2, pltpu.roll(t, n - HEAD_DIM // 2, 1), pltpu.roll(t, HEAD_DIM // 2, 1))


def _rope(t, cos, sin):
    reps = t.shape[1] // LANES
    if reps > 1:
        cos, sin = jnp.tile(cos, (1, reps)), jnp.tile(sin, (1, reps))
    return t * cos + _rot_half(t) * sin


def _rope_bwd(d, cos, sin):
    reps = d.shape[1] // LANES
    if reps > 1:
        cos, sin = jnp.tile(cos, (1, reps)), jnp.tile(sin, (1, reps))
    return d * cos + _rot_half(d * sin)


KV_SPREAD = 4 * LANES


def _spread_kv(t):
    low = lax.broadcasted_iota(jnp.int32, t.shape, 1) < HEAD_DIM
    swapped = pltpu.roll(t, HEAD_DIM, 1)
    zero = jnp.zeros_like(t)
    return jnp.concatenate([jnp.where(low, t, zero), jnp.where(low, zero, swapped),
                            jnp.where(low, swapped, zero), jnp.where(low, zero, t)], axis=1)


def _rms(x):
    r = lax.rsqrt(jnp.mean(x * x, axis=-1, keepdims=True) + RMS_EPS)
    return x * r, r


def _rms_bwd(dh, n, r, g):
    dn = dh * g
    dx = r * (dn - n * jnp.mean(dn * n, axis=-1, keepdims=True))
    return dx, jnp.sum(dh * n, axis=0, keepdims=True)


def _token_tile(s):
    return min(512, s)


def _window_mean(window_sum, pos, size):
    head = window_sum[:POOL_HALO, :] / jnp.minimum(pos[:POOL_HALO, :] + 1, size).astype(F32)
    return jnp.concatenate([head, window_sum[POOL_HALO:, :] * (1.0 / size)], axis=0)


def _fwd_inproj(x, g_mix, win_t, b_in, cos, sin, w_pool, b_pool, pool_scale, rider=None):
    s = x.shape[0]
    tm = _token_tile(s)

    def body(x_ref, g_ref, w_ref, b_ref, cos_ref, sin_ref, wp_ref, bp_ref, ps_ref,
             q_ref, k_ref, v_ref, vt_ref, mix_ref, pool_ref, tail_ref):
        i = pl.program_id(0)

        @pl.when(i == 0)
        def _():
            tail_ref[...] = jnp.zeros_like(tail_ref)

        n, _ = _rms(x_ref[...])
        h = (n * g_ref[...]).astype(BF16)
        z = _nt(h, w_ref[...]) + b_ref[...]
        cos_t, sin_t = cos_ref[...], sin_ref[...]
        q_ref[...] = (_rope(z[:, :ATTN_WIDTH], cos_t, sin_t) * Q_SCALE).astype(BF16)
        k_ref[...] = _spread_kv(_rope(z[:, ATTN_WIDTH:ATTN_WIDTH + KV_WIDTH], cos_t, sin_t)).astype(BF16)
        vz = _spread_kv(z[:, ATTN_WIDTH + KV_WIDTH:ATTN_WIDTH + 2 * KV_WIDTH])
        v_ref[...] = vz.astype(BF16)
        vt_ref[...] = vz.T.astype(BF16)
        u = z[:, ATTN_WIDTH + 2 * KV_WIDTH:]
        u_ext = jnp.concatenate([tail_ref[...], u], axis=0)
        tail_ref[...] = u[tm - POOL_HALO:, :]
        pos = lax.broadcasted_iota(jnp.int32, (tm, POOL_GROUP_WIDTH), 0) + i * tm
        for g, size in enumerate(POOL_SIZES):
            cols = slice(g * POOL_GROUP_WIDTH, (g + 1) * POOL_GROUP_WIDTH)
            a = u_ext[:, cols]
            shift = 1
            while shift < size:
                a = a + pltpu.roll(a, shift, 0)
                shift *= 2
            mixed = (_window_mean(a[POOL_HALO:, :], pos, size) - u[:, cols]).astype(BF16)
            pre = _nn(mixed, wp_ref[g]) + bp_ref[:, cols]
            mix_ref[:, cols] = mixed
            pool_ref[:, cols] = (pre * ps_ref[:, cols]).astype(BF16)

    bf = lambda w: jax.ShapeDtypeStruct((s, w), BF16)
    return _gridded(
        body, rider, name="fwd_inproj", grid=(s // tm,),
        in_specs=[_rows(tm, D_MODEL), _full((1, D_MODEL)), _full((IN_WIDTH, D_MODEL)), _full((1, IN_WIDTH)),
                  _rows(tm, LANES), _rows(tm, LANES), _full((4, POOL_GROUP_WIDTH, POOL_GROUP_WIDTH)),
                  _full((1, POOL_WIDTH)), _full((1, POOL_WIDTH))],
        out_specs=[_rows(tm, ATTN_WIDTH), _rows(tm, KV_SPREAD), _rows(tm, KV_SPREAD),
                   pl.BlockSpec((KV_SPREAD, tm), lambda i: (0, i)), _rows(tm, POOL_WIDTH), _rows(tm, POOL_WIDTH)],
        out_shape=[bf(ATTN_WIDTH), bf(KV_SPREAD), bf(KV_SPREAD), jax.ShapeDtypeStruct((KV_SPREAD, s), BF16),
                   bf(POOL_WIDTH), bf(POOL_WIDTH)],
        scratch_shapes=[pltpu.VMEM((POOL_HALO, POOL_WIDTH), F32)],
        args=(x, g_mix, win_t, b_in, cos, sin, w_pool, b_pool, pool_scale))


ATTN_TILE = 1024
PAIR = 2 * LANES


def _band_masks(tile):
    j = lax.broadcasted_iota(jnp.int32, (4 * BLOCK, 2 * BLOCK), 0) % (2 * BLOCK)
    r = lax.broadcasted_iota(jnp.int32, (4 * BLOCK, 2 * BLOCK), 1) % BLOCK
    band = (j > r) & (j <= r + BLOCK)
    return band & ((tile > 0) | (j >= BLOCK)), band


def _band(cur_ref, prev_ref, b, kv):
    halves = []
    for half in range(2):
        cols = slice(kv * PAIR + half * LANES, kv * PAIR + (half + 1) * LANES)
        if b == 0:
            halves.append(jnp.concatenate([prev_ref[:, cols], cur_ref[0:BLOCK, cols]], axis=0))
        else:
            halves.append(cur_ref[(b - 1) * BLOCK:(b + 1) * BLOCK, cols])
    return jnp.concatenate(halves, axis=0)


def _stack_pair(ref, rows, kv):
    return jnp.concatenate([ref[rows, kv * PAIR:kv * PAIR + LANES], ref[rows, kv * PAIR + LANES:(kv + 1) * PAIR]], axis=0)


def _pair_heads(kv, half):
    return GQA_GROUP * kv + half, GQA_GROUP * kv + 2 + half


def _band_t(cur_ref, prev_ref, b, kv):
    halves = []
    for half in range(2):
        lanes = slice(kv * PAIR + half * LANES, kv * PAIR + (half + 1) * LANES)
        if b == 0:
            halves.append(jnp.concatenate([prev_ref[lanes, :], cur_ref[lanes, 0:BLOCK]], axis=1))
        else:
            halves.append(cur_ref[lanes, (b - 1) * BLOCK:(b + 1) * BLOCK])
    return jnp.concatenate(halves, axis=1)


def _reduce_rows(x, op, reduce):
    while x.shape[0] > 8:
        half = x.shape[0] // 2
        x = op(x[:half], x[half:])
    return reduce(x, axis=0, keepdims=True)


def _per_query(ref, rows, top, bottom):
    return jnp.concatenate([ref[top:top + 1, rows], ref[bottom:bottom + 1, rows]], axis=1)


def _sink_per_query(sink_ref, top, bottom):
    first_slab = lax.broadcasted_iota(jnp.int32, (1, 2 * BLOCK), 1) < BLOCK
    return jnp.where(first_slab, sink_ref[:, top:top + 1], sink_ref[:, bottom:bottom + 1])


def _attn_fwd(q, kz, vt, sinks, rider=None):
    s = q.shape[0]
    tq = min(ATTN_TILE, s)

    def body(q_ref, k_ref, kp_ref, vt_ref, vtp_ref, sink_ref, o_ref, lse_ref):
        first, band = _band_masks(pl.program_id(0))
        chains = [(b, kv) for b in range(tq // BLOCK) for kv in range(N_KV_HEADS)]

        def scores(b, kv):
            rows = slice(b * BLOCK, (b + 1) * BLOCK)
            return _nt(_band(k_ref, kp_ref, b, kv), _stack_pair(q_ref, rows, kv))

        def store(b, kv, ot):
            rows = slice(b * BLOCK, (b + 1) * BLOCK)
            o = ot.T.astype(BF16)
            o_ref[rows, kv * PAIR:kv * PAIR + LANES] = o[:BLOCK]
            o_ref[rows, kv * PAIR + LANES:(kv + 1) * PAIR] = o[BLOCK:]

        ahead = scores(*chains[0])
        behind = None
        for n, (b, kv) in enumerate(chains):
            rows = slice(b * BLOCK, (b + 1) * BLOCK)
            st = jnp.where(first if b == 0 else band, ahead, -jnp.inf)
            if n + 1 < len(chains):
                ahead = scores(*chains[n + 1])
            probs = []
            for half in range(2):
                top, bottom = _pair_heads(kv, half)
                sink = _sink_per_query(sink_ref, top, bottom)
                sh = st[half * 2 * BLOCK:(half + 1) * 2 * BLOCK, :]
                m = jnp.maximum(_reduce_rows(sh, jnp.maximum, jnp.max), sink)
                p = jnp.exp(sh - m)
                denom = _reduce_rows(p, jnp.add, jnp.sum) + jnp.exp(sink - m)
                probs.append((p * (1.0 / denom)).astype(BF16))
                lse = m + jnp.log(denom)
                lse_ref[top:top + 1, rows] = lse[:, :BLOCK]
                lse_ref[bottom:bottom + 1, rows] = lse[:, BLOCK:]
            ot = _nn(_band_t(vt_ref, vtp_ref, b, kv), jnp.concatenate(probs, axis=0))
            if behind is not None:
                store(*behind)
            behind = (b, kv, ot)
        store(*behind)

    per = tq // BLOCK
    cur = lambda w: pl.BlockSpec((tq, w), lambda i: (i, 0))
    prev = pl.BlockSpec((BLOCK, KV_SPREAD), lambda i: (jnp.maximum(per * i - 1, 0), 0))
    cur_t = pl.BlockSpec((KV_SPREAD, tq), lambda i: (0, i))
    prev_t = pl.BlockSpec((KV_SPREAD, BLOCK), lambda i: (0, jnp.maximum(per * i - 1, 0)))
    return _gridded(
        body, rider, name="attn_fwd", grid=(s // tq,),
        in_specs=[cur(ATTN_WIDTH), cur(KV_SPREAD), prev, cur_t, prev_t, _full((1, N_Q_HEADS))],
        out_specs=[cur(ATTN_WIDTH), pl.BlockSpec((N_Q_HEADS, tq), lambda i: (0, i))],
        out_shape=[jax.ShapeDtypeStruct((s, ATTN_WIDTH), BF16), jax.ShapeDtypeStruct((N_Q_HEADS, s), F32)],
        scratch_shapes=[], args=(q, kz, kz, vt, vt, sinks))


FF_CHUNK = 256
TN_ROW_CHUNK = 256


def _resident(shape):
    return pl.BlockSpec(shape, lambda *_: (0,) * len(shape), pipeline_mode=pl.Buffered(1))


def _flush_rows(acc_ref, out_ref, sem, rows, is_last):
    @pl.when(is_last)
    def _():
        pltpu.make_async_copy(acc_ref.at[rows, :], out_ref.at[rows, :], sem).start()


def _flush_wait(acc_ref, out_ref, sem, is_last):
    @pl.when(is_last)
    def _():
        pltpu.make_async_copy(acc_ref, out_ref, sem).wait()


def _accumulate_tn(acc_ref, a_ref, b, out_ref, sem, is_last):
    for m0 in range(0, acc_ref.shape[0], TN_ROW_CHUNK):
        rows = slice(m0, m0 + TN_ROW_CHUNK)
        acc_ref[rows, :] += _tn(a_ref[:, rows], b)
        _flush_rows(acc_ref, out_ref, sem, rows, is_last)
    _flush_wait(acc_ref, out_ref, sem, is_last)


def _fwd_outproj_ffn_act(attn, pool, w_out, b_out, x, g_ffn, wg_t, wu_t, rider=None):
    s = x.shape[0]
    tm = _token_tile(s)

    def body(a_ref, p_ref, w_ref, b_ref, x_ref, g_ref, wg_ref, wu_ref, x2_ref, gate_ref, up_ref, act_ref):
        x2 = x_ref[...] + _nn(a_ref[...], w_ref[:ATTN_WIDTH, :]) + _nn(p_ref[...], w_ref[ATTN_WIDTH:, :]) + b_ref[...]
        x2_ref[...] = x2
        n, _ = _rms(x2)
        h = (n * g_ref[...]).astype(BF16)

        def products(c0):
            return _nt(h, wg_ref[c0:c0 + FF_CHUNK, :]), _nt(h, wu_ref[c0:c0 + FF_CHUNK, :])

        ahead = products(0)
        for c0 in range(0, D_FF, FF_CHUNK):
            cols = slice(c0, c0 + FF_CHUNK)
            gate, up = ahead
            if c0 + FF_CHUNK < D_FF:
                ahead = products(c0 + FF_CHUNK)
            gate_ref[:, cols] = gate.astype(BF16)
            up_ref[:, cols] = up.astype(BF16)
            act_ref[:, cols] = (gate * jax.nn.sigmoid(gate) * up).astype(BF16)

    act_shape = jax.ShapeDtypeStruct((s, D_FF), BF16)
    return _gridded(
        body, rider, name="fwd_outproj_ffn_act", grid=(s // tm,),
        in_specs=[_rows(tm, ATTN_WIDTH), _rows(tm, POOL_WIDTH), _resident((D_MODEL, D_MODEL)), _full((1, D_MODEL)),
                  _rows(tm, D_MODEL), _full((1, D_MODEL)), _resident((D_FF, D_MODEL)), _resident((D_FF, D_MODEL))],
        out_specs=[_rows(tm, D_MODEL)] + [_rows(tm, D_FF)] * 3,
        out_shape=[jax.ShapeDtypeStruct((s, D_MODEL), F32)] + [act_shape] * 3,
        scratch_shapes=[], args=(attn, pool, w_out, b_out, x, g_ffn, wg_t, wu_t))


def _fwd_down_loss(act, x2, wd, g_final, target):
    s = x2.shape[0]
    tm = _token_tile(s)
    last = s // tm - 1

    def body(a_ref, x2_ref, wd_ref, g_ref, t_ref, dx3_ref, sq_ref, dg_ref, dwd_ref, acc_ref, sem):
        @pl.when(pl.program_id(0) == 0)
        def _():
            sq_ref[...] = jnp.zeros_like(sq_ref)
            dg_ref[...] = jnp.zeros_like(dg_ref)
            acc_ref[...] = jnp.zeros_like(acc_ref)

        x3 = x2_ref[...] + _nn(a_ref[...], wd_ref[...])
        n, r = _rms(x3)
        g = g_ref[...]
        diff = n * g - t_ref[...]
        sq_ref[...] += jnp.sum(diff * diff, axis=0, keepdims=True)
        dx3, dg = _rms_bwd(diff * (1.0 / D_MODEL), n, r, g)
        dg_ref[...] += dg
        dx3_ref[...] = dx3
        _accumulate_tn(acc_ref, a_ref, dx3.astype(BF16), dwd_ref, sem, pl.program_id(0) == last)

    return pl.pallas_call(
        body, name="fwd_down_loss", grid=(s // tm,),
        in_specs=[_rows(tm, D_FF), _rows(tm, D_MODEL), _resident((D_FF, D_MODEL)), _full((1, D_MODEL)), _rows(tm, D_MODEL)],
        out_specs=[_rows(tm, D_MODEL), _full((1, D_MODEL)), _full((1, D_MODEL)), HBM],
        out_shape=[jax.ShapeDtypeStruct((s, D_MODEL), F32),
                   jax.ShapeDtypeStruct((1, D_MODEL), F32), jax.ShapeDtypeStruct((1, D_MODEL), F32),
                   jax.ShapeDtypeStruct((D_FF, D_MODEL), F32)],
        scratch_shapes=[pltpu.VMEM((D_FF, D_MODEL), F32), pltpu.SemaphoreType.DMA],
        compiler_params=_params("arbitrary"),
    )(act, x2, wd, g_final, target)


FFN_BWD_TILE = 256


def _bwd_ffn(dx3, gate, up, x2, wd, wg_t, wu_t, g_ffn, rider=None):
    s = x2.shape[0]
    tm = min(FFN_BWD_TILE, s)
    last = s // tm - 1

    def body(dx3_ref, gate_ref, up_ref, x2_ref, wd_ref, wg_ref, wu_ref, g_ref,
             dx2_ref, dg_ref, db_ref, dwg_ref, dwu_ref, dgate_ref, dup_ref, accg_ref, accu_ref, sems):
        @pl.when(pl.program_id(0) == 0)
        def _():
            dg_ref[...] = jnp.zeros_like(dg_ref)
            db_ref[...] = jnp.zeros_like(db_ref)
            accg_ref[...] = jnp.zeros_like(accg_ref)
            accu_ref[...] = jnp.zeros_like(accu_ref)

        dx3b = dx3_ref[...].astype(BF16)
        g = g_ref[...]
        n, r = _rms(x2_ref[...])
        h = (n * g).astype(BF16)
        ahead = _nt(dx3b, wd_ref[0:FF_CHUNK, :])
        for c0 in range(0, D_FF, FF_CHUNK):
            cols = slice(c0, c0 + FF_CHUNK)
            dact = ahead
            if c0 + FF_CHUNK < D_FF:
                ahead = _nt(dx3b, wd_ref[c0 + FF_CHUNK:c0 + 2 * FF_CHUNK, :])
            gate = gate_ref[:, cols].astype(F32)
            up = up_ref[:, cols].astype(F32)
            sig = jax.nn.sigmoid(gate)
            silu = gate * sig
            dup = (dact * silu).astype(BF16)
            dgate = (dact * up * (sig + silu * (1.0 - sig))).astype(BF16)
            dup_ref[:, cols] = dup
            dgate_ref[:, cols] = dgate
            accg_ref[cols, :] += _tn(dgate, h)
            accu_ref[cols, :] += _tn(dup, h)
        dh2 = _nn(dgate_ref[...], wg_ref[...]) + _nn(dup_ref[...], wu_ref[...])
        dx, dg = _rms_bwd(dh2, n, r, g)
        dx2 = dx3_ref[...] + dx
        dg_ref[...] += dg
        db_ref[...] += jnp.sum(dx2, axis=0, keepdims=True)
        dx2_ref[...] = dx2

        @pl.when(pl.program_id(0) == last)
        def _():
            outs = [pltpu.make_async_copy(accg_ref, dwg_ref, sems.at[0]), pltpu.make_async_copy(accu_ref, dwu_ref, sems.at[1])]
            for cp in outs:
                cp.start()
            for cp in outs:
                cp.wait()

    grad_shape = jax.ShapeDtypeStruct((D_FF, D_MODEL), F32)
    weight = _resident((D_FF, D_MODEL))
    return _gridded(
        body, rider, name="bwd_ffn", grid=(s // tm,),
        in_specs=[_rows(tm, D_MODEL), _rows(tm, D_FF), _rows(tm, D_FF),
                  _rows(tm, D_MODEL), weight, weight, weight, _full((1, D_MODEL))],
        out_specs=[_rows(tm, D_MODEL), _full((1, D_MODEL)), _full((1, D_MODEL)), HBM, HBM],
        out_shape=[jax.ShapeDtypeStruct((s, D_MODEL), F32),
                   jax.ShapeDtypeStruct((1, D_MODEL), F32), jax.ShapeDtypeStruct((1, D_MODEL), F32), grad_shape, grad_shape],
        scratch_shapes=[pltpu.VMEM((tm, D_FF), BF16), pltpu.VMEM((tm, D_FF), BF16),
                        pltpu.VMEM((D_FF, D_MODEL), F32), pltpu.VMEM((D_FF, D_MODEL), F32), pltpu.SemaphoreType.DMA((2,))],
        args=(dx3, gate, up, x2, wd, wg_t, wu_t, g_ffn))


def _bwd_outproj_pool(dx2, attn, pool, mixed, w_out, w_pool, b_pool, pool_scale, rider=None):
    s = dx2.shape[0]
    tm = min(2 * _token_tile(s), s)
    nt = s // tm

    def body(dx_ref, a_ref, p_ref, mix_ref, w_ref, wp_ref, bp_ref, ps_ref,
             dattn_ref, du_ref, dwout_ref, dwp_ref, dbp_ref, dps_ref, head_ref):
        step = pl.program_id(0)
        tile = nt - 1 - step

        @pl.when(step == 0)
        def _():
            head_ref[...] = jnp.zeros_like(head_ref)
            dwout_ref[...] = jnp.zeros_like(dwout_ref)
            dwp_ref[...] = jnp.zeros_like(dwp_ref)
            dbp_ref[...] = jnp.zeros_like(dbp_ref)
            dps_ref[...] = jnp.zeros_like(dps_ref)

        dx = dx_ref[...].astype(BF16)
        dwout_ref[:ATTN_WIDTH, :] += _tn(a_ref[...], dx)
        dwout_ref[ATTN_WIDTH:, :] += _tn(p_ref[...], dx)
        dcat = _nt(dx, w_ref[...])
        dattn_ref[...] = dcat[:, :ATTN_WIDTH].astype(BF16)
        dpool = dcat[:, ATTN_WIDTH:]
        pos = lax.broadcasted_iota(jnp.int32, (tm, POOL_GROUP_WIDTH), 0) + tile * tm
        head = head_ref[...]
        n_ext = tm + POOL_HALO
        for g, size in enumerate(POOL_SIZES):
            cols = slice(g * POOL_GROUP_WIDTH, (g + 1) * POOL_GROUP_WIDTH)
            mixed_g = mix_ref[:, cols]
            pre = _nn(mixed_g, wp_ref[g]) + bp_ref[:, cols]
            dy = dpool[:, cols]
            dps_ref[:, cols] += jnp.sum(dy * pre, axis=0, keepdims=True)
            dpre = dy * ps_ref[:, cols]
            dbp_ref[:, cols] += jnp.sum(dpre, axis=0, keepdims=True)
            dpre_b = dpre.astype(BF16)
            dwp_ref[g] += _tn(mixed_g, dpre_b)
            dmixed = _nt(dpre_b, wp_ref[g])
            w = _window_mean(dmixed, pos, size)
            head_ref[:, cols] = w[:POOL_HALO, :]
            a = jnp.concatenate([w, head[:, cols]], axis=0)
            shift = 1
            while shift < size:
                a = a + pltpu.roll(a, n_ext - shift, 0)
                shift *= 2
            du_ref[:, cols] = (a[:tm, :] - dmixed).astype(BF16)

    rev = lambda w: pl.BlockSpec((tm, w), lambda i: (nt - 1 - i, 0))
    return _gridded(
        body, rider, name="bwd_outproj_pool", grid=(nt,),
        in_specs=[rev(D_MODEL), rev(ATTN_WIDTH), rev(POOL_WIDTH), rev(POOL_WIDTH), _full((D_MODEL, D_MODEL)),
                  _full((4, POOL_GROUP_WIDTH, POOL_GROUP_WIDTH)), _full((1, POOL_WIDTH)), _full((1, POOL_WIDTH))],
        out_specs=[rev(ATTN_WIDTH), rev(POOL_WIDTH), _full((D_MODEL, D_MODEL)),
                   _full((4, POOL_GROUP_WIDTH, POOL_GROUP_WIDTH)), _full((1, POOL_WIDTH)), _full((1, POOL_WIDTH))],
        out_shape=[jax.ShapeDtypeStruct((s, ATTN_WIDTH), BF16), jax.ShapeDtypeStruct((s, POOL_WIDTH), BF16),
                   jax.ShapeDtypeStruct((D_MODEL, D_MODEL), F32),
                   jax.ShapeDtypeStruct((4, POOL_GROUP_WIDTH, POOL_GROUP_WIDTH), F32),
                   jax.ShapeDtypeStruct((1, POOL_WIDTH), F32), jax.ShapeDtypeStruct((1, POOL_WIDTH), F32)],
        scratch_shapes=[pltpu.VMEM((POOL_HALO, POOL_WIDTH), F32)],
        args=(dx2, attn, pool, mixed, w_out, w_pool, b_pool, pool_scale))


def _fold_spread(t):
    low = lax.broadcasted_iota(jnp.int32, (2 * BLOCK, LANES), 1) < HEAD_DIM
    kept = jnp.where(low, t[:2 * BLOCK, :], t[2 * BLOCK:, :])
    return kept + pltpu.roll(kept, HEAD_DIM, 1)


def _attn_bwd(q, kz, vz, dattn, lse, sinks, rider=None):
    s = q.shape[0]
    tq = min(ATTN_TILE, s)
    nt = s // tq
    per = tq // BLOCK

    def body(q_ref, k_ref, kp_ref, v_ref, vp_ref, do_ref, lse_ref, sink_ref,
             dq_ref, dk_ref, dv_ref, dsink_ref, dk_acc, dv_acc, dk_carry, dv_carry):
        step = pl.program_id(0)

        @pl.when(step == 0)
        def _():
            dk_carry[...] = jnp.zeros_like(dk_carry)
            dv_carry[...] = jnp.zeros_like(dv_carry)
            dsink_ref[...] = jnp.zeros_like(dsink_ref)

        dk_acc[0:tq, :] = jnp.zeros((tq, KV_WIDTH), F32)
        dv_acc[0:tq, :] = jnp.zeros((tq, KV_WIDTH), F32)
        dk_acc[tq:, :] = dk_carry[...]
        dv_acc[tq:, :] = dv_carry[...]
        first, band = _band_masks(nt - 1 - step)
        low = lax.broadcasted_iota(jnp.int32, (2 * BLOCK, LANES), 1) < HEAD_DIM
        chains = [(b, kv) for b in range(per) for kv in range(N_KV_HEADS)]

        def operands(b, kv):
            rows = slice(b * BLOCK, (b + 1) * BLOCK)
            qab = _stack_pair(q_ref, rows, kv)
            doab = _stack_pair(do_ref, rows, kv)
            kzb = _band(k_ref, kp_ref, b, kv)
            return qab, doab, kzb, _nt(kzb, qab), _nt(_band(v_ref, vp_ref, b, kv), doab)

        folded = {}

        def finish(b, kv, dqab, dkz, dvz):
            rows = slice(b * BLOCK, (b + 1) * BLOCK)
            dq_ref[rows, kv * PAIR:kv * PAIR + LANES] = dqab[:BLOCK] * Q_SCALE
            dq_ref[rows, kv * PAIR + LANES:(kv + 1) * PAIR] = dqab[BLOCK:] * Q_SCALE
            folded[kv] = (_fold_spread(dkz), _fold_spread(dvz))
            if kv == N_KV_HEADS - 1:
                band_rows = slice(b * BLOCK, (b + 2) * BLOCK)
                dk_acc[band_rows, :] += jnp.where(low, folded[0][0], folded[1][0])
                dv_acc[band_rows, :] += jnp.where(low, folded[0][1], folded[1][1])

        ahead = operands(*chains[0])
        behind = None
        for n, (b, kv) in enumerate(chains):
            rows = slice(b * BLOCK, (b + 1) * BLOCK)
            mask = first if b == 0 else band
            qab, doab, kzb, st, dpt = ahead
            if n + 1 < len(chains):
                ahead = operands(*chains[n + 1])
            probs, dscores = [], []
            for half in range(2):
                top, bottom = _pair_heads(kv, half)
                keys = slice(half * 2 * BLOCK, (half + 1) * 2 * BLOCK)
                lse_h = _per_query(lse_ref, rows, top, bottom)
                p = jnp.where(mask[keys, :], jnp.exp(st[keys, :] - lse_h), 0.0)
                dph = dpt[keys, :]
                delta = _reduce_rows(p * dph, jnp.add, jnp.sum)
                probs.append(p.astype(BF16))
                dscores.append((p * (dph - delta)).astype(BF16))
                leak = jnp.exp(_sink_per_query(sink_ref, top, bottom) - lse_h) * delta
                dsink_ref[:, top:top + 1] -= jnp.sum(leak[:, :BLOCK], axis=1, keepdims=True)
                dsink_ref[:, bottom:bottom + 1] -= jnp.sum(leak[:, BLOCK:], axis=1, keepdims=True)
            ds = jnp.concatenate(dscores, axis=0)
            results = (_tn(ds, kzb), _nn(ds, qab), _nn(jnp.concatenate(probs, axis=0), doab))
            if behind is not None:
                finish(*behind)
            behind = (b, kv, *results)
        finish(*behind)
        dk_ref[...] = dk_acc[BLOCK:, :]
        dv_ref[...] = dv_acc[BLOCK:, :]
        dk_carry[...] = dk_acc[0:BLOCK, :]
        dv_carry[...] = dv_acc[0:BLOCK, :]

    cur = lambda w: pl.BlockSpec((tq, w), lambda i: (nt - 1 - i, 0))
    prev = pl.BlockSpec((BLOCK, KV_SPREAD), lambda i: (jnp.maximum(per * (nt - 1 - i) - 1, 0), 0))
    acc = pltpu.VMEM((tq + BLOCK, KV_WIDTH), F32)
    carry = pltpu.VMEM((BLOCK, KV_WIDTH), F32)
    return _gridded(
        body, rider, name="attn_bwd", grid=(nt,),
        in_specs=[cur(ATTN_WIDTH), cur(KV_SPREAD), prev, cur(KV_SPREAD), prev, cur(ATTN_WIDTH),
                  pl.BlockSpec((N_Q_HEADS, tq), lambda i: (0, nt - 1 - i)), _full((1, N_Q_HEADS))],
        out_specs=[cur(ATTN_WIDTH), cur(KV_WIDTH), cur(KV_WIDTH), _full((1, N_Q_HEADS))],
        out_shape=[jax.ShapeDtypeStruct((s, ATTN_WIDTH), F32), jax.ShapeDtypeStruct((s, KV_WIDTH), F32),
                   jax.ShapeDtypeStruct((s, KV_WIDTH), F32), jax.ShapeDtypeStruct((1, N_Q_HEADS), F32)],
        scratch_shapes=[acc, acc, carry, carry],
        args=(q, kz, kz, vz, vz, dattn, lse, sinks))


def _bwd_inproj(dq, dk, dv, du, cos, sin, win_t, x, g_mix, dx2):
    s = x.shape[0]
    tm = _token_tile(s)

    def body(dq_ref, dk_ref, dv_ref, du_ref, cos_ref, sin_ref, w_ref, x_ref, g_ref, dx2_ref,
             dx_ref, dw_ref, db_ref, dg_ref):
        @pl.when(pl.program_id(0) == 0)
        def _():
            dw_ref[...] = jnp.zeros_like(dw_ref)
            db_ref[...] = jnp.zeros_like(db_ref)
            dg_ref[...] = jnp.zeros_like(dg_ref)

        cos_t, sin_t = cos_ref[...], sin_ref[...]
        dz32 = jnp.concatenate([_rope_bwd(dq_ref[...], cos_t, sin_t), _rope_bwd(dk_ref[...], cos_t, sin_t),
                                dv_ref[...], du_ref[...].astype(F32)], axis=1)
        db_ref[...] += jnp.sum(dz32, axis=0, keepdims=True)
        dz = dz32.astype(BF16)
        g = g_ref[...]
        n, r = _rms(x_ref[...])
        h = (n * g).astype(BF16)
        dh = _nn(dz, w_ref[...])
        for m0 in range(0, IN_WIDTH, TN_ROW_CHUNK):
            dw_ref[m0:m0 + TN_ROW_CHUNK, :] += _tn(dz[:, m0:m0 + TN_ROW_CHUNK], h)
        dx, dg = _rms_bwd(dh, n, r, g)
        dg_ref[...] += dg
        dx_ref[...] = dx2_ref[...] + dx

    return _gridded(
        body, None, name="bwd_inproj", grid=(s // tm,),
        in_specs=[_rows(tm, ATTN_WIDTH), _rows(tm, KV_WIDTH), _rows(tm, KV_WIDTH), _rows(tm, POOL_WIDTH),
                  _rows(tm, LANES), _rows(tm, LANES), _full((IN_WIDTH, D_MODEL)), _rows(tm, D_MODEL),
                  _full((1, D_MODEL)), _rows(tm, D_MODEL)],
        out_specs=[_rows(tm, D_MODEL), _full((IN_WIDTH, D_MODEL)), _full((1, IN_WIDTH)), _full((1, D_MODEL))],
        out_shape=[jax.ShapeDtypeStruct((s, D_MODEL), F32), jax.ShapeDtypeStruct((IN_WIDTH, D_MODEL), F32),
                   jax.ShapeDtypeStruct((1, IN_WIDTH), F32), jax.ShapeDtypeStruct((1, D_MODEL), F32)],
        scratch_shapes=[], args=(dq, dk, dv, du, cos, sin, win_t, x, g_mix, dx2))


def _rope_tables(s):
    inv_freq = 1.0 / (ROPE_THETA ** (jnp.arange(0, HEAD_DIM, 2, dtype=F32) / HEAD_DIM))
    ang = jnp.arange(s, dtype=F32)[:, None] * inv_freq[None, :]
    cos, sin = jnp.cos(ang), jnp.sin(ang)
    return jnp.tile(cos, (1, 4)), jnp.tile(jnp.concatenate([-sin, sin], axis=1), (1, 2))


def _place():
    return lax.axis_index("x"), lax.axis_index("y"), lax.axis_index("c")


def _other_chips(x, y):
    return [(1 - x, y), (x, 1 - y), (1 - x, 1 - y)]


def _gather_rider(blocks, relay_early=False):
    nm = len(blocks)

    def plan(ins, outs, sems):
        send_sems, recv_sems, local_sems = sems
        x, y, c = _place()
        me, sibling = (x, y, c), (x, y, 1 - c)
        chips = _other_chips(x, y)

        def rows(m, px, py, pc):
            r = ins[m].shape[0]
            return outs[m].at[pl.ds((4 * px + 2 * py + pc) * r, r), :]

        def copy(m, k, block, to, src=None):
            return pltpu.make_async_remote_copy(
                src_ref=rows(m, *block) if src is None else src, dst_ref=rows(m, *block),
                send_sem=send_sems.at[k * nm + m], recv_sem=recv_sems.at[k * nm + m],
                device_id=to, device_id_type=MESH)

        mine = [pltpu.make_async_copy(ins[m], rows(m, *me), local_sems.at[m]) for m in range(nm)]
        first = [copy(m, 0, me, sibling, src=ins[m]) for m in range(nm)]
        first += [copy(m, 1 + j, me, (*chip, c), src=ins[m]) for j, chip in enumerate(chips) for m in range(nm)]
        return me, sibling, chips, copy, mine, first

    def start(ins, outs, sems):
        *_, mine, first = plan(ins, outs, sems)
        for cp in mine + first:
            cp.start()

    def passed_on(ins, outs, sems):
        me, sibling, chips, copy, _, _ = plan(ins, outs, sems)
        return [copy(m, 4 + j, (*chip, me[2]), sibling) for j, chip in enumerate(chips) for m in range(nm)]

    def relay(ins, outs, sems):
        me, _, chips, copy, _, _ = plan(ins, outs, sems)
        forwards = passed_on(ins, outs, sems)
        for j, chip in enumerate(chips):
            for m in range(nm):
                copy(m, 1 + j, (*chip, me[2]), me).wait_recv()
                forwards[j * nm + m].start()

    def finish(ins, outs, sems):
        me, sibling, chips, copy, mine, first = plan(ins, outs, sems)
        for m in range(nm):
            copy(m, 0, sibling, me).wait_recv()
        for j, chip in enumerate(chips):
            for m in range(nm):
                copy(m, 4 + j, (*chip, 1 - me[2]), me).wait_recv()
        for cp in first + passed_on(ins, outs, sems):
            cp.wait_send()
        for cp in mine:
            cp.wait()

    return _Rider(
        arrays=list(blocks), out_shape=[jax.ShapeDtypeStruct((N_DEV * b.shape[0], b.shape[1]), b.dtype) for b in blocks],
        sems=[pltpu.SemaphoreType.DMA((7 * nm,)), pltpu.SemaphoreType.DMA((7 * nm,)), pltpu.SemaphoreType.DMA((nm,))],
        start=start, finish=finish, relay=relay, relay_early=relay_early)


def _exchange_rider(copies_of, arrays, out_shape, n_copies):
    def copies(ins, outs, sems):
        send_sems, recv_sems = sems
        return [pltpu.make_async_remote_copy(src_ref=src, dst_ref=dst, send_sem=send_sems.at[k], recv_sem=recv_sems.at[k],
                                             device_id=to, device_id_type=MESH)
                for k, (src, dst, to) in enumerate(copies_of(ins, outs))]

    def start(ins, outs, sems):
        for cp in copies(ins, outs, sems):
            cp.start()

    def finish(ins, outs, sems):
        cps = copies(ins, outs, sems)
        for cp in cps:
            cp.wait_recv()
        for cp in cps:
            cp.wait_send()

    return _Rider(arrays=list(arrays), out_shape=out_shape,
                  sems=[pltpu.SemaphoreType.DMA((n_copies,)), pltpu.SemaphoreType.DMA((n_copies,))], start=start, finish=finish)


def _sibling_rider(grads):
    def copies_of(ins, outs):
        x, y, c = _place()
        for g_ref, o_ref in zip(ins, outs):
            r = g_ref.shape[0] // N_DEV
            for q in range(N_CHIPS):
                yield g_ref.at[pl.ds((2 * q + 1 - c) * r, r), :], o_ref.at[pl.ds(q * r, r), :], (x, y, 1 - c)

    return _exchange_rider(copies_of, grads, [jax.ShapeDtypeStruct((g.shape[0] // 2, g.shape[1]), F32) for g in grads],
                           len(grads) * N_CHIPS)


def _chip_sum(grad, from_sibling, place):
    r = grad.shape[0] // N_DEV
    w = grad.shape[1]

    def body(place_ref, g_ref, s_ref, wire_ref, own_ref):
        total = g_ref[...] + s_ref[...]
        wire_ref[...] = total.astype(BF16)

        @pl.when(pl.program_id(0) == place_ref[1])
        def _():
            own_ref[...] = total

    grid_spec = pltpu.PrefetchScalarGridSpec(
        num_scalar_prefetch=1, grid=(N_CHIPS,),
        in_specs=[pl.BlockSpec((r, w), lambda q, p: (2 * q + p[0], 0)), pl.BlockSpec((r, w), lambda q, p: (q, 0))],
        out_specs=[pl.BlockSpec((r, w), lambda q, p: (q, 0)), pl.BlockSpec((r, w), lambda q, p: (0, 0))])
    return pl.pallas_call(
        body, name="grad_chip_sum", grid_spec=grid_spec,
        out_shape=[jax.ShapeDtypeStruct((N_CHIPS * r, w), BF16), jax.ShapeDtypeStruct((r, w), F32)],
        compiler_params=_params("arbitrary"),
    )(place, grad, from_sibling)


def _chips_rider(wires):
    def copies_of(ins, outs):
        x, y, c = _place()
        for w_ref, o_ref in zip(ins, outs):
            r = w_ref.shape[0] // N_CHIPS
            for j, (px, py) in enumerate(_other_chips(x, y)):
                yield w_ref.at[pl.ds((2 * px + py) * r, r), :], o_ref.at[pl.ds(j * r, r), :], (px, py, c)

    return _exchange_rider(copies_of, wires,
                           [jax.ShapeDtypeStruct((3 * (w.shape[0] // N_CHIPS), w.shape[1]), BF16) for w in wires], len(wires) * 3)


SEM = pl.BlockSpec(memory_space=pltpu.SEMAPHORE)
DATAFLOW = pltpu.SideEffectType.DATAFLOW_SIDE_EFFECTING


def _last_exchange_copies(wire_refs, late_ref, land_refs, land_late_ref, send_sems, recv_sems):
    x, y, c = _place()
    ends = []
    for w_ref, o_ref in zip(wire_refs, land_refs):
        r = w_ref.shape[0] // N_CHIPS
        for j, (px, py) in enumerate(_other_chips(x, y)):
            ends.append((w_ref.at[pl.ds((2 * px + py) * r, r), :], o_ref.at[pl.ds(j * r, r), :], (px, py, c)))
    rows = late_ref.shape[0]
    mine = land_late_ref.at[pl.ds((4 * x + 2 * y + c) * rows, rows), :]
    peers = [(x, y, 1 - c)] + [(px, py, pc) for px, py in _other_chips(x, y) for pc in (c, 1 - c)]
    ends += [(late_ref, mine, peer) for peer in peers]
    return [pltpu.make_async_remote_copy(src_ref=src, dst_ref=dst, send_sem=send_sems[k], recv_sem=recv_sems[k],
                                         device_id=to, device_id_type=MESH) for k, (src, dst, to) in enumerate(ends)]


N_LAST_COPIES = 2 * 3 + (N_DEV - 1)


def _last_exchange_start(wires, late):
    n = N_LAST_COPIES
    lands = [lax.empty((3 * (w.shape[0] // N_CHIPS), w.shape[1]), w.dtype) for w in wires]
    land_late = lax.empty((N_DEV * late.shape[0], late.shape[1]), late.dtype)

    def body(wout_ref, win_ref, late_ref, land_wout_ref, land_win_ref, land_late_ref, *outs):
        send_sems, recv_sems, token_ref = outs[:n], outs[n:2 * n], outs[-1]
        for cp in _last_exchange_copies([wout_ref, win_ref], late_ref, [land_wout_ref, land_win_ref], land_late_ref,
                                        send_sems, recv_sems):
            cp.start()
        token_ref[...] = jnp.zeros_like(token_ref)

    operands = [pltpu.with_memory_space_constraint(a, pltpu.HBM) for a in (*wires, late, *lands, land_late)]
    thru = [pltpu.HBM(a.shape, a.dtype) for a in operands]
    out = pl.pallas_call(
        body, name="last_exchange_start",
        out_shape=[pltpu.SemaphoreType.DMA(())] * (2 * n) + thru + [jax.ShapeDtypeStruct((8, LANES), F32)],
        in_specs=[HBM] * 6, out_specs=[SEM] * (2 * n) + [HBM] * 6 + [VMEM],
        input_output_aliases={i: 2 * n + i for i in range(6)},
        compiler_params=pltpu.CompilerParams(has_side_effects=DATAFLOW),
    )(*operands)
    return out[:2 * n], out[2 * n:2 * n + 6], out[-1]


def _last_exchange_wait(sems, buffers, after):
    n = N_LAST_COPIES

    def body(wout_ref, win_ref, late_ref, land_wout_ref, land_win_ref, land_late_ref, *rest):
        send_sems, recv_sems = rest[:n], rest[n:2 * n]
        for cp in _last_exchange_copies([wout_ref, win_ref], late_ref, [land_wout_ref, land_win_ref], land_late_ref,
                                        send_sems, recv_sems):
            cp.wait_send()
            cp.wait_recv()

    out = pl.pallas_call(
        body, name="last_exchange_wait", out_shape=[pltpu.HBM(a.shape, a.dtype) for a in buffers],
        in_specs=[HBM] * 6 + [SEM] * (2 * n) + [pl.BlockSpec(memory_space=pl.ANY)], out_specs=[HBM] * 6,
        input_output_aliases={i: i for i in range(6)},
        compiler_params=pltpu.CompilerParams(has_side_effects=DATAFLOW),
    )(*buffers, *sems, after)
    return out[3], out[4], out[5]


def _adamw_math(w, g, m, v):
    m = ADAM_B1 * m + (1.0 - ADAM_B1) * g
    v = ADAM_B2 * v + (1.0 - ADAM_B2) * jnp.square(g)
    m_hat = m / (1.0 - ADAM_B1 ** ADAM_STEP)
    v_hat = v / (1.0 - ADAM_B2 ** ADAM_STEP)
    delta = -ADAM_LR * (m_hat / (jnp.sqrt(v_hat) + ADAM_EPS) + ADAM_WD * w)
    return delta, m, v


def _reduce_adamw(own, received, w, m, v, after):
    r = own.shape[0]

    def body(own_ref, rec_ref, w_ref, m_ref, v_ref, after_ref, g_ref, d_ref, nm_ref, nv_ref):
        del after_ref
        g = ((own_ref[...] + rec_ref[0:r, :].astype(F32)) + rec_ref[r:2 * r, :].astype(F32)) + rec_ref[2 * r:, :].astype(F32)
        g_ref[...] = g
        d_ref[...], nm_ref[...], nv_ref[...] = _adamw_math(w_ref[...], g, m_ref[...], v_ref[...])

    shape = jax.ShapeDtypeStruct(own.shape, F32)
    return pl.pallas_call(
        body, name="reduce_adamw", in_specs=[VMEM] * 5 + [pl.BlockSpec(memory_space=pl.ANY)], out_specs=[VMEM] * 4,
        out_shape=[shape] * 4, compiler_params=_params(),
    )(own, received, w, m, v, after)


SMALL_WIDE = (("w_pool", 65536),)
SMALL_EARLY = (("b_pool", 512), ("pool_scale", 512), ("b_out", 1024), ("g_ffn", 1024), ("g_final", 1024), ("loss", 1024))
SMALL_LATE = (("sinks", 8), ("g_mix", 1024), ("b_in", 1280))
SMALL = SMALL_WIDE + SMALL_EARLY + SMALL_LATE


def _small_rows(size):
    return -(-size // (8 * LANES)) * 8


def _pack_small(values, entries=SMALL):
    parts = []
    for name, size in entries:
        flat = values[name].reshape(-1).astype(F32)
        parts.append(jnp.pad(flat, (0, _small_rows(size) * LANES - size)).reshape(-1, LANES))
    return jnp.concatenate(parts, axis=0)


def _unpack_small(packed, shapes):
    out, row = {}, 0
    for name, size in SMALL:
        rows = _small_rows(size)
        if name in shapes:
            out[name] = packed[row:row + rows].reshape(-1)[:size].reshape(shapes[name])
        row += rows
    return out


def _small_sum_adamw(gathered, w, m, v):
    n = len(gathered)

    def body(*refs):
        w_ref, m_ref, v_ref, g_ref, d_ref, nm_ref, nv_ref = refs[n:]

        def total(ref):
            rows = ref.shape[0] // N_DEV
            acc = ref[0:rows, :].astype(F32)
            for dev in range(1, N_DEV):
                acc = acc + ref[dev * rows:(dev + 1) * rows, :].astype(F32)
            return acc

        g = jnp.concatenate([total(ref) for ref in refs[:n]], axis=0)
        g_ref[...] = g
        d_ref[...], nm_ref[...], nv_ref[...] = _adamw_math(w_ref[...], g, m_ref[...], v_ref[...])

    shape = jax.ShapeDtypeStruct(w.shape, F32)
    return pl.pallas_call(
        body, name="small_sum_adamw", in_specs=[VMEM] * (n + 3), out_specs=[VMEM] * 4, out_shape=[shape] * 4,
        compiler_params=_params(),
    )(*gathered, w, m, v)


def kernel(x, g_mix, w_in, b_in, sinks, w_pool, b_pool, pool_scale, w_out, b_out, g_ffn, w_gate, w_up, w_down, g_final, loss_target, m_g_mix, m_w_in, m_b_in, m_sinks, m_w_pool, m_b_pool, m_pool_scale, m_w_out, m_b_out, m_g_ffn, m_w_gate, m_w_up, m_w_down, m_g_final, v_g_mix, v_w_in, v_b_in, v_sinks, v_w_pool, v_b_pool, v_pool_scale, v_w_out, v_b_out, v_g_ffn, v_w_gate, v_w_up, v_w_down, v_g_final):
    weights = dict(g_mix=g_mix, w_in=w_in, b_in=b_in, sinks=sinks, w_pool=w_pool, b_pool=b_pool, pool_scale=pool_scale,
                   w_out=w_out, b_out=b_out, g_ffn=g_ffn, w_gate=w_gate, w_up=w_up, w_down=w_down, g_final=g_final)
    mom1 = dict(g_mix=m_g_mix, w_in=m_w_in, b_in=m_b_in, sinks=m_sinks, w_pool=m_w_pool, b_pool=m_b_pool,
                pool_scale=m_pool_scale, w_out=m_w_out, b_out=m_b_out, g_ffn=m_g_ffn, w_gate=m_w_gate, w_up=m_w_up,
                w_down=m_w_down, g_final=m_g_final)
    mom2 = dict(g_mix=v_g_mix, w_in=v_w_in, b_in=v_b_in, sinks=v_sinks, w_pool=v_w_pool, b_pool=v_b_pool,
                pool_scale=v_pool_scale, w_out=v_w_out, b_out=v_b_out, g_ffn=v_g_ffn, w_gate=v_w_gate, w_up=v_w_up,
                w_down=v_w_down, g_final=v_g_final)
    order = ("g_mix", "w_in", "b_in", "sinks", "w_pool", "b_pool", "pool_scale", "w_out", "b_out", "g_ffn",
             "w_gate", "w_up", "w_down", "g_final")
    big = ("w_in", "w_out", "w_gate", "w_up", "w_down")
    transposed = ("w_in", "w_gate", "w_up")

    def row_shard(name, a):
        return a[0].T if name in transposed else a[0]

    shard = {n: row_shard(n, weights[n]).astype(BF16) for n in big}
    xs, target = x[0], loss_target[0]
    cos, sin = _rope_tables(xs.shape[0])
    wp_b = w_pool[0].astype(BF16)
    bp = b_pool.reshape(1, POOL_WIDTH)
    ps = pool_scale.reshape(1, POOL_WIDTH)
    g_fin = g_final.reshape(1, D_MODEL)
    px, py, pc = _place()
    place = jnp.stack([pc, 2 * px + py]).astype(jnp.int32)

    (win_t,) = _alone(_gather_rider([shard["w_in"]]), "gather_w_in")
    q, kz, vz, vt, mixed, pool, w_out_b, wg_t = _fwd_inproj(
        xs, g_mix, win_t, b_in, cos, sin, wp_b, bp, ps,
        rider=_gather_rider([shard["w_out"], shard["w_gate"]], relay_early=True))
    attn, lse, wu_t = _attn_fwd(q, kz, vt, sinks, rider=_gather_rider([shard["w_up"]]))
    x2, gate, up, act, wd = _fwd_outproj_ffn_act(attn, pool, w_out_b, b_out, xs, g_ffn, wg_t, wu_t,
                                                 rider=_gather_rider([shard["w_down"]], relay_early=True))
    dx3, sq, dg_final, d_wd = _fwd_down_loss(act, x2, wd, g_fin, target)

    dx2, dg_ffn, db_out, d_wg_t, d_wu_t, wd_sibling = _bwd_ffn(
        dx3, gate, up, x2, wd, wg_t, wu_t, g_ffn, rider=_sibling_rider([d_wd]))
    wd_sum = _chip_sum(d_wd, wd_sibling, place)
    in_grads = [d_wg_t, d_wu_t]
    dattn, du, d_wout, d_wpool, d_bpool, d_pscale, wd_received, *in_sibling = _bwd_outproj_pool(
        dx2, attn, pool, mixed, w_out_b, wp_b, bp, ps, rider=_join(_chips_rider([wd_sum[0]]), _sibling_rider(in_grads)))
    in_sums = [_chip_sum(g, s, place) for g, s in zip(in_grads, in_sibling)]
    small_wide = _pack_small(dict(w_pool=d_wpool), SMALL_WIDE).astype(BF16)
    small_early = _pack_small(dict(b_pool=d_bpool, pool_scale=d_pscale, b_out=db_out, g_ffn=dg_ffn,
                                   g_final=dg_final, loss=sq), SMALL_EARLY)
    dq, dk, dv, d_sinks, *landed = _attn_bwd(
        q, kz, vz, dattn, lse, sinks,
        rider=_join(_chips_rider([wire for wire, _ in in_sums]), _sibling_rider([d_wout]),
                    _gather_rider([small_wide, small_early])))
    ffn_sums, ffn_received = in_sums + [wd_sum], landed[:2] + [wd_received]
    wout_sum = _chip_sum(d_wout, landed[2], place)
    gathered_wide, gathered_early = landed[3], landed[4]
    dx, d_win_t, d_bin, d_gmix = _bwd_inproj(dq, dk, dv, du, cos, sin, win_t, xs, g_mix, dx2)
    (win_sibling,) = _alone(_sibling_rider([d_win_t]), "grad_exchange_sibling")
    win_sum = _chip_sum(d_win_t, win_sibling, place)
    small_late = _pack_small(dict(sinks=d_sinks, g_mix=d_gmix, b_in=d_bin), SMALL_LATE)

    grad, delta, new_m, new_v = {}, {}, {}, {}

    def update(n, own, rec, after):
        results = _reduce_adamw(own, rec, row_shard(n, weights[n]), row_shard(n, mom1[n]), row_shard(n, mom2[n]), after)
        grad[n], delta[n], new_m[n], new_v[n] = [(a.T if n in transposed else a)[None] for a in results]
        return results[0]

    sems, in_flight, after = _last_exchange_start([wout_sum[0], win_sum[0]], small_late)
    for n, (_, own), rec in zip(("w_gate", "w_up", "w_down"), ffn_sums, ffn_received):
        after = update(n, own, rec, after)
    wout_received, win_received, gathered_late = _last_exchange_wait(sems, in_flight, after)
    gathered_late = lax.dynamic_update_slice(gathered_late, small_late, ((4 * px + 2 * py + pc) * small_late.shape[0], 0))
    update("w_out", wout_sum[1], wout_received, after)
    update("w_in", win_sum[1], win_received, after)

    shapes = {n: weights[n].shape for n in order if n not in big}
    zero_loss = jnp.zeros((1, D_MODEL), F32)
    packed = _small_sum_adamw(
        [gathered_wide, gathered_early, gathered_late], _pack_small({**weights, "loss": zero_loss}),
        _pack_small({**mom1, "loss": zero_loss}), _pack_small({**mom2, "loss": zero_loss}))
    for store, pk in zip((grad, delta, new_m, new_v), packed):
        store.update(_unpack_small(pk, shapes))
    loss_rows = _unpack_small(packed[0], {"loss": (D_MODEL,)})["loss"]
    loss = (0.5 / D_MODEL) * jnp.sum(loss_rows)

    return (loss, dx[None], *[grad[n] for n in order], *[delta[n] for n in order],
            *[new_m[n] for n in order], *[new_v[n] for n in order])
```

```python
from typing import Any, Callable, NamedTuple, Sequence

import jax
import jax.numpy as jnp
from jax import lax
from jax.experimental import pallas as pl
from jax.experimental.pallas import tpu as pltpu

D_MODEL = 1024
ATTN_WIDTH = 512
KV_WIDTH = 128
POOL_WIDTH = 512
HEAD_DIM = 64
N_Q_HEADS = 8
N_KV_HEADS = 2
GQA_GROUP = 4
BLOCK = 128
POOL_SIZES = (2, 4, 8, 16)
POOL_GROUP_WIDTH = 128
POOL_HALO = 16
IN_WIDTH = 1280
D_FF = 2816
RMS_EPS = 1e-5
ROPE_THETA = 10000.0
Q_SCALE = HEAD_DIM ** -0.5

ADAM_LR = 0.001
ADAM_B1 = 0.9
ADAM_B2 = 0.999
ADAM_EPS = 1e-08
ADAM_WD = 0.01
ADAM_STEP = 10

N_DEV = 8
N_CHIPS = 4
LANES = 128
VMEM_LIMIT_BYTES = 60 * 1024 * 1024

F32 = jnp.float32
BF16 = jnp.bfloat16
MESH = pl.DeviceIdType.MESH
HBM = pl.BlockSpec(memory_space=pltpu.HBM)
VMEM = pl.BlockSpec(memory_space=pltpu.VMEM)


def _params(*semantics):
    return pltpu.CompilerParams(dimension_semantics=semantics or None, vmem_limit_bytes=VMEM_LIMIT_BYTES)


def _nn(a, b):
    return jnp.dot(a, b, preferred_element_type=F32)


def _nt(a, b):
    return lax.dot_general(a, b, (((1,), (1,)), ((), ())), preferred_element_type=F32)


def _tn(a, b):
    return lax.dot_general(a, b, (((0,), (0,)), ((), ())), preferred_element_type=F32)


def _full(shape):
    return pl.BlockSpec(shape, lambda *_: (0,) * len(shape))


def _rows(tm, width):
    return pl.BlockSpec((tm, width), lambda i, *_: (i, 0))


def _nothing(ins, outs, sems):
    del ins, outs, sems


RELAY_STEPS_BEFORE_LAST = 2


class _Rider(NamedTuple):
    arrays: Sequence[Any]
    out_shape: Sequence[Any]
    sems: Sequence[Any]
    start: Callable[..., None]
    finish: Callable[..., None]
    relay: Callable[..., None] = _nothing
    relay_early: bool = False


def _gridded(body, rider, *, name, grid, in_specs, out_specs, out_shape, scratch_shapes, args):
    params = _params("arbitrary")
    if rider is None:
        return pl.pallas_call(body, name=name, grid=grid, in_specs=in_specs, out_specs=out_specs, out_shape=out_shape,
                              scratch_shapes=scratch_shapes, compiler_params=params)(*args)
    bounds, total = [], 0
    for n in (len(in_specs), len(rider.arrays), len(out_specs), len(rider.out_shape), len(scratch_shapes), len(rider.sems)):
        bounds.append((total, total + n))
        total += n
    last = grid[0] - 1
    relay_step = max(last - RELAY_STEPS_BEFORE_LAST, 0) if rider.relay_early else last

    def riding(*refs):
        ins, r_ins, outs, r_outs, scratch, r_sems = (refs[lo:hi] for lo, hi in bounds)

        @pl.when(pl.program_id(0) == 0)
        def _():
            rider.start(r_ins, r_outs, r_sems)

        body(*ins, *outs, *scratch)

        @pl.when(pl.program_id(0) == relay_step)
        def _():
            rider.relay(r_ins, r_outs, r_sems)

        @pl.when(pl.program_id(0) == last)
        def _():
            rider.finish(r_ins, r_outs, r_sems)

    return pl.pallas_call(
        riding, name=name, grid=grid, in_specs=list(in_specs) + [HBM] * len(rider.arrays),
        out_specs=list(out_specs) + [HBM] * len(rider.out_shape), out_shape=list(out_shape) + list(rider.out_shape),
        scratch_shapes=list(scratch_shapes) + list(rider.sems), compiler_params=params)(*args, *rider.arrays)


def _join(*riders):
    def phase(which):
        def run(ins, outs, sems):
            i = o = s = 0
            for r in riders:
                ni, no, ns = len(r.arrays), len(r.out_shape), len(r.sems)
                getattr(r, which)(ins[i:i + ni], outs[o:o + no], sems[s:s + ns])
                i, o, s = i + ni, o + no, s + ns
        return run

    return _Rider(arrays=[a for r in riders for a in r.arrays], out_shape=[a for r in riders for a in r.out_shape],
                  sems=[a for r in riders for a in r.sems], start=phase("start"), finish=phase("finish"), relay=phase("relay"),
                  relay_early=all(r.relay_early for r in riders if r.relay is not _nothing))


def _alone(rider, name):
    n_in, n_out = len(rider.arrays), len(rider.out_shape)

    def body(*refs):
        parts = refs[:n_in], refs[n_in:n_in + n_out], refs[n_in + n_out:]
        rider.start(*parts)
        rider.relay(*parts)
        rider.finish(*parts)

    return pl.pallas_call(body, name=name, in_specs=[HBM] * n_in, out_specs=[HBM] * n_out, out_shape=list(rider.out_shape),
                          scratch_shapes=list(rider.sems))(*rider.arrays)


def _rot_half(t):
    n = t.shape[1]
    lane = lax.broadcasted_iota(jnp.int32, t.shape, 1)
    return jnp.where((lane % HEAD_DIM) < HEAD_DIM // 2, pltpu.roll(t, n - HEAD_DIM // 2, 1), pltpu.roll(t, HEAD_DIM // 2, 1))


def _tile_tables(base_ref, tile_ref):
    start = tile_ref[0]
    cos_0, sin_0 = start[0:1, :], start[1:2, :]
    return base_ref[0] * cos_0 - base_ref[1] * sin_0, base_ref[2] * cos_0 + base_ref[3] * sin_0


def _rope(t, cos, sin):
    reps = t.shape[1] // LANES
    if reps > 1:
        cos, sin = jnp.tile(cos, (1, reps)), jnp.tile(sin, (1, reps))
    return t * cos + _rot_half(t) * sin


def _rope_bwd(d, cos, sin):
    reps = d.shape[1] // LANES
    if reps > 1:
        cos, sin = jnp.tile(cos, (1, reps)), jnp.tile(sin, (1, reps))
    return d * cos + _rot_half(d * sin)


KV_SPREAD = 4 * LANES


def _spread_kv(t):
    low = lax.broadcasted_iota(jnp.int32, t.shape, 1) < HEAD_DIM
    swapped = pltpu.roll(t, HEAD_DIM, 1)
    zero = jnp.zeros_like(t)
    return jnp.concatenate([jnp.where(low, t, zero), jnp.where(low, zero, swapped),
                            jnp.where(low, swapped, zero), jnp.where(low, zero, t)], axis=1)


def _rms(x):
    r = lax.rsqrt(jnp.mean(x * x, axis=-1, keepdims=True) + RMS_EPS)
    return x * r, r


def _rms_bwd(dh, n, r, g):
    dn = dh * g
    dx = r * (dn - n * jnp.mean(dn * n, axis=-1, keepdims=True))
    return dx, jnp.sum(dh * n, axis=0, keepdims=True)


def _token_tile(s):
    return min(512, s)


def _window_mean(window_sum, pos, size):
    head = window_sum[:POOL_HALO, :] / jnp.minimum(pos[:POOL_HALO, :] + 1, size).astype(F32)
    return jnp.concatenate([head, window_sum[POOL_HALO:, :] * (1.0 / size)], axis=0)


def _fwd_inproj(x, g_mix, win_t, b_in, cos, sin, w_pool, b_pool, pool_scale, rider=None):
    s = x.shape[0]
    tm = _token_tile(s)

    def body(x_ref, g_ref, w_ref, b_ref, cos_ref, sin_ref, wp_ref, bp_ref, ps_ref,
             q_ref, k_ref, v_ref, vt_ref, mix_ref, pool_ref, tail_ref):
        i = pl.program_id(0)

        @pl.when(i == 0)
        def _():
            tail_ref[...] = jnp.zeros_like(tail_ref)

        n, _ = _rms(x_ref[...])
        h = (n * g_ref[...]).astype(BF16)
        z = _nt(h, w_ref[...]) + b_ref[...]
        cos_t, sin_t = _tile_tables(cos_ref, sin_ref)
        q_ref[...] = (_rope(z[:, :ATTN_WIDTH], cos_t, sin_t) * Q_SCALE).astype(BF16)
        k_ref[...] = _spread_kv(_rope(z[:, ATTN_WIDTH:ATTN_WIDTH + KV_WIDTH], cos_t, sin_t)).astype(BF16)
        vz = _spread_kv(z[:, ATTN_WIDTH + KV_WIDTH:ATTN_WIDTH + 2 * KV_WIDTH])
        v_ref[...] = vz.astype(BF16)
        vt_ref[...] = vz.T.astype(BF16)
        u = z[:, ATTN_WIDTH + 2 * KV_WIDTH:]
        u_ext = jnp.concatenate([tail_ref[...], u], axis=0)
        tail_ref[...] = u[tm - POOL_HALO:, :]
        pos = lax.broadcasted_iota(jnp.int32, (tm, POOL_GROUP_WIDTH), 0) + i * tm
        for g, size in enumerate(POOL_SIZES):
            cols = slice(g * POOL_GROUP_WIDTH, (g + 1) * POOL_GROUP_WIDTH)
            a = u_ext[:, cols]
            shift = 1
            while shift < size:
                a = a + pltpu.roll(a, shift, 0)
                shift *= 2
            mixed = (_window_mean(a[POOL_HALO:, :], pos, size) - u[:, cols]).astype(BF16)
            pre = _nn(mixed, wp_ref[g]) + bp_ref[:, cols]
            mix_ref[:, cols] = mixed
            pool_ref[:, cols] = (pre * ps_ref[:, cols]).astype(BF16)

    bf = lambda w: jax.ShapeDtypeStruct((s, w), BF16)
    return _gridded(
        body, rider, name="fwd_inproj", grid=(s // tm,),
        in_specs=[_rows(tm, D_MODEL), _full((1, D_MODEL)), _full((IN_WIDTH, D_MODEL)), _full((1, IN_WIDTH)),
                  _full((4, tm, LANES)), pl.BlockSpec((1, 2, LANES), lambda i: (i, 0, 0)), _full((4, POOL_GROUP_WIDTH, POOL_GROUP_WIDTH)),
                  _full((1, POOL_WIDTH)), _full((1, POOL_WIDTH))],
        out_specs=[_rows(tm, ATTN_WIDTH), _rows(tm, KV_SPREAD), _rows(tm, KV_SPREAD),
                   pl.BlockSpec((KV_SPREAD, tm), lambda i: (0, i)), _rows(tm, POOL_WIDTH), _rows(tm, POOL_WIDTH)],
        out_shape=[bf(ATTN_WIDTH), bf(KV_SPREAD), bf(KV_SPREAD), jax.ShapeDtypeStruct((KV_SPREAD, s), BF16),
                   bf(POOL_WIDTH), bf(POOL_WIDTH)],
        scratch_shapes=[pltpu.VMEM((POOL_HALO, POOL_WIDTH), F32)],
        args=(x, g_mix, win_t, b_in, cos, sin, w_pool, b_pool, pool_scale))


ATTN_TILE = 1024
PAIR = 2 * LANES


def _band_masks(tile):
    j = lax.broadcasted_iota(jnp.int32, (4 * BLOCK, 2 * BLOCK), 0) % (2 * BLOCK)
    r = lax.broadcasted_iota(jnp.int32, (4 * BLOCK, 2 * BLOCK), 1) % BLOCK
    band = (j > r) & (j <= r + BLOCK)
    return band & ((tile > 0) | (j >= BLOCK)), band


def _band(cur_ref, prev_ref, b, kv):
    halves = []
    for half in range(2):
        cols = slice(kv * PAIR + half * LANES, kv * PAIR + (half + 1) * LANES)
        if b == 0:
            halves.append(jnp.concatenate([prev_ref[:, cols], cur_ref[0:BLOCK, cols]], axis=0))
        else:
            halves.append(cur_ref[(b - 1) * BLOCK:(b + 1) * BLOCK, cols])
    return jnp.concatenate(halves, axis=0)


def _stack_pair(ref, rows, kv):
    return jnp.concatenate([ref[rows, kv * PAIR:kv * PAIR + LANES], ref[rows, kv * PAIR + LANES:(kv + 1) * PAIR]], axis=0)


def _pair_heads(kv, half):
    return GQA_GROUP * kv + half, GQA_GROUP * kv + 2 + half


def _band_t(cur_ref, prev_ref, b, kv):
    halves = []
    for half in range(2):
        lanes = slice(kv * PAIR + half * LANES, kv * PAIR + (half + 1) * LANES)
        if b == 0:
            halves.append(jnp.concatenate([prev_ref[lanes, :], cur_ref[lanes, 0:BLOCK]], axis=1))
        else:
            halves.append(cur_ref[lanes, (b - 1) * BLOCK:(b + 1) * BLOCK])
    return jnp.concatenate(halves, axis=1)


def _reduce_rows(x, op, reduce):
    while x.shape[0] > 8:
        half = x.shape[0] // 2
        x = op(x[:half], x[half:])
    return reduce(x, axis=0, keepdims=True)


def _per_query(ref, rows, top, bottom):
    return jnp.concatenate([ref[top:top + 1, rows], ref[bottom:bottom + 1, rows]], axis=1)


def _sink_per_query(sink_ref, top, bottom):
    first_slab = lax.broadcasted_iota(jnp.int32, (1, 2 * BLOCK), 1) < BLOCK
    return jnp.where(first_slab, sink_ref[:, top:top + 1], sink_ref[:, bottom:bottom + 1])


def _attn_fwd(q, kz, vt, sinks, rider=None):
    s = q.shape[0]
    tq = min(ATTN_TILE, s)

    def body(q_ref, k_ref, kp_ref, vt_ref, vtp_ref, sink_ref, o_ref, lse_ref):
        first, band = _band_masks(pl.program_id(0))
        chains = [(b, kv) for b in range(tq // BLOCK) for kv in range(N_KV_HEADS)]

        def scores(b, kv):
            rows = slice(b * BLOCK, (b + 1) * BLOCK)
            return _nt(_band(k_ref, kp_ref, b, kv), _stack_pair(q_ref, rows, kv))

        def store(b, kv, ot):
            rows = slice(b * BLOCK, (b + 1) * BLOCK)
            o = ot.T.astype(BF16)
            o_ref[rows, kv * PAIR:kv * PAIR + LANES] = o[:BLOCK]
            o_ref[rows, kv * PAIR + LANES:(kv + 1) * PAIR] = o[BLOCK:]

        ahead = scores(*chains[0])
        behind = None
        for n, (b, kv) in enumerate(chains):
            rows = slice(b * BLOCK, (b + 1) * BLOCK)
            st = jnp.where(first if b == 0 else band, ahead, -jnp.inf)
            if n + 1 < len(chains):
                ahead = scores(*chains[n + 1])
            probs = []
            for half in range(2):
                top, bottom = _pair_heads(kv, half)
                sink = _sink_per_query(sink_ref, top, bottom)
                sh = st[half * 2 * BLOCK:(half + 1) * 2 * BLOCK, :]
                m = jnp.maximum(_reduce_rows(sh, jnp.maximum, jnp.max), sink)
                p = jnp.exp(sh - m)
                denom = _reduce_rows(p, jnp.add, jnp.sum) + jnp.exp(sink - m)
                probs.append((p * (1.0 / denom)).astype(BF16))
                lse = m + jnp.log(denom)
                lse_ref[top:top + 1, rows] = lse[:, :BLOCK]
                lse_ref[bottom:bottom + 1, rows] = lse[:, BLOCK:]
            ot = _nn(_band_t(vt_ref, vtp_ref, b, kv), jnp.concatenate(probs, axis=0))
            if behind is not None:
                store(*behind)
            behind = (b, kv, ot)
        store(*behind)

    per = tq // BLOCK
    cur = lambda w: pl.BlockSpec((tq, w), lambda i: (i, 0))
    prev = pl.BlockSpec((BLOCK, KV_SPREAD), lambda i: (jnp.maximum(per * i - 1, 0), 0))
    cur_t = pl.BlockSpec((KV_SPREAD, tq), lambda i: (0, i))
    prev_t = pl.BlockSpec((KV_SPREAD, BLOCK), lambda i: (0, jnp.maximum(per * i - 1, 0)))
    return _gridded(
        body, rider, name="attn_fwd", grid=(s // tq,),
        in_specs=[cur(ATTN_WIDTH), cur(KV_SPREAD), prev, cur_t, prev_t, _full((1, N_Q_HEADS))],
        out_specs=[cur(ATTN_WIDTH), pl.BlockSpec((N_Q_HEADS, tq), lambda i: (0, i))],
        out_shape=[jax.ShapeDtypeStruct((s, ATTN_WIDTH), BF16), jax.ShapeDtypeStruct((N_Q_HEADS, s), F32)],
        scratch_shapes=[], args=(q, kz, kz, vt, vt, sinks))


FF_CHUNK = 256
TN_ROW_CHUNK = 256


def _resident(shape):
    return pl.BlockSpec(shape, lambda *_: (0,) * len(shape), pipeline_mode=pl.Buffered(1))


def _flush_rows(acc_ref, out_ref, sem, rows, is_last):
    @pl.when(is_last)
    def _():
        pltpu.make_async_copy(acc_ref.at[rows, :], out_ref.at[rows, :], sem).start()


def _flush_wait(acc_ref, out_ref, sem, is_last):
    @pl.when(is_last)
    def _():
        pltpu.make_async_copy(acc_ref, out_ref, sem).wait()


def _accumulate_tn(acc_ref, a_ref, b, out_ref, sem, is_last):
    for m0 in range(0, acc_ref.shape[0], TN_ROW_CHUNK):
        rows = slice(m0, m0 + TN_ROW_CHUNK)
        acc_ref[rows, :] += _tn(a_ref[:, rows], b)
        _flush_rows(acc_ref, out_ref, sem, rows, is_last)
    _flush_wait(acc_ref, out_ref, sem, is_last)


def _fwd_outproj_ffn_act(attn, pool, w_out, b_out, x, g_ffn, wg_t, wu_t, rider=None):
    s = x.shape[0]
    tm = _token_tile(s)

    def body(a_ref, p_ref, w_ref, b_ref, x_ref, g_ref, wg_ref, wu_ref, x2_ref, gate_ref, up_ref, act_ref):
        x2 = x_ref[...] + _nn(a_ref[...], w_ref[:ATTN_WIDTH, :]) + _nn(p_ref[...], w_ref[ATTN_WIDTH:, :]) + b_ref[...]
        x2_ref[...] = x2
        n, _ = _rms(x2)
        h = (n * g_ref[...]).astype(BF16)

        def products(c0):
            return _nt(h, wg_ref[c0:c0 + FF_CHUNK, :]), _nt(h, wu_ref[c0:c0 + FF_CHUNK, :])

        ahead = products(0)
        for c0 in range(0, D_FF, FF_CHUNK):
            cols = slice(c0, c0 + FF_CHUNK)
            gate, up = ahead
            if c0 + FF_CHUNK < D_FF:
                ahead = products(c0 + FF_CHUNK)
            gate_ref[:, cols] = gate.astype(BF16)
            up_ref[:, cols] = up.astype(BF16)
            act_ref[:, cols] = (gate * jax.nn.sigmoid(gate) * up).astype(BF16)

    act_shape = jax.ShapeDtypeStruct((s, D_FF), BF16)
    return _gridded(
        body, rider, name="fwd_outproj_ffn_act", grid=(s // tm,),
        in_specs=[_rows(tm, ATTN_WIDTH), _rows(tm, POOL_WIDTH), _resident((D_MODEL, D_MODEL)), _full((1, D_MODEL)),
                  _rows(tm, D_MODEL), _full((1, D_MODEL)), _resident((D_FF, D_MODEL)), _resident((D_FF, D_MODEL))],
        out_specs=[_rows(tm, D_MODEL)] + [_rows(tm, D_FF)] * 3,
        out_shape=[jax.ShapeDtypeStruct((s, D_MODEL), F32)] + [act_shape] * 3,
        scratch_shapes=[], args=(attn, pool, w_out, b_out, x, g_ffn, wg_t, wu_t))


def _fwd_down_loss(act, x2, wd, g_final, target):
    s = x2.shape[0]
    tm = _token_tile(s)
    last = s // tm - 1

    def body(a_ref, x2_ref, wd_ref, g_ref, t_ref, dx3_ref, sq_ref, dg_ref, dwd_ref, acc_ref, sem):
        @pl.when(pl.program_id(0) == 0)
        def _():
            sq_ref[...] = jnp.zeros_like(sq_ref)
            dg_ref[...] = jnp.zeros_like(dg_ref)
            acc_ref[...] = jnp.zeros_like(acc_ref)

        x3 = x2_ref[...] + _nn(a_ref[...], wd_ref[...])
        n, r = _rms(x3)
        g = g_ref[...]
        diff = n * g - t_ref[...]
        sq_ref[...] += jnp.sum(diff * diff, axis=0, keepdims=True)
        dx3, dg = _rms_bwd(diff * (1.0 / D_MODEL), n, r, g)
        dg_ref[...] += dg
        dx3_ref[...] = dx3
        _accumulate_tn(acc_ref, a_ref, dx3.astype(BF16), dwd_ref, sem, pl.program_id(0) == last)

    return pl.pallas_call(
        body, name="fwd_down_loss", grid=(s // tm,),
        in_specs=[_rows(tm, D_FF), _rows(tm, D_MODEL), _resident((D_FF, D_MODEL)), _full((1, D_MODEL)), _rows(tm, D_MODEL)],
        out_specs=[_rows(tm, D_MODEL), _full((1, D_MODEL)), _full((1, D_MODEL)), HBM],
        out_shape=[jax.ShapeDtypeStruct((s, D_MODEL), F32),
                   jax.ShapeDtypeStruct((1, D_MODEL), F32), jax.ShapeDtypeStruct((1, D_MODEL), F32),
                   jax.ShapeDtypeStruct((D_FF, D_MODEL), F32)],
        scratch_shapes=[pltpu.VMEM((D_FF, D_MODEL), F32), pltpu.SemaphoreType.DMA],
        compiler_params=_params("arbitrary"),
    )(act, x2, wd, g_final, target)


FFN_BWD_TILE = 256


def _bwd_ffn(dx3, gate, up, x2, wd, wg_t, wu_t, g_ffn, rider=None):
    s = x2.shape[0]
    tm = min(FFN_BWD_TILE, s)
    last = s // tm - 1

    def body(dx3_ref, gate_ref, up_ref, x2_ref, wd_ref, wg_ref, wu_ref, g_ref,
             dx2_ref, dg_ref, db_ref, dwg_ref, dwu_ref, dgate_ref, dup_ref, accg_ref, accu_ref, sems):
        @pl.when(pl.program_id(0) == 0)
        def _():
            dg_ref[...] = jnp.zeros_like(dg_ref)
            db_ref[...] = jnp.zeros_like(db_ref)
            accg_ref[...] = jnp.zeros_like(accg_ref)
            accu_ref[...] = jnp.zeros_like(accu_ref)

        dx3b = dx3_ref[...].astype(BF16)
        g = g_ref[...]
        n, r = _rms(x2_ref[...])
        h = (n * g).astype(BF16)
        ahead = _nt(dx3b, wd_ref[0:FF_CHUNK, :])
        for c0 in range(0, D_FF, FF_CHUNK):
            cols = slice(c0, c0 + FF_CHUNK)
            dact = ahead
            if c0 + FF_CHUNK < D_FF:
                ahead = _nt(dx3b, wd_ref[c0 + FF_CHUNK:c0 + 2 * FF_CHUNK, :])
            gate = gate_ref[:, cols].astype(F32)
            up = up_ref[:, cols].astype(F32)
            sig = jax.nn.sigmoid(gate)
            silu = gate * sig
            dup = (dact * silu).astype(BF16)
            dgate = (dact * up * (sig + silu * (1.0 - sig))).astype(BF16)
            dup_ref[:, cols] = dup
            dgate_ref[:, cols] = dgate
            accg_ref[cols, :] += _tn(dgate, h)
            accu_ref[cols, :] += _tn(dup, h)
        dh2 = _nn(dgate_ref[...], wg_ref[...]) + _nn(dup_ref[...], wu_ref[...])
        dx, dg = _rms_bwd(dh2, n, r, g)
        dx2 = dx3_ref[...] + dx
        dg_ref[...] += dg
        db_ref[...] += jnp.sum(dx2, axis=0, keepdims=True)
        dx2_ref[...] = dx2

        @pl.when(pl.program_id(0) == last)
        def _():
            outs = [pltpu.make_async_copy(accg_ref, dwg_ref, sems.at[0]), pltpu.make_async_copy(accu_ref, dwu_ref, sems.at[1])]
            for cp in outs:
                cp.start()
            for cp in outs:
                cp.wait()

    grad_shape = jax.ShapeDtypeStruct((D_FF, D_MODEL), F32)
    weight = _resident((D_FF, D_MODEL))
    return _gridded(
        body, rider, name="bwd_ffn", grid=(s // tm,),
        in_specs=[_rows(tm, D_MODEL), _rows(tm, D_FF), _rows(tm, D_FF),
                  _rows(tm, D_MODEL), weight, weight, weight, _full((1, D_MODEL))],
        out_specs=[_rows(tm, D_MODEL), _full((1, D_MODEL)), _full((1, D_MODEL)), HBM, HBM],
        out_shape=[jax.ShapeDtypeStruct((s, D_MODEL), F32),
                   jax.ShapeDtypeStruct((1, D_MODEL), F32), jax.ShapeDtypeStruct((1, D_MODEL), F32), grad_shape, grad_shape],
        scratch_shapes=[pltpu.VMEM((tm, D_FF), BF16), pltpu.VMEM((tm, D_FF), BF16),
                        pltpu.VMEM((D_FF, D_MODEL), F32), pltpu.VMEM((D_FF, D_MODEL), F32), pltpu.SemaphoreType.DMA((2,))],
        args=(dx3, gate, up, x2, wd, wg_t, wu_t, g_ffn))


def _bwd_outproj_pool(dx2, attn, pool, mixed, w_out, w_pool, b_pool, pool_scale, rider=None):
    s = dx2.shape[0]
    tm = min(2 * _token_tile(s), s)
    nt = s // tm

    def body(dx_ref, a_ref, p_ref, mix_ref, w_ref, wp_ref, bp_ref, ps_ref,
             dattn_ref, du_ref, dwout_ref, dwp_ref, dbp_ref, dps_ref, head_ref):
        step = pl.program_id(0)
        tile = nt - 1 - step

        @pl.when(step == 0)
        def _():
            head_ref[...] = jnp.zeros_like(head_ref)
            dwout_ref[...] = jnp.zeros_like(dwout_ref)
            dwp_ref[...] = jnp.zeros_like(dwp_ref)
            dbp_ref[...] = jnp.zeros_like(dbp_ref)
            dps_ref[...] = jnp.zeros_like(dps_ref)

        dx = dx_ref[...].astype(BF16)
        dwout_ref[:ATTN_WIDTH, :] += _tn(a_ref[...], dx)
        dwout_ref[ATTN_WIDTH:, :] += _tn(p_ref[...], dx)
        dcat = _nt(dx, w_ref[...])
        dattn_ref[...] = dcat[:, :ATTN_WIDTH].astype(BF16)
        dpool = dcat[:, ATTN_WIDTH:]
        pos = lax.broadcasted_iota(jnp.int32, (tm, POOL_GROUP_WIDTH), 0) + tile * tm
        head = head_ref[...]
        n_ext = tm + POOL_HALO
        for g, size in enumerate(POOL_SIZES):
            cols = slice(g * POOL_GROUP_WIDTH, (g + 1) * POOL_GROUP_WIDTH)
            mixed_g = mix_ref[:, cols]
            pre = _nn(mixed_g, wp_ref[g]) + bp_ref[:, cols]
            dy = dpool[:, cols]
            dps_ref[:, cols] += jnp.sum(dy * pre, axis=0, keepdims=True)
            dpre = dy * ps_ref[:, cols]
            dbp_ref[:, cols] += jnp.sum(dpre, axis=0, keepdims=True)
            dpre_b = dpre.astype(BF16)
            dwp_ref[g] += _tn(mixed_g, dpre_b)
            dmixed = _nt(dpre_b, wp_ref[g])
            w = _window_mean(dmixed, pos, size)
            head_ref[:, cols] = w[:POOL_HALO, :]
            a = jnp.concatenate([w, head[:, cols]], axis=0)
            shift = 1
            while shift < size:
                a = a + pltpu.roll(a, n_ext - shift, 0)
                shift *= 2
            du_ref[:, cols] = (a[:tm, :] - dmixed).astype(BF16)

    rev = lambda w: pl.BlockSpec((tm, w), lambda i: (nt - 1 - i, 0))
    return _gridded(
        body, rider, name="bwd_outproj_pool", grid=(nt,),
        in_specs=[rev(D_MODEL), rev(ATTN_WIDTH), rev(POOL_WIDTH), rev(POOL_WIDTH), _full((D_MODEL, D_MODEL)),
                  _full((4, POOL_GROUP_WIDTH, POOL_GROUP_WIDTH)), _full((1, POOL_WIDTH)), _full((1, POOL_WIDTH))],
        out_specs=[rev(ATTN_WIDTH), rev(POOL_WIDTH), _full((D_MODEL, D_MODEL)),
                   _full((4, POOL_GROUP_WIDTH, POOL_GROUP_WIDTH)), _full((1, POOL_WIDTH)), _full((1, POOL_WIDTH))],
        out_shape=[jax.ShapeDtypeStruct((s, ATTN_WIDTH), BF16), jax.ShapeDtypeStruct((s, POOL_WIDTH), BF16),
                   jax.ShapeDtypeStruct((D_MODEL, D_MODEL), F32),
                   jax.ShapeDtypeStruct((4, POOL_GROUP_WIDTH, POOL_GROUP_WIDTH), F32),
                   jax.ShapeDtypeStruct((1, POOL_WIDTH), F32), jax.ShapeDtypeStruct((1, POOL_WIDTH), F32)],
        scratch_shapes=[pltpu.VMEM((POOL_HALO, POOL_WIDTH), F32)],
        args=(dx2, attn, pool, mixed, w_out, w_pool, b_pool, pool_scale))


def _fold_spread(t):
    low = lax.broadcasted_iota(jnp.int32, (2 * BLOCK, LANES), 1) < HEAD_DIM
    kept = jnp.where(low, t[:2 * BLOCK, :], t[2 * BLOCK:, :])
    return kept + pltpu.roll(kept, HEAD_DIM, 1)


def _attn_bwd(q, kz, vz, dattn, lse, sinks, rider=None):
    s = q.shape[0]
    tq = min(ATTN_TILE, s)
    nt = s // tq
    per = tq // BLOCK

    def body(q_ref, k_ref, kp_ref, v_ref, vp_ref, do_ref, lse_ref, sink_ref,
             dq_ref, dk_ref, dv_ref, dsink_ref, dk_acc, dv_acc, dk_carry, dv_carry):
        step = pl.program_id(0)

        @pl.when(step == 0)
        def _():
            dk_carry[...] = jnp.zeros_like(dk_carry)
            dv_carry[...] = jnp.zeros_like(dv_carry)
            dsink_ref[...] = jnp.zeros_like(dsink_ref)

        dk_acc[0:tq, :] = jnp.zeros((tq, KV_WIDTH), F32)
        dv_acc[0:tq, :] = jnp.zeros((tq, KV_WIDTH), F32)
        dk_acc[tq:, :] = dk_carry[...]
        dv_acc[tq:, :] = dv_carry[...]
        first, band = _band_masks(nt - 1 - step)
        low = lax.broadcasted_iota(jnp.int32, (2 * BLOCK, LANES), 1) < HEAD_DIM
        chains = [(b, kv) for b in range(per) for kv in range(N_KV_HEADS)]

        def operands(b, kv):
            rows = slice(b * BLOCK, (b + 1) * BLOCK)
            qab = _stack_pair(q_ref, rows, kv)
            doab = _stack_pair(do_ref, rows, kv)
            kzb = _band(k_ref, kp_ref, b, kv)
            return qab, doab, kzb, _nt(kzb, qab), _nt(_band(v_ref, vp_ref, b, kv), doab)

        folded = {}

        def finish(b, kv, dqab, dkz, dvz):
            rows = slice(b * BLOCK, (b + 1) * BLOCK)
            dq_ref[rows, kv * PAIR:kv * PAIR + LANES] = dqab[:BLOCK] * Q_SCALE
            dq_ref[rows, kv * PAIR + LANES:(kv + 1) * PAIR] = dqab[BLOCK:] * Q_SCALE
            folded[kv] = (_fold_spread(dkz), _fold_spread(dvz))
            if kv == N_KV_HEADS - 1:
                band_rows = slice(b * BLOCK, (b + 2) * BLOCK)
                dk_acc[band_rows, :] += jnp.where(low, folded[0][0], folded[1][0])
                dv_acc[band_rows, :] += jnp.where(low, folded[0][1], folded[1][1])

        ahead = operands(*chains[0])
        behind = None
        for n, (b, kv) in enumerate(chains):
            rows = slice(b * BLOCK, (b + 1) * BLOCK)
            mask = first if b == 0 else band
            qab, doab, kzb, st, dpt = ahead
            if n + 1 < len(chains):
                ahead = operands(*chains[n + 1])
            probs, dscores = [], []
            for half in range(2):
                top, bottom = _pair_heads(kv, half)
                keys = slice(half * 2 * BLOCK, (half + 1) * 2 * BLOCK)
                lse_h = _per_query(lse_ref, rows, top, bottom)
                p = jnp.where(mask[keys, :], jnp.exp(st[keys, :] - lse_h), 0.0)
                dph = dpt[keys, :]
                delta = _reduce_rows(p * dph, jnp.add, jnp.sum)
                probs.append(p.astype(BF16))
                dscores.append((p * (dph - delta)).astype(BF16))
                leak = jnp.exp(_sink_per_query(sink_ref, top, bottom) - lse_h) * delta
                dsink_ref[:, top:top + 1] -= jnp.sum(leak[:, :BLOCK], axis=1, keepdims=True)
                dsink_ref[:, bottom:bottom + 1] -= jnp.sum(leak[:, BLOCK:], axis=1, keepdims=True)
            ds = jnp.concatenate(dscores, axis=0)
            results = (_tn(ds, kzb), _nn(ds, qab), _nn(jnp.concatenate(probs, axis=0), doab))
            if behind is not None:
                finish(*behind)
            behind = (b, kv, *results)
        finish(*behind)
        dk_ref[...] = dk_acc[BLOCK:, :]
        dv_ref[...] = dv_acc[BLOCK:, :]
        dk_carry[...] = dk_acc[0:BLOCK, :]
        dv_carry[...] = dv_acc[0:BLOCK, :]

    cur = lambda w: pl.BlockSpec((tq, w), lambda i: (nt - 1 - i, 0))
    prev = pl.BlockSpec((BLOCK, KV_SPREAD), lambda i: (jnp.maximum(per * (nt - 1 - i) - 1, 0), 0))
    acc = pltpu.VMEM((tq + BLOCK, KV_WIDTH), F32)
    carry = pltpu.VMEM((BLOCK, KV_WIDTH), F32)
    return _gridded(
        body, rider, name="attn_bwd", grid=(nt,),
        in_specs=[cur(ATTN_WIDTH), cur(KV_SPREAD), prev, cur(KV_SPREAD), prev, cur(ATTN_WIDTH),
                  pl.BlockSpec((N_Q_HEADS, tq), lambda i: (0, nt - 1 - i)), _full((1, N_Q_HEADS))],
        out_specs=[cur(ATTN_WIDTH), cur(KV_WIDTH), cur(KV_WIDTH), _full((1, N_Q_HEADS))],
        out_shape=[jax.ShapeDtypeStruct((s, ATTN_WIDTH), F32), jax.ShapeDtypeStruct((s, KV_WIDTH), F32),
                   jax.ShapeDtypeStruct((s, KV_WIDTH), F32), jax.ShapeDtypeStruct((1, N_Q_HEADS), F32)],
        scratch_shapes=[acc, acc, carry, carry],
        args=(q, kz, kz, vz, vz, dattn, lse, sinks))


def _bwd_inproj(dq, dk, dv, du, cos, sin, win_t, x, g_mix, dx2):
    s = x.shape[0]
    tm = _token_tile(s)

    def body(dq_ref, dk_ref, dv_ref, du_ref, cos_ref, sin_ref, w_ref, x_ref, g_ref, dx2_ref,
             dx_ref, dw_ref, db_ref, dg_ref):
        @pl.when(pl.program_id(0) == 0)
        def _():
            dw_ref[...] = jnp.zeros_like(dw_ref)
            db_ref[...] = jnp.zeros_like(db_ref)
            dg_ref[...] = jnp.zeros_like(dg_ref)

        cos_t, sin_t = _tile_tables(cos_ref, sin_ref)
        dz32 = jnp.concatenate([_rope_bwd(dq_ref[...], cos_t, sin_t), _rope_bwd(dk_ref[...], cos_t, sin_t),
                                dv_ref[...], du_ref[...].astype(F32)], axis=1)
        db_ref[...] += jnp.sum(dz32, axis=0, keepdims=True)
        dz = dz32.astype(BF16)
        g = g_ref[...]
        n, r = _rms(x_ref[...])
        h = (n * g).astype(BF16)
        dh = _nn(dz, w_ref[...])
        for m0 in range(0, IN_WIDTH, TN_ROW_CHUNK):
            dw_ref[m0:m0 + TN_ROW_CHUNK, :] += _tn(dz[:, m0:m0 + TN_ROW_CHUNK], h)
        dx, dg = _rms_bwd(dh, n, r, g)
        dg_ref[...] += dg
        dx_ref[...] = dx2_ref[...] + dx

    return _gridded(
        body, None, name="bwd_inproj", grid=(s // tm,),
        in_specs=[_rows(tm, ATTN_WIDTH), _rows(tm, KV_WIDTH), _rows(tm, KV_WIDTH), _rows(tm, POOL_WIDTH),
                  _full((4, tm, LANES)), pl.BlockSpec((1, 2, LANES), lambda i: (i, 0, 0)), _full((IN_WIDTH, D_MODEL)), _rows(tm, D_MODEL),
                  _full((1, D_MODEL)), _rows(tm, D_MODEL)],
        out_specs=[_rows(tm, D_MODEL), _full((IN_WIDTH, D_MODEL)), _full((1, IN_WIDTH)), _full((1, D_MODEL))],
        out_shape=[jax.ShapeDtypeStruct((s, D_MODEL), F32), jax.ShapeDtypeStruct((IN_WIDTH, D_MODEL), F32),
                   jax.ShapeDtypeStruct((1, IN_WIDTH), F32), jax.ShapeDtypeStruct((1, D_MODEL), F32)],
        scratch_shapes=[], args=(dq, dk, dv, du, cos, sin, win_t, x, g_mix, dx2))


def _rope_tables(s, tm):
    inv_freq = jnp.tile(1.0 / (ROPE_THETA ** (jnp.arange(0, HEAD_DIM, 2, dtype=F32) / HEAD_DIM)), 4)
    sign = jnp.tile(jnp.repeat(jnp.array([-1.0, 1.0], F32), HEAD_DIM // 2), 2)
    within = jnp.arange(tm, dtype=F32)[:, None] * inv_freq[None, :]
    start = jnp.arange(0, s, tm, dtype=F32)[:, None] * inv_freq[None, :]
    cos, sin = jnp.cos(within), jnp.sin(within)
    base = jnp.stack([cos, sin, sign * sin, sign * cos])
    return base, jnp.stack([jnp.cos(start), jnp.sin(start)], axis=1)


def _place():
    return lax.axis_index("x"), lax.axis_index("y"), lax.axis_index("c")


def _other_chips(x, y):
    return [(1 - x, y), (x, 1 - y), (1 - x, 1 - y)]


def _gather_rider(blocks, relay_early=False):
    nm = len(blocks)

    def plan(ins, outs, sems):
        send_sems, recv_sems, local_sems = sems
        x, y, c = _place()
        me, sibling = (x, y, c), (x, y, 1 - c)
        chips = _other_chips(x, y)

        def rows(m, px, py, pc):
            r = ins[m].shape[0]
            return outs[m].at[pl.ds((4 * px + 2 * py + pc) * r, r), :]

        def copy(m, k, block, to, src=None):
            return pltpu.make_async_remote_copy(
                src_ref=rows(m, *block) if src is None else src, dst_ref=rows(m, *block),
                send_sem=send_sems.at[k * nm + m], recv_sem=recv_sems.at[k * nm + m],
                device_id=to, device_id_type=MESH)

        mine = [pltpu.make_async_copy(ins[m], rows(m, *me), local_sems.at[m]) for m in range(nm)]
        first = [copy(m, 0, me, sibling, src=ins[m]) for m in range(nm)]
        first += [copy(m, 1 + j, me, (*chip, c), src=ins[m]) for j, chip in enumerate(chips) for m in range(nm)]
        return me, sibling, chips, copy, mine, first

    def start(ins, outs, sems):
        *_, mine, first = plan(ins, outs, sems)
        for cp in mine + first:
            cp.start()

    def passed_on(ins, outs, sems):
        me, sibling, chips, copy, _, _ = plan(ins, outs, sems)
        return [copy(m, 4 + j, (*chip, me[2]), sibling) for j, chip in enumerate(chips) for m in range(nm)]

    def relay(ins, outs, sems):
        me, _, chips, copy, _, _ = plan(ins, outs, sems)
        forwards = passed_on(ins, outs, sems)
        for j, chip in enumerate(chips):
            for m in range(nm):
                copy(m, 1 + j, (*chip, me[2]), me).wait_recv()
                forwards[j * nm + m].start()

    def finish(ins, outs, sems):
        me, sibling, chips, copy, mine, first = plan(ins, outs, sems)
        for m in range(nm):
            copy(m, 0, sibling, me).wait_recv()
        for j, chip in enumerate(chips):
            for m in range(nm):
                copy(m, 4 + j, (*chip, 1 - me[2]), me).wait_recv()
        for cp in first + passed_on(ins, outs, sems):
            cp.wait_send()
        for cp in mine:
            cp.wait()

    return _Rider(
        arrays=list(blocks), out_shape=[jax.ShapeDtypeStruct((N_DEV * b.shape[0], b.shape[1]), b.dtype) for b in blocks],
        sems=[pltpu.SemaphoreType.DMA((7 * nm,)), pltpu.SemaphoreType.DMA((7 * nm,)), pltpu.SemaphoreType.DMA((nm,))],
        start=start, finish=finish, relay=relay, relay_early=relay_early)


def _exchange_rider(copies_of, arrays, out_shape, n_copies):
    def copies(ins, outs, sems):
        send_sems, recv_sems = sems
        return [pltpu.make_async_remote_copy(src_ref=src, dst_ref=dst, send_sem=send_sems.at[k], recv_sem=recv_sems.at[k],
                                             device_id=to, device_id_type=MESH)
                for k, (src, dst, to) in enumerate(copies_of(ins, outs))]

    def start(ins, outs, sems):
        for cp in copies(ins, outs, sems):
            cp.start()

    def finish(ins, outs, sems):
        cps = copies(ins, outs, sems)
        for cp in cps:
            cp.wait_recv()
        for cp in cps:
            cp.wait_send()

    return _Rider(arrays=list(arrays), out_shape=out_shape,
                  sems=[pltpu.SemaphoreType.DMA((n_copies,)), pltpu.SemaphoreType.DMA((n_copies,))], start=start, finish=finish)


def _sibling_rider(grads):
    def copies_of(ins, outs):
        x, y, c = _place()
        for g_ref, o_ref in zip(ins, outs):
            r = g_ref.shape[0] // N_DEV
            for q in range(N_CHIPS):
                yield g_ref.at[pl.ds((2 * q + 1 - c) * r, r), :], o_ref.at[pl.ds(q * r, r), :], (x, y, 1 - c)

    return _exchange_rider(copies_of, grads, [jax.ShapeDtypeStruct((g.shape[0] // 2, g.shape[1]), F32) for g in grads],
                           len(grads) * N_CHIPS)


def _chip_sum(grad, from_sibling, place):
    r = grad.shape[0] // N_DEV
    w = grad.shape[1]

    def body(place_ref, g_ref, s_ref, wire_ref, own_ref):
        total = g_ref[...] + s_ref[...]
        wire_ref[...] = total.astype(BF16)

        @pl.when(pl.program_id(0) == place_ref[1])
        def _():
            own_ref[...] = total

    grid_spec = pltpu.PrefetchScalarGridSpec(
        num_scalar_prefetch=1, grid=(N_CHIPS,),
        in_specs=[pl.BlockSpec((r, w), lambda q, p: (2 * q + p[0], 0)), pl.BlockSpec((r, w), lambda q, p: (q, 0))],
        out_specs=[pl.BlockSpec((r, w), lambda q, p: (q, 0)), pl.BlockSpec((r, w), lambda q, p: (0, 0))])
    return pl.pallas_call(
        body, name="grad_chip_sum", grid_spec=grid_spec,
        out_shape=[jax.ShapeDtypeStruct((N_CHIPS * r, w), BF16), jax.ShapeDtypeStruct((r, w), F32)],
        compiler_params=_params("arbitrary"),
    )(place, grad, from_sibling)


def _chips_rider(wires):
    def copies_of(ins, outs):
        x, y, c = _place()
        for w_ref, o_ref in zip(ins, outs):
            r = w_ref.shape[0] // N_CHIPS
            for j, (px, py) in enumerate(_other_chips(x, y)):
                yield w_ref.at[pl.ds((2 * px + py) * r, r), :], o_ref.at[pl.ds(j * r, r), :], (px, py, c)

    return _exchange_rider(copies_of, wires,
                           [jax.ShapeDtypeStruct((3 * (w.shape[0] // N_CHIPS), w.shape[1]), BF16) for w in wires], len(wires) * 3)


SEM = pl.BlockSpec(memory_space=pltpu.SEMAPHORE)
DATAFLOW = pltpu.SideEffectType.DATAFLOW_SIDE_EFFECTING


def _last_exchange_copies(wire_refs, late_ref, land_refs, land_late_ref, send_sems, recv_sems):
    x, y, c = _place()
    ends = []
    for w_ref, o_ref in zip(wire_refs, land_refs):
        r = w_ref.shape[0] // N_CHIPS
        for j, (px, py) in enumerate(_other_chips(x, y)):
            ends.append((w_ref.at[pl.ds((2 * px + py) * r, r), :], o_ref.at[pl.ds(j * r, r), :], (px, py, c)))
    rows = late_ref.shape[0]
    mine = land_late_ref.at[pl.ds((4 * x + 2 * y + c) * rows, rows), :]
    peers = [(x, y, 1 - c)] + [(px, py, pc) for px, py in _other_chips(x, y) for pc in (c, 1 - c)]
    ends += [(late_ref, mine, peer) for peer in peers]
    return [pltpu.make_async_remote_copy(src_ref=src, dst_ref=dst, send_sem=send_sems[k], recv_sem=recv_sems[k],
                                         device_id=to, device_id_type=MESH) for k, (src, dst, to) in enumerate(ends)]


N_LAST_COPIES = 2 * 3 + (N_DEV - 1)


def _last_exchange_start(wires, late):
    n = N_LAST_COPIES
    lands = [lax.empty((3 * (w.shape[0] // N_CHIPS), w.shape[1]), w.dtype) for w in wires]
    land_late = lax.empty((N_DEV * late.shape[0], late.shape[1]), late.dtype)

    def body(wout_ref, win_ref, late_ref, land_wout_ref, land_win_ref, land_late_ref, *outs):
        send_sems, recv_sems, token_ref = outs[:n], outs[n:2 * n], outs[-1]
        for cp in _last_exchange_copies([wout_ref, win_ref], late_ref, [land_wout_ref, land_win_ref], land_late_ref,
                                        send_sems, recv_sems):
            cp.start()
        token_ref[...] = jnp.zeros_like(token_ref)

    operands = [pltpu.with_memory_space_constraint(a, pltpu.HBM) for a in (*wires, late, *lands, land_late)]
    thru = [pltpu.HBM(a.shape, a.dtype) for a in operands]
    out = pl.pallas_call(
        body, name="last_exchange_start",
        out_shape=[pltpu.SemaphoreType.DMA(())] * (2 * n) + thru + [jax.ShapeDtypeStruct((8, LANES), F32)],
        in_specs=[HBM] * 6, out_specs=[SEM] * (2 * n) + [HBM] * 6 + [VMEM],
        input_output_aliases={i: 2 * n + i for i in range(6)},
        compiler_params=pltpu.CompilerParams(has_side_effects=DATAFLOW),
    )(*operands)
    return out[:2 * n], out[2 * n:2 * n + 6], out[-1]


def _last_exchange_wait(sems, buffers, after):
    n = N_LAST_COPIES

    def body(wout_ref, win_ref, late_ref, land_wout_ref, land_win_ref, land_late_ref, *rest):
        send_sems, recv_sems = rest[:n], rest[n:2 * n]
        for cp in _last_exchange_copies([wout_ref, win_ref], late_ref, [land_wout_ref, land_win_ref], land_late_ref,
                                        send_sems, recv_sems):
            cp.wait_send()
            cp.wait_recv()

    out = pl.pallas_call(
        body, name="last_exchange_wait", out_shape=[pltpu.HBM(a.shape, a.dtype) for a in buffers],
        in_specs=[HBM] * 6 + [SEM] * (2 * n) + [pl.BlockSpec(memory_space=pl.ANY)], out_specs=[HBM] * 6,
        input_output_aliases={i: i for i in range(6)},
        compiler_params=pltpu.CompilerParams(has_side_effects=DATAFLOW),
    )(*buffers, *sems, after)
    return out[3], out[4], out[5]


def _adamw_math(w, g, m, v):
    m = ADAM_B1 * m + (1.0 - ADAM_B1) * g
    v = ADAM_B2 * v + (1.0 - ADAM_B2) * jnp.square(g)
    m_hat = m / (1.0 - ADAM_B1 ** ADAM_STEP)
    v_hat = v / (1.0 - ADAM_B2 ** ADAM_STEP)
    delta = -ADAM_LR * (m_hat / (jnp.sqrt(v_hat) + ADAM_EPS) + ADAM_WD * w)
    return delta, m, v


def _reduce_adamw(own, received, w, m, v, after):
    r = own.shape[0]

    def body(own_ref, rec_ref, w_ref, m_ref, v_ref, after_ref, g_ref, d_ref, nm_ref, nv_ref):
        del after_ref
        g = ((own_ref[...] + rec_ref[0:r, :].astype(F32)) + rec_ref[r:2 * r, :].astype(F32)) + rec_ref[2 * r:, :].astype(F32)
        g_ref[...] = g
        d_ref[...], nm_ref[...], nv_ref[...] = _adamw_math(w_ref[...], g, m_ref[...], v_ref[...])

    shape = jax.ShapeDtypeStruct(own.shape, F32)
    return pl.pallas_call(
        body, name="reduce_adamw", in_specs=[VMEM] * 5 + [pl.BlockSpec(memory_space=pl.ANY)], out_specs=[VMEM] * 4,
        out_shape=[shape] * 4, compiler_params=_params(),
    )(own, received, w, m, v, after)


SMALL_WIDE = (("w_pool", 65536),)
SMALL_EARLY = (("b_pool", 512), ("pool_scale", 512), ("b_out", 1024), ("g_ffn", 1024), ("g_final", 1024), ("loss", 1024))
SMALL_LATE = (("sinks", 8), ("g_mix", 1024), ("b_in", 1280))
SMALL = SMALL_WIDE + SMALL_EARLY + SMALL_LATE


def _small_rows(size):
    return -(-size // (8 * LANES)) * 8


def _pack_small(values, entries=SMALL):
    parts = []
    for name, size in entries:
        flat = values[name].reshape(-1).astype(F32)
        parts.append(jnp.pad(flat, (0, _small_rows(size) * LANES - size)).reshape(-1, LANES))
    return jnp.concatenate(parts, axis=0)


def _unpack_small(packed, shapes):
    out, row = {}, 0
    for name, size in SMALL:
        rows = _small_rows(size)
        if name in shapes:
            out[name] = packed[row:row + rows].reshape(-1)[:size].reshape(shapes[name])
        row += rows
    return out


def _small_sum_adamw(gathered, w, m, v):
    n = len(gathered)

    def body(*refs):
        w_ref, m_ref, v_ref, g_ref, d_ref, nm_ref, nv_ref = refs[n:]

        def total(ref):
            rows = ref.shape[0] // N_DEV
            acc = ref[0:rows, :].astype(F32)
            for dev in range(1, N_DEV):
                acc = acc + ref[dev * rows:(dev + 1) * rows, :].astype(F32)
            return acc

        g = jnp.concatenate([total(ref) for ref in refs[:n]], axis=0)
        g_ref[...] = g
        d_ref[...], nm_ref[...], nv_ref[...] = _adamw_math(w_ref[...], g, m_ref[...], v_ref[...])

    shape = jax.ShapeDtypeStruct(w.shape, F32)
    return pl.pallas_call(
        body, name="small_sum_adamw", in_specs=[VMEM] * (n + 3), out_specs=[VMEM] * 4, out_shape=[shape] * 4,
        compiler_params=_params(),
    )(*gathered, w, m, v)


def kernel(x, g_mix, w_in, b_in, sinks, w_pool, b_pool, pool_scale, w_out, b_out, g_ffn, w_gate, w_up, w_down, g_final, loss_target, m_g_mix, m_w_in, m_b_in, m_sinks, m_w_pool, m_b_pool, m_pool_scale, m_w_out, m_b_out, m_g_ffn, m_w_gate, m_w_up, m_w_down, m_g_final, v_g_mix, v_w_in, v_b_in, v_sinks, v_w_pool, v_b_pool, v_pool_scale, v_w_out, v_b_out, v_g_ffn, v_w_gate, v_w_up, v_w_down, v_g_final):
    weights = dict(g_mix=g_mix, w_in=w_in, b_in=b_in, sinks=sinks, w_pool=w_pool, b_pool=b_pool, pool_scale=pool_scale,
                   w_out=w_out, b_out=b_out, g_ffn=g_ffn, w_gate=w_gate, w_up=w_up, w_down=w_down, g_final=g_final)
    mom1 = dict(g_mix=m_g_mix, w_in=m_w_in, b_in=m_b_in, sinks=m_sinks, w_pool=m_w_pool, b_pool=m_b_pool,
                pool_scale=m_pool_scale, w_out=m_w_out, b_out=m_b_out, g_ffn=m_g_ffn, w_gate=m_w_gate, w_up=m_w_up,
                w_down=m_w_down, g_final=m_g_final)
    mom2 = dict(g_mix=v_g_mix, w_in=v_w_in, b_in=v_b_in, sinks=v_sinks, w_pool=v_w_pool, b_pool=v_b_pool,
                pool_scale=v_pool_scale, w_out=v_w_out, b_out=v_b_out, g_ffn=v_g_ffn, w_gate=v_w_gate, w_up=v_w_up,
                w_down=v_w_down, g_final=v_g_final)
    order = ("g_mix", "w_in", "b_in", "sinks", "w_pool", "b_pool", "pool_scale", "w_out", "b_out", "g_ffn",
             "w_gate", "w_up", "w_down", "g_final")
    big = ("w_in", "w_out", "w_gate", "w_up", "w_down")
    transposed = ("w_in", "w_gate", "w_up")

    def row_shard(name, a):
        return a[0].T if name in transposed else a[0]

    shard = {n: row_shard(n, weights[n]).astype(BF16) for n in big}
    xs, target = x[0], loss_target[0]
    cos, sin = _rope_tables(xs.shape[0], _token_tile(xs.shape[0]))
    wp_b = w_pool[0].astype(BF16)
    bp = b_pool.reshape(1, POOL_WIDTH)
    ps = pool_scale.reshape(1, POOL_WIDTH)
    g_fin = g_final.reshape(1, D_MODEL)
    px, py, pc = _place()
    place = jnp.stack([pc, 2 * px + py]).astype(jnp.int32)

    (win_t,) = _alone(_gather_rider([shard["w_in"]]), "gather_w_in")
    q, kz, vz, vt, mixed, pool, w_out_b, wg_t = _fwd_inproj(
        xs, g_mix, win_t, b_in, cos, sin, wp_b, bp, ps,
        rider=_gather_rider([shard["w_out"], shard["w_gate"]], relay_early=True))
    attn, lse, wu_t = _attn_fwd(q, kz, vt, sinks, rider=_gather_rider([shard["w_up"]]))
    x2, gate, up, act, wd = _fwd_outproj_ffn_act(attn, pool, w_out_b, b_out, xs, g_ffn, wg_t, wu_t,
                                                 rider=_gather_rider([shard["w_down"]], relay_early=True))
    dx3, sq, dg_final, d_wd = _fwd_down_loss(act, x2, wd, g_fin, target)

    dx2, dg_ffn, db_out, d_wg_t, d_wu_t, wd_sibling = _bwd_ffn(
        dx3, gate, up, x2, wd, wg_t, wu_t, g_ffn, rider=_sibling_rider([d_wd]))
    wd_sum = _chip_sum(d_wd, wd_sibling, place)
    in_grads = [d_wg_t, d_wu_t]
    dattn, du, d_wout, d_wpool, d_bpool, d_pscale, wd_received, *in_sibling = _bwd_outproj_pool(
        dx2, attn, pool, mixed, w_out_b, wp_b, bp, ps, rider=_join(_chips_rider([wd_sum[0]]), _sibling_rider(in_grads)))
    in_sums = [_chip_sum(g, s, place) for g, s in zip(in_grads, in_sibling)]
    small_wide = _pack_small(dict(w_pool=d_wpool), SMALL_WIDE).astype(BF16)
    small_early = _pack_small(dict(b_pool=d_bpool, pool_scale=d_pscale, b_out=db_out, g_ffn=dg_ffn,
                                   g_final=dg_final, loss=sq), SMALL_EARLY)
    dq, dk, dv, d_sinks, *landed = _attn_bwd(
        q, kz, vz, dattn, lse, sinks,
        rider=_join(_chips_rider([wire for wire, _ in in_sums]), _sibling_rider([d_wout]),
                    _gather_rider([small_wide, small_early])))
    ffn_sums, ffn_received = in_sums + [wd_sum], landed[:2] + [wd_received]
    wout_sum = _chip_sum(d_wout, landed[2], place)
    gathered_wide, gathered_early = landed[3], landed[4]
    dx, d_win_t, d_bin, d_gmix = _bwd_inproj(dq, dk, dv, du, cos, sin, win_t, xs, g_mix, dx2)
    (win_sibling,) = _alone(_sibling_rider([d_win_t]), "grad_exchange_sibling")
    win_sum = _chip_sum(d_win_t, win_sibling, place)
    small_late = _pack_small(dict(sinks=d_sinks, g_mix=d_gmix, b_in=d_bin), SMALL_LATE)

    grad, delta, new_m, new_v = {}, {}, {}, {}

    def update(n, own, rec, after):
        results = _reduce_adamw(own, rec, row_shard(n, weights[n]), row_shard(n, mom1[n]), row_shard(n, mom2[n]), after)
        grad[n], delta[n], new_m[n], new_v[n] = [(a.T if n in transposed else a)[None] for a in results]
        return results[0]

    sems, in_flight, after = _last_exchange_start([wout_sum[0], win_sum[0]], small_late)
    for n, (_, own), rec in zip(("w_gate", "w_up", "w_down"), ffn_sums, ffn_received):
        after = update(n, own, rec, after)
    wout_received, win_received, gathered_late = _last_exchange_wait(sems, in_flight, after)
    gathered_late = lax.dynamic_update_slice(gathered_late, small_late, ((4 * px + 2 * py + pc) * small_late.shape[0], 0))
    update("w_out", wout_sum[1], wout_received, after)
    update("w_in", win_sum[1], win_received, after)

    shapes = {n: weights[n].shape for n in order if n not in big}
    zero_loss = jnp.zeros((1, D_MODEL), F32)
    packed = _small_sum_adamw(
        [gathered_wide, gathered_early, gathered_late], _pack_small({**weights, "loss": zero_loss}),
        _pack_small({**mom1, "loss": zero_loss}), _pack_small({**mom2, "loss": zero_loss}))
    for store, pk in zip((grad, delta, new_m, new_v), packed):
        store.update(_unpack_small(pk, shapes))
    loss_rows = _unpack_small(packed[0], {"loss": (D_MODEL,)})["loss"]
    loss = (0.5 / D_MODEL) * jnp.sum(loss_rows)

    return (loss, dx[None], *[grad[n] for n in order], *[delta[n] for n in order],
            *[new_m[n] for n in order], *[new_v[n] for n in order])
```

```python
from typing import Any, Callable, NamedTuple, Sequence

import jax
import jax.numpy as jnp
from jax import lax
from jax.experimental import pallas as pl
from jax.experimental.pallas import tpu as pltpu

D_MODEL = 1024
ATTN_WIDTH = 512
KV_WIDTH = 128
POOL_WIDTH = 512
HEAD_DIM = 64
N_Q_HEADS = 8
N_KV_HEADS = 2
GQA_GROUP = 4
BLOCK = 128
POOL_SIZES = (2, 4, 8, 16)
POOL_GROUP_WIDTH = 128
POOL_HALO = 16
IN_WIDTH = 1280
D_FF = 2816
RMS_EPS = 1e-5
ROPE_THETA = 10000.0
Q_SCALE = HEAD_DIM ** -0.5

ADAM_LR = 0.001
ADAM_B1 = 0.9
ADAM_B2 = 0.999
ADAM_EPS = 1e-08
ADAM_WD = 0.01
ADAM_STEP = 10

N_DEV = 8
N_CHIPS = 4
LANES = 128
VMEM_LIMIT_BYTES = 60 * 1024 * 1024

F32 = jnp.float32
BF16 = jnp.bfloat16
MESH = pl.DeviceIdType.MESH
HBM = pl.BlockSpec(memory_space=pltpu.HBM)
VMEM = pl.BlockSpec(memory_space=pltpu.VMEM)


def _params(*semantics):
    return pltpu.CompilerParams(dimension_semantics=semantics or None, vmem_limit_bytes=VMEM_LIMIT_BYTES)


def _nn(a, b):
    return jnp.dot(a, b, preferred_element_type=F32)


def _nt(a, b):
    return lax.dot_general(a, b, (((1,), (1,)), ((), ())), preferred_element_type=F32)


def _tn(a, b):
    return lax.dot_general(a, b, (((0,), (0,)), ((), ())), preferred_element_type=F32)


def _full(shape):
    return pl.BlockSpec(shape, lambda *_: (0,) * len(shape))


def _rows(tm, width):
    return pl.BlockSpec((tm, width), lambda i, *_: (i, 0))


def _nothing(ins, outs, sems):
    del ins, outs, sems


RELAY_STEPS_BEFORE_LAST = 2


class _Rider(NamedTuple):
    arrays: Sequence[Any]
    out_shape: Sequence[Any]
    sems: Sequence[Any]
    start: Callable[..., None]
    finish: Callable[..., None]
    relay: Callable[..., None] = _nothing
    relay_early: bool = False


def _gridded(body, rider, *, name, grid, in_specs, out_specs, out_shape, scratch_shapes, args):
    params = _params("arbitrary")
    if rider is None:
        return pl.pallas_call(body, name=name, grid=grid, in_specs=in_specs, out_specs=out_specs, out_shape=out_shape,
                              scratch_shapes=scratch_shapes, compiler_params=params)(*args)
    bounds, total = [], 0
    for n in (len(in_specs), len(rider.arrays), len(out_specs), len(rider.out_shape), len(scratch_shapes), len(rider.sems)):
        bounds.append((total, total + n))
        total += n
    last = grid[0] - 1
    relay_step = max(last - RELAY_STEPS_BEFORE_LAST, 0) if rider.relay_early else last

    def riding(*refs):
        ins, r_ins, outs, r_outs, scratch, r_sems = (refs[lo:hi] for lo, hi in bounds)

        @pl.when(pl.program_id(0) == 0)
        def _():
            rider.start(r_ins, r_outs, r_sems)

        body(*ins, *outs, *scratch)

        @pl.when(pl.program_id(0) == relay_step)
        def _():
            rider.relay(r_ins, r_outs, r_sems)

        @pl.when(pl.program_id(0) == last)
        def _():
            rider.finish(r_ins, r_outs, r_sems)

    return pl.pallas_call(
        riding, name=name, grid=grid, in_specs=list(in_specs) + [HBM] * len(rider.arrays),
        out_specs=list(out_specs) + [HBM] * len(rider.out_shape), out_shape=list(out_shape) + list(rider.out_shape),
        scratch_shapes=list(scratch_shapes) + list(rider.sems), compiler_params=params)(*args, *rider.arrays)


def _join(*riders):
    def phase(which):
        def run(ins, outs, sems):
            i = o = s = 0
            for r in riders:
                ni, no, ns = len(r.arrays), len(r.out_shape), len(r.sems)
                getattr(r, which)(ins[i:i + ni], outs[o:o + no], sems[s:s + ns])
                i, o, s = i + ni, o + no, s + ns
        return run

    return _Rider(arrays=[a for r in riders for a in r.arrays], out_shape=[a for r in riders for a in r.out_shape],
                  sems=[a for r in riders for a in r.sems], start=phase("start"), finish=phase("finish"), relay=phase("relay"),
                  relay_early=all(r.relay_early for r in riders if r.relay is not _nothing))


def _alone(rider, name):
    n_in, n_out = len(rider.arrays), len(rider.out_shape)

    def body(*refs):
        parts = refs[:n_in], refs[n_in:n_in + n_out], refs[n_in + n_out:]
        rider.start(*parts)
        rider.relay(*parts)
        rider.finish(*parts)

    return pl.pallas_call(body, name=name, in_specs=[HBM] * n_in, out_specs=[HBM] * n_out, out_shape=list(rider.out_shape),
                          scratch_shapes=list(rider.sems))(*rider.arrays)


def _rot_half(t):
    n = t.shape[1]
    lane = lax.broadcasted_iota(jnp.int32, t.shape, 1)
    return jnp.where((lane % HEAD_DIM) < HEAD_DIM // 2, pltpu.roll(t, n - HEAD_DIM // 2, 1), pltpu.roll(t, HEAD_DIM // 2, 1))


def _tile_tables(base_ref, tile_ref):
    start = tile_ref[0]
    cos_0, sin_0 = start[0:1, :], start[1:2, :]
    return base_ref[0] * cos_0 - base_ref[1] * sin_0, base_ref[2] * cos_0 + base_ref[3] * sin_0


def _rope(t, cos, sin):
    reps = t.shape[1] // LANES
    if reps > 1:
        cos, sin = jnp.tile(cos, (1, reps)), jnp.tile(sin, (1, reps))
    return t * cos + _rot_half(t) * sin


def _rope_bwd(d, cos, sin):
    reps = d.shape[1] // LANES
    if reps > 1:
        cos, sin = jnp.tile(cos, (1, reps)), jnp.tile(sin, (1, reps))
    return d * cos + _rot_half(d * sin)


KV_SPREAD = 4 * LANES


def _spread_kv(t):
    low = lax.broadcasted_iota(jnp.int32, t.shape, 1) < HEAD_DIM
    swapped = pltpu.roll(t, HEAD_DIM, 1)
    zero = jnp.zeros_like(t)
    return jnp.concatenate([jnp.where(low, t, zero), jnp.where(low, zero, swapped),
                            jnp.where(low, swapped, zero), jnp.where(low, zero, t)], axis=1)


def _rms(x):
    r = lax.rsqrt(jnp.mean(x * x, axis=-1, keepdims=True) + RMS_EPS)
    return x * r, r


def _rms_bwd(dh, n, r, g):
    dn = dh * g
    dx = r * (dn - n * jnp.mean(dn * n, axis=-1, keepdims=True))
    return dx, jnp.sum(dh * n, axis=0, keepdims=True)


def _token_tile(s):
    return min(512, s)


def _window_mean(window_sum, pos, size):
    head = window_sum[:POOL_HALO, :] / jnp.minimum(pos[:POOL_HALO, :] + 1, size).astype(F32)
    return jnp.concatenate([head, window_sum[POOL_HALO:, :] * (1.0 / size)], axis=0)


def _fwd_inproj(x, g_mix, win_t, b_in, cos, sin, w_pool, b_pool, pool_scale, rider=None):
    s = x.shape[0]
    tm = _token_tile(s)

    def body(x_ref, g_ref, w_ref, b_ref, cos_ref, sin_ref, wp_ref, bp_ref, ps_ref,
             q_ref, k_ref, v_ref, vt_ref, mix_ref, pool_ref, tail_ref):
        i = pl.program_id(0)

        @pl.when(i == 0)
        def _():
            tail_ref[...] = jnp.zeros_like(tail_ref)

        n, _ = _rms(x_ref[...])
        h = (n * g_ref[...]).astype(BF16)
        z = _nt(h, w_ref[...]) + b_ref[...]
        cos_t, sin_t = _tile_tables(cos_ref, sin_ref)
        q_ref[...] = (_rope(z[:, :ATTN_WIDTH], cos_t, sin_t) * Q_SCALE).astype(BF16)
        k_ref[...] = _spread_kv(_rope(z[:, ATTN_WIDTH:ATTN_WIDTH + KV_WIDTH], cos_t, sin_t)).astype(BF16)
        vz = _spread_kv(z[:, ATTN_WIDTH + KV_WIDTH:ATTN_WIDTH + 2 * KV_WIDTH])
        v_ref[...] = vz.astype(BF16)
        vt_ref[...] = vz.T.astype(BF16)
        u = z[:, ATTN_WIDTH + 2 * KV_WIDTH:]
        u_ext = jnp.concatenate([tail_ref[...], u], axis=0)
        tail_ref[...] = u[tm - POOL_HALO:, :]
        pos = lax.broadcasted_iota(jnp.int32, (tm, POOL_GROUP_WIDTH), 0) + i * tm
        for g, size in enumerate(POOL_SIZES):
            cols = slice(g * POOL_GROUP_WIDTH, (g + 1) * POOL_GROUP_WIDTH)
            a = u_ext[:, cols]
            shift = 1
            while shift < size:
                a = a + pltpu.roll(a, shift, 0)
                shift *= 2
            mixed = (_window_mean(a[POOL_HALO:, :], pos, size) - u[:, cols]).astype(BF16)
            pre = _nn(mixed, wp_ref[g]) + bp_ref[:, cols]
            mix_ref[:, cols] = mixed
            pool_ref[:, cols] = (pre * ps_ref[:, cols]).astype(BF16)

    bf = lambda w: jax.ShapeDtypeStruct((s, w), BF16)
    return _gridded(
        body, rider, name="fwd_inproj", grid=(s // tm,),
        in_specs=[_rows(tm, D_MODEL), _full((1, D_MODEL)), _full((IN_WIDTH, D_MODEL)), _full((1, IN_WIDTH)),
                  _full((4, tm, LANES)), pl.BlockSpec((1, 2, LANES), lambda i: (i, 0, 0)), _full((4, POOL_GROUP_WIDTH, POOL_GROUP_WIDTH)),
                  _full((1, POOL_WIDTH)), _full((1, POOL_WIDTH))],
        out_specs=[_rows(tm, ATTN_WIDTH), _rows(tm, KV_SPREAD), _rows(tm, KV_SPREAD),
                   pl.BlockSpec((KV_SPREAD, tm), lambda i: (0, i)), _rows(tm, POOL_WIDTH), _rows(tm, POOL_WIDTH)],
        out_shape=[bf(ATTN_WIDTH), bf(KV_SPREAD), bf(KV_SPREAD), jax.ShapeDtypeStruct((KV_SPREAD, s), BF16),
                   bf(POOL_WIDTH), bf(POOL_WIDTH)],
        scratch_shapes=[pltpu.VMEM((POOL_HALO, POOL_WIDTH), F32)],
        args=(x, g_mix, win_t, b_in, cos, sin, w_pool, b_pool, pool_scale))


ATTN_TILE = 1024
PAIR = 2 * LANES


def _band_masks(tile):
    j = lax.broadcasted_iota(jnp.int32, (4 * BLOCK, 2 * BLOCK), 0) % (2 * BLOCK)
    r = lax.broadcasted_iota(jnp.int32, (4 * BLOCK, 2 * BLOCK), 1) % BLOCK
    band = (j > r) & (j <= r + BLOCK)
    return band & ((tile > 0) | (j >= BLOCK)), band


def _band(cur_ref, prev_ref, b, kv):
    halves = []
    for half in range(2):
        cols = slice(kv * PAIR + half * LANES, kv * PAIR + (half + 1) * LANES)
        if b == 0:
            halves.append(jnp.concatenate([prev_ref[:, cols], cur_ref[0:BLOCK, cols]], axis=0))
        else:
            halves.append(cur_ref[(b - 1) * BLOCK:(b + 1) * BLOCK, cols])
    return jnp.concatenate(halves, axis=0)


def _stack_pair(ref, rows, kv):
    return jnp.concatenate([ref[rows, kv * PAIR:kv * PAIR + LANES], ref[rows, kv * PAIR + LANES:(kv + 1) * PAIR]], axis=0)


def _pair_heads(kv, half):
    return GQA_GROUP * kv + half, GQA_GROUP * kv + 2 + half


def _band_t(cur_ref, prev_ref, b, kv):
    halves = []
    for half in range(2):
        lanes = slice(kv * PAIR + half * LANES, kv * PAIR + (half + 1) * LANES)
        if b == 0:
            halves.append(jnp.concatenate([prev_ref[lanes, :], cur_ref[lanes, 0:BLOCK]], axis=1))
        else:
            halves.append(cur_ref[lanes, (b - 1) * BLOCK:(b + 1) * BLOCK])
    return jnp.concatenate(halves, axis=1)


def _reduce_rows(x, op, reduce):
    while x.shape[0] > 8:
        half = x.shape[0] // 2
        x = op(x[:half], x[half:])
    return reduce(x, axis=0, keepdims=True)


def _per_query(ref, rows, top, bottom):
    return jnp.concatenate([ref[top:top + 1, rows], ref[bottom:bottom + 1, rows]], axis=1)


def _sink_per_query(sink_ref, top, bottom):
    first_slab = lax.broadcasted_iota(jnp.int32, (1, 2 * BLOCK), 1) < BLOCK
    return jnp.where(first_slab, sink_ref[:, top:top + 1], sink_ref[:, bottom:bottom + 1])


def _attn_fwd(q, kz, vt, sinks, rider=None):
    s = q.shape[0]
    tq = min(ATTN_TILE, s)

    def body(q_ref, k_ref, kp_ref, vt_ref, vtp_ref, sink_ref, o_ref, lse_ref):
        first, band = _band_masks(pl.program_id(0))
        chains = [(b, kv) for b in range(tq // BLOCK) for kv in range(N_KV_HEADS)]

        def scores(b, kv):
            rows = slice(b * BLOCK, (b + 1) * BLOCK)
            return _nt(_band(k_ref, kp_ref, b, kv), _stack_pair(q_ref, rows, kv))

        def store(b, kv, ot):
            rows = slice(b * BLOCK, (b + 1) * BLOCK)
            o = ot.T.astype(BF16)
            o_ref[rows, kv * PAIR:kv * PAIR + LANES] = o[:BLOCK]
            o_ref[rows, kv * PAIR + LANES:(kv + 1) * PAIR] = o[BLOCK:]

        ahead = scores(*chains[0])
        behind = None
        for n, (b, kv) in enumerate(chains):
            rows = slice(b * BLOCK, (b + 1) * BLOCK)
            st = jnp.where(first if b == 0 else band, ahead, -jnp.inf)
            if n + 1 < len(chains):
                ahead = scores(*chains[n + 1])
            probs = []
            for half in range(2):
                top, bottom = _pair_heads(kv, half)
                sink = _sink_per_query(sink_ref, top, bottom)
                sh = st[half * 2 * BLOCK:(half + 1) * 2 * BLOCK, :]
                m = jnp.maximum(_reduce_rows(sh, jnp.maximum, jnp.max), sink)
                p = jnp.exp(sh - m)
                denom = _reduce_rows(p, jnp.add, jnp.sum) + jnp.exp(sink - m)
                probs.append((p * (1.0 / denom)).astype(BF16))
                lse = m + jnp.log(denom)
                lse_ref[top:top + 1, rows] = lse[:, :BLOCK]
                lse_ref[bottom:bottom + 1, rows] = lse[:, BLOCK:]
            ot = _nn(_band_t(vt_ref, vtp_ref, b, kv), jnp.concatenate(probs, axis=0))
            if behind is not None:
                store(*behind)
            behind = (b, kv, ot)
        store(*behind)

    per = tq // BLOCK
    cur = lambda w: pl.BlockSpec((tq, w), lambda i: (i, 0))
    prev = pl.BlockSpec((BLOCK, KV_SPREAD), lambda i: (jnp.maximum(per * i - 1, 0), 0))
    cur_t = pl.BlockSpec((KV_SPREAD, tq), lambda i: (0, i))
    prev_t = pl.BlockSpec((KV_SPREAD, BLOCK), lambda i: (0, jnp.maximum(per * i - 1, 0)))
    return _gridded(
        body, rider, name="attn_fwd", grid=(s // tq,),
        in_specs=[cur(ATTN_WIDTH), cur(KV_SPREAD), prev, cur_t, prev_t, _full((1, N_Q_HEADS))],
        out_specs=[cur(ATTN_WIDTH), pl.BlockSpec((N_Q_HEADS, tq), lambda i: (0, i))],
        out_shape=[jax.ShapeDtypeStruct((s, ATTN_WIDTH), BF16), jax.ShapeDtypeStruct((N_Q_HEADS, s), F32)],
        scratch_shapes=[], args=(q, kz, kz, vt, vt, sinks))


FF_CHUNK = 256
TN_ROW_CHUNK = 256


def _resident(shape):
    return pl.BlockSpec(shape, lambda *_: (0,) * len(shape), pipeline_mode=pl.Buffered(1))


def _flush_rows(acc_ref, out_ref, sem, rows, is_last):
    @pl.when(is_last)
    def _():
        pltpu.make_async_copy(acc_ref.at[rows, :], out_ref.at[rows, :], sem).start()


def _flush_wait(acc_ref, out_ref, sem, is_last):
    @pl.when(is_last)
    def _():
        pltpu.make_async_copy(acc_ref, out_ref, sem).wait()


def _accumulate_tn(acc_ref, a_ref, b, out_ref, sem, is_last):
    for m0 in range(0, acc_ref.shape[0], TN_ROW_CHUNK):
        rows = slice(m0, m0 + TN_ROW_CHUNK)
        acc_ref[rows, :] += _tn(a_ref[:, rows], b)
        _flush_rows(acc_ref, out_ref, sem, rows, is_last)
    _flush_wait(acc_ref, out_ref, sem, is_last)


def _fwd_outproj_ffn_act(attn, pool, w_out, b_out, x, g_ffn, wg_t, wu_t, rider=None):
    s = x.shape[0]
    tm = _token_tile(s)

    def body(a_ref, p_ref, w_ref, b_ref, x_ref, g_ref, wg_ref, wu_ref, x2_ref, gate_ref, up_ref, act_ref):
        x2 = x_ref[...] + _nn(a_ref[...], w_ref[:ATTN_WIDTH, :]) + _nn(p_ref[...], w_ref[ATTN_WIDTH:, :]) + b_ref[...]
        x2_ref[...] = x2
        n, _ = _rms(x2)
        h = (n * g_ref[...]).astype(BF16)

        def products(c0):
            return _nt(h, wg_ref[c0:c0 + FF_CHUNK, :]), _nt(h, wu_ref[c0:c0 + FF_CHUNK, :])

        ahead = products(0)
        for c0 in range(0, D_FF, FF_CHUNK):
            cols = slice(c0, c0 + FF_CHUNK)
            gate, up = ahead
            if c0 + FF_CHUNK < D_FF:
                ahead = products(c0 + FF_CHUNK)
            gate_ref[:, cols] = gate.astype(BF16)
            up_ref[:, cols] = up.astype(BF16)
            act_ref[:, cols] = (gate * jax.nn.sigmoid(gate) * up).astype(BF16)

    act_shape = jax.ShapeDtypeStruct((s, D_FF), BF16)
    return _gridded(
        body, rider, name="fwd_outproj_ffn_act", grid=(s // tm,),
        in_specs=[_rows(tm, ATTN_WIDTH), _rows(tm, POOL_WIDTH), _resident((D_MODEL, D_MODEL)), _full((1, D_MODEL)),
                  _rows(tm, D_MODEL), _full((1, D_MODEL)), _resident((D_FF, D_MODEL)), _resident((D_FF, D_MODEL))],
        out_specs=[_rows(tm, D_MODEL)] + [_rows(tm, D_FF)] * 3,
        out_shape=[jax.ShapeDtypeStruct((s, D_MODEL), F32)] + [act_shape] * 3,
        scratch_shapes=[], args=(attn, pool, w_out, b_out, x, g_ffn, wg_t, wu_t))


def _fwd_down_loss(act, x2, wd, g_final, target):
    s = x2.shape[0]
    tm = _token_tile(s)
    last = s // tm - 1

    def body(a_ref, x2_ref, wd_ref, g_ref, t_ref, dx3_ref, sq_ref, dg_ref, dwd_ref, acc_ref, sem):
        @pl.when(pl.program_id(0) == 0)
        def _():
            sq_ref[...] = jnp.zeros_like(sq_ref)
            dg_ref[...] = jnp.zeros_like(dg_ref)
            acc_ref[...] = jnp.zeros_like(acc_ref)

        halves = [slice(0, tm // 2), slice(tm // 2, tm)]
        g = g_ref[...]
        down = [_nn(a_ref[rows, :], wd_ref[...]) for rows in halves]
        for half, (rows, y) in enumerate(zip(halves, down)):
            x3 = x2_ref[rows, :] + y
            n, r = _rms(x3)
            diff = n * g - t_ref[rows, :]
            sq_ref[...] += jnp.sum(diff * diff, axis=0, keepdims=True)
            dx3, dg = _rms_bwd(diff * (1.0 / D_MODEL), n, r, g)
            dg_ref[...] += dg
            dx3_ref[rows, :] = dx3
            dx3b = dx3.astype(BF16)
            if half == 0:
                for m0 in range(0, D_FF, TN_ROW_CHUNK):
                    acc_ref[m0:m0 + TN_ROW_CHUNK, :] += _tn(a_ref[rows, m0:m0 + TN_ROW_CHUNK], dx3b)
            else:
                _accumulate_tn(acc_ref, a_ref.at[rows, :], dx3b, dwd_ref, sem, pl.program_id(0) == last)

    return pl.pallas_call(
        body, name="fwd_down_loss", grid=(s // tm,),
        in_specs=[_rows(tm, D_FF), _rows(tm, D_MODEL), _resident((D_FF, D_MODEL)), _full((1, D_MODEL)), _rows(tm, D_MODEL)],
        out_specs=[_rows(tm, D_MODEL), _full((1, D_MODEL)), _full((1, D_MODEL)), HBM],
        out_shape=[jax.ShapeDtypeStruct((s, D_MODEL), F32),
                   jax.ShapeDtypeStruct((1, D_MODEL), F32), jax.ShapeDtypeStruct((1, D_MODEL), F32),
                   jax.ShapeDtypeStruct((D_FF, D_MODEL), F32)],
        scratch_shapes=[pltpu.VMEM((D_FF, D_MODEL), F32), pltpu.SemaphoreType.DMA],
        compiler_params=_params("arbitrary"),
    )(act, x2, wd, g_final, target)


FFN_BWD_TILE = 256


def _bwd_ffn(dx3, gate, up, x2, wd, wg_t, wu_t, g_ffn, rider=None):
    s = x2.shape[0]
    tm = min(FFN_BWD_TILE, s)
    last = s // tm - 1

    def body(dx3_ref, gate_ref, up_ref, x2_ref, wd_ref, wg_ref, wu_ref, g_ref,
             dx2_ref, dg_ref, db_ref, dwg_ref, dwu_ref, dgate_ref, dup_ref, accg_ref, accu_ref, sems):
        @pl.when(pl.program_id(0) == 0)
        def _():
            dg_ref[...] = jnp.zeros_like(dg_ref)
            db_ref[...] = jnp.zeros_like(db_ref)
            accg_ref[...] = jnp.zeros_like(accg_ref)
            accu_ref[...] = jnp.zeros_like(accu_ref)

        dx3b = dx3_ref[...].astype(BF16)
        g = g_ref[...]
        n, r = _rms(x2_ref[...])
        h = (n * g).astype(BF16)
        ahead = _nt(dx3b, wd_ref[0:FF_CHUNK, :])
        for c0 in range(0, D_FF, FF_CHUNK):
            cols = slice(c0, c0 + FF_CHUNK)
            dact = ahead
            if c0 + FF_CHUNK < D_FF:
                ahead = _nt(dx3b, wd_ref[c0 + FF_CHUNK:c0 + 2 * FF_CHUNK, :])
            gate = gate_ref[:, cols].astype(F32)
            up = up_ref[:, cols].astype(F32)
            sig = jax.nn.sigmoid(gate)
            silu = gate * sig
            dup = (dact * silu).astype(BF16)
            dgate = (dact * up * (sig + silu * (1.0 - sig))).astype(BF16)
            dup_ref[:, cols] = dup
            dgate_ref[:, cols] = dgate
            accg_ref[cols, :] += _tn(dgate, h)
            accu_ref[cols, :] += _tn(dup, h)
        dh2 = _nn(dgate_ref[...], wg_ref[...]) + _nn(dup_ref[...], wu_ref[...])
        dx, dg = _rms_bwd(dh2, n, r, g)
        dx2 = dx3_ref[...] + dx
        dg_ref[...] += dg
        db_ref[...] += jnp.sum(dx2, axis=0, keepdims=True)
        dx2_ref[...] = dx2

        @pl.when(pl.program_id(0) == last)
        def _():
            outs = [pltpu.make_async_copy(accg_ref, dwg_ref, sems.at[0]), pltpu.make_async_copy(accu_ref, dwu_ref, sems.at[1])]
            for cp in outs:
                cp.start()
            for cp in outs:
                cp.wait()

    grad_shape = jax.ShapeDtypeStruct((D_FF, D_MODEL), F32)
    weight = _resident((D_FF, D_MODEL))
    return _gridded(
        body, rider, name="bwd_ffn", grid=(s // tm,),
        in_specs=[_rows(tm, D_MODEL), _rows(tm, D_FF), _rows(tm, D_FF),
                  _rows(tm, D_MODEL), weight, weight, weight, _full((1, D_MODEL))],
        out_specs=[_rows(tm, D_MODEL), _full((1, D_MODEL)), _full((1, D_MODEL)), HBM, HBM],
        out_shape=[jax.ShapeDtypeStruct((s, D_MODEL), F32),
                   jax.ShapeDtypeStruct((1, D_MODEL), F32), jax.ShapeDtypeStruct((1, D_MODEL), F32), grad_shape, grad_shape],
        scratch_shapes=[pltpu.VMEM((tm, D_FF), BF16), pltpu.VMEM((tm, D_FF), BF16),
                        pltpu.VMEM((D_FF, D_MODEL), F32), pltpu.VMEM((D_FF, D_MODEL), F32), pltpu.SemaphoreType.DMA((2,))],
        args=(dx3, gate, up, x2, wd, wg_t, wu_t, g_ffn))


def _bwd_outproj_pool(dx2, attn, pool, mixed, w_out, w_pool, b_pool, pool_scale, rider=None):
    s = dx2.shape[0]
    tm = min(2 * _token_tile(s), s)
    nt = s // tm

    def body(dx_ref, a_ref, p_ref, mix_ref, w_ref, wp_ref, bp_ref, ps_ref,
             dattn_ref, du_ref, dwout_ref, dwp_ref, dbp_ref, dps_ref, head_ref):
        step = pl.program_id(0)
        tile = nt - 1 - step

        @pl.when(step == 0)
        def _():
            head_ref[...] = jnp.zeros_like(head_ref)
            dwout_ref[...] = jnp.zeros_like(dwout_ref)
            dwp_ref[...] = jnp.zeros_like(dwp_ref)
            dbp_ref[...] = jnp.zeros_like(dbp_ref)
            dps_ref[...] = jnp.zeros_like(dps_ref)

        dx = dx_ref[...].astype(BF16)
        dwout_ref[:ATTN_WIDTH, :] += _tn(a_ref[...], dx)
        dwout_ref[ATTN_WIDTH:, :] += _tn(p_ref[...], dx)
        dcat = _nt(dx, w_ref[...])
        dattn_ref[...] = dcat[:, :ATTN_WIDTH].astype(BF16)
        dpool = dcat[:, ATTN_WIDTH:]
        pos = lax.broadcasted_iota(jnp.int32, (tm, POOL_GROUP_WIDTH), 0) + tile * tm
        head = head_ref[...]
        n_ext = tm + POOL_HALO
        for g, size in enumerate(POOL_SIZES):
            cols = slice(g * POOL_GROUP_WIDTH, (g + 1) * POOL_GROUP_WIDTH)
            mixed_g = mix_ref[:, cols]
            pre = _nn(mixed_g, wp_ref[g]) + bp_ref[:, cols]
            dy = dpool[:, cols]
            dps_ref[:, cols] += jnp.sum(dy * pre, axis=0, keepdims=True)
            dpre = dy * ps_ref[:, cols]
            dbp_ref[:, cols] += jnp.sum(dpre, axis=0, keepdims=True)
            dpre_b = dpre.astype(BF16)
            dwp_ref[g] += _tn(mixed_g, dpre_b)
            dmixed = _nt(dpre_b, wp_ref[g])
            w = _window_mean(dmixed, pos, size)
            head_ref[:, cols] = w[:POOL_HALO, :]
            a = jnp.concatenate([w, head[:, cols]], axis=0)
            shift = 1
            while shift < size:
                a = a + pltpu.roll(a, n_ext - shift, 0)
                shift *= 2
            du_ref[:, cols] = (a[:tm, :] - dmixed).astype(BF16)

    rev = lambda w: pl.BlockSpec((tm, w), lambda i: (nt - 1 - i, 0))
    return _gridded(
        body, rider, name="bwd_outproj_pool", grid=(nt,),
        in_specs=[rev(D_MODEL), rev(ATTN_WIDTH), rev(POOL_WIDTH), rev(POOL_WIDTH), _full((D_MODEL, D_MODEL)),
                  _full((4, POOL_GROUP_WIDTH, POOL_GROUP_WIDTH)), _full((1, POOL_WIDTH)), _full((1, POOL_WIDTH))],
        out_specs=[rev(ATTN_WIDTH), rev(POOL_WIDTH), _full((D_MODEL, D_MODEL)),
                   _full((4, POOL_GROUP_WIDTH, POOL_GROUP_WIDTH)), _full((1, POOL_WIDTH)), _full((1, POOL_WIDTH))],
        out_shape=[jax.ShapeDtypeStruct((s, ATTN_WIDTH), BF16), jax.ShapeDtypeStruct((s, POOL_WIDTH), BF16),
                   jax.ShapeDtypeStruct((D_MODEL, D_MODEL), F32),
                   jax.ShapeDtypeStruct((4, POOL_GROUP_WIDTH, POOL_GROUP_WIDTH), F32),
                   jax.ShapeDtypeStruct((1, POOL_WIDTH), F32), jax.ShapeDtypeStruct((1, POOL_WIDTH), F32)],
        scratch_shapes=[pltpu.VMEM((POOL_HALO, POOL_WIDTH), F32)],
        args=(dx2, attn, pool, mixed, w_out, w_pool, b_pool, pool_scale))


def _fold_spread(t):
    low = lax.broadcasted_iota(jnp.int32, (2 * BLOCK, LANES), 1) < HEAD_DIM
    kept = jnp.where(low, t[:2 * BLOCK, :], t[2 * BLOCK:, :])
    return kept + pltpu.roll(kept, HEAD_DIM, 1)


def _attn_bwd(q, kz, vz, dattn, lse, sinks, rider=None):
    s = q.shape[0]
    tq = min(ATTN_TILE, s)
    nt = s // tq
    per = tq // BLOCK

    def body(q_ref, k_ref, kp_ref, v_ref, vp_ref, do_ref, lse_ref, sink_ref,
             dq_ref, dk_ref, dv_ref, dsink_ref, dk_acc, dv_acc, dk_carry, dv_carry):
        step = pl.program_id(0)

        @pl.when(step == 0)
        def _():
            dk_carry[...] = jnp.zeros_like(dk_carry)
            dv_carry[...] = jnp.zeros_like(dv_carry)
            dsink_ref[...] = jnp.zeros_like(dsink_ref)

        dk_acc[0:tq, :] = jnp.zeros((tq, KV_WIDTH), F32)
        dv_acc[0:tq, :] = jnp.zeros((tq, KV_WIDTH), F32)
        dk_acc[tq:, :] = dk_carry[...]
        dv_acc[tq:, :] = dv_carry[...]
        first, band = _band_masks(nt - 1 - step)
        low = lax.broadcasted_iota(jnp.int32, (2 * BLOCK, LANES), 1) < HEAD_DIM
        chains = [(b, kv) for b in range(per) for kv in range(N_KV_HEADS)]

        def operands(b, kv):
            rows = slice(b * BLOCK, (b + 1) * BLOCK)
            qab = _stack_pair(q_ref, rows, kv)
            doab = _stack_pair(do_ref, rows, kv)
            kzb = _band(k_ref, kp_ref, b, kv)
            return qab, doab, kzb, _nt(kzb, qab), _nt(_band(v_ref, vp_ref, b, kv), doab)

        folded = {}

        def finish(b, kv, dqab, dkz, dvz):
            rows = slice(b * BLOCK, (b + 1) * BLOCK)
            dq_ref[rows, kv * PAIR:kv * PAIR + LANES] = dqab[:BLOCK] * Q_SCALE
            dq_ref[rows, kv * PAIR + LANES:(kv + 1) * PAIR] = dqab[BLOCK:] * Q_SCALE
            folded[kv] = (_fold_spread(dkz), _fold_spread(dvz))
            if kv == N_KV_HEADS - 1:
                band_rows = slice(b * BLOCK, (b + 2) * BLOCK)
                dk_acc[band_rows, :] += jnp.where(low, folded[0][0], folded[1][0])
                dv_acc[band_rows, :] += jnp.where(low, folded[0][1], folded[1][1])

        ahead = operands(*chains[0])
        behind = None
        for n, (b, kv) in enumerate(chains):
            rows = slice(b * BLOCK, (b + 1) * BLOCK)
            mask = first if b == 0 else band
            qab, doab, kzb, st, dpt = ahead
            if n + 1 < len(chains):
                ahead = operands(*chains[n + 1])
            probs, dscores = [], []
            for half in range(2):
                top, bottom = _pair_heads(kv, half)
                keys = slice(half * 2 * BLOCK, (half + 1) * 2 * BLOCK)
                lse_h = _per_query(lse_ref, rows, top, bottom)
                p = jnp.where(mask[keys, :], jnp.exp(st[keys, :] - lse_h), 0.0)
                dph = dpt[keys, :]
                delta = _reduce_rows(p * dph, jnp.add, jnp.sum)
                probs.append(p.astype(BF16))
                dscores.append((p * (dph - delta)).astype(BF16))
                leak = jnp.exp(_sink_per_query(sink_ref, top, bottom) - lse_h) * delta
                dsink_ref[:, top:top + 1] -= jnp.sum(leak[:, :BLOCK], axis=1, keepdims=True)
                dsink_ref[:, bottom:bottom + 1] -= jnp.sum(leak[:, BLOCK:], axis=1, keepdims=True)
            ds = jnp.concatenate(dscores, axis=0)
            results = (_tn(ds, kzb), _nn(ds, qab), _nn(jnp.concatenate(probs, axis=0), doab))
            if behind is not None:
                finish(*behind)
            behind = (b, kv, *results)
        finish(*behind)
        dk_ref[...] = dk_acc[BLOCK:, :]
        dv_ref[...] = dv_acc[BLOCK:, :]
        dk_carry[...] = dk_acc[0:BLOCK, :]
        dv_carry[...] = dv_acc[0:BLOCK, :]

    cur = lambda w: pl.BlockSpec((tq, w), lambda i: (nt - 1 - i, 0))
    prev = pl.BlockSpec((BLOCK, KV_SPREAD), lambda i: (jnp.maximum(per * (nt - 1 - i) - 1, 0), 0))
    acc = pltpu.VMEM((tq + BLOCK, KV_WIDTH), F32)
    carry = pltpu.VMEM((BLOCK, KV_WIDTH), F32)
    return _gridded(
        body, rider, name="attn_bwd", grid=(nt,),
        in_specs=[cur(ATTN_WIDTH), cur(KV_SPREAD), prev, cur(KV_SPREAD), prev, cur(ATTN_WIDTH),
                  pl.BlockSpec((N_Q_HEADS, tq), lambda i: (0, nt - 1 - i)), _full((1, N_Q_HEADS))],
        out_specs=[cur(ATTN_WIDTH), cur(KV_WIDTH), cur(KV_WIDTH), _full((1, N_Q_HEADS))],
        out_shape=[jax.ShapeDtypeStruct((s, ATTN_WIDTH), F32), jax.ShapeDtypeStruct((s, KV_WIDTH), F32),
                   jax.ShapeDtypeStruct((s, KV_WIDTH), F32), jax.ShapeDtypeStruct((1, N_Q_HEADS), F32)],
        scratch_shapes=[acc, acc, carry, carry],
        args=(q, kz, kz, vz, vz, dattn, lse, sinks))


def _bwd_inproj(dq, dk, dv, du, cos, sin, win_t, x, g_mix, dx2):
    s = x.shape[0]
    tm = _token_tile(s)

    def body(dq_ref, dk_ref, dv_ref, du_ref, cos_ref, sin_ref, w_ref, x_ref, g_ref, dx2_ref,
             dx_ref, dw_ref, db_ref, dg_ref):
        @pl.when(pl.program_id(0) == 0)
        def _():
            dw_ref[...] = jnp.zeros_like(dw_ref)
            db_ref[...] = jnp.zeros_like(db_ref)
            dg_ref[...] = jnp.zeros_like(dg_ref)

        cos_t, sin_t = _tile_tables(cos_ref, sin_ref)
        g = g_ref[...]
        staged = []
        for rows in (slice(0, tm // 2), slice(tm // 2, tm)):
            dz32 = jnp.concatenate([_rope_bwd(dq_ref[rows, :], cos_t[rows, :], sin_t[rows, :]),
                                    _rope_bwd(dk_ref[rows, :], cos_t[rows, :], sin_t[rows, :]),
                                    dv_ref[rows, :], du_ref[rows, :].astype(F32)], axis=1)
            db_ref[...] += jnp.sum(dz32, axis=0, keepdims=True)
            dz = dz32.astype(BF16)
            n, r = _rms(x_ref[rows, :])
            staged.append((rows, dz, (n * g).astype(BF16), n, r, _nn(dz, w_ref[...])))
        dz = jnp.concatenate([part[1] for part in staged], axis=0)
        h = jnp.concatenate([part[2] for part in staged], axis=0)
        for m0 in range(0, IN_WIDTH, TN_ROW_CHUNK):
            dw_ref[m0:m0 + TN_ROW_CHUNK, :] += _tn(dz[:, m0:m0 + TN_ROW_CHUNK], h)
        for rows, _, _, n, r, dh in staged:
            dx, dg = _rms_bwd(dh, n, r, g)
            dg_ref[...] += dg
            dx_ref[rows, :] = dx2_ref[rows, :] + dx

    return _gridded(
        body, None, name="bwd_inproj", grid=(s // tm,),
        in_specs=[_rows(tm, ATTN_WIDTH), _rows(tm, KV_WIDTH), _rows(tm, KV_WIDTH), _rows(tm, POOL_WIDTH),
                  _full((4, tm, LANES)), pl.BlockSpec((1, 2, LANES), lambda i: (i, 0, 0)), _full((IN_WIDTH, D_MODEL)), _rows(tm, D_MODEL),
                  _full((1, D_MODEL)), _rows(tm, D_MODEL)],
        out_specs=[_rows(tm, D_MODEL), _full((IN_WIDTH, D_MODEL)), _full((1, IN_WIDTH)), _full((1, D_MODEL))],
        out_shape=[jax.ShapeDtypeStruct((s, D_MODEL), F32), jax.ShapeDtypeStruct((IN_WIDTH, D_MODEL), F32),
                   jax.ShapeDtypeStruct((1, IN_WIDTH), F32), jax.ShapeDtypeStruct((1, D_MODEL), F32)],
        scratch_shapes=[], args=(dq, dk, dv, du, cos, sin, win_t, x, g_mix, dx2))


def _rope_tables(s, tm):
    inv_freq = jnp.tile(1.0 / (ROPE_THETA ** (jnp.arange(0, HEAD_DIM, 2, dtype=F32) / HEAD_DIM)), 4)
    sign = jnp.tile(jnp.repeat(jnp.array([-1.0, 1.0], F32), HEAD_DIM // 2), 2)
    within = jnp.arange(tm, dtype=F32)[:, None] * inv_freq[None, :]
    start = jnp.arange(0, s, tm, dtype=F32)[:, None] * inv_freq[None, :]
    cos, sin = jnp.cos(within), jnp.sin(within)
    base = jnp.stack([cos, sin, sign * sin, sign * cos])
    return base, jnp.stack([jnp.cos(start), jnp.sin(start)], axis=1)


def _place():
    return lax.axis_index("x"), lax.axis_index("y"), lax.axis_index("c")


def _other_chips(x, y):
    return [(1 - x, y), (x, 1 - y), (1 - x, 1 - y)]


def _gather_rider(blocks, relay_early=False):
    nm = len(blocks)

    def plan(ins, outs, sems):
        send_sems, recv_sems, local_sems = sems
        x, y, c = _place()
        me, sibling = (x, y, c), (x, y, 1 - c)
        chips = _other_chips(x, y)

        def rows(m, px, py, pc):
            r = ins[m].shape[0]
            return outs[m].at[pl.ds((4 * px + 2 * py + pc) * r, r), :]

        def copy(m, k, block, to, src=None):
            return pltpu.make_async_remote_copy(
                src_ref=rows(m, *block) if src is None else src, dst_ref=rows(m, *block),
                send_sem=send_sems.at[k * nm + m], recv_sem=recv_sems.at[k * nm + m],
                device_id=to, device_id_type=MESH)

        mine = [pltpu.make_async_copy(ins[m], rows(m, *me), local_sems.at[m]) for m in range(nm)]
        first = [copy(m, 0, me, sibling, src=ins[m]) for m in range(nm)]
        first += [copy(m, 1 + j, me, (*chip, c), src=ins[m]) for j, chip in enumerate(chips) for m in range(nm)]
        return me, sibling, chips, copy, mine, first

    def start(ins, outs, sems):
        *_, mine, first = plan(ins, outs, sems)
        for cp in mine + first:
            cp.start()

    def passed_on(ins, outs, sems):
        me, sibling, chips, copy, _, _ = plan(ins, outs, sems)
        return [copy(m, 4 + j, (*chip, me[2]), sibling) for j, chip in enumerate(chips) for m in range(nm)]

    def relay(ins, outs, sems):
        me, _, chips, copy, _, _ = plan(ins, outs, sems)
        forwards = passed_on(ins, outs, sems)
        for j, chip in enumerate(chips):
            for m in range(nm):
                copy(m, 1 + j, (*chip, me[2]), me).wait_recv()
                forwards[j * nm + m].start()

    def finish(ins, outs, sems):
        me, sibling, chips, copy, mine, first = plan(ins, outs, sems)
        for m in range(nm):
            copy(m, 0, sibling, me).wait_recv()
        for j, chip in enumerate(chips):
            for m in range(nm):
                copy(m, 4 + j, (*chip, 1 - me[2]), me).wait_recv()
        for cp in first + passed_on(ins, outs, sems):
            cp.wait_send()
        for cp in mine:
            cp.wait()

    return _Rider(
        arrays=list(blocks), out_shape=[jax.ShapeDtypeStruct((N_DEV * b.shape[0], b.shape[1]), b.dtype) for b in blocks],
        sems=[pltpu.SemaphoreType.DMA((7 * nm,)), pltpu.SemaphoreType.DMA((7 * nm,)), pltpu.SemaphoreType.DMA((nm,))],
        start=start, finish=finish, relay=relay, relay_early=relay_early)


def _exchange_rider(copies_of, arrays, out_shape, n_copies):
    def copies(ins, outs, sems):
        send_sems, recv_sems = sems
        return [pltpu.make_async_remote_copy(src_ref=src, dst_ref=dst, send_sem=send_sems.at[k], recv_sem=recv_sems.at[k],
                                             device_id=to, device_id_type=MESH)
                for k, (src, dst, to) in enumerate(copies_of(ins, outs))]

    def start(ins, outs, sems):
        for cp in copies(ins, outs, sems):
            cp.start()

    def finish(ins, outs, sems):
        cps = copies(ins, outs, sems)
        for cp in cps:
            cp.wait_recv()
        for cp in cps:
            cp.wait_send()

    return _Rider(arrays=list(arrays), out_shape=out_shape,
                  sems=[pltpu.SemaphoreType.DMA((n_copies,)), pltpu.SemaphoreType.DMA((n_copies,))], start=start, finish=finish)


def _sibling_rider(grads):
    def copies_of(ins, outs):
        x, y, c = _place()
        for g_ref, o_ref in zip(ins, outs):
            r = g_ref.shape[0] // N_DEV
            for q in range(N_CHIPS):
                yield g_ref.at[pl.ds((2 * q + 1 - c) * r, r), :], o_ref.at[pl.ds(q * r, r), :], (x, y, 1 - c)

    return _exchange_rider(copies_of, grads, [jax.ShapeDtypeStruct((g.shape[0] // 2, g.shape[1]), F32) for g in grads],
                           len(grads) * N_CHIPS)


def _chip_sum(grad, from_sibling, place):
    r = grad.shape[0] // N_DEV
    w = grad.shape[1]

    def body(place_ref, g_ref, s_ref, wire_ref, own_ref):
        total = g_ref[...] + s_ref[...]
        wire_ref[...] = total.astype(BF16)

        @pl.when(pl.program_id(0) == place_ref[1])
        def _():
            own_ref[...] = total

    grid_spec = pltpu.PrefetchScalarGridSpec(
        num_scalar_prefetch=1, grid=(N_CHIPS,),
        in_specs=[pl.BlockSpec((r, w), lambda q, p: (2 * q + p[0], 0)), pl.BlockSpec((r, w), lambda q, p: (q, 0))],
        out_specs=[pl.BlockSpec((r, w), lambda q, p: (q, 0)), pl.BlockSpec((r, w), lambda q, p: (0, 0))])
    return pl.pallas_call(
        body, name="grad_chip_sum", grid_spec=grid_spec,
        out_shape=[jax.ShapeDtypeStruct((N_CHIPS * r, w), BF16), jax.ShapeDtypeStruct((r, w), F32)],
        compiler_params=_params("arbitrary"),
    )(place, grad, from_sibling)


def _chips_rider(wires):
    def copies_of(ins, outs):
        x, y, c = _place()
        for w_ref, o_ref in zip(ins, outs):
            r = w_ref.shape[0] // N_CHIPS
            for j, (px, py) in enumerate(_other_chips(x, y)):
                yield w_ref.at[pl.ds((2 * px + py) * r, r), :], o_ref.at[pl.ds(j * r, r), :], (px, py, c)

    return _exchange_rider(copies_of, wires,
                           [jax.ShapeDtypeStruct((3 * (w.shape[0] // N_CHIPS), w.shape[1]), BF16) for w in wires], len(wires) * 3)


SEM = pl.BlockSpec(memory_space=pltpu.SEMAPHORE)
DATAFLOW = pltpu.SideEffectType.DATAFLOW_SIDE_EFFECTING


def _last_exchange_copies(wire_refs, late_ref, land_refs, land_late_ref, send_sems, recv_sems):
    x, y, c = _place()
    ends = []
    for w_ref, o_ref in zip(wire_refs, land_refs):
        r = w_ref.shape[0] // N_CHIPS
        for j, (px, py) in enumerate(_other_chips(x, y)):
            ends.append((w_ref.at[pl.ds((2 * px + py) * r, r), :], o_ref.at[pl.ds(j * r, r), :], (px, py, c)))
    rows = late_ref.shape[0]
    mine = land_late_ref.at[pl.ds((4 * x + 2 * y + c) * rows, rows), :]
    peers = [(x, y, 1 - c)] + [(px, py, pc) for px, py in _other_chips(x, y) for pc in (c, 1 - c)]
    ends += [(late_ref, mine, peer) for peer in peers]
    return [pltpu.make_async_remote_copy(src_ref=src, dst_ref=dst, send_sem=send_sems[k], recv_sem=recv_sems[k],
                                         device_id=to, device_id_type=MESH) for k, (src, dst, to) in enumerate(ends)]


N_LAST_COPIES = 2 * 3 + (N_DEV - 1)


def _last_exchange_start(wires, late):
    n = N_LAST_COPIES
    lands = [lax.empty((3 * (w.shape[0] // N_CHIPS), w.shape[1]), w.dtype) for w in wires]
    land_late = lax.empty((N_DEV * late.shape[0], late.shape[1]), late.dtype)

    def body(wout_ref, win_ref, late_ref, land_wout_ref, land_win_ref, land_late_ref, *outs):
        send_sems, recv_sems, token_ref = outs[:n], outs[n:2 * n], outs[-1]
        for cp in _last_exchange_copies([wout_ref, win_ref], late_ref, [land_wout_ref, land_win_ref], land_late_ref,
                                        send_sems, recv_sems):
            cp.start()
        token_ref[...] = jnp.zeros_like(token_ref)

    operands = [pltpu.with_memory_space_constraint(a, pltpu.HBM) for a in (*wires, late, *lands, land_late)]
    thru = [pltpu.HBM(a.shape, a.dtype) for a in operands]
    out = pl.pallas_call(
        body, name="last_exchange_start",
        out_shape=[pltpu.SemaphoreType.DMA(())] * (2 * n) + thru + [jax.ShapeDtypeStruct((8, LANES), F32)],
        in_specs=[HBM] * 6, out_specs=[SEM] * (2 * n) + [HBM] * 6 + [VMEM],
        input_output_aliases={i: 2 * n + i for i in range(6)},
        compiler_params=pltpu.CompilerParams(has_side_effects=DATAFLOW),
    )(*operands)
    return out[:2 * n], out[2 * n:2 * n + 6], out[-1]


def _last_exchange_wait(sems, buffers, after):
    n = N_LAST_COPIES

    def body(wout_ref, win_ref, late_ref, land_wout_ref, land_win_ref, land_late_ref, *rest):
        send_sems, recv_sems = rest[:n], rest[n:2 * n]
        for cp in _last_exchange_copies([wout_ref, win_ref], late_ref, [land_wout_ref, land_win_ref], land_late_ref,
                                        send_sems, recv_sems):
            cp.wait_send()
            cp.wait_recv()

    out = pl.pallas_call(
        body, name="last_exchange_wait", out_shape=[pltpu.HBM(a.shape, a.dtype) for a in buffers],
        in_specs=[HBM] * 6 + [SEM] * (2 * n) + [pl.BlockSpec(memory_space=pl.ANY)], out_specs=[HBM] * 6,
        input_output_aliases={i: i for i in range(6)},
        compiler_params=pltpu.CompilerParams(has_side_effects=DATAFLOW),
    )(*buffers, *sems, after)
    return out[3], out[4], out[5]


def _adamw_math(w, g, m, v):
    m = ADAM_B1 * m + (1.0 - ADAM_B1) * g
    v = ADAM_B2 * v + (1.0 - ADAM_B2) * jnp.square(g)
    m_hat = m / (1.0 - ADAM_B1 ** ADAM_STEP)
    v_hat = v / (1.0 - ADAM_B2 ** ADAM_STEP)
    delta = -ADAM_LR * (m_hat / (jnp.sqrt(v_hat) + ADAM_EPS) + ADAM_WD * w)
    return delta, m, v


def _reduce_adamw(own, received, w, m, v, after):
    r = own.shape[0]

    def body(own_ref, rec_ref, w_ref, m_ref, v_ref, after_ref, g_ref, d_ref, nm_ref, nv_ref):
        del after_ref
        g = ((own_ref[...] + rec_ref[0:r, :].astype(F32)) + rec_ref[r:2 * r, :].astype(F32)) + rec_ref[2 * r:, :].astype(F32)
        g_ref[...] = g
        d_ref[...], nm_ref[...], nv_ref[...] = _adamw_math(w_ref[...], g, m_ref[...], v_ref[...])

    shape = jax.ShapeDtypeStruct(own.shape, F32)
    return pl.pallas_call(
        body, name="reduce_adamw", in_specs=[VMEM] * 5 + [pl.BlockSpec(memory_space=pl.ANY)], out_specs=[VMEM] * 4,
        out_shape=[shape] * 4, compiler_params=_params(),
    )(own, received, w, m, v, after)


SMALL_WIDE = (("w_pool", 65536),)
SMALL_EARLY = (("b_pool", 512), ("pool_scale", 512), ("b_out", 1024), ("g_ffn", 1024), ("g_final", 1024), ("loss", 1024))
SMALL_LATE = (("sinks", 8), ("g_mix", 1024), ("b_in", 1280))
SMALL = SMALL_WIDE + SMALL_EARLY + SMALL_LATE


def _small_rows(size):
    return -(-size // (8 * LANES)) * 8


def _pack_small(values, entries=SMALL):
    parts = []
    for name, size in entries:
        flat = values[name].reshape(-1).astype(F32)
        parts.append(jnp.pad(flat, (0, _small_rows(size) * LANES - size)).reshape(-1, LANES))
    return jnp.concatenate(parts, axis=0)


def _unpack_small(packed, shapes):
    out, row = {}, 0
    for name, size in SMALL:
        rows = _small_rows(size)
        if name in shapes:
            out[name] = packed[row:row + rows].reshape(-1)[:size].reshape(shapes[name])
        row += rows
    return out


def _small_sum_adamw(gathered, w, m, v):
    n = len(gathered)

    def body(*refs):
        w_ref, m_ref, v_ref, g_ref, d_ref, nm_ref, nv_ref = refs[n:]

        def total(ref):
            rows = ref.shape[0] // N_DEV
            acc = ref[0:rows, :].astype(F32)
            for dev in range(1, N_DEV):
                acc = acc + ref[dev * rows:(dev + 1) * rows, :].astype(F32)
            return acc

        g = jnp.concatenate([total(ref) for ref in refs[:n]], axis=0)
        g_ref[...] = g
        d_ref[...], nm_ref[...], nv_ref[...] = _adamw_math(w_ref[...], g, m_ref[...], v_ref[...])

    shape = jax.ShapeDtypeStruct(w.shape, F32)
    return pl.pallas_call(
        body, name="small_sum_adamw", in_specs=[VMEM] * (n + 3), out_specs=[VMEM] * 4, out_shape=[shape] * 4,
        compiler_params=_params(),
    )(*gathered, w, m, v)


def kernel(x, g_mix, w_in, b_in, sinks, w_pool, b_pool, pool_scale, w_out, b_out, g_ffn, w_gate, w_up, w_down, g_final, loss_target, m_g_mix, m_w_in, m_b_in, m_sinks, m_w_pool, m_b_pool, m_pool_scale, m_w_out, m_b_out, m_g_ffn, m_w_gate, m_w_up, m_w_down, m_g_final, v_g_mix, v_w_in, v_b_in, v_sinks, v_w_pool, v_b_pool, v_pool_scale, v_w_out, v_b_out, v_g_ffn, v_w_gate, v_w_up, v_w_down, v_g_final):
    weights = dict(g_mix=g_mix, w_in=w_in, b_in=b_in, sinks=sinks, w_pool=w_pool, b_pool=b_pool, pool_scale=pool_scale,
                   w_out=w_out, b_out=b_out, g_ffn=g_ffn, w_gate=w_gate, w_up=w_up, w_down=w_down, g_final=g_final)
    mom1 = dict(g_mix=m_g_mix, w_in=m_w_in, b_in=m_b_in, sinks=m_sinks, w_pool=m_w_pool, b_pool=m_b_pool,
                pool_scale=m_pool_scale, w_out=m_w_out, b_out=m_b_out, g_ffn=m_g_ffn, w_gate=m_w_gate, w_up=m_w_up,
                w_down=m_w_down, g_final=m_g_final)
    mom2 = dict(g_mix=v_g_mix, w_in=v_w_in, b_in=v_b_in, sinks=v_sinks, w_pool=v_w_pool, b_pool=v_b_pool,
                pool_scale=v_pool_scale, w_out=v_w_out, b_out=v_b_out, g_ffn=v_g_ffn, w_gate=v_w_gate, w_up=v_w_up,
                w_down=v_w_down, g_final=v_g_final)
    order = ("g_mix", "w_in", "b_in", "sinks", "w_pool", "b_pool", "pool_scale", "w_out", "b_out", "g_ffn",
             "w_gate", "w_up", "w_down", "g_final")
    big = ("w_in", "w_out", "w_gate", "w_up", "w_down")
    transposed = ("w_in", "w_gate", "w_up")

    def row_shard(name, a):
        return a[0].T if name in transposed else a[0]

    shard = {n: row_shard(n, weights[n]).astype(BF16) for n in big}
    xs, target = x[0], loss_target[0]
    cos, sin = _rope_tables(xs.shape[0], _token_tile(xs.shape[0]))
    wp_b = w_pool[0].astype(BF16)
    bp = b_pool.reshape(1, POOL_WIDTH)
    ps = pool_scale.reshape(1, POOL_WIDTH)
    g_fin = g_final.reshape(1, D_MODEL)
    px, py, pc = _place()
    place = jnp.stack([pc, 2 * px + py]).astype(jnp.int32)

    (win_t,) = _alone(_gather_rider([shard["w_in"]]), "gather_w_in")
    q, kz, vz, vt, mixed, pool, w_out_b, wg_t = _fwd_inproj(
        xs, g_mix, win_t, b_in, cos, sin, wp_b, bp, ps,
        rider=_gather_rider([shard["w_out"], shard["w_gate"]], relay_early=True))
    attn, lse, wu_t = _attn_fwd(q, kz, vt, sinks, rider=_gather_rider([shard["w_up"]]))
    x2, gate, up, act, wd = _fwd_outproj_ffn_act(attn, pool, w_out_b, b_out, xs, g_ffn, wg_t, wu_t,
                                                 rider=_gather_rider([shard["w_down"]], relay_early=True))
    dx3, sq, dg_final, d_wd = _fwd_down_loss(act, x2, wd, g_fin, target)

    dx2, dg_ffn, db_out, d_wg_t, d_wu_t, wd_sibling = _bwd_ffn(
        dx3, gate, up, x2, wd, wg_t, wu_t, g_ffn, rider=_sibling_rider([d_wd]))
    wd_sum = _chip_sum(d_wd, wd_sibling, place)
    in_grads = [d_wg_t, d_wu_t]
    dattn, du, d_wout, d_wpool, d_bpool, d_pscale, wd_received, *in_sibling = _bwd_outproj_pool(
        dx2, attn, pool, mixed, w_out_b, wp_b, bp, ps, rider=_join(_chips_rider([wd_sum[0]]), _sibling_rider(in_grads)))
    in_sums = [_chip_sum(g, s, place) for g, s in zip(in_grads, in_sibling)]
    small_wide = _pack_small(dict(w_pool=d_wpool), SMALL_WIDE).astype(BF16)
    small_early = _pack_small(dict(b_pool=d_bpool, pool_scale=d_pscale, b_out=db_out, g_ffn=dg_ffn,
                                   g_final=dg_final, loss=sq), SMALL_EARLY)
    dq, dk, dv, d_sinks, *landed = _attn_bwd(
        q, kz, vz, dattn, lse, sinks,
        rider=_join(_chips_rider([wire for wire, _ in in_sums]), _sibling_rider([d_wout]),
                    _gather_rider([small_wide, small_early])))
    ffn_sums, ffn_received = in_sums + [wd_sum], landed[:2] + [wd_received]
    wout_sum = _chip_sum(d_wout, landed[2], place)
    gathered_wide, gathered_early = landed[3], landed[4]
    dx, d_win_t, d_bin, d_gmix = _bwd_inproj(dq, dk, dv, du, cos, sin, win_t, xs, g_mix, dx2)
    (win_sibling,) = _alone(_sibling_rider([d_win_t]), "grad_exchange_sibling")
    win_sum = _chip_sum(d_win_t, win_sibling, place)
    small_late = _pack_small(dict(sinks=d_sinks, g_mix=d_gmix, b_in=d_bin), SMALL_LATE)

    grad, delta, new_m, new_v = {}, {}, {}, {}

    def update(n, own, rec, after):
        results = _reduce_adamw(own, rec, row_shard(n, weights[n]), row_shard(n, mom1[n]), row_shard(n, mom2[n]), after)
        grad[n], delta[n], new_m[n], new_v[n] = [(a.T if n in transposed else a)[None] for a in results]
        return results[0]

    sems, in_flight, after = _last_exchange_start([wout_sum[0], win_sum[0]], small_late)
    for n, (_, own), rec in zip(("w_gate", "w_up", "w_down"), ffn_sums, ffn_received):
        after = update(n, own, rec, after)
    wout_received, win_received, gathered_late = _last_exchange_wait(sems, in_flight, after)
    gathered_late = lax.dynamic_update_slice(gathered_late, small_late, ((4 * px + 2 * py + pc) * small_late.shape[0], 0))
    update("w_out", wout_sum[1], wout_received, after)
    update("w_in", win_sum[1], win_received, after)

    shapes = {n: weights[n].shape for n in order if n not in big}
    zero_loss = jnp.zeros((1, D_MODEL), F32)
    packed = _small_sum_adamw(
        [gathered_wide, gathered_early, gathered_late], _pack_small({**weights, "loss": zero_loss}),
        _pack_small({**mom1, "loss": zero_loss}), _pack_small({**mom2, "loss": zero_loss}))
    for store, pk in zip((grad, delta, new_m, new_v), packed):
        store.update(_unpack_small(pk, shapes))
    loss_rows = _unpack_small(packed[0], {"loss": (D_MODEL,)})["loss"]
    loss = (0.5 / D_MODEL) * jnp.sum(loss_rows)

    return (loss, dx[None], *[grad[n] for n in order], *[delta[n] for n in order],
            *[new_m[n] for n in order], *[new_v[n] for n in order])
```

```python
from typing import Any, Callable, NamedTuple, Sequence

import jax
import jax.numpy as jnp
from jax import lax
from jax.experimental import pallas as pl
from jax.experimental.pallas import tpu as pltpu

D_MODEL = 1024
ATTN_WIDTH = 512
KV_WIDTH = 128
POOL_WIDTH = 512
HEAD_DIM = 64
N_Q_HEADS = 8
N_KV_HEADS = 2
GQA_GROUP = 4
BLOCK = 128
POOL_SIZES = (2, 4, 8, 16)
POOL_GROUP_WIDTH = 128
POOL_HALO = 16
IN_WIDTH = 1280
D_FF = 2816
RMS_EPS = 1e-5
ROPE_THETA = 10000.0
Q_SCALE = HEAD_DIM ** -0.5

ADAM_LR = 0.001
ADAM_B1 = 0.9
ADAM_B2 = 0.999
ADAM_EPS = 1e-08
ADAM_WD = 0.01
ADAM_STEP = 10

N_DEV = 8
N_CHIPS = 4
LANES = 128
VMEM_LIMIT_BYTES = 60 * 1024 * 1024

F32 = jnp.float32
BF16 = jnp.bfloat16
MESH = pl.DeviceIdType.MESH
HBM = pl.BlockSpec(memory_space=pltpu.HBM)
VMEM = pl.BlockSpec(memory_space=pltpu.VMEM)


def _params(*semantics):
    return pltpu.CompilerParams(dimension_semantics=semantics or None, vmem_limit_bytes=VMEM_LIMIT_BYTES)


def _nn(a, b):
    return jnp.dot(a, b, preferred_element_type=F32)


def _nt(a, b):
    return lax.dot_general(a, b, (((1,), (1,)), ((), ())), preferred_element_type=F32)


def _tn(a, b):
    return lax.dot_general(a, b, (((0,), (0,)), ((), ())), preferred_element_type=F32)


def _full(shape):
    return pl.BlockSpec(shape, lambda *_: (0,) * len(shape))


def _rows(tm, width):
    return pl.BlockSpec((tm, width), lambda i, *_: (i, 0))


def _nothing(ins, outs, sems):
    del ins, outs, sems


RELAY_STEPS_BEFORE_LAST = 2


class _Rider(NamedTuple):
    arrays: Sequence[Any]
    out_shape: Sequence[Any]
    sems: Sequence[Any]
    start: Callable[..., None]
    finish: Callable[..., None]
    relay: Callable[..., None] = _nothing
    relay_early: bool = False


def _gridded(body, rider, *, name, grid, in_specs, out_specs, out_shape, scratch_shapes, args):
    params = _params("arbitrary")
    if rider is None:
        return pl.pallas_call(body, name=name, grid=grid, in_specs=in_specs, out_specs=out_specs, out_shape=out_shape,
                              scratch_shapes=scratch_shapes, compiler_params=params)(*args)
    bounds, total = [], 0
    for n in (len(in_specs), len(rider.arrays), len(out_specs), len(rider.out_shape), len(scratch_shapes), len(rider.sems)):
        bounds.append((total, total + n))
        total += n
    last = grid[0] - 1
    relay_step = max(last - RELAY_STEPS_BEFORE_LAST, 0) if rider.relay_early else last

    def riding(*refs):
        ins, r_ins, outs, r_outs, scratch, r_sems = (refs[lo:hi] for lo, hi in bounds)

        @pl.when(pl.program_id(0) == 0)
        def _():
            rider.start(r_ins, r_outs, r_sems)

        body(*ins, *outs, *scratch)

        @pl.when(pl.program_id(0) == relay_step)
        def _():
            rider.relay(r_ins, r_outs, r_sems)

        @pl.when(pl.program_id(0) == last)
        def _():
            rider.finish(r_ins, r_outs, r_sems)

    return pl.pallas_call(
        riding, name=name, grid=grid, in_specs=list(in_specs) + [HBM] * len(rider.arrays),
        out_specs=list(out_specs) + [HBM] * len(rider.out_shape), out_shape=list(out_shape) + list(rider.out_shape),
        scratch_shapes=list(scratch_shapes) + list(rider.sems), compiler_params=params)(*args, *rider.arrays)


def _join(*riders):
    def phase(which):
        def run(ins, outs, sems):
            i = o = s = 0
            for r in riders:
                ni, no, ns = len(r.arrays), len(r.out_shape), len(r.sems)
                getattr(r, which)(ins[i:i + ni], outs[o:o + no], sems[s:s + ns])
                i, o, s = i + ni, o + no, s + ns
        return run

    return _Rider(arrays=[a for r in riders for a in r.arrays], out_shape=[a for r in riders for a in r.out_shape],
                  sems=[a for r in riders for a in r.sems], start=phase("start"), finish=phase("finish"), relay=phase("relay"),
                  relay_early=all(r.relay_early for r in riders if r.relay is not _nothing))


def _alone(rider, name):
    n_in, n_out = len(rider.arrays), len(rider.out_shape)

    def body(*refs):
        parts = refs[:n_in], refs[n_in:n_in + n_out], refs[n_in + n_out:]
        rider.start(*parts)
        rider.relay(*parts)
        rider.finish(*parts)

    return pl.pallas_call(body, name=name, in_specs=[HBM] * n_in, out_specs=[HBM] * n_out, out_shape=list(rider.out_shape),
                          scratch_shapes=list(rider.sems))(*rider.arrays)


def _rot_half(t):
    n = t.shape[1]
    lane = lax.broadcasted_iota(jnp.int32, t.shape, 1)
    return jnp.where((lane % HEAD_DIM) < HEAD_DIM // 2, pltpu.roll(t, n - HEAD_DIM // 2, 1), pltpu.roll(t, HEAD_DIM // 2, 1))


def _tile_tables(base_ref, tile_ref):
    start = tile_ref[0]
    cos_0, sin_0 = start[0:1, :], start[1:2, :]
    return base_ref[0] * cos_0 - base_ref[1] * sin_0, base_ref[2] * cos_0 + base_ref[3] * sin_0


def _rope(t, cos, sin):
    reps = t.shape[1] // LANES
    if reps > 1:
        cos, sin = jnp.tile(cos, (1, reps)), jnp.tile(sin, (1, reps))
    return t * cos + _rot_half(t) * sin


def _rope_bwd(d, cos, sin):
    reps = d.shape[1] // LANES
    if reps > 1:
        cos, sin = jnp.tile(cos, (1, reps)), jnp.tile(sin, (1, reps))
    return d * cos + _rot_half(d * sin)


KV_SPREAD = 4 * LANES


def _spread_kv(t):
    low = lax.broadcasted_iota(jnp.int32, t.shape, 1) < HEAD_DIM
    swapped = pltpu.roll(t, HEAD_DIM, 1)
    zero = jnp.zeros_like(t)
    return jnp.concatenate([jnp.where(low, t, zero), jnp.where(low, zero, swapped),
                            jnp.where(low, swapped, zero), jnp.where(low, zero, t)], axis=1)


def _rms(x):
    r = lax.rsqrt(jnp.mean(x * x, axis=-1, keepdims=True) + RMS_EPS)
    return x * r, r


def _rms_bwd(dh, n, r, g):
    dn = dh * g
    dx = r * (dn - n * jnp.mean(dn * n, axis=-1, keepdims=True))
    return dx, jnp.sum(dh * n, axis=0, keepdims=True)


def _token_tile(s):
    return min(512, s)


def _window_mean(window_sum, pos, size):
    head = window_sum[:POOL_HALO, :] / jnp.minimum(pos[:POOL_HALO, :] + 1, size).astype(F32)
    return jnp.concatenate([head, window_sum[POOL_HALO:, :] * (1.0 / size)], axis=0)


def _fwd_inproj(x, g_mix, win_t, b_in, cos, sin, w_pool, b_pool, pool_scale, rider=None):
    s = x.shape[0]
    tm = _token_tile(s)

    def body(x_ref, g_ref, w_ref, b_ref, cos_ref, sin_ref, wp_ref, bp_ref, ps_ref,
             q_ref, k_ref, v_ref, vt_ref, mix_ref, pool_ref, tail_ref):
        i = pl.program_id(0)

        @pl.when(i == 0)
        def _():
            tail_ref[...] = jnp.zeros_like(tail_ref)

        half = tm // 2
        halves = [slice(0, half), slice(half, tm)]
        projected = []
        for rows in halves:
            n, _ = _rms(x_ref[rows, :])
            projected.append(_nt((n * g_ref[...]).astype(BF16), w_ref[...]) + b_ref[...])
        cos_tile, sin_tile = _tile_tables(cos_ref, sin_ref)
        tail = tail_ref[...]
        for k, (rows, z) in enumerate(zip(halves, projected)):
            cos_t, sin_t = cos_tile[rows, :], sin_tile[rows, :]
            q_ref[rows, :] = (_rope(z[:, :ATTN_WIDTH], cos_t, sin_t) * Q_SCALE).astype(BF16)
            k_ref[rows, :] = _spread_kv(_rope(z[:, ATTN_WIDTH:ATTN_WIDTH + KV_WIDTH], cos_t, sin_t)).astype(BF16)
            vz = _spread_kv(z[:, ATTN_WIDTH + KV_WIDTH:ATTN_WIDTH + 2 * KV_WIDTH])
            v_ref[rows, :] = vz.astype(BF16)
            vt_ref[:, rows] = vz.T.astype(BF16)
            u = z[:, ATTN_WIDTH + 2 * KV_WIDTH:]
            u_ext = jnp.concatenate([tail, u], axis=0)
            tail = u[half - POOL_HALO:, :]
            pos = lax.broadcasted_iota(jnp.int32, (half, POOL_GROUP_WIDTH), 0) + (i * tm + k * half)
            for g, size in enumerate(POOL_SIZES):
                cols = slice(g * POOL_GROUP_WIDTH, (g + 1) * POOL_GROUP_WIDTH)
                a = u_ext[:, cols]
                shift = 1
                while shift < size:
                    a = a + pltpu.roll(a, shift, 0)
                    shift *= 2
                mixed = (_window_mean(a[POOL_HALO:, :], pos, size) - u[:, cols]).astype(BF16)
                pre = _nn(mixed, wp_ref[g]) + bp_ref[:, cols]
                mix_ref[rows, cols] = mixed
                pool_ref[rows, cols] = (pre * ps_ref[:, cols]).astype(BF16)
        tail_ref[...] = tail

    bf = lambda w: jax.ShapeDtypeStruct((s, w), BF16)
    return _gridded(
        body, rider, name="fwd_inproj", grid=(s // tm,),
        in_specs=[_rows(tm, D_MODEL), _full((1, D_MODEL)), _full((IN_WIDTH, D_MODEL)), _full((1, IN_WIDTH)),
                  _full((4, tm, LANES)), pl.BlockSpec((1, 2, LANES), lambda i: (i, 0, 0)), _full((4, POOL_GROUP_WIDTH, POOL_GROUP_WIDTH)),
                  _full((1, POOL_WIDTH)), _full((1, POOL_WIDTH))],
        out_specs=[_rows(tm, ATTN_WIDTH), _rows(tm, KV_SPREAD), _rows(tm, KV_SPREAD),
                   pl.BlockSpec((KV_SPREAD, tm), lambda i: (0, i)), _rows(tm, POOL_WIDTH), _rows(tm, POOL_WIDTH)],
        out_shape=[bf(ATTN_WIDTH), bf(KV_SPREAD), bf(KV_SPREAD), jax.ShapeDtypeStruct((KV_SPREAD, s), BF16),
                   bf(POOL_WIDTH), bf(POOL_WIDTH)],
        scratch_shapes=[pltpu.VMEM((POOL_HALO, POOL_WIDTH), F32)],
        args=(x, g_mix, win_t, b_in, cos, sin, w_pool, b_pool, pool_scale))


ATTN_TILE = 1024
PAIR = 2 * LANES


def _band_masks(tile):
    j = lax.broadcasted_iota(jnp.int32, (4 * BLOCK, 2 * BLOCK), 0) % (2 * BLOCK)
    r = lax.broadcasted_iota(jnp.int32, (4 * BLOCK, 2 * BLOCK), 1) % BLOCK
    band = (j > r) & (j <= r + BLOCK)
    return band & ((tile > 0) | (j >= BLOCK)), band


def _band(cur_ref, prev_ref, b, kv):
    halves = []
    for half in range(2):
        cols = slice(kv * PAIR + half * LANES, kv * PAIR + (half + 1) * LANES)
        if b == 0:
            halves.append(jnp.concatenate([prev_ref[:, cols], cur_ref[0:BLOCK, cols]], axis=0))
        else:
            halves.append(cur_ref[(b - 1) * BLOCK:(b + 1) * BLOCK, cols])
    return jnp.concatenate(halves, axis=0)


def _stack_pair(ref, rows, kv):
    return jnp.concatenate([ref[rows, kv * PAIR:kv * PAIR + LANES], ref[rows, kv * PAIR + LANES:(kv + 1) * PAIR]], axis=0)


def _pair_heads(kv, half):
    return GQA_GROUP * kv + half, GQA_GROUP * kv + 2 + half


def _band_t(cur_ref, prev_ref, b, kv):
    halves = []
    for half in range(2):
        lanes = slice(kv * PAIR + half * LANES, kv * PAIR + (half + 1) * LANES)
        if b == 0:
            halves.append(jnp.concatenate([prev_ref[lanes, :], cur_ref[lanes, 0:BLOCK]], axis=1))
        else:
            halves.append(cur_ref[lanes, (b - 1) * BLOCK:(b + 1) * BLOCK])
    return jnp.concatenate(halves, axis=1)


def _reduce_rows(x, op, reduce):
    while x.shape[0] > 8:
        half = x.shape[0] // 2
        x = op(x[:half], x[half:])
    return reduce(x, axis=0, keepdims=True)


def _per_query(ref, rows, top, bottom):
    return jnp.concatenate([ref[top:top + 1, rows], ref[bottom:bottom + 1, rows]], axis=1)


def _sink_per_query(sink_ref, top, bottom):
    first_slab = lax.broadcasted_iota(jnp.int32, (1, 2 * BLOCK), 1) < BLOCK
    return jnp.where(first_slab, sink_ref[:, top:top + 1], sink_ref[:, bottom:bottom + 1])


def _attn_fwd(q, kz, vt, sinks, rider=None):
    s = q.shape[0]
    tq = min(ATTN_TILE, s)

    def body(q_ref, k_ref, kp_ref, vt_ref, vtp_ref, sink_ref, o_ref, lse_ref):
        first, band = _band_masks(pl.program_id(0))
        chains = [(b, kv) for b in range(tq // BLOCK) for kv in range(N_KV_HEADS)]

        def scores(b, kv):
            rows = slice(b * BLOCK, (b + 1) * BLOCK)
            return _nt(_band(k_ref, kp_ref, b, kv), _stack_pair(q_ref, rows, kv))

        def store(b, kv, ot):
            rows = slice(b * BLOCK, (b + 1) * BLOCK)
            o = ot.T.astype(BF16)
            o_ref[rows, kv * PAIR:kv * PAIR + LANES] = o[:BLOCK]
            o_ref[rows, kv * PAIR + LANES:(kv + 1) * PAIR] = o[BLOCK:]

        ahead = scores(*chains[0])
        behind = None
        for n, (b, kv) in enumerate(chains):
            rows = slice(b * BLOCK, (b + 1) * BLOCK)
            st = jnp.where(first if b == 0 else band, ahead, -jnp.inf)
            if n + 1 < len(chains):
                ahead = scores(*chains[n + 1])
            probs = []
            for half in range(2):
                top, bottom = _pair_heads(kv, half)
                sink = _sink_per_query(sink_ref, top, bottom)
                sh = st[half * 2 * BLOCK:(half + 1) * 2 * BLOCK, :]
                m = jnp.maximum(_reduce_rows(sh, jnp.maximum, jnp.max), sink)
                p = jnp.exp(sh - m)
                denom = _reduce_rows(p, jnp.add, jnp.sum) + jnp.exp(sink - m)
                probs.append((p * (1.0 / denom)).astype(BF16))
                lse = m + jnp.log(denom)
                lse_ref[top:top + 1, rows] = lse[:, :BLOCK]
                lse_ref[bottom:bottom + 1, rows] = lse[:, BLOCK:]
            ot = _nn(_band_t(vt_ref, vtp_ref, b, kv), jnp.concatenate(probs, axis=0))
            if behind is not None:
                store(*behind)
            behind = (b, kv, ot)
        store(*behind)

    per = tq // BLOCK
    cur = lambda w: pl.BlockSpec((tq, w), lambda i: (i, 0))
    prev = pl.BlockSpec((BLOCK, KV_SPREAD), lambda i: (jnp.maximum(per * i - 1, 0), 0))
    cur_t = pl.BlockSpec((KV_SPREAD, tq), lambda i: (0, i))
    prev_t = pl.BlockSpec((KV_SPREAD, BLOCK), lambda i: (0, jnp.maximum(per * i - 1, 0)))
    return _gridded(
        body, rider, name="attn_fwd", grid=(s // tq,),
        in_specs=[cur(ATTN_WIDTH), cur(KV_SPREAD), prev, cur_t, prev_t, _full((1, N_Q_HEADS))],
        out_specs=[cur(ATTN_WIDTH), pl.BlockSpec((N_Q_HEADS, tq), lambda i: (0, i))],
        out_shape=[jax.ShapeDtypeStruct((s, ATTN_WIDTH), BF16), jax.ShapeDtypeStruct((N_Q_HEADS, s), F32)],
        scratch_shapes=[], args=(q, kz, kz, vt, vt, sinks))


FF_CHUNK = 256
TN_ROW_CHUNK = 256


def _resident(shape):
    return pl.BlockSpec(shape, lambda *_: (0,) * len(shape), pipeline_mode=pl.Buffered(1))


def _flush_rows(acc_ref, out_ref, sem, rows, is_last):
    @pl.when(is_last)
    def _():
        pltpu.make_async_copy(acc_ref.at[rows, :], out_ref.at[rows, :], sem).start()


def _flush_wait(acc_ref, out_ref, sem, is_last):
    @pl.when(is_last)
    def _():
        pltpu.make_async_copy(acc_ref, out_ref, sem).wait()


def _accumulate_tn(acc_ref, a_ref, b, out_ref, sem, is_last):
    for m0 in range(0, acc_ref.shape[0], TN_ROW_CHUNK):
        rows = slice(m0, m0 + TN_ROW_CHUNK)
        acc_ref[rows, :] += _tn(a_ref[:, rows], b)
        _flush_rows(acc_ref, out_ref, sem, rows, is_last)
    _flush_wait(acc_ref, out_ref, sem, is_last)


def _fwd_outproj_ffn_act(attn, pool, w_out, b_out, x, g_ffn, wg_t, wu_t, rider=None):
    s = x.shape[0]
    tm = _token_tile(s)

    def body(a_ref, p_ref, w_ref, b_ref, x_ref, g_ref, wg_ref, wu_ref, x2_ref, gate_ref, up_ref, act_ref):
        x2 = x_ref[...] + _nn(a_ref[...], w_ref[:ATTN_WIDTH, :]) + _nn(p_ref[...], w_ref[ATTN_WIDTH:, :]) + b_ref[...]
        x2_ref[...] = x2
        n, _ = _rms(x2)
        h = (n * g_ref[...]).astype(BF16)

        def products(c0):
            return _nt(h, wg_ref[c0:c0 + FF_CHUNK, :]), _nt(h, wu_ref[c0:c0 + FF_CHUNK, :])

        ahead = products(0)
        for c0 in range(0, D_FF, FF_CHUNK):
            cols = slice(c0, c0 + FF_CHUNK)
            gate, up = ahead
            if c0 + FF_CHUNK < D_FF:
                ahead = products(c0 + FF_CHUNK)
            gate_ref[:, cols] = gate.astype(BF16)
            up_ref[:, cols] = up.astype(BF16)
            act_ref[:, cols] = (gate * jax.nn.sigmoid(gate) * up).astype(BF16)

    act_shape = jax.ShapeDtypeStruct((s, D_FF), BF16)
    return _gridded(
        body, rider, name="fwd_outproj_ffn_act", grid=(s // tm,),
        in_specs=[_rows(tm, ATTN_WIDTH), _rows(tm, POOL_WIDTH), _resident((D_MODEL, D_MODEL)), _full((1, D_MODEL)),
                  _rows(tm, D_MODEL), _full((1, D_MODEL)), _resident((D_FF, D_MODEL)), _resident((D_FF, D_MODEL))],
        out_specs=[_rows(tm, D_MODEL)] + [_rows(tm, D_FF)] * 3,
        out_shape=[jax.ShapeDtypeStruct((s, D_MODEL), F32)] + [act_shape] * 3,
        scratch_shapes=[], args=(attn, pool, w_out, b_out, x, g_ffn, wg_t, wu_t))


def _fwd_down_loss(act, x2, wd, g_final, target):
    s = x2.shape[0]
    tm = _token_tile(s)
    last = s // tm - 1

    def body(a_ref, x2_ref, wd_ref, g_ref, t_ref, dx3_ref, sq_ref, dg_ref, dwd_ref, acc_ref, sem):
        @pl.when(pl.program_id(0) == 0)
        def _():
            sq_ref[...] = jnp.zeros_like(sq_ref)
            dg_ref[...] = jnp.zeros_like(dg_ref)
            acc_ref[...] = jnp.zeros_like(acc_ref)

        halves = [slice(0, tm // 2), slice(tm // 2, tm)]
        g = g_ref[...]
        down = [_nn(a_ref[rows, :], wd_ref[...]) for rows in halves]
        for half, (rows, y) in enumerate(zip(halves, down)):
            x3 = x2_ref[rows, :] + y
            n, r = _rms(x3)
            diff = n * g - t_ref[rows, :]
            sq_ref[...] += jnp.sum(diff * diff, axis=0, keepdims=True)
            dx3, dg = _rms_bwd(diff * (1.0 / D_MODEL), n, r, g)
            dg_ref[...] += dg
            dx3_ref[rows, :] = dx3
            dx3b = dx3.astype(BF16)
            if half == 0:
                for m0 in range(0, D_FF, TN_ROW_CHUNK):
                    acc_ref[m0:m0 + TN_ROW_CHUNK, :] += _tn(a_ref[rows, m0:m0 + TN_ROW_CHUNK], dx3b)
            else:
                _accumulate_tn(acc_ref, a_ref.at[rows, :], dx3b, dwd_ref, sem, pl.program_id(0) == last)

    return pl.pallas_call(
        body, name="fwd_down_loss", grid=(s // tm,),
        in_specs=[_rows(tm, D_FF), _rows(tm, D_MODEL), _resident((D_FF, D_MODEL)), _full((1, D_MODEL)), _rows(tm, D_MODEL)],
        out_specs=[_rows(tm, D_MODEL), _full((1, D_MODEL)), _full((1, D_MODEL)), HBM],
        out_shape=[jax.ShapeDtypeStruct((s, D_MODEL), F32),
                   jax.ShapeDtypeStruct((1, D_MODEL), F32), jax.ShapeDtypeStruct((1, D_MODEL), F32),
                   jax.ShapeDtypeStruct((D_FF, D_MODEL), F32)],
        scratch_shapes=[pltpu.VMEM((D_FF, D_MODEL), F32), pltpu.SemaphoreType.DMA],
        compiler_params=_params("arbitrary"),
    )(act, x2, wd, g_final, target)


FFN_BWD_TILE = 256


def _bwd_ffn(dx3, gate, up, x2, wd, wg_t, wu_t, g_ffn, rider=None):
    s = x2.shape[0]
    tm = min(FFN_BWD_TILE, s)
    last = s // tm - 1

    def body(dx3_ref, gate_ref, up_ref, x2_ref, wd_ref, wg_ref, wu_ref, g_ref,
             dx2_ref, dg_ref, db_ref, dwg_ref, dwu_ref, dgate_ref, dup_ref, accg_ref, accu_ref, sems):
        @pl.when(pl.program_id(0) == 0)
        def _():
            dg_ref[...] = jnp.zeros_like(dg_ref)
            db_ref[...] = jnp.zeros_like(db_ref)
            accg_ref[...] = jnp.zeros_like(accg_ref)
            accu_ref[...] = jnp.zeros_like(accu_ref)

        dx3b = dx3_ref[...].astype(BF16)
        g = g_ref[...]
        n, r = _rms(x2_ref[...])
        h = (n * g).astype(BF16)
        ahead = _nt(dx3b, wd_ref[0:FF_CHUNK, :])
        for c0 in range(0, D_FF, FF_CHUNK):
            cols = slice(c0, c0 + FF_CHUNK)
            dact = ahead
            if c0 + FF_CHUNK < D_FF:
                ahead = _nt(dx3b, wd_ref[c0 + FF_CHUNK:c0 + 2 * FF_CHUNK, :])
            gate = gate_ref[:, cols].astype(F32)
            up = up_ref[:, cols].astype(F32)
            sig = jax.nn.sigmoid(gate)
            silu = gate * sig
            dup = (dact * silu).astype(BF16)
            dgate = (dact * up * (sig + silu * (1.0 - sig))).astype(BF16)
            dup_ref[:, cols] = dup
            dgate_ref[:, cols] = dgate
            accg_ref[cols, :] += _tn(dgate, h)
            accu_ref[cols, :] += _tn(dup, h)
        dh2 = _nn(dgate_ref[...], wg_ref[...]) + _nn(dup_ref[...], wu_ref[...])
        dx, dg = _rms_bwd(dh2, n, r, g)
        dx2 = dx3_ref[...] + dx
        dg_ref[...] += dg
        db_ref[...] += jnp.sum(dx2, axis=0, keepdims=True)
        dx2_ref[...] = dx2

        @pl.when(pl.program_id(0) == last)
        def _():
            outs = [pltpu.make_async_copy(accg_ref, dwg_ref, sems.at[0]), pltpu.make_async_copy(accu_ref, dwu_ref, sems.at[1])]
            for cp in outs:
                cp.start()
            for cp in outs:
                cp.wait()

    grad_shape = jax.ShapeDtypeStruct((D_FF, D_MODEL), F32)
    weight = _resident((D_FF, D_MODEL))
    return _gridded(
        body, rider, name="bwd_ffn", grid=(s // tm,),
        in_specs=[_rows(tm, D_MODEL), _rows(tm, D_FF), _rows(tm, D_FF),
                  _rows(tm, D_MODEL), weight, weight, weight, _full((1, D_MODEL))],
        out_specs=[_rows(tm, D_MODEL), _full((1, D_MODEL)), _full((1, D_MODEL)), HBM, HBM],
        out_shape=[jax.ShapeDtypeStruct((s, D_MODEL), F32),
                   jax.ShapeDtypeStruct((1, D_MODEL), F32), jax.ShapeDtypeStruct((1, D_MODEL), F32), grad_shape, grad_shape],
        scratch_shapes=[pltpu.VMEM((tm, D_FF), BF16), pltpu.VMEM((tm, D_FF), BF16),
                        pltpu.VMEM((D_FF, D_MODEL), F32), pltpu.VMEM((D_FF, D_MODEL), F32), pltpu.SemaphoreType.DMA((2,))],
        args=(dx3, gate, up, x2, wd, wg_t, wu_t, g_ffn))


def _bwd_outproj_pool(dx2, attn, pool, mixed, w_out, w_pool, b_pool, pool_scale, rider=None):
    s = dx2.shape[0]
    tm = min(2 * _token_tile(s), s)
    nt = s // tm

    def body(dx_ref, a_ref, p_ref, mix_ref, w_ref, wp_ref, bp_ref, ps_ref,
             dattn_ref, du_ref, dwout_ref, dwp_ref, dbp_ref, dps_ref, head_ref):
        step = pl.program_id(0)
        tile = nt - 1 - step

        @pl.when(step == 0)
        def _():
            head_ref[...] = jnp.zeros_like(head_ref)
            dwout_ref[...] = jnp.zeros_like(dwout_ref)
            dwp_ref[...] = jnp.zeros_like(dwp_ref)
            dbp_ref[...] = jnp.zeros_like(dbp_ref)
            dps_ref[...] = jnp.zeros_like(dps_ref)

        dx = dx_ref[...].astype(BF16)
        dwout_ref[:ATTN_WIDTH, :] += _tn(a_ref[...], dx)
        dwout_ref[ATTN_WIDTH:, :] += _tn(p_ref[...], dx)
        dcat = _nt(dx, w_ref[...])
        dattn_ref[...] = dcat[:, :ATTN_WIDTH].astype(BF16)
        dpool = dcat[:, ATTN_WIDTH:]
        pos = lax.broadcasted_iota(jnp.int32, (tm, POOL_GROUP_WIDTH), 0) + tile * tm
        head = head_ref[...]
        n_ext = tm + POOL_HALO
        for g, size in enumerate(POOL_SIZES):
            cols = slice(g * POOL_GROUP_WIDTH, (g + 1) * POOL_GROUP_WIDTH)
            mixed_g = mix_ref[:, cols]
            pre = _nn(mixed_g, wp_ref[g]) + bp_ref[:, cols]
            dy = dpool[:, cols]
            dps_ref[:, cols] += jnp.sum(dy * pre, axis=0, keepdims=True)
            dpre = dy * ps_ref[:, cols]
            dbp_ref[:, cols] += jnp.sum(dpre, axis=0, keepdims=True)
            dpre_b = dpre.astype(BF16)
            dwp_ref[g] += _tn(mixed_g, dpre_b)
            dmixed = _nt(dpre_b, wp_ref[g])
            w = _window_mean(dmixed, pos, size)
            head_ref[:, cols] = w[:POOL_HALO, :]
            a = jnp.concatenate([w, head[:, cols]], axis=0)
            shift = 1
            while shift < size:
                a = a + pltpu.roll(a, n_ext - shift, 0)
                shift *= 2
            du_ref[:, cols] = (a[:tm, :] - dmixed).astype(BF16)

    rev = lambda w: pl.BlockSpec((tm, w), lambda i: (nt - 1 - i, 0))
    return _gridded(
        body, rider, name="bwd_outproj_pool", grid=(nt,),
        in_specs=[rev(D_MODEL), rev(ATTN_WIDTH), rev(POOL_WIDTH), rev(POOL_WIDTH), _full((D_MODEL, D_MODEL)),
                  _full((4, POOL_GROUP_WIDTH, POOL_GROUP_WIDTH)), _full((1, POOL_WIDTH)), _full((1, POOL_WIDTH))],
        out_specs=[rev(ATTN_WIDTH), rev(POOL_WIDTH), _full((D_MODEL, D_MODEL)),
                   _full((4, POOL_GROUP_WIDTH, POOL_GROUP_WIDTH)), _full((1, POOL_WIDTH)), _full((1, POOL_WIDTH))],
        out_shape=[jax.ShapeDtypeStruct((s, ATTN_WIDTH), BF16), jax.ShapeDtypeStruct((s, POOL_WIDTH), BF16),
                   jax.ShapeDtypeStruct((D_MODEL, D_MODEL), F32),
                   jax.ShapeDtypeStruct((4, POOL_GROUP_WIDTH, POOL_GROUP_WIDTH), F32),
                   jax.ShapeDtypeStruct((1, POOL_WIDTH), F32), jax.ShapeDtypeStruct((1, POOL_WIDTH), F32)],
        scratch_shapes=[pltpu.VMEM((POOL_HALO, POOL_WIDTH), F32)],
        args=(dx2, attn, pool, mixed, w_out, w_pool, b_pool, pool_scale))


def _fold_spread(t):
    low = lax.broadcasted_iota(jnp.int32, (2 * BLOCK, LANES), 1) < HEAD_DIM
    kept = jnp.where(low, t[:2 * BLOCK, :], t[2 * BLOCK:, :])
    return kept + pltpu.roll(kept, HEAD_DIM, 1)


def _attn_bwd(q, kz, vz, dattn, lse, sinks, rider=None):
    s = q.shape[0]
    tq = min(ATTN_TILE, s)
    nt = s // tq
    per = tq // BLOCK

    def body(q_ref, k_ref, kp_ref, v_ref, vp_ref, do_ref, lse_ref, sink_ref,
             dq_ref, dk_ref, dv_ref, dsink_ref, dk_acc, dv_acc, dk_carry, dv_carry):
        step = pl.program_id(0)

        @pl.when(step == 0)
        def _():
            dk_carry[...] = jnp.zeros_like(dk_carry)
            dv_carry[...] = jnp.zeros_like(dv_carry)
            dsink_ref[...] = jnp.zeros_like(dsink_ref)

        dk_acc[0:tq, :] = jnp.zeros((tq, KV_WIDTH), F32)
        dv_acc[0:tq, :] = jnp.zeros((tq, KV_WIDTH), F32)
        dk_acc[tq:, :] = dk_carry[...]
        dv_acc[tq:, :] = dv_carry[...]
        first, band = _band_masks(nt - 1 - step)
        low = lax.broadcasted_iota(jnp.int32, (2 * BLOCK, LANES), 1) < HEAD_DIM
        chains = [(b, kv) for b in range(per) for kv in range(N_KV_HEADS)]

        def operands(b, kv):
            rows = slice(b * BLOCK, (b + 1) * BLOCK)
            qab = _stack_pair(q_ref, rows, kv)
            doab = _stack_pair(do_ref, rows, kv)
            kzb = _band(k_ref, kp_ref, b, kv)
            return qab, doab, kzb, _nt(kzb, qab), _nt(_band(v_ref, vp_ref, b, kv), doab)

        folded = {}

        def finish(b, kv, dqab, dkz, dvz):
            rows = slice(b * BLOCK, (b + 1) * BLOCK)
            dq_ref[rows, kv * PAIR:kv * PAIR + LANES] = dqab[:BLOCK] * Q_SCALE
            dq_ref[rows, kv * PAIR + LANES:(kv + 1) * PAIR] = dqab[BLOCK:] * Q_SCALE
            folded[kv] = (_fold_spread(dkz), _fold_spread(dvz))
            if kv == N_KV_HEADS - 1:
                band_rows = slice(b * BLOCK, (b + 2) * BLOCK)
                dk_acc[band_rows, :] += jnp.where(low, folded[0][0], folded[1][0])
                dv_acc[band_rows, :] += jnp.where(low, folded[0][1], folded[1][1])

        ahead = operands(*chains[0])
        behind = None
        for n, (b, kv) in enumerate(chains):
            rows = slice(b * BLOCK, (b + 1) * BLOCK)
            mask = first if b == 0 else band
            qab, doab, kzb, st, dpt = ahead
            if n + 1 < len(chains):
                ahead = operands(*chains[n + 1])
            probs, dscores = [], []
            for half in range(2):
                top, bottom = _pair_heads(kv, half)
                keys = slice(half * 2 * BLOCK, (half + 1) * 2 * BLOCK)
                lse_h = _per_query(lse_ref, rows, top, bottom)
                p = jnp.where(mask[keys, :], jnp.exp(st[keys, :] - lse_h), 0.0)
                dph = dpt[keys, :]
                delta = _reduce_rows(p * dph, jnp.add, jnp.sum)
                probs.append(p.astype(BF16))
                dscores.append((p * (dph - delta)).astype(BF16))
                leak = jnp.exp(_sink_per_query(sink_ref, top, bottom) - lse_h) * delta
                dsink_ref[:, top:top + 1] -= jnp.sum(leak[:, :BLOCK], axis=1, keepdims=True)
                dsink_ref[:, bottom:bottom + 1] -= jnp.sum(leak[:, BLOCK:], axis=1, keepdims=True)
            ds = jnp.concatenate(dscores, axis=0)
            results = (_tn(ds, kzb), _nn(ds, qab), _nn(jnp.concatenate(probs, axis=0), doab))
            if behind is not None:
                finish(*behind)
            behind = (b, kv, *results)
        finish(*behind)
        dk_ref[...] = dk_acc[BLOCK:, :]
        dv_ref[...] = dv_acc[BLOCK:, :]
        dk_carry[...] = dk_acc[0:BLOCK, :]
        dv_carry[...] = dv_acc[0:BLOCK, :]

    cur = lambda w: pl.BlockSpec((tq, w), lambda i: (nt - 1 - i, 0))
    prev = pl.BlockSpec((BLOCK, KV_SPREAD), lambda i: (jnp.maximum(per * (nt - 1 - i) - 1, 0), 0))
    acc = pltpu.VMEM((tq + BLOCK, KV_WIDTH), F32)
    carry = pltpu.VMEM((BLOCK, KV_WIDTH), F32)
    return _gridded(
        body, rider, name="attn_bwd", grid=(nt,),
        in_specs=[cur(ATTN_WIDTH), cur(KV_SPREAD), prev, cur(KV_SPREAD), prev, cur(ATTN_WIDTH),
                  pl.BlockSpec((N_Q_HEADS, tq), lambda i: (0, nt - 1 - i)), _full((1, N_Q_HEADS))],
        out_specs=[cur(ATTN_WIDTH), cur(KV_WIDTH), cur(KV_WIDTH), _full((1, N_Q_HEADS))],
        out_shape=[jax.ShapeDtypeStruct((s, ATTN_WIDTH), F32), jax.ShapeDtypeStruct((s, KV_WIDTH), F32),
                   jax.ShapeDtypeStruct((s, KV_WIDTH), F32), jax.ShapeDtypeStruct((1, N_Q_HEADS), F32)],
        scratch_shapes=[acc, acc, carry, carry],
        args=(q, kz, kz, vz, vz, dattn, lse, sinks))


def _bwd_inproj(dq, dk, dv, du, cos, sin, win_t, x, g_mix, dx2):
    s = x.shape[0]
    tm = _token_tile(s)

    def body(dq_ref, dk_ref, dv_ref, du_ref, cos_ref, sin_ref, w_ref, x_ref, g_ref, dx2_ref,
             dx_ref, dw_ref, db_ref, dg_ref):
        @pl.when(pl.program_id(0) == 0)
        def _():
            dw_ref[...] = jnp.zeros_like(dw_ref)
            db_ref[...] = jnp.zeros_like(db_ref)
            dg_ref[...] = jnp.zeros_like(dg_ref)

        cos_t, sin_t = _tile_tables(cos_ref, sin_ref)
        g = g_ref[...]
        staged = []
        for rows in (slice(0, tm // 2), slice(tm // 2, tm)):
            dz32 = jnp.concatenate([_rope_bwd(dq_ref[rows, :], cos_t[rows, :], sin_t[rows, :]),
                                    _rope_bwd(dk_ref[rows, :], cos_t[rows, :], sin_t[rows, :]),
                                    dv_ref[rows, :], du_ref[rows, :].astype(F32)], axis=1)
            db_ref[...] += jnp.sum(dz32, axis=0, keepdims=True)
            dz = dz32.astype(BF16)
            n, r = _rms(x_ref[rows, :])
            staged.append((rows, dz, (n * g).astype(BF16), n, r, _nn(dz, w_ref[...])))
        dz = jnp.concatenate([part[1] for part in staged], axis=0)
        h = jnp.concatenate([part[2] for part in staged], axis=0)
        for m0 in range(0, IN_WIDTH, TN_ROW_CHUNK):
            dw_ref[m0:m0 + TN_ROW_CHUNK, :] += _tn(dz[:, m0:m0 + TN_ROW_CHUNK], h)
        for rows, _, _, n, r, dh in staged:
            dx, dg = _rms_bwd(dh, n, r, g)
            dg_ref[...] += dg
            dx_ref[rows, :] = dx2_ref[rows, :] + dx

    return _gridded(
        body, None, name="bwd_inproj", grid=(s // tm,),
        in_specs=[_rows(tm, ATTN_WIDTH), _rows(tm, KV_WIDTH), _rows(tm, KV_WIDTH), _rows(tm, POOL_WIDTH),
                  _full((4, tm, LANES)), pl.BlockSpec((1, 2, LANES), lambda i: (i, 0, 0)), _full((IN_WIDTH, D_MODEL)), _rows(tm, D_MODEL),
                  _full((1, D_MODEL)), _rows(tm, D_MODEL)],
        out_specs=[_rows(tm, D_MODEL), _full((IN_WIDTH, D_MODEL)), _full((1, IN_WIDTH)), _full((1, D_MODEL))],
        out_shape=[jax.ShapeDtypeStruct((s, D_MODEL), F32), jax.ShapeDtypeStruct((IN_WIDTH, D_MODEL), F32),
                   jax.ShapeDtypeStruct((1, IN_WIDTH), F32), jax.ShapeDtypeStruct((1, D_MODEL), F32)],
        scratch_shapes=[], args=(dq, dk, dv, du, cos, sin, win_t, x, g_mix, dx2))


def _rope_tables(s, tm):
    inv_freq = jnp.tile(1.0 / (ROPE_THETA ** (jnp.arange(0, HEAD_DIM, 2, dtype=F32) / HEAD_DIM)), 4)
    sign = jnp.tile(jnp.repeat(jnp.array([-1.0, 1.0], F32), HEAD_DIM // 2), 2)
    within = jnp.arange(tm, dtype=F32)[:, None] * inv_freq[None, :]
    start = jnp.arange(0, s, tm, dtype=F32)[:, None] * inv_freq[None, :]
    cos, sin = jnp.cos(within), jnp.sin(within)
    base = jnp.stack([cos, sin, sign * sin, sign * cos])
    return base, jnp.stack([jnp.cos(start), jnp.sin(start)], axis=1)


def _place():
    return lax.axis_index("x"), lax.axis_index("y"), lax.axis_index("c")


def _other_chips(x, y):
    return [(1 - x, y), (x, 1 - y), (1 - x, 1 - y)]


def _gather_rider(blocks, relay_early=False):
    nm = len(blocks)

    def plan(ins, outs, sems):
        send_sems, recv_sems, local_sems = sems
        x, y, c = _place()
        me, sibling = (x, y, c), (x, y, 1 - c)
        chips = _other_chips(x, y)

        def rows(m, px, py, pc):
            r = ins[m].shape[0]
            return outs[m].at[pl.ds((4 * px + 2 * py + pc) * r, r), :]

        def copy(m, k, block, to, src=None):
            return pltpu.make_async_remote_copy(
                src_ref=rows(m, *block) if src is None else src, dst_ref=rows(m, *block),
                send_sem=send_sems.at[k * nm + m], recv_sem=recv_sems.at[k * nm + m],
                device_id=to, device_id_type=MESH)

        mine = [pltpu.make_async_copy(ins[m], rows(m, *me), local_sems.at[m]) for m in range(nm)]
        first = [copy(m, 0, me, sibling, src=ins[m]) for m in range(nm)]
        first += [copy(m, 1 + j, me, (*chip, c), src=ins[m]) for j, chip in enumerate(chips) for m in range(nm)]
        return me, sibling, chips, copy, mine, first

    def start(ins, outs, sems):
        *_, mine, first = plan(ins, outs, sems)
        for cp in mine + first:
            cp.start()

    def passed_on(ins, outs, sems):
        me, sibling, chips, copy, _, _ = plan(ins, outs, sems)
        return [copy(m, 4 + j, (*chip, me[2]), sibling) for j, chip in enumerate(chips) for m in range(nm)]

    def relay(ins, outs, sems):
        me, _, chips, copy, _, _ = plan(ins, outs, sems)
        forwards = passed_on(ins, outs, sems)
        for j, chip in enumerate(chips):
            for m in range(nm):
                copy(m, 1 + j, (*chip, me[2]), me).wait_recv()
                forwards[j * nm + m].start()

    def finish(ins, outs, sems):
        me, sibling, chips, copy, mine, first = plan(ins, outs, sems)
        for m in range(nm):
            copy(m, 0, sibling, me).wait_recv()
        for j, chip in enumerate(chips):
            for m in range(nm):
                copy(m, 4 + j, (*chip, 1 - me[2]), me).wait_recv()
        for cp in first + passed_on(ins, outs, sems):
            cp.wait_send()
        for cp in mine:
            cp.wait()

    return _Rider(
        arrays=list(blocks), out_shape=[jax.ShapeDtypeStruct((N_DEV * b.shape[0], b.shape[1]), b.dtype) for b in blocks],
        sems=[pltpu.SemaphoreType.DMA((7 * nm,)), pltpu.SemaphoreType.DMA((7 * nm,)), pltpu.SemaphoreType.DMA((nm,))],
        start=start, finish=finish, relay=relay, relay_early=relay_early)


def _exchange_rider(copies_of, arrays, out_shape, n_copies):
    def copies(ins, outs, sems):
        send_sems, recv_sems = sems
        return [pltpu.make_async_remote_copy(src_ref=src, dst_ref=dst, send_sem=send_sems.at[k], recv_sem=recv_sems.at[k],
                                             device_id=to, device_id_type=MESH)
                for k, (src, dst, to) in enumerate(copies_of(ins, outs))]

    def start(ins, outs, sems):
        for cp in copies(ins, outs, sems):
            cp.start()

    def finish(ins, outs, sems):
        cps = copies(ins, outs, sems)
        for cp in cps:
            cp.wait_recv()
        for cp in cps:
            cp.wait_send()

    return _Rider(arrays=list(arrays), out_shape=out_shape,
                  sems=[pltpu.SemaphoreType.DMA((n_copies,)), pltpu.SemaphoreType.DMA((n_copies,))], start=start, finish=finish)


def _sibling_rider(grads):
    def copies_of(ins, outs):
        x, y, c = _place()
        for g_ref, o_ref in zip(ins, outs):
            r = g_ref.shape[0] // N_DEV
            for q in range(N_CHIPS):
                yield g_ref.at[pl.ds((2 * q + 1 - c) * r, r), :], o_ref.at[pl.ds(q * r, r), :], (x, y, 1 - c)

    return _exchange_rider(copies_of, grads, [jax.ShapeDtypeStruct((g.shape[0] // 2, g.shape[1]), F32) for g in grads],
                           len(grads) * N_CHIPS)


def _chip_sum(grad, from_sibling, place):
    r = grad.shape[0] // N_DEV
    w = grad.shape[1]

    def body(place_ref, g_ref, s_ref, wire_ref, own_ref):
        total = g_ref[...] + s_ref[...]
        wire_ref[...] = total.astype(BF16)

        @pl.when(pl.program_id(0) == place_ref[1])
        def _():
            own_ref[...] = total

    grid_spec = pltpu.PrefetchScalarGridSpec(
        num_scalar_prefetch=1, grid=(N_CHIPS,),
        in_specs=[pl.BlockSpec((r, w), lambda q, p: (2 * q + p[0], 0)), pl.BlockSpec((r, w), lambda q, p: (q, 0))],
        out_specs=[pl.BlockSpec((r, w), lambda q, p: (q, 0)), pl.BlockSpec((r, w), lambda q, p: (0, 0))])
    return pl.pallas_call(
        body, name="grad_chip_sum", grid_spec=grid_spec,
        out_shape=[jax.ShapeDtypeStruct((N_CHIPS * r, w), BF16), jax.ShapeDtypeStruct((r, w), F32)],
        compiler_params=_params("arbitrary"),
    )(place, grad, from_sibling)


def _chips_rider(wires):
    def copies_of(ins, outs):
        x, y, c = _place()
        for w_ref, o_ref in zip(ins, outs):
            r = w_ref.shape[0] // N_CHIPS
            for j, (px, py) in enumerate(_other_chips(x, y)):
                yield w_ref.at[pl.ds((2 * px + py) * r, r), :], o_ref.at[pl.ds(j * r, r), :], (px, py, c)

    return _exchange_rider(copies_of, wires,
                           [jax.ShapeDtypeStruct((3 * (w.shape[0] // N_CHIPS), w.shape[1]), BF16) for w in wires], len(wires) * 3)


SEM = pl.BlockSpec(memory_space=pltpu.SEMAPHORE)
DATAFLOW = pltpu.SideEffectType.DATAFLOW_SIDE_EFFECTING


def _last_exchange_copies(wire_refs, late_ref, land_refs, land_late_ref, send_sems, recv_sems):
    x, y, c = _place()
    ends = []
    for w_ref, o_ref in zip(wire_refs, land_refs):
        r = w_ref.shape[0] // N_CHIPS
        for j, (px, py) in enumerate(_other_chips(x, y)):
            ends.append((w_ref.at[pl.ds((2 * px + py) * r, r), :], o_ref.at[pl.ds(j * r, r), :], (px, py, c)))
    rows = late_ref.shape[0]
    mine = land_late_ref.at[pl.ds((4 * x + 2 * y + c) * rows, rows), :]
    peers = [(x, y, 1 - c)] + [(px, py, pc) for px, py in _other_chips(x, y) for pc in (c, 1 - c)]
    ends += [(late_ref, mine, peer) for peer in peers]
    return [pltpu.make_async_remote_copy(src_ref=src, dst_ref=dst, send_sem=send_sems[k], recv_sem=recv_sems[k],
                                         device_id=to, device_id_type=MESH) for k, (src, dst, to) in enumerate(ends)]


N_LAST_COPIES = 2 * 3 + (N_DEV - 1)


def _last_exchange_start(wires, late):
    n = N_LAST_COPIES
    lands = [lax.empty((3 * (w.shape[0] // N_CHIPS), w.shape[1]), w.dtype) for w in wires]
    land_late = lax.empty((N_DEV * late.shape[0], late.shape[1]), late.dtype)

    def body(wout_ref, win_ref, late_ref, land_wout_ref, land_win_ref, land_late_ref, *outs):
        send_sems, recv_sems, token_ref = outs[:n], outs[n:2 * n], outs[-1]
        for cp in _last_exchange_copies([wout_ref, win_ref], late_ref, [land_wout_ref, land_win_ref], land_late_ref,
                                        send_sems, recv_sems):
            cp.start()
        token_ref[...] = jnp.zeros_like(token_ref)

    operands = [pltpu.with_memory_space_constraint(a, pltpu.HBM) for a in (*wires, late, *lands, land_late)]
    thru = [pltpu.HBM(a.shape, a.dtype) for a in operands]
    out = pl.pallas_call(
        body, name="last_exchange_start",
        out_shape=[pltpu.SemaphoreType.DMA(())] * (2 * n) + thru + [jax.ShapeDtypeStruct((8, LANES), F32)],
        in_specs=[HBM] * 6, out_specs=[SEM] * (2 * n) + [HBM] * 6 + [VMEM],
        input_output_aliases={i: 2 * n + i for i in range(6)},
        compiler_params=pltpu.CompilerParams(has_side_effects=DATAFLOW),
    )(*operands)
    return out[:2 * n], out[2 * n:2 * n + 6], out[-1]


def _last_exchange_wait(sems, buffers, after):
    n = N_LAST_COPIES

    def body(wout_ref, win_ref, late_ref, land_wout_ref, land_win_ref, land_late_ref, *rest):
        send_sems, recv_sems = rest[:n], rest[n:2 * n]
        for cp in _last_exchange_copies([wout_ref, win_ref], late_ref, [land_wout_ref, land_win_ref], land_late_ref,
                                        send_sems, recv_sems):
            cp.wait_send()
            cp.wait_recv()

    out = pl.pallas_call(
        body, name="last_exchange_wait", out_shape=[pltpu.HBM(a.shape, a.dtype) for a in buffers],
        in_specs=[HBM] * 6 + [SEM] * (2 * n) + [pl.BlockSpec(memory_space=pl.ANY)], out_specs=[HBM] * 6,
        input_output_aliases={i: i for i in range(6)},
        compiler_params=pltpu.CompilerParams(has_side_effects=DATAFLOW),
    )(*buffers, *sems, after)
    return out[3], out[4], out[5]


def _adamw_math(w, g, m, v):
    m = ADAM_B1 * m + (1.0 - ADAM_B1) * g
    v = ADAM_B2 * v + (1.0 - ADAM_B2) * jnp.square(g)
    m_hat = m / (1.0 - ADAM_B1 ** ADAM_STEP)
    v_hat = v / (1.0 - ADAM_B2 ** ADAM_STEP)
    delta = -ADAM_LR * (m_hat / (jnp.sqrt(v_hat) + ADAM_EPS) + ADAM_WD * w)
    return delta, m, v


def _reduce_adamw(own, received, w, m, v, after):
    r = own.shape[0]

    def body(own_ref, rec_ref, w_ref, m_ref, v_ref, after_ref, g_ref, d_ref, nm_ref, nv_ref):
        del after_ref
        g = ((own_ref[...] + rec_ref[0:r, :].astype(F32)) + rec_ref[r:2 * r, :].astype(F32)) + rec_ref[2 * r:, :].astype(F32)
        g_ref[...] = g
        d_ref[...], nm_ref[...], nv_ref[...] = _adamw_math(w_ref[...], g, m_ref[...], v_ref[...])

    shape = jax.ShapeDtypeStruct(own.shape, F32)
    return pl.pallas_call(
        body, name="reduce_adamw", in_specs=[VMEM] * 5 + [pl.BlockSpec(memory_space=pl.ANY)], out_specs=[VMEM] * 4,
        out_shape=[shape] * 4, compiler_params=_params(),
    )(own, received, w, m, v, after)


SMALL_WIDE = (("w_pool", 65536),)
SMALL_EARLY = (("b_pool", 512), ("pool_scale", 512), ("b_out", 1024), ("g_ffn", 1024), ("g_final", 1024), ("loss", 1024))
SMALL_LATE = (("sinks", 8), ("g_mix", 1024), ("b_in", 1280))
SMALL = SMALL_WIDE + SMALL_EARLY + SMALL_LATE


def _small_rows(size):
    return -(-size // (8 * LANES)) * 8


def _pack_small(values, entries=SMALL):
    parts = []
    for name, size in entries:
        flat = values[name].reshape(-1).astype(F32)
        parts.append(jnp.pad(flat, (0, _small_rows(size) * LANES - size)).reshape(-1, LANES))
    return jnp.concatenate(parts, axis=0)


def _unpack_small(packed, shapes):
    out, row = {}, 0
    for name, size in SMALL:
        rows = _small_rows(size)
        if name in shapes:
            out[name] = packed[row:row + rows].reshape(-1)[:size].reshape(shapes[name])
        row += rows
    return out


def _small_sum_adamw(gathered, w, m, v):
    n = len(gathered)

    def body(*refs):
        w_ref, m_ref, v_ref, g_ref, d_ref, nm_ref, nv_ref = refs[n:]

        def total(ref):
            rows = ref.shape[0] // N_DEV
            acc = ref[0:rows, :].astype(F32)
            for dev in range(1, N_DEV):
                acc = acc + ref[dev * rows:(dev + 1) * rows, :].astype(F32)
            return acc

        g = jnp.concatenate([total(ref) for ref in refs[:n]], axis=0)
        g_ref[...] = g
        d_ref[...], nm_ref[...], nv_ref[...] = _adamw_math(w_ref[...], g, m_ref[...], v_ref[...])

    shape = jax.ShapeDtypeStruct(w.shape, F32)
    return pl.pallas_call(
        body, name="small_sum_adamw", in_specs=[VMEM] * (n + 3), out_specs=[VMEM] * 4, out_shape=[shape] * 4,
        compiler_params=_params(),
    )(*gathered, w, m, v)


def kernel(x, g_mix, w_in, b_in, sinks, w_pool, b_pool, pool_scale, w_out, b_out, g_ffn, w_gate, w_up, w_down, g_final, loss_target, m_g_mix, m_w_in, m_b_in, m_sinks, m_w_pool, m_b_pool, m_pool_scale, m_w_out, m_b_out, m_g_ffn, m_w_gate, m_w_up, m_w_down, m_g_final, v_g_mix, v_w_in, v_b_in, v_sinks, v_w_pool, v_b_pool, v_pool_scale, v_w_out, v_b_out, v_g_ffn, v_w_gate, v_w_up, v_w_down, v_g_final):
    weights = dict(g_mix=g_mix, w_in=w_in, b_in=b_in, sinks=sinks, w_pool=w_pool, b_pool=b_pool, pool_scale=pool_scale,
                   w_out=w_out, b_out=b_out, g_ffn=g_ffn, w_gate=w_gate, w_up=w_up, w_down=w_down, g_final=g_final)
    mom1 = dict(g_mix=m_g_mix, w_in=m_w_in, b_in=m_b_in, sinks=m_sinks, w_pool=m_w_pool, b_pool=m_b_pool,
                pool_scale=m_pool_scale, w_out=m_w_out, b_out=m_b_out, g_ffn=m_g_ffn, w_gate=m_w_gate, w_up=m_w_up,
                w_down=m_w_down, g_final=m_g_final)
    mom2 = dict(g_mix=v_g_mix, w_in=v_w_in, b_in=v_b_in, sinks=v_sinks, w_pool=v_w_pool, b_pool=v_b_pool,
                pool_scale=v_pool_scale, w_out=v_w_out, b_out=v_b_out, g_ffn=v_g_ffn, w_gate=v_w_gate, w_up=v_w_up,
                w_down=v_w_down, g_final=v_g_final)
    order = ("g_mix", "w_in", "b_in", "sinks", "w_pool", "b_pool", "pool_scale", "w_out", "b_out", "g_ffn",
             "w_gate", "w_up", "w_down", "g_final")
    big = ("w_in", "w_out", "w_gate", "w_up", "w_down")
    transposed = ("w_in", "w_gate", "w_up")

    def row_shard(name, a):
        return a[0].T if name in transposed else a[0]

    shard = {n: row_shard(n, weights[n]).astype(BF16) for n in big}
    xs, target = x[0], loss_target[0]
    cos, sin = _rope_tables(xs.shape[0], _token_tile(xs.shape[0]))
    wp_b = w_pool[0].astype(BF16)
    bp = b_pool.reshape(1, POOL_WIDTH)
    ps = pool_scale.reshape(1, POOL_WIDTH)
    g_fin = g_final.reshape(1, D_MODEL)
    px, py, pc = _place()
    place = jnp.stack([pc, 2 * px + py]).astype(jnp.int32)

    (win_t,) = _alone(_gather_rider([shard["w_in"]]), "gather_w_in")
    q, kz, vz, vt, mixed, pool, w_out_b, wg_t = _fwd_inproj(
        xs, g_mix, win_t, b_in, cos, sin, wp_b, bp, ps,
        rider=_gather_rider([shard["w_out"], shard["w_gate"]], relay_early=True))
    attn, lse, wu_t = _attn_fwd(q, kz, vt, sinks, rider=_gather_rider([shard["w_up"]]))
    x2, gate, up, act, wd = _fwd_outproj_ffn_act(attn, pool, w_out_b, b_out, xs, g_ffn, wg_t, wu_t,
                                                 rider=_gather_rider([shard["w_down"]], relay_early=True))
    dx3, sq, dg_final, d_wd = _fwd_down_loss(act, x2, wd, g_fin, target)

    dx2, dg_ffn, db_out, d_wg_t, d_wu_t, wd_sibling = _bwd_ffn(
        dx3, gate, up, x2, wd, wg_t, wu_t, g_ffn, rider=_sibling_rider([d_wd]))
    wd_sum = _chip_sum(d_wd, wd_sibling, place)
    in_grads = [d_wg_t, d_wu_t]
    dattn, du, d_wout, d_wpool, d_bpool, d_pscale, wd_received, *in_sibling = _bwd_outproj_pool(
        dx2, attn, pool, mixed, w_out_b, wp_b, bp, ps, rider=_join(_chips_rider([wd_sum[0]]), _sibling_rider(in_grads)))
    in_sums = [_chip_sum(g, s, place) for g, s in zip(in_grads, in_sibling)]
    small_wide = _pack_small(dict(w_pool=d_wpool), SMALL_WIDE).astype(BF16)
    small_early = _pack_small(dict(b_pool=d_bpool, pool_scale=d_pscale, b_out=db_out, g_ffn=dg_ffn,
                                   g_final=dg_final, loss=sq), SMALL_EARLY)
    dq, dk, dv, d_sinks, *landed = _attn_bwd(
        q, kz, vz, dattn, lse, sinks,
        rider=_join(_chips_rider([wire for wire, _ in in_sums]), _sibling_rider([d_wout]),
                    _gather_rider([small_wide, small_early])))
    ffn_sums, ffn_received = in_sums + [wd_sum], landed[:2] + [wd_received]
    wout_sum = _chip_sum(d_wout, landed[2], place)
    gathered_wide, gathered_early = landed[3], landed[4]
    dx, d_win_t, d_bin, d_gmix = _bwd_inproj(dq, dk, dv, du, cos, sin, win_t, xs, g_mix, dx2)
    (win_sibling,) = _alone(_sibling_rider([d_win_t]), "grad_exchange_sibling")
    win_sum = _chip_sum(d_win_t, win_sibling, place)
    small_late = _pack_small(dict(sinks=d_sinks, g_mix=d_gmix, b_in=d_bin), SMALL_LATE)

    grad, delta, new_m, new_v = {}, {}, {}, {}

    def update(n, own, rec, after):
        results = _reduce_adamw(own, rec, row_shard(n, weights[n]), row_shard(n, mom1[n]), row_shard(n, mom2[n]), after)
        grad[n], delta[n], new_m[n], new_v[n] = [(a.T if n in transposed else a)[None] for a in results]
        return results[0]

    sems, in_flight, after = _last_exchange_start([wout_sum[0], win_sum[0]], small_late)
    for n, (_, own), rec in zip(("w_gate", "w_up", "w_down"), ffn_sums, ffn_received):
        after = update(n, own, rec, after)
    wout_received, win_received, gathered_late = _last_exchange_wait(sems, in_flight, after)
    gathered_late = lax.dynamic_update_slice(gathered_late, small_late, ((4 * px + 2 * py + pc) * small_late.shape[0], 0))
    update("w_out", wout_sum[1], wout_received, after)
    update("w_in", win_sum[1], win_received, after)

    shapes = {n: weights[n].shape for n in order if n not in big}
    zero_loss = jnp.zeros((1, D_MODEL), F32)
    packed = _small_sum_adamw(
        [gathered_wide, gathered_early, gathered_late], _pack_small({**weights, "loss": zero_loss}),
        _pack_small({**mom1, "loss": zero_loss}), _pack_small({**mom2, "loss": zero_loss}))
    for store, pk in zip((grad, delta, new_m, new_v), packed):
        store.update(_unpack_small(pk, shapes))
    loss_rows = _unpack_small(packed[0], {"loss": (D_MODEL,)})["loss"]
    loss = (0.5 / D_MODEL) * jnp.sum(loss_rows)

    return (loss, dx[None], *[grad[n] for n in order], *[delta[n] for n in order],
            *[new_m[n] for n in order], *[new_v[n] for n in order])
```

```python
from typing import Any, Callable, NamedTuple, Sequence

import jax
import jax.numpy as jnp
from jax import lax
from jax.experimental import pallas as pl
from jax.experimental.pallas import tpu as pltpu

D_MODEL = 1024
ATTN_WIDTH = 512
KV_WIDTH = 128
POOL_WIDTH = 512
HEAD_DIM = 64
N_Q_HEADS = 8
N_KV_HEADS = 2
GQA_GROUP = 4
BLOCK = 128
POOL_SIZES = (2, 4, 8, 16)
POOL_GROUP_WIDTH = 128
POOL_HALO = 16
IN_WIDTH = 1280
D_FF = 2816
RMS_EPS = 1e-5
ROPE_THETA = 10000.0
Q_SCALE = HEAD_DIM ** -0.5

ADAM_LR = 0.001
ADAM_B1 = 0.9
ADAM_B2 = 0.999
ADAM_EPS = 1e-08
ADAM_WD = 0.01
ADAM_STEP = 10

N_DEV = 8
N_CHIPS = 4
LANES = 128
VMEM_LIMIT_BYTES = 60 * 1024 * 1024

F32 = jnp.float32
BF16 = jnp.bfloat16
MESH = pl.DeviceIdType.MESH
HBM = pl.BlockSpec(memory_space=pltpu.HBM)
VMEM = pl.BlockSpec(memory_space=pltpu.VMEM)


def _params(*semantics):
    return pltpu.CompilerParams(dimension_semantics=semantics or None, vmem_limit_bytes=VMEM_LIMIT_BYTES)


def _nn(a, b):
    return jnp.dot(a, b, preferred_element_type=F32)


def _nt(a, b):
    return lax.dot_general(a, b, (((1,), (1,)), ((), ())), preferred_element_type=F32)


def _tn(a, b):
    return lax.dot_general(a, b, (((0,), (0,)), ((), ())), preferred_element_type=F32)


def _full(shape):
    return pl.BlockSpec(shape, lambda *_: (0,) * len(shape))


def _rows(tm, width):
    return pl.BlockSpec((tm, width), lambda i, *_: (i, 0))


def _nothing(ins, outs, sems):
    del ins, outs, sems


RELAY_STEPS_BEFORE_LAST = 2


class _Rider(NamedTuple):
    arrays: Sequence[Any]
    out_shape: Sequence[Any]
    sems: Sequence[Any]
    start: Callable[..., None]
    finish: Callable[..., None]
    relay: Callable[..., None] = _nothing
    relay_early: bool = False


def _gridded(body, rider, *, name, grid, in_specs, out_specs, out_shape, scratch_shapes, args):
    params = _params("arbitrary")
    if rider is None:
        return pl.pallas_call(body, name=name, grid=grid, in_specs=in_specs, out_specs=out_specs, out_shape=out_shape,
                              scratch_shapes=scratch_shapes, compiler_params=params)(*args)
    bounds, total = [], 0
    for n in (len(in_specs), len(rider.arrays), len(out_specs), len(rider.out_shape), len(scratch_shapes), len(rider.sems)):
        bounds.append((total, total + n))
        total += n
    last = grid[0] - 1
    relay_step = max(last - RELAY_STEPS_BEFORE_LAST, 0) if rider.relay_early else last

    def riding(*refs):
        ins, r_ins, outs, r_outs, scratch, r_sems = (refs[lo:hi] for lo, hi in bounds)

        @pl.when(pl.program_id(0) == 0)
        def _():
            rider.start(r_ins, r_outs, r_sems)

        body(*ins, *outs, *scratch)

        @pl.when(pl.program_id(0) == relay_step)
        def _():
            rider.relay(r_ins, r_outs, r_sems)

        @pl.when(pl.program_id(0) == last)
        def _():
            rider.finish(r_ins, r_outs, r_sems)

    return pl.pallas_call(
        riding, name=name, grid=grid, in_specs=list(in_specs) + [HBM] * len(rider.arrays),
        out_specs=list(out_specs) + [HBM] * len(rider.out_shape), out_shape=list(out_shape) + list(rider.out_shape),
        scratch_shapes=list(scratch_shapes) + list(rider.sems), compiler_params=params)(*args, *rider.arrays)


def _join(*riders):
    def phase(which):
        def run(ins, outs, sems):
            i = o = s = 0
            for r in riders:
                ni, no, ns = len(r.arrays), len(r.out_shape), len(r.sems)
                getattr(r, which)(ins[i:i + ni], outs[o:o + no], sems[s:s + ns])
                i, o, s = i + ni, o + no, s + ns
        return run

    return _Rider(arrays=[a for r in riders for a in r.arrays], out_shape=[a for r in riders for a in r.out_shape],
                  sems=[a for r in riders for a in r.sems], start=phase("start"), finish=phase("finish"), relay=phase("relay"),
                  relay_early=all(r.relay_early for r in riders if r.relay is not _nothing))


def _alone(rider, name):
    n_in, n_out = len(rider.arrays), len(rider.out_shape)

    def body(*refs):
        parts = refs[:n_in], refs[n_in:n_in + n_out], refs[n_in + n_out:]
        rider.start(*parts)
        rider.relay(*parts)
        rider.finish(*parts)

    return pl.pallas_call(body, name=name, in_specs=[HBM] * n_in, out_specs=[HBM] * n_out, out_shape=list(rider.out_shape),
                          scratch_shapes=list(rider.sems))(*rider.arrays)


def _rot_half(t):
    n = t.shape[1]
    lane = lax.broadcasted_iota(jnp.int32, t.shape, 1)
    return jnp.where((lane % HEAD_DIM) < HEAD_DIM // 2, pltpu.roll(t, n - HEAD_DIM // 2, 1), pltpu.roll(t, HEAD_DIM // 2, 1))


def _tile_tables(base_ref, tile_ref):
    start = tile_ref[0]
    cos_0, sin_0 = start[0:1, :], start[1:2, :]
    return base_ref[0] * cos_0 - base_ref[1] * sin_0, base_ref[2] * cos_0 + base_ref[3] * sin_0


def _rope(t, cos, sin):
    reps = t.shape[1] // LANES
    if reps > 1:
        cos, sin = jnp.tile(cos, (1, reps)), jnp.tile(sin, (1, reps))
    return t * cos + _rot_half(t) * sin


def _rope_bwd(d, cos, sin):
    reps = d.shape[1] // LANES
    if reps > 1:
        cos, sin = jnp.tile(cos, (1, reps)), jnp.tile(sin, (1, reps))
    return d * cos + _rot_half(d * sin)


KV_SPREAD = 4 * LANES


def _spread_kv(t):
    low = lax.broadcasted_iota(jnp.int32, t.shape, 1) < HEAD_DIM
    swapped = pltpu.roll(t, HEAD_DIM, 1)
    zero = jnp.zeros_like(t)
    return jnp.concatenate([jnp.where(low, t, zero), jnp.where(low, zero, swapped),
                            jnp.where(low, swapped, zero), jnp.where(low, zero, t)], axis=1)


def _rms(x):
    r = lax.rsqrt(jnp.mean(x * x, axis=-1, keepdims=True) + RMS_EPS)
    return x * r, r


def _rms_bwd(dh, n, r, g):
    dn = dh * g
    dx = r * (dn - n * jnp.mean(dn * n, axis=-1, keepdims=True))
    return dx, jnp.sum(dh * n, axis=0, keepdims=True)


def _token_tile(s):
    return min(512, s)


def _window_mean(window_sum, pos, size):
    head = window_sum[:POOL_HALO, :] / jnp.minimum(pos[:POOL_HALO, :] + 1, size).astype(F32)
    return jnp.concatenate([head, window_sum[POOL_HALO:, :] * (1.0 / size)], axis=0)


def _fwd_inproj(x, g_mix, win_t, b_in, cos, sin, w_pool, b_pool, pool_scale, rider=None):
    s = x.shape[0]
    tm = _token_tile(s)

    def body(x_ref, g_ref, w_ref, b_ref, cos_ref, sin_ref, wp_ref, bp_ref, ps_ref,
             q_ref, k_ref, v_ref, vt_ref, mix_ref, pool_ref, tail_ref):
        i = pl.program_id(0)

        @pl.when(i == 0)
        def _():
            tail_ref[...] = jnp.zeros_like(tail_ref)

        half = tm // 2
        halves = [slice(0, half), slice(half, tm)]
        projected = []
        for rows in halves:
            n, _ = _rms(x_ref[rows, :])
            projected.append(_nt((n * g_ref[...]).astype(BF16), w_ref[...]) + b_ref[...])
        cos_tile, sin_tile = _tile_tables(cos_ref, sin_ref)
        tail = tail_ref[...]
        for k, (rows, z) in enumerate(zip(halves, projected)):
            cos_t, sin_t = cos_tile[rows, :], sin_tile[rows, :]
            q_ref[rows, :] = (_rope(z[:, :ATTN_WIDTH], cos_t, sin_t) * Q_SCALE).astype(BF16)
            k_ref[rows, :] = _spread_kv(_rope(z[:, ATTN_WIDTH:ATTN_WIDTH + KV_WIDTH], cos_t, sin_t)).astype(BF16)
            vz = _spread_kv(z[:, ATTN_WIDTH + KV_WIDTH:ATTN_WIDTH + 2 * KV_WIDTH])
            v_ref[rows, :] = vz.astype(BF16)
            vt_ref[:, rows] = vz.T.astype(BF16)
            u = z[:, ATTN_WIDTH + 2 * KV_WIDTH:]
            u_ext = jnp.concatenate([tail, u], axis=0)
            tail = u[half - POOL_HALO:, :]
            pos = lax.broadcasted_iota(jnp.int32, (half, POOL_GROUP_WIDTH), 0) + (i * tm + k * half)
            for g, size in enumerate(POOL_SIZES):
                cols = slice(g * POOL_GROUP_WIDTH, (g + 1) * POOL_GROUP_WIDTH)
                a = u_ext[:, cols]
                shift = 1
                while shift < size:
                    a = a + pltpu.roll(a, shift, 0)
                    shift *= 2
                mixed = (_window_mean(a[POOL_HALO:, :], pos, size) - u[:, cols]).astype(BF16)
                pre = _nn(mixed, wp_ref[g]) + bp_ref[:, cols]
                mix_ref[rows, cols] = mixed
                pool_ref[rows, cols] = (pre * ps_ref[:, cols]).astype(BF16)
        tail_ref[...] = tail

    bf = lambda w: jax.ShapeDtypeStruct((s, w), BF16)
    return _gridded(
        body, rider, name="fwd_inproj", grid=(s // tm,),
        in_specs=[_rows(tm, D_MODEL), _full((1, D_MODEL)), _full((IN_WIDTH, D_MODEL)), _full((1, IN_WIDTH)),
                  _full((4, tm, LANES)), pl.BlockSpec((1, 2, LANES), lambda i: (i, 0, 0)), _full((4, POOL_GROUP_WIDTH, POOL_GROUP_WIDTH)),
                  _full((1, POOL_WIDTH)), _full((1, POOL_WIDTH))],
        out_specs=[_rows(tm, ATTN_WIDTH), _rows(tm, KV_SPREAD), _rows(tm, KV_SPREAD),
                   pl.BlockSpec((KV_SPREAD, tm), lambda i: (0, i)), _rows(tm, POOL_WIDTH), _rows(tm, POOL_WIDTH)],
        out_shape=[bf(ATTN_WIDTH), bf(KV_SPREAD), bf(KV_SPREAD), jax.ShapeDtypeStruct((KV_SPREAD, s), BF16),
                   bf(POOL_WIDTH), bf(POOL_WIDTH)],
        scratch_shapes=[pltpu.VMEM((POOL_HALO, POOL_WIDTH), F32)],
        args=(x, g_mix, win_t, b_in, cos, sin, w_pool, b_pool, pool_scale))


ATTN_TILE = 1024
PAIR = 2 * LANES


def _band_masks(tile):
    j = lax.broadcasted_iota(jnp.int32, (4 * BLOCK, 2 * BLOCK), 0) % (2 * BLOCK)
    r = lax.broadcasted_iota(jnp.int32, (4 * BLOCK, 2 * BLOCK), 1) % BLOCK
    band = (j > r) & (j <= r + BLOCK)
    return band & ((tile > 0) | (j >= BLOCK)), band


def _band(cur_ref, prev_ref, b, kv):
    halves = []
    for half in range(2):
        cols = slice(kv * PAIR + half * LANES, kv * PAIR + (half + 1) * LANES)
        if b == 0:
            halves.append(jnp.concatenate([prev_ref[:, cols], cur_ref[0:BLOCK, cols]], axis=0))
        else:
            halves.append(cur_ref[(b - 1) * BLOCK:(b + 1) * BLOCK, cols])
    return jnp.concatenate(halves, axis=0)


def _stack_pair(ref, rows, kv):
    return jnp.concatenate([ref[rows, kv * PAIR:kv * PAIR + LANES], ref[rows, kv * PAIR + LANES:(kv + 1) * PAIR]], axis=0)


def _pair_heads(kv, half):
    return GQA_GROUP * kv + half, GQA_GROUP * kv + 2 + half


def _band_t(cur_ref, prev_ref, b, kv):
    halves = []
    for half in range(2):
        lanes = slice(kv * PAIR + half * LANES, kv * PAIR + (half + 1) * LANES)
        if b == 0:
            halves.append(jnp.concatenate([prev_ref[lanes, :], cur_ref[lanes, 0:BLOCK]], axis=1))
        else:
            halves.append(cur_ref[lanes, (b - 1) * BLOCK:(b + 1) * BLOCK])
    return jnp.concatenate(halves, axis=1)


def _reduce_rows(x, op, reduce):
    while x.shape[0] > 8:
        half = x.shape[0] // 2
        x = op(x[:half], x[half:])
    return reduce(x, axis=0, keepdims=True)


def _per_query(ref, rows, top, bottom):
    return jnp.concatenate([ref[top:top + 1, rows], ref[bottom:bottom + 1, rows]], axis=1)


def _sink_per_query(sink_ref, top, bottom):
    first_slab = lax.broadcasted_iota(jnp.int32, (1, 2 * BLOCK), 1) < BLOCK
    return jnp.where(first_slab, sink_ref[:, top:top + 1], sink_ref[:, bottom:bottom + 1])


def _attn_fwd(q, kz, vt, sinks, rider=None):
    s = q.shape[0]
    tq = min(ATTN_TILE, s)

    def body(q_ref, k_ref, kp_ref, vt_ref, vtp_ref, sink_ref, o_ref, lse_ref):
        first, band = _band_masks(pl.program_id(0))
        chains = [(b, kv) for b in range(tq // BLOCK) for kv in range(N_KV_HEADS)]

        def scores(b, kv):
            rows = slice(b * BLOCK, (b + 1) * BLOCK)
            return _nt(_band(k_ref, kp_ref, b, kv), _stack_pair(q_ref, rows, kv))

        def store(b, kv, ot):
            rows = slice(b * BLOCK, (b + 1) * BLOCK)
            o = ot.T.astype(BF16)
            o_ref[rows, kv * PAIR:kv * PAIR + LANES] = o[:BLOCK]
            o_ref[rows, kv * PAIR + LANES:(kv + 1) * PAIR] = o[BLOCK:]

        ahead = scores(*chains[0])
        behind = None
        for n, (b, kv) in enumerate(chains):
            rows = slice(b * BLOCK, (b + 1) * BLOCK)
            st = jnp.where(first if b == 0 else band, ahead, -jnp.inf)
            if n + 1 < len(chains):
                ahead = scores(*chains[n + 1])
            probs = []
            for half in range(2):
                top, bottom = _pair_heads(kv, half)
                sink = _sink_per_query(sink_ref, top, bottom)
                sh = st[half * 2 * BLOCK:(half + 1) * 2 * BLOCK, :]
                m = jnp.maximum(_reduce_rows(sh, jnp.maximum, jnp.max), sink)
                p = jnp.exp(sh - m)
                denom = _reduce_rows(p, jnp.add, jnp.sum) + jnp.exp(sink - m)
                probs.append((p * (1.0 / denom)).astype(BF16))
                lse = m + jnp.log(denom)
                lse_ref[top:top + 1, rows] = lse[:, :BLOCK]
                lse_ref[bottom:bottom + 1, rows] = lse[:, BLOCK:]
            ot = _nn(_band_t(vt_ref, vtp_ref, b, kv), jnp.concatenate(probs, axis=0))
            if behind is not None:
                store(*behind)
            behind = (b, kv, ot)
        store(*behind)

    per = tq // BLOCK
    cur = lambda w: pl.BlockSpec((tq, w), lambda i: (i, 0))
    prev = pl.BlockSpec((BLOCK, KV_SPREAD), lambda i: (jnp.maximum(per * i - 1, 0), 0))
    cur_t = pl.BlockSpec((KV_SPREAD, tq), lambda i: (0, i))
    prev_t = pl.BlockSpec((KV_SPREAD, BLOCK), lambda i: (0, jnp.maximum(per * i - 1, 0)))
    return _gridded(
        body, rider, name="attn_fwd", grid=(s // tq,),
        in_specs=[cur(ATTN_WIDTH), cur(KV_SPREAD), prev, cur_t, prev_t, _full((1, N_Q_HEADS))],
        out_specs=[cur(ATTN_WIDTH), pl.BlockSpec((N_Q_HEADS, tq), lambda i: (0, i))],
        out_shape=[jax.ShapeDtypeStruct((s, ATTN_WIDTH), BF16), jax.ShapeDtypeStruct((N_Q_HEADS, s), F32)],
        scratch_shapes=[], args=(q, kz, kz, vt, vt, sinks))


FF_CHUNK = 256
TN_ROW_CHUNK = 256


def _resident(shape):
    return pl.BlockSpec(shape, lambda *_: (0,) * len(shape), pipeline_mode=pl.Buffered(1))


def _flush_rows(acc_ref, out_ref, sem, rows, is_last):
    @pl.when(is_last)
    def _():
        pltpu.make_async_copy(acc_ref.at[rows, :], out_ref.at[rows, :], sem).start()


def _flush_wait(acc_ref, out_ref, sem, is_last):
    @pl.when(is_last)
    def _():
        pltpu.make_async_copy(acc_ref, out_ref, sem).wait()


def _accumulate_tn(acc_ref, a_ref, b, out_ref, sem, is_last):
    for m0 in range(0, acc_ref.shape[0], TN_ROW_CHUNK):
        rows = slice(m0, m0 + TN_ROW_CHUNK)
        acc_ref[rows, :] += _tn(a_ref[:, rows], b)
        _flush_rows(acc_ref, out_ref, sem, rows, is_last)
    _flush_wait(acc_ref, out_ref, sem, is_last)


def _fwd_outproj_ffn_act(attn, pool, w_out, b_out, x, g_ffn, wg_t, wu_t, rider=None):
    s = x.shape[0]
    tm = _token_tile(s)

    def body(a_ref, p_ref, w_ref, b_ref, x_ref, g_ref, wg_ref, wu_ref, x2_ref, gate_ref, up_ref, act_ref):
        x2 = x_ref[...] + _nn(a_ref[...], w_ref[:ATTN_WIDTH, :]) + _nn(p_ref[...], w_ref[ATTN_WIDTH:, :]) + b_ref[...]
        x2_ref[...] = x2
        n, _ = _rms(x2)
        h = (n * g_ref[...]).astype(BF16)

        def products(c0):
            return _nt(h, wg_ref[c0:c0 + FF_CHUNK, :]), _nt(h, wu_ref[c0:c0 + FF_CHUNK, :])

        ahead = products(0)
        for c0 in range(0, D_FF, FF_CHUNK):
            cols = slice(c0, c0 + FF_CHUNK)
            gate, up = ahead
            if c0 + FF_CHUNK < D_FF:
                ahead = products(c0 + FF_CHUNK)
            gate_ref[:, cols] = gate.astype(BF16)
            up_ref[:, cols] = up.astype(BF16)
            act_ref[:, cols] = (gate * jax.nn.sigmoid(gate) * up).astype(BF16)

    act_shape = jax.ShapeDtypeStruct((s, D_FF), BF16)
    return _gridded(
        body, rider, name="fwd_outproj_ffn_act", grid=(s // tm,),
        in_specs=[_rows(tm, ATTN_WIDTH), _rows(tm, POOL_WIDTH), _resident((D_MODEL, D_MODEL)), _full((1, D_MODEL)),
                  _rows(tm, D_MODEL), _full((1, D_MODEL)), _resident((D_FF, D_MODEL)), _resident((D_FF, D_MODEL))],
        out_specs=[_rows(tm, D_MODEL)] + [_rows(tm, D_FF)] * 3,
        out_shape=[jax.ShapeDtypeStruct((s, D_MODEL), F32)] + [act_shape] * 3,
        scratch_shapes=[], args=(attn, pool, w_out, b_out, x, g_ffn, wg_t, wu_t))


def _fwd_down_loss(act, x2, wd, g_final, target):
    s = x2.shape[0]
    tm = _token_tile(s)
    last = s // tm - 1

    def body(a_ref, x2_ref, wd_ref, g_ref, t_ref, dx3_ref, sq_ref, dg_ref, dwd_ref, acc_ref, sem):
        @pl.when(pl.program_id(0) == 0)
        def _():
            sq_ref[...] = jnp.zeros_like(sq_ref)
            dg_ref[...] = jnp.zeros_like(dg_ref)
            acc_ref[...] = jnp.zeros_like(acc_ref)

        halves = [slice(0, tm // 2), slice(tm // 2, tm)]
        g = g_ref[...]
        down = [_nn(a_ref[rows, :], wd_ref[...]) for rows in halves]
        for half, (rows, y) in enumerate(zip(halves, down)):
            x3 = x2_ref[rows, :] + y
            n, r = _rms(x3)
            diff = n * g - t_ref[rows, :]
            sq_ref[...] += jnp.sum(diff * diff, axis=0, keepdims=True)
            dx3, dg = _rms_bwd(diff * (1.0 / D_MODEL), n, r, g)
            dg_ref[...] += dg
            dx3_ref[rows, :] = dx3
            dx3b = dx3.astype(BF16)
            if half == 0:
                for m0 in range(0, D_FF, TN_ROW_CHUNK):
                    acc_ref[m0:m0 + TN_ROW_CHUNK, :] += _tn(a_ref[rows, m0:m0 + TN_ROW_CHUNK], dx3b)
            else:
                _accumulate_tn(acc_ref, a_ref.at[rows, :], dx3b, dwd_ref, sem, pl.program_id(0) == last)

    return pl.pallas_call(
        body, name="fwd_down_loss", grid=(s // tm,),
        in_specs=[_rows(tm, D_FF), _rows(tm, D_MODEL), _resident((D_FF, D_MODEL)), _full((1, D_MODEL)), _rows(tm, D_MODEL)],
        out_specs=[_rows(tm, D_MODEL), _full((1, D_MODEL)), _full((1, D_MODEL)), HBM],
        out_shape=[jax.ShapeDtypeStruct((s, D_MODEL), F32),
                   jax.ShapeDtypeStruct((1, D_MODEL), F32), jax.ShapeDtypeStruct((1, D_MODEL), F32),
                   jax.ShapeDtypeStruct((D_FF, D_MODEL), F32)],
        scratch_shapes=[pltpu.VMEM((D_FF, D_MODEL), F32), pltpu.SemaphoreType.DMA],
        compiler_params=_params("arbitrary"),
    )(act, x2, wd, g_final, target)


FFN_BWD_TILE = 256


def _bwd_ffn(dx3, gate, up, x2, wd, wg_t, wu_t, g_ffn, rider=None):
    s = x2.shape[0]
    tm = min(FFN_BWD_TILE, s)
    last = s // tm - 1

    def body(dx3_ref, gate_ref, up_ref, x2_ref, wd_ref, wg_ref, wu_ref, g_ref,
             dx2_ref, dg_ref, db_ref, dwg_ref, dwu_ref, dgate_ref, dup_ref, accg_ref, accu_ref, sems):
        @pl.when(pl.program_id(0) == 0)
        def _():
            dg_ref[...] = jnp.zeros_like(dg_ref)
            db_ref[...] = jnp.zeros_like(db_ref)
            accg_ref[...] = jnp.zeros_like(accg_ref)
            accu_ref[...] = jnp.zeros_like(accu_ref)

        dx3b = dx3_ref[...].astype(BF16)
        g = g_ref[...]
        n, r = _rms(x2_ref[...])
        h = (n * g).astype(BF16)
        ahead = _nt(dx3b, wd_ref[0:FF_CHUNK, :])
        for c0 in range(0, D_FF, FF_CHUNK):
            cols = slice(c0, c0 + FF_CHUNK)
            dact = ahead
            if c0 + FF_CHUNK < D_FF:
                ahead = _nt(dx3b, wd_ref[c0 + FF_CHUNK:c0 + 2 * FF_CHUNK, :])
            gate = gate_ref[:, cols].astype(F32)
            up = up_ref[:, cols].astype(F32)
            sig = jax.nn.sigmoid(gate)
            silu = gate * sig
            dup = (dact * silu).astype(BF16)
            dgate = (dact * up * (sig + silu * (1.0 - sig))).astype(BF16)
            dup_ref[:, cols] = dup
            dgate_ref[:, cols] = dgate
            accg_ref[cols, :] += _tn(dgate, h)
            accu_ref[cols, :] += _tn(dup, h)
        dh2 = _nn(dgate_ref[...], wg_ref[...]) + _nn(dup_ref[...], wu_ref[...])
        dx, dg = _rms_bwd(dh2, n, r, g)
        dx2 = dx3_ref[...] + dx
        dg_ref[...] += dg
        db_ref[...] += jnp.sum(dx2, axis=0, keepdims=True)
        dx2_ref[...] = dx2

        @pl.when(pl.program_id(0) == last)
        def _():
            outs = [pltpu.make_async_copy(accg_ref, dwg_ref, sems.at[0]), pltpu.make_async_copy(accu_ref, dwu_ref, sems.at[1])]
            for cp in outs:
                cp.start()
            for cp in outs:
                cp.wait()

    grad_shape = jax.ShapeDtypeStruct((D_FF, D_MODEL), F32)
    weight = _resident((D_FF, D_MODEL))
    return _gridded(
        body, rider, name="bwd_ffn", grid=(s // tm,),
        in_specs=[_rows(tm, D_MODEL), _rows(tm, D_FF), _rows(tm, D_FF),
                  _rows(tm, D_MODEL), weight, weight, weight, _full((1, D_MODEL))],
        out_specs=[_rows(tm, D_MODEL), _full((1, D_MODEL)), _full((1, D_MODEL)), HBM, HBM],
        out_shape=[jax.ShapeDtypeStruct((s, D_MODEL), F32),
                   jax.ShapeDtypeStruct((1, D_MODEL), F32), jax.ShapeDtypeStruct((1, D_MODEL), F32), grad_shape, grad_shape],
        scratch_shapes=[pltpu.VMEM((tm, D_FF), BF16), pltpu.VMEM((tm, D_FF), BF16),
                        pltpu.VMEM((D_FF, D_MODEL), F32), pltpu.VMEM((D_FF, D_MODEL), F32), pltpu.SemaphoreType.DMA((2,))],
        args=(dx3, gate, up, x2, wd, wg_t, wu_t, g_ffn))


def _bwd_outproj_pool(dx2, attn, pool, mixed, w_out, w_pool, b_pool, pool_scale, rider=None):
    s = dx2.shape[0]
    tm = min(2 * _token_tile(s), s)
    nt = s // tm

    def body(dx_ref, a_ref, p_ref, mix_ref, w_ref, wp_ref, bp_ref, ps_ref,
             dattn_ref, du_ref, dwout_ref, dwp_ref, dbp_ref, dps_ref, head_ref):
        step = pl.program_id(0)
        tile = nt - 1 - step

        @pl.when(step == 0)
        def _():
            head_ref[...] = jnp.zeros_like(head_ref)
            dwout_ref[...] = jnp.zeros_like(dwout_ref)
            dwp_ref[...] = jnp.zeros_like(dwp_ref)
            dbp_ref[...] = jnp.zeros_like(dbp_ref)
            dps_ref[...] = jnp.zeros_like(dps_ref)

        dx = dx_ref[...].astype(BF16)
        dwout_ref[:ATTN_WIDTH, :] += _tn(a_ref[...], dx)
        dwout_ref[ATTN_WIDTH:, :] += _tn(p_ref[...], dx)
        dcat = _nt(dx, w_ref[...])
        dattn_ref[...] = dcat[:, :ATTN_WIDTH].astype(BF16)
        dpool = dcat[:, ATTN_WIDTH:]
        pos = lax.broadcasted_iota(jnp.int32, (tm, POOL_GROUP_WIDTH), 0) + tile * tm
        head = head_ref[...]
        n_ext = tm + POOL_HALO
        for g, size in enumerate(POOL_SIZES):
            cols = slice(g * POOL_GROUP_WIDTH, (g + 1) * POOL_GROUP_WIDTH)
            mixed_g = mix_ref[:, cols]
            pre = _nn(mixed_g, wp_ref[g]) + bp_ref[:, cols]
            dy = dpool[:, cols]
            dps_ref[:, cols] += jnp.sum(dy * pre, axis=0, keepdims=True)
            dpre = dy * ps_ref[:, cols]
            dbp_ref[:, cols] += jnp.sum(dpre, axis=0, keepdims=True)
            dpre_b = dpre.astype(BF16)
            dwp_ref[g] += _tn(mixed_g, dpre_b)
            dmixed = _nt(dpre_b, wp_ref[g])
            w = _window_mean(dmixed, pos, size)
            head_ref[:, cols] = w[:POOL_HALO, :]
            a = jnp.concatenate([w, head[:, cols]], axis=0)
            shift = 1
            while shift < size:
                a = a + pltpu.roll(a, n_ext - shift, 0)
                shift *= 2
            du_ref[:, cols] = (a[:tm, :] - dmixed).astype(BF16)

    rev = lambda w: pl.BlockSpec((tm, w), lambda i: (nt - 1 - i, 0))
    return _gridded(
        body, rider, name="bwd_outproj_pool", grid=(nt,),
        in_specs=[rev(D_MODEL), rev(ATTN_WIDTH), rev(POOL_WIDTH), rev(POOL_WIDTH), _full((D_MODEL, D_MODEL)),
                  _full((4, POOL_GROUP_WIDTH, POOL_GROUP_WIDTH)), _full((1, POOL_WIDTH)), _full((1, POOL_WIDTH))],
        out_specs=[rev(ATTN_WIDTH), rev(POOL_WIDTH), _full((D_MODEL, D_MODEL)),
                   _full((4, POOL_GROUP_WIDTH, POOL_GROUP_WIDTH)), _full((1, POOL_WIDTH)), _full((1, POOL_WIDTH))],
        out_shape=[jax.ShapeDtypeStruct((s, ATTN_WIDTH), BF16), jax.ShapeDtypeStruct((s, POOL_WIDTH), BF16),
                   jax.ShapeDtypeStruct((D_MODEL, D_MODEL), F32),
                   jax.ShapeDtypeStruct((4, POOL_GROUP_WIDTH, POOL_GROUP_WIDTH), F32),
                   jax.ShapeDtypeStruct((1, POOL_WIDTH), F32), jax.ShapeDtypeStruct((1, POOL_WIDTH), F32)],
        scratch_shapes=[pltpu.VMEM((POOL_HALO, POOL_WIDTH), F32)],
        args=(dx2, attn, pool, mixed, w_out, w_pool, b_pool, pool_scale))


def _fold_spread(t):
    low = lax.broadcasted_iota(jnp.int32, (2 * BLOCK, LANES), 1) < HEAD_DIM
    kept = jnp.where(low, t[:2 * BLOCK, :], t[2 * BLOCK:, :])
    return kept + pltpu.roll(kept, HEAD_DIM, 1)


def _attn_bwd(q, kz, vz, dattn, lse, sinks, rider=None):
    s = q.shape[0]
    tq = min(ATTN_TILE, s)
    nt = s // tq
    per = tq // BLOCK

    def body(q_ref, k_ref, kp_ref, v_ref, vp_ref, do_ref, lse_ref, sink_ref,
             dq_ref, dk_ref, dv_ref, dsink_ref, dk_acc, dv_acc, dk_carry, dv_carry):
        step = pl.program_id(0)

        @pl.when(step == 0)
        def _():
            dk_carry[...] = jnp.zeros_like(dk_carry)
            dv_carry[...] = jnp.zeros_like(dv_carry)
            dsink_ref[...] = jnp.zeros_like(dsink_ref)

        dk_acc[0:tq, :] = jnp.zeros((tq, KV_WIDTH), F32)
        dv_acc[0:tq, :] = jnp.zeros((tq, KV_WIDTH), F32)
        dk_acc[tq:, :] = dk_carry[...]
        dv_acc[tq:, :] = dv_carry[...]
        first, band = _band_masks(nt - 1 - step)
        low = lax.broadcasted_iota(jnp.int32, (2 * BLOCK, LANES), 1) < HEAD_DIM
        chains = [(b, kv) for b in range(per) for kv in range(N_KV_HEADS)]

        def operands(b, kv):
            rows = slice(b * BLOCK, (b + 1) * BLOCK)
            qab = _stack_pair(q_ref, rows, kv)
            doab = _stack_pair(do_ref, rows, kv)
            kzb = _band(k_ref, kp_ref, b, kv)
            return qab, doab, kzb, _nt(kzb, qab), _nt(_band(v_ref, vp_ref, b, kv), doab)

        folded = {}

        def finish(b, kv, dqab, dkz, dvz):
            rows = slice(b * BLOCK, (b + 1) * BLOCK)
            dq_ref[rows, kv * PAIR:kv * PAIR + LANES] = dqab[:BLOCK] * Q_SCALE
            dq_ref[rows, kv * PAIR + LANES:(kv + 1) * PAIR] = dqab[BLOCK:] * Q_SCALE
            folded[kv] = (_fold_spread(dkz), _fold_spread(dvz))
            if kv == N_KV_HEADS - 1:
                band_rows = slice(b * BLOCK, (b + 2) * BLOCK)
                dk_acc[band_rows, :] += jnp.where(low, folded[0][0], folded[1][0])
                dv_acc[band_rows, :] += jnp.where(low, folded[0][1], folded[1][1])

        ahead = operands(*chains[0])
        behind = None
        for n, (b, kv) in enumerate(chains):
            rows = slice(b * BLOCK, (b + 1) * BLOCK)
            mask = first if b == 0 else band
            qab, doab, kzb, st, dpt = ahead
            if n + 1 < len(chains):
                ahead = operands(*chains[n + 1])
            probs, dscores = [], []
            for half in range(2):
                top, bottom = _pair_heads(kv, half)
                keys = slice(half * 2 * BLOCK, (half + 1) * 2 * BLOCK)
                lse_h = _per_query(lse_ref, rows, top, bottom)
                p = jnp.where(mask[keys, :], jnp.exp(st[keys, :] - lse_h), 0.0)
                dph = dpt[keys, :]
                delta = _reduce_rows(p * dph, jnp.add, jnp.sum)
                probs.append(p.astype(BF16))
                dscores.append((p * (dph - delta)).astype(BF16))
                leak = jnp.exp(_sink_per_query(sink_ref, top, bottom) - lse_h) * delta
                dsink_ref[:, top:top + 1] -= jnp.sum(leak[:, :BLOCK], axis=1, keepdims=True)
                dsink_ref[:, bottom:bottom + 1] -= jnp.sum(leak[:, BLOCK:], axis=1, keepdims=True)
            ds = jnp.concatenate(dscores, axis=0)
            results = (_tn(ds, kzb), _nn(ds, qab), _nn(jnp.concatenate(probs, axis=0), doab))
            if behind is not None:
                finish(*behind)
            behind = (b, kv, *results)
        finish(*behind)
        dk_ref[...] = dk_acc[BLOCK:, :]
        dv_ref[...] = dv_acc[BLOCK:, :]
        dk_carry[...] = dk_acc[0:BLOCK, :]
        dv_carry[...] = dv_acc[0:BLOCK, :]

    cur = lambda w: pl.BlockSpec((tq, w), lambda i: (nt - 1 - i, 0))
    prev = pl.BlockSpec((BLOCK, KV_SPREAD), lambda i: (jnp.maximum(per * (nt - 1 - i) - 1, 0), 0))
    acc = pltpu.VMEM((tq + BLOCK, KV_WIDTH), F32)
    carry = pltpu.VMEM((BLOCK, KV_WIDTH), F32)
    return _gridded(
        body, rider, name="attn_bwd", grid=(nt,),
        in_specs=[cur(ATTN_WIDTH), cur(KV_SPREAD), prev, cur(KV_SPREAD), prev, cur(ATTN_WIDTH),
                  pl.BlockSpec((N_Q_HEADS, tq), lambda i: (0, nt - 1 - i)), _full((1, N_Q_HEADS))],
        out_specs=[cur(ATTN_WIDTH), cur(KV_WIDTH), cur(KV_WIDTH), _full((1, N_Q_HEADS))],
        out_shape=[jax.ShapeDtypeStruct((s, ATTN_WIDTH), F32), jax.ShapeDtypeStruct((s, KV_WIDTH), F32),
                   jax.ShapeDtypeStruct((s, KV_WIDTH), F32), jax.ShapeDtypeStruct((1, N_Q_HEADS), F32)],
        scratch_shapes=[acc, acc, carry, carry],
        args=(q, kz, kz, vz, vz, dattn, lse, sinks))


def _bwd_inproj(dq, dk, dv, du, cos, sin, win_t, x, g_mix, dx2):
    s = x.shape[0]
    tm = _token_tile(s)

    def body(dq_ref, dk_ref, dv_ref, du_ref, cos_ref, sin_ref, w_ref, x_ref, g_ref, dx2_ref,
             dx_ref, dw_ref, db_ref, dg_ref):
        @pl.when(pl.program_id(0) == 0)
        def _():
            dw_ref[...] = jnp.zeros_like(dw_ref)
            db_ref[...] = jnp.zeros_like(db_ref)
            dg_ref[...] = jnp.zeros_like(dg_ref)

        cos_t, sin_t = _tile_tables(cos_ref, sin_ref)
        g = g_ref[...]
        staged = []
        for rows in (slice(0, tm // 2), slice(tm // 2, tm)):
            dz32 = jnp.concatenate([_rope_bwd(dq_ref[rows, :], cos_t[rows, :], sin_t[rows, :]),
                                    _rope_bwd(dk_ref[rows, :], cos_t[rows, :], sin_t[rows, :]),
                                    dv_ref[rows, :], du_ref[rows, :].astype(F32)], axis=1)
            db_ref[...] += jnp.sum(dz32, axis=0, keepdims=True)
            dz = dz32.astype(BF16)
            n, r = _rms(x_ref[rows, :])
            staged.append((rows, dz, (n * g).astype(BF16), n, r, _nn(dz, w_ref[...])))
        dz = jnp.concatenate([part[1] for part in staged], axis=0)
        h = jnp.concatenate([part[2] for part in staged], axis=0)
        for m0 in range(0, IN_WIDTH, TN_ROW_CHUNK):
            dw_ref[m0:m0 + TN_ROW_CHUNK, :] += _tn(dz[:, m0:m0 + TN_ROW_CHUNK], h)
        for rows, _, _, n, r, dh in staged:
            dx, dg = _rms_bwd(dh, n, r, g)
            dg_ref[...] += dg
            dx_ref[rows, :] = dx2_ref[rows, :] + dx

    return _gridded(
        body, None, name="bwd_inproj", grid=(s // tm,),
        in_specs=[_rows(tm, ATTN_WIDTH), _rows(tm, KV_WIDTH), _rows(tm, KV_WIDTH), _rows(tm, POOL_WIDTH),
                  _full((4, tm, LANES)), pl.BlockSpec((1, 2, LANES), lambda i: (i, 0, 0)), _full((IN_WIDTH, D_MODEL)), _rows(tm, D_MODEL),
                  _full((1, D_MODEL)), _rows(tm, D_MODEL)],
        out_specs=[_rows(tm, D_MODEL), _full((IN_WIDTH, D_MODEL)), _full((1, IN_WIDTH)), _full((1, D_MODEL))],
        out_shape=[jax.ShapeDtypeStruct((s, D_MODEL), F32), jax.ShapeDtypeStruct((IN_WIDTH, D_MODEL), F32),
                   jax.ShapeDtypeStruct((1, IN_WIDTH), F32), jax.ShapeDtypeStruct((1, D_MODEL), F32)],
        scratch_shapes=[], args=(dq, dk, dv, du, cos, sin, win_t, x, g_mix, dx2))


def _rope_tables(s, tm):
    inv_freq = jnp.tile(1.0 / (ROPE_THETA ** (jnp.arange(0, HEAD_DIM, 2, dtype=F32) / HEAD_DIM)), 4)
    sign = jnp.tile(jnp.repeat(jnp.array([-1.0, 1.0], F32), HEAD_DIM // 2), 2)
    within = jnp.arange(tm, dtype=F32)[:, None] * inv_freq[None, :]
    start = jnp.arange(0, s, tm, dtype=F32)[:, None] * inv_freq[None, :]
    cos, sin = jnp.cos(within), jnp.sin(within)
    base = jnp.stack([cos, sin, sign * sin, sign * cos])
    return base, jnp.stack([jnp.cos(start), jnp.sin(start)], axis=1)


def _place():
    return lax.axis_index("x"), lax.axis_index("y"), lax.axis_index("c")


def _other_chips(x, y):
    return [(1 - x, y), (x, 1 - y), (1 - x, 1 - y)]


def _gather_rider(blocks, relay_early=False):
    nm = len(blocks)

    def plan(ins, outs, sems):
        send_sems, recv_sems, local_sems = sems
        x, y, c = _place()
        me, sibling = (x, y, c), (x, y, 1 - c)
        chips = _other_chips(x, y)

        def rows(m, px, py, pc):
            r = ins[m].shape[0]
            return outs[m].at[pl.ds((4 * px + 2 * py + pc) * r, r), :]

        def copy(m, k, block, to, src=None):
            return pltpu.make_async_remote_copy(
                src_ref=rows(m, *block) if src is None else src, dst_ref=rows(m, *block),
                send_sem=send_sems.at[k * nm + m], recv_sem=recv_sems.at[k * nm + m],
                device_id=to, device_id_type=MESH)

        mine = [pltpu.make_async_copy(ins[m], rows(m, *me), local_sems.at[m]) for m in range(nm)]
        first = [copy(m, 0, me, sibling, src=ins[m]) for m in range(nm)]
        first += [copy(m, 1 + j, me, (*chip, c), src=ins[m]) for j, chip in enumerate(chips) for m in range(nm)]
        return me, sibling, chips, copy, mine, first

    def start(ins, outs, sems):
        *_, mine, first = plan(ins, outs, sems)
        for cp in mine + first:
            cp.start()

    def passed_on(ins, outs, sems):
        me, sibling, chips, copy, _, _ = plan(ins, outs, sems)
        return [copy(m, 4 + j, (*chip, me[2]), sibling) for j, chip in enumerate(chips) for m in range(nm)]

    def relay(ins, outs, sems):
        me, _, chips, copy, _, _ = plan(ins, outs, sems)
        forwards = passed_on(ins, outs, sems)
        for j, chip in enumerate(chips):
            for m in range(nm):
                copy(m, 1 + j, (*chip, me[2]), me).wait_recv()
                forwards[j * nm + m].start()

    def finish(ins, outs, sems):
        me, sibling, chips, copy, mine, first = plan(ins, outs, sems)
        for m in range(nm):
            copy(m, 0, sibling, me).wait_recv()
        for j, chip in enumerate(chips):
            for m in range(nm):
                copy(m, 4 + j, (*chip, 1 - me[2]), me).wait_recv()
        for cp in first + passed_on(ins, outs, sems):
            cp.wait_send()
        for cp in mine:
            cp.wait()

    return _Rider(
        arrays=list(blocks), out_shape=[jax.ShapeDtypeStruct((N_DEV * b.shape[0], b.shape[1]), b.dtype) for b in blocks],
        sems=[pltpu.SemaphoreType.DMA((7 * nm,)), pltpu.SemaphoreType.DMA((7 * nm,)), pltpu.SemaphoreType.DMA((nm,))],
        start=start, finish=finish, relay=relay, relay_early=relay_early)


def _exchange_rider(copies_of, arrays, out_shape, n_copies):
    def copies(ins, outs, sems):
        send_sems, recv_sems = sems
        return [pltpu.make_async_remote_copy(src_ref=src, dst_ref=dst, send_sem=send_sems.at[k], recv_sem=recv_sems.at[k],
                                             device_id=to, device_id_type=MESH)
                for k, (src, dst, to) in enumerate(copies_of(ins, outs))]

    def start(ins, outs, sems):
        for cp in copies(ins, outs, sems):
            cp.start()

    def finish(ins, outs, sems):
        cps = copies(ins, outs, sems)
        for cp in cps:
            cp.wait_recv()
        for cp in cps:
            cp.wait_send()

    return _Rider(arrays=list(arrays), out_shape=out_shape,
                  sems=[pltpu.SemaphoreType.DMA((n_copies,)), pltpu.SemaphoreType.DMA((n_copies,))], start=start, finish=finish)


def _sibling_rider(grads):
    def copies_of(ins, outs):
        x, y, c = _place()
        for g_ref, o_ref in zip(ins, outs):
            r = g_ref.shape[0] // N_DEV
            for q in range(N_CHIPS):
                yield g_ref.at[pl.ds((2 * q + 1 - c) * r, r), :], o_ref.at[pl.ds(q * r, r), :], (x, y, 1 - c)

    return _exchange_rider(copies_of, grads, [jax.ShapeDtypeStruct((g.shape[0] // 2, g.shape[1]), F32) for g in grads],
                           len(grads) * N_CHIPS)


def _chip_sums(grads, from_sibling, place):
    n = len(grads)
    shapes = [(g.shape[0] // N_DEV, g.shape[1]) for g in grads]

    def body(place_ref, *refs):
        for k in range(n):
            g_ref, s_ref, wire_ref, own_ref = refs[k], refs[n + k], refs[2 * n + 2 * k], refs[2 * n + 2 * k + 1]
            total = g_ref[...] + s_ref[...]
            wire_ref[...] = total.astype(BF16)

            @pl.when(pl.program_id(0) == place_ref[1])
            def _(own_ref=own_ref, total=total):
                own_ref[...] = total

    grid_spec = pltpu.PrefetchScalarGridSpec(
        num_scalar_prefetch=1, grid=(N_CHIPS,),
        in_specs=[pl.BlockSpec(shape, lambda q, p: (2 * q + p[0], 0)) for shape in shapes]
        + [pl.BlockSpec(shape, lambda q, p: (q, 0)) for shape in shapes],
        out_specs=[spec for shape in shapes
                   for spec in (pl.BlockSpec(shape, lambda q, p: (q, 0)), pl.BlockSpec(shape, lambda q, p: (0, 0)))])
    out = pl.pallas_call(
        body, name="grad_chip_sum", grid_spec=grid_spec,
        out_shape=[s for r, w in shapes
                   for s in (jax.ShapeDtypeStruct((N_CHIPS * r, w), BF16), jax.ShapeDtypeStruct((r, w), F32))],
        compiler_params=_params("arbitrary"),
    )(place, *grads, *from_sibling)
    return [(out[2 * k], out[2 * k + 1]) for k in range(n)]


def _chip_sum(grad, from_sibling, place):
    return _chip_sums([grad], [from_sibling], place)[0]


def _chips_rider(wires):
    def copies_of(ins, outs):
        x, y, c = _place()
        for w_ref, o_ref in zip(ins, outs):
            r = w_ref.shape[0] // N_CHIPS
            for j, (px, py) in enumerate(_other_chips(x, y)):
                yield w_ref.at[pl.ds((2 * px + py) * r, r), :], o_ref.at[pl.ds(j * r, r), :], (px, py, c)

    return _exchange_rider(copies_of, wires,
                           [jax.ShapeDtypeStruct((3 * (w.shape[0] // N_CHIPS), w.shape[1]), BF16) for w in wires], len(wires) * 3)


SEM = pl.BlockSpec(memory_space=pltpu.SEMAPHORE)
DATAFLOW = pltpu.SideEffectType.DATAFLOW_SIDE_EFFECTING


def _last_exchange_copies(wire_refs, late_ref, land_refs, land_late_ref, send_sems, recv_sems):
    x, y, c = _place()
    ends = []
    for w_ref, o_ref in zip(wire_refs, land_refs):
        r = w_ref.shape[0] // N_CHIPS
        for j, (px, py) in enumerate(_other_chips(x, y)):
            ends.append((w_ref.at[pl.ds((2 * px + py) * r, r), :], o_ref.at[pl.ds(j * r, r), :], (px, py, c)))
    rows = late_ref.shape[0]
    mine = land_late_ref.at[pl.ds((4 * x + 2 * y + c) * rows, rows), :]
    peers = [(x, y, 1 - c)] + [(px, py, pc) for px, py in _other_chips(x, y) for pc in (c, 1 - c)]
    ends += [(late_ref, mine, peer) for peer in peers]
    return [pltpu.make_async_remote_copy(src_ref=src, dst_ref=dst, send_sem=send_sems[k], recv_sem=recv_sems[k],
                                         device_id=to, device_id_type=MESH) for k, (src, dst, to) in enumerate(ends)]


N_LAST_COPIES = 2 * 3 + (N_DEV - 1)


def _last_exchange_start(wires, late):
    n = N_LAST_COPIES
    lands = [lax.empty((3 * (w.shape[0] // N_CHIPS), w.shape[1]), w.dtype) for w in wires]
    land_late = lax.empty((N_DEV * late.shape[0], late.shape[1]), late.dtype)

    def body(wout_ref, win_ref, late_ref, land_wout_ref, land_win_ref, land_late_ref, *outs):
        send_sems, recv_sems, token_ref = outs[:n], outs[n:2 * n], outs[-1]
        for cp in _last_exchange_copies([wout_ref, win_ref], late_ref, [land_wout_ref, land_win_ref], land_late_ref,
                                        send_sems, recv_sems):
            cp.start()
        token_ref[...] = jnp.zeros_like(token_ref)

    operands = [pltpu.with_memory_space_constraint(a, pltpu.HBM) for a in (*wires, late, *lands, land_late)]
    thru = [pltpu.HBM(a.shape, a.dtype) for a in operands]
    out = pl.pallas_call(
        body, name="last_exchange_start",
        out_shape=[pltpu.SemaphoreType.DMA(())] * (2 * n) + thru + [jax.ShapeDtypeStruct((8, LANES), F32)],
        in_specs=[HBM] * 6, out_specs=[SEM] * (2 * n) + [HBM] * 6 + [VMEM],
        input_output_aliases={i: 2 * n + i for i in range(6)},
        compiler_params=pltpu.CompilerParams(has_side_effects=DATAFLOW),
    )(*operands)
    return out[:2 * n], out[2 * n:2 * n + 6], out[-1]


def _last_exchange_wait(sems, buffers, after):
    n = N_LAST_COPIES

    def body(wout_ref, win_ref, late_ref, land_wout_ref, land_win_ref, land_late_ref, *rest):
        send_sems, recv_sems = rest[:n], rest[n:2 * n]
        for cp in _last_exchange_copies([wout_ref, win_ref], late_ref, [land_wout_ref, land_win_ref], land_late_ref,
                                        send_sems, recv_sems):
            cp.wait_send()
            cp.wait_recv()

    out = pl.pallas_call(
        body, name="last_exchange_wait", out_shape=[pltpu.HBM(a.shape, a.dtype) for a in buffers],
        in_specs=[HBM] * 6 + [SEM] * (2 * n) + [pl.BlockSpec(memory_space=pl.ANY)], out_specs=[HBM] * 6,
        input_output_aliases={i: i for i in range(6)},
        compiler_params=pltpu.CompilerParams(has_side_effects=DATAFLOW),
    )(*buffers, *sems, after)
    return out[3], out[4], out[5]


def _adamw_math(w, g, m, v):
    m = ADAM_B1 * m + (1.0 - ADAM_B1) * g
    v = ADAM_B2 * v + (1.0 - ADAM_B2) * jnp.square(g)
    m_hat = m / (1.0 - ADAM_B1 ** ADAM_STEP)
    v_hat = v / (1.0 - ADAM_B2 ** ADAM_STEP)
    delta = -ADAM_LR * (m_hat / (jnp.sqrt(v_hat) + ADAM_EPS) + ADAM_WD * w)
    return delta, m, v


def _reduce_adamw(own, received, w, m, v, after):
    r = own.shape[0]

    def body(own_ref, rec_ref, w_ref, m_ref, v_ref, after_ref, g_ref, d_ref, nm_ref, nv_ref):
        del after_ref
        g = ((own_ref[...] + rec_ref[0:r, :].astype(F32)) + rec_ref[r:2 * r, :].astype(F32)) + rec_ref[2 * r:, :].astype(F32)
        g_ref[...] = g
        d_ref[...], nm_ref[...], nv_ref[...] = _adamw_math(w_ref[...], g, m_ref[...], v_ref[...])

    shape = jax.ShapeDtypeStruct(own.shape, F32)
    return pl.pallas_call(
        body, name="reduce_adamw", in_specs=[VMEM] * 5 + [pl.BlockSpec(memory_space=pl.ANY)], out_specs=[VMEM] * 4,
        out_shape=[shape] * 4, compiler_params=_params(),
    )(own, received, w, m, v, after)


SMALL_WIDE = (("w_pool", 65536),)
SMALL_EARLY = (("b_pool", 512), ("pool_scale", 512), ("b_out", 1024), ("g_ffn", 1024), ("g_final", 1024), ("loss", 1024))
SMALL_LATE = (("sinks", 8), ("g_mix", 1024), ("b_in", 1280))
SMALL = SMALL_WIDE + SMALL_EARLY + SMALL_LATE


def _small_rows(size):
    return -(-size // (8 * LANES)) * 8


def _pack_small(values, entries=SMALL):
    parts = []
    for name, size in entries:
        flat = values[name].reshape(-1).astype(F32)
        parts.append(jnp.pad(flat, (0, _small_rows(size) * LANES - size)).reshape(-1, LANES))
    return jnp.concatenate(parts, axis=0)


def _unpack_small(packed, shapes):
    out, row = {}, 0
    for name, size in SMALL:
        rows = _small_rows(size)
        if name in shapes:
            out[name] = packed[row:row + rows].reshape(-1)[:size].reshape(shapes[name])
        row += rows
    return out


def _small_sum_adamw(gathered, w, m, v):
    n = len(gathered)

    def body(*refs):
        w_ref, m_ref, v_ref, g_ref, d_ref, nm_ref, nv_ref = refs[n:]

        def total(ref):
            rows = ref.shape[0] // N_DEV
            acc = ref[0:rows, :].astype(F32)
            for dev in range(1, N_DEV):
                acc = acc + ref[dev * rows:(dev + 1) * rows, :].astype(F32)
            return acc

        g = jnp.concatenate([total(ref) for ref in refs[:n]], axis=0)
        g_ref[...] = g
        d_ref[...], nm_ref[...], nv_ref[...] = _adamw_math(w_ref[...], g, m_ref[...], v_ref[...])

    shape = jax.ShapeDtypeStruct(w.shape, F32)
    return pl.pallas_call(
        body, name="small_sum_adamw", in_specs=[VMEM] * (n + 3), out_specs=[VMEM] * 4, out_shape=[shape] * 4,
        compiler_params=_params(),
    )(*gathered, w, m, v)


def kernel(x, g_mix, w_in, b_in, sinks, w_pool, b_pool, pool_scale, w_out, b_out, g_ffn, w_gate, w_up, w_down, g_final, loss_target, m_g_mix, m_w_in, m_b_in, m_sinks, m_w_pool, m_b_pool, m_pool_scale, m_w_out, m_b_out, m_g_ffn, m_w_gate, m_w_up, m_w_down, m_g_final, v_g_mix, v_w_in, v_b_in, v_sinks, v_w_pool, v_b_pool, v_pool_scale, v_w_out, v_b_out, v_g_ffn, v_w_gate, v_w_up, v_w_down, v_g_final):
    weights = dict(g_mix=g_mix, w_in=w_in, b_in=b_in, sinks=sinks, w_pool=w_pool, b_pool=b_pool, pool_scale=pool_scale,
                   w_out=w_out, b_out=b_out, g_ffn=g_ffn, w_gate=w_gate, w_up=w_up, w_down=w_down, g_final=g_final)
    mom1 = dict(g_mix=m_g_mix, w_in=m_w_in, b_in=m_b_in, sinks=m_sinks, w_pool=m_w_pool, b_pool=m_b_pool,
                pool_scale=m_pool_scale, w_out=m_w_out, b_out=m_b_out, g_ffn=m_g_ffn, w_gate=m_w_gate, w_up=m_w_up,
                w_down=m_w_down, g_final=m_g_final)
    mom2 = dict(g_mix=v_g_mix, w_in=v_w_in, b_in=v_b_in, sinks=v_sinks, w_pool=v_w_pool, b_pool=v_b_pool,
                pool_scale=v_pool_scale, w_out=v_w_out, b_out=v_b_out, g_ffn=v_g_ffn, w_gate=v_w_gate, w_up=v_w_up,
                w_down=v_w_down, g_final=v_g_final)
    order = ("g_mix", "w_in", "b_in", "sinks", "w_pool", "b_pool", "pool_scale", "w_out", "b_out", "g_ffn",
             "w_gate", "w_up", "w_down", "g_final")
    big = ("w_in", "w_out", "w_gate", "w_up", "w_down")
    transposed = ("w_in", "w_gate", "w_up")

    def row_shard(name, a):
        return a[0].T if name in transposed else a[0]

    shard = {n: row_shard(n, weights[n]).astype(BF16) for n in big}
    xs, target = x[0], loss_target[0]
    cos, sin = _rope_tables(xs.shape[0], _token_tile(xs.shape[0]))
    wp_b = w_pool[0].astype(BF16)
    bp = b_pool.reshape(1, POOL_WIDTH)
    ps = pool_scale.reshape(1, POOL_WIDTH)
    g_fin = g_final.reshape(1, D_MODEL)
    px, py, pc = _place()
    place = jnp.stack([pc, 2 * px + py]).astype(jnp.int32)

    (win_t,) = _alone(_gather_rider([shard["w_in"]]), "gather_w_in")
    q, kz, vz, vt, mixed, pool, w_out_b, wg_t = _fwd_inproj(
        xs, g_mix, win_t, b_in, cos, sin, wp_b, bp, ps,
        rider=_gather_rider([shard["w_out"], shard["w_gate"]], relay_early=True))
    attn, lse, wu_t = _attn_fwd(q, kz, vt, sinks, rider=_gather_rider([shard["w_up"]]))
    x2, gate, up, act, wd = _fwd_outproj_ffn_act(attn, pool, w_out_b, b_out, xs, g_ffn, wg_t, wu_t,
                                                 rider=_gather_rider([shard["w_down"]], relay_early=True))
    dx3, sq, dg_final, d_wd = _fwd_down_loss(act, x2, wd, g_fin, target)

    dx2, dg_ffn, db_out, d_wg_t, d_wu_t, wd_sibling = _bwd_ffn(
        dx3, gate, up, x2, wd, wg_t, wu_t, g_ffn, rider=_sibling_rider([d_wd]))
    wd_sum = _chip_sum(d_wd, wd_sibling, place)
    in_grads = [d_wg_t, d_wu_t]
    dattn, du, d_wout, d_wpool, d_bpool, d_pscale, wd_received, *in_sibling = _bwd_outproj_pool(
        dx2, attn, pool, mixed, w_out_b, wp_b, bp, ps, rider=_join(_chips_rider([wd_sum[0]]), _sibling_rider(in_grads)))
    in_sums = _chip_sums(in_grads, in_sibling, place)
    small_wide = _pack_small(dict(w_pool=d_wpool), SMALL_WIDE).astype(BF16)
    small_early = _pack_small(dict(b_pool=d_bpool, pool_scale=d_pscale, b_out=db_out, g_ffn=dg_ffn,
                                   g_final=dg_final, loss=sq), SMALL_EARLY)
    dq, dk, dv, d_sinks, *landed = _attn_bwd(
        q, kz, vz, dattn, lse, sinks,
        rider=_join(_chips_rider([wire for wire, _ in in_sums]), _sibling_rider([d_wout]),
                    _gather_rider([small_wide, small_early])))
    ffn_sums, ffn_received = in_sums + [wd_sum], landed[:2] + [wd_received]
    wout_sum = _chip_sum(d_wout, landed[2], place)
    gathered_wide, gathered_early = landed[3], landed[4]
    dx, d_win_t, d_bin, d_gmix = _bwd_inproj(dq, dk, dv, du, cos, sin, win_t, xs, g_mix, dx2)
    (win_sibling,) = _alone(_sibling_rider([d_win_t]), "grad_exchange_sibling")
    win_sum = _chip_sum(d_win_t, win_sibling, place)
    small_late = _pack_small(dict(sinks=d_sinks, g_mix=d_gmix, b_in=d_bin), SMALL_LATE)

    grad, delta, new_m, new_v = {}, {}, {}, {}

    def update(n, own, rec, after):
        results = _reduce_adamw(own, rec, row_shard(n, weights[n]), row_shard(n, mom1[n]), row_shard(n, mom2[n]), after)
        grad[n], delta[n], new_m[n], new_v[n] = [(a.T if n in transposed else a)[None] for a in results]
        return results[0]

    sems, in_flight, after = _last_exchange_start([wout_sum[0], win_sum[0]], small_late)
    for n, (_, own), rec in zip(("w_gate", "w_up", "w_down"), ffn_sums, ffn_received):
        after = update(n, own, rec, after)
    wout_received, win_received, gathered_late = _last_exchange_wait(sems, in_flight, after)
    gathered_late = lax.dynamic_update_slice(gathered_late, small_late, ((4 * px + 2 * py + pc) * small_late.shape[0], 0))
    update("w_out", wout_sum[1], wout_received, after)
    update("w_in", win_sum[1], win_received, after)

    shapes = {n: weights[n].shape for n in order if n not in big}
    zero_loss = jnp.zeros((1, D_MODEL), F32)
    packed = _small_sum_adamw(
        [gathered_wide, gathered_early, gathered_late], _pack_small({**weights, "loss": zero_loss}),
        _pack_small({**mom1, "loss": zero_loss}), _pack_small({**mom2, "loss": zero_loss}))
    for store, pk in zip((grad, delta, new_m, new_v), packed):
        store.update(_unpack_small(pk, shapes))
    loss_rows = _unpack_small(packed[0], {"loss": (D_MODEL,)})["loss"]
    loss = (0.5 / D_MODEL) * jnp.sum(loss_rows)

    return (loss, dx[None], *[grad[n] for n in order], *[delta[n] for n in order],
            *[new_m[n] for n in order], *[new_v[n] for n in order])
```

```python
from typing import Any, Callable, NamedTuple, Sequence

import jax
import jax.numpy as jnp
from jax import lax
from jax.experimental import pallas as pl
from jax.experimental.pallas import tpu as pltpu

D_MODEL = 1024
ATTN_WIDTH = 512
KV_WIDTH = 128
POOL_WIDTH = 512
HEAD_DIM = 64
N_Q_HEADS = 8
N_KV_HEADS = 2
GQA_GROUP = 4
BLOCK = 128
POOL_SIZES = (2, 4, 8, 16)
POOL_GROUP_WIDTH = 128
POOL_HALO = 16
IN_WIDTH = 1280
D_FF = 2816
RMS_EPS = 1e-5
ROPE_THETA = 10000.0
Q_SCALE = HEAD_DIM ** -0.5

ADAM_LR = 0.001
ADAM_B1 = 0.9
ADAM_B2 = 0.999
ADAM_EPS = 1e-08
ADAM_WD = 0.01
ADAM_STEP = 10

N_DEV = 8
N_CHIPS = 4
LANES = 128
VMEM_LIMIT_BYTES = 60 * 1024 * 1024

F32 = jnp.float32
BF16 = jnp.bfloat16
MESH = pl.DeviceIdType.MESH
HBM = pl.BlockSpec(memory_space=pltpu.HBM)
VMEM = pl.BlockSpec(memory_space=pltpu.VMEM)


def _params(*semantics):
    return pltpu.CompilerParams(dimension_semantics=semantics or None, vmem_limit_bytes=VMEM_LIMIT_BYTES)


def _nn(a, b):
    return jnp.dot(a, b, preferred_element_type=F32)


def _nt(a, b):
    return lax.dot_general(a, b, (((1,), (1,)), ((), ())), preferred_element_type=F32)


def _tn(a, b):
    return lax.dot_general(a, b, (((0,), (0,)), ((), ())), preferred_element_type=F32)


def _full(shape):
    return pl.BlockSpec(shape, lambda *_: (0,) * len(shape))


def _rows(tm, width):
    return pl.BlockSpec((tm, width), lambda i, *_: (i, 0))


def _nothing(ins, outs, sems):
    del ins, outs, sems


RELAY_STEPS_BEFORE_LAST = 2


class _Rider(NamedTuple):
    arrays: Sequence[Any]
    out_shape: Sequence[Any]
    sems: Sequence[Any]
    start: Callable[..., None]
    finish: Callable[..., None]
    relay: Callable[..., None] = _nothing
    relay_early: bool = False


def _gridded(body, rider, *, name, grid, in_specs, out_specs, out_shape, scratch_shapes, args):
    params = _params("arbitrary")
    if rider is None:
        return pl.pallas_call(body, name=name, grid=grid, in_specs=in_specs, out_specs=out_specs, out_shape=out_shape,
                              scratch_shapes=scratch_shapes, compiler_params=params)(*args)
    bounds, total = [], 0
    for n in (len(in_specs), len(rider.arrays), len(out_specs), len(rider.out_shape), len(scratch_shapes), len(rider.sems)):
        bounds.append((total, total + n))
        total += n
    last = grid[0] - 1
    relay_step = max(last - RELAY_STEPS_BEFORE_LAST, 0) if rider.relay_early else last

    def riding(*refs):
        ins, r_ins, outs, r_outs, scratch, r_sems = (refs[lo:hi] for lo, hi in bounds)

        @pl.when(pl.program_id(0) == 0)
        def _():
            rider.start(r_ins, r_outs, r_sems)

        body(*ins, *outs, *scratch)

        @pl.when(pl.program_id(0) == relay_step)
        def _():
            rider.relay(r_ins, r_outs, r_sems)

        @pl.when(pl.program_id(0) == last)
        def _():
            rider.finish(r_ins, r_outs, r_sems)

    return pl.pallas_call(
        riding, name=name, grid=grid, in_specs=list(in_specs) + [HBM] * len(rider.arrays),
        out_specs=list(out_specs) + [HBM] * len(rider.out_shape), out_shape=list(out_shape) + list(rider.out_shape),
        scratch_shapes=list(scratch_shapes) + list(rider.sems), compiler_params=params)(*args, *rider.arrays)


def _join(*riders):
    def phase(which):
        def run(ins, outs, sems):
            i = o = s = 0
            for r in riders:
                ni, no, ns = len(r.arrays), len(r.out_shape), len(r.sems)
                getattr(r, which)(ins[i:i + ni], outs[o:o + no], sems[s:s + ns])
                i, o, s = i + ni, o + no, s + ns
        return run

    return _Rider(arrays=[a for r in riders for a in r.arrays], out_shape=[a for r in riders for a in r.out_shape],
                  sems=[a for r in riders for a in r.sems], start=phase("start"), finish=phase("finish"), relay=phase("relay"),
                  relay_early=all(r.relay_early for r in riders if r.relay is not _nothing))


def _alone(rider, name):
    n_in, n_out = len(rider.arrays), len(rider.out_shape)

    def body(*refs):
        parts = refs[:n_in], refs[n_in:n_in + n_out], refs[n_in + n_out:]
        rider.start(*parts)
        rider.relay(*parts)
        rider.finish(*parts)

    return pl.pallas_call(body, name=name, in_specs=[HBM] * n_in, out_specs=[HBM] * n_out, out_shape=list(rider.out_shape),
                          scratch_shapes=list(rider.sems))(*rider.arrays)


def _rot_half(t):
    n = t.shape[1]
    lane = lax.broadcasted_iota(jnp.int32, t.shape, 1)
    return jnp.where((lane % HEAD_DIM) < HEAD_DIM // 2, pltpu.roll(t, n - HEAD_DIM // 2, 1), pltpu.roll(t, HEAD_DIM // 2, 1))


def _tile_tables(base_ref, tile_ref):
    start = tile_ref[0]
    cos_0, sin_0 = start[0:1, :], start[1:2, :]
    return base_ref[0] * cos_0 - base_ref[1] * sin_0, base_ref[2] * cos_0 + base_ref[3] * sin_0


def _rope(t, cos, sin):
    reps = t.shape[1] // LANES
    if reps > 1:
        cos, sin = jnp.tile(cos, (1, reps)), jnp.tile(sin, (1, reps))
    return t * cos + _rot_half(t) * sin


def _rope_bwd(d, cos, sin):
    reps = d.shape[1] // LANES
    if reps > 1:
        cos, sin = jnp.tile(cos, (1, reps)), jnp.tile(sin, (1, reps))
    return d * cos + _rot_half(d * sin)


KV_SPREAD = 4 * LANES


def _spread_kv(t):
    low = lax.broadcasted_iota(jnp.int32, t.shape, 1) < HEAD_DIM
    swapped = pltpu.roll(t, HEAD_DIM, 1)
    zero = jnp.zeros_like(t)
    return jnp.concatenate([jnp.where(low, t, zero), jnp.where(low, zero, swapped),
                            jnp.where(low, swapped, zero), jnp.where(low, zero, t)], axis=1)


def _rms(x):
    r = lax.rsqrt(jnp.mean(x * x, axis=-1, keepdims=True) + RMS_EPS)
    return x * r, r


def _rms_bwd(dh, n, r, g):
    dn = dh * g
    dx = r * (dn - n * jnp.mean(dn * n, axis=-1, keepdims=True))
    return dx, jnp.sum(dh * n, axis=0, keepdims=True)


def _token_tile(s):
    return min(512, s)


def _window_mean(window_sum, pos, size):
    head = window_sum[:POOL_HALO, :] / jnp.minimum(pos[:POOL_HALO, :] + 1, size).astype(F32)
    return jnp.concatenate([head, window_sum[POOL_HALO:, :] * (1.0 / size)], axis=0)


def _fwd_inproj(x, g_mix, win_t, b_in, cos, sin, w_pool, b_pool, pool_scale, rider=None):
    s = x.shape[0]
    tm = _token_tile(s)

    def body(x_ref, g_ref, w_ref, b_ref, cos_ref, sin_ref, wp_ref, bp_ref, ps_ref,
             q_ref, k_ref, v_ref, vt_ref, mix_ref, pool_ref, tail_ref):
        i = pl.program_id(0)

        @pl.when(i == 0)
        def _():
            tail_ref[...] = jnp.zeros_like(tail_ref)

        half = tm // 2
        halves = [slice(0, half), slice(half, tm)]
        projected = []
        for rows in halves:
            n, _ = _rms(x_ref[rows, :])
            projected.append(_nt((n * g_ref[...]).astype(BF16), w_ref[...]) + b_ref[...])
        cos_tile, sin_tile = _tile_tables(cos_ref, sin_ref)
        tail = tail_ref[...]
        for k, (rows, z) in enumerate(zip(halves, projected)):
            cos_t, sin_t = cos_tile[rows, :], sin_tile[rows, :]
            q_ref[rows, :] = (_rope(z[:, :ATTN_WIDTH], cos_t, sin_t) * Q_SCALE).astype(BF16)
            k_ref[rows, :] = _spread_kv(_rope(z[:, ATTN_WIDTH:ATTN_WIDTH + KV_WIDTH], cos_t, sin_t)).astype(BF16)
            vz = _spread_kv(z[:, ATTN_WIDTH + KV_WIDTH:ATTN_WIDTH + 2 * KV_WIDTH])
            v_ref[rows, :] = vz.astype(BF16)
            vt_ref[:, rows] = vz.T.astype(BF16)
            u = z[:, ATTN_WIDTH + 2 * KV_WIDTH:]
            u_ext = jnp.concatenate([tail, u], axis=0)
            tail = u[half - POOL_HALO:, :]
            pos = lax.broadcasted_iota(jnp.int32, (half, POOL_GROUP_WIDTH), 0) + (i * tm + k * half)
            for g, size in enumerate(POOL_SIZES):
                cols = slice(g * POOL_GROUP_WIDTH, (g + 1) * POOL_GROUP_WIDTH)
                a = u_ext[:, cols]
                shift = 1
                while shift < size:
                    a = a + pltpu.roll(a, shift, 0)
                    shift *= 2
                mixed = (_window_mean(a[POOL_HALO:, :], pos, size) - u[:, cols]).astype(BF16)
                pre = _nn(mixed, wp_ref[g]) + bp_ref[:, cols]
                mix_ref[rows, cols] = mixed
                pool_ref[rows, cols] = (pre * ps_ref[:, cols]).astype(BF16)
        tail_ref[...] = tail

    bf = lambda w: jax.ShapeDtypeStruct((s, w), BF16)
    return _gridded(
        body, rider, name="fwd_inproj", grid=(s // tm,),
        in_specs=[_rows(tm, D_MODEL), _full((1, D_MODEL)), _full((IN_WIDTH, D_MODEL)), _full((1, IN_WIDTH)),
                  _full((4, tm, LANES)), pl.BlockSpec((1, 2, LANES), lambda i: (i, 0, 0)), _full((4, POOL_GROUP_WIDTH, POOL_GROUP_WIDTH)),
                  _full((1, POOL_WIDTH)), _full((1, POOL_WIDTH))],
        out_specs=[_rows(tm, ATTN_WIDTH), _rows(tm, KV_SPREAD), _rows(tm, KV_SPREAD),
                   pl.BlockSpec((KV_SPREAD, tm), lambda i: (0, i)), _rows(tm, POOL_WIDTH), _rows(tm, POOL_WIDTH)],
        out_shape=[bf(ATTN_WIDTH), bf(KV_SPREAD), bf(KV_SPREAD), jax.ShapeDtypeStruct((KV_SPREAD, s), BF16),
                   bf(POOL_WIDTH), bf(POOL_WIDTH)],
        scratch_shapes=[pltpu.VMEM((POOL_HALO, POOL_WIDTH), F32)],
        args=(x, g_mix, win_t, b_in, cos, sin, w_pool, b_pool, pool_scale))


ATTN_TILE = 1024
PAIR = 2 * LANES


def _band_masks(tile):
    j = lax.broadcasted_iota(jnp.int32, (4 * BLOCK, 2 * BLOCK), 0) % (2 * BLOCK)
    r = lax.broadcasted_iota(jnp.int32, (4 * BLOCK, 2 * BLOCK), 1) % BLOCK
    band = (j > r) & (j <= r + BLOCK)
    return band & ((tile > 0) | (j >= BLOCK)), band


def _band(cur_ref, prev_ref, b, kv):
    halves = []
    for half in range(2):
        cols = slice(kv * PAIR + half * LANES, kv * PAIR + (half + 1) * LANES)
        if b == 0:
            halves.append(jnp.concatenate([prev_ref[:, cols], cur_ref[0:BLOCK, cols]], axis=0))
        else:
            halves.append(cur_ref[(b - 1) * BLOCK:(b + 1) * BLOCK, cols])
    return jnp.concatenate(halves, axis=0)


def _stack_pair(ref, rows, kv):
    return jnp.concatenate([ref[rows, kv * PAIR:kv * PAIR + LANES], ref[rows, kv * PAIR + LANES:(kv + 1) * PAIR]], axis=0)


def _pair_heads(kv, half):
    return GQA_GROUP * kv + half, GQA_GROUP * kv + 2 + half


def _band_t(cur_ref, prev_ref, b, kv):
    halves = []
    for half in range(2):
        lanes = slice(kv * PAIR + half * LANES, kv * PAIR + (half + 1) * LANES)
        if b == 0:
            halves.append(jnp.concatenate([prev_ref[lanes, :], cur_ref[lanes, 0:BLOCK]], axis=1))
        else:
            halves.append(cur_ref[lanes, (b - 1) * BLOCK:(b + 1) * BLOCK])
    return jnp.concatenate(halves, axis=1)


def _reduce_rows(x, op, reduce):
    while x.shape[0] > 8:
        half = x.shape[0] // 2
        x = op(x[:half], x[half:])
    return reduce(x, axis=0, keepdims=True)


def _per_query(ref, rows, top, bottom):
    return jnp.concatenate([ref[top:top + 1, rows], ref[bottom:bottom + 1, rows]], axis=1)


def _sink_per_query(sink_ref, top, bottom):
    first_slab = lax.broadcasted_iota(jnp.int32, (1, 2 * BLOCK), 1) < BLOCK
    return jnp.where(first_slab, sink_ref[:, top:top + 1], sink_ref[:, bottom:bottom + 1])


def _attn_fwd(q, kz, vt, sinks, rider=None):
    s = q.shape[0]
    tq = min(ATTN_TILE, s)

    def body(q_ref, k_ref, kp_ref, vt_ref, vtp_ref, sink_ref, o_ref, lse_ref):
        first, band = _band_masks(pl.program_id(0))
        chains = [(b, kv) for b in range(tq // BLOCK) for kv in range(N_KV_HEADS)]

        def scores(b, kv):
            rows = slice(b * BLOCK, (b + 1) * BLOCK)
            return _nt(_band(k_ref, kp_ref, b, kv), _stack_pair(q_ref, rows, kv))

        def store(b, kv, ot):
            rows = slice(b * BLOCK, (b + 1) * BLOCK)
            o = ot.T.astype(BF16)
            o_ref[rows, kv * PAIR:kv * PAIR + LANES] = o[:BLOCK]
            o_ref[rows, kv * PAIR + LANES:(kv + 1) * PAIR] = o[BLOCK:]

        ahead = scores(*chains[0])
        behind = None
        for n, (b, kv) in enumerate(chains):
            rows = slice(b * BLOCK, (b + 1) * BLOCK)
            st = jnp.where(first if b == 0 else band, ahead, -jnp.inf)
            if n + 1 < len(chains):
                ahead = scores(*chains[n + 1])
            probs = []
            for half in range(2):
                top, bottom = _pair_heads(kv, half)
                sink = _sink_per_query(sink_ref, top, bottom)
                sh = st[half * 2 * BLOCK:(half + 1) * 2 * BLOCK, :]
                m = jnp.maximum(_reduce_rows(sh, jnp.maximum, jnp.max), sink)
                p = jnp.exp(sh - m)
                denom = _reduce_rows(p, jnp.add, jnp.sum) + jnp.exp(sink - m)
                probs.append((p * (1.0 / denom)).astype(BF16))
                lse = m + jnp.log(denom)
                lse_ref[top:top + 1, rows] = lse[:, :BLOCK]
                lse_ref[bottom:bottom + 1, rows] = lse[:, BLOCK:]
            ot = _nn(_band_t(vt_ref, vtp_ref, b, kv), jnp.concatenate(probs, axis=0))
            if behind is not None:
                store(*behind)
            behind = (b, kv, ot)
        store(*behind)

    per = tq // BLOCK
    cur = lambda w: pl.BlockSpec((tq, w), lambda i: (i, 0))
    prev = pl.BlockSpec((BLOCK, KV_SPREAD), lambda i: (jnp.maximum(per * i - 1, 0), 0))
    cur_t = pl.BlockSpec((KV_SPREAD, tq), lambda i: (0, i))
    prev_t = pl.BlockSpec((KV_SPREAD, BLOCK), lambda i: (0, jnp.maximum(per * i - 1, 0)))
    return _gridded(
        body, rider, name="attn_fwd", grid=(s // tq,),
        in_specs=[cur(ATTN_WIDTH), cur(KV_SPREAD), prev, cur_t, prev_t, _full((1, N_Q_HEADS))],
        out_specs=[cur(ATTN_WIDTH), pl.BlockSpec((N_Q_HEADS, tq), lambda i: (0, i))],
        out_shape=[jax.ShapeDtypeStruct((s, ATTN_WIDTH), BF16), jax.ShapeDtypeStruct((N_Q_HEADS, s), F32)],
        scratch_shapes=[], args=(q, kz, kz, vt, vt, sinks))


FF_CHUNK = 256
TN_ROW_CHUNK = 256


def _resident(shape):
    return pl.BlockSpec(shape, lambda *_: (0,) * len(shape), pipeline_mode=pl.Buffered(1))


def _flush_rows(acc_ref, out_ref, sem, rows, is_last):
    @pl.when(is_last)
    def _():
        pltpu.make_async_copy(acc_ref.at[rows, :], out_ref.at[rows, :], sem).start()


def _flush_wait(acc_ref, out_ref, sem, is_last):
    @pl.when(is_last)
    def _():
        pltpu.make_async_copy(acc_ref, out_ref, sem).wait()


def _accumulate_tn(acc_ref, a_ref, b, out_ref, sem, is_last):
    for m0 in range(0, acc_ref.shape[0], TN_ROW_CHUNK):
        rows = slice(m0, m0 + TN_ROW_CHUNK)
        acc_ref[rows, :] += _tn(a_ref[:, rows], b)
        _flush_rows(acc_ref, out_ref, sem, rows, is_last)
    _flush_wait(acc_ref, out_ref, sem, is_last)


def _fwd_outproj_ffn_act(attn, pool, w_out, b_out, x, g_ffn, wg_t, wu_t, rider=None):
    s = x.shape[0]
    tm = _token_tile(s)

    def body(a_ref, p_ref, w_ref, b_ref, x_ref, g_ref, wg_ref, wu_ref, x2_ref, gate_ref, up_ref, act_ref):
        x2 = x_ref[...] + _nn(a_ref[...], w_ref[:ATTN_WIDTH, :]) + _nn(p_ref[...], w_ref[ATTN_WIDTH:, :]) + b_ref[...]
        x2_ref[...] = x2
        n, _ = _rms(x2)
        h = (n * g_ref[...]).astype(BF16)

        def products(c0):
            return _nt(h, wg_ref[c0:c0 + FF_CHUNK, :]), _nt(h, wu_ref[c0:c0 + FF_CHUNK, :])

        ahead = products(0)
        for c0 in range(0, D_FF, FF_CHUNK):
            cols = slice(c0, c0 + FF_CHUNK)
            gate, up = ahead
            if c0 + FF_CHUNK < D_FF:
                ahead = products(c0 + FF_CHUNK)
            gate_ref[:, cols] = gate.astype(BF16)
            up_ref[:, cols] = up.astype(BF16)
            act_ref[:, cols] = (gate * jax.nn.sigmoid(gate) * up).astype(BF16)

    act_shape = jax.ShapeDtypeStruct((s, D_FF), BF16)
    return _gridded(
        body, rider, name="fwd_outproj_ffn_act", grid=(s // tm,),
        in_specs=[_rows(tm, ATTN_WIDTH), _rows(tm, POOL_WIDTH), _resident((D_MODEL, D_MODEL)), _full((1, D_MODEL)),
                  _rows(tm, D_MODEL), _full((1, D_MODEL)), _resident((D_FF, D_MODEL)), _resident((D_FF, D_MODEL))],
        out_specs=[_rows(tm, D_MODEL)] + [_rows(tm, D_FF)] * 3,
        out_shape=[jax.ShapeDtypeStruct((s, D_MODEL), F32)] + [act_shape] * 3,
        scratch_shapes=[], args=(attn, pool, w_out, b_out, x, g_ffn, wg_t, wu_t))


def _fwd_down_loss(act, x2, wd, g_final, target):
    s = x2.shape[0]
    tm = _token_tile(s)
    last = s // tm - 1

    def body(a_ref, x2_ref, wd_ref, g_ref, t_ref, dx3_ref, sq_ref, dg_ref, dwd_ref, acc_ref, sem):
        @pl.when(pl.program_id(0) == 0)
        def _():
            sq_ref[...] = jnp.zeros_like(sq_ref)
            dg_ref[...] = jnp.zeros_like(dg_ref)
            acc_ref[...] = jnp.zeros_like(acc_ref)

        halves = [slice(0, tm // 2), slice(tm // 2, tm)]
        g = g_ref[...]
        down = [_nn(a_ref[rows, :], wd_ref[...]) for rows in halves]
        for half, (rows, y) in enumerate(zip(halves, down)):
            x3 = x2_ref[rows, :] + y
            n, r = _rms(x3)
            diff = n * g - t_ref[rows, :]
            sq_ref[...] += jnp.sum(diff * diff, axis=0, keepdims=True)
            dx3, dg = _rms_bwd(diff * (1.0 / D_MODEL), n, r, g)
            dg_ref[...] += dg
            dx3_ref[rows, :] = dx3
            dx3b = dx3.astype(BF16)
            if half == 0:
                for m0 in range(0, D_FF, TN_ROW_CHUNK):
                    acc_ref[m0:m0 + TN_ROW_CHUNK, :] += _tn(a_ref[rows, m0:m0 + TN_ROW_CHUNK], dx3b)
            else:
                _accumulate_tn(acc_ref, a_ref.at[rows, :], dx3b, dwd_ref, sem, pl.program_id(0) == last)

    return pl.pallas_call(
        body, name="fwd_down_loss", grid=(s // tm,),
        in_specs=[_rows(tm, D_FF), _rows(tm, D_MODEL), _resident((D_FF, D_MODEL)), _full((1, D_MODEL)), _rows(tm, D_MODEL)],
        out_specs=[_rows(tm, D_MODEL), _full((1, D_MODEL)), _full((1, D_MODEL)), HBM],
        out_shape=[jax.ShapeDtypeStruct((s, D_MODEL), F32),
                   jax.ShapeDtypeStruct((1, D_MODEL), F32), jax.ShapeDtypeStruct((1, D_MODEL), F32),
                   jax.ShapeDtypeStruct((D_FF, D_MODEL), F32)],
        scratch_shapes=[pltpu.VMEM((D_FF, D_MODEL), F32), pltpu.SemaphoreType.DMA],
        compiler_params=_params("arbitrary"),
    )(act, x2, wd, g_final, target)


FFN_BWD_TILE = 256


def _bwd_ffn(dx3, gate, up, x2, wd, wg_t, wu_t, g_ffn, rider=None):
    s = x2.shape[0]
    tm = min(FFN_BWD_TILE, s)
    last = s // tm - 1

    def body(dx3_ref, gate_ref, up_ref, x2_ref, wd_ref, wg_ref, wu_ref, g_ref,
             dx2_ref, dg_ref, db_ref, dwg_ref, dwu_ref, dgate_ref, dup_ref, accg_ref, accu_ref, sems):
        @pl.when(pl.program_id(0) == 0)
        def _():
            dg_ref[...] = jnp.zeros_like(dg_ref)
            db_ref[...] = jnp.zeros_like(db_ref)
            accg_ref[...] = jnp.zeros_like(accg_ref)
            accu_ref[...] = jnp.zeros_like(accu_ref)

        dx3b = dx3_ref[...].astype(BF16)
        g = g_ref[...]
        n, r = _rms(x2_ref[...])
        h = (n * g).astype(BF16)
        ahead = _nt(dx3b, wd_ref[0:FF_CHUNK, :])
        for c0 in range(0, D_FF, FF_CHUNK):
            cols = slice(c0, c0 + FF_CHUNK)
            dact = ahead
            if c0 + FF_CHUNK < D_FF:
                ahead = _nt(dx3b, wd_ref[c0 + FF_CHUNK:c0 + 2 * FF_CHUNK, :])
            gate = gate_ref[:, cols].astype(F32)
            up = up_ref[:, cols].astype(F32)
            sig = jax.nn.sigmoid(gate)
            silu = gate * sig
            dup = (dact * silu).astype(BF16)
            dgate = (dact * up * (sig + silu * (1.0 - sig))).astype(BF16)
            dup_ref[:, cols] = dup
            dgate_ref[:, cols] = dgate
            accg_ref[cols, :] += _tn(dgate, h)
            accu_ref[cols, :] += _tn(dup, h)
        dh2 = _nn(dgate_ref[...], wg_ref[...]) + _nn(dup_ref[...], wu_ref[...])
        dx, dg = _rms_bwd(dh2, n, r, g)
        dx2 = dx3_ref[...] + dx
        dg_ref[...] += dg
        db_ref[...] += jnp.sum(dx2, axis=0, keepdims=True)
        dx2_ref[...] = dx2

        @pl.when(pl.program_id(0) == last)
        def _():
            outs = [pltpu.make_async_copy(accg_ref, dwg_ref, sems.at[0]), pltpu.make_async_copy(accu_ref, dwu_ref, sems.at[1])]
            for cp in outs:
                cp.start()
            for cp in outs:
                cp.wait()

    grad_shape = jax.ShapeDtypeStruct((D_FF, D_MODEL), F32)
    weight = _resident((D_FF, D_MODEL))
    return _gridded(
        body, rider, name="bwd_ffn", grid=(s // tm,),
        in_specs=[_rows(tm, D_MODEL), _rows(tm, D_FF), _rows(tm, D_FF),
                  _rows(tm, D_MODEL), weight, weight, weight, _full((1, D_MODEL))],
        out_specs=[_rows(tm, D_MODEL), _full((1, D_MODEL)), _full((1, D_MODEL)), HBM, HBM],
        out_shape=[jax.ShapeDtypeStruct((s, D_MODEL), F32),
                   jax.ShapeDtypeStruct((1, D_MODEL), F32), jax.ShapeDtypeStruct((1, D_MODEL), F32), grad_shape, grad_shape],
        scratch_shapes=[pltpu.VMEM((tm, D_FF), BF16), pltpu.VMEM((tm, D_FF), BF16),
                        pltpu.VMEM((D_FF, D_MODEL), F32), pltpu.VMEM((D_FF, D_MODEL), F32), pltpu.SemaphoreType.DMA((2,))],
        args=(dx3, gate, up, x2, wd, wg_t, wu_t, g_ffn))


def _bwd_outproj_pool(dx2, attn, pool, mixed, w_out, w_pool, b_pool, pool_scale, rider=None):
    s = dx2.shape[0]
    tm = min(2 * _token_tile(s), s)
    nt = s // tm

    def body(dx_ref, a_ref, p_ref, mix_ref, w_ref, wp_ref, bp_ref, ps_ref,
             dattn_ref, du_ref, dwout_ref, dwp_ref, dbp_ref, dps_ref, head_ref):
        step = pl.program_id(0)
        tile = nt - 1 - step

        @pl.when(step == 0)
        def _():
            head_ref[...] = jnp.zeros_like(head_ref)
            dwout_ref[...] = jnp.zeros_like(dwout_ref)
            dwp_ref[...] = jnp.zeros_like(dwp_ref)
            dbp_ref[...] = jnp.zeros_like(dbp_ref)
            dps_ref[...] = jnp.zeros_like(dps_ref)

        dx = dx_ref[...].astype(BF16)
        dwout_ref[:ATTN_WIDTH, :] += _tn(a_ref[...], dx)
        dwout_ref[ATTN_WIDTH:, :] += _tn(p_ref[...], dx)
        dcat = _nt(dx, w_ref[...])
        dattn_ref[...] = dcat[:, :ATTN_WIDTH].astype(BF16)
        dpool = dcat[:, ATTN_WIDTH:]
        pos = lax.broadcasted_iota(jnp.int32, (tm, POOL_GROUP_WIDTH), 0) + tile * tm
        head = head_ref[...]
        n_ext = tm + POOL_HALO
        for g, size in enumerate(POOL_SIZES):
            cols = slice(g * POOL_GROUP_WIDTH, (g + 1) * POOL_GROUP_WIDTH)
            mixed_g = mix_ref[:, cols]
            pre = _nn(mixed_g, wp_ref[g]) + bp_ref[:, cols]
            dy = dpool[:, cols]
            dps_ref[:, cols] += jnp.sum(dy * pre, axis=0, keepdims=True)
            dpre = dy * ps_ref[:, cols]
            dbp_ref[:, cols] += jnp.sum(dpre, axis=0, keepdims=True)
            dpre_b = dpre.astype(BF16)
            dwp_ref[g] += _tn(mixed_g, dpre_b)
            dmixed = _nt(dpre_b, wp_ref[g])
            w = _window_mean(dmixed, pos, size)
            head_ref[:, cols] = w[:POOL_HALO, :]
            a = jnp.concatenate([w, head[:, cols]], axis=0)
            shift = 1
            while shift < size:
                a = a + pltpu.roll(a, n_ext - shift, 0)
                shift *= 2
            du_ref[:, cols] = (a[:tm, :] - dmixed).astype(BF16)

    rev = lambda w: pl.BlockSpec((tm, w), lambda i: (nt - 1 - i, 0))
    return _gridded(
        body, rider, name="bwd_outproj_pool", grid=(nt,),
        in_specs=[rev(D_MODEL), rev(ATTN_WIDTH), rev(POOL_WIDTH), rev(POOL_WIDTH), _full((D_MODEL, D_MODEL)),
                  _full((4, POOL_GROUP_WIDTH, POOL_GROUP_WIDTH)), _full((1, POOL_WIDTH)), _full((1, POOL_WIDTH))],
        out_specs=[rev(ATTN_WIDTH), rev(POOL_WIDTH), _full((D_MODEL, D_MODEL)),
                   _full((4, POOL_GROUP_WIDTH, POOL_GROUP_WIDTH)), _full((1, POOL_WIDTH)), _full((1, POOL_WIDTH))],
        out_shape=[jax.ShapeDtypeStruct((s, ATTN_WIDTH), BF16), jax.ShapeDtypeStruct((s, POOL_WIDTH), BF16),
                   jax.ShapeDtypeStruct((D_MODEL, D_MODEL), F32),
                   jax.ShapeDtypeStruct((4, POOL_GROUP_WIDTH, POOL_GROUP_WIDTH), F32),
                   jax.ShapeDtypeStruct((1, POOL_WIDTH), F32), jax.ShapeDtypeStruct((1, POOL_WIDTH), F32)],
        scratch_shapes=[pltpu.VMEM((POOL_HALO, POOL_WIDTH), F32)],
        args=(dx2, attn, pool, mixed, w_out, w_pool, b_pool, pool_scale))


def _fold_spread(t):
    low = lax.broadcasted_iota(jnp.int32, (2 * BLOCK, LANES), 1) < HEAD_DIM
    kept = jnp.where(low, t[:2 * BLOCK, :], t[2 * BLOCK:, :])
    return kept + pltpu.roll(kept, HEAD_DIM, 1)


def _attn_bwd(q, kz, vz, dattn, lse, sinks, rider=None):
    s = q.shape[0]
    tq = min(ATTN_TILE, s)
    nt = s // tq
    per = tq // BLOCK

    def body(q_ref, k_ref, kp_ref, v_ref, vp_ref, do_ref, lse_ref, sink_ref,
             dq_ref, dk_ref, dv_ref, dsink_ref, dk_acc, dv_acc, dk_carry, dv_carry):
        step = pl.program_id(0)

        @pl.when(step == 0)
        def _():
            dk_carry[...] = jnp.zeros_like(dk_carry)
            dv_carry[...] = jnp.zeros_like(dv_carry)
            dsink_ref[...] = jnp.zeros_like(dsink_ref)

        dk_acc[0:tq, :] = jnp.zeros((tq, KV_WIDTH), F32)
        dv_acc[0:tq, :] = jnp.zeros((tq, KV_WIDTH), F32)
        dk_acc[tq:, :] = dk_carry[...]
        dv_acc[tq:, :] = dv_carry[...]
        first, band = _band_masks(nt - 1 - step)
        low = lax.broadcasted_iota(jnp.int32, (2 * BLOCK, LANES), 1) < HEAD_DIM
        chains = [(b, kv) for b in range(per) for kv in range(N_KV_HEADS)]

        def operands(b, kv):
            rows = slice(b * BLOCK, (b + 1) * BLOCK)
            qab = _stack_pair(q_ref, rows, kv)
            doab = _stack_pair(do_ref, rows, kv)
            kzb = _band(k_ref, kp_ref, b, kv)
            return qab, doab, kzb, _nt(kzb, qab), _nt(_band(v_ref, vp_ref, b, kv), doab)

        folded = {}

        def finish(b, kv, dqab, dkz, dvz):
            rows = slice(b * BLOCK, (b + 1) * BLOCK)
            dq_ref[rows, kv * PAIR:kv * PAIR + LANES] = dqab[:BLOCK] * Q_SCALE
            dq_ref[rows, kv * PAIR + LANES:(kv + 1) * PAIR] = dqab[BLOCK:] * Q_SCALE
            folded[kv] = (_fold_spread(dkz), _fold_spread(dvz))
            if kv == N_KV_HEADS - 1:
                band_rows = slice(b * BLOCK, (b + 2) * BLOCK)
                dk_acc[band_rows, :] += jnp.where(low, folded[0][0], folded[1][0])
                dv_acc[band_rows, :] += jnp.where(low, folded[0][1], folded[1][1])

        ahead = operands(*chains[0])
        behind = None
        for n, (b, kv) in enumerate(chains):
            rows = slice(b * BLOCK, (b + 1) * BLOCK)
            mask = first if b == 0 else band
            qab, doab, kzb, st, dpt = ahead
            if n + 1 < len(chains):
                ahead = operands(*chains[n + 1])
            probs, dscores = [], []
            for half in range(2):
                top, bottom = _pair_heads(kv, half)
                keys = slice(half * 2 * BLOCK, (half + 1) * 2 * BLOCK)
                lse_h = _per_query(lse_ref, rows, top, bottom)
                p = jnp.where(mask[keys, :], jnp.exp(st[keys, :] - lse_h), 0.0)
                dph = dpt[keys, :]
                delta = _reduce_rows(p * dph, jnp.add, jnp.sum)
                probs.append(p.astype(BF16))
                dscores.append((p * (dph - delta)).astype(BF16))
                leak = jnp.exp(_sink_per_query(sink_ref, top, bottom) - lse_h) * delta
                dsink_ref[:, top:top + 1] -= jnp.sum(leak[:, :BLOCK], axis=1, keepdims=True)
                dsink_ref[:, bottom:bottom + 1] -= jnp.sum(leak[:, BLOCK:], axis=1, keepdims=True)
            ds = jnp.concatenate(dscores, axis=0)
            results = (_tn(ds, kzb), _nn(ds, qab), _nn(jnp.concatenate(probs, axis=0), doab))
            if behind is not None:
                finish(*behind)
            behind = (b, kv, *results)
        finish(*behind)
        dk_ref[...] = dk_acc[BLOCK:, :]
        dv_ref[...] = dv_acc[BLOCK:, :]
        dk_carry[...] = dk_acc[0:BLOCK, :]
        dv_carry[...] = dv_acc[0:BLOCK, :]

    cur = lambda w: pl.BlockSpec((tq, w), lambda i: (nt - 1 - i, 0))
    prev = pl.BlockSpec((BLOCK, KV_SPREAD), lambda i: (jnp.maximum(per * (nt - 1 - i) - 1, 0), 0))
    acc = pltpu.VMEM((tq + BLOCK, KV_WIDTH), F32)
    carry = pltpu.VMEM((BLOCK, KV_WIDTH), F32)
    return _gridded(
        body, rider, name="attn_bwd", grid=(nt,),
        in_specs=[cur(ATTN_WIDTH), cur(KV_SPREAD), prev, cur(KV_SPREAD), prev, cur(ATTN_WIDTH),
                  pl.BlockSpec((N_Q_HEADS, tq), lambda i: (0, nt - 1 - i)), _full((1, N_Q_HEADS))],
        out_specs=[cur(ATTN_WIDTH), cur(KV_WIDTH), cur(KV_WIDTH), _full((1, N_Q_HEADS))],
        out_shape=[jax.ShapeDtypeStruct((s, ATTN_WIDTH), F32), jax.ShapeDtypeStruct((s, KV_WIDTH), F32),
                   jax.ShapeDtypeStruct((s, KV_WIDTH), F32), jax.ShapeDtypeStruct((1, N_Q_HEADS), F32)],
        scratch_shapes=[acc, acc, carry, carry],
        args=(q, kz, kz, vz, vz, dattn, lse, sinks))


def _bwd_inproj(dq, dk, dv, du, cos, sin, win_t, x, g_mix, dx2):
    s = x.shape[0]
    tm = _token_tile(s)

    def body(dq_ref, dk_ref, dv_ref, du_ref, cos_ref, sin_ref, w_ref, x_ref, g_ref, dx2_ref,
             dx_ref, dw_ref, db_ref, dg_ref):
        @pl.when(pl.program_id(0) == 0)
        def _():
            dw_ref[...] = jnp.zeros_like(dw_ref)
            db_ref[...] = jnp.zeros_like(db_ref)
            dg_ref[...] = jnp.zeros_like(dg_ref)

        cos_t, sin_t = _tile_tables(cos_ref, sin_ref)
        g = g_ref[...]
        staged = []
        for rows in (slice(0, tm // 2), slice(tm // 2, tm)):
            dz32 = jnp.concatenate([_rope_bwd(dq_ref[rows, :], cos_t[rows, :], sin_t[rows, :]),
                                    _rope_bwd(dk_ref[rows, :], cos_t[rows, :], sin_t[rows, :]),
                                    dv_ref[rows, :], du_ref[rows, :].astype(F32)], axis=1)
            db_ref[...] += jnp.sum(dz32, axis=0, keepdims=True)
            dz = dz32.astype(BF16)
            n, r = _rms(x_ref[rows, :])
            staged.append((rows, dz, (n * g).astype(BF16), n, r, _nn(dz, w_ref[...])))
        dz = jnp.concatenate([part[1] for part in staged], axis=0)
        h = jnp.concatenate([part[2] for part in staged], axis=0)
        for m0 in range(0, IN_WIDTH, TN_ROW_CHUNK):
            dw_ref[m0:m0 + TN_ROW_CHUNK, :] += _tn(dz[:, m0:m0 + TN_ROW_CHUNK], h)
        for rows, _, _, n, r, dh in staged:
            dx, dg = _rms_bwd(dh, n, r, g)
            dg_ref[...] += dg
            dx_ref[rows, :] = dx2_ref[rows, :] + dx

    return _gridded(
        body, None, name="bwd_inproj", grid=(s // tm,),
        in_specs=[_rows(tm, ATTN_WIDTH), _rows(tm, KV_WIDTH), _rows(tm, KV_WIDTH), _rows(tm, POOL_WIDTH),
                  _full((4, tm, LANES)), pl.BlockSpec((1, 2, LANES), lambda i: (i, 0, 0)), _full((IN_WIDTH, D_MODEL)), _rows(tm, D_MODEL),
                  _full((1, D_MODEL)), _rows(tm, D_MODEL)],
        out_specs=[_rows(tm, D_MODEL), _full((IN_WIDTH, D_MODEL)), _full((1, IN_WIDTH)), _full((1, D_MODEL))],
        out_shape=[jax.ShapeDtypeStruct((s, D_MODEL), F32), jax.ShapeDtypeStruct((IN_WIDTH, D_MODEL), F32),
                   jax.ShapeDtypeStruct((1, IN_WIDTH), F32), jax.ShapeDtypeStruct((1, D_MODEL), F32)],
        scratch_shapes=[], args=(dq, dk, dv, du, cos, sin, win_t, x, g_mix, dx2))


def _rope_tables(s, tm):
    inv_freq = jnp.tile(1.0 / (ROPE_THETA ** (jnp.arange(0, HEAD_DIM, 2, dtype=F32) / HEAD_DIM)), 4)
    sign = jnp.tile(jnp.repeat(jnp.array([-1.0, 1.0], F32), HEAD_DIM // 2), 2)
    within = jnp.arange(tm, dtype=F32)[:, None] * inv_freq[None, :]
    start = jnp.arange(0, s, tm, dtype=F32)[:, None] * inv_freq[None, :]
    cos, sin = jnp.cos(within), jnp.sin(within)
    base = jnp.stack([cos, sin, sign * sin, sign * cos])
    return base, jnp.stack([jnp.cos(start), jnp.sin(start)], axis=1)


def _place():
    return lax.axis_index("x"), lax.axis_index("y"), lax.axis_index("c")


def _other_chips(x, y):
    return [(1 - x, y), (x, 1 - y), (1 - x, 1 - y)]


def _gather_rider(blocks, relay_early=False):
    nm = len(blocks)

    def plan(ins, outs, sems):
        send_sems, recv_sems, local_sems = sems
        x, y, c = _place()
        me, sibling = (x, y, c), (x, y, 1 - c)
        chips = _other_chips(x, y)

        def rows(m, px, py, pc):
            r = ins[m].shape[0]
            return outs[m].at[pl.ds((4 * px + 2 * py + pc) * r, r), :]

        def copy(m, k, block, to, src=None):
            return pltpu.make_async_remote_copy(
                src_ref=rows(m, *block) if src is None else src, dst_ref=rows(m, *block),
                send_sem=send_sems.at[k * nm + m], recv_sem=recv_sems.at[k * nm + m],
                device_id=to, device_id_type=MESH)

        mine = [pltpu.make_async_copy(ins[m], rows(m, *me), local_sems.at[m]) for m in range(nm)]
        first = [copy(m, 0, me, sibling, src=ins[m]) for m in range(nm)]
        first += [copy(m, 1 + j, me, (*chip, c), src=ins[m]) for j, chip in enumerate(chips) for m in range(nm)]
        return me, sibling, chips, copy, mine, first

    def start(ins, outs, sems):
        *_, mine, first = plan(ins, outs, sems)
        for cp in mine + first:
            cp.start()

    def passed_on(ins, outs, sems):
        me, sibling, chips, copy, _, _ = plan(ins, outs, sems)
        return [copy(m, 4 + j, (*chip, me[2]), sibling) for j, chip in enumerate(chips) for m in range(nm)]

    def relay(ins, outs, sems):
        me, _, chips, copy, _, _ = plan(ins, outs, sems)
        forwards = passed_on(ins, outs, sems)
        for j, chip in enumerate(chips):
            for m in range(nm):
                copy(m, 1 + j, (*chip, me[2]), me).wait_recv()
                forwards[j * nm + m].start()

    def finish(ins, outs, sems):
        me, sibling, chips, copy, mine, first = plan(ins, outs, sems)
        for m in range(nm):
            copy(m, 0, sibling, me).wait_recv()
        for j, chip in enumerate(chips):
            for m in range(nm):
                copy(m, 4 + j, (*chip, 1 - me[2]), me).wait_recv()
        for cp in first + passed_on(ins, outs, sems):
            cp.wait_send()
        for cp in mine:
            cp.wait()

    return _Rider(
        arrays=list(blocks), out_shape=[jax.ShapeDtypeStruct((N_DEV * b.shape[0], b.shape[1]), b.dtype) for b in blocks],
        sems=[pltpu.SemaphoreType.DMA((7 * nm,)), pltpu.SemaphoreType.DMA((7 * nm,)), pltpu.SemaphoreType.DMA((nm,))],
        start=start, finish=finish, relay=relay, relay_early=relay_early)


def _exchange_rider(copies_of, arrays, out_shape, n_copies):
    def copies(ins, outs, sems):
        send_sems, recv_sems = sems
        return [pltpu.make_async_remote_copy(src_ref=src, dst_ref=dst, send_sem=send_sems.at[k], recv_sem=recv_sems.at[k],
                                             device_id=to, device_id_type=MESH)
                for k, (src, dst, to) in enumerate(copies_of(ins, outs))]

    def start(ins, outs, sems):
        for cp in copies(ins, outs, sems):
            cp.start()

    def finish(ins, outs, sems):
        cps = copies(ins, outs, sems)
        for cp in cps:
            cp.wait_recv()
        for cp in cps:
            cp.wait_send()

    return _Rider(arrays=list(arrays), out_shape=out_shape,
                  sems=[pltpu.SemaphoreType.DMA((n_copies,)), pltpu.SemaphoreType.DMA((n_copies,))], start=start, finish=finish)


def _sibling_rider(grads):
    def copies_of(ins, outs):
        x, y, c = _place()
        for g_ref, o_ref in zip(ins, outs):
            r = g_ref.shape[0] // N_DEV
            for q in range(N_CHIPS):
                yield g_ref.at[pl.ds((2 * q + 1 - c) * r, r), :], o_ref.at[pl.ds(q * r, r), :], (x, y, 1 - c)

    return _exchange_rider(copies_of, grads, [jax.ShapeDtypeStruct((g.shape[0] // 2, g.shape[1]), F32) for g in grads],
                           len(grads) * N_CHIPS)


def _chip_sums(grads, from_sibling, place):
    n = len(grads)
    shapes = [(g.shape[0] // N_DEV, g.shape[1]) for g in grads]

    def body(place_ref, *refs):
        for k in range(n):
            g_ref, s_ref, wire_ref, own_ref = refs[k], refs[n + k], refs[2 * n + 2 * k], refs[2 * n + 2 * k + 1]
            total = g_ref[...] + s_ref[...]
            wire_ref[...] = total.astype(BF16)

            @pl.when(pl.program_id(0) == place_ref[1])
            def _(own_ref=own_ref, total=total):
                own_ref[...] = total

    grid_spec = pltpu.PrefetchScalarGridSpec(
        num_scalar_prefetch=1, grid=(N_CHIPS,),
        in_specs=[pl.BlockSpec(shape, lambda q, p: (2 * q + p[0], 0)) for shape in shapes]
        + [pl.BlockSpec(shape, lambda q, p: (q, 0)) for shape in shapes],
        out_specs=[spec for shape in shapes
                   for spec in (pl.BlockSpec(shape, lambda q, p: (q, 0)), pl.BlockSpec(shape, lambda q, p: (0, 0)))])
    out = pl.pallas_call(
        body, name="grad_chip_sum", grid_spec=grid_spec,
        out_shape=[s for r, w in shapes
                   for s in (jax.ShapeDtypeStruct((N_CHIPS * r, w), BF16), jax.ShapeDtypeStruct((r, w), F32))],
        compiler_params=_params("arbitrary"),
    )(place, *grads, *from_sibling)
    return [(out[2 * k], out[2 * k + 1]) for k in range(n)]


def _chip_sum(grad, from_sibling, place):
    return _chip_sums([grad], [from_sibling], place)[0]


def _chips_rider(wires):
    def copies_of(ins, outs):
        x, y, c = _place()
        for w_ref, o_ref in zip(ins, outs):
            r = w_ref.shape[0] // N_CHIPS
            for j, (px, py) in enumerate(_other_chips(x, y)):
                yield w_ref.at[pl.ds((2 * px + py) * r, r), :], o_ref.at[pl.ds(j * r, r), :], (px, py, c)

    return _exchange_rider(copies_of, wires,
                           [jax.ShapeDtypeStruct((3 * (w.shape[0] // N_CHIPS), w.shape[1]), BF16) for w in wires], len(wires) * 3)


SEM = pl.BlockSpec(memory_space=pltpu.SEMAPHORE)
DATAFLOW = pltpu.SideEffectType.DATAFLOW_SIDE_EFFECTING


def _split_copies(wire_refs, late_ref, land_refs, land_late_ref, send_sems, recv_sems):
    x, y, c = _place()
    ends = []
    for w_ref, o_ref in zip(wire_refs, land_refs):
        r = w_ref.shape[0] // N_CHIPS
        for j, (px, py) in enumerate(_other_chips(x, y)):
            ends.append((w_ref.at[pl.ds((2 * px + py) * r, r), :], o_ref.at[pl.ds(j * r, r), :], (px, py, c)))
    if late_ref is not None:
        rows = late_ref.shape[0]
        mine = land_late_ref.at[pl.ds((4 * x + 2 * y + c) * rows, rows), :]
        peers = [(x, y, 1 - c)] + [(px, py, pc) for px, py in _other_chips(x, y) for pc in (c, 1 - c)]
        ends += [(late_ref, mine, peer) for peer in peers]
    return [pltpu.make_async_remote_copy(src_ref=src, dst_ref=dst, send_sem=send_sems[k], recv_sem=recv_sems[k],
                                         device_id=to, device_id_type=MESH) for k, (src, dst, to) in enumerate(ends)]


def _split_refs(refs, nw, has_late):
    wires, refs = refs[:nw], refs[nw:]
    late, refs = (refs[0], refs[1:]) if has_late else (None, refs)
    lands, refs = refs[:nw], refs[nw:]
    land_late, refs = (refs[0], refs[1:]) if has_late else (None, refs)
    return wires, late, lands, land_late, refs


def _split_exchange_start(wires, late, name):
    nw, has_late = len(wires), late is not None
    n = 3 * nw + (N_DEV - 1 if has_late else 0)
    lands = [lax.empty((3 * (w.shape[0] // N_CHIPS), w.shape[1]), w.dtype) for w in wires]
    arrays = [*wires, *([late] if has_late else []), *lands,
              *([lax.empty((N_DEV * late.shape[0], late.shape[1]), late.dtype)] if has_late else [])]
    n_ops = len(arrays)

    def body(*refs):
        wire_refs, late_ref, land_refs, land_late_ref, outs = _split_refs(refs, nw, has_late)
        for cp in _split_copies(wire_refs, late_ref, land_refs, land_late_ref, outs[:n], outs[n:2 * n]):
            cp.start()
        outs[-1][...] = jnp.zeros_like(outs[-1])

    operands = [pltpu.with_memory_space_constraint(a, pltpu.HBM) for a in arrays]
    out = pl.pallas_call(
        body, name=name,
        out_shape=[pltpu.SemaphoreType.DMA(())] * (2 * n) + [pltpu.HBM(a.shape, a.dtype) for a in operands]
        + [jax.ShapeDtypeStruct((8, LANES), F32)],
        in_specs=[HBM] * n_ops, out_specs=[SEM] * (2 * n) + [HBM] * n_ops + [VMEM],
        input_output_aliases={i: 2 * n + i for i in range(n_ops)},
        compiler_params=pltpu.CompilerParams(has_side_effects=DATAFLOW),
    )(*operands)
    return out[:2 * n], out[2 * n:2 * n + n_ops], out[-1]


def _split_exchange_wait(sems, buffers, after, nw, has_late, name):
    n = len(sems) // 2
    n_ops = len(buffers)

    def body(*refs):
        wire_refs, late_ref, land_refs, land_late_ref, rest = _split_refs(refs, nw, has_late)
        for cp in _split_copies(wire_refs, late_ref, land_refs, land_late_ref, rest[:n], rest[n:2 * n]):
            cp.wait_send()
            cp.wait_recv()

    out = pl.pallas_call(
        body, name=name, out_shape=[pltpu.HBM(a.shape, a.dtype) for a in buffers],
        in_specs=[HBM] * n_ops + [SEM] * (2 * n) + [pl.BlockSpec(memory_space=pl.ANY)], out_specs=[HBM] * n_ops,
        input_output_aliases={i: i for i in range(n_ops)},
        compiler_params=pltpu.CompilerParams(has_side_effects=DATAFLOW),
    )(*buffers, *sems, after)
    return out[nw + (1 if has_late else 0):]


def _adamw_math(w, g, m, v):
    m = ADAM_B1 * m + (1.0 - ADAM_B1) * g
    v = ADAM_B2 * v + (1.0 - ADAM_B2) * jnp.square(g)
    m_hat = m / (1.0 - ADAM_B1 ** ADAM_STEP)
    v_hat = v / (1.0 - ADAM_B2 ** ADAM_STEP)
    delta = -ADAM_LR * (m_hat / (jnp.sqrt(v_hat) + ADAM_EPS) + ADAM_WD * w)
    return delta, m, v


def _reduce_adamw(own, received, w, m, v, after):
    r = own.shape[0]

    def body(own_ref, rec_ref, w_ref, m_ref, v_ref, after_ref, g_ref, d_ref, nm_ref, nv_ref):
        del after_ref
        g = ((own_ref[...] + rec_ref[0:r, :].astype(F32)) + rec_ref[r:2 * r, :].astype(F32)) + rec_ref[2 * r:, :].astype(F32)
        g_ref[...] = g
        d_ref[...], nm_ref[...], nv_ref[...] = _adamw_math(w_ref[...], g, m_ref[...], v_ref[...])

    shape = jax.ShapeDtypeStruct(own.shape, F32)
    return pl.pallas_call(
        body, name="reduce_adamw", in_specs=[VMEM] * 5 + [pl.BlockSpec(memory_space=pl.ANY)], out_specs=[VMEM] * 4,
        out_shape=[shape] * 4, compiler_params=_params(),
    )(own, received, w, m, v, after)


SMALL_WIDE = (("w_pool", 65536),)
SMALL_EARLY = (("b_pool", 512), ("pool_scale", 512), ("b_out", 1024), ("g_ffn", 1024), ("g_final", 1024), ("loss", 1024))
SMALL_LATE = (("sinks", 8), ("g_mix", 1024), ("b_in", 1280))
SMALL = SMALL_WIDE + SMALL_EARLY + SMALL_LATE


def _small_rows(size):
    return -(-size // (8 * LANES)) * 8


def _pack_small(values, entries=SMALL):
    parts = []
    for name, size in entries:
        flat = values[name].reshape(-1).astype(F32)
        parts.append(jnp.pad(flat, (0, _small_rows(size) * LANES - size)).reshape(-1, LANES))
    return jnp.concatenate(parts, axis=0)


def _unpack_small(packed, shapes):
    out, row = {}, 0
    for name, size in SMALL:
        rows = _small_rows(size)
        if name in shapes:
            out[name] = packed[row:row + rows].reshape(-1)[:size].reshape(shapes[name])
        row += rows
    return out


def _small_sum_adamw(gathered, w, m, v):
    n = len(gathered)

    def body(*refs):
        w_ref, m_ref, v_ref, g_ref, d_ref, nm_ref, nv_ref = refs[n:]

        def total(ref):
            rows = ref.shape[0] // N_DEV
            acc = ref[0:rows, :].astype(F32)
            for dev in range(1, N_DEV):
                acc = acc + ref[dev * rows:(dev + 1) * rows, :].astype(F32)
            return acc

        g = jnp.concatenate([total(ref) for ref in refs[:n]], axis=0)
        g_ref[...] = g
        d_ref[...], nm_ref[...], nv_ref[...] = _adamw_math(w_ref[...], g, m_ref[...], v_ref[...])

    shape = jax.ShapeDtypeStruct(w.shape, F32)
    return pl.pallas_call(
        body, name="small_sum_adamw", in_specs=[VMEM] * (n + 3), out_specs=[VMEM] * 4, out_shape=[shape] * 4,
        compiler_params=_params(),
    )(*gathered, w, m, v)


def kernel(x, g_mix, w_in, b_in, sinks, w_pool, b_pool, pool_scale, w_out, b_out, g_ffn, w_gate, w_up, w_down, g_final, loss_target, m_g_mix, m_w_in, m_b_in, m_sinks, m_w_pool, m_b_pool, m_pool_scale, m_w_out, m_b_out, m_g_ffn, m_w_gate, m_w_up, m_w_down, m_g_final, v_g_mix, v_w_in, v_b_in, v_sinks, v_w_pool, v_b_pool, v_pool_scale, v_w_out, v_b_out, v_g_ffn, v_w_gate, v_w_up, v_w_down, v_g_final):
    weights = dict(g_mix=g_mix, w_in=w_in, b_in=b_in, sinks=sinks, w_pool=w_pool, b_pool=b_pool, pool_scale=pool_scale,
                   w_out=w_out, b_out=b_out, g_ffn=g_ffn, w_gate=w_gate, w_up=w_up, w_down=w_down, g_final=g_final)
    mom1 = dict(g_mix=m_g_mix, w_in=m_w_in, b_in=m_b_in, sinks=m_sinks, w_pool=m_w_pool, b_pool=m_b_pool,
                pool_scale=m_pool_scale, w_out=m_w_out, b_out=m_b_out, g_ffn=m_g_ffn, w_gate=m_w_gate, w_up=m_w_up,
                w_down=m_w_down, g_final=m_g_final)
    mom2 = dict(g_mix=v_g_mix, w_in=v_w_in, b_in=v_b_in, sinks=v_sinks, w_pool=v_w_pool, b_pool=v_b_pool,
                pool_scale=v_pool_scale, w_out=v_w_out, b_out=v_b_out, g_ffn=v_g_ffn, w_gate=v_w_gate, w_up=v_w_up,
                w_down=v_w_down, g_final=v_g_final)
    order = ("g_mix", "w_in", "b_in", "sinks", "w_pool", "b_pool", "pool_scale", "w_out", "b_out", "g_ffn",
             "w_gate", "w_up", "w_down", "g_final")
    big = ("w_in", "w_out", "w_gate", "w_up", "w_down")
    transposed = ("w_in", "w_gate", "w_up")

    def row_shard(name, a):
        return a[0].T if name in transposed else a[0]

    shard = {n: row_shard(n, weights[n]).astype(BF16) for n in big}
    xs, target = x[0], loss_target[0]
    cos, sin = _rope_tables(xs.shape[0], _token_tile(xs.shape[0]))
    wp_b = w_pool[0].astype(BF16)
    bp = b_pool.reshape(1, POOL_WIDTH)
    ps = pool_scale.reshape(1, POOL_WIDTH)
    g_fin = g_final.reshape(1, D_MODEL)
    px, py, pc = _place()
    place = jnp.stack([pc, 2 * px + py]).astype(jnp.int32)

    (win_t,) = _alone(_gather_rider([shard["w_in"]]), "gather_w_in")
    q, kz, vz, vt, mixed, pool, w_out_b, wg_t = _fwd_inproj(
        xs, g_mix, win_t, b_in, cos, sin, wp_b, bp, ps,
        rider=_gather_rider([shard["w_out"], shard["w_gate"]], relay_early=True))
    attn, lse, wu_t = _attn_fwd(q, kz, vt, sinks, rider=_gather_rider([shard["w_up"]]))
    x2, gate, up, act, wd = _fwd_outproj_ffn_act(attn, pool, w_out_b, b_out, xs, g_ffn, wg_t, wu_t,
                                                 rider=_gather_rider([shard["w_down"]], relay_early=True))
    dx3, sq, dg_final, d_wd = _fwd_down_loss(act, x2, wd, g_fin, target)

    dx2, dg_ffn, db_out, d_wg_t, d_wu_t, wd_sibling = _bwd_ffn(
        dx3, gate, up, x2, wd, wg_t, wu_t, g_ffn, rider=_sibling_rider([d_wd]))
    wd_sum = _chip_sum(d_wd, wd_sibling, place)
    in_grads = [d_wg_t, d_wu_t]
    dattn, du, d_wout, d_wpool, d_bpool, d_pscale, wd_received, *in_sibling = _bwd_outproj_pool(
        dx2, attn, pool, mixed, w_out_b, wp_b, bp, ps, rider=_join(_chips_rider([wd_sum[0]]), _sibling_rider(in_grads)))
    in_sums = _chip_sums(in_grads, in_sibling, place)
    small_wide = _pack_small(dict(w_pool=d_wpool), SMALL_WIDE).astype(BF16)
    small_early = _pack_small(dict(b_pool=d_bpool, pool_scale=d_pscale, b_out=db_out, g_ffn=dg_ffn,
                                   g_final=dg_final, loss=sq), SMALL_EARLY)
    in_sems, in_flight, in_token = _split_exchange_start([wire for wire, _ in in_sums], None, "ffn_exchange_start")
    dq, dk, dv, d_sinks, *landed = _attn_bwd(
        q, kz, vz, dattn, lse, sinks + in_token[:1, :N_Q_HEADS],
        rider=_join(_sibling_rider([d_wout]), _gather_rider([small_wide, small_early])))
    wout_sum = _chip_sum(d_wout, landed[0], place)
    gathered_wide, gathered_early = landed[1], landed[2]
    dx, d_win_t, d_bin, d_gmix = _bwd_inproj(dq, dk, dv, du, cos, sin, win_t, xs, g_mix, dx2)
    (win_sibling,) = _alone(_sibling_rider([d_win_t]), "grad_exchange_sibling")
    win_sum = _chip_sum(d_win_t, win_sibling, place)
    small_late = _pack_small(dict(sinks=d_sinks, g_mix=d_gmix, b_in=d_bin), SMALL_LATE)

    grad, delta, new_m, new_v = {}, {}, {}, {}

    def update(n, own, rec, after):
        results = _reduce_adamw(own, rec, row_shard(n, weights[n]), row_shard(n, mom1[n]), row_shard(n, mom2[n]), after)
        grad[n], delta[n], new_m[n], new_v[n] = [(a.T if n in transposed else a)[None] for a in results]
        return results[0]

    sems, last_flight, after = _split_exchange_start([wout_sum[0], win_sum[0]], small_late, "last_exchange_start")
    ffn_sums = in_sums + [wd_sum]
    ffn_received = list(_split_exchange_wait(in_sems, in_flight, after, 2, False, "ffn_exchange_wait")) + [wd_received]
    after = ffn_received[0]
    for n, (_, own), rec in zip(("w_gate", "w_up", "w_down"), ffn_sums, ffn_received):
        after = update(n, own, rec, after)
    wout_received, win_received, gathered_late = _split_exchange_wait(sems, last_flight, after, 2, True, "last_exchange_wait")
    gathered_late = lax.dynamic_update_slice(gathered_late, small_late, ((4 * px + 2 * py + pc) * small_late.shape[0], 0))
    update("w_out", wout_sum[1], wout_received, after)
    update("w_in", win_sum[1], win_received, after)

    shapes = {n: weights[n].shape for n in order if n not in big}
    zero_loss = jnp.zeros((1, D_MODEL), F32)
    packed = _small_sum_adamw(
        [gathered_wide, gathered_early, gathered_late], _pack_small({**weights, "loss": zero_loss}),
        _pack_small({**mom1, "loss": zero_loss}), _pack_small({**mom2, "loss": zero_loss}))
    for store, pk in zip((grad, delta, new_m, new_v), packed):
        store.update(_unpack_small(pk, shapes))
    loss_rows = _unpack_small(packed[0], {"loss": (D_MODEL,)})["loss"]
    loss = (0.5 / D_MODEL) * jnp.sum(loss_rows)

    return (loss, dx[None], *[grad[n] for n in order], *[delta[n] for n in order],
            *[new_m[n] for n in order], *[new_v[n] for n in order])
```
